```python
import math
import jax, jax.numpy as jnp
from jax import lax
import numpy as np

D_MODEL = 1024
BATCH = 16
SEQ = 4096
DEPTH = 2

HEAD_DIM = 64
LRU_WIDTH = D_MODEL // 2
LRU_BLOCKS = LRU_WIDTH // HEAD_DIM
SB_HEADS = D_MODEL // 256
FOX_HEADS = D_MODEL // 256
SB_WIDTH = SB_HEADS * HEAD_DIM
FOX_WIDTH = FOX_HEADS * HEAD_DIM
D_MIX = LRU_WIDTH + SB_WIDTH + FOX_WIDTH
CONV_WIDTH = 4
LRU_C = 8.0
D_FF = 2816
Q_BLOCK = 128
N_SUB = 3
EPS = 1e-6

SPLIT_SIZES = (LRU_WIDTH, LRU_WIDTH,
               SB_WIDTH, SB_WIDTH, SB_WIDTH,
               FOX_WIDTH, FOX_WIDTH, FOX_WIDTH,
               FOX_HEADS)
N_IN = sum(SPLIT_SIZES)
SPLIT_POINTS = tuple(int(v) for v in np.cumsum(SPLIT_SIZES)[:-1])

kernel_name = "hybrid_macaron_rglru_stickbreak_fox"


def _rms(x):
    xf = x.astype(jnp.float32)
    y = xf * lax.rsqrt(jnp.mean(xf * xf, axis=-1, keepdims=True) + EPS)
    return y.astype(x.dtype)


def _swiglu(h, w_up, w_down):
    gu = h @ w_up
    g, u = jnp.split(gu, 2, axis=-1)
    return (jax.nn.silu(g) * u) @ w_down


def _causal_depthwise_conv(x, w, b):
    y = lax.conv_general_dilated(
        x, w[:, None, :], window_strides=(1,), padding=[(CONV_WIDTH - 1, 0)],
        dimension_numbers=("NWC", "WIO", "NWC"), feature_group_count=x.shape[-1])
    return y + b


def _lru_combine(left, right):
    a1, b1 = left
    a2, b2 = right
    return a1 * a2, a2 * b1 + b2


def _rg_lru(u, w_r, b_r, w_i, b_i, lam):
    B, S, _ = u.shape
    ub = u.reshape(B, S, LRU_BLOCKS, HEAD_DIM)
    r = jax.nn.sigmoid(jnp.einsum("bshi,hij->bshj", ub, w_r).reshape(B, S, LRU_WIDTH) + b_r)
    i = jax.nn.sigmoid(jnp.einsum("bshi,hij->bshj", ub, w_i).reshape(B, S, LRU_WIDTH) + b_i)
    r = r.astype(jnp.float32)
    log_a = -LRU_C * r * jax.nn.softplus(-lam.astype(jnp.float32))
    a = jnp.exp(log_a)
    mult = jnp.sqrt(-jnp.expm1(2.0 * log_a))
    bt = mult * (i.astype(jnp.float32) * u.astype(jnp.float32))
    _, h = lax.associative_scan(_lru_combine, (a, bt), axis=1)
    return h.astype(u.dtype)


def _to_heads(t, n_heads):
    B, S, _ = t.shape
    return t.reshape(B, S, n_heads, HEAD_DIM).transpose(0, 2, 1, 3)


def _from_blocks(out):
    nb, B, H, Q, Dh = out.shape
    return out.transpose(1, 0, 3, 2, 4).reshape(B, nb * Q, H * Dh)


def _stick_breaking_attention(q, k, v):
    S = q.shape[2]
    scale = HEAD_DIM ** -0.5
    key_pos = jnp.arange(S)

    def block(bi):
        start = bi * Q_BLOCK
        qb = lax.dynamic_slice_in_dim(q, start, Q_BLOCK, axis=2)
        z = jnp.einsum("bhqd,bhkd->bhqk", qb, k).astype(jnp.float32) * scale
        q_pos = start + jnp.arange(Q_BLOCK)
        past = key_pos[None, :] < q_pos[:, None]
        log_beta = jax.nn.log_sigmoid(z)
        log_1mb = jnp.where(past, jax.nn.log_sigmoid(-z), 0.0)
        after = lax.cumsum(log_1mb, axis=3, reverse=True) - log_1mb
        w = jnp.where(past, jnp.exp(log_beta + after), 0.0)
        return jnp.einsum("bhqk,bhkd->bhqd", w.astype(v.dtype), v)

    return _from_blocks(lax.map(block, jnp.arange(S // Q_BLOCK)))


def _forgetting_attention(q, k, v, log_f):
    S = q.shape[2]
    scale = HEAD_DIM ** -0.5
    key_pos = jnp.arange(S)
    F = jnp.cumsum(log_f, axis=-1)

    def block(bi):
        start = bi * Q_BLOCK
        qb = lax.dynamic_slice_in_dim(q, start, Q_BLOCK, axis=2)
        Fq = lax.dynamic_slice_in_dim(F, start, Q_BLOCK, axis=2)
        logits = jnp.einsum("bhqd,bhkd->bhqk", qb, k).astype(jnp.float32) * scale
        logits = logits + Fq[..., :, None] - F[..., None, :]
        q_pos = start + jnp.arange(Q_BLOCK)
        causal = key_pos[None, :] <= q_pos[:, None]
        logits = jnp.where(causal, logits, -jnp.inf)
        p = jax.nn.softmax(logits, axis=-1)
        return jnp.einsum("bhqk,bhkd->bhqd", p.astype(v.dtype), v)

    return _from_blocks(lax.map(block, jnp.arange(S // Q_BLOCK)))


def _mixer(h, w_in, b_fgate, conv_w, conv_b, w_rgate, b_rgate, w_igate, b_igate,
           lru_lambda, g_qk, g_mix_out, w_out):
    proj = h @ w_in
    (lru_x, lru_g, sb_q, sb_k, sb_v, fx_q, fx_k, fx_v, fx_f) = jnp.split(proj, SPLIT_POINTS, axis=-1)

    u = _causal_depthwise_conv(lru_x, conv_w, conv_b)
    y_lru = _rg_lru(u, w_rgate, b_rgate, w_igate, b_igate, lru_lambda) * jax.nn.gelu(lru_g)

    y_sb = _stick_breaking_attention(_to_heads(sb_q, SB_HEADS), _to_heads(sb_k, SB_HEADS),
                                     _to_heads(sb_v, SB_HEADS))

    fq = _rms(_to_heads(fx_q, FOX_HEADS)) * g_qk[0]
    fk = _rms(_to_heads(fx_k, FOX_HEADS)) * g_qk[1]
    log_f = jax.nn.log_sigmoid(fx_f.astype(jnp.float32) + b_fgate.astype(jnp.float32))
    y_fox = _forgetting_attention(fq, fk, _to_heads(fx_v, FOX_HEADS), log_f.transpose(0, 2, 1))

    y = jnp.concatenate([_rms(y_lru), _rms(y_sb), _rms(y_fox)], axis=-1) * g_mix_out
    return y @ w_out


def _fwd_setup_inputs(seed: int = 0) -> dict:
    key = jax.random.key(seed)
    ks = jax.random.split(key, 20)
    f32 = jnp.float32
    nrm = lambda k, shape, s: jax.random.normal(k, shape, f32) * s
    x = jax.random.normal(ks[0], (BATCH, SEQ, D_MODEL), f32)
    c = jax.random.normal(ks[1], (BATCH, D_MODEL), f32)
    w_ada = nrm(ks[2], (DEPTH, D_MODEL, N_SUB * 3 * D_MODEL), 0.1 * D_MODEL ** -0.5)
    b_ada = nrm(ks[3], (DEPTH, N_SUB * 3 * D_MODEL), 0.01)
    g_norm = 1.0 + nrm(ks[4], (DEPTH, N_SUB, D_MODEL), 0.05)
    w_ffn_up = nrm(ks[5], (DEPTH, 2, D_MODEL, 2 * D_FF), D_MODEL ** -0.5)
    w_ffn_down = nrm(ks[6], (DEPTH, 2, D_FF, D_MODEL), D_FF ** -0.5)
    w_in = nrm(ks[7], (DEPTH, D_MODEL, N_IN), D_MODEL ** -0.5)
    b_fgate = 3.0 + nrm(ks[8], (DEPTH, FOX_HEADS), 0.1)
    conv_w = nrm(ks[9], (DEPTH, CONV_WIDTH, LRU_WIDTH), CONV_WIDTH ** -0.5)
    conv_b = nrm(ks[10], (DEPTH, LRU_WIDTH), 0.01)
    w_rgate = nrm(ks[11], (DEPTH, LRU_BLOCKS, HEAD_DIM, HEAD_DIM), HEAD_DIM ** -0.5)
    b_rgate = nrm(ks[12], (DEPTH, LRU_WIDTH), 0.01)
    w_igate = nrm(ks[13], (DEPTH, LRU_BLOCKS, HEAD_DIM, HEAD_DIM), HEAD_DIM ** -0.5)
    b_igate = nrm(ks[14], (DEPTH, LRU_WIDTH), 0.01)
    a_c = jax.random.uniform(ks[15], (DEPTH, LRU_WIDTH), f32, 0.9, 0.999)
    s = a_c ** (1.0 / LRU_C)
    lru_lambda = jnp.log(s) - jnp.log1p(-s)
    g_qk = 1.0 + nrm(ks[16], (DEPTH, 2, HEAD_DIM), 0.05)
    g_mix_out = 1.0 + nrm(ks[17], (DEPTH, D_MIX), 0.05)
    w_out = nrm(ks[18], (DEPTH, D_MIX, D_MODEL), D_MIX ** -0.5)
    return {"x": x, "c": c, "w_ada": w_ada, "b_ada": b_ada, "g_norm": g_norm,
            "w_ffn_up": w_ffn_up, "w_ffn_down": w_ffn_down, "w_in": w_in,
            "b_fgate": b_fgate, "conv_w": conv_w, "conv_b": conv_b,
            "w_rgate": w_rgate, "b_rgate": b_rgate, "w_igate": w_igate,
            "b_igate": b_igate, "lru_lambda": lru_lambda, "g_qk": g_qk,
            "g_mix_out": g_mix_out, "w_out": w_out}


def _fwd_reference(x, c, w_ada, b_ada, g_norm, w_ffn_up, w_ffn_down, w_in, b_fgate, conv_w,
              conv_b, w_rgate, b_rgate, w_igate, b_igate, lru_lambda, g_qk, g_mix_out, w_out):
    B = x.shape[0]
    c_act = jax.nn.silu(c)
    for l in range(DEPTH):
        mod = (c_act @ w_ada[l] + b_ada[l]).reshape(B, N_SUB, 3, D_MODEL)

        def norm_mod(h, j):
            shift = mod[:, j, 0][:, None, :]
            scale = mod[:, j, 1][:, None, :]
            return _rms(h) * g_norm[l, j] * (1.0 + scale) + shift

        def gate(j):
            return (1.0 + mod[:, j, 2])[:, None, :]

        x = x + 0.5 * gate(0) * _swiglu(norm_mod(x, 0), w_ffn_up[l, 0], w_ffn_down[l, 0])
        x = x + gate(1) * _mixer(norm_mod(x, 1), w_in[l], b_fgate[l], conv_w[l], conv_b[l],
                                 w_rgate[l], b_rgate[l], w_igate[l], b_igate[l],
                                 lru_lambda[l], g_qk[l], g_mix_out[l], w_out[l])
        x = x + 0.5 * gate(2) * _swiglu(norm_mod(x, 2), w_ffn_up[l, 1], w_ffn_down[l, 1])
    return x


import jax as _jax
import jax.numpy as _jnp

TWIN_FORMAT = 'train_step'
FWD_PARAMS = ['x', 'c', 'w_ada', 'b_ada', 'g_norm', 'w_ffn_up', 'w_ffn_down', 'w_in', 'b_fgate', 'conv_w', 'conv_b', 'w_rgate', 'b_rgate', 'w_igate', 'b_igate', 'lru_lambda', 'g_qk', 'g_mix_out', 'w_out']
TWIN_WEIGHTS = ['w_ada', 'b_ada', 'g_norm', 'w_ffn_up', 'w_ffn_down', 'w_in', 'b_fgate', 'conv_w', 'conv_b', 'w_rgate', 'b_rgate', 'w_igate', 'b_igate', 'lru_lambda', 'g_qk', 'g_mix_out', 'w_out']
TWIN_DIFF_INPUT = 'x'
TWIN_INPUTS = ['x', 'c', 'w_ada', 'b_ada', 'g_norm', 'w_ffn_up', 'w_ffn_down', 'w_in', 'b_fgate', 'conv_w', 'conv_b', 'w_rgate', 'b_rgate', 'w_igate', 'b_igate', 'lru_lambda', 'g_qk', 'g_mix_out', 'w_out', 'loss_target', 'm_w_ada', 'm_b_ada', 'm_g_norm', 'm_w_ffn_up', 'm_w_ffn_down', 'm_w_in', 'm_b_fgate', 'm_conv_w', 'm_conv_b', 'm_w_rgate', 'm_b_rgate', 'm_w_igate', 'm_b_igate', 'm_lru_lambda', 'm_g_qk', 'm_g_mix_out', 'm_w_out', 'v_w_ada', 'v_b_ada', 'v_g_norm', 'v_w_ffn_up', 'v_w_ffn_down', 'v_w_in', 'v_b_fgate', 'v_conv_w', 'v_conv_b', 'v_w_rgate', 'v_b_rgate', 'v_w_igate', 'v_b_igate', 'v_lru_lambda', 'v_g_qk', 'v_g_mix_out', 'v_w_out']
TWIN_OUTPUTS = ['loss', 'grad_x', 'grad_w_ada', 'grad_b_ada', 'grad_g_norm', 'grad_w_ffn_up', 'grad_w_ffn_down', 'grad_w_in', 'grad_b_fgate', 'grad_conv_w', 'grad_conv_b', 'grad_w_rgate', 'grad_b_rgate', 'grad_w_igate', 'grad_b_igate', 'grad_lru_lambda', 'grad_g_qk', 'grad_g_mix_out', 'grad_w_out', 'delta_w_ada', 'delta_b_ada', 'delta_g_norm', 'delta_w_ffn_up', 'delta_w_ffn_down', 'delta_w_in', 'delta_b_fgate', 'delta_conv_w', 'delta_conv_b', 'delta_w_rgate', 'delta_b_rgate', 'delta_w_igate', 'delta_b_igate', 'delta_lru_lambda', 'delta_g_qk', 'delta_g_mix_out', 'delta_w_out', 'new_m_w_ada', 'new_m_b_ada', 'new_m_g_norm', 'new_m_w_ffn_up', 'new_m_w_ffn_down', 'new_m_w_in', 'new_m_b_fgate', 'new_m_conv_w', 'new_m_conv_b', 'new_m_w_rgate', 'new_m_b_rgate', 'new_m_w_igate', 'new_m_b_igate', 'new_m_lru_lambda', 'new_m_g_qk', 'new_m_g_mix_out', 'new_m_w_out', 'new_v_w_ada', 'new_v_b_ada', 'new_v_g_norm', 'new_v_w_ffn_up', 'new_v_w_ffn_down', 'new_v_w_in', 'new_v_b_fgate', 'new_v_conv_w', 'new_v_conv_b', 'new_v_w_rgate', 'new_v_b_rgate', 'new_v_w_igate', 'new_v_b_igate', 'new_v_lru_lambda', 'new_v_g_qk', 'new_v_g_mix_out', 'new_v_w_out']
TWIN_LEAF_KINDS = {'loss': 'loss', 'grad_x': 'grad_x', 'grad_w_ada': 'grad_w', 'grad_b_ada': 'grad_w', 'grad_g_norm': 'grad_w', 'grad_w_ffn_up': 'grad_w', 'grad_w_ffn_down': 'grad_w', 'grad_w_in': 'grad_w', 'grad_b_fgate': 'grad_w', 'grad_conv_w': 'grad_w', 'grad_conv_b': 'grad_w', 'grad_w_rgate': 'grad_w', 'grad_b_rgate': 'grad_w', 'grad_w_igate': 'grad_w', 'grad_b_igate': 'grad_w', 'grad_lru_lambda': 'grad_w', 'grad_g_qk': 'grad_w', 'grad_g_mix_out': 'grad_w', 'grad_w_out': 'grad_w', 'delta_w_ada': 'delta_w', 'delta_b_ada': 'delta_w', 'delta_g_norm': 'delta_w', 'delta_w_ffn_up': 'delta_w', 'delta_w_ffn_down': 'delta_w', 'delta_w_in': 'delta_w', 'delta_b_fgate': 'delta_w', 'delta_conv_w': 'delta_w', 'delta_conv_b': 'delta_w', 'delta_w_rgate': 'delta_w', 'delta_b_rgate': 'delta_w', 'delta_w_igate': 'delta_w', 'delta_b_igate': 'delta_w', 'delta_lru_lambda': 'delta_w', 'delta_g_qk': 'delta_w', 'delta_g_mix_out': 'delta_w', 'delta_w_out': 'delta_w', 'new_m_w_ada': 'new_m', 'new_m_b_ada': 'new_m', 'new_m_g_norm': 'new_m', 'new_m_w_ffn_up': 'new_m', 'new_m_w_ffn_down': 'new_m', 'new_m_w_in': 'new_m', 'new_m_b_fgate': 'new_m', 'new_m_conv_w': 'new_m', 'new_m_conv_b': 'new_m', 'new_m_w_rgate': 'new_m', 'new_m_b_rgate': 'new_m', 'new_m_w_igate': 'new_m', 'new_m_b_igate': 'new_m', 'new_m_lru_lambda': 'new_m', 'new_m_g_qk': 'new_m', 'new_m_g_mix_out': 'new_m', 'new_m_w_out': 'new_m', 'new_v_w_ada': 'new_v', 'new_v_b_ada': 'new_v', 'new_v_g_norm': 'new_v', 'new_v_w_ffn_up': 'new_v', 'new_v_w_ffn_down': 'new_v', 'new_v_w_in': 'new_v', 'new_v_b_fgate': 'new_v', 'new_v_conv_w': 'new_v', 'new_v_conv_b': 'new_v', 'new_v_w_rgate': 'new_v', 'new_v_b_rgate': 'new_v', 'new_v_w_igate': 'new_v', 'new_v_b_igate': 'new_v', 'new_v_lru_lambda': 'new_v', 'new_v_g_qk': 'new_v', 'new_v_g_mix_out': 'new_v', 'new_v_w_out': 'new_v'}


def _forward(args):
    return _fwd_reference(*[args[k] for k in FWD_PARAMS])


def _output_shape():
    out = _jax.eval_shape(lambda: _forward(_fwd_setup_inputs(0)))
    return out.shape, out.dtype

N_MICROBATCH = 1
ADAM_LR = 0.001
ADAM_B1 = 0.9
ADAM_B2 = 0.999
ADAM_EPS = 1e-08
ADAM_WD = 0.01
ADAM_STEP = 10
PER_EXAMPLE_BATCH_AXIS = {'x': 0, 'c': 0, 'loss_target': 0}
SHARED_INPUTS = []
_WEIGHT_DTYPES = {'w_ada': _jnp.float32, 'b_ada': _jnp.float32, 'g_norm': _jnp.float32, 'w_ffn_up': _jnp.float32, 'w_ffn_down': _jnp.float32, 'w_in': _jnp.float32, 'b_fgate': _jnp.float32, 'conv_w': _jnp.float32, 'conv_b': _jnp.float32, 'w_rgate': _jnp.float32, 'b_rgate': _jnp.float32, 'w_igate': _jnp.float32, 'b_igate': _jnp.float32, 'lru_lambda': _jnp.float32, 'g_qk': _jnp.float32, 'g_mix_out': _jnp.float32, 'w_out': _jnp.float32}
MOMENT_SCALE = {'w_ada': 1.616899e+01, 'b_ada': 3.660617e+01, 'g_norm': 9.702967e+00, 'w_ffn_up': 3.552184e-01, 'w_ffn_down': 6.396489e-01, 'w_in': 3.339689e+00, 'b_fgate': 2.017861e+01, 'conv_w': 1.138708e+01, 'conv_b': 7.509518e+01, 'w_rgate': 3.053067e+00, 'b_rgate': 1.945169e+00, 'w_igate': 5.565048e+00, 'b_igate': 4.705268e+00, 'lru_lambda': 3.521245e+00, 'g_qk': 1.115822e+00, 'g_mix_out': 9.105058e+01, 'w_out': 8.085843e+00}


def _to_microbatches(a, axis):
    t = _jnp.moveaxis(a, axis, 0)
    t = t.reshape((N_MICROBATCH, t.shape[0] // N_MICROBATCH) + t.shape[1:])
    return _jnp.moveaxis(t, 1, axis + 1)


def setup_inputs(seed: int = 0) -> dict:
    inp = _fwd_setup_inputs(seed)
    key = _jax.random.fold_in(_jax.random.key(seed), 7919)
    shape, _ = _output_shape()
    out = dict(inp)
    out["loss_target"] = _jax.random.normal(_jax.random.fold_in(key, 0), shape, _jnp.float32)
    for i, name in enumerate(TWIN_WEIGHTS):
        w = inp[name].astype(_jnp.float32)
        if MOMENT_SCALE is None:
            s = _jnp.sqrt(_jnp.mean(_jnp.square(w)) + 1e-30)
        else:
            s = MOMENT_SCALE[name]
        km, kv = _jax.random.split(_jax.random.fold_in(key, i + 1))
        out[name] = w
        out["m_" + name] = s * _jax.random.normal(km, w.shape, _jnp.float32)
        out["v_" + name] = (s * s) * _jax.random.uniform(kv, w.shape, _jnp.float32, 0.5, 1.5)
    if N_MICROBATCH > 1:
        for name, axis in PER_EXAMPLE_BATCH_AXIS.items():
            out[name] = _to_microbatches(out[name], axis)
    return {'x': out['x'], 'c': out['c'], 'w_ada': out['w_ada'], 'b_ada': out['b_ada'], 'g_norm': out['g_norm'], 'w_ffn_up': out['w_ffn_up'], 'w_ffn_down': out['w_ffn_down'], 'w_in': out['w_in'], 'b_fgate': out['b_fgate'], 'conv_w': out['conv_w'], 'conv_b': out['conv_b'], 'w_rgate': out['w_rgate'], 'b_rgate': out['b_rgate'], 'w_igate': out['w_igate'], 'b_igate': out['b_igate'], 'lru_lambda': out['lru_lambda'], 'g_qk': out['g_qk'], 'g_mix_out': out['g_mix_out'], 'w_out': out['w_out'], 'loss_target': out['loss_target'], 'm_w_ada': out['m_w_ada'], 'm_b_ada': out['m_b_ada'], 'm_g_norm': out['m_g_norm'], 'm_w_ffn_up': out['m_w_ffn_up'], 'm_w_ffn_down': out['m_w_ffn_down'], 'm_w_in': out['m_w_in'], 'm_b_fgate': out['m_b_fgate'], 'm_conv_w': out['m_conv_w'], 'm_conv_b': out['m_conv_b'], 'm_w_rgate': out['m_w_rgate'], 'm_b_rgate': out['m_b_rgate'], 'm_w_igate': out['m_w_igate'], 'm_b_igate': out['m_b_igate'], 'm_lru_lambda': out['m_lru_lambda'], 'm_g_qk': out['m_g_qk'], 'm_g_mix_out': out['m_g_mix_out'], 'm_w_out': out['m_w_out'], 'v_w_ada': out['v_w_ada'], 'v_b_ada': out['v_b_ada'], 'v_g_norm': out['v_g_norm'], 'v_w_ffn_up': out['v_w_ffn_up'], 'v_w_ffn_down': out['v_w_ffn_down'], 'v_w_in': out['v_w_in'], 'v_b_fgate': out['v_b_fgate'], 'v_conv_w': out['v_conv_w'], 'v_conv_b': out['v_conv_b'], 'v_w_rgate': out['v_w_rgate'], 'v_b_rgate': out['v_b_rgate'], 'v_w_igate': out['v_w_igate'], 'v_b_igate': out['v_b_igate'], 'v_lru_lambda': out['v_lru_lambda'], 'v_g_qk': out['v_g_qk'], 'v_g_mix_out': out['v_g_mix_out'], 'v_w_out': out['v_w_out']}


def _loss(weights, diff, rest, loss_target):
    with _jax.named_scope("forward"):
        args = {**rest, TWIN_DIFF_INPUT: diff, **{k: w.astype(_WEIGHT_DTYPES[k]) for k, w in weights.items()}}
        y = _forward(args)
    with _jax.named_scope("loss_head"):
        err = _jnp.square(y.astype(_jnp.float32) - loss_target)
        return 0.5 * _jnp.sum(_jnp.mean(err, axis=-1)) if err.ndim else 0.5 * err


def _adamw(w, g, m, v):
    m = ADAM_B1 * m + (1.0 - ADAM_B1) * g
    v = ADAM_B2 * v + (1.0 - ADAM_B2) * _jnp.square(g)
    m_hat = m / (1.0 - ADAM_B1 ** ADAM_STEP)
    v_hat = v / (1.0 - ADAM_B2 ** ADAM_STEP)
    delta = -ADAM_LR * (m_hat / (_jnp.sqrt(v_hat) + ADAM_EPS) + ADAM_WD * w)
    return delta, m, v


def reference(x, c, w_ada, b_ada, g_norm, w_ffn_up, w_ffn_down, w_in, b_fgate, conv_w, conv_b, w_rgate, b_rgate, w_igate, b_igate, lru_lambda, g_qk, g_mix_out, w_out, loss_target, m_w_ada, m_b_ada, m_g_norm, m_w_ffn_up, m_w_ffn_down, m_w_in, m_b_fgate, m_conv_w, m_conv_b, m_w_rgate, m_b_rgate, m_w_igate, m_b_igate, m_lru_lambda, m_g_qk, m_g_mix_out, m_w_out, v_w_ada, v_b_ada, v_g_norm, v_w_ffn_up, v_w_ffn_down, v_w_in, v_b_fgate, v_conv_w, v_conv_b, v_w_rgate, v_b_rgate, v_w_igate, v_b_igate, v_lru_lambda, v_g_qk, v_g_mix_out, v_w_out):
    given = dict(x=x, c=c, w_ada=w_ada, b_ada=b_ada, g_norm=g_norm, w_ffn_up=w_ffn_up, w_ffn_down=w_ffn_down, w_in=w_in, b_fgate=b_fgate, conv_w=conv_w, conv_b=conv_b, w_rgate=w_rgate, b_rgate=b_rgate, w_igate=w_igate, b_igate=b_igate, lru_lambda=lru_lambda, g_qk=g_qk, g_mix_out=g_mix_out, w_out=w_out, loss_target=loss_target, m_w_ada=m_w_ada, m_b_ada=m_b_ada, m_g_norm=m_g_norm, m_w_ffn_up=m_w_ffn_up, m_w_ffn_down=m_w_ffn_down, m_w_in=m_w_in, m_b_fgate=m_b_fgate, m_conv_w=m_conv_w, m_conv_b=m_conv_b, m_w_rgate=m_w_rgate, m_b_rgate=m_b_rgate, m_w_igate=m_w_igate, m_b_igate=m_b_igate, m_lru_lambda=m_lru_lambda, m_g_qk=m_g_qk, m_g_mix_out=m_g_mix_out, m_w_out=m_w_out, v_w_ada=v_w_ada, v_b_ada=v_b_ada, v_g_norm=v_g_norm, v_w_ffn_up=v_w_ffn_up, v_w_ffn_down=v_w_ffn_down, v_w_in=v_w_in, v_b_fgate=v_b_fgate, v_conv_w=v_conv_w, v_conv_b=v_conv_b, v_w_rgate=v_w_rgate, v_b_rgate=v_b_rgate, v_w_igate=v_w_igate, v_b_igate=v_b_igate, v_lru_lambda=v_lru_lambda, v_g_qk=v_g_qk, v_g_mix_out=v_g_mix_out, v_w_out=v_w_out)
    weights = {n: given[n] for n in TWIN_WEIGHTS}
    shared = {n: given[n] for n in SHARED_INPUTS}
    per_example = {n: given[n] for n in ['x', 'c']}
    grad_fn = _jax.value_and_grad(_loss, argnums=(0, 1))

    def one_microbatch(ex, loss_target):
        ex = dict(ex)
        diff = ex.pop(TWIN_DIFF_INPUT)
        return grad_fn(weights, diff, {**shared, **ex}, loss_target)

    if N_MICROBATCH == 1:
        loss, (grad_w, grad_x) = one_microbatch(per_example, given["loss_target"])
    else:
        def body(carry, xs):
            loss_sum, grad_sum = carry
            l_k, (gw_k, gx_k) = one_microbatch(xs[0], xs[1])
            with _jax.named_scope("update"):
                return (loss_sum + l_k, _jax.tree.map(_jnp.add, grad_sum, gw_k)), gx_k

        init = (_jnp.zeros((), _jnp.float32), _jax.tree.map(_jnp.zeros_like, weights))
        (loss, grad_w), grad_x = _jax.lax.scan(body, init, (per_example, given["loss_target"]))
    with _jax.named_scope("update"):
        delta_w, new_m, new_v = {}, {}, {}
        for n in TWIN_WEIGHTS:
            delta_w[n], new_m[n], new_v[n] = _adamw(weights[n], grad_w[n], given["m_" + n], given["v_" + n])
    return (loss, grad_x, *[grad_w[n] for n in TWIN_WEIGHTS], *[delta_w[n] for n in TWIN_WEIGHTS],
            *[new_m[n] for n in TWIN_WEIGHTS], *[new_v[n] for n in TWIN_WEIGHTS])
```

```python
import math

import jax
import jax.numpy as jnp
from jax import lax
from jax.experimental import pallas as pl
from jax.experimental.pallas import tpu as pltpu

F32 = jnp.float32
BF16 = jnp.bfloat16
MESH = pl.DeviceIdType.MESH

D = 1024
HD = 64
LW = 512
NH = 4
AW = NH * HD
N_IN = 2564
N_INP = 2688
F_BLK = 2560 // 128
EPS = 1e-6
LRU_C = 8.0
SCALE = HD ** -0.5
NEG = -1e30
TQ = 256
TK = 128

ADAM_LR, ADAM_B1, ADAM_B2, ADAM_EPS, ADAM_WD, ADAM_STEP = 0.001, 0.9, 0.999, 1e-08, 0.01, 10

VMEM_BIG = 56 * 1024 * 1024
N_DEV = 8
N_SHARD = 4
ROW = 1024


def _cp(sem, vmem=None):
    return pltpu.CompilerParams(dimension_semantics=sem, vmem_limit_bytes=vmem)


def _dot(a, b):
    return jnp.dot(a, b, preferred_element_type=F32)


def _dot_nt(a, b):
    return lax.dot_general(a, b, (((1,), (1,)), ((), ())), preferred_element_type=F32)


def _dot_tn(a, b):
    return lax.dot_general(a, b, (((0,), (0,)), ((), ())), preferred_element_type=F32)


def _log1p(e):
    small = e * (1.0 - e * (0.5 - e * (1.0 / 3.0 - e * 0.25)))
    return jnp.where(e < 0.01, small, jnp.log(1.0 + e))


def _expm1_neg(x):
    small = x * (1.0 + x * 0.5 * (1.0 + x * (1.0 / 3.0) * (1.0 + x * 0.25 * (1.0 + x * 0.2))))
    return jnp.where(x > -0.05, small, jnp.exp(x) - 1.0)


def _sigmoid(x):
    return 1.0 / (1.0 + jnp.exp(-x))


_GELU_C = math.sqrt(2.0 / math.pi)


def _gelu_and_grad(x):
    x2 = x * x
    th = jnp.tanh(_GELU_C * (x + 0.044715 * x * x2))
    g = 0.5 * x * (1.0 + th)
    dg = 0.5 * (1.0 + th) + 0.5 * x * (1.0 - th * th) * _GELU_C * (1.0 + 3.0 * 0.044715 * x2)
    return g, dg


def _rms_rows(x):
    rstd = lax.rsqrt(jnp.mean(x * x, axis=-1, keepdims=True) + EPS)
    return x * rstd, rstd


def _rms_bwd(xn, rstd, dyn):
    return rstd * (dyn - xn * jnp.mean(dyn * xn, axis=-1, keepdims=True))


def _colsum(x):
    return jnp.sum(x, axis=0, keepdims=True)


def _rowsum(x):
    return jnp.sum(x, axis=1, keepdims=True)


def _split3(x):
    hi = x.astype(BF16)
    r = x - hi.astype(F32)
    mid = r.astype(BF16)
    lo = (r - mid.astype(F32)).astype(BF16)
    return hi, mid, lo


def _cumsum_mm(x, ones_tri, parts=3):
    ps = _split3(x)[:parts]
    acc = _dot(ps[0], ones_tri)
    for p in ps[1:]:
        acc = acc + _dot(p, ones_tri)
    return acc


def _tri(n, kind):
    r = lax.broadcasted_iota(jnp.int32, (n, n), 0)
    c = lax.broadcasted_iota(jnp.int32, (n, n), 1)
    m = {"row_gt_col": r > c, "row_le_col": r <= c, "row_lt_col": r < c}[kind]
    return jnp.where(m, 1.0, 0.0).astype(BF16)


def _normmod(x, mod_ref):
    xn, rstd = _rms_rows(x)
    h = xn * mod_ref[3:4, :] * mod_ref[1:2, :] + mod_ref[0:1, :]
    return h, xn, rstd


def _normmod_bwd(dh, xn, rstd, mod_ref, dmod_ref, wacc_ref):
    gn = mod_ref[3:4, :]
    sc = mod_ref[1:2, :]
    dmod_ref[0:1, :] += _colsum(dh)
    t = _colsum(dh * xn)
    dmod_ref[1:2, :] += t * gn
    wacc_ref[0:1, :] += t * sc
    return _rms_bwd(xn, rstd, dh * (gn * sc))


def _tile(n, want):
    t = min(n, want)
    while n % t:
        t //= 2
    return t


def _tile_div8(n, cap):
    best = 8
    for t in range(8, min(n, cap) + 1, 8):
        if n % t == 0:
            best = t
    return best


def _ffn_fwd(x, mod, wup5, wdn4, l, j, S):
    T = x.shape[0]
    tf = wup5.shape[-1]
    nk = 2
    tm = _tile(S, 512)
    tpb = S // tm

    def body(x_ref, mod_ref, wg_ref, wu_ref, wd_ref, xo_ref, g_ref, u_ref, f_ref, h_sc, acc_sc):
        k = pl.program_id(1)

        @pl.when(k == 0)
        def _():
            h, _, _ = _normmod(x_ref[...], mod_ref)
            h_sc[...] = h.astype(BF16)
            acc_sc[...] = jnp.zeros_like(acc_sc)

        h = h_sc[...]
        g = _dot(h, wg_ref[...])
        u = _dot(h, wu_ref[...])
        g_ref[...] = g.astype(BF16)
        u_ref[...] = u.astype(BF16)
        a = (g * _sigmoid(g)) * u
        acc_sc[...] += _dot(a.astype(BF16), wd_ref[...])

        @pl.when(k == nk - 1)
        def _():
            f = acc_sc[...]
            f_ref[...] = f.astype(BF16)
            xo_ref[...] = x_ref[...] + (0.5 * mod_ref[2:3, :]) * f

    return pl.pallas_call(
        body, name=f"ffn_fwd_{l}_{j}",
        grid=(T // tm, nk),
        in_specs=[
            pl.BlockSpec((tm, D), lambda i, k: (i, 0)),
            pl.BlockSpec((None, 8, D), lambda i, k: (i // tpb, 0, 0)),
            pl.BlockSpec((None, None, None, D, tf), lambda i, k: (k, l, j, 0, 0)),
            pl.BlockSpec((None, None, None, D, tf), lambda i, k: (nk + k, l, j, 0, 0)),
            pl.BlockSpec((None, None, tf, D), lambda i, k: (l, j, k, 0)),
        ],
        out_specs=[
            pl.BlockSpec((tm, D), lambda i, k: (i, 0)),
            pl.BlockSpec((tm, tf), lambda i, k: (i, k)),
            pl.BlockSpec((tm, tf), lambda i, k: (i, k)),
            pl.BlockSpec((tm, D), lambda i, k: (i, 0)),
        ],
        out_shape=[
            jax.ShapeDtypeStruct((T, D), F32),
            jax.ShapeDtypeStruct((T, nk * tf), BF16),
            jax.ShapeDtypeStruct((T, nk * tf), BF16),
            jax.ShapeDtypeStruct((T, D), BF16),
        ],
        scratch_shapes=[pltpu.VMEM((tm, D), BF16), pltpu.VMEM((tm, D), F32)],
        compiler_params=_cp(("arbitrary", "arbitrary"), VMEM_BIG),
    )(x, mod, wup5, wup5, wdn4)


def _ffn_bwd(x, dy, mod, f, g, u, wup5, wdn4, l, j, S):
    T = x.shape[0]
    tf = wup5.shape[-1]
    nk = 2
    tm = _tile(S, 256)
    tpb = S // tm

    def body(x_ref, dy_ref, mod_ref, f_ref, g_ref, u_ref, wg_ref, wu_ref, wd_ref,
             dx_ref, dmod_ref, wacc_ref, h_ref, df_ref, a_ref, dg_ref, du_ref, df_sc, dh_sc):
        i = pl.program_id(0)
        k = pl.program_id(1)

        @pl.when((i == 0) & (k == 0))
        def _():
            wacc_ref[...] = jnp.zeros_like(wacc_ref)

        @pl.when((i % tpb == 0) & (k == 0))
        def _():
            dmod_ref[...] = jnp.zeros_like(dmod_ref)

        @pl.when(k == 0)
        def _():
            dy_ = dy_ref[...]
            h, _, _ = _normmod(x_ref[...], mod_ref)
            h_ref[...] = h.astype(BF16)
            dfb = ((0.5 * mod_ref[2:3, :]) * dy_).astype(BF16)
            df_sc[...] = dfb
            df_ref[...] = dfb
            dmod_ref[2:3, :] += _colsum(0.5 * f_ref[...].astype(F32) * dy_)
            dh_sc[...] = jnp.zeros_like(dh_sc)

        da = _dot_nt(df_sc[...], wd_ref[...])
        gg = g_ref[...].astype(F32)
        uu = u_ref[...].astype(F32)
        sig = _sigmoid(gg)
        s = gg * sig
        a_ref[...] = (s * uu).astype(BF16)
        du = (da * s).astype(BF16)
        dg = (da * uu * (sig * (1.0 + gg * (1.0 - sig)))).astype(BF16)
        dg_ref[...] = dg
        du_ref[...] = du
        dh_sc[...] += _dot_nt(dg, wg_ref[...]) + _dot_nt(du, wu_ref[...])

        @pl.when(k == nk - 1)
        def _():
            _, xn, rstd = _normmod(x_ref[...], mod_ref)
            dx_ref[...] = dy_ref[...] + _normmod_bwd(dh_sc[...], xn, rstd, mod_ref, dmod_ref, wacc_ref)

    return pl.pallas_call(
        body, name=f"ffn_bwd_{l}_{j}",
        grid=(T // tm, nk),
        in_specs=[
            pl.BlockSpec((tm, D), lambda i, k: (i, 0)),
            pl.BlockSpec((tm, D), lambda i, k: (i, 0)),
            pl.BlockSpec((None, 8, D), lambda i, k: (i // tpb, 0, 0)),
            pl.BlockSpec((tm, D), lambda i, k: (i, 0)),
            pl.BlockSpec((tm, tf), lambda i, k: (i, k)),
            pl.BlockSpec((tm, tf), lambda i, k: (i, k)),
            pl.BlockSpec((None, None, None, D, tf), lambda i, k: (k, l, j, 0, 0)),
            pl.BlockSpec((None, None, None, D, tf), lambda i, k: (nk + k, l, j, 0, 0)),
            pl.BlockSpec((None, None, tf, D), lambda i, k: (l, j, k, 0)),
        ],
        out_specs=[
            pl.BlockSpec((tm, D), lambda i, k: (i, 0)),
            pl.BlockSpec((None, 8, D), lambda i, k: (i // tpb, 0, 0)),
            pl.BlockSpec((8, D), lambda i, k: (0, 0)),
            pl.BlockSpec((tm, D), lambda i, k: (i, 0)),
            pl.BlockSpec((tm, D), lambda i, k: (i, 0)),
            pl.BlockSpec((tm, tf), lambda i, k: (i, k)),
            pl.BlockSpec((tm, tf), lambda i, k: (i, k)),
            pl.BlockSpec((tm, tf), lambda i, k: (i, k)),
        ],
        out_shape=[
            jax.ShapeDtypeStruct((T, D), F32),
            jax.ShapeDtypeStruct((T // S, 8, D), F32),
            jax.ShapeDtypeStruct((8, D), F32),
            jax.ShapeDtypeStruct((T, D), BF16),
            jax.ShapeDtypeStruct((T, D), BF16),
            jax.ShapeDtypeStruct((T, nk * tf), BF16),
            jax.ShapeDtypeStruct((T, nk * tf), BF16),
            jax.ShapeDtypeStruct((T, nk * tf), BF16),
        ],
        scratch_shapes=[pltpu.VMEM((tm, D), BF16), pltpu.VMEM((tm, D), F32)],
        compiler_params=_cp(("arbitrary", "arbitrary"), VMEM_BIG),
    )(x, dy, mod, f, g, u, wup5, wup5, wdn4)


def _mm_tn(a, b, name, tma=None, tnb=None, split_n=False):
    T, M = a.shape
    N = b.shape[1]
    tma = tma or M
    tnb = tnb or N
    tt = _tile(T, 1024)

    def body(a_ref, b_ref, o_ref):
        @pl.when(pl.program_id(2) == 0)
        def _():
            o_ref[...] = jnp.zeros_like(o_ref)

        o_ref[...] += _dot_tn(a_ref[...], b_ref[...])

    if split_n:
        out_shape = jax.ShapeDtypeStruct((N // tnb, M, tnb), F32)
        out_spec = pl.BlockSpec((None, tma, tnb), lambda m, n, t: (n, m, 0))
    else:
        out_shape = jax.ShapeDtypeStruct((M, N), F32)
        out_spec = pl.BlockSpec((tma, tnb), lambda m, n, t: (m, n))
    return pl.pallas_call(
        body, name=name,
        grid=(M // tma, N // tnb, T // tt),
        in_specs=[pl.BlockSpec((tt, tma), lambda m, n, t: (t, m)),
                  pl.BlockSpec((tt, tnb), lambda m, n, t: (t, n))],
        out_specs=out_spec,
        out_shape=out_shape,
        compiler_params=_cp(("arbitrary", "arbitrary", "arbitrary"), VMEM_BIG),
    )(a, b)


def _mix_in_fwd(x, mod, winp, l, S):
    T = x.shape[0]
    tm = _tile(S, 512)
    tpb = S // tm

    def body(x_ref, mod_ref, w_ref, h_ref, p_ref):
        h, _, _ = _normmod(x_ref[...], mod_ref)
        hb = h.astype(BF16)
        h_ref[...] = hb
        p_ref[...] = _dot(hb, w_ref[...])

    return pl.pallas_call(
        body, name=f"mix_in_fwd_{l}",
        grid=(T // tm,),
        in_specs=[pl.BlockSpec((tm, D), lambda i: (i, 0)),
                  pl.BlockSpec((None, 8, D), lambda i: (i // tpb, 0, 0)),
                  pl.BlockSpec((None, D, N_INP), lambda i: (l, 0, 0))],
        out_specs=[pl.BlockSpec((tm, D), lambda i: (i, 0)),
                   pl.BlockSpec((tm, N_INP), lambda i: (i, 0))],
        out_shape=[jax.ShapeDtypeStruct((T, D), BF16), jax.ShapeDtypeStruct((T, N_INP), F32)],
        compiler_params=_cp(("arbitrary",), VMEM_BIG),
    )(x, mod, winp)


def _mix_in_bwd(x, dres, mod, dproj, winp, l, S):
    T = x.shape[0]
    tm = _tile(S, 512)
    tpb = S // tm

    def body(x_ref, dr_ref, mod_ref, dp_ref, w_ref, dx_ref, dmod_ref, wacc_ref):
        i = pl.program_id(0)

        @pl.when(i == 0)
        def _():
            wacc_ref[...] = jnp.zeros_like(wacc_ref)

        @pl.when(i % tpb == 0)
        def _():
            dmod_ref[...] = jnp.zeros_like(dmod_ref)

        dh = _dot_nt(dp_ref[...], w_ref[...])
        _, xn, rstd = _normmod(x_ref[...], mod_ref)
        dx_ref[...] = dr_ref[...] + _normmod_bwd(dh, xn, rstd, mod_ref, dmod_ref, wacc_ref)

    return pl.pallas_call(
        body, name=f"mix_in_bwd_{l}",
        grid=(T // tm,),
        in_specs=[pl.BlockSpec((tm, D), lambda i: (i, 0)),
                  pl.BlockSpec((tm, D), lambda i: (i, 0)),
                  pl.BlockSpec((None, 8, D), lambda i: (i // tpb, 0, 0)),
                  pl.BlockSpec((tm, N_INP), lambda i: (i, 0)),
                  pl.BlockSpec((None, D, N_INP), lambda i: (l, 0, 0))],
        out_specs=[pl.BlockSpec((tm, D), lambda i: (i, 0)),
                   pl.BlockSpec((None, 8, D), lambda i: (i // tpb, 0, 0)),
                   pl.BlockSpec((8, D), lambda i: (0, 0))],
        out_shape=[jax.ShapeDtypeStruct((T, D), F32),
                   jax.ShapeDtypeStruct((T // S, 8, D), F32),
                   jax.ShapeDtypeStruct((8, D), F32)],
        compiler_params=_cp(("arbitrary",), VMEM_BIG),
    )(x, dres, mod, dproj, winp)


_GROUPS = ((0, LW), (LW, LW + AW), (LW + AW, D))


def _mix_out_fwd(x, ylru, osb, ofox, mod, gmix, wout, l, S):
    T = x.shape[0]
    tm = _tile(S, 512)
    tpb = S // tm

    def body(x_ref, yl_ref, sb_ref, fx_ref, mod_ref, gm_ref, w_ref, xo_ref, y_ref, mo_ref):
        for src, (lo, hi) in zip((yl_ref, sb_ref, fx_ref), _GROUPS):
            vn, _ = _rms_rows(src[...])
            y_ref[:, lo:hi] = (vn * gm_ref[0:1, lo:hi]).astype(BF16)
        mo = _dot(y_ref[...], w_ref[...])
        mo_ref[...] = mo.astype(BF16)
        xo_ref[...] = x_ref[...] + mod_ref[2:3, :] * mo

    return pl.pallas_call(
        body, name=f"mix_out_fwd_{l}",
        grid=(T // tm,),
        in_specs=[pl.BlockSpec((tm, D), lambda i: (i, 0)),
                  pl.BlockSpec((tm, LW), lambda i: (i, 0)),
                  pl.BlockSpec((tm, AW), lambda i: (i, 0)),
                  pl.BlockSpec((tm, AW), lambda i: (i, 0)),
                  pl.BlockSpec((None, 8, D), lambda i: (i // tpb, 0, 0)),
                  pl.BlockSpec((None, 8, D), lambda i: (l, 0, 0)),
                  pl.BlockSpec((None, D, D), lambda i: (l, 0, 0))],
        out_specs=[pl.BlockSpec((tm, D), lambda i: (i, 0)),
                   pl.BlockSpec((tm, D), lambda i: (i, 0)),
                   pl.BlockSpec((tm, D), lambda i: (i, 0))],
        out_shape=[jax.ShapeDtypeStruct((T, D), F32),
                   jax.ShapeDtypeStruct((T, D), BF16),
                   jax.ShapeDtypeStruct((T, D), BF16)],
        compiler_params=_cp(("arbitrary",), VMEM_BIG),
    )(x, ylru, osb, ofox, mod, gmix, wout)


def _mix_out_bwd(dx2, ylru, osb, ofox, mo, mod, gmix, wout, l, S):
    T = dx2.shape[0]
    tm = _tile(S, 512)
    tpb = S // tm

    def body(dx_ref, yl_ref, sb_ref, fx_ref, mo_ref, mod_ref, gm_ref, w_ref,
             dyl_ref, dsb_ref, dfx_ref, dmo_ref, dmod_ref, wacc_ref):
        i = pl.program_id(0)

        @pl.when(i == 0)
        def _():
            wacc_ref[...] = jnp.zeros_like(wacc_ref)

        @pl.when(i % tpb == 0)
        def _():
            dmod_ref[...] = jnp.zeros_like(dmod_ref)

        dx = dx_ref[...]
        dmod_ref[2:3, :] += _colsum(mo_ref[...].astype(F32) * dx)
        dmo = (mod_ref[2:3, :] * dx).astype(BF16)
        dmo_ref[...] = dmo
        dy = _dot_nt(dmo, w_ref[...])
        for src, dst, (lo, hi) in zip((yl_ref, sb_ref, fx_ref), (dyl_ref, dsb_ref, dfx_ref), _GROUPS):
            vn, rstd = _rms_rows(src[...])
            dyg = dy[:, lo:hi]
            wacc_ref[0:1, lo:hi] += _colsum(dyg * vn)
            dst[...] = _rms_bwd(vn, rstd, dyg * gm_ref[0:1, lo:hi])

    return pl.pallas_call(
        body, name=f"mix_out_bwd_{l}",
        grid=(T // tm,),
        in_specs=[pl.BlockSpec((tm, D), lambda i: (i, 0)),
                  pl.BlockSpec((tm, LW), lambda i: (i, 0)),
                  pl.BlockSpec((tm, AW), lambda i: (i, 0)),
                  pl.BlockSpec((tm, AW), lambda i: (i, 0)),
                  pl.BlockSpec((tm, D), lambda i: (i, 0)),
                  pl.BlockSpec((None, 8, D), lambda i: (i // tpb, 0, 0)),
                  pl.BlockSpec((None, 8, D), lambda i: (l, 0, 0)),
                  pl.BlockSpec((None, D, D), lambda i: (l, 0, 0))],
        out_specs=[pl.BlockSpec((tm, LW), lambda i: (i, 0)),
                   pl.BlockSpec((tm, AW), lambda i: (i, 0)),
                   pl.BlockSpec((tm, AW), lambda i: (i, 0)),
                   pl.BlockSpec((tm, D), lambda i: (i, 0)),
                   pl.BlockSpec((None, 8, D), lambda i: (i // tpb, 0, 0)),
                   pl.BlockSpec((8, D), lambda i: (0, 0))],
        out_shape=[jax.ShapeDtypeStruct((T, LW), F32),
                   jax.ShapeDtypeStruct((T, AW), F32),
                   jax.ShapeDtypeStruct((T, AW), F32),
                   jax.ShapeDtypeStruct((T, D), BF16),
                   jax.ShapeDtypeStruct((T // S, 8, D), F32),
                   jax.ShapeDtypeStruct((8, D), F32)],
        compiler_params=_cp(("arbitrary",), VMEM_BIG),
    )(dx2, ylru, osb, ofox, mo, mod, gmix, wout)


def _loss_head(y, tgt, S):
    T = y.shape[0]
    tm = _tile(S, 512)

    def body(y_ref, t_ref, dy_ref, l_ref):
        @pl.when(pl.program_id(0) == 0)
        def _():
            l_ref[...] = jnp.zeros_like(l_ref)

        d = y_ref[...] - t_ref[...]
        dy_ref[...] = d * (1.0 / D)
        l_ref[...] += (0.5 / D) * _rowsum(_colsum(d * d))

    return pl.pallas_call(
        body, name="loss_head",
        grid=(T // tm,),
        in_specs=[pl.BlockSpec((tm, D), lambda i: (i, 0)), pl.BlockSpec((tm, D), lambda i: (i, 0))],
        out_specs=[pl.BlockSpec((tm, D), lambda i: (i, 0)), pl.BlockSpec((8, 128), lambda i: (0, 0))],
        out_shape=[jax.ShapeDtypeStruct((T, D), F32), jax.ShapeDtypeStruct((8, 128), F32)],
        compiler_params=_cp(("arbitrary",)),
    )(y, tgt)


def _lru_gates(u, vp_ref, wr_ref, wi_ref):
    ub = u.astype(BF16)
    r = _sigmoid(_dot(ub, wr_ref[...]) + vp_ref[1:2, :])
    ig = _sigmoid(_dot(ub, wi_ref[...]) + vp_ref[2:3, :])
    lam = vp_ref[3:4, :]
    sp = jnp.maximum(-lam, 0.0) + _log1p(jnp.exp(-jnp.abs(lam)))
    log_a = (-LRU_C) * r * sp
    a = jnp.exp(log_a)
    mult = jnp.sqrt(-_expm1_neg(2.0 * log_a))
    return ub, r, ig, sp, a, mult


def _conv_taps(x, xp, row, cw_ref):
    xs = [x]
    for d in (1, 2, 3):
        xs.append(jnp.where(row >= d, pltpu.roll(x, d, 0), pltpu.roll(xp, d, 0)))
    u = xs[0] * cw_ref[3:4, :]
    for d in (1, 2, 3):
        u = u + xs[d] * cw_ref[3 - d:4 - d, :]
    return xs, u


def _lru_fwd(proj, cw, vp, wr, wi, l, S):
    T = proj.shape[0]
    ts = _tile(S, 256)
    nb = S // ts

    def body(x_ref, lg_ref, cw_ref, vp_ref, wr_ref, wi_ref, y_ref, h_ref, xp_sc, hc_sc):
        @pl.when(pl.program_id(1) == 0)
        def _():
            xp_sc[...] = jnp.zeros_like(xp_sc)
            hc_sc[...] = jnp.zeros_like(hc_sc)

        row = lax.broadcasted_iota(jnp.int32, (ts, LW), 0)
        x = x_ref[...]
        _, u = _conv_taps(x, xp_sc[...], row, cw_ref)
        u = u + vp_ref[0:1, :]
        xp_sc[...] = x
        _, _, ig, _, a, mult = _lru_gates(u, vp_ref, wr_ref, wi_ref)
        bv = mult * (ig * u)
        av = a
        d = 1
        while d < ts:
            a_s = jnp.where(row >= d, pltpu.roll(av, d, 0), 1.0)
            b_s = jnp.where(row >= d, pltpu.roll(bv, d, 0), 0.0)
            bv = av * b_s + bv
            av = av * a_s
            d *= 2
        h = bv + av * hc_sc[7:8, :]
        hc_sc[...] = h[ts - 8:ts, :]
        h_ref[...] = h
        gl, _ = _gelu_and_grad(lg_ref[...])
        y_ref[...] = h * gl

    return pl.pallas_call(
        body, name=f"lru_fwd_{l}",
        grid=(T // S, nb),
        in_specs=[pl.BlockSpec((ts, LW), lambda b, j: (b * nb + j, 0)),
                  pl.BlockSpec((ts, LW), lambda b, j: (b * nb + j, 1)),
                  pl.BlockSpec((None, 8, LW), lambda b, j: (l, 0, 0)),
                  pl.BlockSpec((None, 8, LW), lambda b, j: (l, 0, 0)),
                  pl.BlockSpec((None, LW, LW), lambda b, j: (l, 0, 0)),
                  pl.BlockSpec((None, LW, LW), lambda b, j: (l, 0, 0))],
        out_specs=[pl.BlockSpec((ts, LW), lambda b, j: (b * nb + j, 0)),
                   pl.BlockSpec((ts, LW), lambda b, j: (b * nb + j, 0))],
        out_shape=[jax.ShapeDtypeStruct((T, LW), F32), jax.ShapeDtypeStruct((T, LW), F32)],
        scratch_shapes=[pltpu.VMEM((ts, LW), F32), pltpu.VMEM((8, LW), F32)],
        compiler_params=_cp(("arbitrary", "arbitrary")),
    )(proj, proj, cw, vp, wr, wi)


def _lru_bwd(dyl, proj, h, cw, vp, wr, wi, l, S):
    T = proj.shape[0]
    ts = _tile(S, 256)
    nb = S // ts

    def body(dy_ref, x_ref, xprev_ref, lg_ref, h_ref, hprev_ref, cw_ref, vp_ref, wr_ref, wi_ref,
             dx_ref, dlg_ref, dpr_ref, dpi_ref, ub_ref, wacc_ref, gc_sc, af_sc, dun_sc):
        b = pl.program_id(0)
        j = pl.program_id(1)
        first = j == nb - 1

        @pl.when((b == 0) & (j == 0))
        def _():
            wacc_ref[...] = jnp.zeros_like(wacc_ref)

        @pl.when(j == 0)
        def _():
            gc_sc[...] = jnp.zeros_like(gc_sc)
            af_sc[...] = jnp.ones_like(af_sc)
            dun_sc[...] = jnp.zeros_like(dun_sc)

        row = lax.broadcasted_iota(jnp.int32, (ts, LW), 0)
        keep = jnp.where(first, 0.0, 1.0)
        x = x_ref[...]
        xs, u = _conv_taps(x, xprev_ref[...] * keep, row, cw_ref)
        u = u + vp_ref[0:1, :]
        ub, r, ig, sp, a, mult = _lru_gates(u, vp_ref, wr_ref, wi_ref)
        ub_ref[...] = ub
        hh = h_ref[...]
        h_m1 = jnp.where(row >= 1, pltpu.roll(hh, 1, 0), pltpu.roll(hprev_ref[...] * keep, 1, 0))
        dy = dy_ref[...]
        gl, dgl = _gelu_and_grad(lg_ref[...])
        dlg_ref[...] = (dy * hh * dgl).astype(BF16)
        bv = dy * gl
        av = jnp.where(row < ts - 1, pltpu.roll(a, ts - 1, 0), af_sc[0:1, :])
        d = 1
        while d < ts:
            a_s = jnp.where(row < ts - d, pltpu.roll(av, ts - d, 0), 1.0)
            b_s = jnp.where(row < ts - d, pltpu.roll(bv, ts - d, 0), 0.0)
            bv = av * b_s + bv
            av = av * a_s
            d *= 2
        gt = bv + av * gc_sc[0:1, :]
        gc_sc[...] = gt[0:8, :]
        af_sc[...] = a[0:8, :]
        da = gt * h_m1
        d_ig = gt * mult * u
        d_mult = gt * ig * u
        du = gt * mult * ig
        dlog_a = da * a - d_mult * (a * a) / mult
        dpre_r = (dlog_a * ((-LRU_C) * sp)) * r * (1.0 - r)
        dpre_i = d_ig * ig * (1.0 - ig)
        lam = vp_ref[3:4, :]
        wacc_ref[7:8, :] += _colsum(dlog_a * r) * (LRU_C * _sigmoid(-lam))
        wacc_ref[5:6, :] += _colsum(dpre_r)
        wacc_ref[6:7, :] += _colsum(dpre_i)
        dprb = dpre_r.astype(BF16)
        dpib = dpre_i.astype(BF16)
        dpr_ref[...] = dprb
        dpi_ref[...] = dpib
        du = du + _dot_nt(dprb, wr_ref[...]) + _dot_nt(dpib, wi_ref[...])
        wacc_ref[4:5, :] += _colsum(du)
        dun = dun_sc[...]
        dx = du * cw_ref[3:4, :]
        wacc_ref[3:4, :] += _colsum(du * xs[0])
        for dd in (1, 2, 3):
            du_s = jnp.where(row < ts - dd, pltpu.roll(du, ts - dd, 0), pltpu.roll(dun, ts - dd, 0))
            dx = dx + du_s * cw_ref[3 - dd:4 - dd, :]
            wacc_ref[3 - dd:4 - dd, :] += _colsum(du * xs[dd])
        dun_sc[...] = du
        dx_ref[...] = dx.astype(BF16)

    def tb(b, j):
        return b * nb + (nb - 1 - j)

    def tbp(b, j):
        return b * nb + jnp.maximum(nb - 2 - j, 0)

    return pl.pallas_call(
        body, name=f"lru_bwd_{l}",
        grid=(T // S, nb),
        in_specs=[pl.BlockSpec((ts, LW), lambda b, j: (tb(b, j), 0)),
                  pl.BlockSpec((ts, LW), lambda b, j: (tb(b, j), 0)),
                  pl.BlockSpec((ts, LW), lambda b, j: (tbp(b, j), 0)),
                  pl.BlockSpec((ts, LW), lambda b, j: (tb(b, j), 1)),
                  pl.BlockSpec((ts, LW), lambda b, j: (tb(b, j), 0)),
                  pl.BlockSpec((ts, LW), lambda b, j: (tbp(b, j), 0)),
                  pl.BlockSpec((None, 8, LW), lambda b, j: (l, 0, 0)),
                  pl.BlockSpec((None, 8, LW), lambda b, j: (l, 0, 0)),
                  pl.BlockSpec((None, LW, LW), lambda b, j: (l, 0, 0)),
                  pl.BlockSpec((None, LW, LW), lambda b, j: (l, 0, 0))],
        out_specs=[pl.BlockSpec((ts, LW), lambda b, j: (tb(b, j), 0)),
                   pl.BlockSpec((ts, LW), lambda b, j: (tb(b, j), 0)),
                   pl.BlockSpec((ts, LW), lambda b, j: (tb(b, j), 0)),
                   pl.BlockSpec((ts, LW), lambda b, j: (tb(b, j), 0)),
                   pl.BlockSpec((ts, LW), lambda b, j: (tb(b, j), 0)),
                   pl.BlockSpec((8, LW), lambda b, j: (0, 0))],
        out_shape=[jax.ShapeDtypeStruct((T, LW), BF16),
                   jax.ShapeDtypeStruct((T, LW), BF16),
                   jax.ShapeDtypeStruct((T, LW), BF16),
                   jax.ShapeDtypeStruct((T, LW), BF16),
                   jax.ShapeDtypeStruct((T, LW), BF16),
                   jax.ShapeDtypeStruct((8, LW), F32)],
        scratch_shapes=[pltpu.VMEM((8, LW), F32), pltpu.VMEM((8, LW), F32), pltpu.VMEM((ts, LW), F32)],
        compiler_params=_cp(("arbitrary", "arbitrary")),
    )(dyl, proj, proj, proj, h, h, cw, vp, wr, wi)


def _fgate_fwd(proj, bfp, l, S):
    T = proj.shape[0]

    def body(x_ref, b_ref, o_ref):
        z = x_ref[...] + b_ref[0:1, :]
        v = jnp.minimum(z, 0.0) - _log1p(jnp.exp(-jnp.abs(z)))
        row = lax.broadcasted_iota(jnp.int32, (S, 128), 0)
        d = 1
        while d < S:
            v = v + jnp.where(row >= d, pltpu.roll(v, d, 0), 0.0)
            d *= 2
        o_ref[...] = v

    return pl.pallas_call(
        body, name=f"fgate_fwd_{l}",
        grid=(T // S,),
        in_specs=[pl.BlockSpec((S, 128), lambda b: (b, F_BLK)),
                  pl.BlockSpec((None, 8, 128), lambda b: (l, 0, 0))],
        out_specs=pl.BlockSpec((S, 128), lambda b: (b, 0)),
        out_shape=jax.ShapeDtypeStruct((T, 128), F32),
        compiler_params=_cp(("arbitrary",)),
    )(proj, bfp)


def _fgate_bwd(dcum, proj, bfp, l, S):
    T = proj.shape[0]

    def body(d_ref, x_ref, b_ref, o_ref, wacc_ref):
        @pl.when(pl.program_id(0) == 0)
        def _():
            wacc_ref[...] = jnp.zeros_like(wacc_ref)

        v = d_ref[...]
        row = lax.broadcasted_iota(jnp.int32, (S, 128), 0)
        d = 1
        while d < S:
            v = v + jnp.where(row < S - d, pltpu.roll(v, S - d, 0), 0.0)
            d *= 2
        z = x_ref[...] + b_ref[0:1, :]
        dz = v * _sigmoid(-z)
        o_ref[...] = dz.astype(BF16)
        wacc_ref[0:1, :] += _colsum(dz)

    return pl.pallas_call(
        body, name=f"fgate_bwd_{l}",
        grid=(T // S,),
        in_specs=[pl.BlockSpec((S, 128), lambda b: (b, 0)),
                  pl.BlockSpec((S, 128), lambda b: (b, F_BLK)),
                  pl.BlockSpec((None, 8, 128), lambda b: (l, 0, 0))],
        out_specs=[pl.BlockSpec((S, 128), lambda b: (b, 0)), pl.BlockSpec((8, 128), lambda b: (0, 0))],
        out_shape=[jax.ShapeDtypeStruct((T, 128), BF16), jax.ShapeDtypeStruct((8, 128), F32)],
        compiler_params=_cp(("arbitrary",)),
    )(dcum, proj, bfp)


def _logsig_parts(z):
    e = jnp.exp(-jnp.abs(z))
    l1p = jnp.log(1.0 + e)
    return e, jnp.minimum(z, 0.0) - l1p, -jnp.maximum(z, 0.0) - l1p


def _sb_fwd(q, k, v, l):
    B, H, nq, tq, _ = q.shape
    nk, tk = k.shape[2], k.shape[3]
    rr = tq // tk

    def body(q_ref, k_ref, v_ref, o_ref, t1_ref):
        tri = _tri(tk, "row_gt_col")
        ti = lax.broadcasted_iota(jnp.int32, (tq, 1), 0)
        si = lax.broadcasted_iota(jnp.int32, (1, tk), 1)

        def qloop(qb, carry):
            qq = q_ref[qb]
            tpos = qb * tq + ti
            nkb = (qb + 1) * rr

            def kloop(i, c):
                acc, run = c
                kb = nkb - 1 - i
                z = _dot_nt(qq, k_ref[kb]) * SCALE
                past = (kb * tk + si) < tpos
                _, lb, l1 = _logsig_parts(z)
                l1m = jnp.where(past, l1, 0.0)
                aft = _cumsum_mm(l1m, tri) + run
                w = jnp.where(past, jnp.exp(lb + aft), 0.0)
                acc = acc + _dot(w.astype(BF16), v_ref[kb])
                return acc, run + _rowsum(l1m)

            acc, run = lax.fori_loop(0, nkb, kloop, (jnp.zeros((tq, HD), F32), jnp.zeros((tq, 1), F32)))
            o_ref[qb] = acc
            t1_ref[qb] = run
            return carry

        lax.fori_loop(0, nq, qloop, 0)

    qs = pl.BlockSpec((None, None, nq, tq, HD), lambda b, h: (b, h, 0, 0, 0))
    ks = pl.BlockSpec((None, None, nk, tk, HD), lambda b, h: (b, h, 0, 0, 0))
    return pl.pallas_call(
        body, name=f"sb_fwd_{l}",
        grid=(B, H),
        in_specs=[qs, ks, ks],
        out_specs=[qs, pl.BlockSpec((None, None, nq, tq, 1), lambda b, h: (b, h, 0, 0, 0))],
        out_shape=[jax.ShapeDtypeStruct((B, H, nq, tq, HD), F32),
                   jax.ShapeDtypeStruct((B, H, nq, tq, 1), F32)],
        compiler_params=_cp(("arbitrary", "arbitrary"), VMEM_BIG),
    )(q, k, v)


def _sb_bwd(q, k, v, do, t1, l):
    B, H, nq, tq, _ = q.shape
    nk, tk = k.shape[2], k.shape[3]
    rr = tq // tk

    def body(q_ref, k_ref, v_ref, do_ref, t1_ref, dq_ref, dk_ref, dv_ref, dk_sc, dv_sc):
        dk_sc[...] = jnp.zeros_like(dk_sc)
        dv_sc[...] = jnp.zeros_like(dv_sc)
        tri_in = _tri(tk, "row_le_col")
        tri_ex = _tri(tk, "row_lt_col")
        ti = lax.broadcasted_iota(jnp.int32, (tq, 1), 0)
        si = lax.broadcasted_iota(jnp.int32, (1, tk), 1)

        def qloop(qb, carry):
            qq = q_ref[qb]
            dob = do_ref[qb].astype(BF16)
            tot = t1_ref[qb]
            tpos = qb * tq + ti
            nkb = (qb + 1) * rr

            def kloop(kb, c):
                dq, run1, rung = c
                kk = k_ref[kb]
                vv = v_ref[kb]
                z = _dot_nt(qq, kk) * SCALE
                past = (kb * tk + si) < tpos
                e, lb, l1 = _logsig_parts(z)
                l1m = jnp.where(past, l1, 0.0)
                aft = tot - (run1 + _cumsum_mm(l1m, tri_in))
                w = jnp.where(past, jnp.exp(lb + aft), 0.0)
                gm = w * _dot_nt(dob, vv)
                cpre = rung + _cumsum_mm(gm, tri_ex, parts=2)
                inv = 1.0 / (1.0 + e)
                sig = jnp.where(z >= 0.0, inv, e * inv)
                dz = jnp.where(past, gm * (1.0 - sig) - cpre * sig, 0.0).astype(BF16)
                dv_sc[kb] += _dot_tn(w.astype(BF16), dob)
                dk_sc[kb] += _dot_tn(dz, qq) * SCALE
                dq = dq + _dot(dz, kk) * SCALE
                return dq, run1 + _rowsum(l1m), rung + _rowsum(gm)

            z1 = jnp.zeros((tq, 1), F32)
            dq, _, _ = lax.fori_loop(0, nkb, kloop, (jnp.zeros((tq, HD), F32), z1, z1))
            dq_ref[qb] = dq.astype(BF16)
            return carry

        lax.fori_loop(0, nq, qloop, 0)
        dk_ref[...] = dk_sc[...].astype(BF16)
        dv_ref[...] = dv_sc[...].astype(BF16)

    qs = pl.BlockSpec((None, None, nq, tq, HD), lambda b, h: (b, h, 0, 0, 0))
    ks = pl.BlockSpec((None, None, nk, tk, HD), lambda b, h: (b, h, 0, 0, 0))
    return pl.pallas_call(
        body, name=f"sb_bwd_{l}",
        grid=(B, H),
        in_specs=[qs, ks, ks, qs, pl.BlockSpec((None, None, nq, tq, 1), lambda b, h: (b, h, 0, 0, 0))],
        out_specs=[qs, ks, ks],
        out_shape=[jax.ShapeDtypeStruct((B, H, nq, tq, HD), BF16),
                   jax.ShapeDtypeStruct((B, H, nk, tk, HD), BF16),
                   jax.ShapeDtypeStruct((B, H, nk, tk, HD), BF16)],
        scratch_shapes=[pltpu.VMEM((nk, tk, HD), F32), pltpu.VMEM((nk, tk, HD), F32)],
        compiler_params=_cp(("arbitrary", "arbitrary"), VMEM_BIG),
    )(q, k, v, do, t1)


def _fox_fwd(q, k, v, cq, ck, gqk, l):
    B, H, nq, tq, _ = q.shape
    nk, tk = k.shape[2], k.shape[3]
    rr = tq // tk

    def body(q_ref, k_ref, v_ref, cq_ref, ck_ref, g_ref, o_ref, lse_ref, fk_sc):
        g0 = g_ref[0:1, :]
        g1 = g_ref[1:2, :]

        def kprep(kb, c):
            kn, _ = _rms_rows(k_ref[kb])
            fk_sc[kb] = (kn * g1).astype(BF16)
            return c

        lax.fori_loop(0, nk, kprep, 0)
        ti = lax.broadcasted_iota(jnp.int32, (tq, 1), 0)
        si = lax.broadcasted_iota(jnp.int32, (1, tk), 1)

        def qloop(qb, carry):
            qn, _ = _rms_rows(q_ref[qb])
            fq = (qn * g0).astype(BF16)
            cqq = cq_ref[qb]
            tpos = qb * tq + ti

            def kloop(kb, c):
                m, lsum, acc = c
                s = _dot_nt(fq, fk_sc[kb]) * SCALE + cqq - ck_ref[kb]
                s = jnp.where((kb * tk + si) <= tpos, s, NEG)
                m2 = jnp.maximum(m, jnp.max(s, axis=1, keepdims=True))
                al = jnp.exp(m - m2)
                p = jnp.exp(s - m2)
                return m2, al * lsum + _rowsum(p), al * acc + _dot(p.astype(BF16), v_ref[kb])

            m, lsum, acc = lax.fori_loop(
                0, (qb + 1) * rr, kloop,
                (jnp.full((tq, 1), NEG, F32), jnp.zeros((tq, 1), F32), jnp.zeros((tq, HD), F32)))
            o_ref[qb] = acc / lsum
            lse_ref[qb] = m + jnp.log(lsum)
            return carry

        lax.fori_loop(0, nq, qloop, 0)

    qs = pl.BlockSpec((None, None, nq, tq, HD), lambda b, h: (b, h, 0, 0, 0))
    ks = pl.BlockSpec((None, None, nk, tk, HD), lambda b, h: (b, h, 0, 0, 0))
    cqs = pl.BlockSpec((None, None, nq, tq, 1), lambda b, h: (b, h, 0, 0, 0))
    cks = pl.BlockSpec((None, None, nk, 1, tk), lambda b, h: (b, h, 0, 0, 0))
    return pl.pallas_call(
        body, name=f"fox_fwd_{l}",
        grid=(B, H),
        in_specs=[qs, ks, ks, cqs, cks, pl.BlockSpec((None, 8, HD), lambda b, h: (l, 0, 0))],
        out_specs=[qs, cqs],
        out_shape=[jax.ShapeDtypeStruct((B, H, nq, tq, HD), F32),
                   jax.ShapeDtypeStruct((B, H, nq, tq, 1), F32)],
        scratch_shapes=[pltpu.VMEM((nk, tk, HD), BF16)],
        compiler_params=_cp(("arbitrary", "arbitrary"), VMEM_BIG),
    )(q, k, v, cq, ck, gqk)


def _fox_bwd(q, k, v, cq, ck, gqk, do, lse, l):
    B, H, nq, tq, _ = q.shape
    nk, tk = k.shape[2], k.shape[3]
    rr = tq // tk

    def body(q_ref, k_ref, v_ref, cq_ref, ck_ref, g_ref, do_ref, lse_ref,
             dq_ref, dk_ref, dv_ref, dc_ref, wacc_ref, fk_sc, dfk_sc, dv_sc):
        @pl.when((pl.program_id(0) == 0) & (pl.program_id(1) == 0))
        def _():
            wacc_ref[...] = jnp.zeros_like(wacc_ref)

        g0 = g_ref[0:1, :]
        g1 = g_ref[1:2, :]
        dfk_sc[...] = jnp.zeros_like(dfk_sc)
        dv_sc[...] = jnp.zeros_like(dv_sc)
        dc_ref[...] = jnp.zeros_like(dc_ref)

        def kprep(kb, c):
            kn, _ = _rms_rows(k_ref[kb])
            fk_sc[kb] = (kn * g1).astype(BF16)
            return c

        lax.fori_loop(0, nk, kprep, 0)
        ti = lax.broadcasted_iota(jnp.int32, (tq, 1), 0)
        si = lax.broadcasted_iota(jnp.int32, (1, tk), 1)

        def qloop(qb, carry):
            qn, qr = _rms_rows(q_ref[qb])
            fq = (qn * g0).astype(BF16)
            cqq = cq_ref[qb]
            lse = lse_ref[qb]
            dob = do_ref[qb].astype(BF16)
            tpos = qb * tq + ti

            def probs(kb):
                s = _dot_nt(fq, fk_sc[kb]) * SCALE + cqq - ck_ref[kb]
                p = jnp.where((kb * tk + si) <= tpos, jnp.exp(s - lse), 0.0)
                return p, _dot_nt(dob, v_ref[kb])

            def dloop(kb, acc):
                p, dp = probs(kb)
                return acc + _rowsum(p * dp)

            dlt = lax.fori_loop(0, (qb + 1) * rr, dloop, jnp.zeros((tq, 1), F32))

            def kloop(kb, dfq):
                fk = fk_sc[kb]
                p, dp = probs(kb)
                ds = p * (dp - dlt)
                dsb = ds.astype(BF16)
                dv_sc[kb] += _dot_tn(p.astype(BF16), dob)
                dfk_sc[kb] += _dot_tn(dsb, fq) * SCALE
                dc_ref[kb] += jnp.broadcast_to(-_colsum(ds), (8, tk))
                return dfq + _dot(dsb, fk) * SCALE

            dfq = lax.fori_loop(0, (qb + 1) * rr, kloop, jnp.zeros((tq, HD), F32))
            wacc_ref[0:1, :] += _colsum(dfq * qn)
            dq_ref[qb] = _rms_bwd(qn, qr, dfq * g0).astype(BF16)
            return carry

        lax.fori_loop(0, nq, qloop, 0)

        def kfin(kb, c):
            kn, kr = _rms_rows(k_ref[kb])
            dfk = dfk_sc[kb]
            wacc_ref[1:2, :] += _colsum(dfk * kn)
            dk_ref[kb] = _rms_bwd(kn, kr, dfk * g1).astype(BF16)
            return c

        lax.fori_loop(0, nk, kfin, 0)
        dv_ref[...] = dv_sc[...].astype(BF16)

    qs = pl.BlockSpec((None, None, nq, tq, HD), lambda b, h: (b, h, 0, 0, 0))
    ks = pl.BlockSpec((None, None, nk, tk, HD), lambda b, h: (b, h, 0, 0, 0))
    cqs = pl.BlockSpec((None, None, nq, tq, 1), lambda b, h: (b, h, 0, 0, 0))
    cks = pl.BlockSpec((None, None, nk, 1, tk), lambda b, h: (b, h, 0, 0, 0))
    return pl.pallas_call(
        body, name=f"fox_bwd_{l}",
        grid=(B, H),
        in_specs=[qs, ks, ks, cqs, cks, pl.BlockSpec((None, 8, HD), lambda b, h: (l, 0, 0)), qs, cqs],
        out_specs=[qs, ks, ks,
                   pl.BlockSpec((None, None, nk, 8, tk), lambda b, h: (b, h, 0, 0, 0)),
                   pl.BlockSpec((8, HD), lambda b, h: (0, 0))],
        out_shape=[jax.ShapeDtypeStruct((B, H, nq, tq, HD), BF16),
                   jax.ShapeDtypeStruct((B, H, nk, tk, HD), BF16),
                   jax.ShapeDtypeStruct((B, H, nk, tk, HD), BF16),
                   jax.ShapeDtypeStruct((B, H, nk, 8, tk), F32),
                   jax.ShapeDtypeStruct((8, HD), F32)],
        scratch_shapes=[pltpu.VMEM((nk, tk, HD), BF16), pltpu.VMEM((nk, tk, HD), F32),
                        pltpu.VMEM((nk, tk, HD), F32)],
        compiler_params=_cp(("arbitrary", "arbitrary"), VMEM_BIG),
    )(q, k, v, cq, ck, gqk, do, lse)


def _ada_fwd(c_all, w_ada, b_cols):
    nb, ncol = c_all.shape[0], w_ada.shape[2]
    tn = _tile(ncol, 768)

    def body(c_ref, w_ref, b_ref, o_ref):
        c = c_ref[...]
        ca = (c * _sigmoid(c)).astype(BF16)
        o_ref[...] = _dot(ca, w_ref[...].astype(BF16)) + b_ref[...]

    return pl.pallas_call(
        body, name="ada_fwd",
        grid=(2, ncol // tn),
        in_specs=[pl.BlockSpec((nb, D), lambda l, n: (0, 0)),
                  pl.BlockSpec((None, D, tn), lambda l, n: (l, 0, n)),
                  pl.BlockSpec((None, 1, tn), lambda l, n: (l, 0, n))],
        out_specs=pl.BlockSpec((None, nb, tn), lambda l, n: (l, 0, n)),
        out_shape=jax.ShapeDtypeStruct((2, nb, ncol), F32),
        compiler_params=_cp(("arbitrary", "arbitrary")),
    )(c_all, w_ada, b_cols)


def _ada_bwd(c_all, dmod_cols):
    nb, ncol = c_all.shape[0], dmod_cols.shape[2]
    tn = _tile(ncol, 768)

    def body(c_ref, d_ref, o_ref):
        c = c_ref[...]
        ca = (c * _sigmoid(c)).astype(BF16)
        o_ref[...] = _dot_tn(ca, d_ref[...].astype(BF16))

    return pl.pallas_call(
        body, name="ada_bwd",
        grid=(2, ncol // tn),
        in_specs=[pl.BlockSpec((nb, D), lambda l, n: (0, 0)),
                  pl.BlockSpec((None, nb, tn), lambda l, n: (l, 0, n))],
        out_specs=pl.BlockSpec((None, D, tn), lambda l, n: (l, 0, n)),
        out_shape=jax.ShapeDtypeStruct((2, D, ncol), F32),
        compiler_params=_cp(("arbitrary", "arbitrary")),
    )(c_all, dmod_cols)


def _sum_lead(a, name):
    n, R, C = a.shape
    tr = _tile_div8(R, 256)

    def body(a_ref, o_ref):
        acc = a_ref[0]
        for i in range(1, n):
            acc = acc + a_ref[i]
        o_ref[...] = acc

    return pl.pallas_call(
        body, name=name,
        grid=(R // tr,),
        in_specs=[pl.BlockSpec((n, tr, C), lambda i: (0, i, 0))],
        out_specs=pl.BlockSpec((tr, C), lambda i: (i, 0)),
        out_shape=jax.ShapeDtypeStruct((R, C), F32),
        compiler_params=_cp(("arbitrary",)),
    )(a)


def _adamw(w, g, m, v, name):
    R, C = w.shape
    tr = _tile_div8(R, max(8, (1 << 18) // C))
    c1 = 1.0 / (1.0 - ADAM_B1 ** ADAM_STEP)
    c2 = 1.0 / (1.0 - ADAM_B2 ** ADAM_STEP)

    def body(w_ref, g_ref, m_ref, v_ref, d_ref, mo_ref, vo_ref):
        gg = g_ref[...]
        mn = ADAM_B1 * m_ref[...] + (1.0 - ADAM_B1) * gg
        vn = ADAM_B2 * v_ref[...] + (1.0 - ADAM_B2) * (gg * gg)
        mo_ref[...] = mn
        vo_ref[...] = vn
        d_ref[...] = (-ADAM_LR) * ((mn * c1) / (jnp.sqrt(vn * c2) + ADAM_EPS) + ADAM_WD * w_ref[...])

    spec = pl.BlockSpec((tr, C), lambda i: (i, 0))
    sh = jax.ShapeDtypeStruct((R, C), F32)
    return pl.pallas_call(
        body, name=name, grid=(R // tr,),
        in_specs=[spec] * 4, out_specs=[spec] * 3, out_shape=[sh] * 3,
        compiler_params=_cp(("arbitrary",)),
    )(w, g, m, v)


def _coords():
    return lax.axis_index("x"), lax.axis_index("y"), lax.axis_index("c")


def _all_gather8(blk, name, vmem):
    m_per, n = blk.shape
    space = pltpu.VMEM if vmem else pl.ANY

    def body(x_ref, out_ref, send_sems, recv_sems, local_sem):
        x, y, c = _coords()
        me, sibling = (x, y, c), (x, y, 1 - c)
        chips = [(1 - x, y), (x, 1 - y), (1 - x, 1 - y)]

        def rows(px, py, pc):
            return out_ref.at[4 * px + 2 * py + pc]

        def copy(k, block, to, src=None):
            return pltpu.make_async_remote_copy(
                src_ref=rows(*block) if src is None else src, dst_ref=rows(*block),
                send_sem=send_sems.at[k], recv_sem=recv_sems.at[k], device_id=to, device_id_type=MESH)

        mine = pltpu.make_async_copy(x_ref, rows(*me), local_sem)
        mine.start()
        first = [copy(0, me, sibling, src=x_ref)]
        first += [copy(1 + j, me, (*chip, c), src=x_ref) for j, chip in enumerate(chips)]
        for cp in first:
            cp.start()
        passed = [copy(4 + j, (*chip, c), sibling) for j, chip in enumerate(chips)]
        for j, chip in enumerate(chips):
            copy(1 + j, (*chip, c), me).wait_recv()
            passed[j].start()
        copy(0, sibling, me).wait_recv()
        for j, chip in enumerate(chips):
            copy(4 + j, (*chip, 1 - c), me).wait_recv()
        for cp in first + passed:
            cp.wait_send()
        mine.wait()

    return pl.pallas_call(
        body, name=name,
        out_shape=jax.ShapeDtypeStruct((N_DEV, m_per, n), blk.dtype),
        in_specs=[pl.BlockSpec(memory_space=space)],
        out_specs=pl.BlockSpec(memory_space=space),
        scratch_shapes=[pltpu.SemaphoreType.DMA((7,)), pltpu.SemaphoreType.DMA((7,)), pltpu.SemaphoreType.DMA],
        compiler_params=pltpu.CompilerParams(vmem_limit_bytes=VMEM_BIG if vmem else None),
    )(blk)


def _rs_to_sibling(g):
    ns, _, P, _ = g.shape

    def body(g_ref, r_ref, send_sems, recv_sems):
        x, y, c = _coords()
        cps = [pltpu.make_async_remote_copy(
            src_ref=g_ref.at[s, 1 - c], dst_ref=r_ref.at[s], send_sem=send_sems.at[s], recv_sem=recv_sems.at[s],
            device_id=(x, y, 1 - c), device_id_type=MESH) for s in range(ns)]
        for cp in cps:
            cp.start()
        for cp in cps:
            cp.wait()

    return pl.pallas_call(
        body, name="rs_to_sibling",
        out_shape=jax.ShapeDtypeStruct((ns, P, ROW), g.dtype),
        in_specs=[pl.BlockSpec(memory_space=pl.ANY)],
        out_specs=pl.BlockSpec(memory_space=pl.ANY),
        scratch_shapes=[pltpu.SemaphoreType.DMA((ns,)), pltpu.SemaphoreType.DMA((ns,))],
    )(g)


def _rs_to_chips(hsum):
    _, P, _ = hsum.shape

    def body(h_ref, r_ref, send_sems, recv_sems):
        x, y, c = _coords()
        chips = [(1 - x, y), (x, 1 - y), (1 - x, 1 - y)]
        cps = [pltpu.make_async_remote_copy(
            src_ref=h_ref.at[2 * px + py], dst_ref=r_ref.at[k], send_sem=send_sems.at[k], recv_sem=recv_sems.at[k],
            device_id=(px, py, c), device_id_type=MESH) for k, (px, py) in enumerate(chips)]
        for cp in cps:
            cp.start()
        for cp in cps:
            cp.wait()

    return pl.pallas_call(
        body, name="rs_to_chips",
        out_shape=jax.ShapeDtypeStruct((3, P, ROW), hsum.dtype),
        in_specs=[pl.BlockSpec(memory_space=pl.ANY)],
        out_specs=pl.BlockSpec(memory_space=pl.ANY),
        scratch_shapes=[pltpu.SemaphoreType.DMA((3,)), pltpu.SemaphoreType.DMA((3,))],
    )(hsum)


def _share_halves(fin):
    P = fin.shape[0]

    def body(f_ref, o_ref, send_sem, recv_sem, local_sem):
        x, y, c = _coords()
        mine = pltpu.make_async_copy(f_ref, o_ref.at[c], local_sem)
        mine.start()
        out = pltpu.make_async_remote_copy(
            src_ref=f_ref, dst_ref=o_ref.at[c], send_sem=send_sem, recv_sem=recv_sem,
            device_id=(x, y, 1 - c), device_id_type=MESH)
        out.start()
        pltpu.make_async_remote_copy(
            src_ref=f_ref, dst_ref=o_ref.at[1 - c], send_sem=send_sem, recv_sem=recv_sem,
            device_id=(x, y, 1 - c), device_id_type=MESH).wait_recv()
        out.wait_send()
        mine.wait()

    return pl.pallas_call(
        body, name="share_halves",
        out_shape=jax.ShapeDtypeStruct((2, P, ROW), fin.dtype),
        in_specs=[pl.BlockSpec(memory_space=pl.ANY)],
        out_specs=pl.BlockSpec(memory_space=pl.ANY),
        scratch_shapes=[pltpu.SemaphoreType.DMA, pltpu.SemaphoreType.DMA, pltpu.SemaphoreType.DMA],
    )(fin)


def _add_sibling(g, r, cidx):
    ns, _, P, _ = g.shape
    tr = _tile_div8(P, 768)

    def body(c_ref, g_ref, r_ref, o_ref):
        o_ref[...] = g_ref[...] + r_ref[...]

    return pl.pallas_call(
        body, name="rs_add_sibling",
        grid_spec=pltpu.PrefetchScalarGridSpec(
            num_scalar_prefetch=1, grid=(ns, P // tr),
            in_specs=[pl.BlockSpec((None, None, tr, ROW), lambda s, i, c_ref: (s, c_ref[0], i, 0)),
                      pl.BlockSpec((None, tr, ROW), lambda s, i, c_ref: (s, i, 0))],
            out_specs=pl.BlockSpec((None, tr, ROW), lambda s, i, c_ref: (s, i, 0))),
        out_shape=jax.ShapeDtypeStruct((ns, P, ROW), F32),
        compiler_params=_cp(("arbitrary", "arbitrary")),
    )(cidx, g, r)


def _add_chips(hsum, r3, sidx):
    _, P, _ = hsum.shape
    tr = _tile_div8(P, 768)

    def body(s_ref, h_ref, r_ref, o_ref):
        o_ref[...] = ((h_ref[...] + r_ref[0]) + r_ref[1]) + r_ref[2]

    return pl.pallas_call(
        body, name="rs_add_chips",
        grid_spec=pltpu.PrefetchScalarGridSpec(
            num_scalar_prefetch=1, grid=(P // tr,),
            in_specs=[pl.BlockSpec((None, tr, ROW), lambda i, s_ref: (s_ref[0], i, 0)),
                      pl.BlockSpec((3, tr, ROW), lambda i, s_ref: (0, i, 0))],
            out_specs=pl.BlockSpec((tr, ROW), lambda i, s_ref: (i, 0))),
        out_shape=jax.ShapeDtypeStruct((P, ROW), F32),
        compiler_params=_cp(("arbitrary",)),
    )(sidx, hsum, r3)


def _pack_rows(parts, rows, dtype):
    flat = jnp.concatenate([p.reshape(-1).astype(dtype) for p in parts])
    return jnp.pad(flat, (0, rows * ROW - flat.shape[0])).reshape(rows, ROW)


def _unpack(flat, shapes):
    out, off = [], 0
    for sh in shapes:
        n = math.prod(sh)
        out.append(flat[off:off + n].reshape(sh))
        off += n
    return out


def _heads(t, B, S, blk):
    return t.reshape(B, S, NH, HD).transpose(0, 2, 1, 3).reshape(B, NH, S // blk, blk, HD)


def _unheads(t, B, S):
    return t.reshape(B, NH, S, HD).transpose(0, 2, 1, 3).reshape(B * S, AW)


def _block_diag(w):
    eye = jnp.eye(LW // HD, dtype=w.dtype)
    return jnp.einsum("lhij,hg->lhigj", w, eye).reshape(w.shape[0], LW, LW)


def _diag_blocks(w):
    nbk = LW // HD
    w4 = w.reshape(nbk, HD, nbk, HD)
    return jnp.stack([w4[h, :, h, :] for h in range(nbk)])


def _rows8(rows, width):
    z = jnp.zeros((width,), F32)
    return jnp.stack(list(rows) + [z] * (8 - len(rows)))


def kernel(x, c, w_ada, b_ada, g_norm, w_ffn_up, w_ffn_down, w_in, b_fgate, conv_w, conv_b, w_rgate, b_rgate, w_igate, b_igate, lru_lambda, g_qk, g_mix_out, w_out, loss_target, m_w_ada, m_b_ada, m_g_norm, m_w_ffn_up, m_w_ffn_down, m_w_in, m_b_fgate, m_conv_w, m_conv_b, m_w_rgate, m_b_rgate, m_w_igate, m_b_igate, m_lru_lambda, m_g_qk, m_g_mix_out, m_w_out, v_w_ada, v_b_ada, v_g_norm, v_w_ffn_up, v_w_ffn_down, v_w_in, v_b_fgate, v_conv_w, v_conv_b, v_w_rgate, v_b_rgate, v_w_igate, v_b_igate, v_lru_lambda, v_g_qk, v_g_mix_out, v_w_out):
    B, S, _ = x.shape
    T = B * S
    xi, yi, ci = _coords()
    sidx = 2 * xi + yi
    didx = 4 * xi + 2 * yi + ci
    ada_cols = w_ada.shape[2]
    gn_cols = g_norm.shape[2]
    cw_cols = conv_w.shape[2]
    n_all = B * N_DEV

    blk1 = _pack_rows([c, jnp.pad(g_norm.reshape(-1), (0, 2 * ROW - g_norm.size)), conv_w], 8, F32)
    ag1 = _all_gather8(blk1, "ag_small_in", True)
    c_all = ag1[:, 0:B].reshape(n_all, D)
    chip_rows = ag1[0::2]
    g_norm_full = chip_rows[:, 2:4].reshape(N_SHARD, 2 * ROW)[:, :g_norm.size] \
        .reshape(N_SHARD, 2, 3, gn_cols).transpose(1, 2, 0, 3).reshape(2, 3, D)
    conv_w_full = chip_rows[:, 4].reshape(N_SHARD, 2, 4, cw_cols).transpose(1, 2, 0, 3).reshape(2, 4, LW)

    b_cols = lax.dynamic_slice(b_ada, (0, sidx * ada_cols), (2, ada_cols)).reshape(2, 1, ada_cols)
    mod_cols = _ada_fwd(c_all, w_ada, b_cols)
    mrows = (2 * n_all * ada_cols) // ROW
    ag2 = _all_gather8(mod_cols.reshape(mrows, ROW), "ag_mod", True)
    mod_sh = ag2[0::2].reshape(N_SHARD, 2, n_all, ada_cols)
    mod_me = lax.dynamic_slice(mod_sh, (0, 0, didx * B, 0), (N_SHARD, 2, B, ada_cols))
    mod_me = mod_me.transpose(1, 2, 0, 3).reshape(2, B, 3, 3, D)
    zrow = jnp.zeros((B, D), F32)
    mods = [[jnp.stack([mod_me[l, :, j, 0], 1.0 + mod_me[l, :, j, 1], 1.0 + mod_me[l, :, j, 2],
                        jnp.broadcast_to(g_norm_full[l, j], (B, D)), zrow, zrow, zrow, zrow], axis=1)
             for j in range(3)] for l in range(2)]

    wparts = [w_ffn_up, w_ffn_down, w_in, w_out]
    wshapes = [w.shape for w in wparts]
    n_flat = sum(w.size for w in wparts)
    half_rows = (-(-n_flat // ROW) + 1) // 2
    P = -(-half_rows // 64) * 64
    wpack = _pack_rows(wparts, 2 * P, BF16)
    whalf = lax.dynamic_slice(wpack, (ci * P, 0), (P, ROW))
    wg = _all_gather8(whalf, "ag_weights", False).reshape(N_SHARD, 2 * P * ROW)
    g_up, g_dn, g_in, g_out = _unpack_shards(wg, wshapes)
    wup5 = g_up
    dff = g_dn.shape[3] * N_SHARD
    wdn4 = g_dn.transpose(1, 2, 0, 3, 4).reshape(2, 2, dff, D)
    win_full = g_in.transpose(1, 2, 0, 3).reshape(2, D, N_IN)
    winp = jnp.pad(win_full, ((0, 0), (0, 0), (0, N_INP - N_IN)))
    wout = g_out.transpose(1, 0, 2, 3).reshape(2, D, D)

    wr_d = _block_diag(w_rgate).astype(BF16)
    wi_d = _block_diag(w_igate).astype(BF16)
    cw8 = jnp.pad(conv_w_full, ((0, 0), (0, 4), (0, 0)))
    vp8 = jnp.stack([_rows8([conv_b[l], b_rgate[l], b_igate[l], lru_lambda[l]], LW) for l in range(2)])
    bfp = jnp.pad(b_fgate, ((0, 0), (0, 128 - NH)))[:, None, :] * jnp.ones((1, 8, 1), F32)
    gqk8 = jnp.pad(g_qk, ((0, 0), (0, 6), (0, 0)))
    gmix8 = jnp.pad(g_mix_out[:, None, :], ((0, 0), (0, 7), (0, 0)))

    x2 = x.reshape(T, D)
    tgt = loss_target.reshape(T, D)

    saved = []
    xc = x2
    for l in range(2):
        sv = {}
        sv["x0"] = xc
        xc, sv["g0"], sv["u0"], sv["f0"] = _ffn_fwd(xc, mods[l][0], wup5, wdn4, l, 0, S)
        sv["x1"] = xc
        sv["h1"], proj = _mix_in_fwd(xc, mods[l][1], winp, l, S)
        sv["proj"] = proj
        sv["ylru"], sv["hl"] = _lru_fwd(proj, cw8, vp8, wr_d, wi_d, l, S)
        col = LW * 2
        sbq = _heads(proj[:, col:col + AW].astype(BF16), B, S, TQ_(S))
        sbk = _heads(proj[:, col + AW:col + 2 * AW].astype(BF16), B, S, TK_(S))
        sbv = _heads(proj[:, col + 2 * AW:col + 3 * AW].astype(BF16), B, S, TK_(S))
        sv["sb"] = (sbq, sbk, sbv)
        osb, sv["t1"] = _sb_fwd(sbq, sbk, sbv, l)
        col += 3 * AW
        fxq = _heads(proj[:, col:col + AW], B, S, TQ_(S))
        fxk = _heads(proj[:, col + AW:col + 2 * AW], B, S, TK_(S))
        fxv = _heads(proj[:, col + 2 * AW:col + 3 * AW].astype(BF16), B, S, TK_(S))
        cum = _fgate_fwd(proj, bfp, l, S)[:, :NH].reshape(B, S, NH).transpose(0, 2, 1)
        cq = cum.reshape(B, NH, S // TQ_(S), TQ_(S), 1)
        ck = cum.reshape(B, NH, S // TK_(S), 1, TK_(S))
        sv["fx"] = (fxq, fxk, fxv, cq, ck)
        ofx, sv["lse"] = _fox_fwd(fxq, fxk, fxv, cq, ck, gqk8, l)
        sv["ofx_h"] = ofx
        sv["osb"] = _unheads(osb, B, S)
        sv["ofx"] = _unheads(ofx, B, S)
        xc, sv["y"], sv["mo"] = _mix_out_fwd(xc, sv["ylru"], sv["osb"], sv["ofx"], mods[l][1], gmix8, wout, l, S)
        sv["x2"] = xc
        xc, sv["g2"], sv["u2"], sv["f2"] = _ffn_fwd(xc, mods[l][2], wup5, wdn4, l, 1, S)
        saved.append(sv)

    dxc, lpart = _loss_head(xc, tgt, S)
    loss = lax.psum(lpart[0, 0], ("x", "y", "c"))

    tf = wup5.shape[-1]
    g_up_l = [[None, None], [None, None]]
    g_dn_l = [[None, None], [None, None]]
    g_in_l, g_out_l = [None, None], [None, None]
    dmods = [[None] * 3 for _ in range(2)]
    small = [dict() for _ in range(2)]

    def ffn_back(l, j, xin, dy, sv, sub):
        dx, dmod, wacc, hb, dfb, ab, dgb, dub = _ffn_bwd(
            xin, dy, mods[l][sub], sv[f"f{sub}"], sv[f"g{sub}"], sv[f"u{sub}"], wup5, wdn4, l, j, S)
        gup = jnp.concatenate([
            _mm_tn(hb, dgb, f"dw_up_g_{l}_{j}", tnb=tf, split_n=True),
            _mm_tn(hb, dub, f"dw_up_u_{l}_{j}", tnb=tf, split_n=True)])
        gdn = _mm_tn(ab, dfb, f"dw_dn_{l}_{j}", tma=tf)
        g_up_l[l][j] = gup
        g_dn_l[l][j] = gdn
        dmods[l][sub] = dmod
        small[l][f"gn{sub}"] = wacc[0]
        return dx

    for l in (1, 0):
        sv = saved[l]
        dxc = ffn_back(l, 1, sv["x2"], dxc, sv, 2)
        dyl, dsb, dfx, dmo, dmod1, wacc_mo = _mix_out_bwd(
            dxc, sv["ylru"], sv["osb"], sv["ofx"], sv["mo"], mods[l][1], gmix8, wout, l, S)
        small[l]["gmix"] = wacc_mo[0]
        g_out_l[l] = _mm_tn(sv["y"], dmo, f"dw_out_{l}")
        sbq, sbk, sbv = sv["sb"]
        dsq, dsk, dsv = _sb_bwd(sbq, sbk, sbv, _heads(dsb, B, S, TQ_(S)), sv["t1"], l)
        fxq, fxk, fxv, cq, ck = sv["fx"]
        dfq, dfk, dfv, dck, wacc_fx = _fox_bwd(fxq, fxk, fxv, cq, ck, gqk8,
                                               _heads(dfx, B, S, TQ_(S)), sv["lse"], l)
        small[l]["gqk"] = wacc_fx[0:2]
        dcum = dck[:, :, :, 0, :].reshape(B, NH, S).transpose(0, 2, 1).reshape(T, NH)
        dff_, wacc_fg = _fgate_bwd(jnp.pad(dcum, ((0, 0), (0, 128 - NH))), sv["proj"], bfp, l, S)
        small[l]["bf"] = wacc_fg[0, :NH]
        dlx, dlg, dpr, dpi, ub, wacc_lru = _lru_bwd(dyl, sv["proj"], sv["hl"], cw8, vp8, wr_d, wi_d, l, S)
        small[l]["lru"] = wacc_lru
        small[l]["wr"] = _diag_blocks(_mm_tn(ub, dpr, f"dw_rgate_{l}"))
        small[l]["wi"] = _diag_blocks(_mm_tn(ub, dpi, f"dw_igate_{l}"))
        dproj = jnp.concatenate(
            [dlx, dlg, _unheads(dsq, B, S), _unheads(dsk, B, S), _unheads(dsv, B, S),
             _unheads(dfq, B, S), _unheads(dfk, B, S), _unheads(dfv, B, S), dff_], axis=1)
        g_in_l[l] = _mm_tn(sv["h1"], dproj, f"dw_in_{l}", tnb=N_INP // 3)[:, :N_IN]
        dxc, dmod_in, wacc_in = _mix_in_bwd(sv["x1"], dxc, mods[l][1], dproj, winp, l, S)
        dmods[l][1] = dmod_in + dmod1
        small[l]["gn1"] = wacc_in[0]
        dxc = ffn_back(l, 0, sv["x0"], dxc, sv, 0)
    grad_x = dxc.reshape(B, S, D)

    dmod_loc = jnp.stack([jnp.stack([dmods[l][j][:, 0:3, :] for j in range(3)], axis=1) for l in range(2)])
    drows = 2 * B * 9
    blk3 = _pack_rows([dmod_loc], -(-drows // 8) * 8, F32)
    ag3 = _all_gather8(blk3, "ag_dmod", True)
    dmod_all = ag3[:, :drows].reshape(N_DEV, 2, B, 9 * D).transpose(1, 0, 2, 3).reshape(2, n_all, 9 * D)
    dmod_mine = lax.dynamic_slice(dmod_all, (0, 0, sidx * ada_cols), (2, n_all, ada_cols))
    grad_w_ada = _ada_bwd(c_all, dmod_mine)
    dmod_rows = jnp.pad(dmod_all.transpose(1, 0, 2).reshape(n_all, 2 * 9, D), ((0, 0), (0, 6), (0, 0)))
    grad_b_ada = _sum_lead(dmod_rows, "grad_b_ada")[:2 * 9].reshape(2, 9 * D)

    sm_parts = [
        jnp.stack([small[l]["bf"] for l in range(2)]),
        jnp.stack([small[l]["lru"][4] for l in range(2)]),
        jnp.stack([small[l]["wr"] for l in range(2)]),
        jnp.stack([small[l]["lru"][5] for l in range(2)]),
        jnp.stack([small[l]["wi"] for l in range(2)]),
        jnp.stack([small[l]["lru"][6] for l in range(2)]),
        jnp.stack([small[l]["lru"][7] for l in range(2)]),
        jnp.stack([small[l]["gqk"] for l in range(2)]),
        jnp.stack([small[l]["gmix"] for l in range(2)]),
        jnp.stack([jnp.stack([small[l][f"gn{j}"] for j in range(3)]) for l in range(2)]),
        jnp.stack([small[l]["lru"][0:4] for l in range(2)]),
    ]
    sm_shapes = [p.shape for p in sm_parts]
    sm_rows = -(-sum(p.size for p in sm_parts) // (8 * ROW)) * 8
    ag4 = _all_gather8(_pack_rows(sm_parts, sm_rows, F32), "ag_small_grads", True)
    sm_sum = _sum_lead(ag4, "sum_small_grads").reshape(-1)
    (g_bf, g_cb, g_wr, g_br, g_wi, g_bi, g_lam, g_gqk, g_gmix, g_gn_full, g_cw_full) = _unpack(sm_sum, sm_shapes)
    g_gn = lax.dynamic_slice(g_gn_full, (0, 0, sidx * gn_cols), (2, 3, gn_cols))
    g_cw = lax.dynamic_slice(g_cw_full, (0, 0, sidx * cw_cols), (2, 4, cw_cols))

    gparts = [
        jnp.stack([jnp.stack(g_up_l[l], axis=1) for l in range(2)], axis=1),
        jnp.stack([jnp.stack([g.reshape(N_SHARD, -1, D) for g in g_dn_l[l]], axis=1) for l in range(2)], axis=1),
        jnp.stack([g.reshape(D, N_SHARD, -1).transpose(1, 0, 2) for g in g_in_l], axis=1),
        jnp.stack([g.reshape(N_SHARD, -1, D) for g in g_out_l], axis=1),
    ]
    gflat = jnp.concatenate([p.reshape(N_SHARD, -1) for p in gparts], axis=1)
    gpack = jnp.pad(gflat, ((0, 0), (0, 2 * P * ROW - gflat.shape[1]))).reshape(N_SHARD, 2, P, ROW)
    cvec = jnp.reshape(ci, (1,)).astype(jnp.int32)
    svec = jnp.reshape(sidx, (1,)).astype(jnp.int32)
    hsum = _add_sibling(gpack, _rs_to_sibling(gpack), cvec)
    fin = _add_chips(hsum, _rs_to_chips(hsum), svec)
    gshard = _share_halves(fin).reshape(-1)
    gw_up, gw_dn, gw_in, gw_out = _unpack(gshard, wshapes)

    def upd(w, g, m, v, name):
        sh = w.shape
        two = (w.size // sh[-1], sh[-1])
        dlt, mn, vn = _adamw(w.reshape(two), g.reshape(two), m.reshape(two), v.reshape(two), name)
        return dlt.reshape(sh), mn.reshape(sh), vn.reshape(sh)

    big = {
        "w_ada": (w_ada, grad_w_ada, m_w_ada, v_w_ada),
        "w_ffn_up": (w_ffn_up, gw_up, m_w_ffn_up, v_w_ffn_up),
        "w_ffn_down": (w_ffn_down, gw_dn, m_w_ffn_down, v_w_ffn_down),
        "w_in": (w_in, gw_in, m_w_in, v_w_in),
        "w_out": (w_out, gw_out, m_w_out, v_w_out),
    }
    res = {n: (t[1],) + upd(*t, f"adamw_{n}") for n, t in big.items()}

    smalls = {
        "b_ada": (b_ada, grad_b_ada, m_b_ada, v_b_ada),
        "g_norm": (g_norm, g_gn, m_g_norm, v_g_norm),
        "b_fgate": (b_fgate, g_bf, m_b_fgate, v_b_fgate),
        "conv_w": (conv_w, g_cw, m_conv_w, v_conv_w),
        "conv_b": (conv_b, g_cb, m_conv_b, v_conv_b),
        "w_rgate": (w_rgate, g_wr, m_w_rgate, v_w_rgate),
        "b_rgate": (b_rgate, g_br, m_b_rgate, v_b_rgate),
        "w_igate": (w_igate, g_wi, m_w_igate, v_w_igate),
        "b_igate": (b_igate, g_bi, m_b_igate, v_b_igate),
        "lru_lambda": (lru_lambda, g_lam, m_lru_lambda, v_lru_lambda),
        "g_qk": (g_qk, g_gqk, m_g_qk, v_g_qk),
        "g_mix_out": (g_mix_out, g_gmix, m_g_mix_out, v_g_mix_out),
    }
    names = list(smalls)
    shapes = [smalls[n][0].shape for n in names]
    prow = -(-sum(math.prod(s) for s in shapes) // (8 * ROW)) * 8
    packed = [_pack_rows([smalls[n][i].reshape(shapes[k]) for k, n in enumerate(names)], prow, F32) for i in range(4)]
    outs = _adamw(packed[0], packed[1], packed[2], packed[3], "adamw_small")
    un = [_unpack(o.reshape(-1), shapes) for o in outs]
    for k, n in enumerate(names):
        res[n] = (smalls[n][1].reshape(shapes[k]), un[0][k], un[1][k], un[2][k])

    order = ["w_ada", "b_ada", "g_norm", "w_ffn_up", "w_ffn_down", "w_in", "b_fgate", "conv_w", "conv_b",
             "w_rgate", "b_rgate", "w_igate", "b_igate", "lru_lambda", "g_qk", "g_mix_out", "w_out"]
    return (loss, grad_x, *[res[n][0] for n in order], *[res[n][1] for n in order],
            *[res[n][2] for n in order], *[res[n][3] for n in order])


def TQ_(S):
    return min(TQ, S)


def TK_(S):
    return min(TK, S)


def _unpack_shards(wg, shapes):
    out, off = [], 0
    for sh in shapes:
        n = math.prod(sh)
        out.append(wg[:, off:off + n].reshape((N_SHARD,) + tuple(sh)))
        off += n
    return out
```

```python
import math

import jax
import jax.numpy as jnp
from jax import lax
from jax.experimental import pallas as pl
from jax.experimental.pallas import tpu as pltpu

F32 = jnp.float32
BF16 = jnp.bfloat16
MESH = pl.DeviceIdType.MESH

D = 1024
HD = 64
LW = 512
NH = 4
AW = NH * HD
N_IN = 2564
N_INP = 2688
F_BLK = 2560 // 128
EPS = 1e-6
LRU_C = 8.0
SCALE = HD ** -0.5
NEG = -1e30
TQ = 256
TK = 128

ADAM_LR, ADAM_B1, ADAM_B2, ADAM_EPS, ADAM_WD, ADAM_STEP = 0.001, 0.9, 0.999, 1e-08, 0.01, 10

VMEM_BIG = 56 * 1024 * 1024
N_DEV = 8
N_SHARD = 4
ROW = 1024


def _cp(sem, vmem=None):
    return pltpu.CompilerParams(dimension_semantics=sem, vmem_limit_bytes=vmem)


def _dot(a, b):
    return jnp.dot(a, b, preferred_element_type=F32)


def _dot_nt(a, b):
    return lax.dot_general(a, b, (((1,), (1,)), ((), ())), preferred_element_type=F32)


def _dot_tn(a, b):
    return lax.dot_general(a, b, (((0,), (0,)), ((), ())), preferred_element_type=F32)


def _log1p(e):
    small = e * (1.0 - e * (0.5 - e * (1.0 / 3.0 - e * 0.25)))
    return jnp.where(e < 0.01, small, jnp.log(1.0 + e))


def _expm1_neg(x):
    small = x * (1.0 + x * 0.5 * (1.0 + x * (1.0 / 3.0) * (1.0 + x * 0.25 * (1.0 + x * 0.2))))
    return jnp.where(x > -0.05, small, jnp.exp(x) - 1.0)


def _sigmoid(x):
    return 1.0 / (1.0 + jnp.exp(-x))


_GELU_C = math.sqrt(2.0 / math.pi)


def _gelu_and_grad(x):
    x2 = x * x
    th = jnp.tanh(_GELU_C * (x + 0.044715 * x * x2))
    g = 0.5 * x * (1.0 + th)
    dg = 0.5 * (1.0 + th) + 0.5 * x * (1.0 - th * th) * _GELU_C * (1.0 + 3.0 * 0.044715 * x2)
    return g, dg


def _rms_rows(x):
    rstd = lax.rsqrt(jnp.mean(x * x, axis=-1, keepdims=True) + EPS)
    return x * rstd, rstd


def _rms_bwd(xn, rstd, dyn):
    return rstd * (dyn - xn * jnp.mean(dyn * xn, axis=-1, keepdims=True))


def _colsum(x):
    return jnp.sum(x, axis=0, keepdims=True)


def _rowsum(x):
    return jnp.sum(x, axis=1, keepdims=True)


def _split3(x):
    hi = x.astype(BF16)
    r = x - hi.astype(F32)
    mid = r.astype(BF16)
    lo = (r - mid.astype(F32)).astype(BF16)
    return hi, mid, lo


def _cumsum_mm(x, ones_tri, parts=3):
    ps = _split3(x)[:parts]
    acc = _dot(ps[0], ones_tri)
    for p in ps[1:]:
        acc = acc + _dot(p, ones_tri)
    return acc


def _tri(n, kind):
    r = lax.broadcasted_iota(jnp.int32, (n, n), 0)
    c = lax.broadcasted_iota(jnp.int32, (n, n), 1)
    m = {"row_gt_col": r > c, "row_le_col": r <= c, "row_lt_col": r < c}[kind]
    return jnp.where(m, 1.0, 0.0).astype(BF16)


def _normmod(x, mod_ref):
    xn, rstd = _rms_rows(x)
    h = xn * mod_ref[3:4, :] * mod_ref[1:2, :] + mod_ref[0:1, :]
    return h, xn, rstd


def _normmod_bwd(dh, xn, rstd, mod_ref, dmod_ref, wacc_ref):
    gn = mod_ref[3:4, :]
    sc = mod_ref[1:2, :]
    dmod_ref[0:1, :] += _colsum(dh)
    t = _colsum(dh * xn)
    dmod_ref[1:2, :] += t * gn
    wacc_ref[0:1, :] += t * sc
    return _rms_bwd(xn, rstd, dh * (gn * sc))


def _tile(n, want):
    t = min(n, want)
    while n % t:
        t //= 2
    return t


def _tile_div8(n, cap):
    best = 8
    for t in range(8, min(n, cap) + 1, 8):
        if n % t == 0:
            best = t
    return best


def _ffn_fwd(x, mod, wup5, wdn4, l, j, S):
    T = x.shape[0]
    tf = wup5.shape[-1]
    nk = 2
    tm = _tile(S, 512)
    tpb = S // tm

    def body(x_ref, mod_ref, wg_ref, wu_ref, wd_ref, xo_ref, g_ref, u_ref, f_ref, h_sc, acc_sc):
        k = pl.program_id(1)

        @pl.when(k == 0)
        def _():
            h, _, _ = _normmod(x_ref[...], mod_ref)
            h_sc[...] = h.astype(BF16)
            acc_sc[...] = jnp.zeros_like(acc_sc)

        h = h_sc[...]
        g = _dot(h, wg_ref[...])
        u = _dot(h, wu_ref[...])
        g_ref[...] = g.astype(BF16)
        u_ref[...] = u.astype(BF16)
        a = (g * _sigmoid(g)) * u
        acc_sc[...] += _dot(a.astype(BF16), wd_ref[...])

        @pl.when(k == nk - 1)
        def _():
            f = acc_sc[...]
            f_ref[...] = f.astype(BF16)
            xo_ref[...] = x_ref[...] + (0.5 * mod_ref[2:3, :]) * f

    return pl.pallas_call(
        body, name=f"ffn_fwd_{l}_{j}",
        grid=(T // tm, nk),
        in_specs=[
            pl.BlockSpec((tm, D), lambda i, k: (i, 0)),
            pl.BlockSpec((None, 8, D), lambda i, k: (i // tpb, 0, 0)),
            pl.BlockSpec((None, None, None, D, tf), lambda i, k: (k, l, j, 0, 0)),
            pl.BlockSpec((None, None, None, D, tf), lambda i, k: (nk + k, l, j, 0, 0)),
            pl.BlockSpec((None, None, tf, D), lambda i, k: (l, j, k, 0)),
        ],
        out_specs=[
            pl.BlockSpec((tm, D), lambda i, k: (i, 0)),
            pl.BlockSpec((tm, tf), lambda i, k: (i, k)),
            pl.BlockSpec((tm, tf), lambda i, k: (i, k)),
            pl.BlockSpec((tm, D), lambda i, k: (i, 0)),
        ],
        out_shape=[
            jax.ShapeDtypeStruct((T, D), F32),
            jax.ShapeDtypeStruct((T, nk * tf), BF16),
            jax.ShapeDtypeStruct((T, nk * tf), BF16),
            jax.ShapeDtypeStruct((T, D), BF16),
        ],
        scratch_shapes=[pltpu.VMEM((tm, D), BF16), pltpu.VMEM((tm, D), F32)],
        compiler_params=_cp(("arbitrary", "arbitrary"), VMEM_BIG),
    )(x, mod, wup5, wup5, wdn4)


def _ffn_bwd(x, dy, mod, f, g, u, wup5, wdn4, l, j, S):
    T = x.shape[0]
    tf = wup5.shape[-1]
    nk = 2
    tm = _tile(S, 256)
    tpb = S // tm

    def body(x_ref, dy_ref, mod_ref, f_ref, g_ref, u_ref, wg_ref, wu_ref, wd_ref,
             dx_ref, dmod_ref, wacc_ref, h_ref, df_ref, a_ref, dgu_ref, df_sc, dh_sc):
        i = pl.program_id(0)
        k = pl.program_id(1)

        @pl.when((i == 0) & (k == 0))
        def _():
            wacc_ref[...] = jnp.zeros_like(wacc_ref)

        @pl.when((i % tpb == 0) & (k == 0))
        def _():
            dmod_ref[...] = jnp.zeros_like(dmod_ref)

        @pl.when(k == 0)
        def _():
            dy_ = dy_ref[...]
            h, _, _ = _normmod(x_ref[...], mod_ref)
            h_ref[...] = h.astype(BF16)
            dfb = ((0.5 * mod_ref[2:3, :]) * dy_).astype(BF16)
            df_sc[...] = dfb
            df_ref[...] = dfb
            dmod_ref[2:3, :] += _colsum(0.5 * f_ref[...].astype(F32) * dy_)
            dh_sc[...] = jnp.zeros_like(dh_sc)

        da = _dot_nt(df_sc[...], wd_ref[...])
        gg = g_ref[...].astype(F32)
        uu = u_ref[...].astype(F32)
        sig = _sigmoid(gg)
        s = gg * sig
        a_ref[...] = (s * uu).astype(BF16)
        du = (da * s).astype(BF16)
        dg = (da * uu * (sig * (1.0 + gg * (1.0 - sig)))).astype(BF16)
        dgu_ref[0] = dg
        dgu_ref[1] = du
        dh_sc[...] += _dot_nt(dg, wg_ref[...]) + _dot_nt(du, wu_ref[...])

        @pl.when(k == nk - 1)
        def _():
            _, xn, rstd = _normmod(x_ref[...], mod_ref)
            dx_ref[...] = dy_ref[...] + _normmod_bwd(dh_sc[...], xn, rstd, mod_ref, dmod_ref, wacc_ref)

    return pl.pallas_call(
        body, name=f"ffn_bwd_{l}_{j}",
        grid=(T // tm, nk),
        in_specs=[
            pl.BlockSpec((tm, D), lambda i, k: (i, 0)),
            pl.BlockSpec((tm, D), lambda i, k: (i, 0)),
            pl.BlockSpec((None, 8, D), lambda i, k: (i // tpb, 0, 0)),
            pl.BlockSpec((tm, D), lambda i, k: (i, 0)),
            pl.BlockSpec((tm, tf), lambda i, k: (i, k)),
            pl.BlockSpec((tm, tf), lambda i, k: (i, k)),
            pl.BlockSpec((None, None, None, D, tf), lambda i, k: (k, l, j, 0, 0)),
            pl.BlockSpec((None, None, None, D, tf), lambda i, k: (nk + k, l, j, 0, 0)),
            pl.BlockSpec((None, None, tf, D), lambda i, k: (l, j, k, 0)),
        ],
        out_specs=[
            pl.BlockSpec((tm, D), lambda i, k: (i, 0)),
            pl.BlockSpec((None, 8, D), lambda i, k: (i // tpb, 0, 0)),
            pl.BlockSpec((8, D), lambda i, k: (0, 0)),
            pl.BlockSpec((tm, D), lambda i, k: (i, 0)),
            pl.BlockSpec((tm, D), lambda i, k: (i, 0)),
            pl.BlockSpec((tm, tf), lambda i, k: (i, k)),
            pl.BlockSpec((2, tm, tf), lambda i, k: (0, i, k)),
        ],
        out_shape=[
            jax.ShapeDtypeStruct((T, D), F32),
            jax.ShapeDtypeStruct((T // S, 8, D), F32),
            jax.ShapeDtypeStruct((8, D), F32),
            jax.ShapeDtypeStruct((T, D), BF16),
            jax.ShapeDtypeStruct((T, D), BF16),
            jax.ShapeDtypeStruct((T, nk * tf), BF16),
            jax.ShapeDtypeStruct((2, T, nk * tf), BF16),
        ],
        scratch_shapes=[pltpu.VMEM((tm, D), BF16), pltpu.VMEM((tm, D), F32)],
        compiler_params=_cp(("arbitrary", "arbitrary"), VMEM_BIG),
    )(x, dy, mod, f, g, u, wup5, wup5, wdn4)


def _mm_tn(a, b, name, tma=None, tnb=None, split_n=False):
    T, M = a.shape
    b3 = b if b.ndim == 3 else b[None]
    nb, _, N = b3.shape
    tma = tma or M
    tnb = tnb or N
    npb = N // tnb
    tt = _tile(T, 1024)

    def body(a_ref, b_ref, o_ref):
        @pl.when(pl.program_id(2) == 0)
        def _():
            o_ref[...] = jnp.zeros_like(o_ref)

        o_ref[...] += _dot_tn(a_ref[...], b_ref[...])

    if split_n:
        out_shape = jax.ShapeDtypeStruct((nb * npb, M, tnb), F32)
        out_spec = pl.BlockSpec((None, tma, tnb), lambda m, n, t: (n, m, 0))
    else:
        assert nb == 1
        out_shape = jax.ShapeDtypeStruct((M, N), F32)
        out_spec = pl.BlockSpec((tma, tnb), lambda m, n, t: (m, n))
    return pl.pallas_call(
        body, name=name,
        grid=(M // tma, nb * npb, T // tt),
        in_specs=[pl.BlockSpec((tt, tma), lambda m, n, t: (t, m)),
                  pl.BlockSpec((None, tt, tnb), lambda m, n, t: (n // npb, t, n % npb))],
        out_specs=out_spec,
        out_shape=out_shape,
        compiler_params=_cp(("arbitrary", "arbitrary", "arbitrary"), VMEM_BIG),
    )(a, b3)


def _mix_in_fwd(x, mod, winp, l, S):
    T = x.shape[0]
    tm = _tile(S, 512)
    tpb = S // tm

    def body(x_ref, mod_ref, w_ref, h_ref, p_ref):
        h, _, _ = _normmod(x_ref[...], mod_ref)
        hb = h.astype(BF16)
        h_ref[...] = hb
        p_ref[...] = _dot(hb, w_ref[...])

    return pl.pallas_call(
        body, name=f"mix_in_fwd_{l}",
        grid=(T // tm,),
        in_specs=[pl.BlockSpec((tm, D), lambda i: (i, 0)),
                  pl.BlockSpec((None, 8, D), lambda i: (i // tpb, 0, 0)),
                  pl.BlockSpec((None, D, N_INP), lambda i: (l, 0, 0))],
        out_specs=[pl.BlockSpec((tm, D), lambda i: (i, 0)),
                   pl.BlockSpec((tm, N_INP), lambda i: (i, 0))],
        out_shape=[jax.ShapeDtypeStruct((T, D), BF16), jax.ShapeDtypeStruct((T, N_INP), F32)],
        compiler_params=_cp(("arbitrary",), VMEM_BIG),
    )(x, mod, winp)


def _mix_in_bwd(x, dres, mod, dproj, winp, l, S):
    T = x.shape[0]
    tm = _tile(S, 512)
    tpb = S // tm

    def body(x_ref, dr_ref, mod_ref, dp_ref, w_ref, dx_ref, dmod_ref, wacc_ref):
        i = pl.program_id(0)

        @pl.when(i == 0)
        def _():
            wacc_ref[...] = jnp.zeros_like(wacc_ref)

        @pl.when(i % tpb == 0)
        def _():
            dmod_ref[...] = jnp.zeros_like(dmod_ref)

        dh = _dot_nt(dp_ref[...], w_ref[...])
        _, xn, rstd = _normmod(x_ref[...], mod_ref)
        dx_ref[...] = dr_ref[...] + _normmod_bwd(dh, xn, rstd, mod_ref, dmod_ref, wacc_ref)

    return pl.pallas_call(
        body, name=f"mix_in_bwd_{l}",
        grid=(T // tm,),
        in_specs=[pl.BlockSpec((tm, D), lambda i: (i, 0)),
                  pl.BlockSpec((tm, D), lambda i: (i, 0)),
                  pl.BlockSpec((None, 8, D), lambda i: (i // tpb, 0, 0)),
                  pl.BlockSpec((tm, N_INP), lambda i: (i, 0)),
                  pl.BlockSpec((None, D, N_INP), lambda i: (l, 0, 0))],
        out_specs=[pl.BlockSpec((tm, D), lambda i: (i, 0)),
                   pl.BlockSpec((None, 8, D), lambda i: (i // tpb, 0, 0)),
                   pl.BlockSpec((8, D), lambda i: (0, 0))],
        out_shape=[jax.ShapeDtypeStruct((T, D), F32),
                   jax.ShapeDtypeStruct((T // S, 8, D), F32),
                   jax.ShapeDtypeStruct((8, D), F32)],
        compiler_params=_cp(("arbitrary",), VMEM_BIG),
    )(x, dres, mod, dproj, winp)


_GROUPS = ((0, LW), (LW, LW + AW), (LW + AW, D))


def _mix_out_fwd(x, ylru, osb, ofox, mod, gmix, wout, l, S):
    T = x.shape[0]
    tm = _tile(S, 512)
    tpb = S // tm

    def body(x_ref, yl_ref, sb_ref, fx_ref, mod_ref, gm_ref, w_ref, xo_ref, y_ref, mo_ref):
        for src, (lo, hi) in zip((yl_ref, sb_ref, fx_ref), _GROUPS):
            vn, _ = _rms_rows(src[...])
            y_ref[:, lo:hi] = (vn * gm_ref[0:1, lo:hi]).astype(BF16)
        mo = _dot(y_ref[...], w_ref[...])
        mo_ref[...] = mo.astype(BF16)
        xo_ref[...] = x_ref[...] + mod_ref[2:3, :] * mo

    return pl.pallas_call(
        body, name=f"mix_out_fwd_{l}",
        grid=(T // tm,),
        in_specs=[pl.BlockSpec((tm, D), lambda i: (i, 0)),
                  pl.BlockSpec((tm, LW), lambda i: (i, 0)),
                  pl.BlockSpec((tm, AW), lambda i: (i, 0)),
                  pl.BlockSpec((tm, AW), lambda i: (i, 0)),
                  pl.BlockSpec((None, 8, D), lambda i: (i // tpb, 0, 0)),
                  pl.BlockSpec((None, 8, D), lambda i: (l, 0, 0)),
                  pl.BlockSpec((None, D, D), lambda i: (l, 0, 0))],
        out_specs=[pl.BlockSpec((tm, D), lambda i: (i, 0)),
                   pl.BlockSpec((tm, D), lambda i: (i, 0)),
                   pl.BlockSpec((tm, D), lambda i: (i, 0))],
        out_shape=[jax.ShapeDtypeStruct((T, D), F32),
                   jax.ShapeDtypeStruct((T, D), BF16),
                   jax.ShapeDtypeStruct((T, D), BF16)],
        compiler_params=_cp(("arbitrary",), VMEM_BIG),
    )(x, ylru, osb, ofox, mod, gmix, wout)


def _mix_out_bwd(dx2, ylru, osb, ofox, mo, mod, gmix, wout, l, S):
    T = dx2.shape[0]
    tm = _tile(S, 512)
    tpb = S // tm

    def body(dx_ref, yl_ref, sb_ref, fx_ref, mo_ref, mod_ref, gm_ref, w_ref,
             dyl_ref, dsb_ref, dfx_ref, dmo_ref, dmod_ref, wacc_ref):
        i = pl.program_id(0)

        @pl.when(i == 0)
        def _():
            wacc_ref[...] = jnp.zeros_like(wacc_ref)

        @pl.when(i % tpb == 0)
        def _():
            dmod_ref[...] = jnp.zeros_like(dmod_ref)

        dx = dx_ref[...]
        dmod_ref[2:3, :] += _colsum(mo_ref[...].astype(F32) * dx)
        dmo = (mod_ref[2:3, :] * dx).astype(BF16)
        dmo_ref[...] = dmo
        dy = _dot_nt(dmo, w_ref[...])
        for src, dst, (lo, hi) in zip((yl_ref, sb_ref, fx_ref), (dyl_ref, dsb_ref, dfx_ref), _GROUPS):
            vn, rstd = _rms_rows(src[...])
            dyg = dy[:, lo:hi]
            wacc_ref[0:1, lo:hi] += _colsum(dyg * vn)
            dst[...] = _rms_bwd(vn, rstd, dyg * gm_ref[0:1, lo:hi])

    return pl.pallas_call(
        body, name=f"mix_out_bwd_{l}",
        grid=(T // tm,),
        in_specs=[pl.BlockSpec((tm, D), lambda i: (i, 0)),
                  pl.BlockSpec((tm, LW), lambda i: (i, 0)),
                  pl.BlockSpec((tm, AW), lambda i: (i, 0)),
                  pl.BlockSpec((tm, AW), lambda i: (i, 0)),
                  pl.BlockSpec((tm, D), lambda i: (i, 0)),
                  pl.BlockSpec((None, 8, D), lambda i: (i // tpb, 0, 0)),
                  pl.BlockSpec((None, 8, D), lambda i: (l, 0, 0)),
                  pl.BlockSpec((None, D, D), lambda i: (l, 0, 0))],
        out_specs=[pl.BlockSpec((tm, LW), lambda i: (i, 0)),
                   pl.BlockSpec((tm, AW), lambda i: (i, 0)),
                   pl.BlockSpec((tm, AW), lambda i: (i, 0)),
                   pl.BlockSpec((tm, D), lambda i: (i, 0)),
                   pl.BlockSpec((None, 8, D), lambda i: (i // tpb, 0, 0)),
                   pl.BlockSpec((8, D), lambda i: (0, 0))],
        out_shape=[jax.ShapeDtypeStruct((T, LW), F32),
                   jax.ShapeDtypeStruct((T, AW), F32),
                   jax.ShapeDtypeStruct((T, AW), F32),
                   jax.ShapeDtypeStruct((T, D), BF16),
                   jax.ShapeDtypeStruct((T // S, 8, D), F32),
                   jax.ShapeDtypeStruct((8, D), F32)],
        compiler_params=_cp(("arbitrary",), VMEM_BIG),
    )(dx2, ylru, osb, ofox, mo, mod, gmix, wout)


def _loss_head(y, tgt, S):
    T = y.shape[0]
    tm = _tile(S, 512)

    def body(y_ref, t_ref, dy_ref, l_ref):
        @pl.when(pl.program_id(0) == 0)
        def _():
            l_ref[...] = jnp.zeros_like(l_ref)

        d = y_ref[...] - t_ref[...]
        dy_ref[...] = d * (1.0 / D)
        l_ref[...] += (0.5 / D) * _rowsum(_colsum(d * d))

    return pl.pallas_call(
        body, name="loss_head",
        grid=(T // tm,),
        in_specs=[pl.BlockSpec((tm, D), lambda i: (i, 0)), pl.BlockSpec((tm, D), lambda i: (i, 0))],
        out_specs=[pl.BlockSpec((tm, D), lambda i: (i, 0)), pl.BlockSpec((8, 128), lambda i: (0, 0))],
        out_shape=[jax.ShapeDtypeStruct((T, D), F32), jax.ShapeDtypeStruct((8, 128), F32)],
        compiler_params=_cp(("arbitrary",)),
    )(y, tgt)


def _lru_gates(u, vp_ref, wr_ref, wi_ref):
    ub = u.astype(BF16)
    r = _sigmoid(_dot(ub, wr_ref[...]) + vp_ref[1:2, :])
    ig = _sigmoid(_dot(ub, wi_ref[...]) + vp_ref[2:3, :])
    lam = vp_ref[3:4, :]
    sp = jnp.maximum(-lam, 0.0) + _log1p(jnp.exp(-jnp.abs(lam)))
    log_a = (-LRU_C) * r * sp
    a = jnp.exp(log_a)
    mult = jnp.sqrt(-_expm1_neg(2.0 * log_a))
    return ub, r, ig, sp, a, mult


def _conv_taps(x, xp, row, cw_ref):
    xs = [x]
    for d in (1, 2, 3):
        xs.append(jnp.where(row >= d, pltpu.roll(x, d, 0), pltpu.roll(xp, d, 0)))
    u = xs[0] * cw_ref[3:4, :]
    for d in (1, 2, 3):
        u = u + xs[d] * cw_ref[3 - d:4 - d, :]
    return xs, u


def _lru_fwd(proj, cw, vp, wr, wi, l, S):
    T = proj.shape[0]
    ts = _tile(S, 256)
    nb = S // ts

    def body(x_ref, lg_ref, cw_ref, vp_ref, wr_ref, wi_ref, y_ref, h_ref, xp_sc, hc_sc):
        @pl.when(pl.program_id(1) == 0)
        def _():
            xp_sc[...] = jnp.zeros_like(xp_sc)
            hc_sc[...] = jnp.zeros_like(hc_sc)

        row = lax.broadcasted_iota(jnp.int32, (ts, LW), 0)
        x = x_ref[...]
        _, u = _conv_taps(x, xp_sc[...], row, cw_ref)
        u = u + vp_ref[0:1, :]
        xp_sc[...] = x
        _, _, ig, _, a, mult = _lru_gates(u, vp_ref, wr_ref, wi_ref)
        bv = mult * (ig * u)
        av = a
        d = 1
        while d < ts:
            a_s = jnp.where(row >= d, pltpu.roll(av, d, 0), 1.0)
            b_s = jnp.where(row >= d, pltpu.roll(bv, d, 0), 0.0)
            bv = av * b_s + bv
            av = av * a_s
            d *= 2
        h = bv + av * hc_sc[7:8, :]
        hc_sc[...] = h[ts - 8:ts, :]
        h_ref[...] = h
        gl, _ = _gelu_and_grad(lg_ref[...])
        y_ref[...] = h * gl

    return pl.pallas_call(
        body, name=f"lru_fwd_{l}",
        grid=(T // S, nb),
        in_specs=[pl.BlockSpec((ts, LW), lambda b, j: (b * nb + j, 0)),
                  pl.BlockSpec((ts, LW), lambda b, j: (b * nb + j, 1)),
                  pl.BlockSpec((None, 8, LW), lambda b, j: (l, 0, 0)),
                  pl.BlockSpec((None, 8, LW), lambda b, j: (l, 0, 0)),
                  pl.BlockSpec((None, LW, LW), lambda b, j: (l, 0, 0)),
                  pl.BlockSpec((None, LW, LW), lambda b, j: (l, 0, 0))],
        out_specs=[pl.BlockSpec((ts, LW), lambda b, j: (b * nb + j, 0)),
                   pl.BlockSpec((ts, LW), lambda b, j: (b * nb + j, 0))],
        out_shape=[jax.ShapeDtypeStruct((T, LW), F32), jax.ShapeDtypeStruct((T, LW), F32)],
        scratch_shapes=[pltpu.VMEM((ts, LW), F32), pltpu.VMEM((8, LW), F32)],
        compiler_params=_cp(("arbitrary", "arbitrary")),
    )(proj, proj, cw, vp, wr, wi)


def _lru_bwd(dyl, proj, h, cw, vp, wr, wi, l, S):
    T = proj.shape[0]
    ts = _tile(S, 256)
    nb = S // ts

    def body(dy_ref, x_ref, xprev_ref, lg_ref, h_ref, hprev_ref, cw_ref, vp_ref, wr_ref, wi_ref,
             dx_ref, dlg_ref, dpr_ref, dpi_ref, ub_ref, wacc_ref, gc_sc, af_sc, dun_sc):
        b = pl.program_id(0)
        j = pl.program_id(1)
        first = j == nb - 1

        @pl.when((b == 0) & (j == 0))
        def _():
            wacc_ref[...] = jnp.zeros_like(wacc_ref)

        @pl.when(j == 0)
        def _():
            gc_sc[...] = jnp.zeros_like(gc_sc)
            af_sc[...] = jnp.ones_like(af_sc)
            dun_sc[...] = jnp.zeros_like(dun_sc)

        row = lax.broadcasted_iota(jnp.int32, (ts, LW), 0)
        keep = jnp.where(first, 0.0, 1.0)
        x = x_ref[...]
        xs, u = _conv_taps(x, xprev_ref[...] * keep, row, cw_ref)
        u = u + vp_ref[0:1, :]
        ub, r, ig, sp, a, mult = _lru_gates(u, vp_ref, wr_ref, wi_ref)
        ub_ref[...] = ub
        hh = h_ref[...]
        h_m1 = jnp.where(row >= 1, pltpu.roll(hh, 1, 0), pltpu.roll(hprev_ref[...] * keep, 1, 0))
        dy = dy_ref[...]
        gl, dgl = _gelu_and_grad(lg_ref[...])
        dlg_ref[...] = (dy * hh * dgl).astype(BF16)
        bv = dy * gl
        av = jnp.where(row < ts - 1, pltpu.roll(a, ts - 1, 0), af_sc[0:1, :])
        d = 1
        while d < ts:
            a_s = jnp.where(row < ts - d, pltpu.roll(av, ts - d, 0), 1.0)
            b_s = jnp.where(row < ts - d, pltpu.roll(bv, ts - d, 0), 0.0)
            bv = av * b_s + bv
            av = av * a_s
            d *= 2
        gt = bv + av * gc_sc[0:1, :]
        gc_sc[...] = gt[0:8, :]
        af_sc[...] = a[0:8, :]
        da = gt * h_m1
        d_ig = gt * mult * u
        d_mult = gt * ig * u
        du = gt * mult * ig
        dlog_a = da * a - d_mult * (a * a) / mult
        dpre_r = (dlog_a * ((-LRU_C) * sp)) * r * (1.0 - r)
        dpre_i = d_ig * ig * (1.0 - ig)
        lam = vp_ref[3:4, :]
        wacc_ref[7:8, :] += _colsum(dlog_a * r) * (LRU_C * _sigmoid(-lam))
        wacc_ref[5:6, :] += _colsum(dpre_r)
        wacc_ref[6:7, :] += _colsum(dpre_i)
        dprb = dpre_r.astype(BF16)
        dpib = dpre_i.astype(BF16)
        dpr_ref[...] = dprb
        dpi_ref[...] = dpib
        du = du + _dot_nt(dprb, wr_ref[...]) + _dot_nt(dpib, wi_ref[...])
        wacc_ref[4:5, :] += _colsum(du)
        dun = dun_sc[...]
        dx = du * cw_ref[3:4, :]
        wacc_ref[3:4, :] += _colsum(du * xs[0])
        for dd in (1, 2, 3):
            du_s = jnp.where(row < ts - dd, pltpu.roll(du, ts - dd, 0), pltpu.roll(dun, ts - dd, 0))
            dx = dx + du_s * cw_ref[3 - dd:4 - dd, :]
            wacc_ref[3 - dd:4 - dd, :] += _colsum(du * xs[dd])
        dun_sc[...] = du
        dx_ref[...] = dx.astype(BF16)

    def tb(b, j):
        return b * nb + (nb - 1 - j)

    def tbp(b, j):
        return b * nb + jnp.maximum(nb - 2 - j, 0)

    return pl.pallas_call(
        body, name=f"lru_bwd_{l}",
        grid=(T // S, nb),
        in_specs=[pl.BlockSpec((ts, LW), lambda b, j: (tb(b, j), 0)),
                  pl.BlockSpec((ts, LW), lambda b, j: (tb(b, j), 0)),
                  pl.BlockSpec((ts, LW), lambda b, j: (tbp(b, j), 0)),
                  pl.BlockSpec((ts, LW), lambda b, j: (tb(b, j), 1)),
                  pl.BlockSpec((ts, LW), lambda b, j: (tb(b, j), 0)),
                  pl.BlockSpec((ts, LW), lambda b, j: (tbp(b, j), 0)),
                  pl.BlockSpec((None, 8, LW), lambda b, j: (l, 0, 0)),
                  pl.BlockSpec((None, 8, LW), lambda b, j: (l, 0, 0)),
                  pl.BlockSpec((None, LW, LW), lambda b, j: (l, 0, 0)),
                  pl.BlockSpec((None, LW, LW), lambda b, j: (l, 0, 0))],
        out_specs=[pl.BlockSpec((ts, LW), lambda b, j: (tb(b, j), 0)),
                   pl.BlockSpec((ts, LW), lambda b, j: (tb(b, j), 0)),
                   pl.BlockSpec((ts, LW), lambda b, j: (tb(b, j), 0)),
                   pl.BlockSpec((ts, LW), lambda b, j: (tb(b, j), 0)),
                   pl.BlockSpec((ts, LW), lambda b, j: (tb(b, j), 0)),
                   pl.BlockSpec((8, LW), lambda b, j: (0, 0))],
        out_shape=[jax.ShapeDtypeStruct((T, LW), BF16),
                   jax.ShapeDtypeStruct((T, LW), BF16),
                   jax.ShapeDtypeStruct((T, LW), BF16),
                   jax.ShapeDtypeStruct((T, LW), BF16),
                   jax.ShapeDtypeStruct((T, LW), BF16),
                   jax.ShapeDtypeStruct((8, LW), F32)],
        scratch_shapes=[pltpu.VMEM((8, LW), F32), pltpu.VMEM((8, LW), F32), pltpu.VMEM((ts, LW), F32)],
        compiler_params=_cp(("arbitrary", "arbitrary")),
    )(dyl, proj, proj, proj, h, h, cw, vp, wr, wi)


def _fgate_fwd(proj, bfp, l, S):
    T = proj.shape[0]

    def body(x_ref, b_ref, o_ref):
        z = x_ref[...] + b_ref[0:1, :]
        v = jnp.minimum(z, 0.0) - _log1p(jnp.exp(-jnp.abs(z)))
        row = lax.broadcasted_iota(jnp.int32, (S, 128), 0)
        d = 1
        while d < S:
            v = v + jnp.where(row >= d, pltpu.roll(v, d, 0), 0.0)
            d *= 2
        o_ref[...] = v

    return pl.pallas_call(
        body, name=f"fgate_fwd_{l}",
        grid=(T // S,),
        in_specs=[pl.BlockSpec((S, 128), lambda b: (b, F_BLK)),
                  pl.BlockSpec((None, 8, 128), lambda b: (l, 0, 0))],
        out_specs=pl.BlockSpec((S, 128), lambda b: (b, 0)),
        out_shape=jax.ShapeDtypeStruct((T, 128), F32),
        compiler_params=_cp(("arbitrary",)),
    )(proj, bfp)


def _fgate_bwd(dcum, proj, bfp, l, S):
    T = proj.shape[0]

    def body(d_ref, x_ref, b_ref, o_ref, wacc_ref):
        @pl.when(pl.program_id(0) == 0)
        def _():
            wacc_ref[...] = jnp.zeros_like(wacc_ref)

        v = d_ref[...]
        row = lax.broadcasted_iota(jnp.int32, (S, 128), 0)
        d = 1
        while d < S:
            v = v + jnp.where(row < S - d, pltpu.roll(v, S - d, 0), 0.0)
            d *= 2
        z = x_ref[...] + b_ref[0:1, :]
        dz = v * _sigmoid(-z)
        o_ref[...] = dz.astype(BF16)
        wacc_ref[0:1, :] += _colsum(dz)

    return pl.pallas_call(
        body, name=f"fgate_bwd_{l}",
        grid=(T // S,),
        in_specs=[pl.BlockSpec((S, 128), lambda b: (b, 0)),
                  pl.BlockSpec((S, 128), lambda b: (b, F_BLK)),
                  pl.BlockSpec((None, 8, 128), lambda b: (l, 0, 0))],
        out_specs=[pl.BlockSpec((S, 128), lambda b: (b, 0)), pl.BlockSpec((8, 128), lambda b: (0, 0))],
        out_shape=[jax.ShapeDtypeStruct((T, 128), BF16), jax.ShapeDtypeStruct((8, 128), F32)],
        compiler_params=_cp(("arbitrary",)),
    )(dcum, proj, bfp)


def _logsig_parts(z):
    e = jnp.exp(-jnp.abs(z))
    l1p = jnp.log(1.0 + e)
    return e, jnp.minimum(z, 0.0) - l1p, -jnp.maximum(z, 0.0) - l1p


def _sb_fwd(q, k, v, l):
    B, H, nq, tq, _ = q.shape
    nk, tk = k.shape[2], k.shape[3]
    rr = tq // tk

    def body(q_ref, k_ref, v_ref, o_ref, t1_ref):
        tri = _tri(tk, "row_gt_col")
        ti = lax.broadcasted_iota(jnp.int32, (tq, 1), 0)
        si = lax.broadcasted_iota(jnp.int32, (1, tk), 1)

        def qloop(qb, carry):
            qq = q_ref[qb]
            tpos = qb * tq + ti
            nkb = (qb + 1) * rr

            def kloop(i, c):
                acc, run = c
                kb = nkb - 1 - i
                z = _dot_nt(qq, k_ref[kb]) * SCALE
                past = (kb * tk + si) < tpos
                _, lb, l1 = _logsig_parts(z)
                l1m = jnp.where(past, l1, 0.0)
                aft = _cumsum_mm(l1m, tri) + run
                w = jnp.where(past, jnp.exp(lb + aft), 0.0)
                acc = acc + _dot(w.astype(BF16), v_ref[kb])
                return acc, run + _rowsum(l1m)

            acc, run = lax.fori_loop(0, nkb, kloop, (jnp.zeros((tq, HD), F32), jnp.zeros((tq, 1), F32)))
            o_ref[qb] = acc
            t1_ref[qb] = run
            return carry

        lax.fori_loop(0, nq, qloop, 0)

    qs = pl.BlockSpec((None, None, nq, tq, HD), lambda b, h: (b, h, 0, 0, 0))
    ks = pl.BlockSpec((None, None, nk, tk, HD), lambda b, h: (b, h, 0, 0, 0))
    return pl.pallas_call(
        body, name=f"sb_fwd_{l}",
        grid=(B, H),
        in_specs=[qs, ks, ks],
        out_specs=[qs, pl.BlockSpec((None, None, nq, tq, 1), lambda b, h: (b, h, 0, 0, 0))],
        out_shape=[jax.ShapeDtypeStruct((B, H, nq, tq, HD), F32),
                   jax.ShapeDtypeStruct((B, H, nq, tq, 1), F32)],
        compiler_params=_cp(("arbitrary", "arbitrary"), VMEM_BIG),
    )(q, k, v)


def _sb_bwd(q, k, v, do, t1, l):
    B, H, nq, tq, _ = q.shape
    nk, tk = k.shape[2], k.shape[3]
    rr = tq // tk

    def body(q_ref, k_ref, v_ref, do_ref, t1_ref, dq_ref, dk_ref, dv_ref, dk_sc, dv_sc):
        dk_sc[...] = jnp.zeros_like(dk_sc)
        dv_sc[...] = jnp.zeros_like(dv_sc)
        tri_in = _tri(tk, "row_le_col")
        tri_ex = _tri(tk, "row_lt_col")
        ti = lax.broadcasted_iota(jnp.int32, (tq, 1), 0)
        si = lax.broadcasted_iota(jnp.int32, (1, tk), 1)

        def qloop(qb, carry):
            qq = q_ref[qb]
            dob = do_ref[qb].astype(BF16)
            tot = t1_ref[qb]
            tpos = qb * tq + ti
            nkb = (qb + 1) * rr

            def kloop(kb, c):
                dq, run1, rung = c
                kk = k_ref[kb]
                vv = v_ref[kb]
                z = _dot_nt(qq, kk) * SCALE
                past = (kb * tk + si) < tpos
                e, lb, l1 = _logsig_parts(z)
                l1m = jnp.where(past, l1, 0.0)
                aft = tot - (run1 + _cumsum_mm(l1m, tri_in))
                w = jnp.where(past, jnp.exp(lb + aft), 0.0)
                gm = w * _dot_nt(dob, vv)
                cpre = rung + _cumsum_mm(gm, tri_ex, parts=2)
                inv = 1.0 / (1.0 + e)
                sig = jnp.where(z >= 0.0, inv, e * inv)
                dz = jnp.where(past, gm * (1.0 - sig) - cpre * sig, 0.0).astype(BF16)
                dv_sc[kb] += _dot_tn(w.astype(BF16), dob)
                dk_sc[kb] += _dot_tn(dz, qq) * SCALE
                dq = dq + _dot(dz, kk) * SCALE
                return dq, run1 + _rowsum(l1m), rung + _rowsum(gm)

            z1 = jnp.zeros((tq, 1), F32)
            dq, _, _ = lax.fori_loop(0, nkb, kloop, (jnp.zeros((tq, HD), F32), z1, z1))
            dq_ref[qb] = dq.astype(BF16)
            return carry

        lax.fori_loop(0, nq, qloop, 0)
        dk_ref[...] = dk_sc[...].astype(BF16)
        dv_ref[...] = dv_sc[...].astype(BF16)

    qs = pl.BlockSpec((None, None, nq, tq, HD), lambda b, h: (b, h, 0, 0, 0))
    ks = pl.BlockSpec((None, None, nk, tk, HD), lambda b, h: (b, h, 0, 0, 0))
    return pl.pallas_call(
        body, name=f"sb_bwd_{l}",
        grid=(B, H),
        in_specs=[qs, ks, ks, qs, pl.BlockSpec((None, None, nq, tq, 1), lambda b, h: (b, h, 0, 0, 0))],
        out_specs=[qs, ks, ks],
        out_shape=[jax.ShapeDtypeStruct((B, H, nq, tq, HD), BF16),
                   jax.ShapeDtypeStruct((B, H, nk, tk, HD), BF16),
                   jax.ShapeDtypeStruct((B, H, nk, tk, HD), BF16)],
        scratch_shapes=[pltpu.VMEM((nk, tk, HD), F32), pltpu.VMEM((nk, tk, HD), F32)],
        compiler_params=_cp(("arbitrary", "arbitrary"), VMEM_BIG),
    )(q, k, v, do, t1)


def _fox_fwd(q, k, v, cq, ck, gqk, l):
    B, H, nq, tq, _ = q.shape
    nk, tk = k.shape[2], k.shape[3]
    rr = tq // tk

    def body(q_ref, k_ref, v_ref, cq_ref, ck_ref, g_ref, o_ref, lse_ref, fk_sc):
        g0 = g_ref[0:1, :]
        g1 = g_ref[1:2, :]

        def kprep(kb, c):
            kn, _ = _rms_rows(k_ref[kb])
            fk_sc[kb] = (kn * g1).astype(BF16)
            return c

        lax.fori_loop(0, nk, kprep, 0)
        ti = lax.broadcasted_iota(jnp.int32, (tq, 1), 0)
        si = lax.broadcasted_iota(jnp.int32, (1, tk), 1)

        def qloop(qb, carry):
            qn, _ = _rms_rows(q_ref[qb])
            fq = (qn * g0).astype(BF16)
            cqq = cq_ref[qb]
            tpos = qb * tq + ti

            def kloop(kb, c):
                m, lsum, acc = c
                s = _dot_nt(fq, fk_sc[kb]) * SCALE + cqq - ck_ref[kb]
                s = jnp.where((kb * tk + si) <= tpos, s, NEG)
                m2 = jnp.maximum(m, jnp.max(s, axis=1, keepdims=True))
                al = jnp.exp(m - m2)
                p = jnp.exp(s - m2)
                return m2, al * lsum + _rowsum(p), al * acc + _dot(p.astype(BF16), v_ref[kb])

            m, lsum, acc = lax.fori_loop(
                0, (qb + 1) * rr, kloop,
                (jnp.full((tq, 1), NEG, F32), jnp.zeros((tq, 1), F32), jnp.zeros((tq, HD), F32)))
            o_ref[qb] = acc / lsum
            lse_ref[qb] = m + jnp.log(lsum)
            return carry

        lax.fori_loop(0, nq, qloop, 0)

    qs = pl.BlockSpec((None, None, nq, tq, HD), lambda b, h: (b, h, 0, 0, 0))
    ks = pl.BlockSpec((None, None, nk, tk, HD), lambda b, h: (b, h, 0, 0, 0))
    cqs = pl.BlockSpec((None, None, nq, tq, 1), lambda b, h: (b, h, 0, 0, 0))
    cks = pl.BlockSpec((None, None, nk, 1, tk), lambda b, h: (b, h, 0, 0, 0))
    return pl.pallas_call(
        body, name=f"fox_fwd_{l}",
        grid=(B, H),
        in_specs=[qs, ks, ks, cqs, cks, pl.BlockSpec((None, 8, HD), lambda b, h: (l, 0, 0))],
        out_specs=[qs, cqs],
        out_shape=[jax.ShapeDtypeStruct((B, H, nq, tq, HD), F32),
                   jax.ShapeDtypeStruct((B, H, nq, tq, 1), F32)],
        scratch_shapes=[pltpu.VMEM((nk, tk, HD), BF16)],
        compiler_params=_cp(("arbitrary", "arbitrary"), VMEM_BIG),
    )(q, k, v, cq, ck, gqk)


def _fox_bwd(q, k, v, cq, ck, gqk, do, lse, l):
    B, H, nq, tq, _ = q.shape
    nk, tk = k.shape[2], k.shape[3]
    rr = tq // tk

    def body(q_ref, k_ref, v_ref, cq_ref, ck_ref, g_ref, do_ref, lse_ref,
             dq_ref, dk_ref, dv_ref, dc_ref, wacc_ref, fk_sc, dfk_sc, dv_sc):
        @pl.when((pl.program_id(0) == 0) & (pl.program_id(1) == 0))
        def _():
            wacc_ref[...] = jnp.zeros_like(wacc_ref)

        g0 = g_ref[0:1, :]
        g1 = g_ref[1:2, :]
        dfk_sc[...] = jnp.zeros_like(dfk_sc)
        dv_sc[...] = jnp.zeros_like(dv_sc)
        dc_ref[...] = jnp.zeros_like(dc_ref)

        def kprep(kb, c):
            kn, _ = _rms_rows(k_ref[kb])
            fk_sc[kb] = (kn * g1).astype(BF16)
            return c

        lax.fori_loop(0, nk, kprep, 0)
        ti = lax.broadcasted_iota(jnp.int32, (tq, 1), 0)
        si = lax.broadcasted_iota(jnp.int32, (1, tk), 1)

        def qloop(qb, carry):
            qn, qr = _rms_rows(q_ref[qb])
            fq = (qn * g0).astype(BF16)
            cqq = cq_ref[qb]
            lse = lse_ref[qb]
            dob = do_ref[qb].astype(BF16)
            tpos = qb * tq + ti

            def probs(kb):
                s = _dot_nt(fq, fk_sc[kb]) * SCALE + cqq - ck_ref[kb]
                p = jnp.where((kb * tk + si) <= tpos, jnp.exp(s - lse), 0.0)
                return p, _dot_nt(dob, v_ref[kb])

            def dloop(kb, acc):
                p, dp = probs(kb)
                return acc + _rowsum(p * dp)

            dlt = lax.fori_loop(0, (qb + 1) * rr, dloop, jnp.zeros((tq, 1), F32))

            def kloop(kb, dfq):
                fk = fk_sc[kb]
                p, dp = probs(kb)
                ds = p * (dp - dlt)
                dsb = ds.astype(BF16)
                dv_sc[kb] += _dot_tn(p.astype(BF16), dob)
                dfk_sc[kb] += _dot_tn(dsb, fq) * SCALE
                dc_ref[kb] += jnp.broadcast_to(-_colsum(ds), (8, tk))
                return dfq + _dot(dsb, fk) * SCALE

            dfq = lax.fori_loop(0, (qb + 1) * rr, kloop, jnp.zeros((tq, HD), F32))
            wacc_ref[0:1, :] += _colsum(dfq * qn)
            dq_ref[qb] = _rms_bwd(qn, qr, dfq * g0).astype(BF16)
            return carry

        lax.fori_loop(0, nq, qloop, 0)

        def kfin(kb, c):
            kn, kr = _rms_rows(k_ref[kb])
            dfk = dfk_sc[kb]
            wacc_ref[1:2, :] += _colsum(dfk * kn)
            dk_ref[kb] = _rms_bwd(kn, kr, dfk * g1).astype(BF16)
            return c

        lax.fori_loop(0, nk, kfin, 0)
        dv_ref[...] = dv_sc[...].astype(BF16)

    qs = pl.BlockSpec((None, None, nq, tq, HD), lambda b, h: (b, h, 0, 0, 0))
    ks = pl.BlockSpec((None, None, nk, tk, HD), lambda b, h: (b, h, 0, 0, 0))
    cqs = pl.BlockSpec((None, None, nq, tq, 1), lambda b, h: (b, h, 0, 0, 0))
    cks = pl.BlockSpec((None, None, nk, 1, tk), lambda b, h: (b, h, 0, 0, 0))
    return pl.pallas_call(
        body, name=f"fox_bwd_{l}",
        grid=(B, H),
        in_specs=[qs, ks, ks, cqs, cks, pl.BlockSpec((None, 8, HD), lambda b, h: (l, 0, 0)), qs, cqs],
        out_specs=[qs, ks, ks,
                   pl.BlockSpec((None, None, nk, 8, tk), lambda b, h: (b, h, 0, 0, 0)),
                   pl.BlockSpec((8, HD), lambda b, h: (0, 0))],
        out_shape=[jax.ShapeDtypeStruct((B, H, nq, tq, HD), BF16),
                   jax.ShapeDtypeStruct((B, H, nk, tk, HD), BF16),
                   jax.ShapeDtypeStruct((B, H, nk, tk, HD), BF16),
                   jax.ShapeDtypeStruct((B, H, nk, 8, tk), F32),
                   jax.ShapeDtypeStruct((8, HD), F32)],
        scratch_shapes=[pltpu.VMEM((nk, tk, HD), BF16), pltpu.VMEM((nk, tk, HD), F32),
                        pltpu.VMEM((nk, tk, HD), F32)],
        compiler_params=_cp(("arbitrary", "arbitrary"), VMEM_BIG),
    )(q, k, v, cq, ck, gqk, do, lse)


SBQ_BLK, SBK_BLK, SBV_BLK = 8, 10, 12
FXQ_BLK, FXK_BLK, FXV_BLK = 14, 16, 18
PAIR = 2 * HD


def _lane_masks():
    lane = lax.broadcasted_iota(jnp.int32, (1, PAIR), 1)
    return lane, lane < HD


def _pair_select(m0, a0, a1):
    return jnp.where(m0, a0, a1)


def _pair_split(x, m0):
    return jnp.where(m0, x, 0.0).astype(BF16), jnp.where(m0, 0.0, x).astype(BF16)


def _pair_mean(x, m0):
    s0 = _rowsum(jnp.where(m0, x, 0.0))
    s1 = _rowsum(x) - s0
    return jnp.where(m0, s0, s1) * (1.0 / HD)


def _pair_rms(x, m0):
    rstd = lax.rsqrt(_pair_mean(x * x, m0) + EPS)
    return x * rstd, rstd


def _pair_rms_bwd(xn, rstd, dyn, m0):
    return rstd * (dyn - xn * _pair_mean(dyn * xn, m0))


def _logsig2(z):
    l1p = jnp.log(1.0 + jnp.exp(-jnp.abs(z)))
    lb = jnp.minimum(z, 0.0) - l1p
    return lb, lb - z


def _rows(ref, blk, size):
    return ref[pl.ds(pl.multiple_of(blk * size, size), size), :]


def _sbp_fwd(proj, l, S):
    T = proj.shape[0]
    tq, tk = TQ_(S), TK_(S)
    assert tq == 2 * tk
    nq = S // tq

    def body(q_ref, k_ref, v_ref, o_ref, t1_ref, kb_sc, vb_sc):
        kb_sc[...] = k_ref[...].astype(BF16)
        vb_sc[...] = v_ref[...].astype(BF16)
        lane, m0 = _lane_masks()
        tri = _tri(tk, "row_gt_col")
        ti = lax.broadcasted_iota(jnp.int32, (tq, 1), 0)
        si = lax.broadcasted_iota(jnp.int32, (1, tk), 1)

        def qloop(qb, carry):
            qh = _pair_split(_rows(q_ref, qb, tq) * SCALE, m0)
            tpos = qb * tq + ti

            def step(kbs, c, masked):
                pre = []
                for h in range(2):
                    for kb in kbs:
                        z = _dot_nt(qh[h], _rows(kb_sc, kb, tk))
                        lb, l1 = _logsig2(z)
                        past = None
                        if masked:
                            past = (kb * tk + si) < tpos
                            l1 = jnp.where(past, l1, 0.0)
                        pre.append((lb, l1, _cumsum_mm(l1, tri, parts=2), past))
                out = []
                for h in range(2):
                    acc, run = c[h]
                    for n, kb in enumerate(kbs):
                        lb, l1, cs, past = pre[2 * h + n]
                        w = jnp.exp(lb + (cs + run))
                        if masked:
                            w = jnp.where(past, w, 0.0)
                        acc = acc + _dot(w.astype(BF16), _rows(vb_sc, kb, tk))
                        run = run + (cs[:, 0:1] + l1[:, 0:1])
                    out.append((acc, run))
                return tuple(out)

            zero = (jnp.zeros((tq, PAIR), F32), jnp.zeros((tq, 1), F32))
            c = step((2 * qb + 1, 2 * qb), (zero, zero), True)
            c = lax.fori_loop(0, qb, lambda i, cc: step((2 * (qb - i) - 1, 2 * (qb - i) - 2), cc, False), c)
            r0 = pl.multiple_of(qb * tq, tq)
            o_ref[pl.ds(r0, tq), :] = _pair_select(m0, c[0][0], c[1][0])
            t1_ref[pl.ds(r0, tq), :] = jnp.where(lane == 0, c[0][1], jnp.where(lane == 1, c[1][1], 0.0))
            return carry

        lax.fori_loop(0, nq, qloop, 0)

    def col(blk):
        return pl.BlockSpec((S, PAIR), lambda b, p: (b, blk + p))

    return pl.pallas_call(
        body, name=f"sb_fwd_{l}",
        grid=(T // S, 2),
        in_specs=[col(SBQ_BLK), col(SBK_BLK), col(SBV_BLK)],
        out_specs=[col(0), col(0)],
        out_shape=[jax.ShapeDtypeStruct((T, AW), F32), jax.ShapeDtypeStruct((T, AW), F32)],
        scratch_shapes=[pltpu.VMEM((S, PAIR), BF16), pltpu.VMEM((S, PAIR), BF16)],
        compiler_params=_cp(("arbitrary", "arbitrary"), VMEM_BIG),
    )(proj, proj, proj)


def _sbp_bwd(proj, do, t1, l, S):
    T = proj.shape[0]
    tq, tk = TQ_(S), TK_(S)
    assert tq == 2 * tk
    nq = S // tq

    def body(q_ref, k_ref, v_ref, do_ref, t1_ref, dq_ref, dk_ref, dv_ref, kb_sc, vb_sc, dk_sc, dv_sc):
        kb_sc[...] = k_ref[...].astype(BF16)
        vb_sc[...] = v_ref[...].astype(BF16)
        dk_sc[...] = jnp.zeros_like(dk_sc)
        dv_sc[...] = jnp.zeros_like(dv_sc)
        _, m0 = _lane_masks()
        tri_in = _tri(tk, "row_le_col")
        tri_ex = _tri(tk, "row_lt_col")
        ti = lax.broadcasted_iota(jnp.int32, (tq, 1), 0)
        si = lax.broadcasted_iota(jnp.int32, (1, tk), 1)

        def qloop(qb, carry):
            qh = _pair_split(_rows(q_ref, qb, tq) * SCALE, m0)
            doh = _pair_split(_rows(do_ref, qb, tq), m0)
            t1v = _rows(t1_ref, qb, tq)
            tot = (t1v[:, 0:1], t1v[:, 1:2])
            tpos = qb * tq + ti

            def step(kbs, c, masked):
                pre = []
                for h in range(2):
                    for kb in kbs:
                        kk = _rows(kb_sc, kb, tk)
                        z = _dot_nt(qh[h], kk)
                        lb, l1 = _logsig2(z)
                        past = None
                        if masked:
                            past = (kb * tk + si) < tpos
                            l1 = jnp.where(past, l1, 0.0)
                        sig = jnp.exp(lb)
                        pre.append((lb, sig, _cumsum_mm(l1, tri_in), _dot_nt(doh[h], _rows(vb_sc, kb, tk)), past, kk))
                out = []
                for h in range(2):
                    dq, run1, rung = c[h]
                    for n, kb in enumerate(kbs):
                        lb, sig, p1, dw, past, kk = pre[2 * h + n]
                        w = jnp.exp(lb + (tot[h] - (run1 + p1)))
                        if masked:
                            w = jnp.where(past, w, 0.0)
                        gm = w * dw
                        cx = _cumsum_mm(gm, tri_ex, parts=2)
                        dz = gm - (gm + (rung + cx)) * sig
                        if masked:
                            dz = jnp.where(past, dz, 0.0)
                        dz = dz.astype(BF16)
                        r = pl.ds(pl.multiple_of(kb * tk, tk), tk)
                        dv_sc[r, :] += _dot_tn(w.astype(BF16), doh[h])
                        dk_sc[r, :] += _dot_tn(dz, qh[h])
                        dq = dq + _dot(dz, kk)
                        run1 = run1 + p1[:, tk - 1:tk]
                        rung = rung + (cx[:, tk - 1:tk] + gm[:, tk - 1:tk])
                    out.append((dq, run1, rung))
                return tuple(out)

            z1 = jnp.zeros((tq, 1), F32)
            zero = (jnp.zeros((tq, PAIR), F32), z1, z1)
            c = lax.fori_loop(0, qb, lambda i, cc: step((2 * i, 2 * i + 1), cc, False), (zero, zero))
            c = step((2 * qb, 2 * qb + 1), c, True)
            r0 = pl.multiple_of(qb * tq, tq)
            dq_ref[pl.ds(r0, tq), :] = (_pair_select(m0, c[0][0], c[1][0]) * SCALE).astype(BF16)
            return carry

        lax.fori_loop(0, nq, qloop, 0)
        dk_ref[...] = dk_sc[...].astype(BF16)
        dv_ref[...] = dv_sc[...].astype(BF16)

    def col(blk):
        return pl.BlockSpec((S, PAIR), lambda b, p: (b, blk + p))

    sh = jax.ShapeDtypeStruct((T, AW), BF16)
    return pl.pallas_call(
        body, name=f"sb_bwd_{l}",
        grid=(T // S, 2),
        in_specs=[col(SBQ_BLK), col(SBK_BLK), col(SBV_BLK), col(0), col(0)],
        out_specs=[col(0), col(0), col(0)],
        out_shape=[sh, sh, sh],
        scratch_shapes=[pltpu.VMEM((S, PAIR), BF16), pltpu.VMEM((S, PAIR), BF16),
                        pltpu.VMEM((S, PAIR), F32), pltpu.VMEM((S, PAIR), F32)],
        compiler_params=_cp(("arbitrary", "arbitrary"), VMEM_BIG),
    )(proj, proj, proj, do, t1)


def _foxp_fwd(proj, cum, ck, gqk2, l, S):
    T = proj.shape[0]
    tq, tk = TQ_(S), TK_(S)
    assert tq == 2 * tk
    nq, nk = S // tq, S // tk

    def body(q_ref, k_ref, v_ref, cum_ref, ck_ref, g_ref, o_ref, nl_ref, fk_sc, vb_sc):
        lane, m0 = _lane_masks()
        p = pl.program_id(1)
        kn, _ = _pair_rms(k_ref[...], m0)
        fk_sc[...] = (kn * g_ref[1:2, :]).astype(BF16)
        vb_sc[...] = v_ref[...].astype(BF16)
        ti = lax.broadcasted_iota(jnp.int32, (tq, 1), 0)
        si = lax.broadcasted_iota(jnp.int32, (1, tk), 1)

        def qloop(qb, carry):
            qn, _ = _pair_rms(_rows(q_ref, qb, tq), m0)
            fqh = _pair_split(qn * (g_ref[0:1, :] * SCALE), m0)
            cumv = _rows(cum_ref, qb, tq)
            cq = [_rowsum(jnp.where(lane == 2 * p + h, cumv, 0.0)) for h in range(2)]
            tpos = qb * tq + ti

            def step(kbs, c, masked):
                out = []
                for h in range(2):
                    m, lsum, acc = c[h]
                    ss = []
                    for kb in kbs:
                        s = _dot_nt(fqh[h], _rows(fk_sc, kb, tk)) + (cq[h] - ck_ref[h, kb])
                        if masked:
                            s = jnp.where((kb * tk + si) <= tpos, s, NEG)
                        ss.append(s)
                    m2 = jnp.maximum(m, jnp.maximum(jnp.max(ss[0], axis=1, keepdims=True),
                                                    jnp.max(ss[1], axis=1, keepdims=True)))
                    al = jnp.exp(m - m2)
                    lsum = al * lsum
                    acc = al * acc
                    for s, kb in zip(ss, kbs):
                        pr = jnp.exp(s - m2)
                        lsum = lsum + _rowsum(pr)
                        acc = acc + _dot(pr.astype(BF16), _rows(vb_sc, kb, tk))
                    out.append((m2, lsum, acc))
                return tuple(out)

            zero = (jnp.full((tq, 1), NEG, F32), jnp.zeros((tq, 1), F32), jnp.zeros((tq, PAIR), F32))
            c = lax.fori_loop(0, qb, lambda i, cc: step((2 * i, 2 * i + 1), cc, False), (zero, zero))
            c = step((2 * qb, 2 * qb + 1), c, True)
            r0 = pl.multiple_of(qb * tq, tq)
            o_ref[pl.ds(r0, tq), :] = _pair_select(m0, c[0][2] / c[0][1], c[1][2] / c[1][1])
            nl = [cq[h] - (c[h][0] + jnp.log(c[h][1])) for h in range(2)]
            nl_ref[pl.ds(r0, tq), :] = jnp.where(lane == 0, nl[0], jnp.where(lane == 1, nl[1], 0.0))
            return carry

        lax.fori_loop(0, nq, qloop, 0)

    def col(blk):
        return pl.BlockSpec((S, PAIR), lambda b, p: (b, blk + p))

    return pl.pallas_call(
        body, name=f"fox_fwd_{l}",
        grid=(T // S, 2),
        in_specs=[col(FXQ_BLK), col(FXK_BLK), col(FXV_BLK),
                  pl.BlockSpec((S, 128), lambda b, p: (b, 0)),
                  pl.BlockSpec((None, 2, nk, 1, tk), lambda b, p: (b, p, 0, 0, 0)),
                  pl.BlockSpec((None, 8, PAIR), lambda b, p: (l, 0, 0))],
        out_specs=[col(0), col(0)],
        out_shape=[jax.ShapeDtypeStruct((T, AW), F32), jax.ShapeDtypeStruct((T, AW), F32)],
        scratch_shapes=[pltpu.VMEM((S, PAIR), BF16), pltpu.VMEM((S, PAIR), BF16)],
        compiler_params=_cp(("arbitrary", "arbitrary"), VMEM_BIG),
    )(proj, proj, proj, cum, ck, gqk2)


def _foxp_bwd(proj, do, nl, ck, gqk2, l, S):
    T = proj.shape[0]
    tq, tk = TQ_(S), TK_(S)
    assert tq == 2 * tk
    nq, nk = S // tq, S // tk

    def body(q_ref, k_ref, v_ref, do_ref, nl_ref, ck_ref, g_ref,
             dq_ref, dk_ref, dv_ref, dc_ref, wacc_ref, fk_sc, vb_sc, dfk_sc, dv_sc):
        @pl.when((pl.program_id(0) == 0) & (pl.program_id(1) == 0))
        def _():
            wacc_ref[...] = jnp.zeros_like(wacc_ref)

        _, m0 = _lane_masks()
        g0 = g_ref[0:1, :]
        g1 = g_ref[1:2, :]
        kn, kr = _pair_rms(k_ref[...], m0)
        fk_sc[...] = (kn * g1).astype(BF16)
        vb_sc[...] = v_ref[...].astype(BF16)
        dfk_sc[...] = jnp.zeros_like(dfk_sc)
        dv_sc[...] = jnp.zeros_like(dv_sc)
        dc_ref[...] = jnp.zeros_like(dc_ref)
        ti = lax.broadcasted_iota(jnp.int32, (tq, 1), 0)
        si = lax.broadcasted_iota(jnp.int32, (1, tk), 1)

        def qloop(qb, carry):
            qn, qr = _pair_rms(_rows(q_ref, qb, tq), m0)
            fqh = _pair_split(qn * (g0 * SCALE), m0)
            doh = _pair_split(_rows(do_ref, qb, tq), m0)
            nlv = _rows(nl_ref, qb, tq)
            cql = (nlv[:, 0:1], nlv[:, 1:2])
            tpos = qb * tq + ti

            def probs(h, kb, masked):
                s = _dot_nt(fqh[h], _rows(fk_sc, kb, tk)) + (cql[h] - ck_ref[h, kb])
                pr = jnp.exp(s)
                if masked:
                    pr = jnp.where((kb * tk + si) <= tpos, pr, 0.0)
                return pr, _dot_nt(doh[h], _rows(vb_sc, kb, tk))

            def dstep(kbs, c, masked):
                out = []
                for h in range(2):
                    acc = c[h]
                    for kb in kbs:
                        pr, dp = probs(h, kb, masked)
                        acc = acc + _rowsum(pr * dp)
                    out.append(acc)
                return tuple(out)

            z1 = jnp.zeros((tq, 1), F32)
            dlt = lax.fori_loop(0, qb, lambda i, cc: dstep((2 * i, 2 * i + 1), cc, False), (z1, z1))
            dlt = dstep((2 * qb, 2 * qb + 1), dlt, True)

            def step(kbs, c, masked):
                out = []
                for h in range(2):
                    dfq = c[h]
                    for kb in kbs:
                        pr, dp = probs(h, kb, masked)
                        ds = pr * (dp - dlt[h])
                        dsb = ds.astype(BF16)
                        r = pl.ds(pl.multiple_of(kb * tk, tk), tk)
                        dv_sc[r, :] += _dot_tn(pr.astype(BF16), doh[h])
                        dfk_sc[r, :] += _dot_tn(dsb, fqh[h])
                        dc_ref[h, kb] += jnp.broadcast_to(-_colsum(ds), (8, tk))
                        dfq = dfq + _dot(dsb, _rows(fk_sc, kb, tk))
                    out.append(dfq)
                return tuple(out)

            zq = jnp.zeros((tq, PAIR), F32)
            c = lax.fori_loop(0, qb, lambda i, cc: step((2 * i, 2 * i + 1), cc, False), (zq, zq))
            c = step((2 * qb, 2 * qb + 1), c, True)
            dfq = _pair_select(m0, c[0], c[1]) * SCALE
            wacc_ref[0:1, :] += _colsum(dfq * qn)
            r0 = pl.multiple_of(qb * tq, tq)
            dq_ref[pl.ds(r0, tq), :] = _pair_rms_bwd(qn, qr, dfq * g0, m0).astype(BF16)
            return carry

        lax.fori_loop(0, nq, qloop, 0)
        dfk = dfk_sc[...]
        wacc_ref[1:2, :] += _colsum(dfk * kn)
        dk_ref[...] = _pair_rms_bwd(kn, kr, dfk * g1, m0).astype(BF16)
        dv_ref[...] = dv_sc[...].astype(BF16)

    def col(blk):
        return pl.BlockSpec((S, PAIR), lambda b, p: (b, blk + p))

    sh = jax.ShapeDtypeStruct((T, AW), BF16)
    return pl.pallas_call(
        body, name=f"fox_bwd_{l}",
        grid=(T // S, 2),
        in_specs=[col(FXQ_BLK), col(FXK_BLK), col(FXV_BLK), col(0), col(0),
                  pl.BlockSpec((None, 2, nk, 1, tk), lambda b, p: (b, p, 0, 0, 0)),
                  pl.BlockSpec((None, 8, PAIR), lambda b, p: (l, 0, 0))],
        out_specs=[col(0), col(0), col(0),
                   pl.BlockSpec((None, 2, nk, 8, tk), lambda b, p: (b, p, 0, 0, 0)),
                   pl.BlockSpec((8, PAIR), lambda b, p: (0, 0))],
        out_shape=[sh, sh, sh,
                   jax.ShapeDtypeStruct((T // S, NH, nk, 8, tk), F32),
                   jax.ShapeDtypeStruct((8, PAIR), F32)],
        scratch_shapes=[pltpu.VMEM((S, PAIR), BF16), pltpu.VMEM((S, PAIR), BF16),
                        pltpu.VMEM((S, PAIR), F32), pltpu.VMEM((S, PAIR), F32)],
        compiler_params=_cp(("arbitrary", "arbitrary"), VMEM_BIG),
    )(proj, proj, proj, do, nl, ck, gqk2)


def _ada_fwd(c_all, w_ada, b_cols):
    nb, ncol = c_all.shape[0], w_ada.shape[2]
    tn = _tile(ncol, 768)

    def body(c_ref, w_ref, b_ref, o_ref):
        c = c_ref[...]
        ca = (c * _sigmoid(c)).astype(BF16)
        o_ref[...] = _dot(ca, w_ref[...].astype(BF16)) + b_ref[...]

    return pl.pallas_call(
        body, name="ada_fwd",
        grid=(2, ncol // tn),
        in_specs=[pl.BlockSpec((nb, D), lambda l, n: (0, 0)),
                  pl.BlockSpec((None, D, tn), lambda l, n: (l, 0, n)),
                  pl.BlockSpec((None, 1, tn), lambda l, n: (l, 0, n))],
        out_specs=pl.BlockSpec((None, nb, tn), lambda l, n: (l, 0, n)),
        out_shape=jax.ShapeDtypeStruct((2, nb, ncol), F32),
        compiler_params=_cp(("arbitrary", "arbitrary")),
    )(c_all, w_ada, b_cols)


def _ada_bwd(c_all, dmod_cols):
    nb, ncol = c_all.shape[0], dmod_cols.shape[2]
    tn = _tile(ncol, 768)

    def body(c_ref, d_ref, o_ref):
        c = c_ref[...]
        ca = (c * _sigmoid(c)).astype(BF16)
        o_ref[...] = _dot_tn(ca, d_ref[...].astype(BF16))

    return pl.pallas_call(
        body, name="ada_bwd",
        grid=(2, ncol // tn),
        in_specs=[pl.BlockSpec((nb, D), lambda l, n: (0, 0)),
                  pl.BlockSpec((None, nb, tn), lambda l, n: (l, 0, n))],
        out_specs=pl.BlockSpec((None, D, tn), lambda l, n: (l, 0, n)),
        out_shape=jax.ShapeDtypeStruct((2, D, ncol), F32),
        compiler_params=_cp(("arbitrary", "arbitrary")),
    )(c_all, dmod_cols)


def _sum_lead(a, name):
    n, R, C = a.shape
    tr = _tile_div8(R, 256)

    def body(a_ref, o_ref):
        acc = a_ref[0]
        for i in range(1, n):
            acc = acc + a_ref[i]
        o_ref[...] = acc

    return pl.pallas_call(
        body, name=name,
        grid=(R // tr,),
        in_specs=[pl.BlockSpec((n, tr, C), lambda i: (0, i, 0))],
        out_specs=pl.BlockSpec((tr, C), lambda i: (i, 0)),
        out_shape=jax.ShapeDtypeStruct((R, C), F32),
        compiler_params=_cp(("arbitrary",)),
    )(a)


def _adamw(w, g, m, v, name):
    R, C = w.shape
    tr = _tile_div8(R, max(8, (1 << 18) // C))
    c1 = 1.0 / (1.0 - ADAM_B1 ** ADAM_STEP)
    c2 = 1.0 / (1.0 - ADAM_B2 ** ADAM_STEP)

    def body(w_ref, g_ref, m_ref, v_ref, d_ref, mo_ref, vo_ref):
        gg = g_ref[...]
        mn = ADAM_B1 * m_ref[...] + (1.0 - ADAM_B1) * gg
        vn = ADAM_B2 * v_ref[...] + (1.0 - ADAM_B2) * (gg * gg)
        mo_ref[...] = mn
        vo_ref[...] = vn
        d_ref[...] = (-ADAM_LR) * ((mn * c1) / (jnp.sqrt(vn * c2) + ADAM_EPS) + ADAM_WD * w_ref[...])

    spec = pl.BlockSpec((tr, C), lambda i: (i, 0))
    sh = jax.ShapeDtypeStruct((R, C), F32)
    return pl.pallas_call(
        body, name=name, grid=(R // tr,),
        in_specs=[spec] * 4, out_specs=[spec] * 3, out_shape=[sh] * 3,
        compiler_params=_cp(("arbitrary",)),
    )(w, g, m, v)


def _coords():
    return lax.axis_index("x"), lax.axis_index("y"), lax.axis_index("c")


def _all_gather8(blk, name, vmem):
    m_per, n = blk.shape
    space = pltpu.VMEM if vmem else pl.ANY

    def body(x_ref, out_ref, send_sems, recv_sems, local_sem):
        x, y, c = _coords()
        me, sibling = (x, y, c), (x, y, 1 - c)
        chips = [(1 - x, y), (x, 1 - y), (1 - x, 1 - y)]

        def rows(px, py, pc):
            return out_ref.at[4 * px + 2 * py + pc]

        def copy(k, block, to, src=None):
            return pltpu.make_async_remote_copy(
                src_ref=rows(*block) if src is None else src, dst_ref=rows(*block),
                send_sem=send_sems.at[k], recv_sem=recv_sems.at[k], device_id=to, device_id_type=MESH)

        mine = pltpu.make_async_copy(x_ref, rows(*me), local_sem)
        mine.start()
        first = [copy(0, me, sibling, src=x_ref)]
        first += [copy(1 + j, me, (*chip, c), src=x_ref) for j, chip in enumerate(chips)]
        for cp in first:
            cp.start()
        passed = [copy(4 + j, (*chip, c), sibling) for j, chip in enumerate(chips)]
        for j, chip in enumerate(chips):
            copy(1 + j, (*chip, c), me).wait_recv()
            passed[j].start()
        copy(0, sibling, me).wait_recv()
        for j, chip in enumerate(chips):
            copy(4 + j, (*chip, 1 - c), me).wait_recv()
        for cp in first + passed:
            cp.wait_send()
        mine.wait()

    return pl.pallas_call(
        body, name=name,
        out_shape=jax.ShapeDtypeStruct((N_DEV, m_per, n), blk.dtype),
        in_specs=[pl.BlockSpec(memory_space=space)],
        out_specs=pl.BlockSpec(memory_space=space),
        scratch_shapes=[pltpu.SemaphoreType.DMA((7,)), pltpu.SemaphoreType.DMA((7,)), pltpu.SemaphoreType.DMA],
        compiler_params=pltpu.CompilerParams(vmem_limit_bytes=VMEM_BIG if vmem else None),
    )(blk)


def _ag_weights(up_b, dn_b, in_b, out_b):
    _, nj, dn_rows, _ = dn_b.shape
    out_rows = out_b.shape[1]
    n_piece = 3 + nj

    def body(up_ref, dn_ref, in_ref, out_ref, gup_ref, gdn_ref, gin_ref, gout_ref, send_sems, recv_sems, local_sems):
        x, y, c = _coords()
        me, sibling = (x, y, c), (x, y, 1 - c)
        chips = [(1 - x, y), (x, 1 - y), (1 - x, 1 - y)]

        def dsts(px, py, pc):
            s = 2 * px + py
            return ([gup_ref.at[s, pc], gin_ref.at[s, pc], gout_ref.at[pc, pl.ds(s * out_rows, out_rows)]]
                    + [gdn_ref.at[pc, j, pl.ds(s * dn_rows, dn_rows)] for j in range(nj)])

        srcs = [up_ref.at[c], in_ref.at[c], out_ref.at[c]] + [dn_ref.at[c, j] for j in range(nj)]

        def copies(k, block, to, own=False):
            d = dsts(*block)
            return [pltpu.make_async_remote_copy(
                src_ref=srcs[p] if own else d[p], dst_ref=d[p], send_sem=send_sems.at[k, p],
                recv_sem=recv_sems.at[k, p], device_id=to, device_id_type=MESH) for p in range(n_piece)]

        mine = [pltpu.make_async_copy(srcs[p], d, local_sems.at[p]) for p, d in enumerate(dsts(*me))]
        for cp in mine:
            cp.start()
        first = copies(0, me, sibling, own=True)
        for j, chip in enumerate(chips):
            first += copies(1 + j, me, (*chip, c), own=True)
        for cp in first:
            cp.start()
        passed = []
        for j, chip in enumerate(chips):
            for cp in copies(1 + j, (*chip, c), me):
                cp.wait_recv()
            fwd = copies(4 + j, (*chip, c), sibling)
            for cp in fwd:
                cp.start()
            passed += fwd
        for cp in copies(0, sibling, me):
            cp.wait_recv()
        for j, chip in enumerate(chips):
            for cp in copies(4 + j, (*chip, 1 - c), me):
                cp.wait_recv()
        for cp in first + passed:
            cp.wait_send()
        for cp in mine:
            cp.wait()

    nl = up_b.shape[0]
    anyspec = pl.BlockSpec(memory_space=pl.ANY)
    return pl.pallas_call(
        body, name="ag_weights",
        out_shape=[jax.ShapeDtypeStruct((N_SHARD,) + up_b.shape, BF16),
                   jax.ShapeDtypeStruct((nl, nj, N_SHARD * dn_rows, D), BF16),
                   jax.ShapeDtypeStruct((N_SHARD,) + in_b.shape, BF16),
                   jax.ShapeDtypeStruct((nl, N_SHARD * out_rows, D), BF16)],
        in_specs=[anyspec] * 4, out_specs=[anyspec] * 4,
        scratch_shapes=[pltpu.SemaphoreType.DMA((7, n_piece)), pltpu.SemaphoreType.DMA((7, n_piece)),
                        pltpu.SemaphoreType.DMA((n_piece,))],
    )(up_b, dn_b, in_b, out_b)


def _rs_to_sibling(pieces):
    n = len(pieces)

    def body(*refs):
        g, r, (send_sems, recv_sems) = refs[:n], refs[n:2 * n], refs[2 * n:]
        x, y, c = _coords()
        cps = []
        for p in range(n):
            r2 = g[p].shape[1] // 2
            cps.append(pltpu.make_async_remote_copy(
                src_ref=g[p].at[:, pl.ds((1 - c) * r2, r2)], dst_ref=r[p], send_sem=send_sems.at[p],
                recv_sem=recv_sems.at[p], device_id=(x, y, 1 - c), device_id_type=MESH))
        for cp in cps:
            cp.start()
        for cp in cps:
            cp.wait()

    anyspec = pl.BlockSpec(memory_space=pl.ANY)
    return pl.pallas_call(
        body, name="rs_to_sibling",
        out_shape=[jax.ShapeDtypeStruct((N_SHARD, g.shape[1] // 2, g.shape[2]), g.dtype) for g in pieces],
        in_specs=[anyspec] * n, out_specs=[anyspec] * n,
        scratch_shapes=[pltpu.SemaphoreType.DMA((n,)), pltpu.SemaphoreType.DMA((n,))],
    )(*pieces)


def _rs_to_chips(hs):
    n = len(hs)

    def body(*refs):
        h, r, (send_sems, recv_sems) = refs[:n], refs[n:2 * n], refs[2 * n:]
        x, y, c = _coords()
        chips = [(1 - x, y), (x, 1 - y), (1 - x, 1 - y)]
        cps = [pltpu.make_async_remote_copy(
            src_ref=h[p].at[2 * px + py], dst_ref=r[p].at[k], send_sem=send_sems.at[k, p], recv_sem=recv_sems.at[k, p],
            device_id=(px, py, c), device_id_type=MESH) for k, (px, py) in enumerate(chips) for p in range(n)]
        for cp in cps:
            cp.start()
        for cp in cps:
            cp.wait()

    anyspec = pl.BlockSpec(memory_space=pl.ANY)
    return pl.pallas_call(
        body, name="rs_to_chips",
        out_shape=[jax.ShapeDtypeStruct((3,) + h.shape[1:], h.dtype) for h in hs],
        in_specs=[anyspec] * n, out_specs=[anyspec] * n,
        scratch_shapes=[pltpu.SemaphoreType.DMA((3, n)), pltpu.SemaphoreType.DMA((3, n))],
    )(*hs)


def _share_halves(fins, places, out_shapes):
    n, no = len(fins), len(out_shapes)

    def body(*refs):
        f, o, (send_sems, recv_sems, local_sems) = refs[:n], refs[n:n + no], refs[n + no:]
        x, y, c = _coords()

        def dst(p, half):
            oi, lead = places[p]
            r2 = f[p].shape[0]
            return o[oi].at[(*lead, pl.ds(half * r2, r2))]

        mine = [pltpu.make_async_copy(f[p], dst(p, c), local_sems.at[p]) for p in range(n)]
        outs = [pltpu.make_async_remote_copy(
            src_ref=f[p], dst_ref=dst(p, c), send_sem=send_sems.at[p], recv_sem=recv_sems.at[p],
            device_id=(x, y, 1 - c), device_id_type=MESH) for p in range(n)]
        for cp in mine + outs:
            cp.start()
        for p in range(n):
            pltpu.make_async_remote_copy(
                src_ref=f[p], dst_ref=dst(p, 1 - c), send_sem=send_sems.at[p], recv_sem=recv_sems.at[p],
                device_id=(x, y, 1 - c), device_id_type=MESH).wait_recv()
        for cp in outs:
            cp.wait_send()
        for cp in mine:
            cp.wait()

    anyspec = pl.BlockSpec(memory_space=pl.ANY)
    return pl.pallas_call(
        body, name="share_halves",
        out_shape=[jax.ShapeDtypeStruct(s, F32) for s in out_shapes],
        in_specs=[anyspec] * n, out_specs=[anyspec] * no,
        scratch_shapes=[pltpu.SemaphoreType.DMA((n,)), pltpu.SemaphoreType.DMA((n,)), pltpu.SemaphoreType.DMA((n,))],
    )(*fins)


def _add_rows(r2, cols, n_arrays):
    lanes = -(-cols // 128) * 128
    return _tile_div8(r2, max(8, (24 << 20) // (2 * n_arrays * lanes * 4)))


def _add_sibling(pieces, recvs, cidx, name):
    n = len(pieces)
    _, R, C = pieces[0].shape
    r2 = R // 2
    tr = _add_rows(r2, C, 3 * n)
    nt = r2 // tr

    def body(c_ref, *refs):
        for p in range(n):
            refs[2 * n + p][...] = refs[p][...] + refs[n + p][...]

    return pl.pallas_call(
        body, name=name,
        grid_spec=pltpu.PrefetchScalarGridSpec(
            num_scalar_prefetch=1, grid=(N_SHARD, nt),
            in_specs=[pl.BlockSpec((None, tr, C), lambda s, i, c_ref: (s, c_ref[0] * nt + i, 0))] * n
            + [pl.BlockSpec((None, tr, C), lambda s, i, c_ref: (s, i, 0))] * n,
            out_specs=[pl.BlockSpec((None, tr, C), lambda s, i, c_ref: (s, i, 0))] * n),
        out_shape=[jax.ShapeDtypeStruct((N_SHARD, r2, C), F32)] * n,
        compiler_params=_cp(("arbitrary", "arbitrary"), VMEM_BIG),
    )(cidx, *pieces, *recvs)


def _add_chips(hs, r3s, sidx, name):
    n = len(hs)
    _, r2, C = hs[0].shape
    tr = _add_rows(r2, C, 5 * n)

    def body(s_ref, *refs):
        for p in range(n):
            r = refs[n + p]
            refs[2 * n + p][...] = ((refs[p][...] + r[0]) + r[1]) + r[2]

    return pl.pallas_call(
        body, name=name,
        grid_spec=pltpu.PrefetchScalarGridSpec(
            num_scalar_prefetch=1, grid=(r2 // tr,),
            in_specs=[pl.BlockSpec((None, tr, C), lambda i, s_ref: (s_ref[0], i, 0))] * n
            + [pl.BlockSpec((3, tr, C), lambda i, s_ref: (0, i, 0))] * n,
            out_specs=[pl.BlockSpec((tr, C), lambda i, s_ref: (i, 0))] * n),
        out_shape=[jax.ShapeDtypeStruct((r2, C), F32)] * n,
        compiler_params=_cp(("arbitrary",), VMEM_BIG),
    )(sidx, *hs, *r3s)


def _pack_rows(parts, rows, dtype):
    flat = jnp.concatenate([p.reshape(-1).astype(dtype) for p in parts])
    return jnp.pad(flat, (0, rows * ROW - flat.shape[0])).reshape(rows, ROW)


def _unpack(flat, shapes):
    out, off = [], 0
    for sh in shapes:
        n = math.prod(sh)
        out.append(flat[off:off + n].reshape(sh))
        off += n
    return out


def _heads(t, B, S, blk):
    return t.reshape(B, S, NH, HD).transpose(0, 2, 1, 3).reshape(B, NH, S // blk, blk, HD)


def _unheads(t, B, S):
    return t.reshape(B, NH, S, HD).transpose(0, 2, 1, 3).reshape(B * S, AW)


def _block_diag(w):
    eye = jnp.eye(LW // HD, dtype=w.dtype)
    return jnp.einsum("lhij,hg->lhigj", w, eye).reshape(w.shape[0], LW, LW)


def _diag_blocks(w):
    nbk = LW // HD
    w4 = w.reshape(nbk, HD, nbk, HD)
    return jnp.stack([w4[h, :, h, :] for h in range(nbk)])


def _rows8(rows, width):
    z = jnp.zeros((width,), F32)
    return jnp.stack(list(rows) + [z] * (8 - len(rows)))


def kernel(x, c, w_ada, b_ada, g_norm, w_ffn_up, w_ffn_down, w_in, b_fgate, conv_w, conv_b, w_rgate, b_rgate, w_igate, b_igate, lru_lambda, g_qk, g_mix_out, w_out, loss_target, m_w_ada, m_b_ada, m_g_norm, m_w_ffn_up, m_w_ffn_down, m_w_in, m_b_fgate, m_conv_w, m_conv_b, m_w_rgate, m_b_rgate, m_w_igate, m_b_igate, m_lru_lambda, m_g_qk, m_g_mix_out, m_w_out, v_w_ada, v_b_ada, v_g_norm, v_w_ffn_up, v_w_ffn_down, v_w_in, v_b_fgate, v_conv_w, v_conv_b, v_w_rgate, v_b_rgate, v_w_igate, v_b_igate, v_lru_lambda, v_g_qk, v_g_mix_out, v_w_out):
    B, S, _ = x.shape
    T = B * S
    xi, yi, ci = _coords()
    sidx = 2 * xi + yi
    didx = 4 * xi + 2 * yi + ci
    ada_cols = w_ada.shape[2]
    gn_cols = g_norm.shape[2]
    cw_cols = conv_w.shape[2]
    n_all = B * N_DEV

    blk1 = _pack_rows([c, jnp.pad(g_norm.reshape(-1), (0, 2 * ROW - g_norm.size)), conv_w], 8, F32)
    ag1 = _all_gather8(blk1, "ag_small_in", True)
    c_all = ag1[:, 0:B].reshape(n_all, D)
    chip_rows = ag1[0::2]
    g_norm_full = chip_rows[:, 2:4].reshape(N_SHARD, 2 * ROW)[:, :g_norm.size] \
        .reshape(N_SHARD, 2, 3, gn_cols).transpose(1, 2, 0, 3).reshape(2, 3, D)
    conv_w_full = chip_rows[:, 4].reshape(N_SHARD, 2, 4, cw_cols).transpose(1, 2, 0, 3).reshape(2, 4, LW)

    b_cols = lax.dynamic_slice(b_ada, (0, sidx * ada_cols), (2, ada_cols)).reshape(2, 1, ada_cols)
    mod_cols = _ada_fwd(c_all, w_ada, b_cols)
    mrows = (2 * n_all * ada_cols) // ROW
    ag2 = _all_gather8(mod_cols.reshape(mrows, ROW), "ag_mod", True)
    mod_sh = ag2[0::2].reshape(N_SHARD, 2, n_all, ada_cols)
    mod_me = lax.dynamic_slice(mod_sh, (0, 0, didx * B, 0), (N_SHARD, 2, B, ada_cols))
    mod_me = mod_me.transpose(1, 2, 0, 3).reshape(2, B, 3, 3, D)
    zrow = jnp.zeros((B, D), F32)
    mods = [[jnp.stack([mod_me[l, :, j, 0], 1.0 + mod_me[l, :, j, 1], 1.0 + mod_me[l, :, j, 2],
                        jnp.broadcast_to(g_norm_full[l, j], (B, D)), zrow, zrow, zrow, zrow], axis=1)
             for j in range(3)] for l in range(2)]

    wup5, wdn4, g_in, wout = _ag_weights(w_ffn_up.astype(BF16), w_ffn_down.astype(BF16),
                                         w_in.astype(BF16), w_out.astype(BF16))
    win_full = g_in.transpose(1, 2, 0, 3).reshape(2, D, N_IN)
    winp = jnp.pad(win_full, ((0, 0), (0, 0), (0, N_INP - N_IN)))

    wr_d = _block_diag(w_rgate).astype(BF16)
    wi_d = _block_diag(w_igate).astype(BF16)
    cw8 = jnp.pad(conv_w_full, ((0, 0), (0, 4), (0, 0)))
    vp8 = jnp.stack([_rows8([conv_b[l], b_rgate[l], b_igate[l], lru_lambda[l]], LW) for l in range(2)])
    bfp = jnp.pad(b_fgate, ((0, 0), (0, 128 - NH)))[:, None, :] * jnp.ones((1, 8, 1), F32)
    gqk2 = jnp.tile(jnp.pad(g_qk, ((0, 0), (0, 6), (0, 0))), (1, 1, 2))
    gmix8 = jnp.pad(g_mix_out[:, None, :], ((0, 0), (0, 7), (0, 0)))

    x2 = x.reshape(T, D)
    tgt = loss_target.reshape(T, D)

    saved = []
    xc = x2
    for l in range(2):
        sv = {}
        sv["x0"] = xc
        xc, sv["g0"], sv["u0"], sv["f0"] = _ffn_fwd(xc, mods[l][0], wup5, wdn4, l, 0, S)
        sv["x1"] = xc
        sv["h1"], proj = _mix_in_fwd(xc, mods[l][1], winp, l, S)
        sv["proj"] = proj
        sv["ylru"], sv["hl"] = _lru_fwd(proj, cw8, vp8, wr_d, wi_d, l, S)
        sv["osb"], sv["t1"] = _sbp_fwd(proj, l, S)
        cum = _fgate_fwd(proj, bfp, l, S)
        sv["ck"] = cum[:, :NH].reshape(B, S, NH).transpose(0, 2, 1).reshape(B, NH, S // TK_(S), 1, TK_(S))
        sv["ofx"], sv["nl"] = _foxp_fwd(proj, cum, sv["ck"], gqk2, l, S)
        xc, sv["y"], sv["mo"] = _mix_out_fwd(xc, sv["ylru"], sv["osb"], sv["ofx"], mods[l][1], gmix8, wout, l, S)
        sv["x2"] = xc
        xc, sv["g2"], sv["u2"], sv["f2"] = _ffn_fwd(xc, mods[l][2], wup5, wdn4, l, 1, S)
        saved.append(sv)

    dxc, lpart = _loss_head(xc, tgt, S)
    loss = lax.psum(lpart[0, 0], ("x", "y", "c"))

    tf = wup5.shape[-1]
    g_up_l = [[None, None], [None, None]]
    g_dn_l = [[None, None], [None, None]]
    g_in_l, g_out_l = [None, None], [None, None]
    dmods = [[None] * 3 for _ in range(2)]
    small = [dict() for _ in range(2)]

    def ffn_back(l, j, xin, dy, sv, sub):
        dx, dmod, wacc, hb, dfb, ab, dgub = _ffn_bwd(
            xin, dy, mods[l][sub], sv[f"f{sub}"], sv[f"g{sub}"], sv[f"u{sub}"], wup5, wdn4, l, j, S)
        g_up_l[l][j] = _mm_tn(hb, dgub, f"dw_up_{l}_{j}", tnb=tf, split_n=True)
        g_dn_l[l][j] = _mm_tn(ab, dfb, f"dw_dn_{l}_{j}", tma=tf).reshape(N_SHARD, -1, D)
        dmods[l][sub] = dmod
        small[l][f"gn{sub}"] = wacc[0]
        return dx

    for l in (1, 0):
        sv = saved[l]
        dxc = ffn_back(l, 1, sv["x2"], dxc, sv, 2)
        dyl, dsb, dfx, dmo, dmod1, wacc_mo = _mix_out_bwd(
            dxc, sv["ylru"], sv["osb"], sv["ofx"], sv["mo"], mods[l][1], gmix8, wout, l, S)
        small[l]["gmix"] = wacc_mo[0]
        g_out_l[l] = _mm_tn(sv["y"], dmo, f"dw_out_{l}")
        dsq, dsk, dsv = _sbp_bwd(sv["proj"], dsb, sv["t1"], l, S)
        dfq, dfk, dfv, dck, wacc_fx = _foxp_bwd(sv["proj"], dfx, sv["nl"], sv["ck"], gqk2, l, S)
        small[l]["gqk"] = wacc_fx[0:2, :HD] + wacc_fx[0:2, HD:]
        dcum = dck[:, :, :, 0, :].reshape(B, NH, S).transpose(0, 2, 1).reshape(T, NH)
        dff_, wacc_fg = _fgate_bwd(jnp.pad(dcum, ((0, 0), (0, 128 - NH))), sv["proj"], bfp, l, S)
        small[l]["bf"] = wacc_fg[0, :NH]
        dlx, dlg, dpr, dpi, ub, wacc_lru = _lru_bwd(dyl, sv["proj"], sv["hl"], cw8, vp8, wr_d, wi_d, l, S)
        small[l]["lru"] = wacc_lru
        small[l]["wr"] = _diag_blocks(_mm_tn(ub, dpr, f"dw_rgate_{l}"))
        small[l]["wi"] = _diag_blocks(_mm_tn(ub, dpi, f"dw_igate_{l}"))
        dproj = jnp.concatenate(
            [dlx, dlg, dsq, dsk, dsv, dfq, dfk, dfv, dff_], axis=1)
        g_in_l[l] = _mm_tn(sv["h1"], dproj, f"dw_in_{l}", tnb=N_INP // 3)[:, :N_IN]
        dxc, dmod_in, wacc_in = _mix_in_bwd(sv["x1"], dxc, mods[l][1], dproj, winp, l, S)
        dmods[l][1] = dmod_in + dmod1
        small[l]["gn1"] = wacc_in[0]
        dxc = ffn_back(l, 0, sv["x0"], dxc, sv, 0)
    grad_x = dxc.reshape(B, S, D)

    dmod_loc = jnp.stack([jnp.stack([dmods[l][j][:, 0:3, :] for j in range(3)], axis=1) for l in range(2)])
    drows = 2 * B * 9
    blk3 = _pack_rows([dmod_loc], -(-drows // 8) * 8, F32)
    ag3 = _all_gather8(blk3, "ag_dmod", True)
    dmod_all = ag3[:, :drows].reshape(N_DEV, 2, B, 9 * D).transpose(1, 0, 2, 3).reshape(2, n_all, 9 * D)
    dmod_mine = lax.dynamic_slice(dmod_all, (0, 0, sidx * ada_cols), (2, n_all, ada_cols))
    grad_w_ada = _ada_bwd(c_all, dmod_mine)
    dmod_rows = jnp.pad(dmod_all.transpose(1, 0, 2).reshape(n_all, 2 * 9, D), ((0, 0), (0, 6), (0, 0)))
    grad_b_ada = _sum_lead(dmod_rows, "grad_b_ada")[:2 * 9].reshape(2, 9 * D)

    sm_parts = [
        jnp.stack([small[l]["bf"] for l in range(2)]),
        jnp.stack([small[l]["lru"][4] for l in range(2)]),
        jnp.stack([small[l]["wr"] for l in range(2)]),
        jnp.stack([small[l]["lru"][5] for l in range(2)]),
        jnp.stack([small[l]["wi"] for l in range(2)]),
        jnp.stack([small[l]["lru"][6] for l in range(2)]),
        jnp.stack([small[l]["lru"][7] for l in range(2)]),
        jnp.stack([small[l]["gqk"] for l in range(2)]),
        jnp.stack([small[l]["gmix"] for l in range(2)]),
        jnp.stack([jnp.stack([small[l][f"gn{j}"] for j in range(3)]) for l in range(2)]),
        jnp.stack([small[l]["lru"][0:4] for l in range(2)]),
    ]
    sm_shapes = [p.shape for p in sm_parts]
    sm_rows = -(-sum(p.size for p in sm_parts) // (8 * ROW)) * 8
    ag4 = _all_gather8(_pack_rows(sm_parts, sm_rows, F32), "ag_small_grads", True)
    sm_sum = _sum_lead(ag4, "sum_small_grads").reshape(-1)
    (g_bf, g_cb, g_wr, g_br, g_wi, g_bi, g_lam, g_gqk, g_gmix, g_gn_full, g_cw_full) = _unpack(sm_sum, sm_shapes)
    g_gn = lax.dynamic_slice(g_gn_full, (0, 0, sidx * gn_cols), (2, 3, gn_cols))
    g_cw = lax.dynamic_slice(g_cw_full, (0, 0, sidx * cw_cols), (2, 4, cw_cols))

    lj = [(l, j) for l in range(2) for j in range(2)]
    groups = [
        ("up", [g_up_l[l][j] for l, j in lj], 0, lj),
        ("dn", [g_dn_l[l][j] for l, j in lj], 1, lj),
        ("in", [g.reshape(D, N_SHARD, -1).transpose(1, 0, 2) for g in g_in_l], 2, [(0,), (1,)]),
        ("out", [g.reshape(N_SHARD, -1, D) for g in g_out_l], 3, [(0,), (1,)]),
    ]
    pieces = [p for _, ps, _, _ in groups for p in ps]
    places = [(oi, lead) for _, ps, oi, leads in groups for lead in leads]
    cvec = jnp.reshape(ci, (1,)).astype(jnp.int32)
    svec = jnp.reshape(sidx, (1,)).astype(jnp.int32)
    recv_a = _rs_to_sibling(pieces)
    hs, off = [], 0
    for gname, ps, _, _ in groups:
        hs += _add_sibling(ps, recv_a[off:off + len(ps)], cvec, f"rs_add_sibling_{gname}")
        off += len(ps)
    recv_b = _rs_to_chips(hs)
    fins, off = [], 0
    for gname, ps, _, _ in groups:
        fins += _add_chips(hs[off:off + len(ps)], recv_b[off:off + len(ps)], svec, f"rs_add_chips_{gname}")
        off += len(ps)
    gw_up, gw_dn, gw_in, gw_out = _share_halves(
        fins, places, [w_ffn_up.shape, w_ffn_down.shape, w_in.shape, w_out.shape])

    def upd(w, g, m, v, name):
        sh = w.shape
        two = (w.size // sh[-1], sh[-1])
        dlt, mn, vn = _adamw(w.reshape(two), g.reshape(two), m.reshape(two), v.reshape(two), name)
        return dlt.reshape(sh), mn.reshape(sh), vn.reshape(sh)

    big = {
        "w_ada": (w_ada, grad_w_ada, m_w_ada, v_w_ada),
        "w_ffn_up": (w_ffn_up, gw_up, m_w_ffn_up, v_w_ffn_up),
        "w_ffn_down": (w_ffn_down, gw_dn, m_w_ffn_down, v_w_ffn_down),
        "w_in": (w_in, gw_in, m_w_in, v_w_in),
        "w_out": (w_out, gw_out, m_w_out, v_w_out),
    }
    res = {n: (t[1],) + upd(*t, f"adamw_{n}") for n, t in big.items()}

    smalls = {
        "b_ada": (b_ada, grad_b_ada, m_b_ada, v_b_ada),
        "g_norm": (g_norm, g_gn, m_g_norm, v_g_norm),
        "b_fgate": (b_fgate, g_bf, m_b_fgate, v_b_fgate),
        "conv_w": (conv_w, g_cw, m_conv_w, v_conv_w),
        "conv_b": (conv_b, g_cb, m_conv_b, v_conv_b),
        "w_rgate": (w_rgate, g_wr, m_w_rgate, v_w_rgate),
        "b_rgate": (b_rgate, g_br, m_b_rgate, v_b_rgate),
        "w_igate": (w_igate, g_wi, m_w_igate, v_w_igate),
        "b_igate": (b_igate, g_bi, m_b_igate, v_b_igate),
        "lru_lambda": (lru_lambda, g_lam, m_lru_lambda, v_lru_lambda),
        "g_qk": (g_qk, g_gqk, m_g_qk, v_g_qk),
        "g_mix_out": (g_mix_out, g_gmix, m_g_mix_out, v_g_mix_out),
    }
    names = list(smalls)
    shapes = [smalls[n][0].shape for n in names]
    prow = -(-sum(math.prod(s) for s in shapes) // (8 * ROW)) * 8
    packed = [_pack_rows([smalls[n][i].reshape(shapes[k]) for k, n in enumerate(names)], prow, F32) for i in range(4)]
    outs = _adamw(packed[0], packed[1], packed[2], packed[3], "adamw_small")
    un = [_unpack(o.reshape(-1), shapes) for o in outs]
    for k, n in enumerate(names):
        res[n] = (smalls[n][1].reshape(shapes[k]), un[0][k], un[1][k], un[2][k])

    order = ["w_ada", "b_ada", "g_norm", "w_ffn_up", "w_ffn_down", "w_in", "b_fgate", "conv_w", "conv_b",
             "w_rgate", "b_rgate", "w_igate", "b_igate", "lru_lambda", "g_qk", "g_mix_out", "w_out"]
    return (loss, grad_x, *[res[n][0] for n in order], *[res[n][1] for n in order],
            *[res[n][2] for n in order], *[res[n][3] for n in order])


def TQ_(S):
    return min(TQ, S)


def TK_(S):
    return min(TK, S)


def _unpack_shards(wg, shapes):
    out, off = [], 0
    for sh in shapes:
        n = math.prod(sh)
        out.append(wg[:, off:off + n].reshape((N_SHARD,) + tuple(sh)))
        off += n
    return out
```

```python
import math

import jax
import jax.numpy as jnp
from jax import lax
from jax.experimental import pallas as pl
from jax.experimental.pallas import tpu as pltpu

F32 = jnp.float32
BF16 = jnp.bfloat16
MESH = pl.DeviceIdType.MESH

D = 1024
HD = 64
LW = 512
NH = 4
AW = NH * HD
N_IN = 2564
N_INP = 2688
F_BLK = 2560 // 128
EPS = 1e-6
LRU_C = 8.0
SCALE = HD ** -0.5
NEG = -1e30
TQ = 256
TK = 128

ADAM_LR, ADAM_B1, ADAM_B2, ADAM_EPS, ADAM_WD, ADAM_STEP = 0.001, 0.9, 0.999, 1e-08, 0.01, 10

VMEM_BIG = 56 * 1024 * 1024
N_DEV = 8
N_SHARD = 4
ROW = 1024


def _cp(sem, vmem=None):
    return pltpu.CompilerParams(dimension_semantics=sem, vmem_limit_bytes=vmem)


def _dot(a, b):
    return jnp.dot(a, b, preferred_element_type=F32)


def _dot_nt(a, b):
    return lax.dot_general(a, b, (((1,), (1,)), ((), ())), preferred_element_type=F32)


def _dot_tn(a, b):
    return lax.dot_general(a, b, (((0,), (0,)), ((), ())), preferred_element_type=F32)


def _log1p(e):
    small = e * (1.0 - e * (0.5 - e * (1.0 / 3.0 - e * 0.25)))
    return jnp.where(e < 0.01, small, jnp.log(1.0 + e))


def _expm1_neg(x):
    small = x * (1.0 + x * 0.5 * (1.0 + x * (1.0 / 3.0) * (1.0 + x * 0.25 * (1.0 + x * 0.2))))
    return jnp.where(x > -0.05, small, jnp.exp(x) - 1.0)


def _sigmoid(x):
    return 1.0 / (1.0 + jnp.exp(-x))


_GELU_C = math.sqrt(2.0 / math.pi)


def _gelu_and_grad(x):
    x2 = x * x
    th = jnp.tanh(_GELU_C * (x + 0.044715 * x * x2))
    g = 0.5 * x * (1.0 + th)
    dg = 0.5 * (1.0 + th) + 0.5 * x * (1.0 - th * th) * _GELU_C * (1.0 + 3.0 * 0.044715 * x2)
    return g, dg


def _rms_rows(x):
    rstd = lax.rsqrt(jnp.mean(x * x, axis=-1, keepdims=True) + EPS)
    return x * rstd, rstd


def _rms_bwd(xn, rstd, dyn):
    return rstd * (dyn - xn * jnp.mean(dyn * xn, axis=-1, keepdims=True))


def _colsum(x):
    return jnp.sum(x, axis=0, keepdims=True)


def _rowsum(x):
    return jnp.sum(x, axis=1, keepdims=True)


def _split3(x):
    hi = x.astype(BF16)
    r = x - hi.astype(F32)
    mid = r.astype(BF16)
    lo = (r - mid.astype(F32)).astype(BF16)
    return hi, mid, lo


def _cumsum_mm(x, ones_tri, parts=3):
    ps = _split3(x)[:parts]
    acc = _dot(ps[0], ones_tri)
    for p in ps[1:]:
        acc = acc + _dot(p, ones_tri)
    return acc


def _tri(n, kind):
    r = lax.broadcasted_iota(jnp.int32, (n, n), 0)
    c = lax.broadcasted_iota(jnp.int32, (n, n), 1)
    m = {"row_gt_col": r > c, "row_le_col": r <= c, "row_lt_col": r < c}[kind]
    return jnp.where(m, 1.0, 0.0).astype(BF16)


def _normmod(x, mod_ref):
    xn, rstd = _rms_rows(x)
    h = xn * mod_ref[3:4, :] * mod_ref[1:2, :] + mod_ref[0:1, :]
    return h, xn, rstd


def _normmod_bwd(dh, xn, rstd, mod_ref, dmod_ref, wacc_ref):
    gn = mod_ref[3:4, :]
    sc = mod_ref[1:2, :]
    dmod_ref[0:1, :] += _colsum(dh)
    t = _colsum(dh * xn)
    dmod_ref[1:2, :] += t * gn
    wacc_ref[0:1, :] += t * sc
    return _rms_bwd(xn, rstd, dh * (gn * sc))


def _tile(n, want):
    t = min(n, want)
    while n % t:
        t //= 2
    return t


def _tile_div8(n, cap, mult=8):
    best = mult
    for t in range(mult, min(n, cap) + 1, mult):
        if n % t == 0:
            best = t
    assert n % best == 0
    return best


def _ffn_fwd(x, mod, wup5, wdn4, l, j, S):
    T = x.shape[0]
    tf = wup5.shape[-1]
    nk = 2
    tm = _tile(S, 512)
    tpb = S // tm

    def body(x_ref, mod_ref, wg_ref, wu_ref, wd_ref, xo_ref, g_ref, u_ref, f_ref, h_sc, acc_sc):
        k = pl.program_id(1)

        @pl.when(k == 0)
        def _():
            h, _, _ = _normmod(x_ref[...], mod_ref)
            h_sc[...] = h.astype(BF16)
            acc_sc[...] = jnp.zeros_like(acc_sc)

        h = h_sc[...]
        g = _dot(h, wg_ref[...])
        u = _dot(h, wu_ref[...])
        g_ref[...] = g.astype(BF16)
        u_ref[...] = u.astype(BF16)
        a = (g * _sigmoid(g)) * u
        acc_sc[...] += _dot(a.astype(BF16), wd_ref[...])

        @pl.when(k == nk - 1)
        def _():
            f = acc_sc[...]
            f_ref[...] = f.astype(BF16)
            xo_ref[...] = x_ref[...] + (0.5 * mod_ref[2:3, :]) * f

    return pl.pallas_call(
        body, name=f"ffn_fwd_{l}_{j}",
        grid=(T // tm, nk),
        in_specs=[
            pl.BlockSpec((tm, D), lambda i, k: (i, 0)),
            pl.BlockSpec((None, 8, D), lambda i, k: (i // tpb, 0, 0)),
            pl.BlockSpec((None, None, None, D, tf), lambda i, k: (k, l, j, 0, 0)),
            pl.BlockSpec((None, None, None, D, tf), lambda i, k: (nk + k, l, j, 0, 0)),
            pl.BlockSpec((None, None, tf, D), lambda i, k: (l, j, k, 0)),
        ],
        out_specs=[
            pl.BlockSpec((tm, D), lambda i, k: (i, 0)),
            pl.BlockSpec((tm, tf), lambda i, k: (i, k)),
            pl.BlockSpec((tm, tf), lambda i, k: (i, k)),
            pl.BlockSpec((tm, D), lambda i, k: (i, 0)),
        ],
        out_shape=[
            jax.ShapeDtypeStruct((T, D), F32),
            jax.ShapeDtypeStruct((T, nk * tf), BF16),
            jax.ShapeDtypeStruct((T, nk * tf), BF16),
            jax.ShapeDtypeStruct((T, D), BF16),
        ],
        scratch_shapes=[pltpu.VMEM((tm, D), BF16), pltpu.VMEM((tm, D), F32)],
        compiler_params=_cp(("arbitrary", "arbitrary"), VMEM_BIG),
    )(x, mod, wup5, wup5, wdn4)


def _ffn_bwd(x, dy, mod, f, g, u, wup5, wdn4, l, j, S):
    T = x.shape[0]
    tf = wup5.shape[-1]
    nk = 2
    tm = _tile(S, 256)
    tpb = S // tm

    def body(x_ref, dy_ref, mod_ref, f_ref, g_ref, u_ref, wup_ref, wd_ref,
             dx_ref, dmod_ref, wacc_ref, h_ref, df_ref, a_ref, dgu_ref):
        i = pl.program_id(0)

        @pl.when(i == 0)
        def _():
            wacc_ref[...] = jnp.zeros_like(wacc_ref)

        @pl.when(i % tpb == 0)
        def _():
            dmod_ref[...] = jnp.zeros_like(dmod_ref)

        dy_ = dy_ref[...]
        h, xn, rstd = _normmod(x_ref[...], mod_ref)
        h_ref[...] = h.astype(BF16)
        dfb = ((0.5 * mod_ref[2:3, :]) * dy_).astype(BF16)
        df_ref[...] = dfb
        dmod_ref[2:3, :] += _colsum(0.5 * f_ref[...].astype(F32) * dy_)
        dh = None
        for k in range(nk):
            cols = slice(k * tf, (k + 1) * tf)
            da = _dot_nt(dfb, wd_ref[cols, :])
            gg = g_ref[:, cols].astype(F32)
            uu = u_ref[:, cols].astype(F32)
            sig = _sigmoid(gg)
            s = gg * sig
            a_ref[:, cols] = (s * uu).astype(BF16)
            du = (da * s).astype(BF16)
            dg = (da * uu * (sig * (1.0 + gg * (1.0 - sig)))).astype(BF16)
            dgu_ref[0, :, cols] = dg
            dgu_ref[1, :, cols] = du
            part = _dot_nt(dg, wup_ref[k]) + _dot_nt(du, wup_ref[nk + k])
            dh = part if dh is None else dh + part
        dx_ref[...] = dy_ + _normmod_bwd(dh, xn, rstd, mod_ref, dmod_ref, wacc_ref)

    once = pl.Buffered(1)
    return pl.pallas_call(
        body, name=f"ffn_bwd_{l}_{j}",
        grid=(T // tm,),
        in_specs=[
            pl.BlockSpec((tm, D), lambda i: (i, 0)),
            pl.BlockSpec((tm, D), lambda i: (i, 0)),
            pl.BlockSpec((None, 8, D), lambda i: (i // tpb, 0, 0)),
            pl.BlockSpec((tm, D), lambda i: (i, 0)),
            pl.BlockSpec((tm, nk * tf), lambda i: (i, 0)),
            pl.BlockSpec((tm, nk * tf), lambda i: (i, 0)),
            pl.BlockSpec((2 * nk, None, None, D, tf), lambda i: (0, l, j, 0, 0), pipeline_mode=once),
            pl.BlockSpec((None, None, nk * tf, D), lambda i: (l, j, 0, 0), pipeline_mode=once),
        ],
        out_specs=[
            pl.BlockSpec((tm, D), lambda i: (i, 0)),
            pl.BlockSpec((None, 8, D), lambda i: (i // tpb, 0, 0)),
            pl.BlockSpec((8, D), lambda i: (0, 0)),
            pl.BlockSpec((tm, D), lambda i: (i, 0)),
            pl.BlockSpec((tm, D), lambda i: (i, 0)),
            pl.BlockSpec((tm, nk * tf), lambda i: (i, 0)),
            pl.BlockSpec((2, tm, nk * tf), lambda i: (0, i, 0)),
        ],
        out_shape=[
            jax.ShapeDtypeStruct((T, D), F32),
            jax.ShapeDtypeStruct((T // S, 8, D), F32),
            jax.ShapeDtypeStruct((8, D), F32),
            jax.ShapeDtypeStruct((T, D), BF16),
            jax.ShapeDtypeStruct((T, D), BF16),
            jax.ShapeDtypeStruct((T, nk * tf), BF16),
            jax.ShapeDtypeStruct((2, T, nk * tf), BF16),
        ],
        compiler_params=_cp(("arbitrary",), VMEM_BIG),
    )(x, dy, mod, f, g, u, wup5, wdn4)


def _mm_tn(a, b, name, tma=None, tnb=None, split_n=False, with_bf16=False):
    T, M = a.shape
    b3 = b if b.ndim == 3 else b[None]
    nb, _, N = b3.shape
    tma = tma or M
    tnb = tnb or N
    npb = N // tnb
    tt = _tile(T, 1024)
    nt = T // tt

    def body(a_ref, b_ref, o_ref, *ob_ref):
        @pl.when(pl.program_id(2) == 0)
        def _():
            o_ref[...] = jnp.zeros_like(o_ref)

        o_ref[...] += _dot_tn(a_ref[...], b_ref[...])

        if with_bf16:
            @pl.when(pl.program_id(2) == nt - 1)
            def _():
                ob_ref[0][...] = o_ref[...].astype(BF16)

    if split_n:
        shape = (nb * npb, M, tnb)
        out_spec = pl.BlockSpec((None, tma, tnb), lambda m, n, t: (n, m, 0))
    else:
        assert nb == 1
        shape = (M, N)
        out_spec = pl.BlockSpec((tma, tnb), lambda m, n, t: (m, n))
    dts = (F32, BF16) if with_bf16 else (F32,)
    out = pl.pallas_call(
        body, name=name,
        grid=(M // tma, nb * npb, nt),
        in_specs=[pl.BlockSpec((tt, tma), lambda m, n, t: (t, m)),
                  pl.BlockSpec((None, tt, tnb), lambda m, n, t: (n // npb, t, n % npb))],
        out_specs=[out_spec] * len(dts),
        out_shape=[jax.ShapeDtypeStruct(shape, dt) for dt in dts],
        compiler_params=_cp(("arbitrary", "arbitrary", "arbitrary"), VMEM_BIG),
    )(a, b3)
    return tuple(out) if with_bf16 else out[0]


def _mix_in_fwd(x, mod, winp, l, S):
    T = x.shape[0]
    tm = _tile(S, 512)
    tpb = S // tm

    def body(x_ref, mod_ref, w_ref, h_ref, p_ref):
        h, _, _ = _normmod(x_ref[...], mod_ref)
        hb = h.astype(BF16)
        h_ref[...] = hb
        p_ref[...] = _dot(hb, w_ref[...])

    return pl.pallas_call(
        body, name=f"mix_in_fwd_{l}",
        grid=(T // tm,),
        in_specs=[pl.BlockSpec((tm, D), lambda i: (i, 0)),
                  pl.BlockSpec((None, 8, D), lambda i: (i // tpb, 0, 0)),
                  pl.BlockSpec((None, D, N_INP), lambda i: (l, 0, 0))],
        out_specs=[pl.BlockSpec((tm, D), lambda i: (i, 0)),
                   pl.BlockSpec((tm, N_INP), lambda i: (i, 0))],
        out_shape=[jax.ShapeDtypeStruct((T, D), BF16), jax.ShapeDtypeStruct((T, N_INP), F32)],
        compiler_params=_cp(("arbitrary",), VMEM_BIG),
    )(x, mod, winp)


def _mix_in_bwd(x, dres, mod, dproj, winp, l, S):
    T = x.shape[0]
    tm = _tile(S, 512)
    tpb = S // tm

    def body(x_ref, dr_ref, mod_ref, dp_ref, w_ref, dx_ref, dmod_ref, wacc_ref):
        i = pl.program_id(0)

        @pl.when(i == 0)
        def _():
            wacc_ref[...] = jnp.zeros_like(wacc_ref)

        @pl.when(i % tpb == 0)
        def _():
            dmod_ref[...] = jnp.zeros_like(dmod_ref)

        dh = _dot_nt(dp_ref[...], w_ref[...])
        _, xn, rstd = _normmod(x_ref[...], mod_ref)
        dx_ref[...] = dr_ref[...] + _normmod_bwd(dh, xn, rstd, mod_ref, dmod_ref, wacc_ref)

    return pl.pallas_call(
        body, name=f"mix_in_bwd_{l}",
        grid=(T // tm,),
        in_specs=[pl.BlockSpec((tm, D), lambda i: (i, 0)),
                  pl.BlockSpec((tm, D), lambda i: (i, 0)),
                  pl.BlockSpec((None, 8, D), lambda i: (i // tpb, 0, 0)),
                  pl.BlockSpec((tm, N_INP), lambda i: (i, 0)),
                  pl.BlockSpec((None, D, N_INP), lambda i: (l, 0, 0))],
        out_specs=[pl.BlockSpec((tm, D), lambda i: (i, 0)),
                   pl.BlockSpec((None, 8, D), lambda i: (i // tpb, 0, 0)),
                   pl.BlockSpec((8, D), lambda i: (0, 0))],
        out_shape=[jax.ShapeDtypeStruct((T, D), F32),
                   jax.ShapeDtypeStruct((T // S, 8, D), F32),
                   jax.ShapeDtypeStruct((8, D), F32)],
        compiler_params=_cp(("arbitrary",), VMEM_BIG),
    )(x, dres, mod, dproj, winp)


_GROUPS = ((0, LW), (LW, LW + AW), (LW + AW, D))


def _mix_out_fwd(x, ylru, osb, ofox, mod, gmix, wout, l, S):
    T = x.shape[0]
    tm = _tile(S, 512)
    tpb = S // tm

    def body(x_ref, yl_ref, sb_ref, fx_ref, mod_ref, gm_ref, w_ref, xo_ref, y_ref, mo_ref):
        for src, (lo, hi) in zip((yl_ref, sb_ref, fx_ref), _GROUPS):
            vn, _ = _rms_rows(src[...])
            y_ref[:, lo:hi] = (vn * gm_ref[0:1, lo:hi]).astype(BF16)
        mo = _dot(y_ref[...], w_ref[...])
        mo_ref[...] = mo.astype(BF16)
        xo_ref[...] = x_ref[...] + mod_ref[2:3, :] * mo

    return pl.pallas_call(
        body, name=f"mix_out_fwd_{l}",
        grid=(T // tm,),
        in_specs=[pl.BlockSpec((tm, D), lambda i: (i, 0)),
                  pl.BlockSpec((tm, LW), lambda i: (i, 0)),
                  pl.BlockSpec((tm, AW), lambda i: (i, 0)),
                  pl.BlockSpec((tm, AW), lambda i: (i, 0)),
                  pl.BlockSpec((None, 8, D), lambda i: (i // tpb, 0, 0)),
                  pl.BlockSpec((None, 8, D), lambda i: (l, 0, 0)),
                  pl.BlockSpec((None, D, D), lambda i: (l, 0, 0))],
        out_specs=[pl.BlockSpec((tm, D), lambda i: (i, 0)),
                   pl.BlockSpec((tm, D), lambda i: (i, 0)),
                   pl.BlockSpec((tm, D), lambda i: (i, 0))],
        out_shape=[jax.ShapeDtypeStruct((T, D), F32),
                   jax.ShapeDtypeStruct((T, D), BF16),
                   jax.ShapeDtypeStruct((T, D), BF16)],
        compiler_params=_cp(("arbitrary",), VMEM_BIG),
    )(x, ylru, osb, ofox, mod, gmix, wout)


def _mix_out_bwd(dx2, ylru, osb, ofox, mo, mod, gmix, wout, l, S):
    T = dx2.shape[0]
    tm = _tile(S, 512)
    tpb = S // tm

    def body(dx_ref, yl_ref, sb_ref, fx_ref, mo_ref, mod_ref, gm_ref, w_ref,
             dyl_ref, dsb_ref, dfx_ref, dmo_ref, dmod_ref, wacc_ref):
        i = pl.program_id(0)

        @pl.when(i == 0)
        def _():
            wacc_ref[...] = jnp.zeros_like(wacc_ref)

        @pl.when(i % tpb == 0)
        def _():
            dmod_ref[...] = jnp.zeros_like(dmod_ref)

        dx = dx_ref[...]
        dmod_ref[2:3, :] += _colsum(mo_ref[...].astype(F32) * dx)
        dmo = (mod_ref[2:3, :] * dx).astype(BF16)
        dmo_ref[...] = dmo
        dy = _dot_nt(dmo, w_ref[...])
        for src, dst, (lo, hi) in zip((yl_ref, sb_ref, fx_ref), (dyl_ref, dsb_ref, dfx_ref), _GROUPS):
            vn, rstd = _rms_rows(src[...])
            dyg = dy[:, lo:hi]
            wacc_ref[0:1, lo:hi] += _colsum(dyg * vn)
            dst[...] = _rms_bwd(vn, rstd, dyg * gm_ref[0:1, lo:hi])

    return pl.pallas_call(
        body, name=f"mix_out_bwd_{l}",
        grid=(T // tm,),
        in_specs=[pl.BlockSpec((tm, D), lambda i: (i, 0)),
                  pl.BlockSpec((tm, LW), lambda i: (i, 0)),
                  pl.BlockSpec((tm, AW), lambda i: (i, 0)),
                  pl.BlockSpec((tm, AW), lambda i: (i, 0)),
                  pl.BlockSpec((tm, D), lambda i: (i, 0)),
                  pl.BlockSpec((None, 8, D), lambda i: (i // tpb, 0, 0)),
                  pl.BlockSpec((None, 8, D), lambda i: (l, 0, 0)),
                  pl.BlockSpec((None, D, D), lambda i: (l, 0, 0))],
        out_specs=[pl.BlockSpec((tm, LW), lambda i: (i, 0)),
                   pl.BlockSpec((tm, AW), lambda i: (i, 0)),
                   pl.BlockSpec((tm, AW), lambda i: (i, 0)),
                   pl.BlockSpec((tm, D), lambda i: (i, 0)),
                   pl.BlockSpec((None, 8, D), lambda i: (i // tpb, 0, 0)),
                   pl.BlockSpec((8, D), lambda i: (0, 0))],
        out_shape=[jax.ShapeDtypeStruct((T, LW), F32),
                   jax.ShapeDtypeStruct((T, AW), F32),
                   jax.ShapeDtypeStruct((T, AW), F32),
                   jax.ShapeDtypeStruct((T, D), BF16),
                   jax.ShapeDtypeStruct((T // S, 8, D), F32),
                   jax.ShapeDtypeStruct((8, D), F32)],
        compiler_params=_cp(("arbitrary",), VMEM_BIG),
    )(dx2, ylru, osb, ofox, mo, mod, gmix, wout)


def _loss_head(y, tgt, S):
    T = y.shape[0]
    tm = _tile(S, 512)

    def body(y_ref, t_ref, dy_ref, l_ref):
        @pl.when(pl.program_id(0) == 0)
        def _():
            l_ref[...] = jnp.zeros_like(l_ref)

        d = y_ref[...] - t_ref[...]
        dy_ref[...] = d * (1.0 / D)
        l_ref[...] += (0.5 / D) * _rowsum(_colsum(d * d))

    return pl.pallas_call(
        body, name="loss_head",
        grid=(T // tm,),
        in_specs=[pl.BlockSpec((tm, D), lambda i: (i, 0)), pl.BlockSpec((tm, D), lambda i: (i, 0))],
        out_specs=[pl.BlockSpec((tm, D), lambda i: (i, 0)), pl.BlockSpec((8, 128), lambda i: (0, 0))],
        out_shape=[jax.ShapeDtypeStruct((T, D), F32), jax.ShapeDtypeStruct((8, 128), F32)],
        compiler_params=_cp(("arbitrary",)),
    )(y, tgt)


def _lru_gates(u, vp_ref, wr_ref, wi_ref):
    ub = u.astype(BF16)
    r = _sigmoid(_dot(ub, wr_ref[...]) + vp_ref[1:2, :])
    ig = _sigmoid(_dot(ub, wi_ref[...]) + vp_ref[2:3, :])
    lam = vp_ref[3:4, :]
    sp = jnp.maximum(-lam, 0.0) + _log1p(jnp.exp(-jnp.abs(lam)))
    log_a = (-LRU_C) * r * sp
    a = jnp.exp(log_a)
    mult = jnp.sqrt(-_expm1_neg(2.0 * log_a))
    return ub, r, ig, sp, a, mult


def _conv_taps(x, xp, row, cw_ref):
    xs = [x]
    for d in (1, 2, 3):
        xs.append(jnp.where(row >= d, pltpu.roll(x, d, 0), pltpu.roll(xp, d, 0)))
    u = xs[0] * cw_ref[3:4, :]
    for d in (1, 2, 3):
        u = u + xs[d] * cw_ref[3 - d:4 - d, :]
    return xs, u


def _lru_fwd(proj, cw, vp, wr, wi, l, S):
    T = proj.shape[0]
    ts = _tile(S, 256)
    nb = S // ts

    def body(x_ref, lg_ref, cw_ref, vp_ref, wr_ref, wi_ref, y_ref, h_ref, xp_sc, hc_sc):
        @pl.when(pl.program_id(1) == 0)
        def _():
            xp_sc[...] = jnp.zeros_like(xp_sc)
            hc_sc[...] = jnp.zeros_like(hc_sc)

        row = lax.broadcasted_iota(jnp.int32, (ts, LW), 0)
        x = x_ref[...]
        _, u = _conv_taps(x, xp_sc[...], row, cw_ref)
        u = u + vp_ref[0:1, :]
        xp_sc[...] = x
        _, _, ig, _, a, mult = _lru_gates(u, vp_ref, wr_ref, wi_ref)
        bv = mult * (ig * u)
        av = a
        d = 1
        while d < ts:
            a_s = jnp.where(row >= d, pltpu.roll(av, d, 0), 1.0)
            b_s = jnp.where(row >= d, pltpu.roll(bv, d, 0), 0.0)
            bv = av * b_s + bv
            av = av * a_s
            d *= 2
        h = bv + av * hc_sc[7:8, :]
        hc_sc[...] = h[ts - 8:ts, :]
        h_ref[...] = h
        gl, _ = _gelu_and_grad(lg_ref[...])
        y_ref[...] = h * gl

    return pl.pallas_call(
        body, name=f"lru_fwd_{l}",
        grid=(T // S, nb),
        in_specs=[pl.BlockSpec((ts, LW), lambda b, j: (b * nb + j, 0)),
                  pl.BlockSpec((ts, LW), lambda b, j: (b * nb + j, 1)),
                  pl.BlockSpec((None, 8, LW), lambda b, j: (l, 0, 0)),
                  pl.BlockSpec((None, 8, LW), lambda b, j: (l, 0, 0)),
                  pl.BlockSpec((None, LW, LW), lambda b, j: (l, 0, 0)),
                  pl.BlockSpec((None, LW, LW), lambda b, j: (l, 0, 0))],
        out_specs=[pl.BlockSpec((ts, LW), lambda b, j: (b * nb + j, 0)),
                   pl.BlockSpec((ts, LW), lambda b, j: (b * nb + j, 0))],
        out_shape=[jax.ShapeDtypeStruct((T, LW), F32), jax.ShapeDtypeStruct((T, LW), F32)],
        scratch_shapes=[pltpu.VMEM((ts, LW), F32), pltpu.VMEM((8, LW), F32)],
        compiler_params=_cp(("arbitrary", "arbitrary")),
    )(proj, proj, cw, vp, wr, wi)


def _lru_bwd(dyl, proj, h, cw, vp, wr, wi, l, S):
    T = proj.shape[0]
    ts = _tile(S, 256)
    nb = S // ts

    def body(dy_ref, x_ref, xprev_ref, lg_ref, h_ref, hprev_ref, cw_ref, vp_ref, wr_ref, wi_ref,
             dx_ref, dlg_ref, dpr_ref, dpi_ref, ub_ref, wacc_ref, gc_sc, af_sc, dun_sc):
        b = pl.program_id(0)
        j = pl.program_id(1)
        first = j == nb - 1

        @pl.when((b == 0) & (j == 0))
        def _():
            wacc_ref[...] = jnp.zeros_like(wacc_ref)

        @pl.when(j == 0)
        def _():
            gc_sc[...] = jnp.zeros_like(gc_sc)
            af_sc[...] = jnp.ones_like(af_sc)
            dun_sc[...] = jnp.zeros_like(dun_sc)

        row = lax.broadcasted_iota(jnp.int32, (ts, LW), 0)
        keep = jnp.where(first, 0.0, 1.0)
        x = x_ref[...]
        xs, u = _conv_taps(x, xprev_ref[...] * keep, row, cw_ref)
        u = u + vp_ref[0:1, :]
        ub, r, ig, sp, a, mult = _lru_gates(u, vp_ref, wr_ref, wi_ref)
        ub_ref[...] = ub
        hh = h_ref[...]
        h_m1 = jnp.where(row >= 1, pltpu.roll(hh, 1, 0), pltpu.roll(hprev_ref[...] * keep, 1, 0))
        dy = dy_ref[...]
        gl, dgl = _gelu_and_grad(lg_ref[...])
        dlg_ref[...] = (dy * hh * dgl).astype(BF16)
        bv = dy * gl
        av = jnp.where(row < ts - 1, pltpu.roll(a, ts - 1, 0), af_sc[0:1, :])
        d = 1
        while d < ts:
            a_s = jnp.where(row < ts - d, pltpu.roll(av, ts - d, 0), 1.0)
            b_s = jnp.where(row < ts - d, pltpu.roll(bv, ts - d, 0), 0.0)
            bv = av * b_s + bv
            av = av * a_s
            d *= 2
        gt = bv + av * gc_sc[0:1, :]
        gc_sc[...] = gt[0:8, :]
        af_sc[...] = a[0:8, :]
        da = gt * h_m1
        d_ig = gt * mult * u
        d_mult = gt * ig * u
        du = gt * mult * ig
        dlog_a = da * a - d_mult * (a * a) / mult
        dpre_r = (dlog_a * ((-LRU_C) * sp)) * r * (1.0 - r)
        dpre_i = d_ig * ig * (1.0 - ig)
        lam = vp_ref[3:4, :]
        wacc_ref[7:8, :] += _colsum(dlog_a * r) * (LRU_C * _sigmoid(-lam))
        wacc_ref[5:6, :] += _colsum(dpre_r)
        wacc_ref[6:7, :] += _colsum(dpre_i)
        dprb = dpre_r.astype(BF16)
        dpib = dpre_i.astype(BF16)
        dpr_ref[...] = dprb
        dpi_ref[...] = dpib
        du = du + _dot_nt(dprb, wr_ref[...]) + _dot_nt(dpib, wi_ref[...])
        wacc_ref[4:5, :] += _colsum(du)
        dun = dun_sc[...]
        dx = du * cw_ref[3:4, :]
        wacc_ref[3:4, :] += _colsum(du * xs[0])
        for dd in (1, 2, 3):
            du_s = jnp.where(row < ts - dd, pltpu.roll(du, ts - dd, 0), pltpu.roll(dun, ts - dd, 0))
            dx = dx + du_s * cw_ref[3 - dd:4 - dd, :]
            wacc_ref[3 - dd:4 - dd, :] += _colsum(du * xs[dd])
        dun_sc[...] = du
        dx_ref[...] = dx.astype(BF16)

    def tb(b, j):
        return b * nb + (nb - 1 - j)

    def tbp(b, j):
        return b * nb + jnp.maximum(nb - 2 - j, 0)

    return pl.pallas_call(
        body, name=f"lru_bwd_{l}",
        grid=(T // S, nb),
        in_specs=[pl.BlockSpec((ts, LW), lambda b, j: (tb(b, j), 0)),
                  pl.BlockSpec((ts, LW), lambda b, j: (tb(b, j), 0)),
                  pl.BlockSpec((ts, LW), lambda b, j: (tbp(b, j), 0)),
                  pl.BlockSpec((ts, LW), lambda b, j: (tb(b, j), 1)),
                  pl.BlockSpec((ts, LW), lambda b, j: (tb(b, j), 0)),
                  pl.BlockSpec((ts, LW), lambda b, j: (tbp(b, j), 0)),
                  pl.BlockSpec((None, 8, LW), lambda b, j: (l, 0, 0)),
                  pl.BlockSpec((None, 8, LW), lambda b, j: (l, 0, 0)),
                  pl.BlockSpec((None, LW, LW), lambda b, j: (l, 0, 0)),
                  pl.BlockSpec((None, LW, LW), lambda b, j: (l, 0, 0))],
        out_specs=[pl.BlockSpec((ts, LW), lambda b, j: (tb(b, j), 0)),
                   pl.BlockSpec((ts, LW), lambda b, j: (tb(b, j), 0)),
                   pl.BlockSpec((ts, LW), lambda b, j: (tb(b, j), 0)),
                   pl.BlockSpec((ts, LW), lambda b, j: (tb(b, j), 0)),
                   pl.BlockSpec((ts, LW), lambda b, j: (tb(b, j), 0)),
                   pl.BlockSpec((8, LW), lambda b, j: (0, 0))],
        out_shape=[jax.ShapeDtypeStruct((T, LW), BF16),
                   jax.ShapeDtypeStruct((T, LW), BF16),
                   jax.ShapeDtypeStruct((T, LW), BF16),
                   jax.ShapeDtypeStruct((T, LW), BF16),
                   jax.ShapeDtypeStruct((T, LW), BF16),
                   jax.ShapeDtypeStruct((8, LW), F32)],
        scratch_shapes=[pltpu.VMEM((8, LW), F32), pltpu.VMEM((8, LW), F32), pltpu.VMEM((ts, LW), F32)],
        compiler_params=_cp(("arbitrary", "arbitrary")),
    )(dyl, proj, proj, proj, h, h, cw, vp, wr, wi)


def _fgate_fwd(proj, bfp, l, S):
    T = proj.shape[0]

    def body(x_ref, b_ref, o_ref):
        z = x_ref[...] + b_ref[0:1, :]
        v = jnp.minimum(z, 0.0) - _log1p(jnp.exp(-jnp.abs(z)))
        row = lax.broadcasted_iota(jnp.int32, (S, 128), 0)
        d = 1
        while d < S:
            v = v + jnp.where(row >= d, pltpu.roll(v, d, 0), 0.0)
            d *= 2
        o_ref[...] = v

    return pl.pallas_call(
        body, name=f"fgate_fwd_{l}",
        grid=(T // S,),
        in_specs=[pl.BlockSpec((S, 128), lambda b: (b, F_BLK)),
                  pl.BlockSpec((None, 8, 128), lambda b: (l, 0, 0))],
        out_specs=pl.BlockSpec((S, 128), lambda b: (b, 0)),
        out_shape=jax.ShapeDtypeStruct((T, 128), F32),
        compiler_params=_cp(("arbitrary",)),
    )(proj, bfp)


def _fgate_bwd(dcum, proj, bfp, l, S):
    T = proj.shape[0]

    def body(d_ref, x_ref, b_ref, o_ref, wacc_ref):
        @pl.when(pl.program_id(0) == 0)
        def _():
            wacc_ref[...] = jnp.zeros_like(wacc_ref)

        v = d_ref[...]
        row = lax.broadcasted_iota(jnp.int32, (S, 128), 0)
        d = 1
        while d < S:
            v = v + jnp.where(row < S - d, pltpu.roll(v, S - d, 0), 0.0)
            d *= 2
        z = x_ref[...] + b_ref[0:1, :]
        dz = v * _sigmoid(-z)
        o_ref[...] = dz.astype(BF16)
        wacc_ref[0:1, :] += _colsum(dz)

    return pl.pallas_call(
        body, name=f"fgate_bwd_{l}",
        grid=(T // S,),
        in_specs=[pl.BlockSpec((S, 128), lambda b: (b, 0)),
                  pl.BlockSpec((S, 128), lambda b: (b, F_BLK)),
                  pl.BlockSpec((None, 8, 128), lambda b: (l, 0, 0))],
        out_specs=[pl.BlockSpec((S, 128), lambda b: (b, 0)), pl.BlockSpec((8, 128), lambda b: (0, 0))],
        out_shape=[jax.ShapeDtypeStruct((T, 128), BF16), jax.ShapeDtypeStruct((8, 128), F32)],
        compiler_params=_cp(("arbitrary",)),
    )(dcum, proj, bfp)


def _logsig_parts(z):
    e = jnp.exp(-jnp.abs(z))
    l1p = jnp.log(1.0 + e)
    return e, jnp.minimum(z, 0.0) - l1p, -jnp.maximum(z, 0.0) - l1p


def _sb_fwd(q, k, v, l):
    B, H, nq, tq, _ = q.shape
    nk, tk = k.shape[2], k.shape[3]
    rr = tq // tk

    def body(q_ref, k_ref, v_ref, o_ref, t1_ref):
        tri = _tri(tk, "row_gt_col")
        ti = lax.broadcasted_iota(jnp.int32, (tq, 1), 0)
        si = lax.broadcasted_iota(jnp.int32, (1, tk), 1)

        def qloop(qb, carry):
            qq = q_ref[qb]
            tpos = qb * tq + ti
            nkb = (qb + 1) * rr

            def kloop(i, c):
                acc, run = c
                kb = nkb - 1 - i
                z = _dot_nt(qq, k_ref[kb]) * SCALE
                past = (kb * tk + si) < tpos
                _, lb, l1 = _logsig_parts(z)
                l1m = jnp.where(past, l1, 0.0)
                aft = _cumsum_mm(l1m, tri) + run
                w = jnp.where(past, jnp.exp(lb + aft), 0.0)
                acc = acc + _dot(w.astype(BF16), v_ref[kb])
                return acc, run + _rowsum(l1m)

            acc, run = lax.fori_loop(0, nkb, kloop, (jnp.zeros((tq, HD), F32), jnp.zeros((tq, 1), F32)))
            o_ref[qb] = acc
            t1_ref[qb] = run
            return carry

        lax.fori_loop(0, nq, qloop, 0)

    qs = pl.BlockSpec((None, None, nq, tq, HD), lambda b, h: (b, h, 0, 0, 0))
    ks = pl.BlockSpec((None, None, nk, tk, HD), lambda b, h: (b, h, 0, 0, 0))
    return pl.pallas_call(
        body, name=f"sb_fwd_{l}",
        grid=(B, H),
        in_specs=[qs, ks, ks],
        out_specs=[qs, pl.BlockSpec((None, None, nq, tq, 1), lambda b, h: (b, h, 0, 0, 0))],
        out_shape=[jax.ShapeDtypeStruct((B, H, nq, tq, HD), F32),
                   jax.ShapeDtypeStruct((B, H, nq, tq, 1), F32)],
        compiler_params=_cp(("arbitrary", "arbitrary"), VMEM_BIG),
    )(q, k, v)


def _sb_bwd(q, k, v, do, t1, l):
    B, H, nq, tq, _ = q.shape
    nk, tk = k.shape[2], k.shape[3]
    rr = tq // tk

    def body(q_ref, k_ref, v_ref, do_ref, t1_ref, dq_ref, dk_ref, dv_ref, dk_sc, dv_sc):
        dk_sc[...] = jnp.zeros_like(dk_sc)
        dv_sc[...] = jnp.zeros_like(dv_sc)
        tri_in = _tri(tk, "row_le_col")
        tri_ex = _tri(tk, "row_lt_col")
        ti = lax.broadcasted_iota(jnp.int32, (tq, 1), 0)
        si = lax.broadcasted_iota(jnp.int32, (1, tk), 1)

        def qloop(qb, carry):
            qq = q_ref[qb]
            dob = do_ref[qb].astype(BF16)
            tot = t1_ref[qb]
            tpos = qb * tq + ti
            nkb = (qb + 1) * rr

            def kloop(kb, c):
                dq, run1, rung = c
                kk = k_ref[kb]
                vv = v_ref[kb]
                z = _dot_nt(qq, kk) * SCALE
                past = (kb * tk + si) < tpos
                e, lb, l1 = _logsig_parts(z)
                l1m = jnp.where(past, l1, 0.0)
                aft = tot - (run1 + _cumsum_mm(l1m, tri_in))
                w = jnp.where(past, jnp.exp(lb + aft), 0.0)
                gm = w * _dot_nt(dob, vv)
                cpre = rung + _cumsum_mm(gm, tri_ex, parts=2)
                inv = 1.0 / (1.0 + e)
                sig = jnp.where(z >= 0.0, inv, e * inv)
                dz = jnp.where(past, gm * (1.0 - sig) - cpre * sig, 0.0).astype(BF16)
                dv_sc[kb] += _dot_tn(w.astype(BF16), dob)
                dk_sc[kb] += _dot_tn(dz, qq) * SCALE
                dq = dq + _dot(dz, kk) * SCALE
                return dq, run1 + _rowsum(l1m), rung + _rowsum(gm)

            z1 = jnp.zeros((tq, 1), F32)
            dq, _, _ = lax.fori_loop(0, nkb, kloop, (jnp.zeros((tq, HD), F32), z1, z1))
            dq_ref[qb] = dq.astype(BF16)
            return carry

        lax.fori_loop(0, nq, qloop, 0)
        dk_ref[...] = dk_sc[...].astype(BF16)
        dv_ref[...] = dv_sc[...].astype(BF16)

    qs = pl.BlockSpec((None, None, nq, tq, HD), lambda b, h: (b, h, 0, 0, 0))
    ks = pl.BlockSpec((None, None, nk, tk, HD), lambda b, h: (b, h, 0, 0, 0))
    return pl.pallas_call(
        body, name=f"sb_bwd_{l}",
        grid=(B, H),
        in_specs=[qs, ks, ks, qs, pl.BlockSpec((None, None, nq, tq, 1), lambda b, h: (b, h, 0, 0, 0))],
        out_specs=[qs, ks, ks],
        out_shape=[jax.ShapeDtypeStruct((B, H, nq, tq, HD), BF16),
                   jax.ShapeDtypeStruct((B, H, nk, tk, HD), BF16),
                   jax.ShapeDtypeStruct((B, H, nk, tk, HD), BF16)],
        scratch_shapes=[pltpu.VMEM((nk, tk, HD), F32), pltpu.VMEM((nk, tk, HD), F32)],
        compiler_params=_cp(("arbitrary", "arbitrary"), VMEM_BIG),
    )(q, k, v, do, t1)


def _fox_fwd(q, k, v, cq, ck, gqk, l):
    B, H, nq, tq, _ = q.shape
    nk, tk = k.shape[2], k.shape[3]
    rr = tq // tk

    def body(q_ref, k_ref, v_ref, cq_ref, ck_ref, g_ref, o_ref, lse_ref, fk_sc):
        g0 = g_ref[0:1, :]
        g1 = g_ref[1:2, :]

        def kprep(kb, c):
            kn, _ = _rms_rows(k_ref[kb])
            fk_sc[kb] = (kn * g1).astype(BF16)
            return c

        lax.fori_loop(0, nk, kprep, 0)
        ti = lax.broadcasted_iota(jnp.int32, (tq, 1), 0)
        si = lax.broadcasted_iota(jnp.int32, (1, tk), 1)

        def qloop(qb, carry):
            qn, _ = _rms_rows(q_ref[qb])
            fq = (qn * g0).astype(BF16)
            cqq = cq_ref[qb]
            tpos = qb * tq + ti

            def kloop(kb, c):
                m, lsum, acc = c
                s = _dot_nt(fq, fk_sc[kb]) * SCALE + cqq - ck_ref[kb]
                s = jnp.where((kb * tk + si) <= tpos, s, NEG)
                m2 = jnp.maximum(m, jnp.max(s, axis=1, keepdims=True))
                al = jnp.exp(m - m2)
                p = jnp.exp(s - m2)
                return m2, al * lsum + _rowsum(p), al * acc + _dot(p.astype(BF16), v_ref[kb])

            m, lsum, acc = lax.fori_loop(
                0, (qb + 1) * rr, kloop,
                (jnp.full((tq, 1), NEG, F32), jnp.zeros((tq, 1), F32), jnp.zeros((tq, HD), F32)))
            o_ref[qb] = acc / lsum
            lse_ref[qb] = m + jnp.log(lsum)
            return carry

        lax.fori_loop(0, nq, qloop, 0)

    qs = pl.BlockSpec((None, None, nq, tq, HD), lambda b, h: (b, h, 0, 0, 0))
    ks = pl.BlockSpec((None, None, nk, tk, HD), lambda b, h: (b, h, 0, 0, 0))
    cqs = pl.BlockSpec((None, None, nq, tq, 1), lambda b, h: (b, h, 0, 0, 0))
    cks = pl.BlockSpec((None, None, nk, 1, tk), lambda b, h: (b, h, 0, 0, 0))
    return pl.pallas_call(
        body, name=f"fox_fwd_{l}",
        grid=(B, H),
        in_specs=[qs, ks, ks, cqs, cks, pl.BlockSpec((None, 8, HD), lambda b, h: (l, 0, 0))],
        out_specs=[qs, cqs],
        out_shape=[jax.ShapeDtypeStruct((B, H, nq, tq, HD), F32),
                   jax.ShapeDtypeStruct((B, H, nq, tq, 1), F32)],
        scratch_shapes=[pltpu.VMEM((nk, tk, HD), BF16)],
        compiler_params=_cp(("arbitrary", "arbitrary"), VMEM_BIG),
    )(q, k, v, cq, ck, gqk)


def _fox_bwd(q, k, v, cq, ck, gqk, do, lse, l):
    B, H, nq, tq, _ = q.shape
    nk, tk = k.shape[2], k.shape[3]
    rr = tq // tk

    def body(q_ref, k_ref, v_ref, cq_ref, ck_ref, g_ref, do_ref, lse_ref,
             dq_ref, dk_ref, dv_ref, dc_ref, wacc_ref, fk_sc, dfk_sc, dv_sc):
        @pl.when((pl.program_id(0) == 0) & (pl.program_id(1) == 0))
        def _():
            wacc_ref[...] = jnp.zeros_like(wacc_ref)

        g0 = g_ref[0:1, :]
        g1 = g_ref[1:2, :]
        dfk_sc[...] = jnp.zeros_like(dfk_sc)
        dv_sc[...] = jnp.zeros_like(dv_sc)
        dc_ref[...] = jnp.zeros_like(dc_ref)

        def kprep(kb, c):
            kn, _ = _rms_rows(k_ref[kb])
            fk_sc[kb] = (kn * g1).astype(BF16)
            return c

        lax.fori_loop(0, nk, kprep, 0)
        ti = lax.broadcasted_iota(jnp.int32, (tq, 1), 0)
        si = lax.broadcasted_iota(jnp.int32, (1, tk), 1)

        def qloop(qb, carry):
            qn, qr = _rms_rows(q_ref[qb])
            fq = (qn * g0).astype(BF16)
            cqq = cq_ref[qb]
            lse = lse_ref[qb]
            dob = do_ref[qb].astype(BF16)
            tpos = qb * tq + ti

            def probs(kb):
                s = _dot_nt(fq, fk_sc[kb]) * SCALE + cqq - ck_ref[kb]
                p = jnp.where((kb * tk + si) <= tpos, jnp.exp(s - lse), 0.0)
                return p, _dot_nt(dob, v_ref[kb])

            def dloop(kb, acc):
                p, dp = probs(kb)
                return acc + _rowsum(p * dp)

            dlt = lax.fori_loop(0, (qb + 1) * rr, dloop, jnp.zeros((tq, 1), F32))

            def kloop(kb, dfq):
                fk = fk_sc[kb]
                p, dp = probs(kb)
                ds = p * (dp - dlt)
                dsb = ds.astype(BF16)
                dv_sc[kb] += _dot_tn(p.astype(BF16), dob)
                dfk_sc[kb] += _dot_tn(dsb, fq) * SCALE
                dc_ref[kb] += jnp.broadcast_to(-_colsum(ds), (8, tk))
                return dfq + _dot(dsb, fk) * SCALE

            dfq = lax.fori_loop(0, (qb + 1) * rr, kloop, jnp.zeros((tq, HD), F32))
            wacc_ref[0:1, :] += _colsum(dfq * qn)
            dq_ref[qb] = _rms_bwd(qn, qr, dfq * g0).astype(BF16)
            return carry

        lax.fori_loop(0, nq, qloop, 0)

        def kfin(kb, c):
            kn, kr = _rms_rows(k_ref[kb])
            dfk = dfk_sc[kb]
            wacc_ref[1:2, :] += _colsum(dfk * kn)
            dk_ref[kb] = _rms_bwd(kn, kr, dfk * g1).astype(BF16)
            return c

        lax.fori_loop(0, nk, kfin, 0)
        dv_ref[...] = dv_sc[...].astype(BF16)

    qs = pl.BlockSpec((None, None, nq, tq, HD), lambda b, h: (b, h, 0, 0, 0))
    ks = pl.BlockSpec((None, None, nk, tk, HD), lambda b, h: (b, h, 0, 0, 0))
    cqs = pl.BlockSpec((None, None, nq, tq, 1), lambda b, h: (b, h, 0, 0, 0))
    cks = pl.BlockSpec((None, None, nk, 1, tk), lambda b, h: (b, h, 0, 0, 0))
    return pl.pallas_call(
        body, name=f"fox_bwd_{l}",
        grid=(B, H),
        in_specs=[qs, ks, ks, cqs, cks, pl.BlockSpec((None, 8, HD), lambda b, h: (l, 0, 0)), qs, cqs],
        out_specs=[qs, ks, ks,
                   pl.BlockSpec((None, None, nk, 8, tk), lambda b, h: (b, h, 0, 0, 0)),
                   pl.BlockSpec((8, HD), lambda b, h: (0, 0))],
        out_shape=[jax.ShapeDtypeStruct((B, H, nq, tq, HD), BF16),
                   jax.ShapeDtypeStruct((B, H, nk, tk, HD), BF16),
                   jax.ShapeDtypeStruct((B, H, nk, tk, HD), BF16),
                   jax.ShapeDtypeStruct((B, H, nk, 8, tk), F32),
                   jax.ShapeDtypeStruct((8, HD), F32)],
        scratch_shapes=[pltpu.VMEM((nk, tk, HD), BF16), pltpu.VMEM((nk, tk, HD), F32),
                        pltpu.VMEM((nk, tk, HD), F32)],
        compiler_params=_cp(("arbitrary", "arbitrary"), VMEM_BIG),
    )(q, k, v, cq, ck, gqk, do, lse)


SBQ_BLK, SBK_BLK, SBV_BLK = 8, 10, 12
FXQ_BLK, FXK_BLK, FXV_BLK = 14, 16, 18
PAIR = 2 * HD


def _lane_masks():
    lane = lax.broadcasted_iota(jnp.int32, (1, PAIR), 1)
    return lane, lane < HD


def _pair_select(m0, a0, a1):
    return jnp.where(m0, a0, a1)


def _pair_split(x, m0):
    return jnp.where(m0, x, 0.0).astype(BF16), jnp.where(m0, 0.0, x).astype(BF16)


def _pair_mean(x, m0):
    s0 = _rowsum(jnp.where(m0, x, 0.0))
    s1 = _rowsum(x) - s0
    return jnp.where(m0, s0, s1) * (1.0 / HD)


def _pair_rms(x, m0):
    rstd = lax.rsqrt(_pair_mean(x * x, m0) + EPS)
    return x * rstd, rstd


def _pair_rms_bwd(xn, rstd, dyn, m0):
    return rstd * (dyn - xn * _pair_mean(dyn * xn, m0))


def _logsig2(z):
    l1p = jnp.log(1.0 + jnp.exp(-jnp.abs(z)))
    lb = jnp.minimum(z, 0.0) - l1p
    return lb, lb - z


def _rows(ref, blk, size):
    return ref[pl.ds(pl.multiple_of(blk * size, size), size), :]


def _sbp_fwd(proj, l, S):
    T = proj.shape[0]
    tq, tk = TQ_(S), TK_(S)
    assert tq == 2 * tk
    nq = S // tq

    def body(q_ref, k_ref, v_ref, o_ref, t1_ref, kb_sc, vb_sc):
        kb_sc[...] = k_ref[...].astype(BF16)
        vb_sc[...] = v_ref[...].astype(BF16)
        lane, m0 = _lane_masks()
        tri = _tri(tk, "row_gt_col")
        ti = lax.broadcasted_iota(jnp.int32, (tq, 1), 0)
        si = lax.broadcasted_iota(jnp.int32, (1, tk), 1)

        def qloop(qb, carry):
            qh = _pair_split(_rows(q_ref, qb, tq) * SCALE, m0)
            tpos = qb * tq + ti

            def step(kbs, c, masked):
                pre = []
                for h in range(2):
                    for kb in kbs:
                        z = _dot_nt(qh[h], _rows(kb_sc, kb, tk))
                        lb, l1 = _logsig2(z)
                        past = None
                        if masked:
                            past = (kb * tk + si) < tpos
                            l1 = jnp.where(past, l1, 0.0)
                        pre.append((lb, l1, _cumsum_mm(l1, tri, parts=2), past))
                out = []
                for h in range(2):
                    acc, run = c[h]
                    for n, kb in enumerate(kbs):
                        lb, l1, cs, past = pre[2 * h + n]
                        w = jnp.exp(lb + (cs + run))
                        if masked:
                            w = jnp.where(past, w, 0.0)
                        acc = acc + _dot(w.astype(BF16), _rows(vb_sc, kb, tk))
                        run = run + (cs[:, 0:1] + l1[:, 0:1])
                    out.append((acc, run))
                return tuple(out)

            zero = (jnp.zeros((tq, PAIR), F32), jnp.zeros((tq, 1), F32))
            c = step((2 * qb + 1, 2 * qb), (zero, zero), True)
            c = lax.fori_loop(0, qb, lambda i, cc: step((2 * (qb - i) - 1, 2 * (qb - i) - 2), cc, False), c)
            r0 = pl.multiple_of(qb * tq, tq)
            o_ref[pl.ds(r0, tq), :] = _pair_select(m0, c[0][0], c[1][0])
            t1_ref[pl.ds(r0, tq), :] = jnp.where(lane == 0, c[0][1], jnp.where(lane == 1, c[1][1], 0.0))
            return carry

        lax.fori_loop(0, nq, qloop, 0)

    def col(blk):
        return pl.BlockSpec((S, PAIR), lambda b, p: (b, blk + p))

    return pl.pallas_call(
        body, name=f"sb_fwd_{l}",
        grid=(T // S, 2),
        in_specs=[col(SBQ_BLK), col(SBK_BLK), col(SBV_BLK)],
        out_specs=[col(0), col(0)],
        out_shape=[jax.ShapeDtypeStruct((T, AW), F32), jax.ShapeDtypeStruct((T, AW), F32)],
        scratch_shapes=[pltpu.VMEM((S, PAIR), BF16), pltpu.VMEM((S, PAIR), BF16)],
        compiler_params=_cp(("arbitrary", "arbitrary"), VMEM_BIG),
    )(proj, proj, proj)


def _sbp_bwd(proj, do, t1, l, S):
    T = proj.shape[0]
    tq, tk = TQ_(S), TK_(S)
    assert tq == 2 * tk
    nq = S // tq

    def body(q_ref, k_ref, v_ref, do_ref, t1_ref, dq_ref, dk_ref, dv_ref, kb_sc, vb_sc, dk_sc, dv_sc):
        kb_sc[...] = k_ref[...].astype(BF16)
        vb_sc[...] = v_ref[...].astype(BF16)
        dk_sc[...] = jnp.zeros_like(dk_sc)
        dv_sc[...] = jnp.zeros_like(dv_sc)
        _, m0 = _lane_masks()
        tri_in = _tri(tk, "row_le_col")
        tri_ex = _tri(tk, "row_lt_col")
        ti = lax.broadcasted_iota(jnp.int32, (tq, 1), 0)
        si = lax.broadcasted_iota(jnp.int32, (1, tk), 1)

        def qloop(qb, carry):
            qh = _pair_split(_rows(q_ref, qb, tq) * SCALE, m0)
            doh = _pair_split(_rows(do_ref, qb, tq), m0)
            t1v = _rows(t1_ref, qb, tq)
            tot = (t1v[:, 0:1], t1v[:, 1:2])
            tpos = qb * tq + ti

            def step(kbs, c, masked):
                pre = []
                for h in range(2):
                    for kb in kbs:
                        kk = _rows(kb_sc, kb, tk)
                        z = _dot_nt(qh[h], kk)
                        lb, l1 = _logsig2(z)
                        past = None
                        if masked:
                            past = (kb * tk + si) < tpos
                            l1 = jnp.where(past, l1, 0.0)
                        sig = jnp.exp(lb)
                        pre.append((lb, sig, _cumsum_mm(l1, tri_in), _dot_nt(doh[h], _rows(vb_sc, kb, tk)), past, kk))
                out = []
                for h in range(2):
                    dq, run1, rung = c[h]
                    for n, kb in enumerate(kbs):
                        lb, sig, p1, dw, past, kk = pre[2 * h + n]
                        w = jnp.exp(lb + (tot[h] - (run1 + p1)))
                        if masked:
                            w = jnp.where(past, w, 0.0)
                        gm = w * dw
                        cx = _cumsum_mm(gm, tri_ex, parts=2)
                        dz = gm - (gm + (rung + cx)) * sig
                        if masked:
                            dz = jnp.where(past, dz, 0.0)
                        dz = dz.astype(BF16)
                        r = pl.ds(pl.multiple_of(kb * tk, tk), tk)
                        dv_sc[r, :] += _dot_tn(w.astype(BF16), doh[h])
                        dk_sc[r, :] += _dot_tn(dz, qh[h])
                        dq = dq + _dot(dz, kk)
                        run1 = run1 + p1[:, tk - 1:tk]
                        rung = rung + (cx[:, tk - 1:tk] + gm[:, tk - 1:tk])
                    out.append((dq, run1, rung))
                return tuple(out)

            z1 = jnp.zeros((tq, 1), F32)
            zero = (jnp.zeros((tq, PAIR), F32), z1, z1)
            c = lax.fori_loop(0, qb, lambda i, cc: step((2 * i, 2 * i + 1), cc, False), (zero, zero))
            c = step((2 * qb, 2 * qb + 1), c, True)
            r0 = pl.multiple_of(qb * tq, tq)
            dq_ref[pl.ds(r0, tq), :] = (_pair_select(m0, c[0][0], c[1][0]) * SCALE).astype(BF16)
            return carry

        lax.fori_loop(0, nq, qloop, 0)
        dk_ref[...] = dk_sc[...].astype(BF16)
        dv_ref[...] = dv_sc[...].astype(BF16)

    def col(blk):
        return pl.BlockSpec((S, PAIR), lambda b, p: (b, blk + p))

    sh = jax.ShapeDtypeStruct((T, AW), BF16)
    return pl.pallas_call(
        body, name=f"sb_bwd_{l}",
        grid=(T // S, 2),
        in_specs=[col(SBQ_BLK), col(SBK_BLK), col(SBV_BLK), col(0), col(0)],
        out_specs=[col(0), col(0), col(0)],
        out_shape=[sh, sh, sh],
        scratch_shapes=[pltpu.VMEM((S, PAIR), BF16), pltpu.VMEM((S, PAIR), BF16),
                        pltpu.VMEM((S, PAIR), F32), pltpu.VMEM((S, PAIR), F32)],
        compiler_params=_cp(("arbitrary", "arbitrary"), VMEM_BIG),
    )(proj, proj, proj, do, t1)


def _foxp_fwd(proj, cum, ck, gqk2, l, S):
    T = proj.shape[0]
    tq, tk = TQ_(S), TK_(S)
    assert tq == 2 * tk
    nq, nk = S // tq, S // tk

    def body(q_ref, k_ref, v_ref, cum_ref, ck_ref, g_ref, o_ref, nl_ref, fk_sc, vb_sc):
        lane, m0 = _lane_masks()
        p = pl.program_id(1)
        kn, _ = _pair_rms(k_ref[...], m0)
        fk_sc[...] = (kn * g_ref[1:2, :]).astype(BF16)
        vb_sc[...] = v_ref[...].astype(BF16)
        ti = lax.broadcasted_iota(jnp.int32, (tq, 1), 0)
        si = lax.broadcasted_iota(jnp.int32, (1, tk), 1)

        def qloop(qb, carry):
            qn, _ = _pair_rms(_rows(q_ref, qb, tq), m0)
            fqh = _pair_split(qn * (g_ref[0:1, :] * SCALE), m0)
            cumv = _rows(cum_ref, qb, tq)
            cq = [_rowsum(jnp.where(lane == 2 * p + h, cumv, 0.0)) for h in range(2)]
            tpos = qb * tq + ti

            def step(kbs, c, masked):
                out = []
                for h in range(2):
                    m, lsum, acc = c[h]
                    ss = []
                    for kb in kbs:
                        s = _dot_nt(fqh[h], _rows(fk_sc, kb, tk)) + (cq[h] - ck_ref[h, kb])
                        if masked:
                            s = jnp.where((kb * tk + si) <= tpos, s, NEG)
                        ss.append(s)
                    m2 = jnp.maximum(m, jnp.maximum(jnp.max(ss[0], axis=1, keepdims=True),
                                                    jnp.max(ss[1], axis=1, keepdims=True)))
                    al = jnp.exp(m - m2)
                    lsum = al * lsum
                    acc = al * acc
                    for s, kb in zip(ss, kbs):
                        pr = jnp.exp(s - m2)
                        lsum = lsum + _rowsum(pr)
                        acc = acc + _dot(pr.astype(BF16), _rows(vb_sc, kb, tk))
                    out.append((m2, lsum, acc))
                return tuple(out)

            zero = (jnp.full((tq, 1), NEG, F32), jnp.zeros((tq, 1), F32), jnp.zeros((tq, PAIR), F32))
            c = lax.fori_loop(0, qb, lambda i, cc: step((2 * i, 2 * i + 1), cc, False), (zero, zero))
            c = step((2 * qb, 2 * qb + 1), c, True)
            r0 = pl.multiple_of(qb * tq, tq)
            o_ref[pl.ds(r0, tq), :] = _pair_select(m0, c[0][2] / c[0][1], c[1][2] / c[1][1])
            nl = [cq[h] - (c[h][0] + jnp.log(c[h][1])) for h in range(2)]
            nl_ref[pl.ds(r0, tq), :] = jnp.where(lane == 0, nl[0], jnp.where(lane == 1, nl[1], 0.0))
            return carry

        lax.fori_loop(0, nq, qloop, 0)

    def col(blk):
        return pl.BlockSpec((S, PAIR), lambda b, p: (b, blk + p))

    return pl.pallas_call(
        body, name=f"fox_fwd_{l}",
        grid=(T // S, 2),
        in_specs=[col(FXQ_BLK), col(FXK_BLK), col(FXV_BLK),
                  pl.BlockSpec((S, 128), lambda b, p: (b, 0)),
                  pl.BlockSpec((None, 2, nk, 1, tk), lambda b, p: (b, p, 0, 0, 0)),
                  pl.BlockSpec((None, 8, PAIR), lambda b, p: (l, 0, 0))],
        out_specs=[col(0), col(0)],
        out_shape=[jax.ShapeDtypeStruct((T, AW), F32), jax.ShapeDtypeStruct((T, AW), F32)],
        scratch_shapes=[pltpu.VMEM((S, PAIR), BF16), pltpu.VMEM((S, PAIR), BF16)],
        compiler_params=_cp(("arbitrary", "arbitrary"), VMEM_BIG),
    )(proj, proj, proj, cum, ck, gqk2)


def _foxp_bwd(proj, do, nl, ck, gqk2, l, S):
    T = proj.shape[0]
    tq, tk = TQ_(S), TK_(S)
    assert tq == 2 * tk
    nq, nk = S // tq, S // tk

    def body(q_ref, k_ref, v_ref, do_ref, nl_ref, ck_ref, g_ref,
             dq_ref, dk_ref, dv_ref, dc_ref, wacc_ref, fk_sc, vb_sc, dfk_sc, dv_sc):
        @pl.when((pl.program_id(0) == 0) & (pl.program_id(1) == 0))
        def _():
            wacc_ref[...] = jnp.zeros_like(wacc_ref)

        _, m0 = _lane_masks()
        g0 = g_ref[0:1, :]
        g1 = g_ref[1:2, :]
        kn, kr = _pair_rms(k_ref[...], m0)
        fk_sc[...] = (kn * g1).astype(BF16)
        vb_sc[...] = v_ref[...].astype(BF16)
        dfk_sc[...] = jnp.zeros_like(dfk_sc)
        dv_sc[...] = jnp.zeros_like(dv_sc)
        dc_ref[...] = jnp.zeros_like(dc_ref)
        ti = lax.broadcasted_iota(jnp.int32, (tq, 1), 0)
        si = lax.broadcasted_iota(jnp.int32, (1, tk), 1)

        def qloop(qb, carry):
            qn, qr = _pair_rms(_rows(q_ref, qb, tq), m0)
            fqh = _pair_split(qn * (g0 * SCALE), m0)
            doh = _pair_split(_rows(do_ref, qb, tq), m0)
            nlv = _rows(nl_ref, qb, tq)
            cql = (nlv[:, 0:1], nlv[:, 1:2])
            tpos = qb * tq + ti

            def probs(h, kb, masked):
                s = _dot_nt(fqh[h], _rows(fk_sc, kb, tk)) + (cql[h] - ck_ref[h, kb])
                pr = jnp.exp(s)
                if masked:
                    pr = jnp.where((kb * tk + si) <= tpos, pr, 0.0)
                return pr, _dot_nt(doh[h], _rows(vb_sc, kb, tk))

            def dstep(kbs, c, masked):
                out = []
                for h in range(2):
                    acc = c[h]
                    for kb in kbs:
                        pr, dp = probs(h, kb, masked)
                        acc = acc + _rowsum(pr * dp)
                    out.append(acc)
                return tuple(out)

            z1 = jnp.zeros((tq, 1), F32)
            dlt = lax.fori_loop(0, qb, lambda i, cc: dstep((2 * i, 2 * i + 1), cc, False), (z1, z1))
            dlt = dstep((2 * qb, 2 * qb + 1), dlt, True)

            def step(kbs, c, masked):
                out = []
                for h in range(2):
                    dfq = c[h]
                    for kb in kbs:
                        pr, dp = probs(h, kb, masked)
                        ds = pr * (dp - dlt[h])
                        dsb = ds.astype(BF16)
                        r = pl.ds(pl.multiple_of(kb * tk, tk), tk)
                        dv_sc[r, :] += _dot_tn(pr.astype(BF16), doh[h])
                        dfk_sc[r, :] += _dot_tn(dsb, fqh[h])
                        dc_ref[h, kb] += jnp.broadcast_to(-_colsum(ds), (8, tk))
                        dfq = dfq + _dot(dsb, _rows(fk_sc, kb, tk))
                    out.append(dfq)
                return tuple(out)

            zq = jnp.zeros((tq, PAIR), F32)
            c = lax.fori_loop(0, qb, lambda i, cc: step((2 * i, 2 * i + 1), cc, False), (zq, zq))
            c = step((2 * qb, 2 * qb + 1), c, True)
            dfq = _pair_select(m0, c[0], c[1]) * SCALE
            wacc_ref[0:1, :] += _colsum(dfq * qn)
            r0 = pl.multiple_of(qb * tq, tq)
            dq_ref[pl.ds(r0, tq), :] = _pair_rms_bwd(qn, qr, dfq * g0, m0).astype(BF16)
            return carry

        lax.fori_loop(0, nq, qloop, 0)
        dfk = dfk_sc[...]
        wacc_ref[1:2, :] += _colsum(dfk * kn)
        dk_ref[...] = _pair_rms_bwd(kn, kr, dfk * g1, m0).astype(BF16)
        dv_ref[...] = dv_sc[...].astype(BF16)

    def col(blk):
        return pl.BlockSpec((S, PAIR), lambda b, p: (b, blk + p))

    sh = jax.ShapeDtypeStruct((T, AW), BF16)
    return pl.pallas_call(
        body, name=f"fox_bwd_{l}",
        grid=(T // S, 2),
        in_specs=[col(FXQ_BLK), col(FXK_BLK), col(FXV_BLK), col(0), col(0),
                  pl.BlockSpec((None, 2, nk, 1, tk), lambda b, p: (b, p, 0, 0, 0)),
                  pl.BlockSpec((None, 8, PAIR), lambda b, p: (l, 0, 0))],
        out_specs=[col(0), col(0), col(0),
                   pl.BlockSpec((None, 2, nk, 8, tk), lambda b, p: (b, p, 0, 0, 0)),
                   pl.BlockSpec((8, PAIR), lambda b, p: (0, 0))],
        out_shape=[sh, sh, sh,
                   jax.ShapeDtypeStruct((T // S, NH, nk, 8, tk), F32),
                   jax.ShapeDtypeStruct((8, PAIR), F32)],
        scratch_shapes=[pltpu.VMEM((S, PAIR), BF16), pltpu.VMEM((S, PAIR), BF16),
                        pltpu.VMEM((S, PAIR), F32), pltpu.VMEM((S, PAIR), F32)],
        compiler_params=_cp(("arbitrary", "arbitrary"), VMEM_BIG),
    )(proj, proj, proj, do, nl, ck, gqk2)


def _ada_fwd(c_all, w_ada, b_cols):
    nb, ncol = c_all.shape[0], w_ada.shape[2]
    tn = _tile(ncol, 768)

    def body(c_ref, w_ref, b_ref, o_ref):
        c = c_ref[...]
        ca = (c * _sigmoid(c)).astype(BF16)
        o_ref[...] = _dot(ca, w_ref[...].astype(BF16)) + b_ref[...]

    return pl.pallas_call(
        body, name="ada_fwd",
        grid=(2, ncol // tn),
        in_specs=[pl.BlockSpec((nb, D), lambda l, n: (0, 0)),
                  pl.BlockSpec((None, D, tn), lambda l, n: (l, 0, n)),
                  pl.BlockSpec((None, 1, tn), lambda l, n: (l, 0, n))],
        out_specs=pl.BlockSpec((None, nb, tn), lambda l, n: (l, 0, n)),
        out_shape=jax.ShapeDtypeStruct((2, nb, ncol), F32),
        compiler_params=_cp(("arbitrary", "arbitrary")),
    )(c_all, w_ada, b_cols)


def _ada_bwd(c_all, dmod_cols):
    nb, ncol = c_all.shape[0], dmod_cols.shape[2]
    tn = _tile(ncol, 768)

    def body(c_ref, d_ref, o_ref):
        c = c_ref[...]
        ca = (c * _sigmoid(c)).astype(BF16)
        o_ref[...] = _dot_tn(ca, d_ref[...].astype(BF16))

    return pl.pallas_call(
        body, name="ada_bwd",
        grid=(2, ncol // tn),
        in_specs=[pl.BlockSpec((nb, D), lambda l, n: (0, 0)),
                  pl.BlockSpec((None, nb, tn), lambda l, n: (l, 0, n))],
        out_specs=pl.BlockSpec((None, D, tn), lambda l, n: (l, 0, n)),
        out_shape=jax.ShapeDtypeStruct((2, D, ncol), F32),
        compiler_params=_cp(("arbitrary", "arbitrary")),
    )(c_all, dmod_cols)


def _sum_lead(a, name):
    n, R, C = a.shape
    tr = _tile_div8(R, 256)

    def body(a_ref, o_ref):
        acc = a_ref[0]
        for i in range(1, n):
            acc = acc + a_ref[i]
        o_ref[...] = acc

    return pl.pallas_call(
        body, name=name,
        grid=(R // tr,),
        in_specs=[pl.BlockSpec((n, tr, C), lambda i: (0, i, 0))],
        out_specs=pl.BlockSpec((tr, C), lambda i: (i, 0)),
        out_shape=jax.ShapeDtypeStruct((R, C), F32),
        compiler_params=_cp(("arbitrary",)),
    )(a)


def _adamw(w, g, m, v, name):
    R, C = w.shape
    tr = _tile_div8(R, max(8, (1 << 18) // C))
    c1 = 1.0 / (1.0 - ADAM_B1 ** ADAM_STEP)
    c2 = 1.0 / (1.0 - ADAM_B2 ** ADAM_STEP)

    def body(w_ref, g_ref, m_ref, v_ref, d_ref, mo_ref, vo_ref):
        gg = g_ref[...]
        mn = ADAM_B1 * m_ref[...] + (1.0 - ADAM_B1) * gg
        vn = ADAM_B2 * v_ref[...] + (1.0 - ADAM_B2) * (gg * gg)
        mo_ref[...] = mn
        vo_ref[...] = vn
        d_ref[...] = (-ADAM_LR) * ((mn * c1) / (jnp.sqrt(vn * c2) + ADAM_EPS) + ADAM_WD * w_ref[...])

    spec = pl.BlockSpec((tr, C), lambda i: (i, 0))
    sh = jax.ShapeDtypeStruct((R, C), F32)
    return pl.pallas_call(
        body, name=name, grid=(R // tr,),
        in_specs=[spec] * 4, out_specs=[spec] * 3, out_shape=[sh] * 3,
        compiler_params=_cp(("arbitrary",)),
    )(w, g, m, v)


def _coords():
    return lax.axis_index("x"), lax.axis_index("y"), lax.axis_index("c")


def _all_gather8(blk, name, vmem):
    m_per, n = blk.shape
    space = pltpu.VMEM if vmem else pl.ANY

    def body(x_ref, out_ref, send_sems, recv_sems, local_sem):
        x, y, c = _coords()
        me, sibling = (x, y, c), (x, y, 1 - c)
        chips = [(1 - x, y), (x, 1 - y), (1 - x, 1 - y)]

        def rows(px, py, pc):
            return out_ref.at[4 * px + 2 * py + pc]

        def copy(k, block, to, src=None):
            return pltpu.make_async_remote_copy(
                src_ref=rows(*block) if src is None else src, dst_ref=rows(*block),
                send_sem=send_sems.at[k], recv_sem=recv_sems.at[k], device_id=to, device_id_type=MESH)

        mine = pltpu.make_async_copy(x_ref, rows(*me), local_sem)
        mine.start()
        first = [copy(0, me, sibling, src=x_ref)]
        first += [copy(1 + j, me, (*chip, c), src=x_ref) for j, chip in enumerate(chips)]
        for cp in first:
            cp.start()
        passed = [copy(4 + j, (*chip, c), sibling) for j, chip in enumerate(chips)]
        for j, chip in enumerate(chips):
            copy(1 + j, (*chip, c), me).wait_recv()
            passed[j].start()
        copy(0, sibling, me).wait_recv()
        for j, chip in enumerate(chips):
            copy(4 + j, (*chip, 1 - c), me).wait_recv()
        for cp in first + passed:
            cp.wait_send()
        mine.wait()

    return pl.pallas_call(
        body, name=name,
        out_shape=jax.ShapeDtypeStruct((N_DEV, m_per, n), blk.dtype),
        in_specs=[pl.BlockSpec(memory_space=space)],
        out_specs=pl.BlockSpec(memory_space=space),
        scratch_shapes=[pltpu.SemaphoreType.DMA((7,)), pltpu.SemaphoreType.DMA((7,)), pltpu.SemaphoreType.DMA],
        compiler_params=pltpu.CompilerParams(vmem_limit_bytes=VMEM_BIG if vmem else None),
    )(blk)


def _ag_weights(up_b, dn_b, in_b, out_b):
    _, nj, dn_rows, _ = dn_b.shape
    out_rows = out_b.shape[1]
    n_piece = 3 + nj

    def body(up_ref, dn_ref, in_ref, out_ref, gup_ref, gdn_ref, gin_ref, gout_ref, send_sems, recv_sems, local_sems):
        x, y, c = _coords()
        me, sibling = (x, y, c), (x, y, 1 - c)
        chips = [(1 - x, y), (x, 1 - y), (1 - x, 1 - y)]

        def dsts(px, py, pc):
            s = 2 * px + py
            return ([gup_ref.at[s, pc], gin_ref.at[s, pc], gout_ref.at[pc, pl.ds(s * out_rows, out_rows)]]
                    + [gdn_ref.at[pc, j, pl.ds(s * dn_rows, dn_rows)] for j in range(nj)])

        srcs = [up_ref.at[c], in_ref.at[c], out_ref.at[c]] + [dn_ref.at[c, j] for j in range(nj)]

        def copies(k, block, to, own=False):
            d = dsts(*block)
            return [pltpu.make_async_remote_copy(
                src_ref=srcs[p] if own else d[p], dst_ref=d[p], send_sem=send_sems.at[k, p],
                recv_sem=recv_sems.at[k, p], device_id=to, device_id_type=MESH) for p in range(n_piece)]

        mine = [pltpu.make_async_copy(srcs[p], d, local_sems.at[p]) for p, d in enumerate(dsts(*me))]
        for cp in mine:
            cp.start()
        first = copies(0, me, sibling, own=True)
        for j, chip in enumerate(chips):
            first += copies(1 + j, me, (*chip, c), own=True)
        for cp in first:
            cp.start()
        passed = []
        for j, chip in enumerate(chips):
            for cp in copies(1 + j, (*chip, c), me):
                cp.wait_recv()
            fwd = copies(4 + j, (*chip, c), sibling)
            for cp in fwd:
                cp.start()
            passed += fwd
        for cp in copies(0, sibling, me):
            cp.wait_recv()
        for j, chip in enumerate(chips):
            for cp in copies(4 + j, (*chip, 1 - c), me):
                cp.wait_recv()
        for cp in first + passed:
            cp.wait_send()
        for cp in mine:
            cp.wait()

    nl = up_b.shape[0]
    anyspec = pl.BlockSpec(memory_space=pl.ANY)
    return pl.pallas_call(
        body, name="ag_weights",
        out_shape=[jax.ShapeDtypeStruct((N_SHARD,) + up_b.shape, BF16),
                   jax.ShapeDtypeStruct((nl, nj, N_SHARD * dn_rows, D), BF16),
                   jax.ShapeDtypeStruct((N_SHARD,) + in_b.shape, BF16),
                   jax.ShapeDtypeStruct((nl, N_SHARD * out_rows, D), BF16)],
        in_specs=[anyspec] * 4, out_specs=[anyspec] * 4,
        scratch_shapes=[pltpu.SemaphoreType.DMA((7, n_piece)), pltpu.SemaphoreType.DMA((7, n_piece)),
                        pltpu.SemaphoreType.DMA((n_piece,))],
    )(up_b, dn_b, in_b, out_b)


def _rs_to_sibling(pieces):
    n = len(pieces)

    def body(*refs):
        g, r, (send_sems, recv_sems) = refs[:n], refs[n:2 * n], refs[2 * n:]
        x, y, c = _coords()
        cps = []
        for p in range(n):
            r2 = g[p].shape[1] // 2
            cps.append(pltpu.make_async_remote_copy(
                src_ref=g[p].at[:, pl.ds((1 - c) * r2, r2)], dst_ref=r[p], send_sem=send_sems.at[p],
                recv_sem=recv_sems.at[p], device_id=(x, y, 1 - c), device_id_type=MESH))
        for cp in cps:
            cp.start()
        for cp in cps:
            cp.wait()

    anyspec = pl.BlockSpec(memory_space=pl.ANY)
    return pl.pallas_call(
        body, name="rs_to_sibling",
        out_shape=[jax.ShapeDtypeStruct((N_SHARD, g.shape[1] // 2, g.shape[2]), g.dtype) for g in pieces],
        in_specs=[anyspec] * n, out_specs=[anyspec] * n,
        scratch_shapes=[pltpu.SemaphoreType.DMA((n,)), pltpu.SemaphoreType.DMA((n,))],
    )(*pieces)


def _rs_to_chips(hs):
    n = len(hs)

    def body(*refs):
        h, r, (send_sems, recv_sems) = refs[:n], refs[n:2 * n], refs[2 * n:]
        x, y, c = _coords()
        chips = [(1 - x, y), (x, 1 - y), (1 - x, 1 - y)]
        cps = [pltpu.make_async_remote_copy(
            src_ref=h[p].at[2 * px + py], dst_ref=r[p].at[k], send_sem=send_sems.at[k, p], recv_sem=recv_sems.at[k, p],
            device_id=(px, py, c), device_id_type=MESH) for k, (px, py) in enumerate(chips) for p in range(n)]
        for cp in cps:
            cp.start()
        for cp in cps:
            cp.wait()

    anyspec = pl.BlockSpec(memory_space=pl.ANY)
    return pl.pallas_call(
        body, name="rs_to_chips",
        out_shape=[jax.ShapeDtypeStruct((3,) + h.shape[1:], h.dtype) for h in hs],
        in_specs=[anyspec] * n, out_specs=[anyspec] * n,
        scratch_shapes=[pltpu.SemaphoreType.DMA((3, n)), pltpu.SemaphoreType.DMA((3, n))],
    )(*hs)


def _share_halves(tensors, places, r2s):
    n, no = len(places), len(tensors)

    def body(*refs):
        o, (send_sems, recv_sems) = refs[no:2 * no], refs[2 * no:]
        x, y, c = _coords()

        def half(p, hc):
            oi, lead = places[p]
            return o[oi].at[(*lead, pl.ds(hc * r2s[p], r2s[p]))]

        outs = [pltpu.make_async_remote_copy(
            src_ref=half(p, c), dst_ref=half(p, c), send_sem=send_sems.at[p], recv_sem=recv_sems.at[p],
            device_id=(x, y, 1 - c), device_id_type=MESH) for p in range(n)]
        for cp in outs:
            cp.start()
        for p in range(n):
            pltpu.make_async_remote_copy(
                src_ref=half(p, 1 - c), dst_ref=half(p, 1 - c), send_sem=send_sems.at[p], recv_sem=recv_sems.at[p],
                device_id=(x, y, 1 - c), device_id_type=MESH).wait_recv()
        for cp in outs:
            cp.wait_send()

    anyspec = pl.BlockSpec(memory_space=pl.ANY)
    return pl.pallas_call(
        body, name="share_halves",
        out_shape=[jax.ShapeDtypeStruct(t.shape, t.dtype) for t in tensors],
        in_specs=[anyspec] * no, out_specs=[anyspec] * no,
        input_output_aliases={i: i for i in range(no)},
        scratch_shapes=[pltpu.SemaphoreType.DMA((n,)), pltpu.SemaphoreType.DMA((n,))],
    )(*tensors)


def _add_rows(r2, cols, n_arrays):
    lanes = -(-cols // 128) * 128
    return _tile_div8(r2, max(16, (24 << 20) // (2 * n_arrays * lanes * 4)), mult=16)


def _add_sibling(pieces, recvs, cidx, name):
    n = len(pieces)
    _, R, C = pieces[0].shape
    r2 = R // 2
    tr = _add_rows(r2, C, 2 * n)
    nt = r2 // tr

    def body(c_ref, *refs):
        for p in range(n):
            refs[2 * n + p][...] = (refs[p][...] + refs[n + p][...].astype(F32)).astype(BF16)

    return pl.pallas_call(
        body, name=name,
        grid_spec=pltpu.PrefetchScalarGridSpec(
            num_scalar_prefetch=1, grid=(N_SHARD, nt),
            in_specs=[pl.BlockSpec((None, tr, C), lambda s, i, c_ref: (s, c_ref[0] * nt + i, 0))] * n
            + [pl.BlockSpec((None, tr, C), lambda s, i, c_ref: (s, i, 0))] * n,
            out_specs=[pl.BlockSpec((None, tr, C), lambda s, i, c_ref: (s, i, 0))] * n),
        out_shape=[jax.ShapeDtypeStruct((N_SHARD, r2, C), BF16)] * n,
        compiler_params=_cp(("arbitrary", "arbitrary"), VMEM_BIG),
    )(cidx, *pieces, *recvs)


def _add_chips_into(piece, recv_a, recv_b, sc, prev, shape, lead, name):
    _, R, C = piece.shape
    r2 = R // 2
    tr = _add_rows(r2, C, 4)
    nt = r2 // tr
    nl = len(lead)

    def body(sc_ref, p_ref, a_ref, b_ref, *rest):
        o_ref = rest[-1]
        acc = p_ref[...] + a_ref[...].astype(F32)
        for k in range(3):
            acc = acc + b_ref[k].astype(F32)
        o_ref[...] = acc

    in_specs = [pl.BlockSpec((None, tr, C), lambda i, sc_ref: (sc_ref[0], sc_ref[1] * nt + i, 0)),
                pl.BlockSpec((None, tr, C), lambda i, sc_ref: (sc_ref[0], i, 0)),
                pl.BlockSpec((3, tr, C), lambda i, sc_ref: (0, i, 0))]
    args = [sc, piece, recv_a, recv_b]
    aliases = {}
    if prev is not None:
        in_specs.append(pl.BlockSpec(memory_space=pl.ANY))
        args.append(prev)
        aliases = {4: 0}
    return pl.pallas_call(
        body, name=name,
        grid_spec=pltpu.PrefetchScalarGridSpec(
            num_scalar_prefetch=1, grid=(nt,), in_specs=in_specs,
            out_specs=pl.BlockSpec((None,) * nl + (tr, C), lambda i, sc_ref: (*lead, sc_ref[1] * nt + i, 0))),
        out_shape=jax.ShapeDtypeStruct(shape, F32),
        input_output_aliases=aliases,
        compiler_params=_cp(("arbitrary",), VMEM_BIG),
    )(*args)


def _pack_rows(parts, rows, dtype):
    flat = jnp.concatenate([p.reshape(-1).astype(dtype) for p in parts])
    return jnp.pad(flat, (0, rows * ROW - flat.shape[0])).reshape(rows, ROW)


def _unpack(flat, shapes):
    out, off = [], 0
    for sh in shapes:
        n = math.prod(sh)
        out.append(flat[off:off + n].reshape(sh))
        off += n
    return out


def _heads(t, B, S, blk):
    return t.reshape(B, S, NH, HD).transpose(0, 2, 1, 3).reshape(B, NH, S // blk, blk, HD)


def _unheads(t, B, S):
    return t.reshape(B, NH, S, HD).transpose(0, 2, 1, 3).reshape(B * S, AW)


def _block_diag(w):
    eye = jnp.eye(LW // HD, dtype=w.dtype)
    return jnp.einsum("lhij,hg->lhigj", w, eye).reshape(w.shape[0], LW, LW)


def _diag_blocks(w):
    nbk = LW // HD
    w4 = w.reshape(nbk, HD, nbk, HD)
    return jnp.stack([w4[h, :, h, :] for h in range(nbk)])


def _rows8(rows, width):
    z = jnp.zeros((width,), F32)
    return jnp.stack(list(rows) + [z] * (8 - len(rows)))


def kernel(x, c, w_ada, b_ada, g_norm, w_ffn_up, w_ffn_down, w_in, b_fgate, conv_w, conv_b, w_rgate, b_rgate, w_igate, b_igate, lru_lambda, g_qk, g_mix_out, w_out, loss_target, m_w_ada, m_b_ada, m_g_norm, m_w_ffn_up, m_w_ffn_down, m_w_in, m_b_fgate, m_conv_w, m_conv_b, m_w_rgate, m_b_rgate, m_w_igate, m_b_igate, m_lru_lambda, m_g_qk, m_g_mix_out, m_w_out, v_w_ada, v_b_ada, v_g_norm, v_w_ffn_up, v_w_ffn_down, v_w_in, v_b_fgate, v_conv_w, v_conv_b, v_w_rgate, v_b_rgate, v_w_igate, v_b_igate, v_lru_lambda, v_g_qk, v_g_mix_out, v_w_out):
    B, S, _ = x.shape
    T = B * S
    xi, yi, ci = _coords()
    sidx = 2 * xi + yi
    didx = 4 * xi + 2 * yi + ci
    ada_cols = w_ada.shape[2]
    gn_cols = g_norm.shape[2]
    cw_cols = conv_w.shape[2]
    n_all = B * N_DEV

    blk1 = _pack_rows([c, jnp.pad(g_norm.reshape(-1), (0, 2 * ROW - g_norm.size)), conv_w], 8, F32)
    ag1 = _all_gather8(blk1, "ag_small_in", True)
    c_all = ag1[:, 0:B].reshape(n_all, D)
    chip_rows = ag1[0::2]
    g_norm_full = chip_rows[:, 2:4].reshape(N_SHARD, 2 * ROW)[:, :g_norm.size] \
        .reshape(N_SHARD, 2, 3, gn_cols).transpose(1, 2, 0, 3).reshape(2, 3, D)
    conv_w_full = chip_rows[:, 4].reshape(N_SHARD, 2, 4, cw_cols).transpose(1, 2, 0, 3).reshape(2, 4, LW)

    b_cols = lax.dynamic_slice(b_ada, (0, sidx * ada_cols), (2, ada_cols)).reshape(2, 1, ada_cols)
    mod_cols = _ada_fwd(c_all, w_ada, b_cols)
    mrows = (2 * n_all * ada_cols) // ROW
    ag2 = _all_gather8(mod_cols.reshape(mrows, ROW), "ag_mod", True)
    mod_sh = ag2[0::2].reshape(N_SHARD, 2, n_all, ada_cols)
    mod_me = lax.dynamic_slice(mod_sh, (0, 0, didx * B, 0), (N_SHARD, 2, B, ada_cols))
    mod_me = mod_me.transpose(1, 2, 0, 3).reshape(2, B, 3, 3, D)
    zrow = jnp.zeros((B, D), F32)
    mods = [[jnp.stack([mod_me[l, :, j, 0], 1.0 + mod_me[l, :, j, 1], 1.0 + mod_me[l, :, j, 2],
                        jnp.broadcast_to(g_norm_full[l, j], (B, D)), zrow, zrow, zrow, zrow], axis=1)
             for j in range(3)] for l in range(2)]

    wup5, wdn4, g_in, wout = _ag_weights(w_ffn_up.astype(BF16), w_ffn_down.astype(BF16),
                                         w_in.astype(BF16), w_out.astype(BF16))
    win_full = g_in.transpose(1, 2, 0, 3).reshape(2, D, N_IN)
    winp = jnp.pad(win_full, ((0, 0), (0, 0), (0, N_INP - N_IN)))

    wr_d = _block_diag(w_rgate).astype(BF16)
    wi_d = _block_diag(w_igate).astype(BF16)
    cw8 = jnp.pad(conv_w_full, ((0, 0), (0, 4), (0, 0)))
    vp8 = jnp.stack([_rows8([conv_b[l], b_rgate[l], b_igate[l], lru_lambda[l]], LW) for l in range(2)])
    bfp = jnp.pad(b_fgate, ((0, 0), (0, 128 - NH)))[:, None, :] * jnp.ones((1, 8, 1), F32)
    gqk2 = jnp.tile(jnp.pad(g_qk, ((0, 0), (0, 6), (0, 0))), (1, 1, 2))
    gmix8 = jnp.pad(g_mix_out[:, None, :], ((0, 0), (0, 7), (0, 0)))

    x2 = x.reshape(T, D)
    tgt = loss_target.reshape(T, D)

    saved = []
    xc = x2
    for l in range(2):
        sv = {}
        sv["x0"] = xc
        xc, sv["g0"], sv["u0"], sv["f0"] = _ffn_fwd(xc, mods[l][0], wup5, wdn4, l, 0, S)
        sv["x1"] = xc
        sv["h1"], proj = _mix_in_fwd(xc, mods[l][1], winp, l, S)
        sv["proj"] = proj
        sv["ylru"], sv["hl"] = _lru_fwd(proj, cw8, vp8, wr_d, wi_d, l, S)
        sv["osb"], sv["t1"] = _sbp_fwd(proj, l, S)
        cum = _fgate_fwd(proj, bfp, l, S)
        sv["ck"] = cum[:, :NH].reshape(B, S, NH).transpose(0, 2, 1).reshape(B, NH, S // TK_(S), 1, TK_(S))
        sv["ofx"], sv["nl"] = _foxp_fwd(proj, cum, sv["ck"], gqk2, l, S)
        xc, sv["y"], sv["mo"] = _mix_out_fwd(xc, sv["ylru"], sv["osb"], sv["ofx"], mods[l][1], gmix8, wout, l, S)
        sv["x2"] = xc
        xc, sv["g2"], sv["u2"], sv["f2"] = _ffn_fwd(xc, mods[l][2], wup5, wdn4, l, 1, S)
        saved.append(sv)

    dxc, lpart = _loss_head(xc, tgt, S)
    loss = lax.psum(lpart[0, 0], ("x", "y", "c"))

    tf = wup5.shape[-1]
    g_up_l = [[None, None], [None, None]]
    g_dn_l = [[None, None], [None, None]]
    g_in_l, g_out_l = [None, None], [None, None]
    dmods = [[None] * 3 for _ in range(2)]
    small = [dict() for _ in range(2)]

    def ffn_back(l, j, xin, dy, sv, sub):
        dx, dmod, wacc, hb, dfb, ab, dgub = _ffn_bwd(
            xin, dy, mods[l][sub], sv[f"f{sub}"], sv[f"g{sub}"], sv[f"u{sub}"], wup5, wdn4, l, j, S)
        g_up_l[l][j] = _mm_tn(hb, dgub, f"dw_up_{l}_{j}", tnb=tf, split_n=True, with_bf16=True)
        g_dn_l[l][j] = tuple(g.reshape(N_SHARD, -1, D)
                             for g in _mm_tn(ab, dfb, f"dw_dn_{l}_{j}", tma=tf, with_bf16=True))
        dmods[l][sub] = dmod
        small[l][f"gn{sub}"] = wacc[0]
        return dx

    for l in (1, 0):
        sv = saved[l]
        dxc = ffn_back(l, 1, sv["x2"], dxc, sv, 2)
        dyl, dsb, dfx, dmo, dmod1, wacc_mo = _mix_out_bwd(
            dxc, sv["ylru"], sv["osb"], sv["ofx"], sv["mo"], mods[l][1], gmix8, wout, l, S)
        small[l]["gmix"] = wacc_mo[0]
        g_out_l[l] = tuple(g.reshape(N_SHARD, -1, D) for g in _mm_tn(sv["y"], dmo, f"dw_out_{l}", with_bf16=True))
        dsq, dsk, dsv = _sbp_bwd(sv["proj"], dsb, sv["t1"], l, S)
        dfq, dfk, dfv, dck, wacc_fx = _foxp_bwd(sv["proj"], dfx, sv["nl"], sv["ck"], gqk2, l, S)
        small[l]["gqk"] = wacc_fx[0:2, :HD] + wacc_fx[0:2, HD:]
        dcum = dck[:, :, :, 0, :].reshape(B, NH, S).transpose(0, 2, 1).reshape(T, NH)
        dff_, wacc_fg = _fgate_bwd(jnp.pad(dcum, ((0, 0), (0, 128 - NH))), sv["proj"], bfp, l, S)
        small[l]["bf"] = wacc_fg[0, :NH]
        dlx, dlg, dpr, dpi, ub, wacc_lru = _lru_bwd(dyl, sv["proj"], sv["hl"], cw8, vp8, wr_d, wi_d, l, S)
        small[l]["lru"] = wacc_lru
        small[l]["wr"] = _diag_blocks(_mm_tn(ub, dpr, f"dw_rgate_{l}"))
        small[l]["wi"] = _diag_blocks(_mm_tn(ub, dpi, f"dw_igate_{l}"))
        dproj = jnp.concatenate(
            [dlx, dlg, dsq, dsk, dsv, dfq, dfk, dfv, dff_], axis=1)
        g_in = _mm_tn(sv["h1"], dproj, f"dw_in_{l}", tnb=N_INP // 3)[:, :N_IN]
        g_in = g_in.reshape(D, N_SHARD, -1).transpose(1, 0, 2)
        g_in_l[l] = (g_in, g_in.astype(BF16))
        dxc, dmod_in, wacc_in = _mix_in_bwd(sv["x1"], dxc, mods[l][1], dproj, winp, l, S)
        dmods[l][1] = dmod_in + dmod1
        small[l]["gn1"] = wacc_in[0]
        dxc = ffn_back(l, 0, sv["x0"], dxc, sv, 0)
    grad_x = dxc.reshape(B, S, D)

    dmod_loc = jnp.stack([jnp.stack([dmods[l][j][:, 0:3, :] for j in range(3)], axis=1) for l in range(2)])
    drows = 2 * B * 9
    blk3 = _pack_rows([dmod_loc], -(-drows // 8) * 8, F32)
    ag3 = _all_gather8(blk3, "ag_dmod", True)
    dmod_all = ag3[:, :drows].reshape(N_DEV, 2, B, 9 * D).transpose(1, 0, 2, 3).reshape(2, n_all, 9 * D)
    dmod_mine = lax.dynamic_slice(dmod_all, (0, 0, sidx * ada_cols), (2, n_all, ada_cols))
    grad_w_ada = _ada_bwd(c_all, dmod_mine)
    dmod_rows = jnp.pad(dmod_all.transpose(1, 0, 2).reshape(n_all, 2 * 9, D), ((0, 0), (0, 6), (0, 0)))
    grad_b_ada = _sum_lead(dmod_rows, "grad_b_ada")[:2 * 9].reshape(2, 9 * D)

    sm_parts = [
        jnp.stack([small[l]["bf"] for l in range(2)]),
        jnp.stack([small[l]["lru"][4] for l in range(2)]),
        jnp.stack([small[l]["wr"] for l in range(2)]),
        jnp.stack([small[l]["lru"][5] for l in range(2)]),
        jnp.stack([small[l]["wi"] for l in range(2)]),
        jnp.stack([small[l]["lru"][6] for l in range(2)]),
        jnp.stack([small[l]["lru"][7] for l in range(2)]),
        jnp.stack([small[l]["gqk"] for l in range(2)]),
        jnp.stack([small[l]["gmix"] for l in range(2)]),
        jnp.stack([jnp.stack([small[l][f"gn{j}"] for j in range(3)]) for l in range(2)]),
        jnp.stack([small[l]["lru"][0:4] for l in range(2)]),
    ]
    sm_shapes = [p.shape for p in sm_parts]
    sm_rows = -(-sum(p.size for p in sm_parts) // (8 * ROW)) * 8
    ag4 = _all_gather8(_pack_rows(sm_parts, sm_rows, F32), "ag_small_grads", True)
    sm_sum = _sum_lead(ag4, "sum_small_grads").reshape(-1)
    (g_bf, g_cb, g_wr, g_br, g_wi, g_bi, g_lam, g_gqk, g_gmix, g_gn_full, g_cw_full) = _unpack(sm_sum, sm_shapes)
    g_gn = lax.dynamic_slice(g_gn_full, (0, 0, sidx * gn_cols), (2, 3, gn_cols))
    g_cw = lax.dynamic_slice(g_cw_full, (0, 0, sidx * cw_cols), (2, 4, cw_cols))

    lj = [(l, j) for l in range(2) for j in range(2)]
    groups = [
        ("up", [g_up_l[l][j] for l, j in lj], w_ffn_up.shape, lj),
        ("dn", [g_dn_l[l][j] for l, j in lj], w_ffn_down.shape, lj),
        ("in", g_in_l, w_in.shape, [(0,), (1,)]),
        ("out", g_out_l, w_out.shape, [(0,), (1,)]),
    ]
    pieces = [p for _, ps, _, _ in groups for p in ps]
    cvec = jnp.reshape(ci, (1,)).astype(jnp.int32)
    scvec = jnp.stack([sidx, ci]).astype(jnp.int32)
    recv_a = _rs_to_sibling([pb for _, pb in pieces])
    hs, off = [], 0
    for gname, ps, _, _ in groups:
        hs += _add_sibling([pf for pf, _ in ps], recv_a[off:off + len(ps)], cvec, f"rs_add_sibling_{gname}")
        off += len(ps)
    recv_b = _rs_to_chips(hs)
    tensors, places, r2s, k = [], [], [], 0
    for gi, (gname, ps, shape, leads) in enumerate(groups):
        t = None
        for (pf, _), lead in zip(ps, leads):
            t = _add_chips_into(pf, recv_a[k], recv_b[k], scvec, t, shape, lead,
                                f"rs_add_chips_{gname}_{'_'.join(map(str, lead))}")
            places.append((gi, lead))
            r2s.append(pf.shape[1] // 2)
            k += 1
        tensors.append(t)
    gw_up, gw_dn, gw_in, gw_out = _share_halves(tensors, places, r2s)

    def upd(w, g, m, v, name):
        sh = w.shape
        two = (w.size // sh[-1], sh[-1])
        dlt, mn, vn = _adamw(w.reshape(two), g.reshape(two), m.reshape(two), v.reshape(two), name)
        return dlt.reshape(sh), mn.reshape(sh), vn.reshape(sh)

    big = {
        "w_ada": (w_ada, grad_w_ada, m_w_ada, v_w_ada),
        "w_ffn_up": (w_ffn_up, gw_up, m_w_ffn_up, v_w_ffn_up),
        "w_ffn_down": (w_ffn_down, gw_dn, m_w_ffn_down, v_w_ffn_down),
        "w_in": (w_in, gw_in, m_w_in, v_w_in),
        "w_out": (w_out, gw_out, m_w_out, v_w_out),
    }
    res = {n: (t[1],) + upd(*t, f"adamw_{n}") for n, t in big.items()}

    smalls = {
        "b_ada": (b_ada, grad_b_ada, m_b_ada, v_b_ada),
        "g_norm": (g_norm, g_gn, m_g_norm, v_g_norm),
        "b_fgate": (b_fgate, g_bf, m_b_fgate, v_b_fgate),
        "conv_w": (conv_w, g_cw, m_conv_w, v_conv_w),
        "conv_b": (conv_b, g_cb, m_conv_b, v_conv_b),
        "w_rgate": (w_rgate, g_wr, m_w_rgate, v_w_rgate),
        "b_rgate": (b_rgate, g_br, m_b_rgate, v_b_rgate),
        "w_igate": (w_igate, g_wi, m_w_igate, v_w_igate),
        "b_igate": (b_igate, g_bi, m_b_igate, v_b_igate),
        "lru_lambda": (lru_lambda, g_lam, m_lru_lambda, v_lru_lambda),
        "g_qk": (g_qk, g_gqk, m_g_qk, v_g_qk),
        "g_mix_out": (g_mix_out, g_gmix, m_g_mix_out, v_g_mix_out),
    }
    names = list(smalls)
    shapes = [smalls[n][0].shape for n in names]
    prow = -(-sum(math.prod(s) for s in shapes) // (8 * ROW)) * 8
    packed = [_pack_rows([smalls[n][i].reshape(shapes[k]) for k, n in enumerate(names)], prow, F32) for i in range(4)]
    outs = _adamw(packed[0], packed[1], packed[2], packed[3], "adamw_small")
    un = [_unpack(o.reshape(-1), shapes) for o in outs]
    for k, n in enumerate(names):
        res[n] = (smalls[n][1].reshape(shapes[k]), un[0][k], un[1][k], un[2][k])

    order = ["w_ada", "b_ada", "g_norm", "w_ffn_up", "w_ffn_down", "w_in", "b_fgate", "conv_w", "conv_b",
             "w_rgate", "b_rgate", "w_igate", "b_igate", "lru_lambda", "g_qk", "g_mix_out", "w_out"]
    return (loss, grad_x, *[res[n][0] for n in order], *[res[n][1] for n in order],
            *[res[n][2] for n in order], *[res[n][3] for n in order])


def TQ_(S):
    return min(TQ, S)


def TK_(S):
    return min(TK, S)


def _unpack_shards(wg, shapes):
    out, off = [], 0
    for sh in shapes:
        n = math.prod(sh)
        out.append(wg[:, off:off + n].reshape((N_SHARD,) + tuple(sh)))
        off += n
    return out
```

```python
import math

import jax
import jax.numpy as jnp
from jax import lax
from jax.experimental import pallas as pl
from jax.experimental.pallas import tpu as pltpu

F32 = jnp.float32
BF16 = jnp.bfloat16
MESH = pl.DeviceIdType.MESH

D = 1024
HD = 64
LW = 512
NH = 4
AW = NH * HD
N_IN = 2564
N_INP = 2688
F_BLK = 2560 // 128
EPS = 1e-6
LRU_C = 8.0
SCALE = HD ** -0.5
NEG = -1e30
TQ = 256
TK = 256

ADAM_LR, ADAM_B1, ADAM_B2, ADAM_EPS, ADAM_WD, ADAM_STEP = 0.001, 0.9, 0.999, 1e-08, 0.01, 10

VMEM_BIG = 56 * 1024 * 1024
N_DEV = 8
N_SHARD = 4
ROW = 1024


def _cp(sem, vmem=None):
    return pltpu.CompilerParams(dimension_semantics=sem, vmem_limit_bytes=vmem)


def _dot(a, b):
    return jnp.dot(a, b, preferred_element_type=F32)


def _dot_nt(a, b):
    return lax.dot_general(a, b, (((1,), (1,)), ((), ())), preferred_element_type=F32)


def _dot_tn(a, b):
    return lax.dot_general(a, b, (((0,), (0,)), ((), ())), preferred_element_type=F32)


def _log1p(e):
    small = e * (1.0 - e * (0.5 - e * (1.0 / 3.0 - e * 0.25)))
    return jnp.where(e < 0.01, small, jnp.log(1.0 + e))


def _expm1_neg(x):
    small = x * (1.0 + x * 0.5 * (1.0 + x * (1.0 / 3.0) * (1.0 + x * 0.25 * (1.0 + x * 0.2))))
    return jnp.where(x > -0.05, small, jnp.exp(x) - 1.0)


def _sigmoid(x):
    return 1.0 / (1.0 + jnp.exp(-x))


_GELU_C = math.sqrt(2.0 / math.pi)


def _gelu_and_grad(x):
    x2 = x * x
    th = jnp.tanh(_GELU_C * (x + 0.044715 * x * x2))
    g = 0.5 * x * (1.0 + th)
    dg = 0.5 * (1.0 + th) + 0.5 * x * (1.0 - th * th) * _GELU_C * (1.0 + 3.0 * 0.044715 * x2)
    return g, dg


def _rms_rows(x):
    rstd = lax.rsqrt(jnp.mean(x * x, axis=-1, keepdims=True) + EPS)
    return x * rstd, rstd


def _rms_bwd(xn, rstd, dyn):
    return rstd * (dyn - xn * jnp.mean(dyn * xn, axis=-1, keepdims=True))


def _colsum(x):
    return jnp.sum(x, axis=0, keepdims=True)


def _rowsum(x):
    return jnp.sum(x, axis=1, keepdims=True)


def _split3(x):
    hi = x.astype(BF16)
    r = x - hi.astype(F32)
    mid = r.astype(BF16)
    lo = (r - mid.astype(F32)).astype(BF16)
    return hi, mid, lo


def _cumsum_mm(x, ones_tri, parts=3):
    ps = _split3(x)[:parts]
    acc = _dot(ps[0], ones_tri)
    for p in ps[1:]:
        acc = acc + _dot(p, ones_tri)
    return acc


def _tri(n, kind):
    r = lax.broadcasted_iota(jnp.int32, (n, n), 0)
    c = lax.broadcasted_iota(jnp.int32, (n, n), 1)
    m = {"row_gt_col": r > c, "row_le_col": r <= c, "row_lt_col": r < c}[kind]
    return jnp.where(m, 1.0, 0.0).astype(BF16)


def _normmod(x, mod_ref):
    xn, rstd = _rms_rows(x)
    h = xn * mod_ref[3:4, :] * mod_ref[1:2, :] + mod_ref[0:1, :]
    return h, xn, rstd


def _normmod_bwd(dh, xn, rstd, mod_ref, dmod_ref, wacc_ref):
    gn = mod_ref[3:4, :]
    sc = mod_ref[1:2, :]
    dmod_ref[0:1, :] += _colsum(dh)
    t = _colsum(dh * xn)
    dmod_ref[1:2, :] += t * gn
    wacc_ref[0:1, :] += t * sc
    return _rms_bwd(xn, rstd, dh * (gn * sc))


def _tile(n, want):
    t = min(n, want)
    while n % t:
        t //= 2
    return t


def _tile_div8(n, cap, mult=8):
    best = mult
    for t in range(mult, min(n, cap) + 1, mult):
        if n % t == 0:
            best = t
    assert n % best == 0
    return best


def _ffn_fwd(x, mod, wup5, wdn4, l, j, S):
    T = x.shape[0]
    tf = wup5.shape[-1]
    nk = 2
    tm = _tile(S, 512)
    tpb = S // tm

    def body(x_ref, mod_ref, wg_ref, wu_ref, wd_ref, xo_ref, g_ref, u_ref, f_ref, h_sc, acc_sc):
        k = pl.program_id(1)

        @pl.when(k == 0)
        def _():
            h, _, _ = _normmod(x_ref[...], mod_ref)
            h_sc[...] = h.astype(BF16)
            acc_sc[...] = jnp.zeros_like(acc_sc)

        h = h_sc[...]
        g = _dot(h, wg_ref[...])
        u = _dot(h, wu_ref[...])
        g_ref[...] = g.astype(BF16)
        u_ref[...] = u.astype(BF16)
        a = (g * _sigmoid(g)) * u
        acc_sc[...] += _dot(a.astype(BF16), wd_ref[...])

        @pl.when(k == nk - 1)
        def _():
            f = acc_sc[...]
            f_ref[...] = f.astype(BF16)
            xo_ref[...] = x_ref[...] + (0.5 * mod_ref[2:3, :]) * f

    return pl.pallas_call(
        body, name=f"ffn_fwd_{l}_{j}",
        grid=(T // tm, nk),
        in_specs=[
            pl.BlockSpec((tm, D), lambda i, k: (i, 0)),
            pl.BlockSpec((None, 8, D), lambda i, k: (i // tpb, 0, 0)),
            pl.BlockSpec((None, None, None, D, tf), lambda i, k: (k, l, j, 0, 0)),
            pl.BlockSpec((None, None, None, D, tf), lambda i, k: (nk + k, l, j, 0, 0)),
            pl.BlockSpec((None, None, tf, D), lambda i, k: (l, j, k, 0)),
        ],
        out_specs=[
            pl.BlockSpec((tm, D), lambda i, k: (i, 0)),
            pl.BlockSpec((tm, tf), lambda i, k: (i, k)),
            pl.BlockSpec((tm, tf), lambda i, k: (i, k)),
            pl.BlockSpec((tm, D), lambda i, k: (i, 0)),
        ],
        out_shape=[
            jax.ShapeDtypeStruct((T, D), F32),
            jax.ShapeDtypeStruct((T, nk * tf), BF16),
            jax.ShapeDtypeStruct((T, nk * tf), BF16),
            jax.ShapeDtypeStruct((T, D), BF16),
        ],
        scratch_shapes=[pltpu.VMEM((tm, D), BF16), pltpu.VMEM((tm, D), F32)],
        compiler_params=_cp(("arbitrary", "arbitrary"), VMEM_BIG),
    )(x, mod, wup5, wup5, wdn4)


def _ffn_bwd(x, dy, mod, f, g, u, wup5, wdn4, l, j, S):
    T = x.shape[0]
    tf = wup5.shape[-1]
    nk = 2
    tm = _tile(S, 256)
    tpb = S // tm

    def body(x_ref, dy_ref, mod_ref, f_ref, g_ref, u_ref, wup_ref, wd_ref,
             dx_ref, dmod_ref, wacc_ref, h_ref, df_ref, a_ref, dgu_ref):
        i = pl.program_id(0)

        @pl.when(i == 0)
        def _():
            wacc_ref[...] = jnp.zeros_like(wacc_ref)

        @pl.when(i % tpb == 0)
        def _():
            dmod_ref[...] = jnp.zeros_like(dmod_ref)

        dy_ = dy_ref[...]
        h, xn, rstd = _normmod(x_ref[...], mod_ref)
        h_ref[...] = h.astype(BF16)
        dfb = ((0.5 * mod_ref[2:3, :]) * dy_).astype(BF16)
        df_ref[...] = dfb
        dmod_ref[2:3, :] += _colsum(0.5 * f_ref[...].astype(F32) * dy_)
        dh = None
        for k in range(nk):
            cols = slice(k * tf, (k + 1) * tf)
            da = _dot_nt(dfb, wd_ref[cols, :])
            gg = g_ref[:, cols].astype(F32)
            uu = u_ref[:, cols].astype(F32)
            sig = _sigmoid(gg)
            s = gg * sig
            a_ref[:, cols] = (s * uu).astype(BF16)
            du = (da * s).astype(BF16)
            dg = (da * uu * (sig * (1.0 + gg * (1.0 - sig)))).astype(BF16)
            dgu_ref[0, :, cols] = dg
            dgu_ref[1, :, cols] = du
            part = _dot_nt(dg, wup_ref[k]) + _dot_nt(du, wup_ref[nk + k])
            dh = part if dh is None else dh + part
        dx_ref[...] = dy_ + _normmod_bwd(dh, xn, rstd, mod_ref, dmod_ref, wacc_ref)

    once = pl.Buffered(1)
    return pl.pallas_call(
        body, name=f"ffn_bwd_{l}_{j}",
        grid=(T // tm,),
        in_specs=[
            pl.BlockSpec((tm, D), lambda i: (i, 0)),
            pl.BlockSpec((tm, D), lambda i: (i, 0)),
            pl.BlockSpec((None, 8, D), lambda i: (i // tpb, 0, 0)),
            pl.BlockSpec((tm, D), lambda i: (i, 0)),
            pl.BlockSpec((tm, nk * tf), lambda i: (i, 0)),
            pl.BlockSpec((tm, nk * tf), lambda i: (i, 0)),
            pl.BlockSpec((2 * nk, None, None, D, tf), lambda i: (0, l, j, 0, 0), pipeline_mode=once),
            pl.BlockSpec((None, None, nk * tf, D), lambda i: (l, j, 0, 0), pipeline_mode=once),
        ],
        out_specs=[
            pl.BlockSpec((tm, D), lambda i: (i, 0)),
            pl.BlockSpec((None, 8, D), lambda i: (i // tpb, 0, 0)),
            pl.BlockSpec((8, D), lambda i: (0, 0)),
            pl.BlockSpec((tm, D), lambda i: (i, 0)),
            pl.BlockSpec((tm, D), lambda i: (i, 0)),
            pl.BlockSpec((tm, nk * tf), lambda i: (i, 0)),
            pl.BlockSpec((2, tm, nk * tf), lambda i: (0, i, 0)),
        ],
        out_shape=[
            jax.ShapeDtypeStruct((T, D), F32),
            jax.ShapeDtypeStruct((T // S, 8, D), F32),
            jax.ShapeDtypeStruct((8, D), F32),
            jax.ShapeDtypeStruct((T, D), BF16),
            jax.ShapeDtypeStruct((T, D), BF16),
            jax.ShapeDtypeStruct((T, nk * tf), BF16),
            jax.ShapeDtypeStruct((2, T, nk * tf), BF16),
        ],
        compiler_params=_cp(("arbitrary",), VMEM_BIG),
    )(x, dy, mod, f, g, u, wup5, wdn4)


def _mm_tn(a, b, name, tma=None, tnb=None, split_n=False, with_bf16=False):
    T, M = a.shape
    b3 = b if b.ndim == 3 else b[None]
    nb, _, N = b3.shape
    tma = tma or M
    tnb = tnb or N
    npb = N // tnb
    tt = _tile(T, 1024)
    nt = T // tt

    def body(a_ref, b_ref, o_ref, *ob_ref):
        @pl.when(pl.program_id(2) == 0)
        def _():
            o_ref[...] = jnp.zeros_like(o_ref)

        o_ref[...] += _dot_tn(a_ref[...], b_ref[...])

        if with_bf16:
            @pl.when(pl.program_id(2) == nt - 1)
            def _():
                ob_ref[0][...] = o_ref[...].astype(BF16)

    if split_n:
        shape = (nb * npb, M, tnb)
        out_spec = pl.BlockSpec((None, tma, tnb), lambda m, n, t: (n, m, 0))
    else:
        assert nb == 1
        shape = (M, N)
        out_spec = pl.BlockSpec((tma, tnb), lambda m, n, t: (m, n))
    dts = (F32, BF16) if with_bf16 else (F32,)
    out = pl.pallas_call(
        body, name=name,
        grid=(M // tma, nb * npb, nt),
        in_specs=[pl.BlockSpec((tt, tma), lambda m, n, t: (t, m)),
                  pl.BlockSpec((None, tt, tnb), lambda m, n, t: (n // npb, t, n % npb))],
        out_specs=[out_spec] * len(dts),
        out_shape=[jax.ShapeDtypeStruct(shape, dt) for dt in dts],
        compiler_params=_cp(("arbitrary", "arbitrary", "arbitrary"), VMEM_BIG),
    )(a, b3)
    return tuple(out) if with_bf16 else out[0]


def _mix_in_fwd(x, mod, winp, l, S):
    T = x.shape[0]
    tm = _tile(S, 512)
    tpb = S // tm

    def body(x_ref, mod_ref, w_ref, h_ref, p_ref):
        h, _, _ = _normmod(x_ref[...], mod_ref)
        hb = h.astype(BF16)
        h_ref[...] = hb
        p_ref[...] = _dot(hb, w_ref[...])

    return pl.pallas_call(
        body, name=f"mix_in_fwd_{l}",
        grid=(T // tm,),
        in_specs=[pl.BlockSpec((tm, D), lambda i: (i, 0)),
                  pl.BlockSpec((None, 8, D), lambda i: (i // tpb, 0, 0)),
                  pl.BlockSpec((None, D, N_INP), lambda i: (l, 0, 0))],
        out_specs=[pl.BlockSpec((tm, D), lambda i: (i, 0)),
                   pl.BlockSpec((tm, N_INP), lambda i: (i, 0))],
        out_shape=[jax.ShapeDtypeStruct((T, D), BF16), jax.ShapeDtypeStruct((T, N_INP), F32)],
        compiler_params=_cp(("arbitrary",), VMEM_BIG),
    )(x, mod, winp)


def _mix_in_bwd(x, dres, mod, dproj, winp, l, S):
    T = x.shape[0]
    tm = _tile(S, 512)
    tpb = S // tm

    def body(x_ref, dr_ref, mod_ref, dp_ref, w_ref, dx_ref, dmod_ref, wacc_ref):
        i = pl.program_id(0)

        @pl.when(i == 0)
        def _():
            wacc_ref[...] = jnp.zeros_like(wacc_ref)

        @pl.when(i % tpb == 0)
        def _():
            dmod_ref[...] = jnp.zeros_like(dmod_ref)

        dh = _dot_nt(dp_ref[...], w_ref[...])
        _, xn, rstd = _normmod(x_ref[...], mod_ref)
        dx_ref[...] = dr_ref[...] + _normmod_bwd(dh, xn, rstd, mod_ref, dmod_ref, wacc_ref)

    return pl.pallas_call(
        body, name=f"mix_in_bwd_{l}",
        grid=(T // tm,),
        in_specs=[pl.BlockSpec((tm, D), lambda i: (i, 0)),
                  pl.BlockSpec((tm, D), lambda i: (i, 0)),
                  pl.BlockSpec((None, 8, D), lambda i: (i // tpb, 0, 0)),
                  pl.BlockSpec((tm, N_INP), lambda i: (i, 0)),
                  pl.BlockSpec((None, D, N_INP), lambda i: (l, 0, 0))],
        out_specs=[pl.BlockSpec((tm, D), lambda i: (i, 0)),
                   pl.BlockSpec((None, 8, D), lambda i: (i // tpb, 0, 0)),
                   pl.BlockSpec((8, D), lambda i: (0, 0))],
        out_shape=[jax.ShapeDtypeStruct((T, D), F32),
                   jax.ShapeDtypeStruct((T // S, 8, D), F32),
                   jax.ShapeDtypeStruct((8, D), F32)],
        compiler_params=_cp(("arbitrary",), VMEM_BIG),
    )(x, dres, mod, dproj, winp)


_GROUPS = ((0, LW), (LW, LW + AW), (LW + AW, D))


def _mix_out_fwd(x, ylru, osb, ofox, mod, gmix, wout, l, S):
    T = x.shape[0]
    tm = _tile(S, 512)
    tpb = S // tm

    def body(x_ref, yl_ref, sb_ref, fx_ref, mod_ref, gm_ref, w_ref, xo_ref, y_ref, mo_ref):
        for src, (lo, hi) in zip((yl_ref, sb_ref, fx_ref), _GROUPS):
            vn, _ = _rms_rows(src[...])
            y_ref[:, lo:hi] = (vn * gm_ref[0:1, lo:hi]).astype(BF16)
        mo = _dot(y_ref[...], w_ref[...])
        mo_ref[...] = mo.astype(BF16)
        xo_ref[...] = x_ref[...] + mod_ref[2:3, :] * mo

    return pl.pallas_call(
        body, name=f"mix_out_fwd_{l}",
        grid=(T // tm,),
        in_specs=[pl.BlockSpec((tm, D), lambda i: (i, 0)),
                  pl.BlockSpec((tm, LW), lambda i: (i, 0)),
                  pl.BlockSpec((tm, AW), lambda i: (i, 0)),
                  pl.BlockSpec((tm, AW), lambda i: (i, 0)),
                  pl.BlockSpec((None, 8, D), lambda i: (i // tpb, 0, 0)),
                  pl.BlockSpec((None, 8, D), lambda i: (l, 0, 0)),
                  pl.BlockSpec((None, D, D), lambda i: (l, 0, 0))],
        out_specs=[pl.BlockSpec((tm, D), lambda i: (i, 0)),
                   pl.BlockSpec((tm, D), lambda i: (i, 0)),
                   pl.BlockSpec((tm, D), lambda i: (i, 0))],
        out_shape=[jax.ShapeDtypeStruct((T, D), F32),
                   jax.ShapeDtypeStruct((T, D), BF16),
                   jax.ShapeDtypeStruct((T, D), BF16)],
        compiler_params=_cp(("arbitrary",), VMEM_BIG),
    )(x, ylru, osb, ofox, mod, gmix, wout)


def _mix_out_bwd(dx2, ylru, osb, ofox, mo, mod, gmix, wout, l, S):
    T = dx2.shape[0]
    tm = _tile(S, 512)
    tpb = S // tm

    def body(dx_ref, yl_ref, sb_ref, fx_ref, mo_ref, mod_ref, gm_ref, w_ref,
             dyl_ref, dsb_ref, dfx_ref, dmo_ref, dmod_ref, wacc_ref):
        i = pl.program_id(0)

        @pl.when(i == 0)
        def _():
            wacc_ref[...] = jnp.zeros_like(wacc_ref)

        @pl.when(i % tpb == 0)
        def _():
            dmod_ref[...] = jnp.zeros_like(dmod_ref)

        dx = dx_ref[...]
        dmod_ref[2:3, :] += _colsum(mo_ref[...].astype(F32) * dx)
        dmo = (mod_ref[2:3, :] * dx).astype(BF16)
        dmo_ref[...] = dmo
        dy = _dot_nt(dmo, w_ref[...])
        for src, dst, (lo, hi) in zip((yl_ref, sb_ref, fx_ref), (dyl_ref, dsb_ref, dfx_ref), _GROUPS):
            vn, rstd = _rms_rows(src[...])
            dyg = dy[:, lo:hi]
            wacc_ref[0:1, lo:hi] += _colsum(dyg * vn)
            dst[...] = _rms_bwd(vn, rstd, dyg * gm_ref[0:1, lo:hi])

    return pl.pallas_call(
        body, name=f"mix_out_bwd_{l}",
        grid=(T // tm,),
        in_specs=[pl.BlockSpec((tm, D), lambda i: (i, 0)),
                  pl.BlockSpec((tm, LW), lambda i: (i, 0)),
                  pl.BlockSpec((tm, AW), lambda i: (i, 0)),
                  pl.BlockSpec((tm, AW), lambda i: (i, 0)),
                  pl.BlockSpec((tm, D), lambda i: (i, 0)),
                  pl.BlockSpec((None, 8, D), lambda i: (i // tpb, 0, 0)),
                  pl.BlockSpec((None, 8, D), lambda i: (l, 0, 0)),
                  pl.BlockSpec((None, D, D), lambda i: (l, 0, 0))],
        out_specs=[pl.BlockSpec((tm, LW), lambda i: (i, 0)),
                   pl.BlockSpec((tm, AW), lambda i: (i, 0)),
                   pl.BlockSpec((tm, AW), lambda i: (i, 0)),
                   pl.BlockSpec((tm, D), lambda i: (i, 0)),
                   pl.BlockSpec((None, 8, D), lambda i: (i // tpb, 0, 0)),
                   pl.BlockSpec((8, D), lambda i: (0, 0))],
        out_shape=[jax.ShapeDtypeStruct((T, LW), F32),
                   jax.ShapeDtypeStruct((T, AW), F32),
                   jax.ShapeDtypeStruct((T, AW), F32),
                   jax.ShapeDtypeStruct((T, D), BF16),
                   jax.ShapeDtypeStruct((T // S, 8, D), F32),
                   jax.ShapeDtypeStruct((8, D), F32)],
        compiler_params=_cp(("arbitrary",), VMEM_BIG),
    )(dx2, ylru, osb, ofox, mo, mod, gmix, wout)


def _loss_head(y, tgt, S):
    T = y.shape[0]
    tm = _tile(S, 512)

    def body(y_ref, t_ref, dy_ref, l_ref):
        @pl.when(pl.program_id(0) == 0)
        def _():
            l_ref[...] = jnp.zeros_like(l_ref)

        d = y_ref[...] - t_ref[...]
        dy_ref[...] = d * (1.0 / D)
        l_ref[...] += (0.5 / D) * _rowsum(_colsum(d * d))

    return pl.pallas_call(
        body, name="loss_head",
        grid=(T // tm,),
        in_specs=[pl.BlockSpec((tm, D), lambda i: (i, 0)), pl.BlockSpec((tm, D), lambda i: (i, 0))],
        out_specs=[pl.BlockSpec((tm, D), lambda i: (i, 0)), pl.BlockSpec((8, 128), lambda i: (0, 0))],
        out_shape=[jax.ShapeDtypeStruct((T, D), F32), jax.ShapeDtypeStruct((8, 128), F32)],
        compiler_params=_cp(("arbitrary",)),
    )(y, tgt)


def _lru_gates(u, vp_ref, wr_ref, wi_ref):
    ub = u.astype(BF16)
    r = _sigmoid(_dot(ub, wr_ref[...]) + vp_ref[1:2, :])
    ig = _sigmoid(_dot(ub, wi_ref[...]) + vp_ref[2:3, :])
    lam = vp_ref[3:4, :]
    sp = jnp.maximum(-lam, 0.0) + _log1p(jnp.exp(-jnp.abs(lam)))
    log_a = (-LRU_C) * r * sp
    a = jnp.exp(log_a)
    mult = jnp.sqrt(-_expm1_neg(2.0 * log_a))
    return ub, r, ig, sp, a, mult


def _conv_taps(x, xp, row, cw_ref):
    xs = [x]
    for d in (1, 2, 3):
        xs.append(jnp.where(row >= d, pltpu.roll(x, d, 0), pltpu.roll(xp, d, 0)))
    u = xs[0] * cw_ref[3:4, :]
    for d in (1, 2, 3):
        u = u + xs[d] * cw_ref[3 - d:4 - d, :]
    return xs, u


def _lru_fwd(proj, cw, vp, wr, wi, l, S):
    T = proj.shape[0]
    ts = _tile(S, 256)
    nb = S // ts

    def body(x_ref, lg_ref, cw_ref, vp_ref, wr_ref, wi_ref, y_ref, h_ref, xp_sc, hc_sc):
        @pl.when(pl.program_id(1) == 0)
        def _():
            xp_sc[...] = jnp.zeros_like(xp_sc)
            hc_sc[...] = jnp.zeros_like(hc_sc)

        row = lax.broadcasted_iota(jnp.int32, (ts, LW), 0)
        x = x_ref[...]
        _, u = _conv_taps(x, xp_sc[...], row, cw_ref)
        u = u + vp_ref[0:1, :]
        xp_sc[...] = x
        _, _, ig, _, a, mult = _lru_gates(u, vp_ref, wr_ref, wi_ref)
        bv = mult * (ig * u)
        av = a
        d = 1
        while d < ts:
            a_s = jnp.where(row >= d, pltpu.roll(av, d, 0), 1.0)
            b_s = jnp.where(row >= d, pltpu.roll(bv, d, 0), 0.0)
            bv = av * b_s + bv
            av = av * a_s
            d *= 2
        h = bv + av * hc_sc[7:8, :]
        hc_sc[...] = h[ts - 8:ts, :]
        h_ref[...] = h
        gl, _ = _gelu_and_grad(lg_ref[...])
        y_ref[...] = h * gl

    return pl.pallas_call(
        body, name=f"lru_fwd_{l}",
        grid=(T // S, nb),
        in_specs=[pl.BlockSpec((ts, LW), lambda b, j: (b * nb + j, 0)),
                  pl.BlockSpec((ts, LW), lambda b, j: (b * nb + j, 1)),
                  pl.BlockSpec((None, 8, LW), lambda b, j: (l, 0, 0)),
                  pl.BlockSpec((None, 8, LW), lambda b, j: (l, 0, 0)),
                  pl.BlockSpec((None, LW, LW), lambda b, j: (l, 0, 0)),
                  pl.BlockSpec((None, LW, LW), lambda b, j: (l, 0, 0))],
        out_specs=[pl.BlockSpec((ts, LW), lambda b, j: (b * nb + j, 0)),
                   pl.BlockSpec((ts, LW), lambda b, j: (b * nb + j, 0))],
        out_shape=[jax.ShapeDtypeStruct((T, LW), F32), jax.ShapeDtypeStruct((T, LW), F32)],
        scratch_shapes=[pltpu.VMEM((ts, LW), F32), pltpu.VMEM((8, LW), F32)],
        compiler_params=_cp(("arbitrary", "arbitrary")),
    )(proj, proj, cw, vp, wr, wi)


def _lru_bwd(dyl, proj, h, cw, vp, wr, wi, l, S):
    T = proj.shape[0]
    ts = _tile(S, 256)
    nb = S // ts

    def body(dy_ref, x_ref, xprev_ref, lg_ref, h_ref, hprev_ref, cw_ref, vp_ref, wr_ref, wi_ref,
             dx_ref, dlg_ref, dpr_ref, dpi_ref, ub_ref, wacc_ref, gc_sc, af_sc, dun_sc):
        b = pl.program_id(0)
        j = pl.program_id(1)
        first = j == nb - 1

        @pl.when((b == 0) & (j == 0))
        def _():
            wacc_ref[...] = jnp.zeros_like(wacc_ref)

        @pl.when(j == 0)
        def _():
            gc_sc[...] = jnp.zeros_like(gc_sc)
            af_sc[...] = jnp.ones_like(af_sc)
            dun_sc[...] = jnp.zeros_like(dun_sc)

        row = lax.broadcasted_iota(jnp.int32, (ts, LW), 0)
        keep = jnp.where(first, 0.0, 1.0)
        x = x_ref[...]
        xs, u = _conv_taps(x, xprev_ref[...] * keep, row, cw_ref)
        u = u + vp_ref[0:1, :]
        ub, r, ig, sp, a, mult = _lru_gates(u, vp_ref, wr_ref, wi_ref)
        ub_ref[...] = ub
        hh = h_ref[...]
        h_m1 = jnp.where(row >= 1, pltpu.roll(hh, 1, 0), pltpu.roll(hprev_ref[...] * keep, 1, 0))
        dy = dy_ref[...]
        gl, dgl = _gelu_and_grad(lg_ref[...])
        dlg_ref[...] = (dy * hh * dgl).astype(BF16)
        bv = dy * gl
        av = jnp.where(row < ts - 1, pltpu.roll(a, ts - 1, 0), af_sc[0:1, :])
        d = 1
        while d < ts:
            a_s = jnp.where(row < ts - d, pltpu.roll(av, ts - d, 0), 1.0)
            b_s = jnp.where(row < ts - d, pltpu.roll(bv, ts - d, 0), 0.0)
            bv = av * b_s + bv
            av = av * a_s
            d *= 2
        gt = bv + av * gc_sc[0:1, :]
        gc_sc[...] = gt[0:8, :]
        af_sc[...] = a[0:8, :]
        da = gt * h_m1
        d_ig = gt * mult * u
        d_mult = gt * ig * u
        du = gt * mult * ig
        dlog_a = da * a - d_mult * (a * a) / mult
        dpre_r = (dlog_a * ((-LRU_C) * sp)) * r * (1.0 - r)
        dpre_i = d_ig * ig * (1.0 - ig)
        lam = vp_ref[3:4, :]
        wacc_ref[7:8, :] += _colsum(dlog_a * r) * (LRU_C * _sigmoid(-lam))
        wacc_ref[5:6, :] += _colsum(dpre_r)
        wacc_ref[6:7, :] += _colsum(dpre_i)
        dprb = dpre_r.astype(BF16)
        dpib = dpre_i.astype(BF16)
        dpr_ref[...] = dprb
        dpi_ref[...] = dpib
        du = du + _dot_nt(dprb, wr_ref[...]) + _dot_nt(dpib, wi_ref[...])
        wacc_ref[4:5, :] += _colsum(du)
        dun = dun_sc[...]
        dx = du * cw_ref[3:4, :]
        wacc_ref[3:4, :] += _colsum(du * xs[0])
        for dd in (1, 2, 3):
            du_s = jnp.where(row < ts - dd, pltpu.roll(du, ts - dd, 0), pltpu.roll(dun, ts - dd, 0))
            dx = dx + du_s * cw_ref[3 - dd:4 - dd, :]
            wacc_ref[3 - dd:4 - dd, :] += _colsum(du * xs[dd])
        dun_sc[...] = du
        dx_ref[...] = dx.astype(BF16)

    def tb(b, j):
        return b * nb + (nb - 1 - j)

    def tbp(b, j):
        return b * nb + jnp.maximum(nb - 2 - j, 0)

    return pl.pallas_call(
        body, name=f"lru_bwd_{l}",
        grid=(T // S, nb),
        in_specs=[pl.BlockSpec((ts, LW), lambda b, j: (tb(b, j), 0)),
                  pl.BlockSpec((ts, LW), lambda b, j: (tb(b, j), 0)),
                  pl.BlockSpec((ts, LW), lambda b, j: (tbp(b, j), 0)),
                  pl.BlockSpec((ts, LW), lambda b, j: (tb(b, j), 1)),
                  pl.BlockSpec((ts, LW), lambda b, j: (tb(b, j), 0)),
                  pl.BlockSpec((ts, LW), lambda b, j: (tbp(b, j), 0)),
                  pl.BlockSpec((None, 8, LW), lambda b, j: (l, 0, 0)),
                  pl.BlockSpec((None, 8, LW), lambda b, j: (l, 0, 0)),
                  pl.BlockSpec((None, LW, LW), lambda b, j: (l, 0, 0)),
                  pl.BlockSpec((None, LW, LW), lambda b, j: (l, 0, 0))],
        out_specs=[pl.BlockSpec((ts, LW), lambda b, j: (tb(b, j), 0)),
                   pl.BlockSpec((ts, LW), lambda b, j: (tb(b, j), 0)),
                   pl.BlockSpec((ts, LW), lambda b, j: (tb(b, j), 0)),
                   pl.BlockSpec((ts, LW), lambda b, j: (tb(b, j), 0)),
                   pl.BlockSpec((ts, LW), lambda b, j: (tb(b, j), 0)),
                   pl.BlockSpec((8, LW), lambda b, j: (0, 0))],
        out_shape=[jax.ShapeDtypeStruct((T, LW), BF16),
                   jax.ShapeDtypeStruct((T, LW), BF16),
                   jax.ShapeDtypeStruct((T, LW), BF16),
                   jax.ShapeDtypeStruct((T, LW), BF16),
                   jax.ShapeDtypeStruct((T, LW), BF16),
                   jax.ShapeDtypeStruct((8, LW), F32)],
        scratch_shapes=[pltpu.VMEM((8, LW), F32), pltpu.VMEM((8, LW), F32), pltpu.VMEM((ts, LW), F32)],
        compiler_params=_cp(("arbitrary", "arbitrary")),
    )(dyl, proj, proj, proj, h, h, cw, vp, wr, wi)


def _fgate_fwd(proj, bfp, l, S):
    T = proj.shape[0]

    def body(x_ref, b_ref, o_ref):
        z = x_ref[...] + b_ref[0:1, :]
        v = jnp.minimum(z, 0.0) - _log1p(jnp.exp(-jnp.abs(z)))
        row = lax.broadcasted_iota(jnp.int32, (S, 128), 0)
        d = 1
        while d < S:
            v = v + jnp.where(row >= d, pltpu.roll(v, d, 0), 0.0)
            d *= 2
        o_ref[...] = v

    return pl.pallas_call(
        body, name=f"fgate_fwd_{l}",
        grid=(T // S,),
        in_specs=[pl.BlockSpec((S, 128), lambda b: (b, F_BLK)),
                  pl.BlockSpec((None, 8, 128), lambda b: (l, 0, 0))],
        out_specs=pl.BlockSpec((S, 128), lambda b: (b, 0)),
        out_shape=jax.ShapeDtypeStruct((T, 128), F32),
        compiler_params=_cp(("arbitrary",)),
    )(proj, bfp)


def _fgate_bwd(dcum, proj, bfp, l, S):
    T = proj.shape[0]

    def body(d_ref, x_ref, b_ref, o_ref, wacc_ref):
        @pl.when(pl.program_id(0) == 0)
        def _():
            wacc_ref[...] = jnp.zeros_like(wacc_ref)

        v = d_ref[...]
        row = lax.broadcasted_iota(jnp.int32, (S, 128), 0)
        d = 1
        while d < S:
            v = v + jnp.where(row < S - d, pltpu.roll(v, S - d, 0), 0.0)
            d *= 2
        z = x_ref[...] + b_ref[0:1, :]
        dz = v * _sigmoid(-z)
        o_ref[...] = dz.astype(BF16)
        wacc_ref[0:1, :] += _colsum(dz)

    return pl.pallas_call(
        body, name=f"fgate_bwd_{l}",
        grid=(T // S,),
        in_specs=[pl.BlockSpec((S, 128), lambda b: (b, 0)),
                  pl.BlockSpec((S, 128), lambda b: (b, F_BLK)),
                  pl.BlockSpec((None, 8, 128), lambda b: (l, 0, 0))],
        out_specs=[pl.BlockSpec((S, 128), lambda b: (b, 0)), pl.BlockSpec((8, 128), lambda b: (0, 0))],
        out_shape=[jax.ShapeDtypeStruct((T, 128), BF16), jax.ShapeDtypeStruct((8, 128), F32)],
        compiler_params=_cp(("arbitrary",)),
    )(dcum, proj, bfp)


def _logsig_parts(z):
    e = jnp.exp(-jnp.abs(z))
    l1p = jnp.log(1.0 + e)
    return e, jnp.minimum(z, 0.0) - l1p, -jnp.maximum(z, 0.0) - l1p


def _sb_fwd(q, k, v, l):
    B, H, nq, tq, _ = q.shape
    nk, tk = k.shape[2], k.shape[3]
    rr = tq // tk

    def body(q_ref, k_ref, v_ref, o_ref, t1_ref):
        tri = _tri(tk, "row_gt_col")
        ti = lax.broadcasted_iota(jnp.int32, (tq, 1), 0)
        si = lax.broadcasted_iota(jnp.int32, (1, tk), 1)

        def qloop(qb, carry):
            qq = q_ref[qb]
            tpos = qb * tq + ti
            nkb = (qb + 1) * rr

            def kloop(i, c):
                acc, run = c
                kb = nkb - 1 - i
                z = _dot_nt(qq, k_ref[kb]) * SCALE
                past = (kb * tk + si) < tpos
                _, lb, l1 = _logsig_parts(z)
                l1m = jnp.where(past, l1, 0.0)
                aft = _cumsum_mm(l1m, tri) + run
                w = jnp.where(past, jnp.exp(lb + aft), 0.0)
                acc = acc + _dot(w.astype(BF16), v_ref[kb])
                return acc, run + _rowsum(l1m)

            acc, run = lax.fori_loop(0, nkb, kloop, (jnp.zeros((tq, HD), F32), jnp.zeros((tq, 1), F32)))
            o_ref[qb] = acc
            t1_ref[qb] = run
            return carry

        lax.fori_loop(0, nq, qloop, 0)

    qs = pl.BlockSpec((None, None, nq, tq, HD), lambda b, h: (b, h, 0, 0, 0))
    ks = pl.BlockSpec((None, None, nk, tk, HD), lambda b, h: (b, h, 0, 0, 0))
    return pl.pallas_call(
        body, name=f"sb_fwd_{l}",
        grid=(B, H),
        in_specs=[qs, ks, ks],
        out_specs=[qs, pl.BlockSpec((None, None, nq, tq, 1), lambda b, h: (b, h, 0, 0, 0))],
        out_shape=[jax.ShapeDtypeStruct((B, H, nq, tq, HD), F32),
                   jax.ShapeDtypeStruct((B, H, nq, tq, 1), F32)],
        compiler_params=_cp(("arbitrary", "arbitrary"), VMEM_BIG),
    )(q, k, v)


def _sb_bwd(q, k, v, do, t1, l):
    B, H, nq, tq, _ = q.shape
    nk, tk = k.shape[2], k.shape[3]
    rr = tq // tk

    def body(q_ref, k_ref, v_ref, do_ref, t1_ref, dq_ref, dk_ref, dv_ref, dk_sc, dv_sc):
        dk_sc[...] = jnp.zeros_like(dk_sc)
        dv_sc[...] = jnp.zeros_like(dv_sc)
        tri_in = _tri(tk, "row_le_col")
        tri_ex = _tri(tk, "row_lt_col")
        ti = lax.broadcasted_iota(jnp.int32, (tq, 1), 0)
        si = lax.broadcasted_iota(jnp.int32, (1, tk), 1)

        def qloop(qb, carry):
            qq = q_ref[qb]
            dob = do_ref[qb].astype(BF16)
            tot = t1_ref[qb]
            tpos = qb * tq + ti
            nkb = (qb + 1) * rr

            def kloop(kb, c):
                dq, run1, rung = c
                kk = k_ref[kb]
                vv = v_ref[kb]
                z = _dot_nt(qq, kk) * SCALE
                past = (kb * tk + si) < tpos
                e, lb, l1 = _logsig_parts(z)
                l1m = jnp.where(past, l1, 0.0)
                aft = tot - (run1 + _cumsum_mm(l1m, tri_in))
                w = jnp.where(past, jnp.exp(lb + aft), 0.0)
                gm = w * _dot_nt(dob, vv)
                cpre = rung + _cumsum_mm(gm, tri_ex, parts=2)
                inv = 1.0 / (1.0 + e)
                sig = jnp.where(z >= 0.0, inv, e * inv)
                dz = jnp.where(past, gm * (1.0 - sig) - cpre * sig, 0.0).astype(BF16)
                dv_sc[kb] += _dot_tn(w.astype(BF16), dob)
                dk_sc[kb] += _dot_tn(dz, qq) * SCALE
                dq = dq + _dot(dz, kk) * SCALE
                return dq, run1 + _rowsum(l1m), rung + _rowsum(gm)

            z1 = jnp.zeros((tq, 1), F32)
            dq, _, _ = lax.fori_loop(0, nkb, kloop, (jnp.zeros((tq, HD), F32), z1, z1))
            dq_ref[qb] = dq.astype(BF16)
            return carry

        lax.fori_loop(0, nq, qloop, 0)
        dk_ref[...] = dk_sc[...].astype(BF16)
        dv_ref[...] = dv_sc[...].astype(BF16)

    qs = pl.BlockSpec((None, None, nq, tq, HD), lambda b, h: (b, h, 0, 0, 0))
    ks = pl.BlockSpec((None, None, nk, tk, HD), lambda b, h: (b, h, 0, 0, 0))
    return pl.pallas_call(
        body, name=f"sb_bwd_{l}",
        grid=(B, H),
        in_specs=[qs, ks, ks, qs, pl.BlockSpec((None, None, nq, tq, 1), lambda b, h: (b, h, 0, 0, 0))],
        out_specs=[qs, ks, ks],
        out_shape=[jax.ShapeDtypeStruct((B, H, nq, tq, HD), BF16),
                   jax.ShapeDtypeStruct((B, H, nk, tk, HD), BF16),
                   jax.ShapeDtypeStruct((B, H, nk, tk, HD), BF16)],
        scratch_shapes=[pltpu.VMEM((nk, tk, HD), F32), pltpu.VMEM((nk, tk, HD), F32)],
        compiler_params=_cp(("arbitrary", "arbitrary"), VMEM_BIG),
    )(q, k, v, do, t1)


def _fox_fwd(q, k, v, cq, ck, gqk, l):
    B, H, nq, tq, _ = q.shape
    nk, tk = k.shape[2], k.shape[3]
    rr = tq // tk

    def body(q_ref, k_ref, v_ref, cq_ref, ck_ref, g_ref, o_ref, lse_ref, fk_sc):
        g0 = g_ref[0:1, :]
        g1 = g_ref[1:2, :]

        def kprep(kb, c):
            kn, _ = _rms_rows(k_ref[kb])
            fk_sc[kb] = (kn * g1).astype(BF16)
            return c

        lax.fori_loop(0, nk, kprep, 0)
        ti = lax.broadcasted_iota(jnp.int32, (tq, 1), 0)
        si = lax.broadcasted_iota(jnp.int32, (1, tk), 1)

        def qloop(qb, carry):
            qn, _ = _rms_rows(q_ref[qb])
            fq = (qn * g0).astype(BF16)
            cqq = cq_ref[qb]
            tpos = qb * tq + ti

            def kloop(kb, c):
                m, lsum, acc = c
                s = _dot_nt(fq, fk_sc[kb]) * SCALE + cqq - ck_ref[kb]
                s = jnp.where((kb * tk + si) <= tpos, s, NEG)
                m2 = jnp.maximum(m, jnp.max(s, axis=1, keepdims=True))
                al = jnp.exp(m - m2)
                p = jnp.exp(s - m2)
                return m2, al * lsum + _rowsum(p), al * acc + _dot(p.astype(BF16), v_ref[kb])

            m, lsum, acc = lax.fori_loop(
                0, (qb + 1) * rr, kloop,
                (jnp.full((tq, 1), NEG, F32), jnp.zeros((tq, 1), F32), jnp.zeros((tq, HD), F32)))
            o_ref[qb] = acc / lsum
            lse_ref[qb] = m + jnp.log(lsum)
            return carry

        lax.fori_loop(0, nq, qloop, 0)

    qs = pl.BlockSpec((None, None, nq, tq, HD), lambda b, h: (b, h, 0, 0, 0))
    ks = pl.BlockSpec((None, None, nk, tk, HD), lambda b, h: (b, h, 0, 0, 0))
    cqs = pl.BlockSpec((None, None, nq, tq, 1), lambda b, h: (b, h, 0, 0, 0))
    cks = pl.BlockSpec((None, None, nk, 1, tk), lambda b, h: (b, h, 0, 0, 0))
    return pl.pallas_call(
        body, name=f"fox_fwd_{l}",
        grid=(B, H),
        in_specs=[qs, ks, ks, cqs, cks, pl.BlockSpec((None, 8, HD), lambda b, h: (l, 0, 0))],
        out_specs=[qs, cqs],
        out_shape=[jax.ShapeDtypeStruct((B, H, nq, tq, HD), F32),
                   jax.ShapeDtypeStruct((B, H, nq, tq, 1), F32)],
        scratch_shapes=[pltpu.VMEM((nk, tk, HD), BF16)],
        compiler_params=_cp(("arbitrary", "arbitrary"), VMEM_BIG),
    )(q, k, v, cq, ck, gqk)


def _fox_bwd(q, k, v, cq, ck, gqk, do, lse, l):
    B, H, nq, tq, _ = q.shape
    nk, tk = k.shape[2], k.shape[3]
    rr = tq // tk

    def body(q_ref, k_ref, v_ref, cq_ref, ck_ref, g_ref, do_ref, lse_ref,
             dq_ref, dk_ref, dv_ref, dc_ref, wacc_ref, fk_sc, dfk_sc, dv_sc):
        @pl.when((pl.program_id(0) == 0) & (pl.program_id(1) == 0))
        def _():
            wacc_ref[...] = jnp.zeros_like(wacc_ref)

        g0 = g_ref[0:1, :]
        g1 = g_ref[1:2, :]
        dfk_sc[...] = jnp.zeros_like(dfk_sc)
        dv_sc[...] = jnp.zeros_like(dv_sc)
        dc_ref[...] = jnp.zeros_like(dc_ref)

        def kprep(kb, c):
            kn, _ = _rms_rows(k_ref[kb])
            fk_sc[kb] = (kn * g1).astype(BF16)
            return c

        lax.fori_loop(0, nk, kprep, 0)
        ti = lax.broadcasted_iota(jnp.int32, (tq, 1), 0)
        si = lax.broadcasted_iota(jnp.int32, (1, tk), 1)

        def qloop(qb, carry):
            qn, qr = _rms_rows(q_ref[qb])
            fq = (qn * g0).astype(BF16)
            cqq = cq_ref[qb]
            lse = lse_ref[qb]
            dob = do_ref[qb].astype(BF16)
            tpos = qb * tq + ti

            def probs(kb):
                s = _dot_nt(fq, fk_sc[kb]) * SCALE + cqq - ck_ref[kb]
                p = jnp.where((kb * tk + si) <= tpos, jnp.exp(s - lse), 0.0)
                return p, _dot_nt(dob, v_ref[kb])

            def dloop(kb, acc):
                p, dp = probs(kb)
                return acc + _rowsum(p * dp)

            dlt = lax.fori_loop(0, (qb + 1) * rr, dloop, jnp.zeros((tq, 1), F32))

            def kloop(kb, dfq):
                fk = fk_sc[kb]
                p, dp = probs(kb)
                ds = p * (dp - dlt)
                dsb = ds.astype(BF16)
                dv_sc[kb] += _dot_tn(p.astype(BF16), dob)
                dfk_sc[kb] += _dot_tn(dsb, fq) * SCALE
                dc_ref[kb] += jnp.broadcast_to(-_colsum(ds), (8, tk))
                return dfq + _dot(dsb, fk) * SCALE

            dfq = lax.fori_loop(0, (qb + 1) * rr, kloop, jnp.zeros((tq, HD), F32))
            wacc_ref[0:1, :] += _colsum(dfq * qn)
            dq_ref[qb] = _rms_bwd(qn, qr, dfq * g0).astype(BF16)
            return carry

        lax.fori_loop(0, nq, qloop, 0)

        def kfin(kb, c):
            kn, kr = _rms_rows(k_ref[kb])
            dfk = dfk_sc[kb]
            wacc_ref[1:2, :] += _colsum(dfk * kn)
            dk_ref[kb] = _rms_bwd(kn, kr, dfk * g1).astype(BF16)
            return c

        lax.fori_loop(0, nk, kfin, 0)
        dv_ref[...] = dv_sc[...].astype(BF16)

    qs = pl.BlockSpec((None, None, nq, tq, HD), lambda b, h: (b, h, 0, 0, 0))
    ks = pl.BlockSpec((None, None, nk, tk, HD), lambda b, h: (b, h, 0, 0, 0))
    cqs = pl.BlockSpec((None, None, nq, tq, 1), lambda b, h: (b, h, 0, 0, 0))
    cks = pl.BlockSpec((None, None, nk, 1, tk), lambda b, h: (b, h, 0, 0, 0))
    return pl.pallas_call(
        body, name=f"fox_bwd_{l}",
        grid=(B, H),
        in_specs=[qs, ks, ks, cqs, cks, pl.BlockSpec((None, 8, HD), lambda b, h: (l, 0, 0)), qs, cqs],
        out_specs=[qs, ks, ks,
                   pl.BlockSpec((None, None, nk, 8, tk), lambda b, h: (b, h, 0, 0, 0)),
                   pl.BlockSpec((8, HD), lambda b, h: (0, 0))],
        out_shape=[jax.ShapeDtypeStruct((B, H, nq, tq, HD), BF16),
                   jax.ShapeDtypeStruct((B, H, nk, tk, HD), BF16),
                   jax.ShapeDtypeStruct((B, H, nk, tk, HD), BF16),
                   jax.ShapeDtypeStruct((B, H, nk, 8, tk), F32),
                   jax.ShapeDtypeStruct((8, HD), F32)],
        scratch_shapes=[pltpu.VMEM((nk, tk, HD), BF16), pltpu.VMEM((nk, tk, HD), F32),
                        pltpu.VMEM((nk, tk, HD), F32)],
        compiler_params=_cp(("arbitrary", "arbitrary"), VMEM_BIG),
    )(q, k, v, cq, ck, gqk, do, lse)


SBQ_BLK, SBK_BLK, SBV_BLK = 8, 10, 12
FXQ_BLK, FXK_BLK, FXV_BLK = 14, 16, 18
PAIR = 2 * HD


def _lane_masks():
    lane = lax.broadcasted_iota(jnp.int32, (1, PAIR), 1)
    return lane, lane < HD


def _pair_select(m0, a0, a1):
    return jnp.where(m0, a0, a1)


def _pair_split(x, m0):
    return jnp.where(m0, x, 0.0).astype(BF16), jnp.where(m0, 0.0, x).astype(BF16)


def _pair_mean(x, m0):
    s0 = _rowsum(jnp.where(m0, x, 0.0))
    s1 = _rowsum(x) - s0
    return jnp.where(m0, s0, s1) * (1.0 / HD)


def _pair_rms(x, m0):
    rstd = lax.rsqrt(_pair_mean(x * x, m0) + EPS)
    return x * rstd, rstd


def _pair_rms_bwd(xn, rstd, dyn, m0):
    return rstd * (dyn - xn * _pair_mean(dyn * xn, m0))


def _logsig2(z):
    l1p = jnp.log(1.0 + jnp.exp(-jnp.abs(z)))
    lb = jnp.minimum(z, 0.0) - l1p
    return lb, lb - z


def _rows(ref, blk, size):
    return ref[pl.ds(pl.multiple_of(blk * size, size), size), :]


def _sbp_fwd(proj, l, S):
    T = proj.shape[0]
    tq, tk = TQ_(S), TK_(S)
    assert tq == 2 * tk
    nq = S // tq

    def body(q_ref, k_ref, v_ref, o_ref, t1_ref, kb_sc, vb_sc):
        kb_sc[...] = k_ref[...].astype(BF16)
        vb_sc[...] = v_ref[...].astype(BF16)
        lane, m0 = _lane_masks()
        tri = _tri(tk, "row_gt_col")
        ti = lax.broadcasted_iota(jnp.int32, (tq, 1), 0)
        si = lax.broadcasted_iota(jnp.int32, (1, tk), 1)

        def qloop(qb, carry):
            qh = _pair_split(_rows(q_ref, qb, tq) * SCALE, m0)
            tpos = qb * tq + ti

            def step(kbs, c, masked):
                pre = []
                for h in range(2):
                    for kb in kbs:
                        z = _dot_nt(qh[h], _rows(kb_sc, kb, tk))
                        lb, l1 = _logsig2(z)
                        past = None
                        if masked:
                            past = (kb * tk + si) < tpos
                            l1 = jnp.where(past, l1, 0.0)
                        pre.append((lb, l1, _cumsum_mm(l1, tri, parts=2), past))
                out = []
                for h in range(2):
                    acc, run = c[h]
                    for n, kb in enumerate(kbs):
                        lb, l1, cs, past = pre[2 * h + n]
                        w = jnp.exp(lb + (cs + run))
                        if masked:
                            w = jnp.where(past, w, 0.0)
                        acc = acc + _dot(w.astype(BF16), _rows(vb_sc, kb, tk))
                        run = run + (cs[:, 0:1] + l1[:, 0:1])
                    out.append((acc, run))
                return tuple(out)

            zero = (jnp.zeros((tq, PAIR), F32), jnp.zeros((tq, 1), F32))
            c = step((2 * qb + 1, 2 * qb), (zero, zero), True)
            c = lax.fori_loop(0, qb, lambda i, cc: step((2 * (qb - i) - 1, 2 * (qb - i) - 2), cc, False), c)
            r0 = pl.multiple_of(qb * tq, tq)
            o_ref[pl.ds(r0, tq), :] = _pair_select(m0, c[0][0], c[1][0])
            t1_ref[pl.ds(r0, tq), :] = jnp.where(lane == 0, c[0][1], jnp.where(lane == 1, c[1][1], 0.0))
            return carry

        lax.fori_loop(0, nq, qloop, 0)

    def col(blk):
        return pl.BlockSpec((S, PAIR), lambda b, p: (b, blk + p))

    return pl.pallas_call(
        body, name=f"sb_fwd_{l}",
        grid=(T // S, 2),
        in_specs=[col(SBQ_BLK), col(SBK_BLK), col(SBV_BLK)],
        out_specs=[col(0), col(0)],
        out_shape=[jax.ShapeDtypeStruct((T, AW), F32), jax.ShapeDtypeStruct((T, AW), F32)],
        scratch_shapes=[pltpu.VMEM((S, PAIR), BF16), pltpu.VMEM((S, PAIR), BF16)],
        compiler_params=_cp(("arbitrary", "arbitrary"), VMEM_BIG),
    )(proj, proj, proj)


def _sbp_bwd(proj, do, t1, l, S):
    T = proj.shape[0]
    tq, tk = TQ_(S), TK_(S)
    assert tq == 2 * tk
    nq = S // tq

    def body(q_ref, k_ref, v_ref, do_ref, t1_ref, dq_ref, dk_ref, dv_ref, kb_sc, vb_sc, dk_sc, dv_sc):
        kb_sc[...] = k_ref[...].astype(BF16)
        vb_sc[...] = v_ref[...].astype(BF16)
        dk_sc[...] = jnp.zeros_like(dk_sc)
        dv_sc[...] = jnp.zeros_like(dv_sc)
        _, m0 = _lane_masks()
        tri_in = _tri(tk, "row_le_col")
        tri_ex = _tri(tk, "row_lt_col")
        ti = lax.broadcasted_iota(jnp.int32, (tq, 1), 0)
        si = lax.broadcasted_iota(jnp.int32, (1, tk), 1)

        def qloop(qb, carry):
            qh = _pair_split(_rows(q_ref, qb, tq) * SCALE, m0)
            doh = _pair_split(_rows(do_ref, qb, tq), m0)
            t1v = _rows(t1_ref, qb, tq)
            tot = (t1v[:, 0:1], t1v[:, 1:2])
            tpos = qb * tq + ti

            def step(kbs, c, masked):
                pre = []
                for h in range(2):
                    for kb in kbs:
                        kk = _rows(kb_sc, kb, tk)
                        z = _dot_nt(qh[h], kk)
                        lb, l1 = _logsig2(z)
                        past = None
                        if masked:
                            past = (kb * tk + si) < tpos
                            l1 = jnp.where(past, l1, 0.0)
                        sig = jnp.exp(lb)
                        pre.append((lb, sig, _cumsum_mm(l1, tri_in), _dot_nt(doh[h], _rows(vb_sc, kb, tk)), past, kk))
                out = []
                for h in range(2):
                    dq, run1, rung = c[h]
                    for n, kb in enumerate(kbs):
                        lb, sig, p1, dw, past, kk = pre[2 * h + n]
                        w = jnp.exp(lb + (tot[h] - (run1 + p1)))
                        if masked:
                            w = jnp.where(past, w, 0.0)
                        gm = w * dw
                        cx = _cumsum_mm(gm, tri_ex, parts=2)
                        dz = gm - (gm + (rung + cx)) * sig
                        if masked:
                            dz = jnp.where(past, dz, 0.0)
                        dz = dz.astype(BF16)
                        r = pl.ds(pl.multiple_of(kb * tk, tk), tk)
                        dv_sc[r, :] += _dot_tn(w.astype(BF16), doh[h])
                        dk_sc[r, :] += _dot_tn(dz, qh[h])
                        dq = dq + _dot(dz, kk)
                        run1 = run1 + p1[:, tk - 1:tk]
                        rung = rung + (cx[:, tk - 1:tk] + gm[:, tk - 1:tk])
                    out.append((dq, run1, rung))
                return tuple(out)

            z1 = jnp.zeros((tq, 1), F32)
            zero = (jnp.zeros((tq, PAIR), F32), z1, z1)
            c = lax.fori_loop(0, qb, lambda i, cc: step((2 * i, 2 * i + 1), cc, False), (zero, zero))
            c = step((2 * qb, 2 * qb + 1), c, True)
            r0 = pl.multiple_of(qb * tq, tq)
            dq_ref[pl.ds(r0, tq), :] = (_pair_select(m0, c[0][0], c[1][0]) * SCALE).astype(BF16)
            return carry

        lax.fori_loop(0, nq, qloop, 0)
        dk_ref[...] = dk_sc[...].astype(BF16)
        dv_ref[...] = dv_sc[...].astype(BF16)

    def col(blk):
        return pl.BlockSpec((S, PAIR), lambda b, p: (b, blk + p))

    sh = jax.ShapeDtypeStruct((T, AW), BF16)
    return pl.pallas_call(
        body, name=f"sb_bwd_{l}",
        grid=(T // S, 2),
        in_specs=[col(SBQ_BLK), col(SBK_BLK), col(SBV_BLK), col(0), col(0)],
        out_specs=[col(0), col(0), col(0)],
        out_shape=[sh, sh, sh],
        scratch_shapes=[pltpu.VMEM((S, PAIR), BF16), pltpu.VMEM((S, PAIR), BF16),
                        pltpu.VMEM((S, PAIR), F32), pltpu.VMEM((S, PAIR), F32)],
        compiler_params=_cp(("arbitrary", "arbitrary"), VMEM_BIG),
    )(proj, proj, proj, do, t1)


def _foxp_fwd(proj, cum, ck, gqk2, l, S):
    T = proj.shape[0]
    tq, tk = TQ_(S), TK_(S)
    assert tq == 2 * tk
    nq, nk = S // tq, S // tk

    def body(q_ref, k_ref, v_ref, cum_ref, ck_ref, g_ref, o_ref, nl_ref, fk_sc, vb_sc):
        lane, m0 = _lane_masks()
        p = pl.program_id(1)
        kn, _ = _pair_rms(k_ref[...], m0)
        fk_sc[...] = (kn * g_ref[1:2, :]).astype(BF16)
        vb_sc[...] = v_ref[...].astype(BF16)
        ti = lax.broadcasted_iota(jnp.int32, (tq, 1), 0)
        si = lax.broadcasted_iota(jnp.int32, (1, tk), 1)

        def qloop(qb, carry):
            qn, _ = _pair_rms(_rows(q_ref, qb, tq), m0)
            fqh = _pair_split(qn * (g_ref[0:1, :] * SCALE), m0)
            cumv = _rows(cum_ref, qb, tq)
            cq = [_rowsum(jnp.where(lane == 2 * p + h, cumv, 0.0)) for h in range(2)]
            tpos = qb * tq + ti

            def step(kbs, c, masked):
                out = []
                for h in range(2):
                    m, lsum, acc = c[h]
                    ss = []
                    for kb in kbs:
                        s = _dot_nt(fqh[h], _rows(fk_sc, kb, tk)) + (cq[h] - ck_ref[h, kb])
                        if masked:
                            s = jnp.where((kb * tk + si) <= tpos, s, NEG)
                        ss.append(s)
                    m2 = jnp.maximum(m, jnp.maximum(jnp.max(ss[0], axis=1, keepdims=True),
                                                    jnp.max(ss[1], axis=1, keepdims=True)))
                    al = jnp.exp(m - m2)
                    lsum = al * lsum
                    acc = al * acc
                    for s, kb in zip(ss, kbs):
                        pr = jnp.exp(s - m2)
                        lsum = lsum + _rowsum(pr)
                        acc = acc + _dot(pr.astype(BF16), _rows(vb_sc, kb, tk))
                    out.append((m2, lsum, acc))
                return tuple(out)

            zero = (jnp.full((tq, 1), NEG, F32), jnp.zeros((tq, 1), F32), jnp.zeros((tq, PAIR), F32))
            c = lax.fori_loop(0, qb, lambda i, cc: step((2 * i, 2 * i + 1), cc, False), (zero, zero))
            c = step((2 * qb, 2 * qb + 1), c, True)
            r0 = pl.multiple_of(qb * tq, tq)
            o_ref[pl.ds(r0, tq), :] = _pair_select(m0, c[0][2] / c[0][1], c[1][2] / c[1][1])
            nl = [cq[h] - (c[h][0] + jnp.log(c[h][1])) for h in range(2)]
            nl_ref[pl.ds(r0, tq), :] = jnp.where(lane == 0, nl[0], jnp.where(lane == 1, nl[1], 0.0))
            return carry

        lax.fori_loop(0, nq, qloop, 0)

    def col(blk):
        return pl.BlockSpec((S, PAIR), lambda b, p: (b, blk + p))

    return pl.pallas_call(
        body, name=f"fox_fwd_{l}",
        grid=(T // S, 2),
        in_specs=[col(FXQ_BLK), col(FXK_BLK), col(FXV_BLK),
                  pl.BlockSpec((S, 128), lambda b, p: (b, 0)),
                  pl.BlockSpec((None, 2, nk, 1, tk), lambda b, p: (b, p, 0, 0, 0)),
                  pl.BlockSpec((None, 8, PAIR), lambda b, p: (l, 0, 0))],
        out_specs=[col(0), col(0)],
        out_shape=[jax.ShapeDtypeStruct((T, AW), F32), jax.ShapeDtypeStruct((T, AW), F32)],
        scratch_shapes=[pltpu.VMEM((S, PAIR), BF16), pltpu.VMEM((S, PAIR), BF16)],
        compiler_params=_cp(("arbitrary", "arbitrary"), VMEM_BIG),
    )(proj, proj, proj, cum, ck, gqk2)


def _foxp_bwd(proj, do, nl, ck, gqk2, l, S):
    T = proj.shape[0]
    tq, tk = TQ_(S), TK_(S)
    assert tq == 2 * tk
    nq, nk = S // tq, S // tk

    def body(q_ref, k_ref, v_ref, do_ref, nl_ref, ck_ref, g_ref,
             dq_ref, dk_ref, dv_ref, dc_ref, wacc_ref, fk_sc, vb_sc, dfk_sc, dv_sc):
        @pl.when((pl.program_id(0) == 0) & (pl.program_id(1) == 0))
        def _():
            wacc_ref[...] = jnp.zeros_like(wacc_ref)

        _, m0 = _lane_masks()
        g0 = g_ref[0:1, :]
        g1 = g_ref[1:2, :]
        kn, kr = _pair_rms(k_ref[...], m0)
        fk_sc[...] = (kn * g1).astype(BF16)
        vb_sc[...] = v_ref[...].astype(BF16)
        dfk_sc[...] = jnp.zeros_like(dfk_sc)
        dv_sc[...] = jnp.zeros_like(dv_sc)
        dc_ref[...] = jnp.zeros_like(dc_ref)
        ti = lax.broadcasted_iota(jnp.int32, (tq, 1), 0)
        si = lax.broadcasted_iota(jnp.int32, (1, tk), 1)

        def qloop(qb, carry):
            qn, qr = _pair_rms(_rows(q_ref, qb, tq), m0)
            fqh = _pair_split(qn * (g0 * SCALE), m0)
            doh = _pair_split(_rows(do_ref, qb, tq), m0)
            nlv = _rows(nl_ref, qb, tq)
            cql = (nlv[:, 0:1], nlv[:, 1:2])
            tpos = qb * tq + ti

            def probs(h, kb, masked):
                s = _dot_nt(fqh[h], _rows(fk_sc, kb, tk)) + (cql[h] - ck_ref[h, kb])
                pr = jnp.exp(s)
                if masked:
                    pr = jnp.where((kb * tk + si) <= tpos, pr, 0.0)
                return pr, _dot_nt(doh[h], _rows(vb_sc, kb, tk))

            def dstep(kbs, c, masked):
                out = []
                for h in range(2):
                    acc = c[h]
                    for kb in kbs:
                        pr, dp = probs(h, kb, masked)
                        acc = acc + _rowsum(pr * dp)
                    out.append(acc)
                return tuple(out)

            z1 = jnp.zeros((tq, 1), F32)
            dlt = lax.fori_loop(0, qb, lambda i, cc: dstep((2 * i, 2 * i + 1), cc, False), (z1, z1))
            dlt = dstep((2 * qb, 2 * qb + 1), dlt, True)

            def step(kbs, c, masked):
                out = []
                for h in range(2):
                    dfq = c[h]
                    for kb in kbs:
                        pr, dp = probs(h, kb, masked)
                        ds = pr * (dp - dlt[h])
                        dsb = ds.astype(BF16)
                        r = pl.ds(pl.multiple_of(kb * tk, tk), tk)
                        dv_sc[r, :] += _dot_tn(pr.astype(BF16), doh[h])
                        dfk_sc[r, :] += _dot_tn(dsb, fqh[h])
                        dc_ref[h, kb] += jnp.broadcast_to(-_colsum(ds), (8, tk))
                        dfq = dfq + _dot(dsb, _rows(fk_sc, kb, tk))
                    out.append(dfq)
                return tuple(out)

            zq = jnp.zeros((tq, PAIR), F32)
            c = lax.fori_loop(0, qb, lambda i, cc: step((2 * i, 2 * i + 1), cc, False), (zq, zq))
            c = step((2 * qb, 2 * qb + 1), c, True)
            dfq = _pair_select(m0, c[0], c[1]) * SCALE
            wacc_ref[0:1, :] += _colsum(dfq * qn)
            r0 = pl.multiple_of(qb * tq, tq)
            dq_ref[pl.ds(r0, tq), :] = _pair_rms_bwd(qn, qr, dfq * g0, m0).astype(BF16)
            return carry

        lax.fori_loop(0, nq, qloop, 0)
        dfk = dfk_sc[...]
        wacc_ref[1:2, :] += _colsum(dfk * kn)
        dk_ref[...] = _pair_rms_bwd(kn, kr, dfk * g1, m0).astype(BF16)
        dv_ref[...] = dv_sc[...].astype(BF16)

    def col(blk):
        return pl.BlockSpec((S, PAIR), lambda b, p: (b, blk + p))

    sh = jax.ShapeDtypeStruct((T, AW), BF16)
    return pl.pallas_call(
        body, name=f"fox_bwd_{l}",
        grid=(T // S, 2),
        in_specs=[col(FXQ_BLK), col(FXK_BLK), col(FXV_BLK), col(0), col(0),
                  pl.BlockSpec((None, 2, nk, 1, tk), lambda b, p: (b, p, 0, 0, 0)),
                  pl.BlockSpec((None, 8, PAIR), lambda b, p: (l, 0, 0))],
        out_specs=[col(0), col(0), col(0),
                   pl.BlockSpec((None, 2, nk, 8, tk), lambda b, p: (b, p, 0, 0, 0)),
                   pl.BlockSpec((8, PAIR), lambda b, p: (0, 0))],
        out_shape=[sh, sh, sh,
                   jax.ShapeDtypeStruct((T // S, NH, nk, 8, tk), F32),
                   jax.ShapeDtypeStruct((8, PAIR), F32)],
        scratch_shapes=[pltpu.VMEM((S, PAIR), BF16), pltpu.VMEM((S, PAIR), BF16),
                        pltpu.VMEM((S, PAIR), F32), pltpu.VMEM((S, PAIR), F32)],
        compiler_params=_cp(("arbitrary", "arbitrary"), VMEM_BIG),
    )(proj, proj, proj, do, nl, ck, gqk2)


def _transpose_blocks(src_ref, dst_sc, nblk, blk):
    for kb in range(nblk):
        dst_sc[kb] = src_ref[kb * blk:(kb + 1) * blk, :].astype(F32).T.astype(BF16)


def _sbq_fwd(proj, l, S):
    T = proj.shape[0]
    tb = TQ_(S)
    nb = S // tb

    def body(q_ref, k_ref, v_ref, o_ref, t1_ref, kt_sc, vb_sc):
        _transpose_blocks(k_ref, kt_sc, nb, tb)
        vb_sc[...] = v_ref[...].astype(BF16)
        lane, m0 = _lane_masks()
        tri = _tri(tb, "row_gt_col")
        past = lax.broadcasted_iota(jnp.int32, (tb, tb), 1) < lax.broadcasted_iota(jnp.int32, (tb, tb), 0)

        def qloop(qb, carry):
            qh = _pair_split(_rows(q_ref, qb, tb) * SCALE, m0)

            def scores(kb):
                return tuple(_dot(qh[h], kt_sc[kb]) for h in range(2))

            def block(kb, kb_next, z, c, masked):
                mid = []
                for h in range(2):
                    lb, l1 = _logsig2(z[h])
                    if masked:
                        l1 = jnp.where(past, l1, 0.0)
                    mid.append((lb, l1, _cumsum_mm(l1, tri, parts=2)))
                z_next = scores(kb_next)
                pv, runs = [], []
                for h in range(2):
                    lb, l1, cs = mid[h]
                    w = jnp.exp(lb + (cs + c[h][1]))
                    if masked:
                        w = jnp.where(past, w, 0.0)
                    pv.append(_dot(w.astype(BF16), _rows(vb_sc, kb, tb)))
                    runs.append(c[h][1] + (cs[:, 0:1] + l1[:, 0:1]))
                return z_next, tuple((c[h][0] + pv[h], runs[h]) for h in range(2))

            zero = (jnp.zeros((tb, PAIR), F32), jnp.zeros((tb, 1), F32))
            z, c = block(qb, jnp.maximum(qb - 1, 0), scores(qb), (zero, zero), True)

            def off_diag(i, zc):
                kb = qb - 1 - i
                return block(kb, jnp.maximum(kb - 1, 0), zc[0], zc[1], False)

            _, c = lax.fori_loop(0, qb, off_diag, (z, c))
            r0 = pl.multiple_of(qb * tb, tb)
            o_ref[pl.ds(r0, tb), :] = _pair_select(m0, c[0][0], c[1][0])
            t1_ref[pl.ds(r0, tb), :] = jnp.where(lane == 0, c[0][1], jnp.where(lane == 1, c[1][1], 0.0))
            return carry

        lax.fori_loop(0, nb, qloop, 0)

    def col(blk):
        return pl.BlockSpec((S, PAIR), lambda b, p: (b, blk + p))

    return pl.pallas_call(
        body, name=f"sb_fwd_{l}",
        grid=(T // S, 2),
        in_specs=[col(SBQ_BLK), col(SBK_BLK), col(SBV_BLK)],
        out_specs=[col(0), col(0)],
        out_shape=[jax.ShapeDtypeStruct((T, AW), F32), jax.ShapeDtypeStruct((T, AW), F32)],
        scratch_shapes=[pltpu.VMEM((nb, PAIR, tb), BF16), pltpu.VMEM((S, PAIR), BF16)],
        compiler_params=_cp(("arbitrary", "arbitrary"), VMEM_BIG),
    )(proj, proj, proj)


def _sbq_bwd(proj, do, t1, l, S):
    T = proj.shape[0]
    tb = TQ_(S)
    nb = S // tb

    def body(q_ref, k_ref, v_ref, do_ref, t1_ref, dq_ref, dk_ref, dv_ref, kb_sc, kt_sc, vt_sc, dkt_sc, dvt_sc):
        kb_sc[...] = k_ref[...].astype(BF16)
        _transpose_blocks(k_ref, kt_sc, nb, tb)
        _transpose_blocks(v_ref, vt_sc, nb, tb)
        dkt_sc[...] = jnp.zeros_like(dkt_sc)
        dvt_sc[...] = jnp.zeros_like(dvt_sc)
        _, m0 = _lane_masks()
        mt0 = lax.broadcasted_iota(jnp.int32, (PAIR, 1), 0) < HD
        tri_in = _tri(tb, "row_le_col")
        tri_ex = _tri(tb, "row_lt_col")
        past = lax.broadcasted_iota(jnp.int32, (tb, tb), 1) < lax.broadcasted_iota(jnp.int32, (tb, tb), 0)

        def qloop(qb, carry):
            qf = _rows(q_ref, qb, tb) * SCALE
            dof = _rows(do_ref, qb, tb)
            qh = _pair_split(qf, m0)
            doh = _pair_split(dof, m0)
            qth = _pair_split(qf.T, mt0)
            doth = _pair_split(dof.T, mt0)
            t1v = _rows(t1_ref, qb, tb)
            tot = (t1v[:, 0:1], t1v[:, 1:2])

            def block(kb, c, masked):
                hs = range(2)
                z = [_dot(qh[h], kt_sc[kb]) for h in hs]
                dw = [_dot(doh[h], vt_sc[kb]) for h in hs]
                st = []
                for h in hs:
                    lb, l1 = _logsig2(z[h])
                    if masked:
                        l1 = jnp.where(past, l1, 0.0)
                    st.append((lb, _cumsum_mm(l1, tri_in)))
                mid = []
                for h in hs:
                    lb, p1 = st[h]
                    w = jnp.exp(lb + (tot[h] - (c[h][1] + p1)))
                    if masked:
                        w = jnp.where(past, w, 0.0)
                    gm = w * dw[h]
                    mid.append((w.astype(BF16), gm, _cumsum_mm(gm, tri_ex, parts=2)))
                out = []
                for h in hs:
                    dq, run1, rung = c[h]
                    wb, gm, cx = mid[h]
                    dz = gm - (gm + (rung + cx)) * jnp.exp(st[h][0])
                    if masked:
                        dz = jnp.where(past, dz, 0.0)
                    dz = dz.astype(BF16)
                    dvt_sc[kb] += _dot(doth[h], wb)
                    dkt_sc[kb] += _dot(qth[h], dz)
                    dq = dq + _dot(dz, _rows(kb_sc, kb, tb))
                    p1 = st[h][1]
                    out.append((dq, run1 + p1[:, tb - 1:tb], rung + (cx[:, tb - 1:tb] + gm[:, tb - 1:tb])))
                return tuple(out)

            z1 = jnp.zeros((tb, 1), F32)
            zero = (jnp.zeros((tb, PAIR), F32), z1, z1)
            c = lax.fori_loop(0, qb, lambda i, cc: block(i, cc, False), (zero, zero))
            c = block(qb, c, True)
            r0 = pl.multiple_of(qb * tb, tb)
            dq_ref[pl.ds(r0, tb), :] = (_pair_select(m0, c[0][0], c[1][0]) * SCALE).astype(BF16)
            return carry

        lax.fori_loop(0, nb, qloop, 0)
        for kb in range(nb):
            dk_ref[kb * tb:(kb + 1) * tb, :] = dkt_sc[kb].T.astype(BF16)
            dv_ref[kb * tb:(kb + 1) * tb, :] = dvt_sc[kb].T.astype(BF16)

    def col(blk):
        return pl.BlockSpec((S, PAIR), lambda b, p: (b, blk + p))

    sh = jax.ShapeDtypeStruct((T, AW), BF16)
    return pl.pallas_call(
        body, name=f"sb_bwd_{l}",
        grid=(T // S, 2),
        in_specs=[col(SBQ_BLK), col(SBK_BLK), col(SBV_BLK), col(0), col(0)],
        out_specs=[col(0), col(0), col(0)],
        out_shape=[sh, sh, sh],
        scratch_shapes=[pltpu.VMEM((S, PAIR), BF16), pltpu.VMEM((nb, PAIR, tb), BF16), pltpu.VMEM((nb, PAIR, tb), BF16),
                        pltpu.VMEM((nb, PAIR, tb), F32), pltpu.VMEM((nb, PAIR, tb), F32)],
        compiler_params=_cp(("arbitrary", "arbitrary"), VMEM_BIG),
    )(proj, proj, proj, do, t1)


def _foxq_fwd(proj, cum, ck, gqk2, l, S):
    T = proj.shape[0]
    tb = TQ_(S)
    nb = S // tb

    def body(q_ref, k_ref, v_ref, cum_ref, ck_ref, g_ref, o_ref, nl_ref, fk_sc, fkt_sc, vb_sc):
        lane, m0 = _lane_masks()
        p = pl.program_id(1)
        kn, _ = _pair_rms(k_ref[...], m0)
        fk_sc[...] = kn * g_ref[1:2, :]
        _transpose_blocks(fk_sc, fkt_sc, nb, tb)
        vb_sc[...] = v_ref[...].astype(BF16)
        causal = lax.broadcasted_iota(jnp.int32, (tb, tb), 1) <= lax.broadcasted_iota(jnp.int32, (tb, tb), 0)

        def qloop(qb, carry):
            qn, _ = _pair_rms(_rows(q_ref, qb, tb), m0)
            fqh = _pair_split(qn * (g_ref[0:1, :] * SCALE), m0)
            cumv = _rows(cum_ref, qb, tb)
            cq = [_rowsum(jnp.where(lane == 2 * p + h, cumv, 0.0)) for h in range(2)]

            def scores(kb):
                return tuple(_dot(fqh[h], fkt_sc[kb]) for h in range(2))

            def block(kb, kb_next, qk, c, masked):
                st = []
                for h in range(2):
                    s = qk[h] + (cq[h] - ck_ref[h, kb])
                    if masked:
                        s = jnp.where(causal, s, NEG)
                    m2 = jnp.maximum(c[h][0], jnp.max(s, axis=1, keepdims=True))
                    pr = jnp.exp(s - m2)
                    st.append((m2, pr, _dot(pr.astype(BF16), _rows(vb_sc, kb, tb))))
                qk_next = scores(kb_next)
                out = []
                for h in range(2):
                    m, lsum, acc = c[h]
                    m2, pr, pv = st[h]
                    al = jnp.exp(m - m2)
                    out.append((m2, al * lsum + _rowsum(pr), al * acc + pv))
                return qk_next, tuple(out)

            zero = (jnp.full((tb, 1), NEG, F32), jnp.zeros((tb, 1), F32), jnp.zeros((tb, PAIR), F32))

            def off_diag(i, sc):
                return block(i, i + 1, sc[0], sc[1], False)

            qk, c = lax.fori_loop(0, qb, off_diag, (scores(0), (zero, zero)))
            _, c = block(qb, qb, qk, c, True)
            r0 = pl.multiple_of(qb * tb, tb)
            o_ref[pl.ds(r0, tb), :] = _pair_select(m0, c[0][2] / c[0][1], c[1][2] / c[1][1])
            nl = [cq[h] - (c[h][0] + jnp.log(c[h][1])) for h in range(2)]
            nl_ref[pl.ds(r0, tb), :] = jnp.where(lane == 0, nl[0], jnp.where(lane == 1, nl[1], 0.0))
            return carry

        lax.fori_loop(0, nb, qloop, 0)

    def col(blk):
        return pl.BlockSpec((S, PAIR), lambda b, p: (b, blk + p))

    return pl.pallas_call(
        body, name=f"fox_fwd_{l}",
        grid=(T // S, 2),
        in_specs=[col(FXQ_BLK), col(FXK_BLK), col(FXV_BLK),
                  pl.BlockSpec((S, 128), lambda b, p: (b, 0)),
                  pl.BlockSpec((None, 2, nb, 1, tb), lambda b, p: (b, p, 0, 0, 0)),
                  pl.BlockSpec((None, 8, PAIR), lambda b, p: (l, 0, 0))],
        out_specs=[col(0), col(0)],
        out_shape=[jax.ShapeDtypeStruct((T, AW), F32), jax.ShapeDtypeStruct((T, AW), F32)],
        scratch_shapes=[pltpu.VMEM((S, PAIR), F32), pltpu.VMEM((nb, PAIR, tb), BF16), pltpu.VMEM((S, PAIR), BF16)],
        compiler_params=_cp(("arbitrary", "arbitrary"), VMEM_BIG),
    )(proj, proj, proj, cum, ck, gqk2)


def _foxq_bwd(proj, do, nl, ck, gqk2, l, S):
    T = proj.shape[0]
    tb = TQ_(S)
    nb = S // tb

    def body(q_ref, k_ref, v_ref, do_ref, nl_ref, ck_ref, g_ref,
             dq_ref, dk_ref, dv_ref, dc_ref, wacc_ref, fk_sc, fkt_sc, vt_sc, dfkt_sc, dvt_sc):
        @pl.when((pl.program_id(0) == 0) & (pl.program_id(1) == 0))
        def _():
            wacc_ref[...] = jnp.zeros_like(wacc_ref)

        _, m0 = _lane_masks()
        mt0 = lax.broadcasted_iota(jnp.int32, (PAIR, 1), 0) < HD
        g0 = g_ref[0:1, :]
        g1 = g_ref[1:2, :]
        fk_sc[...] = (_pair_rms(k_ref[...], m0)[0] * g1).astype(BF16)
        _transpose_blocks(fk_sc, fkt_sc, nb, tb)
        _transpose_blocks(v_ref, vt_sc, nb, tb)
        dfkt_sc[...] = jnp.zeros_like(dfkt_sc)
        dvt_sc[...] = jnp.zeros_like(dvt_sc)
        dc_ref[...] = jnp.zeros_like(dc_ref)
        causal = lax.broadcasted_iota(jnp.int32, (tb, tb), 1) <= lax.broadcasted_iota(jnp.int32, (tb, tb), 0)

        def qloop(qb, carry):
            qn, qr = _pair_rms(_rows(q_ref, qb, tb), m0)
            fqf = qn * (g0 * SCALE)
            dof = _rows(do_ref, qb, tb)
            fqh = _pair_split(fqf, m0)
            doh = _pair_split(dof, m0)
            fqth = _pair_split(fqf.T, mt0)
            doth = _pair_split(dof.T, mt0)
            nlv = _rows(nl_ref, qb, tb)
            cql = (nlv[:, 0:1], nlv[:, 1:2])

            def probs(kb, masked):
                qk = [_dot(fqh[h], fkt_sc[kb]) for h in range(2)]
                dp = [_dot(doh[h], vt_sc[kb]) for h in range(2)]
                pr = []
                for h in range(2):
                    e = jnp.exp(qk[h] + (cql[h] - ck_ref[h, kb]))
                    pr.append(jnp.where(causal, e, 0.0) if masked else e)
                return pr, dp

            def dblock(kb, c, masked):
                pr, dp = probs(kb, masked)
                return tuple(c[h] + _rowsum(pr[h] * dp[h]) for h in range(2))

            z1 = jnp.zeros((tb, 1), F32)
            dlt = lax.fori_loop(0, qb, lambda i, cc: dblock(i, cc, False), (z1, z1))
            dlt = dblock(qb, dlt, True)

            def block(kb, c, masked):
                pr, dp = probs(kb, masked)
                out = []
                for h in range(2):
                    ds = pr[h] * (dp[h] - dlt[h])
                    dsb = ds.astype(BF16)
                    dvt_sc[kb] += _dot(doth[h], pr[h].astype(BF16))
                    dfkt_sc[kb] += _dot(fqth[h], dsb)
                    dc_ref[h, kb] += jnp.broadcast_to(-_colsum(ds), (8, tb))
                    out.append(c[h] + _dot(dsb, _rows(fk_sc, kb, tb)))
                return tuple(out)

            zq = jnp.zeros((tb, PAIR), F32)
            c = lax.fori_loop(0, qb, lambda i, cc: block(i, cc, False), (zq, zq))
            c = block(qb, c, True)
            dfq = _pair_select(m0, c[0], c[1]) * SCALE
            wacc_ref[0:1, :] += _colsum(dfq * qn)
            r0 = pl.multiple_of(qb * tb, tb)
            dq_ref[pl.ds(r0, tb), :] = _pair_rms_bwd(qn, qr, dfq * g0, m0).astype(BF16)
            return carry

        lax.fori_loop(0, nb, qloop, 0)
        for kb in range(nb):
            rows = slice(kb * tb, (kb + 1) * tb)
            dfk = dfkt_sc[kb].T
            knb, krb = _pair_rms(k_ref[rows, :], m0)
            wacc_ref[1:2, :] += _colsum(dfk * knb)
            dk_ref[rows, :] = _pair_rms_bwd(knb, krb, dfk * g1, m0).astype(BF16)
            dv_ref[rows, :] = dvt_sc[kb].T.astype(BF16)

    def col(blk):
        return pl.BlockSpec((S, PAIR), lambda b, p: (b, blk + p))

    sh = jax.ShapeDtypeStruct((T, AW), BF16)
    return pl.pallas_call(
        body, name=f"fox_bwd_{l}",
        grid=(T // S, 2),
        in_specs=[col(FXQ_BLK), col(FXK_BLK), col(FXV_BLK), col(0), col(0),
                  pl.BlockSpec((None, 2, nb, 1, tb), lambda b, p: (b, p, 0, 0, 0)),
                  pl.BlockSpec((None, 8, PAIR), lambda b, p: (l, 0, 0))],
        out_specs=[col(0), col(0), col(0),
                   pl.BlockSpec((None, 2, nb, 8, tb), lambda b, p: (b, p, 0, 0, 0)),
                   pl.BlockSpec((8, PAIR), lambda b, p: (0, 0))],
        out_shape=[sh, sh, sh,
                   jax.ShapeDtypeStruct((T // S, NH, nb, 8, tb), F32),
                   jax.ShapeDtypeStruct((8, PAIR), F32)],
        scratch_shapes=[pltpu.VMEM((S, PAIR), BF16), pltpu.VMEM((nb, PAIR, tb), BF16), pltpu.VMEM((nb, PAIR, tb), BF16),
                        pltpu.VMEM((nb, PAIR, tb), F32), pltpu.VMEM((nb, PAIR, tb), F32)],
        compiler_params=_cp(("arbitrary", "arbitrary"), VMEM_BIG),
    )(proj, proj, proj, do, nl, ck, gqk2)


def _ada_fwd(c_all, w_ada, b_cols):
    nb, ncol = c_all.shape[0], w_ada.shape[2]
    tn = _tile(ncol, 768)

    def body(c_ref, w_ref, b_ref, o_ref):
        c = c_ref[...]
        ca = (c * _sigmoid(c)).astype(BF16)
        o_ref[...] = _dot(ca, w_ref[...].astype(BF16)) + b_ref[...]

    return pl.pallas_call(
        body, name="ada_fwd",
        grid=(2, ncol // tn),
        in_specs=[pl.BlockSpec((nb, D), lambda l, n: (0, 0)),
                  pl.BlockSpec((None, D, tn), lambda l, n: (l, 0, n)),
                  pl.BlockSpec((None, 1, tn), lambda l, n: (l, 0, n))],
        out_specs=pl.BlockSpec((None, nb, tn), lambda l, n: (l, 0, n)),
        out_shape=jax.ShapeDtypeStruct((2, nb, ncol), F32),
        compiler_params=_cp(("arbitrary", "arbitrary")),
    )(c_all, w_ada, b_cols)


def _ada_bwd(c_all, dmod_cols):
    nb, ncol = c_all.shape[0], dmod_cols.shape[2]
    tn = _tile(ncol, 768)

    def body(c_ref, d_ref, o_ref):
        c = c_ref[...]
        ca = (c * _sigmoid(c)).astype(BF16)
        o_ref[...] = _dot_tn(ca, d_ref[...].astype(BF16))

    return pl.pallas_call(
        body, name="ada_bwd",
        grid=(2, ncol // tn),
        in_specs=[pl.BlockSpec((nb, D), lambda l, n: (0, 0)),
                  pl.BlockSpec((None, nb, tn), lambda l, n: (l, 0, n))],
        out_specs=pl.BlockSpec((None, D, tn), lambda l, n: (l, 0, n)),
        out_shape=jax.ShapeDtypeStruct((2, D, ncol), F32),
        compiler_params=_cp(("arbitrary", "arbitrary")),
    )(c_all, dmod_cols)


def _sum_lead(a, name):
    n, R, C = a.shape
    tr = _tile_div8(R, 256)

    def body(a_ref, o_ref):
        acc = a_ref[0]
        for i in range(1, n):
            acc = acc + a_ref[i]
        o_ref[...] = acc

    return pl.pallas_call(
        body, name=name,
        grid=(R // tr,),
        in_specs=[pl.BlockSpec((n, tr, C), lambda i: (0, i, 0))],
        out_specs=pl.BlockSpec((tr, C), lambda i: (i, 0)),
        out_shape=jax.ShapeDtypeStruct((R, C), F32),
        compiler_params=_cp(("arbitrary",)),
    )(a)


def _adamw(w, g, m, v, name):
    R, C = w.shape
    tr = _tile_div8(R, max(8, (1 << 18) // C))
    c1 = 1.0 / (1.0 - ADAM_B1 ** ADAM_STEP)
    c2 = 1.0 / (1.0 - ADAM_B2 ** ADAM_STEP)

    def body(w_ref, g_ref, m_ref, v_ref, d_ref, mo_ref, vo_ref):
        gg = g_ref[...]
        mn = ADAM_B1 * m_ref[...] + (1.0 - ADAM_B1) * gg
        vn = ADAM_B2 * v_ref[...] + (1.0 - ADAM_B2) * (gg * gg)
        mo_ref[...] = mn
        vo_ref[...] = vn
        d_ref[...] = (-ADAM_LR) * ((mn * c1) / (jnp.sqrt(vn * c2) + ADAM_EPS) + ADAM_WD * w_ref[...])

    spec = pl.BlockSpec((tr, C), lambda i: (i, 0))
    sh = jax.ShapeDtypeStruct((R, C), F32)
    return pl.pallas_call(
        body, name=name, grid=(R // tr,),
        in_specs=[spec] * 4, out_specs=[spec] * 3, out_shape=[sh] * 3,
        compiler_params=_cp(("arbitrary",)),
    )(w, g, m, v)


def _coords():
    return lax.axis_index("x"), lax.axis_index("y"), lax.axis_index("c")


def _all_gather8(blk, name, vmem):
    m_per, n = blk.shape
    space = pltpu.VMEM if vmem else pl.ANY

    def body(x_ref, out_ref, send_sems, recv_sems, local_sem):
        x, y, c = _coords()
        me, sibling = (x, y, c), (x, y, 1 - c)
        chips = [(1 - x, y), (x, 1 - y), (1 - x, 1 - y)]

        def rows(px, py, pc):
            return out_ref.at[4 * px + 2 * py + pc]

        def copy(k, block, to, src=None):
            return pltpu.make_async_remote_copy(
                src_ref=rows(*block) if src is None else src, dst_ref=rows(*block),
                send_sem=send_sems.at[k], recv_sem=recv_sems.at[k], device_id=to, device_id_type=MESH)

        mine = pltpu.make_async_copy(x_ref, rows(*me), local_sem)
        mine.start()
        first = [copy(0, me, sibling, src=x_ref)]
        first += [copy(1 + j, me, (*chip, c), src=x_ref) for j, chip in enumerate(chips)]
        for cp in first:
            cp.start()
        passed = [copy(4 + j, (*chip, c), sibling) for j, chip in enumerate(chips)]
        for j, chip in enumerate(chips):
            copy(1 + j, (*chip, c), me).wait_recv()
            passed[j].start()
        copy(0, sibling, me).wait_recv()
        for j, chip in enumerate(chips):
            copy(4 + j, (*chip, 1 - c), me).wait_recv()
        for cp in first + passed:
            cp.wait_send()
        mine.wait()

    return pl.pallas_call(
        body, name=name,
        out_shape=jax.ShapeDtypeStruct((N_DEV, m_per, n), blk.dtype),
        in_specs=[pl.BlockSpec(memory_space=space)],
        out_specs=pl.BlockSpec(memory_space=space),
        scratch_shapes=[pltpu.SemaphoreType.DMA((7,)), pltpu.SemaphoreType.DMA((7,)), pltpu.SemaphoreType.DMA],
        compiler_params=pltpu.CompilerParams(vmem_limit_bytes=VMEM_BIG if vmem else None),
    )(blk)


def _ag_weights(up_b, dn_b, in_b, out_b):
    _, nj, dn_rows, _ = dn_b.shape
    out_rows = out_b.shape[1]
    n_piece = 3 + nj

    def body(up_ref, dn_ref, in_ref, out_ref, gup_ref, gdn_ref, gin_ref, gout_ref, send_sems, recv_sems, local_sems):
        x, y, c = _coords()
        me, sibling = (x, y, c), (x, y, 1 - c)
        chips = [(1 - x, y), (x, 1 - y), (1 - x, 1 - y)]

        def dsts(px, py, pc):
            s = 2 * px + py
            return ([gup_ref.at[s, pc], gin_ref.at[s, pc], gout_ref.at[pc, pl.ds(s * out_rows, out_rows)]]
                    + [gdn_ref.at[pc, j, pl.ds(s * dn_rows, dn_rows)] for j in range(nj)])

        srcs = [up_ref.at[c], in_ref.at[c], out_ref.at[c]] + [dn_ref.at[c, j] for j in range(nj)]

        def copies(k, block, to, own=False):
            d = dsts(*block)
            return [pltpu.make_async_remote_copy(
                src_ref=srcs[p] if own else d[p], dst_ref=d[p], send_sem=send_sems.at[k, p],
                recv_sem=recv_sems.at[k, p], device_id=to, device_id_type=MESH) for p in range(n_piece)]

        mine = [pltpu.make_async_copy(srcs[p], d, local_sems.at[p]) for p, d in enumerate(dsts(*me))]
        for cp in mine:
            cp.start()
        first = copies(0, me, sibling, own=True)
        for j, chip in enumerate(chips):
            first += copies(1 + j, me, (*chip, c), own=True)
        for cp in first:
            cp.start()
        passed = []
        for j, chip in enumerate(chips):
            for cp in copies(1 + j, (*chip, c), me):
                cp.wait_recv()
            fwd = copies(4 + j, (*chip, c), sibling)
            for cp in fwd:
                cp.start()
            passed += fwd
        for cp in copies(0, sibling, me):
            cp.wait_recv()
        for j, chip in enumerate(chips):
            for cp in copies(4 + j, (*chip, 1 - c), me):
                cp.wait_recv()
        for cp in first + passed:
            cp.wait_send()
        for cp in mine:
            cp.wait()

    nl = up_b.shape[0]
    anyspec = pl.BlockSpec(memory_space=pl.ANY)
    return pl.pallas_call(
        body, name="ag_weights",
        out_shape=[jax.ShapeDtypeStruct((N_SHARD,) + up_b.shape, BF16),
                   jax.ShapeDtypeStruct((nl, nj, N_SHARD * dn_rows, D), BF16),
                   jax.ShapeDtypeStruct((N_SHARD,) + in_b.shape, BF16),
                   jax.ShapeDtypeStruct((nl, N_SHARD * out_rows, D), BF16)],
        in_specs=[anyspec] * 4, out_specs=[anyspec] * 4,
        scratch_shapes=[pltpu.SemaphoreType.DMA((7, n_piece)), pltpu.SemaphoreType.DMA((7, n_piece)),
                        pltpu.SemaphoreType.DMA((n_piece,))],
    )(up_b, dn_b, in_b, out_b)


def _rs_to_sibling(pieces):
    n = len(pieces)

    def body(*refs):
        g, r, (send_sems, recv_sems) = refs[:n], refs[n:2 * n], refs[2 * n:]
        x, y, c = _coords()
        cps = []
        for p in range(n):
            r2 = g[p].shape[1] // 2
            cps.append(pltpu.make_async_remote_copy(
                src_ref=g[p].at[:, pl.ds((1 - c) * r2, r2)], dst_ref=r[p], send_sem=send_sems.at[p],
                recv_sem=recv_sems.at[p], device_id=(x, y, 1 - c), device_id_type=MESH))
        for cp in cps:
            cp.start()
        for cp in cps:
            cp.wait()

    anyspec = pl.BlockSpec(memory_space=pl.ANY)
    return pl.pallas_call(
        body, name="rs_to_sibling",
        out_shape=[jax.ShapeDtypeStruct((N_SHARD, g.shape[1] // 2, g.shape[2]), g.dtype) for g in pieces],
        in_specs=[anyspec] * n, out_specs=[anyspec] * n,
        scratch_shapes=[pltpu.SemaphoreType.DMA((n,)), pltpu.SemaphoreType.DMA((n,))],
    )(*pieces)


def _rs_to_chips(hs):
    n = len(hs)

    def body(*refs):
        h, r, (send_sems, recv_sems) = refs[:n], refs[n:2 * n], refs[2 * n:]
        x, y, c = _coords()
        chips = [(1 - x, y), (x, 1 - y), (1 - x, 1 - y)]
        cps = [pltpu.make_async_remote_copy(
            src_ref=h[p].at[2 * px + py], dst_ref=r[p].at[k], send_sem=send_sems.at[k, p], recv_sem=recv_sems.at[k, p],
            device_id=(px, py, c), device_id_type=MESH) for k, (px, py) in enumerate(chips) for p in range(n)]
        for cp in cps:
            cp.start()
        for cp in cps:
            cp.wait()

    anyspec = pl.BlockSpec(memory_space=pl.ANY)
    return pl.pallas_call(
        body, name="rs_to_chips",
        out_shape=[jax.ShapeDtypeStruct((3,) + h.shape[1:], h.dtype) for h in hs],
        in_specs=[anyspec] * n, out_specs=[anyspec] * n,
        scratch_shapes=[pltpu.SemaphoreType.DMA((3, n)), pltpu.SemaphoreType.DMA((3, n))],
    )(*hs)


def _share_halves(tensors, places, r2s):
    n, no = len(places), len(tensors)

    def body(*refs):
        o, (send_sems, recv_sems) = refs[no:2 * no], refs[2 * no:]
        x, y, c = _coords()

        def half(p, hc):
            oi, lead = places[p]
            return o[oi].at[(*lead, pl.ds(hc * r2s[p], r2s[p]))]

        outs = [pltpu.make_async_remote_copy(
            src_ref=half(p, c), dst_ref=half(p, c), send_sem=send_sems.at[p], recv_sem=recv_sems.at[p],
            device_id=(x, y, 1 - c), device_id_type=MESH) for p in range(n)]
        for cp in outs:
            cp.start()
        for p in range(n):
            pltpu.make_async_remote_copy(
                src_ref=half(p, 1 - c), dst_ref=half(p, 1 - c), send_sem=send_sems.at[p], recv_sem=recv_sems.at[p],
                device_id=(x, y, 1 - c), device_id_type=MESH).wait_recv()
        for cp in outs:
            cp.wait_send()

    anyspec = pl.BlockSpec(memory_space=pl.ANY)
    return pl.pallas_call(
        body, name="share_halves",
        out_shape=[jax.ShapeDtypeStruct(t.shape, t.dtype) for t in tensors],
        in_specs=[anyspec] * no, out_specs=[anyspec] * no,
        input_output_aliases={i: i for i in range(no)},
        scratch_shapes=[pltpu.SemaphoreType.DMA((n,)), pltpu.SemaphoreType.DMA((n,))],
    )(*tensors)


def _add_rows(r2, cols, n_arrays):
    lanes = -(-cols // 128) * 128
    return _tile_div8(r2, max(16, (24 << 20) // (2 * n_arrays * lanes * 4)), mult=16)


def _add_sibling(pieces, recvs, cidx, name):
    n = len(pieces)
    _, R, C = pieces[0].shape
    r2 = R // 2
    tr = _add_rows(r2, C, 2 * n)
    nt = r2 // tr

    def body(c_ref, *refs):
        for p in range(n):
            refs[2 * n + p][...] = (refs[p][...] + refs[n + p][...].astype(F32)).astype(BF16)

    return pl.pallas_call(
        body, name=name,
        grid_spec=pltpu.PrefetchScalarGridSpec(
            num_scalar_prefetch=1, grid=(N_SHARD, nt),
            in_specs=[pl.BlockSpec((None, tr, C), lambda s, i, c_ref: (s, c_ref[0] * nt + i, 0))] * n
            + [pl.BlockSpec((None, tr, C), lambda s, i, c_ref: (s, i, 0))] * n,
            out_specs=[pl.BlockSpec((None, tr, C), lambda s, i, c_ref: (s, i, 0))] * n),
        out_shape=[jax.ShapeDtypeStruct((N_SHARD, r2, C), BF16)] * n,
        compiler_params=_cp(("arbitrary", "arbitrary"), VMEM_BIG),
    )(cidx, *pieces, *recvs)


def _add_chips_into(piece, recv_a, recv_b, sc, prev, shape, lead, name):
    _, R, C = piece.shape
    r2 = R // 2
    tr = _add_rows(r2, C, 4)
    nt = r2 // tr
    nl = len(lead)

    def body(sc_ref, p_ref, a_ref, b_ref, *rest):
        o_ref = rest[-1]
        acc = p_ref[...] + a_ref[...].astype(F32)
        for k in range(3):
            acc = acc + b_ref[k].astype(F32)
        o_ref[...] = acc

    in_specs = [pl.BlockSpec((None, tr, C), lambda i, sc_ref: (sc_ref[0], sc_ref[1] * nt + i, 0)),
                pl.BlockSpec((None, tr, C), lambda i, sc_ref: (sc_ref[0], i, 0)),
                pl.BlockSpec((3, tr, C), lambda i, sc_ref: (0, i, 0))]
    args = [sc, piece, recv_a, recv_b]
    aliases = {}
    if prev is not None:
        in_specs.append(pl.BlockSpec(memory_space=pl.ANY))
        args.append(prev)
        aliases = {4: 0}
    return pl.pallas_call(
        body, name=name,
        grid_spec=pltpu.PrefetchScalarGridSpec(
            num_scalar_prefetch=1, grid=(nt,), in_specs=in_specs,
            out_specs=pl.BlockSpec((None,) * nl + (tr, C), lambda i, sc_ref: (*lead, sc_ref[1] * nt + i, 0))),
        out_shape=jax.ShapeDtypeStruct(shape, F32),
        input_output_aliases=aliases,
        compiler_params=_cp(("arbitrary",), VMEM_BIG),
    )(*args)


def _pack_rows(parts, rows, dtype):
    flat = jnp.concatenate([p.reshape(-1).astype(dtype) for p in parts])
    return jnp.pad(flat, (0, rows * ROW - flat.shape[0])).reshape(rows, ROW)


def _unpack(flat, shapes):
    out, off = [], 0
    for sh in shapes:
        n = math.prod(sh)
        out.append(flat[off:off + n].reshape(sh))
        off += n
    return out


def _heads(t, B, S, blk):
    return t.reshape(B, S, NH, HD).transpose(0, 2, 1, 3).reshape(B, NH, S // blk, blk, HD)


def _unheads(t, B, S):
    return t.reshape(B, NH, S, HD).transpose(0, 2, 1, 3).reshape(B * S, AW)


def _block_diag(w):
    eye = jnp.eye(LW // HD, dtype=w.dtype)
    return jnp.einsum("lhij,hg->lhigj", w, eye).reshape(w.shape[0], LW, LW)


def _diag_blocks(w):
    nbk = LW // HD
    w4 = w.reshape(nbk, HD, nbk, HD)
    return jnp.stack([w4[h, :, h, :] for h in range(nbk)])


def _rows8(rows, width):
    z = jnp.zeros((width,), F32)
    return jnp.stack(list(rows) + [z] * (8 - len(rows)))


def kernel(x, c, w_ada, b_ada, g_norm, w_ffn_up, w_ffn_down, w_in, b_fgate, conv_w, conv_b, w_rgate, b_rgate, w_igate, b_igate, lru_lambda, g_qk, g_mix_out, w_out, loss_target, m_w_ada, m_b_ada, m_g_norm, m_w_ffn_up, m_w_ffn_down, m_w_in, m_b_fgate, m_conv_w, m_conv_b, m_w_rgate, m_b_rgate, m_w_igate, m_b_igate, m_lru_lambda, m_g_qk, m_g_mix_out, m_w_out, v_w_ada, v_b_ada, v_g_norm, v_w_ffn_up, v_w_ffn_down, v_w_in, v_b_fgate, v_conv_w, v_conv_b, v_w_rgate, v_b_rgate, v_w_igate, v_b_igate, v_lru_lambda, v_g_qk, v_g_mix_out, v_w_out):
    B, S, _ = x.shape
    T = B * S
    xi, yi, ci = _coords()
    sidx = 2 * xi + yi
    didx = 4 * xi + 2 * yi + ci
    ada_cols = w_ada.shape[2]
    gn_cols = g_norm.shape[2]
    cw_cols = conv_w.shape[2]
    n_all = B * N_DEV

    blk1 = _pack_rows([c, jnp.pad(g_norm.reshape(-1), (0, 2 * ROW - g_norm.size)), conv_w], 8, F32)
    ag1 = _all_gather8(blk1, "ag_small_in", True)
    c_all = ag1[:, 0:B].reshape(n_all, D)
    chip_rows = ag1[0::2]
    g_norm_full = chip_rows[:, 2:4].reshape(N_SHARD, 2 * ROW)[:, :g_norm.size] \
        .reshape(N_SHARD, 2, 3, gn_cols).transpose(1, 2, 0, 3).reshape(2, 3, D)
    conv_w_full = chip_rows[:, 4].reshape(N_SHARD, 2, 4, cw_cols).transpose(1, 2, 0, 3).reshape(2, 4, LW)

    b_cols = lax.dynamic_slice(b_ada, (0, sidx * ada_cols), (2, ada_cols)).reshape(2, 1, ada_cols)
    mod_cols = _ada_fwd(c_all, w_ada, b_cols)
    mrows = (2 * n_all * ada_cols) // ROW
    ag2 = _all_gather8(mod_cols.reshape(mrows, ROW), "ag_mod", True)
    mod_sh = ag2[0::2].reshape(N_SHARD, 2, n_all, ada_cols)
    mod_me = lax.dynamic_slice(mod_sh, (0, 0, didx * B, 0), (N_SHARD, 2, B, ada_cols))
    mod_me = mod_me.transpose(1, 2, 0, 3).reshape(2, B, 3, 3, D)
    zrow = jnp.zeros((B, D), F32)
    mods = [[jnp.stack([mod_me[l, :, j, 0], 1.0 + mod_me[l, :, j, 1], 1.0 + mod_me[l, :, j, 2],
                        jnp.broadcast_to(g_norm_full[l, j], (B, D)), zrow, zrow, zrow, zrow], axis=1)
             for j in range(3)] for l in range(2)]

    wup5, wdn4, g_in, wout = _ag_weights(w_ffn_up.astype(BF16), w_ffn_down.astype(BF16),
                                         w_in.astype(BF16), w_out.astype(BF16))
    win_full = g_in.transpose(1, 2, 0, 3).reshape(2, D, N_IN)
    winp = jnp.pad(win_full, ((0, 0), (0, 0), (0, N_INP - N_IN)))

    wr_d = _block_diag(w_rgate).astype(BF16)
    wi_d = _block_diag(w_igate).astype(BF16)
    cw8 = jnp.pad(conv_w_full, ((0, 0), (0, 4), (0, 0)))
    vp8 = jnp.stack([_rows8([conv_b[l], b_rgate[l], b_igate[l], lru_lambda[l]], LW) for l in range(2)])
    bfp = jnp.pad(b_fgate, ((0, 0), (0, 128 - NH)))[:, None, :] * jnp.ones((1, 8, 1), F32)
    gqk2 = jnp.tile(jnp.pad(g_qk, ((0, 0), (0, 6), (0, 0))), (1, 1, 2))
    gmix8 = jnp.pad(g_mix_out[:, None, :], ((0, 0), (0, 7), (0, 0)))

    x2 = x.reshape(T, D)
    tgt = loss_target.reshape(T, D)

    saved = []
    xc = x2
    for l in range(2):
        sv = {}
        sv["x0"] = xc
        xc, sv["g0"], sv["u0"], sv["f0"] = _ffn_fwd(xc, mods[l][0], wup5, wdn4, l, 0, S)
        sv["x1"] = xc
        sv["h1"], proj = _mix_in_fwd(xc, mods[l][1], winp, l, S)
        sv["proj"] = proj
        sv["ylru"], sv["hl"] = _lru_fwd(proj, cw8, vp8, wr_d, wi_d, l, S)
        sv["osb"], sv["t1"] = _sbq_fwd(proj, l, S)
        cum = _fgate_fwd(proj, bfp, l, S)
        sv["ck"] = cum[:, :NH].reshape(B, S, NH).transpose(0, 2, 1).reshape(B, NH, S // TK_(S), 1, TK_(S))
        sv["ofx"], sv["nl"] = _foxq_fwd(proj, cum, sv["ck"], gqk2, l, S)
        xc, sv["y"], sv["mo"] = _mix_out_fwd(xc, sv["ylru"], sv["osb"], sv["ofx"], mods[l][1], gmix8, wout, l, S)
        sv["x2"] = xc
        xc, sv["g2"], sv["u2"], sv["f2"] = _ffn_fwd(xc, mods[l][2], wup5, wdn4, l, 1, S)
        saved.append(sv)

    dxc, lpart = _loss_head(xc, tgt, S)
    loss = lax.psum(lpart[0, 0], ("x", "y", "c"))

    tf = wup5.shape[-1]
    g_up_l = [[None, None], [None, None]]
    g_dn_l = [[None, None], [None, None]]
    g_in_l, g_out_l = [None, None], [None, None]
    dmods = [[None] * 3 for _ in range(2)]
    small = [dict() for _ in range(2)]

    def ffn_back(l, j, xin, dy, sv, sub):
        dx, dmod, wacc, hb, dfb, ab, dgub = _ffn_bwd(
            xin, dy, mods[l][sub], sv[f"f{sub}"], sv[f"g{sub}"], sv[f"u{sub}"], wup5, wdn4, l, j, S)
        g_up_l[l][j] = _mm_tn(hb, dgub, f"dw_up_{l}_{j}", tnb=tf, split_n=True, with_bf16=True)
        g_dn_l[l][j] = tuple(g.reshape(N_SHARD, -1, D)
                             for g in _mm_tn(ab, dfb, f"dw_dn_{l}_{j}", tma=tf, with_bf16=True))
        dmods[l][sub] = dmod
        small[l][f"gn{sub}"] = wacc[0]
        return dx

    for l in (1, 0):
        sv = saved[l]
        dxc = ffn_back(l, 1, sv["x2"], dxc, sv, 2)
        dyl, dsb, dfx, dmo, dmod1, wacc_mo = _mix_out_bwd(
            dxc, sv["ylru"], sv["osb"], sv["ofx"], sv["mo"], mods[l][1], gmix8, wout, l, S)
        small[l]["gmix"] = wacc_mo[0]
        g_out_l[l] = tuple(g.reshape(N_SHARD, -1, D) for g in _mm_tn(sv["y"], dmo, f"dw_out_{l}", with_bf16=True))
        dsq, dsk, dsv = _sbq_bwd(sv["proj"], dsb, sv["t1"], l, S)
        dfq, dfk, dfv, dck, wacc_fx = _foxq_bwd(sv["proj"], dfx, sv["nl"], sv["ck"], gqk2, l, S)
        small[l]["gqk"] = wacc_fx[0:2, :HD] + wacc_fx[0:2, HD:]
        dcum = dck[:, :, :, 0, :].reshape(B, NH, S).transpose(0, 2, 1).reshape(T, NH)
        dff_, wacc_fg = _fgate_bwd(jnp.pad(dcum, ((0, 0), (0, 128 - NH))), sv["proj"], bfp, l, S)
        small[l]["bf"] = wacc_fg[0, :NH]
        dlx, dlg, dpr, dpi, ub, wacc_lru = _lru_bwd(dyl, sv["proj"], sv["hl"], cw8, vp8, wr_d, wi_d, l, S)
        small[l]["lru"] = wacc_lru
        small[l]["wr"] = _diag_blocks(_mm_tn(ub, dpr, f"dw_rgate_{l}"))
        small[l]["wi"] = _diag_blocks(_mm_tn(ub, dpi, f"dw_igate_{l}"))
        dproj = jnp.concatenate(
            [dlx, dlg, dsq, dsk, dsv, dfq, dfk, dfv, dff_], axis=1)
        g_in = _mm_tn(sv["h1"], dproj, f"dw_in_{l}", tnb=N_INP // 3)[:, :N_IN]
        g_in = g_in.reshape(D, N_SHARD, -1).transpose(1, 0, 2)
        g_in_l[l] = (g_in, g_in.astype(BF16))
        dxc, dmod_in, wacc_in = _mix_in_bwd(sv["x1"], dxc, mods[l][1], dproj, winp, l, S)
        dmods[l][1] = dmod_in + dmod1
        small[l]["gn1"] = wacc_in[0]
        dxc = ffn_back(l, 0, sv["x0"], dxc, sv, 0)
    grad_x = dxc.reshape(B, S, D)

    dmod_loc = jnp.stack([jnp.stack([dmods[l][j][:, 0:3, :] for j in range(3)], axis=1) for l in range(2)])
    drows = 2 * B * 9
    blk3 = _pack_rows([dmod_loc], -(-drows // 8) * 8, F32)
    ag3 = _all_gather8(blk3, "ag_dmod", True)
    dmod_all = ag3[:, :drows].reshape(N_DEV, 2, B, 9 * D).transpose(1, 0, 2, 3).reshape(2, n_all, 9 * D)
    dmod_mine = lax.dynamic_slice(dmod_all, (0, 0, sidx * ada_cols), (2, n_all, ada_cols))
    grad_w_ada = _ada_bwd(c_all, dmod_mine)
    dmod_rows = jnp.pad(dmod_all.transpose(1, 0, 2).reshape(n_all, 2 * 9, D), ((0, 0), (0, 6), (0, 0)))
    grad_b_ada = _sum_lead(dmod_rows, "grad_b_ada")[:2 * 9].reshape(2, 9 * D)

    sm_parts = [
        jnp.stack([small[l]["bf"] for l in range(2)]),
        jnp.stack([small[l]["lru"][4] for l in range(2)]),
        jnp.stack([small[l]["wr"] for l in range(2)]),
        jnp.stack([small[l]["lru"][5] for l in range(2)]),
        jnp.stack([small[l]["wi"] for l in range(2)]),
        jnp.stack([small[l]["lru"][6] for l in range(2)]),
        jnp.stack([small[l]["lru"][7] for l in range(2)]),
        jnp.stack([small[l]["gqk"] for l in range(2)]),
        jnp.stack([small[l]["gmix"] for l in range(2)]),
        jnp.stack([jnp.stack([small[l][f"gn{j}"] for j in range(3)]) for l in range(2)]),
        jnp.stack([small[l]["lru"][0:4] for l in range(2)]),
    ]
    sm_shapes = [p.shape for p in sm_parts]
    sm_rows = -(-sum(p.size for p in sm_parts) // (8 * ROW)) * 8
    ag4 = _all_gather8(_pack_rows(sm_parts, sm_rows, F32), "ag_small_grads", True)
    sm_sum = _sum_lead(ag4, "sum_small_grads").reshape(-1)
    (g_bf, g_cb, g_wr, g_br, g_wi, g_bi, g_lam, g_gqk, g_gmix, g_gn_full, g_cw_full) = _unpack(sm_sum, sm_shapes)
    g_gn = lax.dynamic_slice(g_gn_full, (0, 0, sidx * gn_cols), (2, 3, gn_cols))
    g_cw = lax.dynamic_slice(g_cw_full, (0, 0, sidx * cw_cols), (2, 4, cw_cols))

    lj = [(l, j) for l in range(2) for j in range(2)]
    groups = [
        ("up", [g_up_l[l][j] for l, j in lj], w_ffn_up.shape, lj),
        ("dn", [g_dn_l[l][j] for l, j in lj], w_ffn_down.shape, lj),
        ("in", g_in_l, w_in.shape, [(0,), (1,)]),
        ("out", g_out_l, w_out.shape, [(0,), (1,)]),
    ]
    pieces = [p for _, ps, _, _ in groups for p in ps]
    cvec = jnp.reshape(ci, (1,)).astype(jnp.int32)
    scvec = jnp.stack([sidx, ci]).astype(jnp.int32)
    recv_a = _rs_to_sibling([pb for _, pb in pieces])
    hs, off = [], 0
    for gname, ps, _, _ in groups:
        hs += _add_sibling([pf for pf, _ in ps], recv_a[off:off + len(ps)], cvec, f"rs_add_sibling_{gname}")
        off += len(ps)
    recv_b = _rs_to_chips(hs)
    tensors, places, r2s, k = [], [], [], 0
    for gi, (gname, ps, shape, leads) in enumerate(groups):
        t = None
        for (pf, _), lead in zip(ps, leads):
            t = _add_chips_into(pf, recv_a[k], recv_b[k], scvec, t, shape, lead,
                                f"rs_add_chips_{gname}_{'_'.join(map(str, lead))}")
            places.append((gi, lead))
            r2s.append(pf.shape[1] // 2)
            k += 1
        tensors.append(t)
    gw_up, gw_dn, gw_in, gw_out = _share_halves(tensors, places, r2s)

    def upd(w, g, m, v, name):
        sh = w.shape
        two = (w.size // sh[-1], sh[-1])
        dlt, mn, vn = _adamw(w.reshape(two), g.reshape(two), m.reshape(two), v.reshape(two), name)
        return dlt.reshape(sh), mn.reshape(sh), vn.reshape(sh)

    big = {
        "w_ada": (w_ada, grad_w_ada, m_w_ada, v_w_ada),
        "w_ffn_up": (w_ffn_up, gw_up, m_w_ffn_up, v_w_ffn_up),
        "w_ffn_down": (w_ffn_down, gw_dn, m_w_ffn_down, v_w_ffn_down),
        "w_in": (w_in, gw_in, m_w_in, v_w_in),
        "w_out": (w_out, gw_out, m_w_out, v_w_out),
    }
    res = {n: (t[1],) + upd(*t, f"adamw_{n}") for n, t in big.items()}

    smalls = {
        "b_ada": (b_ada, grad_b_ada, m_b_ada, v_b_ada),
        "g_norm": (g_norm, g_gn, m_g_norm, v_g_norm),
        "b_fgate": (b_fgate, g_bf, m_b_fgate, v_b_fgate),
        "conv_w": (conv_w, g_cw, m_conv_w, v_conv_w),
        "conv_b": (conv_b, g_cb, m_conv_b, v_conv_b),
        "w_rgate": (w_rgate, g_wr, m_w_rgate, v_w_rgate),
        "b_rgate": (b_rgate, g_br, m_b_rgate, v_b_rgate),
        "w_igate": (w_igate, g_wi, m_w_igate, v_w_igate),
        "b_igate": (b_igate, g_bi, m_b_igate, v_b_igate),
        "lru_lambda": (lru_lambda, g_lam, m_lru_lambda, v_lru_lambda),
        "g_qk": (g_qk, g_gqk, m_g_qk, v_g_qk),
        "g_mix_out": (g_mix_out, g_gmix, m_g_mix_out, v_g_mix_out),
    }
    names = list(smalls)
    shapes = [smalls[n][0].shape for n in names]
    prow = -(-sum(math.prod(s) for s in shapes) // (8 * ROW)) * 8
    packed = [_pack_rows([smalls[n][i].reshape(shapes[k]) for k, n in enumerate(names)], prow, F32) for i in range(4)]
    outs = _adamw(packed[0], packed[1], packed[2], packed[3], "adamw_small")
    un = [_unpack(o.reshape(-1), shapes) for o in outs]
    for k, n in enumerate(names):
        res[n] = (smalls[n][1].reshape(shapes[k]), un[0][k], un[1][k], un[2][k])

    order = ["w_ada", "b_ada", "g_norm", "w_ffn_up", "w_ffn_down", "w_in", "b_fgate", "conv_w", "conv_b",
             "w_rgate", "b_rgate", "w_igate", "b_igate", "lru_lambda", "g_qk", "g_mix_out", "w_out"]
    return (loss, grad_x, *[res[n][0] for n in order], *[res[n][1] for n in order],
            *[res[n][2] for n in order], *[res[n][3] for n in order])


def TQ_(S):
    return min(TQ, S)


def TK_(S):
    return min(TK, S)


def _unpack_shards(wg, shapes):
    out, off = [], 0
    for sh in shapes:
        n = math.prod(sh)
        out.append(wg[:, off:off + n].reshape((N_SHARD,) + tuple(sh)))
        off += n
    return out
```

```python
import math

import jax
import jax.numpy as jnp
from jax import lax
from jax.experimental import pallas as pl
from jax.experimental.pallas import tpu as pltpu

F32 = jnp.float32
BF16 = jnp.bfloat16
MESH = pl.DeviceIdType.MESH

D = 1024
HD = 64
LW = 512
NH = 4
AW = NH * HD
N_IN = 2564
N_INP = 2688
F_BLK = 2560 // 128
EPS = 1e-6
LRU_C = 8.0
SCALE = HD ** -0.5
NEG = -1e30
TQ = 256
TK = 256

ADAM_LR, ADAM_B1, ADAM_B2, ADAM_EPS, ADAM_WD, ADAM_STEP = 0.001, 0.9, 0.999, 1e-08, 0.01, 10

VMEM_BIG = 56 * 1024 * 1024
N_DEV = 8
N_SHARD = 4
ROW = 1024


def _cp(sem, vmem=None):
    return pltpu.CompilerParams(dimension_semantics=sem, vmem_limit_bytes=vmem)


def _dot(a, b):
    return jnp.dot(a, b, preferred_element_type=F32)


def _dot_nt(a, b):
    return lax.dot_general(a, b, (((1,), (1,)), ((), ())), preferred_element_type=F32)


def _dot_tn(a, b):
    return lax.dot_general(a, b, (((0,), (0,)), ((), ())), preferred_element_type=F32)


def _log1p(e):
    small = e * (1.0 - e * (0.5 - e * (1.0 / 3.0 - e * 0.25)))
    return jnp.where(e < 0.01, small, jnp.log(1.0 + e))


def _expm1_neg(x):
    small = x * (1.0 + x * 0.5 * (1.0 + x * (1.0 / 3.0) * (1.0 + x * 0.25 * (1.0 + x * 0.2))))
    return jnp.where(x > -0.05, small, jnp.exp(x) - 1.0)


def _sigmoid(x):
    return 1.0 / (1.0 + jnp.exp(-x))


_GELU_C = math.sqrt(2.0 / math.pi)


def _gelu_and_grad(x):
    x2 = x * x
    th = jnp.tanh(_GELU_C * (x + 0.044715 * x * x2))
    g = 0.5 * x * (1.0 + th)
    dg = 0.5 * (1.0 + th) + 0.5 * x * (1.0 - th * th) * _GELU_C * (1.0 + 3.0 * 0.044715 * x2)
    return g, dg


def _rms_rows(x):
    rstd = lax.rsqrt(jnp.mean(x * x, axis=-1, keepdims=True) + EPS)
    return x * rstd, rstd


def _rms_bwd(xn, rstd, dyn):
    return rstd * (dyn - xn * jnp.mean(dyn * xn, axis=-1, keepdims=True))


def _colsum(x):
    return jnp.sum(x, axis=0, keepdims=True)


def _rowsum(x):
    return jnp.sum(x, axis=1, keepdims=True)


def _split3(x):
    hi = x.astype(BF16)
    r = x - hi.astype(F32)
    mid = r.astype(BF16)
    lo = (r - mid.astype(F32)).astype(BF16)
    return hi, mid, lo


def _cumsum_mm(x, ones_tri, parts=3):
    ps = _split3(x)[:parts]
    acc = _dot(ps[0], ones_tri)
    for p in ps[1:]:
        acc = acc + _dot(p, ones_tri)
    return acc


def _tri(n, kind):
    r = lax.broadcasted_iota(jnp.int32, (n, n), 0)
    c = lax.broadcasted_iota(jnp.int32, (n, n), 1)
    m = {"row_gt_col": r > c, "row_le_col": r <= c, "row_lt_col": r < c}[kind]
    return jnp.where(m, 1.0, 0.0).astype(BF16)


def _normmod(x, mod_ref):
    xn, rstd = _rms_rows(x)
    h = xn * mod_ref[3:4, :] * mod_ref[1:2, :] + mod_ref[0:1, :]
    return h, xn, rstd


def _normmod_bwd(dh, xn, rstd, mod_ref, dmod_ref, wacc_ref):
    gn = mod_ref[3:4, :]
    sc = mod_ref[1:2, :]
    dmod_ref[0:1, :] += _colsum(dh)
    t = _colsum(dh * xn)
    dmod_ref[1:2, :] += t * gn
    wacc_ref[0:1, :] += t * sc
    return _rms_bwd(xn, rstd, dh * (gn * sc))


def _tile(n, want):
    t = min(n, want)
    while n % t:
        t //= 2
    return t


def _tile_div8(n, cap, mult=8):
    best = mult
    for t in range(mult, min(n, cap) + 1, mult):
        if n % t == 0:
            best = t
    assert n % best == 0
    return best


def _ffn_fwd(x, mod, wup5, wdn4, l, j, S):
    T = x.shape[0]
    tf = wup5.shape[-1]
    nk = 2
    tm = _tile(S, 512)
    tpb = S // tm

    def body(x_ref, mod_ref, wg_ref, wu_ref, wd_ref, xo_ref, g_ref, u_ref, f_ref, h_sc, acc_sc):
        k = pl.program_id(1)

        @pl.when(k == 0)
        def _():
            h, _, _ = _normmod(x_ref[...], mod_ref)
            h_sc[...] = h.astype(BF16)
            acc_sc[...] = jnp.zeros_like(acc_sc)

        h = h_sc[...]
        g = _dot(h, wg_ref[...])
        u = _dot(h, wu_ref[...])
        g_ref[...] = g.astype(BF16)
        u_ref[...] = u.astype(BF16)
        a = (g * _sigmoid(g)) * u
        acc_sc[...] += _dot(a.astype(BF16), wd_ref[...])

        @pl.when(k == nk - 1)
        def _():
            f = acc_sc[...]
            f_ref[...] = f.astype(BF16)
            xo_ref[...] = x_ref[...] + (0.5 * mod_ref[2:3, :]) * f

    return pl.pallas_call(
        body, name=f"ffn_fwd_{l}_{j}",
        grid=(T // tm, nk),
        in_specs=[
            pl.BlockSpec((tm, D), lambda i, k: (i, 0)),
            pl.BlockSpec((None, 8, D), lambda i, k: (i // tpb, 0, 0)),
            pl.BlockSpec((None, None, D, tf), lambda i, k: (k, j, 0, 0)),
            pl.BlockSpec((None, None, D, tf), lambda i, k: (nk + k, j, 0, 0)),
            pl.BlockSpec((None, tf, D), lambda i, k: (j, k, 0)),
        ],
        out_specs=[
            pl.BlockSpec((tm, D), lambda i, k: (i, 0)),
            pl.BlockSpec((tm, tf), lambda i, k: (i, k)),
            pl.BlockSpec((tm, tf), lambda i, k: (i, k)),
            pl.BlockSpec((tm, D), lambda i, k: (i, 0)),
        ],
        out_shape=[
            jax.ShapeDtypeStruct((T, D), F32),
            jax.ShapeDtypeStruct((T, nk * tf), BF16),
            jax.ShapeDtypeStruct((T, nk * tf), BF16),
            jax.ShapeDtypeStruct((T, D), BF16),
        ],
        scratch_shapes=[pltpu.VMEM((tm, D), BF16), pltpu.VMEM((tm, D), F32)],
        compiler_params=_cp(("arbitrary", "arbitrary"), VMEM_BIG),
    )(x, mod, wup5, wup5, wdn4)


def _ffn_bwd(x, dy, mod, f, g, u, wup5, wdn4, l, j, S):
    T = x.shape[0]
    tf = wup5.shape[-1]
    nk = 2
    tm = _tile(S, 256)
    tpb = S // tm

    def body(x_ref, dy_ref, mod_ref, f_ref, g_ref, u_ref, wup_ref, wd_ref,
             dx_ref, dmod_ref, wacc_ref, h_ref, df_ref, a_ref, dgu_ref):
        i = pl.program_id(0)

        @pl.when(i == 0)
        def _():
            wacc_ref[...] = jnp.zeros_like(wacc_ref)

        @pl.when(i % tpb == 0)
        def _():
            dmod_ref[...] = jnp.zeros_like(dmod_ref)

        dy_ = dy_ref[...]
        h, xn, rstd = _normmod(x_ref[...], mod_ref)
        h_ref[...] = h.astype(BF16)
        dfb = ((0.5 * mod_ref[2:3, :]) * dy_).astype(BF16)
        df_ref[...] = dfb
        dmod_ref[2:3, :] += _colsum(0.5 * f_ref[...].astype(F32) * dy_)
        dh = None
        for k in range(nk):
            cols = slice(k * tf, (k + 1) * tf)
            da = _dot_nt(dfb, wd_ref[cols, :])
            gg = g_ref[:, cols].astype(F32)
            uu = u_ref[:, cols].astype(F32)
            sig = _sigmoid(gg)
            s = gg * sig
            a_ref[:, cols] = (s * uu).astype(BF16)
            du = (da * s).astype(BF16)
            dg = (da * uu * (sig * (1.0 + gg * (1.0 - sig)))).astype(BF16)
            dgu_ref[0, :, cols] = dg
            dgu_ref[1, :, cols] = du
            part = _dot_nt(dg, wup_ref[k]) + _dot_nt(du, wup_ref[nk + k])
            dh = part if dh is None else dh + part
        dx_ref[...] = dy_ + _normmod_bwd(dh, xn, rstd, mod_ref, dmod_ref, wacc_ref)

    once = pl.Buffered(1)
    return pl.pallas_call(
        body, name=f"ffn_bwd_{l}_{j}",
        grid=(T // tm,),
        in_specs=[
            pl.BlockSpec((tm, D), lambda i: (i, 0)),
            pl.BlockSpec((tm, D), lambda i: (i, 0)),
            pl.BlockSpec((None, 8, D), lambda i: (i // tpb, 0, 0)),
            pl.BlockSpec((tm, D), lambda i: (i, 0)),
            pl.BlockSpec((tm, nk * tf), lambda i: (i, 0)),
            pl.BlockSpec((tm, nk * tf), lambda i: (i, 0)),
            pl.BlockSpec((2 * nk, None, D, tf), lambda i: (0, j, 0, 0), pipeline_mode=once),
            pl.BlockSpec((None, nk * tf, D), lambda i: (j, 0, 0), pipeline_mode=once),
        ],
        out_specs=[
            pl.BlockSpec((tm, D), lambda i: (i, 0)),
            pl.BlockSpec((None, 8, D), lambda i: (i // tpb, 0, 0)),
            pl.BlockSpec((8, D), lambda i: (0, 0)),
            pl.BlockSpec((tm, D), lambda i: (i, 0)),
            pl.BlockSpec((tm, D), lambda i: (i, 0)),
            pl.BlockSpec((tm, nk * tf), lambda i: (i, 0)),
            pl.BlockSpec((2, tm, nk * tf), lambda i: (0, i, 0)),
        ],
        out_shape=[
            jax.ShapeDtypeStruct((T, D), F32),
            jax.ShapeDtypeStruct((T // S, 8, D), F32),
            jax.ShapeDtypeStruct((8, D), F32),
            jax.ShapeDtypeStruct((T, D), BF16),
            jax.ShapeDtypeStruct((T, D), BF16),
            jax.ShapeDtypeStruct((T, nk * tf), BF16),
            jax.ShapeDtypeStruct((2, T, nk * tf), BF16),
        ],
        compiler_params=_cp(("arbitrary",), VMEM_BIG),
    )(x, dy, mod, f, g, u, wup5, wdn4)


def _mm_tn(a, b, name, tma=None, tnb=None, split_n=False, with_bf16=False):
    T, M = a.shape
    b3 = b if b.ndim == 3 else b[None]
    nb, _, N = b3.shape
    tma = tma or M
    tnb = tnb or N
    npb = N // tnb
    tt = _tile(T, 1024)
    nt = T // tt

    def body(a_ref, b_ref, o_ref, *ob_ref):
        @pl.when(pl.program_id(2) == 0)
        def _():
            o_ref[...] = jnp.zeros_like(o_ref)

        o_ref[...] += _dot_tn(a_ref[...], b_ref[...])

        if with_bf16:
            @pl.when(pl.program_id(2) == nt - 1)
            def _():
                ob_ref[0][...] = o_ref[...].astype(BF16)

    if split_n:
        shape = (nb * npb, M, tnb)
        out_spec = pl.BlockSpec((None, tma, tnb), lambda m, n, t: (n, m, 0))
    else:
        assert nb == 1
        shape = (M, N)
        out_spec = pl.BlockSpec((tma, tnb), lambda m, n, t: (m, n))
    dts = (F32, BF16) if with_bf16 else (F32,)
    out = pl.pallas_call(
        body, name=name,
        grid=(M // tma, nb * npb, nt),
        in_specs=[pl.BlockSpec((tt, tma), lambda m, n, t: (t, m)),
                  pl.BlockSpec((None, tt, tnb), lambda m, n, t: (n // npb, t, n % npb))],
        out_specs=[out_spec] * len(dts),
        out_shape=[jax.ShapeDtypeStruct(shape, dt) for dt in dts],
        compiler_params=_cp(("arbitrary", "arbitrary", "arbitrary"), VMEM_BIG),
    )(a, b3)
    return tuple(out) if with_bf16 else out[0]


def _mix_in_fwd(x, mod, winp, l, S):
    T = x.shape[0]
    tm = _tile(S, 512)
    tpb = S // tm

    def body(x_ref, mod_ref, w_ref, h_ref, p_ref):
        h, _, _ = _normmod(x_ref[...], mod_ref)
        hb = h.astype(BF16)
        h_ref[...] = hb
        p_ref[...] = _dot(hb, w_ref[...])

    return pl.pallas_call(
        body, name=f"mix_in_fwd_{l}",
        grid=(T // tm,),
        in_specs=[pl.BlockSpec((tm, D), lambda i: (i, 0)),
                  pl.BlockSpec((None, 8, D), lambda i: (i // tpb, 0, 0)),
                  pl.BlockSpec((D, N_INP), lambda i: (0, 0))],
        out_specs=[pl.BlockSpec((tm, D), lambda i: (i, 0)),
                   pl.BlockSpec((tm, N_INP), lambda i: (i, 0))],
        out_shape=[jax.ShapeDtypeStruct((T, D), BF16), jax.ShapeDtypeStruct((T, N_INP), F32)],
        compiler_params=_cp(("arbitrary",), VMEM_BIG),
    )(x, mod, winp)


def _mix_in_bwd(x, dres, mod, dproj, winp, l, S):
    T = x.shape[0]
    tm = _tile(S, 512)
    tpb = S // tm

    def body(x_ref, dr_ref, mod_ref, dp_ref, w_ref, dx_ref, dmod_ref, wacc_ref):
        i = pl.program_id(0)

        @pl.when(i == 0)
        def _():
            wacc_ref[...] = jnp.zeros_like(wacc_ref)

        @pl.when(i % tpb == 0)
        def _():
            dmod_ref[...] = jnp.zeros_like(dmod_ref)

        dh = _dot_nt(dp_ref[...], w_ref[...])
        _, xn, rstd = _normmod(x_ref[...], mod_ref)
        dx_ref[...] = dr_ref[...] + _normmod_bwd(dh, xn, rstd, mod_ref, dmod_ref, wacc_ref)

    return pl.pallas_call(
        body, name=f"mix_in_bwd_{l}",
        grid=(T // tm,),
        in_specs=[pl.BlockSpec((tm, D), lambda i: (i, 0)),
                  pl.BlockSpec((tm, D), lambda i: (i, 0)),
                  pl.BlockSpec((None, 8, D), lambda i: (i // tpb, 0, 0)),
                  pl.BlockSpec((tm, N_INP), lambda i: (i, 0)),
                  pl.BlockSpec((D, N_INP), lambda i: (0, 0))],
        out_specs=[pl.BlockSpec((tm, D), lambda i: (i, 0)),
                   pl.BlockSpec((None, 8, D), lambda i: (i // tpb, 0, 0)),
                   pl.BlockSpec((8, D), lambda i: (0, 0))],
        out_shape=[jax.ShapeDtypeStruct((T, D), F32),
                   jax.ShapeDtypeStruct((T // S, 8, D), F32),
                   jax.ShapeDtypeStruct((8, D), F32)],
        compiler_params=_cp(("arbitrary",), VMEM_BIG),
    )(x, dres, mod, dproj, winp)


_GROUPS = ((0, LW), (LW, LW + AW), (LW + AW, D))


def _mix_out_fwd(x, ylru, osb, ofox, mod, gmix, wout, l, S):
    T = x.shape[0]
    tm = _tile(S, 512)
    tpb = S // tm

    def body(x_ref, yl_ref, sb_ref, fx_ref, mod_ref, gm_ref, w_ref, xo_ref, y_ref, mo_ref):
        for src, (lo, hi) in zip((yl_ref, sb_ref, fx_ref), _GROUPS):
            vn, _ = _rms_rows(src[...])
            y_ref[:, lo:hi] = (vn * gm_ref[0:1, lo:hi]).astype(BF16)
        mo = _dot(y_ref[...], w_ref[...])
        mo_ref[...] = mo.astype(BF16)
        xo_ref[...] = x_ref[...] + mod_ref[2:3, :] * mo

    return pl.pallas_call(
        body, name=f"mix_out_fwd_{l}",
        grid=(T // tm,),
        in_specs=[pl.BlockSpec((tm, D), lambda i: (i, 0)),
                  pl.BlockSpec((tm, LW), lambda i: (i, 0)),
                  pl.BlockSpec((tm, AW), lambda i: (i, 0)),
                  pl.BlockSpec((tm, AW), lambda i: (i, 0)),
                  pl.BlockSpec((None, 8, D), lambda i: (i // tpb, 0, 0)),
                  pl.BlockSpec((None, 8, D), lambda i: (l, 0, 0)),
                  pl.BlockSpec((D, D), lambda i: (0, 0))],
        out_specs=[pl.BlockSpec((tm, D), lambda i: (i, 0)),
                   pl.BlockSpec((tm, D), lambda i: (i, 0)),
                   pl.BlockSpec((tm, D), lambda i: (i, 0))],
        out_shape=[jax.ShapeDtypeStruct((T, D), F32),
                   jax.ShapeDtypeStruct((T, D), BF16),
                   jax.ShapeDtypeStruct((T, D), BF16)],
        compiler_params=_cp(("arbitrary",), VMEM_BIG),
    )(x, ylru, osb, ofox, mod, gmix, wout)


def _mix_out_bwd(dx2, ylru, osb, ofox, mo, mod, gmix, wout, l, S):
    T = dx2.shape[0]
    tm = _tile(S, 512)
    tpb = S // tm

    def body(dx_ref, yl_ref, sb_ref, fx_ref, mo_ref, mod_ref, gm_ref, w_ref,
             dyl_ref, dsb_ref, dfx_ref, dmo_ref, dmod_ref, wacc_ref):
        i = pl.program_id(0)

        @pl.when(i == 0)
        def _():
            wacc_ref[...] = jnp.zeros_like(wacc_ref)

        @pl.when(i % tpb == 0)
        def _():
            dmod_ref[...] = jnp.zeros_like(dmod_ref)

        dx = dx_ref[...]
        dmod_ref[2:3, :] += _colsum(mo_ref[...].astype(F32) * dx)
        dmo = (mod_ref[2:3, :] * dx).astype(BF16)
        dmo_ref[...] = dmo
        dy = _dot_nt(dmo, w_ref[...])
        for src, dst, (lo, hi) in zip((yl_ref, sb_ref, fx_ref), (dyl_ref, dsb_ref, dfx_ref), _GROUPS):
            vn, rstd = _rms_rows(src[...])
            dyg = dy[:, lo:hi]
            wacc_ref[0:1, lo:hi] += _colsum(dyg * vn)
            dst[...] = _rms_bwd(vn, rstd, dyg * gm_ref[0:1, lo:hi])

    return pl.pallas_call(
        body, name=f"mix_out_bwd_{l}",
        grid=(T // tm,),
        in_specs=[pl.BlockSpec((tm, D), lambda i: (i, 0)),
                  pl.BlockSpec((tm, LW), lambda i: (i, 0)),
                  pl.BlockSpec((tm, AW), lambda i: (i, 0)),
                  pl.BlockSpec((tm, AW), lambda i: (i, 0)),
                  pl.BlockSpec((tm, D), lambda i: (i, 0)),
                  pl.BlockSpec((None, 8, D), lambda i: (i // tpb, 0, 0)),
                  pl.BlockSpec((None, 8, D), lambda i: (l, 0, 0)),
                  pl.BlockSpec((D, D), lambda i: (0, 0))],
        out_specs=[pl.BlockSpec((tm, LW), lambda i: (i, 0)),
                   pl.BlockSpec((tm, AW), lambda i: (i, 0)),
                   pl.BlockSpec((tm, AW), lambda i: (i, 0)),
                   pl.BlockSpec((tm, D), lambda i: (i, 0)),
                   pl.BlockSpec((None, 8, D), lambda i: (i // tpb, 0, 0)),
                   pl.BlockSpec((8, D), lambda i: (0, 0))],
        out_shape=[jax.ShapeDtypeStruct((T, LW), F32),
                   jax.ShapeDtypeStruct((T, AW), F32),
                   jax.ShapeDtypeStruct((T, AW), F32),
                   jax.ShapeDtypeStruct((T, D), BF16),
                   jax.ShapeDtypeStruct((T // S, 8, D), F32),
                   jax.ShapeDtypeStruct((8, D), F32)],
        compiler_params=_cp(("arbitrary",), VMEM_BIG),
    )(dx2, ylru, osb, ofox, mo, mod, gmix, wout)


def _loss_head(y, tgt, S):
    T = y.shape[0]
    tm = _tile(S, 512)

    def body(y_ref, t_ref, dy_ref, l_ref):
        @pl.when(pl.program_id(0) == 0)
        def _():
            l_ref[...] = jnp.zeros_like(l_ref)

        d = y_ref[...] - t_ref[...]
        dy_ref[...] = d * (1.0 / D)
        l_ref[...] += (0.5 / D) * _rowsum(_colsum(d * d))

    return pl.pallas_call(
        body, name="loss_head",
        grid=(T // tm,),
        in_specs=[pl.BlockSpec((tm, D), lambda i: (i, 0)), pl.BlockSpec((tm, D), lambda i: (i, 0))],
        out_specs=[pl.BlockSpec((tm, D), lambda i: (i, 0)), pl.BlockSpec((8, 128), lambda i: (0, 0))],
        out_shape=[jax.ShapeDtypeStruct((T, D), F32), jax.ShapeDtypeStruct((8, 128), F32)],
        compiler_params=_cp(("arbitrary",)),
    )(y, tgt)


def _lru_gates(u, vp_ref, wr_ref, wi_ref):
    ub = u.astype(BF16)
    r = _sigmoid(_dot(ub, wr_ref[...]) + vp_ref[1:2, :])
    ig = _sigmoid(_dot(ub, wi_ref[...]) + vp_ref[2:3, :])
    lam = vp_ref[3:4, :]
    sp = jnp.maximum(-lam, 0.0) + _log1p(jnp.exp(-jnp.abs(lam)))
    log_a = (-LRU_C) * r * sp
    a = jnp.exp(log_a)
    mult = jnp.sqrt(-_expm1_neg(2.0 * log_a))
    return ub, r, ig, sp, a, mult


def _conv_taps(x, xp, row, cw_ref):
    xs = [x]
    for d in (1, 2, 3):
        xs.append(jnp.where(row >= d, pltpu.roll(x, d, 0), pltpu.roll(xp, d, 0)))
    u = xs[0] * cw_ref[3:4, :]
    for d in (1, 2, 3):
        u = u + xs[d] * cw_ref[3 - d:4 - d, :]
    return xs, u


def _lru_fwd(proj, cw, vp, wr, wi, l, S):
    T = proj.shape[0]
    ts = _tile(S, 256)
    nb = S // ts

    def body(x_ref, lg_ref, cw_ref, vp_ref, wr_ref, wi_ref, y_ref, h_ref, xp_sc, hc_sc):
        @pl.when(pl.program_id(1) == 0)
        def _():
            xp_sc[...] = jnp.zeros_like(xp_sc)
            hc_sc[...] = jnp.zeros_like(hc_sc)

        row = lax.broadcasted_iota(jnp.int32, (ts, LW), 0)
        x = x_ref[...]
        _, u = _conv_taps(x, xp_sc[...], row, cw_ref)
        u = u + vp_ref[0:1, :]
        xp_sc[...] = x
        _, _, ig, _, a, mult = _lru_gates(u, vp_ref, wr_ref, wi_ref)
        bv = mult * (ig * u)
        av = a
        d = 1
        while d < ts:
            a_s = jnp.where(row >= d, pltpu.roll(av, d, 0), 1.0)
            b_s = jnp.where(row >= d, pltpu.roll(bv, d, 0), 0.0)
            bv = av * b_s + bv
            av = av * a_s
            d *= 2
        h = bv + av * hc_sc[7:8, :]
        hc_sc[...] = h[ts - 8:ts, :]
        h_ref[...] = h
        gl, _ = _gelu_and_grad(lg_ref[...])
        y_ref[...] = h * gl

    return pl.pallas_call(
        body, name=f"lru_fwd_{l}",
        grid=(T // S, nb),
        in_specs=[pl.BlockSpec((ts, LW), lambda b, j: (b * nb + j, 0)),
                  pl.BlockSpec((ts, LW), lambda b, j: (b * nb + j, 1)),
                  pl.BlockSpec((None, 8, LW), lambda b, j: (l, 0, 0)),
                  pl.BlockSpec((None, 8, LW), lambda b, j: (l, 0, 0)),
                  pl.BlockSpec((None, LW, LW), lambda b, j: (l, 0, 0)),
                  pl.BlockSpec((None, LW, LW), lambda b, j: (l, 0, 0))],
        out_specs=[pl.BlockSpec((ts, LW), lambda b, j: (b * nb + j, 0)),
                   pl.BlockSpec((ts, LW), lambda b, j: (b * nb + j, 0))],
        out_shape=[jax.ShapeDtypeStruct((T, LW), F32), jax.ShapeDtypeStruct((T, LW), F32)],
        scratch_shapes=[pltpu.VMEM((ts, LW), F32), pltpu.VMEM((8, LW), F32)],
        compiler_params=_cp(("arbitrary", "arbitrary")),
    )(proj, proj, cw, vp, wr, wi)


def _lru_bwd(dyl, proj, h, cw, vp, wr, wi, l, S):
    T = proj.shape[0]
    ts = _tile(S, 256)
    nb = S // ts

    def body(dy_ref, x_ref, xprev_ref, lg_ref, h_ref, hprev_ref, cw_ref, vp_ref, wr_ref, wi_ref,
             dx_ref, dlg_ref, dpr_ref, dpi_ref, ub_ref, wacc_ref, gc_sc, af_sc, dun_sc):
        b = pl.program_id(0)
        j = pl.program_id(1)
        first = j == nb - 1

        @pl.when((b == 0) & (j == 0))
        def _():
            wacc_ref[...] = jnp.zeros_like(wacc_ref)

        @pl.when(j == 0)
        def _():
            gc_sc[...] = jnp.zeros_like(gc_sc)
            af_sc[...] = jnp.ones_like(af_sc)
            dun_sc[...] = jnp.zeros_like(dun_sc)

        row = lax.broadcasted_iota(jnp.int32, (ts, LW), 0)
        keep = jnp.where(first, 0.0, 1.0)
        x = x_ref[...]
        xs, u = _conv_taps(x, xprev_ref[...] * keep, row, cw_ref)
        u = u + vp_ref[0:1, :]
        ub, r, ig, sp, a, mult = _lru_gates(u, vp_ref, wr_ref, wi_ref)
        ub_ref[...] = ub
        hh = h_ref[...]
        h_m1 = jnp.where(row >= 1, pltpu.roll(hh, 1, 0), pltpu.roll(hprev_ref[...] * keep, 1, 0))
        dy = dy_ref[...]
        gl, dgl = _gelu_and_grad(lg_ref[...])
        dlg_ref[...] = (dy * hh * dgl).astype(BF16)
        bv = dy * gl
        av = jnp.where(row < ts - 1, pltpu.roll(a, ts - 1, 0), af_sc[0:1, :])
        d = 1
        while d < ts:
            a_s = jnp.where(row < ts - d, pltpu.roll(av, ts - d, 0), 1.0)
            b_s = jnp.where(row < ts - d, pltpu.roll(bv, ts - d, 0), 0.0)
            bv = av * b_s + bv
            av = av * a_s
            d *= 2
        gt = bv + av * gc_sc[0:1, :]
        gc_sc[...] = gt[0:8, :]
        af_sc[...] = a[0:8, :]
        da = gt * h_m1
        d_ig = gt * mult * u
        d_mult = gt * ig * u
        du = gt * mult * ig
        dlog_a = da * a - d_mult * (a * a) / mult
        dpre_r = (dlog_a * ((-LRU_C) * sp)) * r * (1.0 - r)
        dpre_i = d_ig * ig * (1.0 - ig)
        lam = vp_ref[3:4, :]
        wacc_ref[7:8, :] += _colsum(dlog_a * r) * (LRU_C * _sigmoid(-lam))
        wacc_ref[5:6, :] += _colsum(dpre_r)
        wacc_ref[6:7, :] += _colsum(dpre_i)
        dprb = dpre_r.astype(BF16)
        dpib = dpre_i.astype(BF16)
        dpr_ref[...] = dprb
        dpi_ref[...] = dpib
        du = du + _dot_nt(dprb, wr_ref[...]) + _dot_nt(dpib, wi_ref[...])
        wacc_ref[4:5, :] += _colsum(du)
        dun = dun_sc[...]
        dx = du * cw_ref[3:4, :]
        wacc_ref[3:4, :] += _colsum(du * xs[0])
        for dd in (1, 2, 3):
            du_s = jnp.where(row < ts - dd, pltpu.roll(du, ts - dd, 0), pltpu.roll(dun, ts - dd, 0))
            dx = dx + du_s * cw_ref[3 - dd:4 - dd, :]
            wacc_ref[3 - dd:4 - dd, :] += _colsum(du * xs[dd])
        dun_sc[...] = du
        dx_ref[...] = dx.astype(BF16)

    def tb(b, j):
        return b * nb + (nb - 1 - j)

    def tbp(b, j):
        return b * nb + jnp.maximum(nb - 2 - j, 0)

    return pl.pallas_call(
        body, name=f"lru_bwd_{l}",
        grid=(T // S, nb),
        in_specs=[pl.BlockSpec((ts, LW), lambda b, j: (tb(b, j), 0)),
                  pl.BlockSpec((ts, LW), lambda b, j: (tb(b, j), 0)),
                  pl.BlockSpec((ts, LW), lambda b, j: (tbp(b, j), 0)),
                  pl.BlockSpec((ts, LW), lambda b, j: (tb(b, j), 1)),
                  pl.BlockSpec((ts, LW), lambda b, j: (tb(b, j), 0)),
                  pl.BlockSpec((ts, LW), lambda b, j: (tbp(b, j), 0)),
                  pl.BlockSpec((None, 8, LW), lambda b, j: (l, 0, 0)),
                  pl.BlockSpec((None, 8, LW), lambda b, j: (l, 0, 0)),
                  pl.BlockSpec((None, LW, LW), lambda b, j: (l, 0, 0)),
                  pl.BlockSpec((None, LW, LW), lambda b, j: (l, 0, 0))],
        out_specs=[pl.BlockSpec((ts, LW), lambda b, j: (tb(b, j), 0)),
                   pl.BlockSpec((ts, LW), lambda b, j: (tb(b, j), 0)),
                   pl.BlockSpec((ts, LW), lambda b, j: (tb(b, j), 0)),
                   pl.BlockSpec((ts, LW), lambda b, j: (tb(b, j), 0)),
                   pl.BlockSpec((ts, LW), lambda b, j: (tb(b, j), 0)),
                   pl.BlockSpec((8, LW), lambda b, j: (0, 0))],
        out_shape=[jax.ShapeDtypeStruct((T, LW), BF16),
                   jax.ShapeDtypeStruct((T, LW), BF16),
                   jax.ShapeDtypeStruct((T, LW), BF16),
                   jax.ShapeDtypeStruct((T, LW), BF16),
                   jax.ShapeDtypeStruct((T, LW), BF16),
                   jax.ShapeDtypeStruct((8, LW), F32)],
        scratch_shapes=[pltpu.VMEM((8, LW), F32), pltpu.VMEM((8, LW), F32), pltpu.VMEM((ts, LW), F32)],
        compiler_params=_cp(("arbitrary", "arbitrary")),
    )(dyl, proj, proj, proj, h, h, cw, vp, wr, wi)


def _fgate_fwd(proj, bfp, l, S):
    T = proj.shape[0]

    def body(x_ref, b_ref, o_ref):
        z = x_ref[...] + b_ref[0:1, :]
        v = jnp.minimum(z, 0.0) - _log1p(jnp.exp(-jnp.abs(z)))
        row = lax.broadcasted_iota(jnp.int32, (S, 128), 0)
        d = 1
        while d < S:
            v = v + jnp.where(row >= d, pltpu.roll(v, d, 0), 0.0)
            d *= 2
        o_ref[...] = v

    return pl.pallas_call(
        body, name=f"fgate_fwd_{l}",
        grid=(T // S,),
        in_specs=[pl.BlockSpec((S, 128), lambda b: (b, F_BLK)),
                  pl.BlockSpec((None, 8, 128), lambda b: (l, 0, 0))],
        out_specs=pl.BlockSpec((S, 128), lambda b: (b, 0)),
        out_shape=jax.ShapeDtypeStruct((T, 128), F32),
        compiler_params=_cp(("arbitrary",)),
    )(proj, bfp)


def _fgate_bwd(dcum, proj, bfp, l, S):
    T = proj.shape[0]

    def body(d_ref, x_ref, b_ref, o_ref, wacc_ref):
        @pl.when(pl.program_id(0) == 0)
        def _():
            wacc_ref[...] = jnp.zeros_like(wacc_ref)

        v = d_ref[...]
        row = lax.broadcasted_iota(jnp.int32, (S, 128), 0)
        d = 1
        while d < S:
            v = v + jnp.where(row < S - d, pltpu.roll(v, S - d, 0), 0.0)
            d *= 2
        z = x_ref[...] + b_ref[0:1, :]
        dz = v * _sigmoid(-z)
        o_ref[...] = dz.astype(BF16)
        wacc_ref[0:1, :] += _colsum(dz)

    return pl.pallas_call(
        body, name=f"fgate_bwd_{l}",
        grid=(T // S,),
        in_specs=[pl.BlockSpec((S, 128), lambda b: (b, 0)),
                  pl.BlockSpec((S, 128), lambda b: (b, F_BLK)),
                  pl.BlockSpec((None, 8, 128), lambda b: (l, 0, 0))],
        out_specs=[pl.BlockSpec((S, 128), lambda b: (b, 0)), pl.BlockSpec((8, 128), lambda b: (0, 0))],
        out_shape=[jax.ShapeDtypeStruct((T, 128), BF16), jax.ShapeDtypeStruct((8, 128), F32)],
        compiler_params=_cp(("arbitrary",)),
    )(dcum, proj, bfp)


def _logsig_parts(z):
    e = jnp.exp(-jnp.abs(z))
    l1p = jnp.log(1.0 + e)
    return e, jnp.minimum(z, 0.0) - l1p, -jnp.maximum(z, 0.0) - l1p


def _sb_fwd(q, k, v, l):
    B, H, nq, tq, _ = q.shape
    nk, tk = k.shape[2], k.shape[3]
    rr = tq // tk

    def body(q_ref, k_ref, v_ref, o_ref, t1_ref):
        tri = _tri(tk, "row_gt_col")
        ti = lax.broadcasted_iota(jnp.int32, (tq, 1), 0)
        si = lax.broadcasted_iota(jnp.int32, (1, tk), 1)

        def qloop(qb, carry):
            qq = q_ref[qb]
            tpos = qb * tq + ti
            nkb = (qb + 1) * rr

            def kloop(i, c):
                acc, run = c
                kb = nkb - 1 - i
                z = _dot_nt(qq, k_ref[kb]) * SCALE
                past = (kb * tk + si) < tpos
                _, lb, l1 = _logsig_parts(z)
                l1m = jnp.where(past, l1, 0.0)
                aft = _cumsum_mm(l1m, tri) + run
                w = jnp.where(past, jnp.exp(lb + aft), 0.0)
                acc = acc + _dot(w.astype(BF16), v_ref[kb])
                return acc, run + _rowsum(l1m)

            acc, run = lax.fori_loop(0, nkb, kloop, (jnp.zeros((tq, HD), F32), jnp.zeros((tq, 1), F32)))
            o_ref[qb] = acc
            t1_ref[qb] = run
            return carry

        lax.fori_loop(0, nq, qloop, 0)

    qs = pl.BlockSpec((None, None, nq, tq, HD), lambda b, h: (b, h, 0, 0, 0))
    ks = pl.BlockSpec((None, None, nk, tk, HD), lambda b, h: (b, h, 0, 0, 0))
    return pl.pallas_call(
        body, name=f"sb_fwd_{l}",
        grid=(B, H),
        in_specs=[qs, ks, ks],
        out_specs=[qs, pl.BlockSpec((None, None, nq, tq, 1), lambda b, h: (b, h, 0, 0, 0))],
        out_shape=[jax.ShapeDtypeStruct((B, H, nq, tq, HD), F32),
                   jax.ShapeDtypeStruct((B, H, nq, tq, 1), F32)],
        compiler_params=_cp(("arbitrary", "arbitrary"), VMEM_BIG),
    )(q, k, v)


def _sb_bwd(q, k, v, do, t1, l):
    B, H, nq, tq, _ = q.shape
    nk, tk = k.shape[2], k.shape[3]
    rr = tq // tk

    def body(q_ref, k_ref, v_ref, do_ref, t1_ref, dq_ref, dk_ref, dv_ref, dk_sc, dv_sc):
        dk_sc[...] = jnp.zeros_like(dk_sc)
        dv_sc[...] = jnp.zeros_like(dv_sc)
        tri_in = _tri(tk, "row_le_col")
        tri_ex = _tri(tk, "row_lt_col")
        ti = lax.broadcasted_iota(jnp.int32, (tq, 1), 0)
        si = lax.broadcasted_iota(jnp.int32, (1, tk), 1)

        def qloop(qb, carry):
            qq = q_ref[qb]
            dob = do_ref[qb].astype(BF16)
            tot = t1_ref[qb]
            tpos = qb * tq + ti
            nkb = (qb + 1) * rr

            def kloop(kb, c):
                dq, run1, rung = c
                kk = k_ref[kb]
                vv = v_ref[kb]
                z = _dot_nt(qq, kk) * SCALE
                past = (kb * tk + si) < tpos
                e, lb, l1 = _logsig_parts(z)
                l1m = jnp.where(past, l1, 0.0)
                aft = tot - (run1 + _cumsum_mm(l1m, tri_in))
                w = jnp.where(past, jnp.exp(lb + aft), 0.0)
                gm = w * _dot_nt(dob, vv)
                cpre = rung + _cumsum_mm(gm, tri_ex, parts=2)
                inv = 1.0 / (1.0 + e)
                sig = jnp.where(z >= 0.0, inv, e * inv)
                dz = jnp.where(past, gm * (1.0 - sig) - cpre * sig, 0.0).astype(BF16)
                dv_sc[kb] += _dot_tn(w.astype(BF16), dob)
                dk_sc[kb] += _dot_tn(dz, qq) * SCALE
                dq = dq + _dot(dz, kk) * SCALE
                return dq, run1 + _rowsum(l1m), rung + _rowsum(gm)

            z1 = jnp.zeros((tq, 1), F32)
            dq, _, _ = lax.fori_loop(0, nkb, kloop, (jnp.zeros((tq, HD), F32), z1, z1))
            dq_ref[qb] = dq.astype(BF16)
            return carry

        lax.fori_loop(0, nq, qloop, 0)
        dk_ref[...] = dk_sc[...].astype(BF16)
        dv_ref[...] = dv_sc[...].astype(BF16)

    qs = pl.BlockSpec((None, None, nq, tq, HD), lambda b, h: (b, h, 0, 0, 0))
    ks = pl.BlockSpec((None, None, nk, tk, HD), lambda b, h: (b, h, 0, 0, 0))
    return pl.pallas_call(
        body, name=f"sb_bwd_{l}",
        grid=(B, H),
        in_specs=[qs, ks, ks, qs, pl.BlockSpec((None, None, nq, tq, 1), lambda b, h: (b, h, 0, 0, 0))],
        out_specs=[qs, ks, ks],
        out_shape=[jax.ShapeDtypeStruct((B, H, nq, tq, HD), BF16),
                   jax.ShapeDtypeStruct((B, H, nk, tk, HD), BF16),
                   jax.ShapeDtypeStruct((B, H, nk, tk, HD), BF16)],
        scratch_shapes=[pltpu.VMEM((nk, tk, HD), F32), pltpu.VMEM((nk, tk, HD), F32)],
        compiler_params=_cp(("arbitrary", "arbitrary"), VMEM_BIG),
    )(q, k, v, do, t1)


def _fox_fwd(q, k, v, cq, ck, gqk, l):
    B, H, nq, tq, _ = q.shape
    nk, tk = k.shape[2], k.shape[3]
    rr = tq // tk

    def body(q_ref, k_ref, v_ref, cq_ref, ck_ref, g_ref, o_ref, lse_ref, fk_sc):
        g0 = g_ref[0:1, :]
        g1 = g_ref[1:2, :]

        def kprep(kb, c):
            kn, _ = _rms_rows(k_ref[kb])
            fk_sc[kb] = (kn * g1).astype(BF16)
            return c

        lax.fori_loop(0, nk, kprep, 0)
        ti = lax.broadcasted_iota(jnp.int32, (tq, 1), 0)
        si = lax.broadcasted_iota(jnp.int32, (1, tk), 1)

        def qloop(qb, carry):
            qn, _ = _rms_rows(q_ref[qb])
            fq = (qn * g0).astype(BF16)
            cqq = cq_ref[qb]
            tpos = qb * tq + ti

            def kloop(kb, c):
                m, lsum, acc = c
                s = _dot_nt(fq, fk_sc[kb]) * SCALE + cqq - ck_ref[kb]
                s = jnp.where((kb * tk + si) <= tpos, s, NEG)
                m2 = jnp.maximum(m, jnp.max(s, axis=1, keepdims=True))
                al = jnp.exp(m - m2)
                p = jnp.exp(s - m2)
                return m2, al * lsum + _rowsum(p), al * acc + _dot(p.astype(BF16), v_ref[kb])

            m, lsum, acc = lax.fori_loop(
                0, (qb + 1) * rr, kloop,
                (jnp.full((tq, 1), NEG, F32), jnp.zeros((tq, 1), F32), jnp.zeros((tq, HD), F32)))
            o_ref[qb] = acc / lsum
            lse_ref[qb] = m + jnp.log(lsum)
            return carry

        lax.fori_loop(0, nq, qloop, 0)

    qs = pl.BlockSpec((None, None, nq, tq, HD), lambda b, h: (b, h, 0, 0, 0))
    ks = pl.BlockSpec((None, None, nk, tk, HD), lambda b, h: (b, h, 0, 0, 0))
    cqs = pl.BlockSpec((None, None, nq, tq, 1), lambda b, h: (b, h, 0, 0, 0))
    cks = pl.BlockSpec((None, None, nk, 1, tk), lambda b, h: (b, h, 0, 0, 0))
    return pl.pallas_call(
        body, name=f"fox_fwd_{l}",
        grid=(B, H),
        in_specs=[qs, ks, ks, cqs, cks, pl.BlockSpec((None, 8, HD), lambda b, h: (l, 0, 0))],
        out_specs=[qs, cqs],
        out_shape=[jax.ShapeDtypeStruct((B, H, nq, tq, HD), F32),
                   jax.ShapeDtypeStruct((B, H, nq, tq, 1), F32)],
        scratch_shapes=[pltpu.VMEM((nk, tk, HD), BF16)],
        compiler_params=_cp(("arbitrary", "arbitrary"), VMEM_BIG),
    )(q, k, v, cq, ck, gqk)


def _fox_bwd(q, k, v, cq, ck, gqk, do, lse, l):
    B, H, nq, tq, _ = q.shape
    nk, tk = k.shape[2], k.shape[3]
    rr = tq // tk

    def body(q_ref, k_ref, v_ref, cq_ref, ck_ref, g_ref, do_ref, lse_ref,
             dq_ref, dk_ref, dv_ref, dc_ref, wacc_ref, fk_sc, dfk_sc, dv_sc):
        @pl.when((pl.program_id(0) == 0) & (pl.program_id(1) == 0))
        def _():
            wacc_ref[...] = jnp.zeros_like(wacc_ref)

        g0 = g_ref[0:1, :]
        g1 = g_ref[1:2, :]
        dfk_sc[...] = jnp.zeros_like(dfk_sc)
        dv_sc[...] = jnp.zeros_like(dv_sc)
        dc_ref[...] = jnp.zeros_like(dc_ref)

        def kprep(kb, c):
            kn, _ = _rms_rows(k_ref[kb])
            fk_sc[kb] = (kn * g1).astype(BF16)
            return c

        lax.fori_loop(0, nk, kprep, 0)
        ti = lax.broadcasted_iota(jnp.int32, (tq, 1), 0)
        si = lax.broadcasted_iota(jnp.int32, (1, tk), 1)

        def qloop(qb, carry):
            qn, qr = _rms_rows(q_ref[qb])
            fq = (qn * g0).astype(BF16)
            cqq = cq_ref[qb]
            lse = lse_ref[qb]
            dob = do_ref[qb].astype(BF16)
            tpos = qb * tq + ti

            def probs(kb):
                s = _dot_nt(fq, fk_sc[kb]) * SCALE + cqq - ck_ref[kb]
                p = jnp.where((kb * tk + si) <= tpos, jnp.exp(s - lse), 0.0)
                return p, _dot_nt(dob, v_ref[kb])

            def dloop(kb, acc):
                p, dp = probs(kb)
                return acc + _rowsum(p * dp)

            dlt = lax.fori_loop(0, (qb + 1) * rr, dloop, jnp.zeros((tq, 1), F32))

            def kloop(kb, dfq):
                fk = fk_sc[kb]
                p, dp = probs(kb)
                ds = p * (dp - dlt)
                dsb = ds.astype(BF16)
                dv_sc[kb] += _dot_tn(p.astype(BF16), dob)
                dfk_sc[kb] += _dot_tn(dsb, fq) * SCALE
                dc_ref[kb] += jnp.broadcast_to(-_colsum(ds), (8, tk))
                return dfq + _dot(dsb, fk) * SCALE

            dfq = lax.fori_loop(0, (qb + 1) * rr, kloop, jnp.zeros((tq, HD), F32))
            wacc_ref[0:1, :] += _colsum(dfq * qn)
            dq_ref[qb] = _rms_bwd(qn, qr, dfq * g0).astype(BF16)
            return carry

        lax.fori_loop(0, nq, qloop, 0)

        def kfin(kb, c):
            kn, kr = _rms_rows(k_ref[kb])
            dfk = dfk_sc[kb]
            wacc_ref[1:2, :] += _colsum(dfk * kn)
            dk_ref[kb] = _rms_bwd(kn, kr, dfk * g1).astype(BF16)
            return c

        lax.fori_loop(0, nk, kfin, 0)
        dv_ref[...] = dv_sc[...].astype(BF16)

    qs = pl.BlockSpec((None, None, nq, tq, HD), lambda b, h: (b, h, 0, 0, 0))
    ks = pl.BlockSpec((None, None, nk, tk, HD), lambda b, h: (b, h, 0, 0, 0))
    cqs = pl.BlockSpec((None, None, nq, tq, 1), lambda b, h: (b, h, 0, 0, 0))
    cks = pl.BlockSpec((None, None, nk, 1, tk), lambda b, h: (b, h, 0, 0, 0))
    return pl.pallas_call(
        body, name=f"fox_bwd_{l}",
        grid=(B, H),
        in_specs=[qs, ks, ks, cqs, cks, pl.BlockSpec((None, 8, HD), lambda b, h: (l, 0, 0)), qs, cqs],
        out_specs=[qs, ks, ks,
                   pl.BlockSpec((None, None, nk, 8, tk), lambda b, h: (b, h, 0, 0, 0)),
                   pl.BlockSpec((8, HD), lambda b, h: (0, 0))],
        out_shape=[jax.ShapeDtypeStruct((B, H, nq, tq, HD), BF16),
                   jax.ShapeDtypeStruct((B, H, nk, tk, HD), BF16),
                   jax.ShapeDtypeStruct((B, H, nk, tk, HD), BF16),
                   jax.ShapeDtypeStruct((B, H, nk, 8, tk), F32),
                   jax.ShapeDtypeStruct((8, HD), F32)],
        scratch_shapes=[pltpu.VMEM((nk, tk, HD), BF16), pltpu.VMEM((nk, tk, HD), F32),
                        pltpu.VMEM((nk, tk, HD), F32)],
        compiler_params=_cp(("arbitrary", "arbitrary"), VMEM_BIG),
    )(q, k, v, cq, ck, gqk, do, lse)


SBQ_BLK, SBK_BLK, SBV_BLK = 8, 10, 12
FXQ_BLK, FXK_BLK, FXV_BLK = 14, 16, 18
PAIR = 2 * HD


def _lane_masks():
    lane = lax.broadcasted_iota(jnp.int32, (1, PAIR), 1)
    return lane, lane < HD


def _pair_select(m0, a0, a1):
    return jnp.where(m0, a0, a1)


def _pair_split(x, m0):
    return jnp.where(m0, x, 0.0).astype(BF16), jnp.where(m0, 0.0, x).astype(BF16)


def _pair_mean(x, m0):
    s0 = _rowsum(jnp.where(m0, x, 0.0))
    s1 = _rowsum(x) - s0
    return jnp.where(m0, s0, s1) * (1.0 / HD)


def _pair_rms(x, m0):
    rstd = lax.rsqrt(_pair_mean(x * x, m0) + EPS)
    return x * rstd, rstd


def _pair_rms_bwd(xn, rstd, dyn, m0):
    return rstd * (dyn - xn * _pair_mean(dyn * xn, m0))


def _logsig2(z):
    l1p = jnp.log(1.0 + jnp.exp(-jnp.abs(z)))
    lb = jnp.minimum(z, 0.0) - l1p
    return lb, lb - z


def _rows(ref, blk, size):
    return ref[pl.ds(pl.multiple_of(blk * size, size), size), :]


def _sbp_fwd(proj, l, S):
    T = proj.shape[0]
    tq, tk = TQ_(S), TK_(S)
    assert tq == 2 * tk
    nq = S // tq

    def body(q_ref, k_ref, v_ref, o_ref, t1_ref, kb_sc, vb_sc):
        kb_sc[...] = k_ref[...].astype(BF16)
        vb_sc[...] = v_ref[...].astype(BF16)
        lane, m0 = _lane_masks()
        tri = _tri(tk, "row_gt_col")
        ti = lax.broadcasted_iota(jnp.int32, (tq, 1), 0)
        si = lax.broadcasted_iota(jnp.int32, (1, tk), 1)

        def qloop(qb, carry):
            qh = _pair_split(_rows(q_ref, qb, tq) * SCALE, m0)
            tpos = qb * tq + ti

            def step(kbs, c, masked):
                pre = []
                for h in range(2):
                    for kb in kbs:
                        z = _dot_nt(qh[h], _rows(kb_sc, kb, tk))
                        lb, l1 = _logsig2(z)
                        past = None
                        if masked:
                            past = (kb * tk + si) < tpos
                            l1 = jnp.where(past, l1, 0.0)
                        pre.append((lb, l1, _cumsum_mm(l1, tri, parts=2), past))
                out = []
                for h in range(2):
                    acc, run = c[h]
                    for n, kb in enumerate(kbs):
                        lb, l1, cs, past = pre[2 * h + n]
                        w = jnp.exp(lb + (cs + run))
                        if masked:
                            w = jnp.where(past, w, 0.0)
                        acc = acc + _dot(w.astype(BF16), _rows(vb_sc, kb, tk))
                        run = run + (cs[:, 0:1] + l1[:, 0:1])
                    out.append((acc, run))
                return tuple(out)

            zero = (jnp.zeros((tq, PAIR), F32), jnp.zeros((tq, 1), F32))
            c = step((2 * qb + 1, 2 * qb), (zero, zero), True)
            c = lax.fori_loop(0, qb, lambda i, cc: step((2 * (qb - i) - 1, 2 * (qb - i) - 2), cc, False), c)
            r0 = pl.multiple_of(qb * tq, tq)
            o_ref[pl.ds(r0, tq), :] = _pair_select(m0, c[0][0], c[1][0])
            t1_ref[pl.ds(r0, tq), :] = jnp.where(lane == 0, c[0][1], jnp.where(lane == 1, c[1][1], 0.0))
            return carry

        lax.fori_loop(0, nq, qloop, 0)

    def col(blk):
        return pl.BlockSpec((S, PAIR), lambda b, p: (b, blk + p))

    return pl.pallas_call(
        body, name=f"sb_fwd_{l}",
        grid=(T // S, 2),
        in_specs=[col(SBQ_BLK), col(SBK_BLK), col(SBV_BLK)],
        out_specs=[col(0), col(0)],
        out_shape=[jax.ShapeDtypeStruct((T, AW), F32), jax.ShapeDtypeStruct((T, AW), F32)],
        scratch_shapes=[pltpu.VMEM((S, PAIR), BF16), pltpu.VMEM((S, PAIR), BF16)],
        compiler_params=_cp(("arbitrary", "arbitrary"), VMEM_BIG),
    )(proj, proj, proj)


def _sbp_bwd(proj, do, t1, l, S):
    T = proj.shape[0]
    tq, tk = TQ_(S), TK_(S)
    assert tq == 2 * tk
    nq = S // tq

    def body(q_ref, k_ref, v_ref, do_ref, t1_ref, dq_ref, dk_ref, dv_ref, kb_sc, vb_sc, dk_sc, dv_sc):
        kb_sc[...] = k_ref[...].astype(BF16)
        vb_sc[...] = v_ref[...].astype(BF16)
        dk_sc[...] = jnp.zeros_like(dk_sc)
        dv_sc[...] = jnp.zeros_like(dv_sc)
        _, m0 = _lane_masks()
        tri_in = _tri(tk, "row_le_col")
        tri_ex = _tri(tk, "row_lt_col")
        ti = lax.broadcasted_iota(jnp.int32, (tq, 1), 0)
        si = lax.broadcasted_iota(jnp.int32, (1, tk), 1)

        def qloop(qb, carry):
            qh = _pair_split(_rows(q_ref, qb, tq) * SCALE, m0)
            doh = _pair_split(_rows(do_ref, qb, tq), m0)
            t1v = _rows(t1_ref, qb, tq)
            tot = (t1v[:, 0:1], t1v[:, 1:2])
            tpos = qb * tq + ti

            def step(kbs, c, masked):
                pre = []
                for h in range(2):
                    for kb in kbs:
                        kk = _rows(kb_sc, kb, tk)
                        z = _dot_nt(qh[h], kk)
                        lb, l1 = _logsig2(z)
                        past = None
                        if masked:
                            past = (kb * tk + si) < tpos
                            l1 = jnp.where(past, l1, 0.0)
                        sig = jnp.exp(lb)
                        pre.append((lb, sig, _cumsum_mm(l1, tri_in), _dot_nt(doh[h], _rows(vb_sc, kb, tk)), past, kk))
                out = []
                for h in range(2):
                    dq, run1, rung = c[h]
                    for n, kb in enumerate(kbs):
                        lb, sig, p1, dw, past, kk = pre[2 * h + n]
                        w = jnp.exp(lb + (tot[h] - (run1 + p1)))
                        if masked:
                            w = jnp.where(past, w, 0.0)
                        gm = w * dw
                        cx = _cumsum_mm(gm, tri_ex, parts=2)
                        dz = gm - (gm + (rung + cx)) * sig
                        if masked:
                            dz = jnp.where(past, dz, 0.0)
                        dz = dz.astype(BF16)
                        r = pl.ds(pl.multiple_of(kb * tk, tk), tk)
                        dv_sc[r, :] += _dot_tn(w.astype(BF16), doh[h])
                        dk_sc[r, :] += _dot_tn(dz, qh[h])
                        dq = dq + _dot(dz, kk)
                        run1 = run1 + p1[:, tk - 1:tk]
                        rung = rung + (cx[:, tk - 1:tk] + gm[:, tk - 1:tk])
                    out.append((dq, run1, rung))
                return tuple(out)

            z1 = jnp.zeros((tq, 1), F32)
            zero = (jnp.zeros((tq, PAIR), F32), z1, z1)
            c = lax.fori_loop(0, qb, lambda i, cc: step((2 * i, 2 * i + 1), cc, False), (zero, zero))
            c = step((2 * qb, 2 * qb + 1), c, True)
            r0 = pl.multiple_of(qb * tq, tq)
            dq_ref[pl.ds(r0, tq), :] = (_pair_select(m0, c[0][0], c[1][0]) * SCALE).astype(BF16)
            return carry

        lax.fori_loop(0, nq, qloop, 0)
        dk_ref[...] = dk_sc[...].astype(BF16)
        dv_ref[...] = dv_sc[...].astype(BF16)

    def col(blk):
        return pl.BlockSpec((S, PAIR), lambda b, p: (b, blk + p))

    sh = jax.ShapeDtypeStruct((T, AW), BF16)
    return pl.pallas_call(
        body, name=f"sb_bwd_{l}",
        grid=(T // S, 2),
        in_specs=[col(SBQ_BLK), col(SBK_BLK), col(SBV_BLK), col(0), col(0)],
        out_specs=[col(0), col(0), col(0)],
        out_shape=[sh, sh, sh],
        scratch_shapes=[pltpu.VMEM((S, PAIR), BF16), pltpu.VMEM((S, PAIR), BF16),
                        pltpu.VMEM((S, PAIR), F32), pltpu.VMEM((S, PAIR), F32)],
        compiler_params=_cp(("arbitrary", "arbitrary"), VMEM_BIG),
    )(proj, proj, proj, do, t1)


def _foxp_fwd(proj, cum, ck, gqk2, l, S):
    T = proj.shape[0]
    tq, tk = TQ_(S), TK_(S)
    assert tq == 2 * tk
    nq, nk = S // tq, S // tk

    def body(q_ref, k_ref, v_ref, cum_ref, ck_ref, g_ref, o_ref, nl_ref, fk_sc, vb_sc):
        lane, m0 = _lane_masks()
        p = pl.program_id(1)
        kn, _ = _pair_rms(k_ref[...], m0)
        fk_sc[...] = (kn * g_ref[1:2, :]).astype(BF16)
        vb_sc[...] = v_ref[...].astype(BF16)
        ti = lax.broadcasted_iota(jnp.int32, (tq, 1), 0)
        si = lax.broadcasted_iota(jnp.int32, (1, tk), 1)

        def qloop(qb, carry):
            qn, _ = _pair_rms(_rows(q_ref, qb, tq), m0)
            fqh = _pair_split(qn * (g_ref[0:1, :] * SCALE), m0)
            cumv = _rows(cum_ref, qb, tq)
            cq = [_rowsum(jnp.where(lane == 2 * p + h, cumv, 0.0)) for h in range(2)]
            tpos = qb * tq + ti

            def step(kbs, c, masked):
                out = []
                for h in range(2):
                    m, lsum, acc = c[h]
                    ss = []
                    for kb in kbs:
                        s = _dot_nt(fqh[h], _rows(fk_sc, kb, tk)) + (cq[h] - ck_ref[h, kb])
                        if masked:
                            s = jnp.where((kb * tk + si) <= tpos, s, NEG)
                        ss.append(s)
                    m2 = jnp.maximum(m, jnp.maximum(jnp.max(ss[0], axis=1, keepdims=True),
                                                    jnp.max(ss[1], axis=1, keepdims=True)))
                    al = jnp.exp(m - m2)
                    lsum = al * lsum
                    acc = al * acc
                    for s, kb in zip(ss, kbs):
                        pr = jnp.exp(s - m2)
                        lsum = lsum + _rowsum(pr)
                        acc = acc + _dot(pr.astype(BF16), _rows(vb_sc, kb, tk))
                    out.append((m2, lsum, acc))
                return tuple(out)

            zero = (jnp.full((tq, 1), NEG, F32), jnp.zeros((tq, 1), F32), jnp.zeros((tq, PAIR), F32))
            c = lax.fori_loop(0, qb, lambda i, cc: step((2 * i, 2 * i + 1), cc, False), (zero, zero))
            c = step((2 * qb, 2 * qb + 1), c, True)
            r0 = pl.multiple_of(qb * tq, tq)
            o_ref[pl.ds(r0, tq), :] = _pair_select(m0, c[0][2] / c[0][1], c[1][2] / c[1][1])
            nl = [cq[h] - (c[h][0] + jnp.log(c[h][1])) for h in range(2)]
            nl_ref[pl.ds(r0, tq), :] = jnp.where(lane == 0, nl[0], jnp.where(lane == 1, nl[1], 0.0))
            return carry

        lax.fori_loop(0, nq, qloop, 0)

    def col(blk):
        return pl.BlockSpec((S, PAIR), lambda b, p: (b, blk + p))

    return pl.pallas_call(
        body, name=f"fox_fwd_{l}",
        grid=(T // S, 2),
        in_specs=[col(FXQ_BLK), col(FXK_BLK), col(FXV_BLK),
                  pl.BlockSpec((S, 128), lambda b, p: (b, 0)),
                  pl.BlockSpec((None, 2, nk, 1, tk), lambda b, p: (b, p, 0, 0, 0)),
                  pl.BlockSpec((None, 8, PAIR), lambda b, p: (l, 0, 0))],
        out_specs=[col(0), col(0)],
        out_shape=[jax.ShapeDtypeStruct((T, AW), F32), jax.ShapeDtypeStruct((T, AW), F32)],
        scratch_shapes=[pltpu.VMEM((S, PAIR), BF16), pltpu.VMEM((S, PAIR), BF16)],
        compiler_params=_cp(("arbitrary", "arbitrary"), VMEM_BIG),
    )(proj, proj, proj, cum, ck, gqk2)


def _foxp_bwd(proj, do, nl, ck, gqk2, l, S):
    T = proj.shape[0]
    tq, tk = TQ_(S), TK_(S)
    assert tq == 2 * tk
    nq, nk = S // tq, S // tk

    def body(q_ref, k_ref, v_ref, do_ref, nl_ref, ck_ref, g_ref,
             dq_ref, dk_ref, dv_ref, dc_ref, wacc_ref, fk_sc, vb_sc, dfk_sc, dv_sc):
        @pl.when((pl.program_id(0) == 0) & (pl.program_id(1) == 0))
        def _():
            wacc_ref[...] = jnp.zeros_like(wacc_ref)

        _, m0 = _lane_masks()
        g0 = g_ref[0:1, :]
        g1 = g_ref[1:2, :]
        kn, kr = _pair_rms(k_ref[...], m0)
        fk_sc[...] = (kn * g1).astype(BF16)
        vb_sc[...] = v_ref[...].astype(BF16)
        dfk_sc[...] = jnp.zeros_like(dfk_sc)
        dv_sc[...] = jnp.zeros_like(dv_sc)
        dc_ref[...] = jnp.zeros_like(dc_ref)
        ti = lax.broadcasted_iota(jnp.int32, (tq, 1), 0)
        si = lax.broadcasted_iota(jnp.int32, (1, tk), 1)

        def qloop(qb, carry):
            qn, qr = _pair_rms(_rows(q_ref, qb, tq), m0)
            fqh = _pair_split(qn * (g0 * SCALE), m0)
            doh = _pair_split(_rows(do_ref, qb, tq), m0)
            nlv = _rows(nl_ref, qb, tq)
            cql = (nlv[:, 0:1], nlv[:, 1:2])
            tpos = qb * tq + ti

            def probs(h, kb, masked):
                s = _dot_nt(fqh[h], _rows(fk_sc, kb, tk)) + (cql[h] - ck_ref[h, kb])
                pr = jnp.exp(s)
                if masked:
                    pr = jnp.where((kb * tk + si) <= tpos, pr, 0.0)
                return pr, _dot_nt(doh[h], _rows(vb_sc, kb, tk))

            def dstep(kbs, c, masked):
                out = []
                for h in range(2):
                    acc = c[h]
                    for kb in kbs:
                        pr, dp = probs(h, kb, masked)
                        acc = acc + _rowsum(pr * dp)
                    out.append(acc)
                return tuple(out)

            z1 = jnp.zeros((tq, 1), F32)
            dlt = lax.fori_loop(0, qb, lambda i, cc: dstep((2 * i, 2 * i + 1), cc, False), (z1, z1))
            dlt = dstep((2 * qb, 2 * qb + 1), dlt, True)

            def step(kbs, c, masked):
                out = []
                for h in range(2):
                    dfq = c[h]
                    for kb in kbs:
                        pr, dp = probs(h, kb, masked)
                        ds = pr * (dp - dlt[h])
                        dsb = ds.astype(BF16)
                        r = pl.ds(pl.multiple_of(kb * tk, tk), tk)
                        dv_sc[r, :] += _dot_tn(pr.astype(BF16), doh[h])
                        dfk_sc[r, :] += _dot_tn(dsb, fqh[h])
                        dc_ref[h, kb] += jnp.broadcast_to(-_colsum(ds), (8, tk))
                        dfq = dfq + _dot(dsb, _rows(fk_sc, kb, tk))
                    out.append(dfq)
                return tuple(out)

            zq = jnp.zeros((tq, PAIR), F32)
            c = lax.fori_loop(0, qb, lambda i, cc: step((2 * i, 2 * i + 1), cc, False), (zq, zq))
            c = step((2 * qb, 2 * qb + 1), c, True)
            dfq = _pair_select(m0, c[0], c[1]) * SCALE
            wacc_ref[0:1, :] += _colsum(dfq * qn)
            r0 = pl.multiple_of(qb * tq, tq)
            dq_ref[pl.ds(r0, tq), :] = _pair_rms_bwd(qn, qr, dfq * g0, m0).astype(BF16)
            return carry

        lax.fori_loop(0, nq, qloop, 0)
        dfk = dfk_sc[...]
        wacc_ref[1:2, :] += _colsum(dfk * kn)
        dk_ref[...] = _pair_rms_bwd(kn, kr, dfk * g1, m0).astype(BF16)
        dv_ref[...] = dv_sc[...].astype(BF16)

    def col(blk):
        return pl.BlockSpec((S, PAIR), lambda b, p: (b, blk + p))

    sh = jax.ShapeDtypeStruct((T, AW), BF16)
    return pl.pallas_call(
        body, name=f"fox_bwd_{l}",
        grid=(T // S, 2),
        in_specs=[col(FXQ_BLK), col(FXK_BLK), col(FXV_BLK), col(0), col(0),
                  pl.BlockSpec((None, 2, nk, 1, tk), lambda b, p: (b, p, 0, 0, 0)),
                  pl.BlockSpec((None, 8, PAIR), lambda b, p: (l, 0, 0))],
        out_specs=[col(0), col(0), col(0),
                   pl.BlockSpec((None, 2, nk, 8, tk), lambda b, p: (b, p, 0, 0, 0)),
                   pl.BlockSpec((8, PAIR), lambda b, p: (0, 0))],
        out_shape=[sh, sh, sh,
                   jax.ShapeDtypeStruct((T // S, NH, nk, 8, tk), F32),
                   jax.ShapeDtypeStruct((8, PAIR), F32)],
        scratch_shapes=[pltpu.VMEM((S, PAIR), BF16), pltpu.VMEM((S, PAIR), BF16),
                        pltpu.VMEM((S, PAIR), F32), pltpu.VMEM((S, PAIR), F32)],
        compiler_params=_cp(("arbitrary", "arbitrary"), VMEM_BIG),
    )(proj, proj, proj, do, nl, ck, gqk2)


def _transpose_blocks(src_ref, dst_sc, nblk, blk):
    for kb in range(nblk):
        dst_sc[kb] = src_ref[kb * blk:(kb + 1) * blk, :].astype(F32).T.astype(BF16)


def _sbq_fwd(proj, l, S, comm=None):
    T = proj.shape[0]
    tb = TQ_(S)
    nb = S // tb
    nbat = T // S
    c_args, c_specs, c_outs, c_scr = _hosted(comm)
    n_ci, n_co = len(c_args), len(c_outs)

    def body(*refs):
        q_ref, k_ref, v_ref = refs[:3]
        c_in = refs[3:3 + n_ci]
        o_ref, t1_ref = refs[3 + n_ci:5 + n_ci]
        c_out = refs[5 + n_ci:5 + n_ci + n_co]
        kt_sc, vb_sc = refs[5 + n_ci + n_co:7 + n_ci + n_co]
        c_sems = refs[7 + n_ci + n_co:]
        step = pl.program_id(0) * 2 + pl.program_id(1)
        if comm is not None:
            @pl.when(step == 0)
            def _():
                comm["start"](c_in, c_out, c_sems)

        _transpose_blocks(k_ref, kt_sc, nb, tb)
        vb_sc[...] = v_ref[...].astype(BF16)
        lane, m0 = _lane_masks()
        tri = _tri(tb, "row_gt_col")
        past = lax.broadcasted_iota(jnp.int32, (tb, tb), 1) < lax.broadcasted_iota(jnp.int32, (tb, tb), 0)

        def qloop(qb, carry):
            qh = _pair_split(_rows(q_ref, qb, tb) * SCALE, m0)

            def scores(kb):
                return tuple(_dot(qh[h], kt_sc[kb]) for h in range(2))

            def block(kb, kb_next, z, c, masked):
                mid = []
                for h in range(2):
                    lb, l1 = _logsig2(z[h])
                    if masked:
                        l1 = jnp.where(past, l1, 0.0)
                    mid.append((lb, l1, _cumsum_mm(l1, tri, parts=2)))
                z_next = scores(kb_next)
                pv, runs = [], []
                for h in range(2):
                    lb, l1, cs = mid[h]
                    w = jnp.exp(lb + (cs + c[h][1]))
                    if masked:
                        w = jnp.where(past, w, 0.0)
                    pv.append(_dot(w.astype(BF16), _rows(vb_sc, kb, tb)))
                    runs.append(c[h][1] + (cs[:, 0:1] + l1[:, 0:1]))
                return z_next, tuple((c[h][0] + pv[h], runs[h]) for h in range(2))

            zero = (jnp.zeros((tb, PAIR), F32), jnp.zeros((tb, 1), F32))
            z, c = block(qb, jnp.maximum(qb - 1, 0), scores(qb), (zero, zero), True)

            def off_diag(i, zc):
                kb = qb - 1 - i
                return block(kb, jnp.maximum(kb - 1, 0), zc[0], zc[1], False)

            _, c = lax.fori_loop(0, qb, off_diag, (z, c))
            r0 = pl.multiple_of(qb * tb, tb)
            o_ref[pl.ds(r0, tb), :] = _pair_select(m0, c[0][0], c[1][0])
            t1_ref[pl.ds(r0, tb), :] = jnp.where(lane == 0, c[0][1], jnp.where(lane == 1, c[1][1], 0.0))
            return carry

        lax.fori_loop(0, nb, qloop, 0)
        if comm is not None:
            @pl.when(step == 2 * nbat - 1)
            def _():
                comm["finish"](c_in, c_out, c_sems)

    def col(blk):
        return pl.BlockSpec((S, PAIR), lambda b, p: (b, blk + p))

    anyspec = pl.BlockSpec(memory_space=pl.ANY)
    out = pl.pallas_call(
        body, name=f"sb_fwd_{l}",
        grid=(nbat, 2),
        in_specs=[col(SBQ_BLK), col(SBK_BLK), col(SBV_BLK)] + c_specs,
        out_specs=[col(0), col(0)] + [anyspec] * n_co,
        out_shape=[jax.ShapeDtypeStruct((T, AW), F32), jax.ShapeDtypeStruct((T, AW), F32)] + c_outs,
        scratch_shapes=[pltpu.VMEM((nb, PAIR, tb), BF16), pltpu.VMEM((S, PAIR), BF16)] + c_scr,
        compiler_params=_cp(("arbitrary", "arbitrary"), VMEM_BIG),
    )(proj, proj, proj, *c_args)
    return out[0], out[1], list(out[2:])


def _sbq_bwd(proj, do, t1, l, S, comm=None):
    T = proj.shape[0]
    tb = TQ_(S)
    nb = S // tb
    nbat = T // S
    c_args, c_specs, c_outs, c_scr = _hosted(comm)
    n_ci, n_co = len(c_args), len(c_outs)

    def body(*refs):
        q_ref, k_ref, v_ref, do_ref, t1_ref = refs[:5]
        c_in = refs[5:5 + n_ci]
        dq_ref, dk_ref, dv_ref = refs[5 + n_ci:8 + n_ci]
        c_out = refs[8 + n_ci:8 + n_ci + n_co]
        kb_sc, kt_sc, vt_sc, dkt_sc, dvt_sc = refs[8 + n_ci + n_co:13 + n_ci + n_co]
        c_sems = refs[13 + n_ci + n_co:]
        step = pl.program_id(0) * 2 + pl.program_id(1)
        if comm is not None:
            @pl.when(step == 0)
            def _():
                comm["start"](c_in, c_out, c_sems)

        kb_sc[...] = k_ref[...].astype(BF16)
        _transpose_blocks(k_ref, kt_sc, nb, tb)
        _transpose_blocks(v_ref, vt_sc, nb, tb)
        dkt_sc[...] = jnp.zeros_like(dkt_sc)
        dvt_sc[...] = jnp.zeros_like(dvt_sc)
        _, m0 = _lane_masks()
        mt0 = lax.broadcasted_iota(jnp.int32, (PAIR, 1), 0) < HD
        tri_in = _tri(tb, "row_le_col")
        tri_ex = _tri(tb, "row_lt_col")
        past = lax.broadcasted_iota(jnp.int32, (tb, tb), 1) < lax.broadcasted_iota(jnp.int32, (tb, tb), 0)

        def qloop(qb, carry):
            qf = _rows(q_ref, qb, tb) * SCALE
            dof = _rows(do_ref, qb, tb)
            qh = _pair_split(qf, m0)
            doh = _pair_split(dof, m0)
            qth = _pair_split(qf.T, mt0)
            doth = _pair_split(dof.T, mt0)
            t1v = _rows(t1_ref, qb, tb)
            tot = (t1v[:, 0:1], t1v[:, 1:2])

            def block(kb, c, masked):
                hs = range(2)
                z = [_dot(qh[h], kt_sc[kb]) for h in hs]
                dw = [_dot(doh[h], vt_sc[kb]) for h in hs]
                st = []
                for h in hs:
                    lb, l1 = _logsig2(z[h])
                    if masked:
                        l1 = jnp.where(past, l1, 0.0)
                    st.append((lb, _cumsum_mm(l1, tri_in, parts=2)))
                mid = []
                for h in hs:
                    lb, p1 = st[h]
                    w = jnp.exp(lb + (tot[h] - (c[h][1] + p1)))
                    if masked:
                        w = jnp.where(past, w, 0.0)
                    gm = w * dw[h]
                    mid.append((w.astype(BF16), gm, _cumsum_mm(gm, tri_ex, parts=2)))
                out = []
                for h in hs:
                    dq, run1, rung = c[h]
                    wb, gm, cx = mid[h]
                    dz = gm - (gm + (rung + cx)) * jnp.exp(st[h][0])
                    if masked:
                        dz = jnp.where(past, dz, 0.0)
                    dz = dz.astype(BF16)
                    dvt_sc[kb] += _dot(doth[h], wb)
                    dkt_sc[kb] += _dot(qth[h], dz)
                    dq = dq + _dot(dz, _rows(kb_sc, kb, tb))
                    p1 = st[h][1]
                    out.append((dq, run1 + p1[:, tb - 1:tb], rung + (cx[:, tb - 1:tb] + gm[:, tb - 1:tb])))
                return tuple(out)

            z1 = jnp.zeros((tb, 1), F32)
            zero = (jnp.zeros((tb, PAIR), F32), z1, z1)
            c = lax.fori_loop(0, qb, lambda i, cc: block(i, cc, False), (zero, zero))
            c = block(qb, c, True)
            r0 = pl.multiple_of(qb * tb, tb)
            dq_ref[pl.ds(r0, tb), :] = (_pair_select(m0, c[0][0], c[1][0]) * SCALE).astype(BF16)
            return carry

        lax.fori_loop(0, nb, qloop, 0)
        for kb in range(nb):
            dk_ref[kb * tb:(kb + 1) * tb, :] = dkt_sc[kb].T.astype(BF16)
            dv_ref[kb * tb:(kb + 1) * tb, :] = dvt_sc[kb].T.astype(BF16)
        if comm is not None:
            @pl.when(step == 2 * nbat - 1)
            def _():
                comm["finish"](c_in, c_out, c_sems)

    def col(blk):
        return pl.BlockSpec((S, PAIR), lambda b, p: (b, blk + p))

    sh = jax.ShapeDtypeStruct((T, AW), BF16)
    anyspec = pl.BlockSpec(memory_space=pl.ANY)
    out = pl.pallas_call(
        body, name=f"sb_bwd_{l}",
        grid=(nbat, 2),
        in_specs=[col(SBQ_BLK), col(SBK_BLK), col(SBV_BLK), col(0), col(0)] + c_specs,
        out_specs=[col(0), col(0), col(0)] + [anyspec] * n_co,
        out_shape=[sh, sh, sh] + c_outs,
        scratch_shapes=[pltpu.VMEM((S, PAIR), BF16), pltpu.VMEM((nb, PAIR, tb), BF16), pltpu.VMEM((nb, PAIR, tb), BF16),
                        pltpu.VMEM((nb, PAIR, tb), F32), pltpu.VMEM((nb, PAIR, tb), F32)] + c_scr,
        compiler_params=_cp(("arbitrary", "arbitrary"), VMEM_BIG),
    )(proj, proj, proj, do, t1, *c_args)
    return out[0], out[1], out[2], list(out[3:])


def _foxq_fwd(proj, cum, ck, gqk2, l, S):
    T = proj.shape[0]
    tb = TQ_(S)
    nb = S // tb

    def body(q_ref, k_ref, v_ref, cum_ref, ck_ref, g_ref, o_ref, nl_ref, fk_sc, fkt_sc, vb_sc):
        lane, m0 = _lane_masks()
        p = pl.program_id(1)
        kn, _ = _pair_rms(k_ref[...], m0)
        fk_sc[...] = kn * g_ref[1:2, :]
        _transpose_blocks(fk_sc, fkt_sc, nb, tb)
        vb_sc[...] = v_ref[...].astype(BF16)
        causal = lax.broadcasted_iota(jnp.int32, (tb, tb), 1) <= lax.broadcasted_iota(jnp.int32, (tb, tb), 0)

        def qloop(qb, carry):
            qn, _ = _pair_rms(_rows(q_ref, qb, tb), m0)
            fqh = _pair_split(qn * (g_ref[0:1, :] * SCALE), m0)
            cumv = _rows(cum_ref, qb, tb)
            cq = [_rowsum(jnp.where(lane == 2 * p + h, cumv, 0.0)) for h in range(2)]

            def scores(kb):
                return tuple(_dot(fqh[h], fkt_sc[kb]) for h in range(2))

            def block(kb, kb_next, qk, c, masked):
                st = []
                for h in range(2):
                    s = qk[h] + (cq[h] - ck_ref[h, kb])
                    if masked:
                        s = jnp.where(causal, s, NEG)
                    m2 = jnp.maximum(c[h][0], jnp.max(s, axis=1, keepdims=True))
                    pr = jnp.exp(s - m2)
                    st.append((m2, pr, _dot(pr.astype(BF16), _rows(vb_sc, kb, tb))))
                qk_next = scores(kb_next)
                out = []
                for h in range(2):
                    m, lsum, acc = c[h]
                    m2, pr, pv = st[h]
                    al = jnp.exp(m - m2)
                    out.append((m2, al * lsum + _rowsum(pr), al * acc + pv))
                return qk_next, tuple(out)

            zero = (jnp.full((tb, 1), NEG, F32), jnp.zeros((tb, 1), F32), jnp.zeros((tb, PAIR), F32))

            def off_diag(i, sc):
                return block(i, i + 1, sc[0], sc[1], False)

            qk, c = lax.fori_loop(0, qb, off_diag, (scores(0), (zero, zero)))
            _, c = block(qb, qb, qk, c, True)
            r0 = pl.multiple_of(qb * tb, tb)
            o_ref[pl.ds(r0, tb), :] = _pair_select(m0, c[0][2] / c[0][1], c[1][2] / c[1][1])
            nl = [cq[h] - (c[h][0] + jnp.log(c[h][1])) for h in range(2)]
            nl_ref[pl.ds(r0, tb), :] = jnp.where(lane == 0, nl[0], jnp.where(lane == 1, nl[1], 0.0))
            return carry

        lax.fori_loop(0, nb, qloop, 0)

    def col(blk):
        return pl.BlockSpec((S, PAIR), lambda b, p: (b, blk + p))

    return pl.pallas_call(
        body, name=f"fox_fwd_{l}",
        grid=(T // S, 2),
        in_specs=[col(FXQ_BLK), col(FXK_BLK), col(FXV_BLK),
                  pl.BlockSpec((S, 128), lambda b, p: (b, 0)),
                  pl.BlockSpec((None, 2, nb, 1, tb), lambda b, p: (b, p, 0, 0, 0)),
                  pl.BlockSpec((None, 8, PAIR), lambda b, p: (l, 0, 0))],
        out_specs=[col(0), col(0)],
        out_shape=[jax.ShapeDtypeStruct((T, AW), F32), jax.ShapeDtypeStruct((T, AW), F32)],
        scratch_shapes=[pltpu.VMEM((S, PAIR), F32), pltpu.VMEM((nb, PAIR, tb), BF16), pltpu.VMEM((S, PAIR), BF16)],
        compiler_params=_cp(("arbitrary", "arbitrary"), VMEM_BIG),
    )(proj, proj, proj, cum, ck, gqk2)


def _foxq_bwd(proj, do, nl, ck, gqk2, l, S):
    T = proj.shape[0]
    tb = TQ_(S)
    nb = S // tb

    def body(q_ref, k_ref, v_ref, do_ref, nl_ref, ck_ref, g_ref,
             dq_ref, dk_ref, dv_ref, dc_ref, wacc_ref, fk_sc, fkt_sc, vt_sc, dfkt_sc, dvt_sc):
        @pl.when((pl.program_id(0) == 0) & (pl.program_id(1) == 0))
        def _():
            wacc_ref[...] = jnp.zeros_like(wacc_ref)

        _, m0 = _lane_masks()
        mt0 = lax.broadcasted_iota(jnp.int32, (PAIR, 1), 0) < HD
        g0 = g_ref[0:1, :]
        g1 = g_ref[1:2, :]
        fk_sc[...] = (_pair_rms(k_ref[...], m0)[0] * g1).astype(BF16)
        _transpose_blocks(fk_sc, fkt_sc, nb, tb)
        _transpose_blocks(v_ref, vt_sc, nb, tb)
        dfkt_sc[...] = jnp.zeros_like(dfkt_sc)
        dvt_sc[...] = jnp.zeros_like(dvt_sc)
        dc_ref[...] = jnp.zeros_like(dc_ref)
        causal = lax.broadcasted_iota(jnp.int32, (tb, tb), 1) <= lax.broadcasted_iota(jnp.int32, (tb, tb), 0)

        def qloop(qb, carry):
            qn, qr = _pair_rms(_rows(q_ref, qb, tb), m0)
            fqf = qn * (g0 * SCALE)
            dof = _rows(do_ref, qb, tb)
            fqh = _pair_split(fqf, m0)
            doh = _pair_split(dof, m0)
            fqth = _pair_split(fqf.T, mt0)
            doth = _pair_split(dof.T, mt0)
            nlv = _rows(nl_ref, qb, tb)
            cql = (nlv[:, 0:1], nlv[:, 1:2])

            def probs(kb, masked):
                qk = [_dot(fqh[h], fkt_sc[kb]) for h in range(2)]
                dp = [_dot(doh[h], vt_sc[kb]) for h in range(2)]
                pr = []
                for h in range(2):
                    e = jnp.exp(qk[h] + (cql[h] - ck_ref[h, kb]))
                    pr.append(jnp.where(causal, e, 0.0) if masked else e)
                return pr, dp

            def dblock(kb, c, masked):
                pr, dp = probs(kb, masked)
                return tuple(c[h] + _rowsum(pr[h] * dp[h]) for h in range(2))

            z1 = jnp.zeros((tb, 1), F32)
            dlt = lax.fori_loop(0, qb, lambda i, cc: dblock(i, cc, False), (z1, z1))
            dlt = dblock(qb, dlt, True)

            def block(kb, c, masked):
                pr, dp = probs(kb, masked)
                out = []
                for h in range(2):
                    ds = pr[h] * (dp[h] - dlt[h])
                    dsb = ds.astype(BF16)
                    dvt_sc[kb] += _dot(doth[h], pr[h].astype(BF16))
                    dfkt_sc[kb] += _dot(fqth[h], dsb)
                    dc_ref[h, kb] += jnp.broadcast_to(-_colsum(ds), (8, tb))
                    out.append(c[h] + _dot(dsb, _rows(fk_sc, kb, tb)))
                return tuple(out)

            zq = jnp.zeros((tb, PAIR), F32)
            c = lax.fori_loop(0, qb, lambda i, cc: block(i, cc, False), (zq, zq))
            c = block(qb, c, True)
            dfq = _pair_select(m0, c[0], c[1]) * SCALE
            wacc_ref[0:1, :] += _colsum(dfq * qn)
            r0 = pl.multiple_of(qb * tb, tb)
            dq_ref[pl.ds(r0, tb), :] = _pair_rms_bwd(qn, qr, dfq * g0, m0).astype(BF16)
            return carry

        lax.fori_loop(0, nb, qloop, 0)
        for kb in range(nb):
            rows = slice(kb * tb, (kb + 1) * tb)
            dfk = dfkt_sc[kb].T
            knb, krb = _pair_rms(k_ref[rows, :], m0)
            wacc_ref[1:2, :] += _colsum(dfk * knb)
            dk_ref[rows, :] = _pair_rms_bwd(knb, krb, dfk * g1, m0).astype(BF16)
            dv_ref[rows, :] = dvt_sc[kb].T.astype(BF16)

    def col(blk):
        return pl.BlockSpec((S, PAIR), lambda b, p: (b, blk + p))

    sh = jax.ShapeDtypeStruct((T, AW), BF16)
    return pl.pallas_call(
        body, name=f"fox_bwd_{l}",
        grid=(T // S, 2),
        in_specs=[col(FXQ_BLK), col(FXK_BLK), col(FXV_BLK), col(0), col(0),
                  pl.BlockSpec((None, 2, nb, 1, tb), lambda b, p: (b, p, 0, 0, 0)),
                  pl.BlockSpec((None, 8, PAIR), lambda b, p: (l, 0, 0))],
        out_specs=[col(0), col(0), col(0),
                   pl.BlockSpec((None, 2, nb, 8, tb), lambda b, p: (b, p, 0, 0, 0)),
                   pl.BlockSpec((8, PAIR), lambda b, p: (0, 0))],
        out_shape=[sh, sh, sh,
                   jax.ShapeDtypeStruct((T // S, NH, nb, 8, tb), F32),
                   jax.ShapeDtypeStruct((8, PAIR), F32)],
        scratch_shapes=[pltpu.VMEM((S, PAIR), BF16), pltpu.VMEM((nb, PAIR, tb), BF16), pltpu.VMEM((nb, PAIR, tb), BF16),
                        pltpu.VMEM((nb, PAIR, tb), F32), pltpu.VMEM((nb, PAIR, tb), F32)],
        compiler_params=_cp(("arbitrary", "arbitrary"), VMEM_BIG),
    )(proj, proj, proj, do, nl, ck, gqk2)


def _ada_fwd(c_all, w_ada, b_cols):
    nb, ncol = c_all.shape[0], w_ada.shape[2]
    tn = _tile(ncol, 768)

    def body(c_ref, w_ref, b_ref, o_ref):
        c = c_ref[...]
        ca = (c * _sigmoid(c)).astype(BF16)
        o_ref[...] = _dot(ca, w_ref[...].astype(BF16)) + b_ref[...]

    return pl.pallas_call(
        body, name="ada_fwd",
        grid=(2, ncol // tn),
        in_specs=[pl.BlockSpec((nb, D), lambda l, n: (0, 0)),
                  pl.BlockSpec((None, D, tn), lambda l, n: (l, 0, n)),
                  pl.BlockSpec((None, 1, tn), lambda l, n: (l, 0, n))],
        out_specs=pl.BlockSpec((None, nb, tn), lambda l, n: (l, 0, n)),
        out_shape=jax.ShapeDtypeStruct((2, nb, ncol), F32),
        compiler_params=_cp(("arbitrary", "arbitrary")),
    )(c_all, w_ada, b_cols)


def _ada_bwd(c_all, dmod_cols):
    nb, ncol = c_all.shape[0], dmod_cols.shape[2]
    tn = _tile(ncol, 768)

    def body(c_ref, d_ref, o_ref):
        c = c_ref[...]
        ca = (c * _sigmoid(c)).astype(BF16)
        o_ref[...] = _dot_tn(ca, d_ref[...].astype(BF16))

    return pl.pallas_call(
        body, name="ada_bwd",
        grid=(2, ncol // tn),
        in_specs=[pl.BlockSpec((nb, D), lambda l, n: (0, 0)),
                  pl.BlockSpec((None, nb, tn), lambda l, n: (l, 0, n))],
        out_specs=pl.BlockSpec((None, D, tn), lambda l, n: (l, 0, n)),
        out_shape=jax.ShapeDtypeStruct((2, D, ncol), F32),
        compiler_params=_cp(("arbitrary", "arbitrary")),
    )(c_all, dmod_cols)


def _sum_lead(a, name):
    n, R, C = a.shape
    tr = _tile_div8(R, 256)

    def body(a_ref, o_ref):
        acc = a_ref[0]
        for i in range(1, n):
            acc = acc + a_ref[i]
        o_ref[...] = acc

    return pl.pallas_call(
        body, name=name,
        grid=(R // tr,),
        in_specs=[pl.BlockSpec((n, tr, C), lambda i: (0, i, 0))],
        out_specs=pl.BlockSpec((tr, C), lambda i: (i, 0)),
        out_shape=jax.ShapeDtypeStruct((R, C), F32),
        compiler_params=_cp(("arbitrary",)),
    )(a)


def _adamw(w, g, m, v, name):
    R, C = w.shape
    tr = _tile_div8(R, max(8, (1 << 18) // C))
    c1 = 1.0 / (1.0 - ADAM_B1 ** ADAM_STEP)
    c2 = 1.0 / (1.0 - ADAM_B2 ** ADAM_STEP)

    def body(w_ref, g_ref, m_ref, v_ref, d_ref, mo_ref, vo_ref):
        gg = g_ref[...]
        mn = ADAM_B1 * m_ref[...] + (1.0 - ADAM_B1) * gg
        vn = ADAM_B2 * v_ref[...] + (1.0 - ADAM_B2) * (gg * gg)
        mo_ref[...] = mn
        vo_ref[...] = vn
        d_ref[...] = (-ADAM_LR) * ((mn * c1) / (jnp.sqrt(vn * c2) + ADAM_EPS) + ADAM_WD * w_ref[...])

    spec = pl.BlockSpec((tr, C), lambda i: (i, 0))
    sh = jax.ShapeDtypeStruct((R, C), F32)
    return pl.pallas_call(
        body, name=name, grid=(R // tr,),
        in_specs=[spec] * 4, out_specs=[spec] * 3, out_shape=[sh] * 3,
        compiler_params=_cp(("arbitrary",)),
    )(w, g, m, v)


def _coords():
    return lax.axis_index("x"), lax.axis_index("y"), lax.axis_index("c")


def _all_gather8(blk, name, vmem):
    m_per, n = blk.shape
    space = pltpu.VMEM if vmem else pl.ANY

    def body(x_ref, out_ref, send_sems, recv_sems, local_sem):
        x, y, c = _coords()
        me, sibling = (x, y, c), (x, y, 1 - c)
        chips = [(1 - x, y), (x, 1 - y), (1 - x, 1 - y)]

        def rows(px, py, pc):
            return out_ref.at[4 * px + 2 * py + pc]

        def copy(k, block, to, src=None):
            return pltpu.make_async_remote_copy(
                src_ref=rows(*block) if src is None else src, dst_ref=rows(*block),
                send_sem=send_sems.at[k], recv_sem=recv_sems.at[k], device_id=to, device_id_type=MESH)

        mine = pltpu.make_async_copy(x_ref, rows(*me), local_sem)
        mine.start()
        first = [copy(0, me, sibling, src=x_ref)]
        first += [copy(1 + j, me, (*chip, c), src=x_ref) for j, chip in enumerate(chips)]
        for cp in first:
            cp.start()
        passed = [copy(4 + j, (*chip, c), sibling) for j, chip in enumerate(chips)]
        for j, chip in enumerate(chips):
            copy(1 + j, (*chip, c), me).wait_recv()
            passed[j].start()
        copy(0, sibling, me).wait_recv()
        for j, chip in enumerate(chips):
            copy(4 + j, (*chip, 1 - c), me).wait_recv()
        for cp in first + passed:
            cp.wait_send()
        mine.wait()

    return pl.pallas_call(
        body, name=name,
        out_shape=jax.ShapeDtypeStruct((N_DEV, m_per, n), blk.dtype),
        in_specs=[pl.BlockSpec(memory_space=space)],
        out_specs=pl.BlockSpec(memory_space=space),
        scratch_shapes=[pltpu.SemaphoreType.DMA((7,)), pltpu.SemaphoreType.DMA((7,)), pltpu.SemaphoreType.DMA],
        compiler_params=pltpu.CompilerParams(vmem_limit_bytes=VMEM_BIG if vmem else None),
    )(blk)


def _run_comm(comm, name):
    n_in, n_out = len(comm["args"]), len(comm["out_shapes"])

    def body(*refs):
        parts = (refs[:n_in], refs[n_in:n_in + n_out], refs[n_in + n_out:])
        comm["start"](*parts)
        comm["finish"](*parts)

    anyspec = pl.BlockSpec(memory_space=pl.ANY)
    return pl.pallas_call(
        body, name=name, out_shape=comm["out_shapes"],
        in_specs=[anyspec] * n_in, out_specs=[anyspec] * n_out, scratch_shapes=comm["scratch"],
    )(*comm["args"])


def _hosted(comm):
    if comm is None:
        return [], [], [], []
    anyspec = pl.BlockSpec(memory_space=pl.ANY)
    return list(comm["args"]), [anyspec] * len(comm["args"]), list(comm["out_shapes"]), list(comm["scratch"])


def _ag_layer_comm(up_b, dn_b, in_b, out_b, l):
    dn_rows = dn_b.shape[2]
    in_half, out_half = in_b.shape[1] // 2, out_b.shape[1] // 2
    n_piece = 4

    def plan(ins, outs, sems):
        up_ref, dn_ref, in_ref, out_ref = ins
        gup_ref, gdn_ref, gin_ref, gout_ref = outs
        send_sems, recv_sems, local_sems = sems
        x, y, c = _coords()
        me, sibling = (x, y, c), (x, y, 1 - c)
        chips = [(1 - x, y), (x, 1 - y), (1 - x, 1 - y)]

        def dsts(px, py, pc):
            s = 2 * px + py
            return [gup_ref.at[s, pc], gdn_ref.at[pc, pl.ds(s * dn_rows, dn_rows)],
                    gin_ref.at[s, pl.ds(pc * in_half, in_half)],
                    gout_ref.at[pl.ds(s * 2 * out_half + pc * out_half, out_half)]]

        srcs = [up_ref.at[l, c], dn_ref.at[l, c], in_ref.at[l, pl.ds(c * in_half, in_half)],
                out_ref.at[l, pl.ds(c * out_half, out_half)]]

        def copies(k, block, to, own=False):
            d = dsts(*block)
            return [pltpu.make_async_remote_copy(
                src_ref=srcs[p] if own else d[p], dst_ref=d[p], send_sem=send_sems.at[k, p],
                recv_sem=recv_sems.at[k, p], device_id=to, device_id_type=MESH) for p in range(n_piece)]

        mine = [pltpu.make_async_copy(srcs[p], d, local_sems.at[p]) for p, d in enumerate(dsts(*me))]
        first = copies(0, me, sibling, own=True)
        for j, chip in enumerate(chips):
            first += copies(1 + j, me, (*chip, c), own=True)
        return me, sibling, chips, c, copies, mine, first

    def start(ins, outs, sems):
        *_, mine, first = plan(ins, outs, sems)
        for cp in mine + first:
            cp.start()

    def finish(ins, outs, sems):
        me, sibling, chips, c, copies, mine, first = plan(ins, outs, sems)
        passed = []
        for j, chip in enumerate(chips):
            for cp in copies(1 + j, (*chip, c), me):
                cp.wait_recv()
            fwd = copies(4 + j, (*chip, c), sibling)
            for cp in fwd:
                cp.start()
            passed += fwd
        for cp in copies(0, sibling, me):
            cp.wait_recv()
        for j, chip in enumerate(chips):
            for cp in copies(4 + j, (*chip, 1 - c), me):
                cp.wait_recv()
        for cp in first + passed:
            cp.wait_send()
        for cp in mine:
            cp.wait()

    return dict(
        args=[up_b, dn_b, in_b, out_b],
        out_shapes=[jax.ShapeDtypeStruct((N_SHARD,) + up_b.shape[1:], BF16),
                    jax.ShapeDtypeStruct((dn_b.shape[1], N_SHARD * dn_rows, D), BF16),
                    jax.ShapeDtypeStruct((N_SHARD,) + in_b.shape[1:], BF16),
                    jax.ShapeDtypeStruct((N_SHARD * out_b.shape[1], D), BF16)],
        scratch=[pltpu.SemaphoreType.DMA((7, n_piece)), pltpu.SemaphoreType.DMA((7, n_piece)),
                 pltpu.SemaphoreType.DMA((n_piece,))],
        start=start, finish=finish)


def _rs_to_chips_comm(hs):
    n = len(hs)

    def copies(h, r, sems):
        send_sems, recv_sems = sems
        x, y, c = _coords()
        chips = [(1 - x, y), (x, 1 - y), (1 - x, 1 - y)]
        return [pltpu.make_async_remote_copy(
            src_ref=h[p].at[2 * px + py], dst_ref=r[p].at[k], send_sem=send_sems.at[k, p], recv_sem=recv_sems.at[k, p],
            device_id=(px, py, c), device_id_type=MESH) for k, (px, py) in enumerate(chips) for p in range(n)]

    def start(h, r, sems):
        for cp in copies(h, r, sems):
            cp.start()

    def finish(h, r, sems):
        for cp in copies(h, r, sems):
            cp.wait()

    return dict(args=list(hs), out_shapes=[jax.ShapeDtypeStruct((3,) + h.shape[1:], h.dtype) for h in hs],
                scratch=[pltpu.SemaphoreType.DMA((3, n)), pltpu.SemaphoreType.DMA((3, n))],
                start=start, finish=finish)


def _rs_to_sibling(pieces, name):
    n = len(pieces)

    def body(*refs):
        g, r, (send_sems, recv_sems) = refs[:n], refs[n:2 * n], refs[2 * n:]
        x, y, c = _coords()
        cps = []
        for p in range(n):
            r2 = g[p].shape[1] // 2
            cps.append(pltpu.make_async_remote_copy(
                src_ref=g[p].at[:, pl.ds((1 - c) * r2, r2)], dst_ref=r[p], send_sem=send_sems.at[p],
                recv_sem=recv_sems.at[p], device_id=(x, y, 1 - c), device_id_type=MESH))
        for cp in cps:
            cp.start()
        for cp in cps:
            cp.wait()

    anyspec = pl.BlockSpec(memory_space=pl.ANY)
    return pl.pallas_call(
        body, name=name,
        out_shape=[jax.ShapeDtypeStruct((N_SHARD, g.shape[1] // 2, g.shape[2]), g.dtype) for g in pieces],
        in_specs=[anyspec] * n, out_specs=[anyspec] * n,
        scratch_shapes=[pltpu.SemaphoreType.DMA((n,)), pltpu.SemaphoreType.DMA((n,))],
    )(*pieces)


def _share_halves(tensors, places, r2s):
    n, no = len(places), len(tensors)

    def body(*refs):
        o, (send_sems, recv_sems) = refs[no:2 * no], refs[2 * no:]
        x, y, c = _coords()

        def half(p, hc):
            oi, lead = places[p]
            return o[oi].at[(*lead, pl.ds(hc * r2s[p], r2s[p]))]

        outs = [pltpu.make_async_remote_copy(
            src_ref=half(p, c), dst_ref=half(p, c), send_sem=send_sems.at[p], recv_sem=recv_sems.at[p],
            device_id=(x, y, 1 - c), device_id_type=MESH) for p in range(n)]
        for cp in outs:
            cp.start()
        for p in range(n):
            pltpu.make_async_remote_copy(
                src_ref=half(p, 1 - c), dst_ref=half(p, 1 - c), send_sem=send_sems.at[p], recv_sem=recv_sems.at[p],
                device_id=(x, y, 1 - c), device_id_type=MESH).wait_recv()
        for cp in outs:
            cp.wait_send()

    anyspec = pl.BlockSpec(memory_space=pl.ANY)
    return pl.pallas_call(
        body, name="share_halves",
        out_shape=[jax.ShapeDtypeStruct(t.shape, t.dtype) for t in tensors],
        in_specs=[anyspec] * no, out_specs=[anyspec] * no,
        input_output_aliases={i: i for i in range(no)},
        scratch_shapes=[pltpu.SemaphoreType.DMA((n,)), pltpu.SemaphoreType.DMA((n,))],
    )(*tensors)


def _add_rows(r2, cols, n_arrays):
    lanes = -(-cols // 128) * 128
    return _tile_div8(r2, max(16, (24 << 20) // (2 * n_arrays * lanes * 4)), mult=16)


def _add_sibling(pieces, recvs, cidx, name):
    n = len(pieces)
    _, R, C = pieces[0].shape
    r2 = R // 2
    tr = _add_rows(r2, C, 2 * n)
    nt = r2 // tr

    def body(c_ref, *refs):
        for p in range(n):
            refs[2 * n + p][...] = (refs[p][...] + refs[n + p][...].astype(F32)).astype(BF16)

    return pl.pallas_call(
        body, name=name,
        grid_spec=pltpu.PrefetchScalarGridSpec(
            num_scalar_prefetch=1, grid=(N_SHARD, nt),
            in_specs=[pl.BlockSpec((None, tr, C), lambda s, i, c_ref: (s, c_ref[0] * nt + i, 0))] * n
            + [pl.BlockSpec((None, tr, C), lambda s, i, c_ref: (s, i, 0))] * n,
            out_specs=[pl.BlockSpec((None, tr, C), lambda s, i, c_ref: (s, i, 0))] * n),
        out_shape=[jax.ShapeDtypeStruct((N_SHARD, r2, C), BF16)] * n,
        compiler_params=_cp(("arbitrary", "arbitrary"), VMEM_BIG),
    )(cidx, *pieces, *recvs)


def _add_chips_into(piece, recv_a, recv_b, sc, prev, shape, lead, name):
    _, R, C = piece.shape
    r2 = R // 2
    tr = _add_rows(r2, C, 4)
    nt = r2 // tr
    nl = len(lead)

    def body(sc_ref, p_ref, a_ref, b_ref, *rest):
        o_ref = rest[-1]
        acc = p_ref[...] + a_ref[...].astype(F32)
        for k in range(3):
            acc = acc + b_ref[k].astype(F32)
        o_ref[...] = acc

    in_specs = [pl.BlockSpec((None, tr, C), lambda i, sc_ref: (sc_ref[0], sc_ref[1] * nt + i, 0)),
                pl.BlockSpec((None, tr, C), lambda i, sc_ref: (sc_ref[0], i, 0)),
                pl.BlockSpec((3, tr, C), lambda i, sc_ref: (0, i, 0))]
    args = [sc, piece, recv_a, recv_b]
    aliases = {}
    if prev is not None:
        in_specs.append(pl.BlockSpec(memory_space=pl.ANY))
        args.append(prev)
        aliases = {4: 0}
    return pl.pallas_call(
        body, name=name,
        grid_spec=pltpu.PrefetchScalarGridSpec(
            num_scalar_prefetch=1, grid=(nt,), in_specs=in_specs,
            out_specs=pl.BlockSpec((None,) * nl + (tr, C), lambda i, sc_ref: (*lead, sc_ref[1] * nt + i, 0))),
        out_shape=jax.ShapeDtypeStruct(shape, F32),
        input_output_aliases=aliases,
        compiler_params=_cp(("arbitrary",), VMEM_BIG),
    )(*args)


def _pack_rows(parts, rows, dtype):
    flat = jnp.concatenate([p.reshape(-1).astype(dtype) for p in parts])
    return jnp.pad(flat, (0, rows * ROW - flat.shape[0])).reshape(rows, ROW)


def _unpack(flat, shapes):
    out, off = [], 0
    for sh in shapes:
        n = math.prod(sh)
        out.append(flat[off:off + n].reshape(sh))
        off += n
    return out


def _heads(t, B, S, blk):
    return t.reshape(B, S, NH, HD).transpose(0, 2, 1, 3).reshape(B, NH, S // blk, blk, HD)


def _unheads(t, B, S):
    return t.reshape(B, NH, S, HD).transpose(0, 2, 1, 3).reshape(B * S, AW)


def _block_diag(w):
    eye = jnp.eye(LW // HD, dtype=w.dtype)
    return jnp.einsum("lhij,hg->lhigj", w, eye).reshape(w.shape[0], LW, LW)


def _diag_blocks(w):
    nbk = LW // HD
    w4 = w.reshape(nbk, HD, nbk, HD)
    return jnp.stack([w4[h, :, h, :] for h in range(nbk)])


def _rows8(rows, width):
    z = jnp.zeros((width,), F32)
    return jnp.stack(list(rows) + [z] * (8 - len(rows)))


def kernel(x, c, w_ada, b_ada, g_norm, w_ffn_up, w_ffn_down, w_in, b_fgate, conv_w, conv_b, w_rgate, b_rgate, w_igate, b_igate, lru_lambda, g_qk, g_mix_out, w_out, loss_target, m_w_ada, m_b_ada, m_g_norm, m_w_ffn_up, m_w_ffn_down, m_w_in, m_b_fgate, m_conv_w, m_conv_b, m_w_rgate, m_b_rgate, m_w_igate, m_b_igate, m_lru_lambda, m_g_qk, m_g_mix_out, m_w_out, v_w_ada, v_b_ada, v_g_norm, v_w_ffn_up, v_w_ffn_down, v_w_in, v_b_fgate, v_conv_w, v_conv_b, v_w_rgate, v_b_rgate, v_w_igate, v_b_igate, v_lru_lambda, v_g_qk, v_g_mix_out, v_w_out):
    B, S, _ = x.shape
    T = B * S
    xi, yi, ci = _coords()
    sidx = 2 * xi + yi
    didx = 4 * xi + 2 * yi + ci
    ada_cols = w_ada.shape[2]
    gn_cols = g_norm.shape[2]
    cw_cols = conv_w.shape[2]
    n_all = B * N_DEV

    blk1 = _pack_rows([c, jnp.pad(g_norm.reshape(-1), (0, 2 * ROW - g_norm.size)), conv_w], 8, F32)
    ag1 = _all_gather8(blk1, "ag_small_in", True)
    c_all = ag1[:, 0:B].reshape(n_all, D)
    chip_rows = ag1[0::2]
    g_norm_full = chip_rows[:, 2:4].reshape(N_SHARD, 2 * ROW)[:, :g_norm.size] \
        .reshape(N_SHARD, 2, 3, gn_cols).transpose(1, 2, 0, 3).reshape(2, 3, D)
    conv_w_full = chip_rows[:, 4].reshape(N_SHARD, 2, 4, cw_cols).transpose(1, 2, 0, 3).reshape(2, 4, LW)

    b_cols = lax.dynamic_slice(b_ada, (0, sidx * ada_cols), (2, ada_cols)).reshape(2, 1, ada_cols)
    mod_cols = _ada_fwd(c_all, w_ada, b_cols)
    mrows = (2 * n_all * ada_cols) // ROW
    ag2 = _all_gather8(mod_cols.reshape(mrows, ROW), "ag_mod", True)
    mod_sh = ag2[0::2].reshape(N_SHARD, 2, n_all, ada_cols)
    mod_me = lax.dynamic_slice(mod_sh, (0, 0, didx * B, 0), (N_SHARD, 2, B, ada_cols))
    mod_me = mod_me.transpose(1, 2, 0, 3).reshape(2, B, 3, 3, D)
    zrow = jnp.zeros((B, D), F32)
    mods = [[jnp.stack([mod_me[l, :, j, 0], 1.0 + mod_me[l, :, j, 1], 1.0 + mod_me[l, :, j, 2],
                        jnp.broadcast_to(g_norm_full[l, j], (B, D)), zrow, zrow, zrow, zrow], axis=1)
             for j in range(3)] for l in range(2)]

    wshards = (w_ffn_up.astype(BF16), w_ffn_down.astype(BF16), w_in.astype(BF16), w_out.astype(BF16))

    def layer_weights(gathered):
        g_up, g_dn, g_in, g_out = gathered
        winp = jnp.pad(g_in.transpose(1, 0, 2).reshape(D, N_IN), ((0, 0), (0, N_INP - N_IN)))
        return dict(up=g_up, dn=g_dn, inp=winp, out=g_out)

    wl = [layer_weights(_run_comm(_ag_layer_comm(*wshards, 0), "ag_weights_0")), None]

    wr_d = _block_diag(w_rgate).astype(BF16)
    wi_d = _block_diag(w_igate).astype(BF16)
    cw8 = jnp.pad(conv_w_full, ((0, 0), (0, 4), (0, 0)))
    vp8 = jnp.stack([_rows8([conv_b[l], b_rgate[l], b_igate[l], lru_lambda[l]], LW) for l in range(2)])
    bfp = jnp.pad(b_fgate, ((0, 0), (0, 128 - NH)))[:, None, :] * jnp.ones((1, 8, 1), F32)
    gqk2 = jnp.tile(jnp.pad(g_qk, ((0, 0), (0, 6), (0, 0))), (1, 1, 2))
    gmix8 = jnp.pad(g_mix_out[:, None, :], ((0, 0), (0, 7), (0, 0)))

    x2 = x.reshape(T, D)
    tgt = loss_target.reshape(T, D)

    saved = []
    xc = x2
    for l in range(2):
        sv = {}
        sv["x0"] = xc
        w = wl[l]
        xc, sv["g0"], sv["u0"], sv["f0"] = _ffn_fwd(xc, mods[l][0], w["up"], w["dn"], l, 0, S)
        sv["x1"] = xc
        sv["h1"], proj = _mix_in_fwd(xc, mods[l][1], w["inp"], l, S)
        sv["proj"] = proj
        sv["ylru"], sv["hl"] = _lru_fwd(proj, cw8, vp8, wr_d, wi_d, l, S)
        sv["osb"], sv["t1"], gathered = _sbq_fwd(proj, l, S, _ag_layer_comm(*wshards, 1) if l == 0 else None)
        if l == 0:
            wl[1] = layer_weights(gathered)
        cum = _fgate_fwd(proj, bfp, l, S)
        sv["ck"] = cum[:, :NH].reshape(B, S, NH).transpose(0, 2, 1).reshape(B, NH, S // TK_(S), 1, TK_(S))
        sv["ofx"], sv["nl"] = _foxq_fwd(proj, cum, sv["ck"], gqk2, l, S)
        xc, sv["y"], sv["mo"] = _mix_out_fwd(xc, sv["ylru"], sv["osb"], sv["ofx"], mods[l][1], gmix8, w["out"], l, S)
        sv["x2"] = xc
        xc, sv["g2"], sv["u2"], sv["f2"] = _ffn_fwd(xc, mods[l][2], w["up"], w["dn"], l, 1, S)
        saved.append(sv)

    dxc, lpart = _loss_head(xc, tgt, S)
    loss = lax.psum(lpart[0, 0], ("x", "y", "c"))

    tf = wl[0]["up"].shape[-1]
    g_up_l = [[None, None], [None, None]]
    g_dn_l = [[None, None], [None, None]]
    g_in_l, g_out_l = [None, None], [None, None]
    dmods = [[None] * 3 for _ in range(2)]
    small = [dict() for _ in range(2)]
    cvec = jnp.reshape(ci, (1,)).astype(jnp.int32)
    scvec = jnp.stack([sidx, ci]).astype(jnp.int32)

    def layer_groups(l):
        return [("up", [g_up_l[l][j] for j in range(2)], [(l, 0), (l, 1)]),
                ("dn", [g_dn_l[l][j] for j in range(2)], [(l, 0), (l, 1)]),
                ("in", [g_in_l[l]], [(l,)]),
                ("out", [g_out_l[l]], [(l,)])]

    def rs_sibling_phase(l):
        groups = layer_groups(l)
        recv_a = _rs_to_sibling([pb for _, ps, _ in groups for _, pb in ps], f"rs_to_sibling_{l}")
        hs, off = [], 0
        for gname, ps, _ in groups:
            hs += _add_sibling([pf for pf, _ in ps], recv_a[off:off + len(ps)], cvec, f"rs_add_sibling_{gname}_{l}")
            off += len(ps)
        return recv_a, hs

    def ffn_back(l, j, xin, dy, sv, sub):
        dx, dmod, wacc, hb, dfb, ab, dgub = _ffn_bwd(
            xin, dy, mods[l][sub], sv[f"f{sub}"], sv[f"g{sub}"], sv[f"u{sub}"], wl[l]["up"], wl[l]["dn"], l, j, S)
        g_up_l[l][j] = _mm_tn(hb, dgub, f"dw_up_{l}_{j}", tnb=tf, split_n=True, with_bf16=True)
        g_dn_l[l][j] = tuple(g.reshape(N_SHARD, -1, D)
                             for g in _mm_tn(ab, dfb, f"dw_dn_{l}_{j}", tma=tf, with_bf16=True))
        dmods[l][sub] = dmod
        small[l][f"gn{sub}"] = wacc[0]
        return dx

    for l in (1, 0):
        sv = saved[l]
        dxc = ffn_back(l, 1, sv["x2"], dxc, sv, 2)
        dyl, dsb, dfx, dmo, dmod1, wacc_mo = _mix_out_bwd(
            dxc, sv["ylru"], sv["osb"], sv["ofx"], sv["mo"], mods[l][1], gmix8, wl[l]["out"], l, S)
        small[l]["gmix"] = wacc_mo[0]
        g_out_l[l] = tuple(g.reshape(N_SHARD, -1, D) for g in _mm_tn(sv["y"], dmo, f"dw_out_{l}", with_bf16=True))
        dsq, dsk, dsv, recv_b1 = _sbq_bwd(sv["proj"], dsb, sv["t1"], l, S,
                                           _rs_to_chips_comm(rs1[1]) if l == 0 else None)
        dfq, dfk, dfv, dck, wacc_fx = _foxq_bwd(sv["proj"], dfx, sv["nl"], sv["ck"], gqk2, l, S)
        small[l]["gqk"] = wacc_fx[0:2, :HD] + wacc_fx[0:2, HD:]
        dcum = dck[:, :, :, 0, :].reshape(B, NH, S).transpose(0, 2, 1).reshape(T, NH)
        dff_, wacc_fg = _fgate_bwd(jnp.pad(dcum, ((0, 0), (0, 128 - NH))), sv["proj"], bfp, l, S)
        small[l]["bf"] = wacc_fg[0, :NH]
        dlx, dlg, dpr, dpi, ub, wacc_lru = _lru_bwd(dyl, sv["proj"], sv["hl"], cw8, vp8, wr_d, wi_d, l, S)
        small[l]["lru"] = wacc_lru
        small[l]["wr"] = _diag_blocks(_mm_tn(ub, dpr, f"dw_rgate_{l}"))
        small[l]["wi"] = _diag_blocks(_mm_tn(ub, dpi, f"dw_igate_{l}"))
        dproj = jnp.concatenate(
            [dlx, dlg, dsq, dsk, dsv, dfq, dfk, dfv, dff_], axis=1)
        g_in = _mm_tn(sv["h1"], dproj, f"dw_in_{l}", tnb=N_INP // 3)[:, :N_IN]
        g_in = g_in.reshape(D, N_SHARD, -1).transpose(1, 0, 2)
        g_in_l[l] = (g_in, g_in.astype(BF16))
        dxc, dmod_in, wacc_in = _mix_in_bwd(sv["x1"], dxc, mods[l][1], dproj, wl[l]["inp"], l, S)
        dmods[l][1] = dmod_in + dmod1
        small[l]["gn1"] = wacc_in[0]
        dxc = ffn_back(l, 0, sv["x0"], dxc, sv, 0)
        if l == 1:
            rs1 = rs_sibling_phase(1)
    grad_x = dxc.reshape(B, S, D)

    dmod_loc = jnp.stack([jnp.stack([dmods[l][j][:, 0:3, :] for j in range(3)], axis=1) for l in range(2)])
    drows = 2 * B * 9
    blk3 = _pack_rows([dmod_loc], -(-drows // 8) * 8, F32)
    ag3 = _all_gather8(blk3, "ag_dmod", True)
    dmod_all = ag3[:, :drows].reshape(N_DEV, 2, B, 9 * D).transpose(1, 0, 2, 3).reshape(2, n_all, 9 * D)
    dmod_mine = lax.dynamic_slice(dmod_all, (0, 0, sidx * ada_cols), (2, n_all, ada_cols))
    grad_w_ada = _ada_bwd(c_all, dmod_mine)
    dmod_rows = jnp.pad(dmod_all.transpose(1, 0, 2).reshape(n_all, 2 * 9, D), ((0, 0), (0, 6), (0, 0)))
    grad_b_ada = _sum_lead(dmod_rows, "grad_b_ada")[:2 * 9].reshape(2, 9 * D)

    sm_parts = [
        jnp.stack([small[l]["bf"] for l in range(2)]),
        jnp.stack([small[l]["lru"][4] for l in range(2)]),
        jnp.stack([small[l]["wr"] for l in range(2)]),
        jnp.stack([small[l]["lru"][5] for l in range(2)]),
        jnp.stack([small[l]["wi"] for l in range(2)]),
        jnp.stack([small[l]["lru"][6] for l in range(2)]),
        jnp.stack([small[l]["lru"][7] for l in range(2)]),
        jnp.stack([small[l]["gqk"] for l in range(2)]),
        jnp.stack([small[l]["gmix"] for l in range(2)]),
        jnp.stack([jnp.stack([small[l][f"gn{j}"] for j in range(3)]) for l in range(2)]),
        jnp.stack([small[l]["lru"][0:4] for l in range(2)]),
    ]
    sm_shapes = [p.shape for p in sm_parts]
    sm_rows = -(-sum(p.size for p in sm_parts) // (8 * ROW)) * 8
    ag4 = _all_gather8(_pack_rows(sm_parts, sm_rows, F32), "ag_small_grads", True)
    sm_sum = _sum_lead(ag4, "sum_small_grads").reshape(-1)
    (g_bf, g_cb, g_wr, g_br, g_wi, g_bi, g_lam, g_gqk, g_gmix, g_gn_full, g_cw_full) = _unpack(sm_sum, sm_shapes)
    g_gn = lax.dynamic_slice(g_gn_full, (0, 0, sidx * gn_cols), (2, 3, gn_cols))
    g_cw = lax.dynamic_slice(g_cw_full, (0, 0, sidx * cw_cols), (2, 4, cw_cols))

    rs0 = rs_sibling_phase(0)
    recv_b0 = _run_comm(_rs_to_chips_comm(rs0[1]), "rs_to_chips_0")
    recv = [(rs0[0], recv_b0), (rs1[0], recv_b1)]
    shapes4 = [w_ffn_up.shape, w_ffn_down.shape, w_in.shape, w_out.shape]
    tensors, places, r2s = [None] * 4, [], []
    for l in range(2):
        k = 0
        for gi, (gname, ps, leads) in enumerate(layer_groups(l)):
            for (pf, _), lead in zip(ps, leads):
                tensors[gi] = _add_chips_into(pf, recv[l][0][k], recv[l][1][k], scvec, tensors[gi], shapes4[gi], lead,
                                              f"rs_add_chips_{gname}_{'_'.join(map(str, lead))}")
                places.append((gi, lead))
                r2s.append(pf.shape[1] // 2)
                k += 1
    gw_up, gw_dn, gw_in, gw_out = _share_halves(tensors, places, r2s)

    def upd(w, g, m, v, name):
        sh = w.shape
        two = (w.size // sh[-1], sh[-1])
        dlt, mn, vn = _adamw(w.reshape(two), g.reshape(two), m.reshape(two), v.reshape(two), name)
        return dlt.reshape(sh), mn.reshape(sh), vn.reshape(sh)

    big = {
        "w_ada": (w_ada, grad_w_ada, m_w_ada, v_w_ada),
        "w_ffn_up": (w_ffn_up, gw_up, m_w_ffn_up, v_w_ffn_up),
        "w_ffn_down": (w_ffn_down, gw_dn, m_w_ffn_down, v_w_ffn_down),
        "w_in": (w_in, gw_in, m_w_in, v_w_in),
        "w_out": (w_out, gw_out, m_w_out, v_w_out),
    }
    res = {n: (t[1],) + upd(*t, f"adamw_{n}") for n, t in big.items()}

    smalls = {
        "b_ada": (b_ada, grad_b_ada, m_b_ada, v_b_ada),
        "g_norm": (g_norm, g_gn, m_g_norm, v_g_norm),
        "b_fgate": (b_fgate, g_bf, m_b_fgate, v_b_fgate),
        "conv_w": (conv_w, g_cw, m_conv_w, v_conv_w),
        "conv_b": (conv_b, g_cb, m_conv_b, v_conv_b),
        "w_rgate": (w_rgate, g_wr, m_w_rgate, v_w_rgate),
        "b_rgate": (b_rgate, g_br, m_b_rgate, v_b_rgate),
        "w_igate": (w_igate, g_wi, m_w_igate, v_w_igate),
        "b_igate": (b_igate, g_bi, m_b_igate, v_b_igate),
        "lru_lambda": (lru_lambda, g_lam, m_lru_lambda, v_lru_lambda),
        "g_qk": (g_qk, g_gqk, m_g_qk, v_g_qk),
        "g_mix_out": (g_mix_out, g_gmix, m_g_mix_out, v_g_mix_out),
    }
    names = list(smalls)
    shapes = [smalls[n][0].shape for n in names]
    prow = -(-sum(math.prod(s) for s in shapes) // (8 * ROW)) * 8
    packed = [_pack_rows([smalls[n][i].reshape(shapes[k]) for k, n in enumerate(names)], prow, F32) for i in range(4)]
    outs = _adamw(packed[0], packed[1], packed[2], packed[3], "adamw_small")
    un = [_unpack(o.reshape(-1), shapes) for o in outs]
    for k, n in enumerate(names):
        res[n] = (smalls[n][1].reshape(shapes[k]), un[0][k], un[1][k], un[2][k])

    order = ["w_ada", "b_ada", "g_norm", "w_ffn_up", "w_ffn_down", "w_in", "b_fgate", "conv_w", "conv_b",
             "w_rgate", "b_rgate", "w_igate", "b_igate", "lru_lambda", "g_qk", "g_mix_out", "w_out"]
    return (loss, grad_x, *[res[n][0] for n in order], *[res[n][1] for n in order],
            *[res[n][2] for n in order], *[res[n][3] for n in order])


def TQ_(S):
    return min(TQ, S)


def TK_(S):
    return min(TK, S)


def _unpack_shards(wg, shapes):
    out, off = [], 0
    for sh in shapes:
        n = math.prod(sh)
        out.append(wg[:, off:off + n].reshape((N_SHARD,) + tuple(sh)))
        off += n
    return out
```

```python
import math

import jax
import jax.numpy as jnp
from jax import lax
from jax.experimental import pallas as pl
from jax.experimental.pallas import tpu as pltpu

F32 = jnp.float32
BF16 = jnp.bfloat16
MESH = pl.DeviceIdType.MESH

D = 1024
HD = 64
LW = 512
NH = 4
AW = NH * HD
N_IN = 2564
N_INP = 2688
F_BLK = 2560 // 128
EPS = 1e-6
LRU_C = 8.0
SCALE = HD ** -0.5
NEG = -1e30
TQ = 256
TK = 256

ADAM_LR, ADAM_B1, ADAM_B2, ADAM_EPS, ADAM_WD, ADAM_STEP = 0.001, 0.9, 0.999, 1e-08, 0.01, 10

VMEM_BIG = 56 * 1024 * 1024
N_DEV = 8
N_SHARD = 4
ROW = 1024


def _cp(sem, vmem=None):
    return pltpu.CompilerParams(dimension_semantics=sem, vmem_limit_bytes=vmem)


def _dot(a, b):
    return jnp.dot(a, b, preferred_element_type=F32)


def _dot_nt(a, b):
    return lax.dot_general(a, b, (((1,), (1,)), ((), ())), preferred_element_type=F32)


def _dot_tn(a, b):
    return lax.dot_general(a, b, (((0,), (0,)), ((), ())), preferred_element_type=F32)


def _log1p(e):
    small = e * (1.0 - e * (0.5 - e * (1.0 / 3.0 - e * 0.25)))
    return jnp.where(e < 0.01, small, jnp.log(1.0 + e))


def _expm1_neg(x):
    small = x * (1.0 + x * 0.5 * (1.0 + x * (1.0 / 3.0) * (1.0 + x * 0.25 * (1.0 + x * 0.2))))
    return jnp.where(x > -0.05, small, jnp.exp(x) - 1.0)


def _sigmoid(x):
    return 1.0 / (1.0 + jnp.exp(-x))


_GELU_C = math.sqrt(2.0 / math.pi)


def _gelu_and_grad(x):
    x2 = x * x
    th = jnp.tanh(_GELU_C * (x + 0.044715 * x * x2))
    g = 0.5 * x * (1.0 + th)
    dg = 0.5 * (1.0 + th) + 0.5 * x * (1.0 - th * th) * _GELU_C * (1.0 + 3.0 * 0.044715 * x2)
    return g, dg


def _rms_rows(x):
    rstd = lax.rsqrt(jnp.mean(x * x, axis=-1, keepdims=True) + EPS)
    return x * rstd, rstd


def _rms_bwd(xn, rstd, dyn):
    return rstd * (dyn - xn * jnp.mean(dyn * xn, axis=-1, keepdims=True))


def _colsum(x):
    return jnp.sum(x, axis=0, keepdims=True)


def _rowsum(x):
    return jnp.sum(x, axis=1, keepdims=True)


def _split3(x):
    hi = x.astype(BF16)
    r = x - hi.astype(F32)
    mid = r.astype(BF16)
    lo = (r - mid.astype(F32)).astype(BF16)
    return hi, mid, lo


def _cumsum_mm(x, ones_tri, parts=3):
    ps = _split3(x)[:parts]
    acc = _dot(ps[0], ones_tri)
    for p in ps[1:]:
        acc = acc + _dot(p, ones_tri)
    return acc


def _tri(n, kind):
    r = lax.broadcasted_iota(jnp.int32, (n, n), 0)
    c = lax.broadcasted_iota(jnp.int32, (n, n), 1)
    m = {"row_gt_col": r > c, "row_le_col": r <= c, "row_lt_col": r < c}[kind]
    return jnp.where(m, 1.0, 0.0).astype(BF16)


def _normmod(x, mod_ref):
    xn, rstd = _rms_rows(x)
    h = xn * mod_ref[3:4, :] * mod_ref[1:2, :] + mod_ref[0:1, :]
    return h, xn, rstd


def _normmod_bwd(dh, xn, rstd, mod_ref, dmod_ref, wacc_ref):
    gn = mod_ref[3:4, :]
    sc = mod_ref[1:2, :]
    dmod_ref[0:1, :] += _colsum(dh)
    t = _colsum(dh * xn)
    dmod_ref[1:2, :] += t * gn
    wacc_ref[0:1, :] += t * sc
    return _rms_bwd(xn, rstd, dh * (gn * sc))


def _tile(n, want):
    t = min(n, want)
    while n % t:
        t //= 2
    return t


def _tile_div8(n, cap, mult=8):
    best = mult
    for t in range(mult, min(n, cap) + 1, mult):
        if n % t == 0:
            best = t
    assert n % best == 0
    return best


def _ffn_fwd(x, mod, wup5, wdn4, l, j, S):
    T = x.shape[0]
    tf = wup5.shape[-1]
    nk = 2
    tm = _tile(S, 512)
    tpb = S // tm

    def body(x_ref, mod_ref, wg_ref, wu_ref, wd_ref, xo_ref, g_ref, u_ref, f_ref, h_sc, acc_sc):
        k = pl.program_id(1)

        @pl.when(k == 0)
        def _():
            h, _, _ = _normmod(x_ref[...], mod_ref)
            h_sc[...] = h.astype(BF16)
            acc_sc[...] = jnp.zeros_like(acc_sc)

        h = h_sc[...]
        g = _dot(h, wg_ref[...])
        u = _dot(h, wu_ref[...])
        g_ref[...] = g.astype(BF16)
        u_ref[...] = u.astype(BF16)
        a = (g * _sigmoid(g)) * u
        acc_sc[...] += _dot(a.astype(BF16), wd_ref[...])

        @pl.when(k == nk - 1)
        def _():
            f = acc_sc[...]
            f_ref[...] = f.astype(BF16)
            xo_ref[...] = x_ref[...] + (0.5 * mod_ref[2:3, :]) * f

    return pl.pallas_call(
        body, name=f"ffn_fwd_{l}_{j}",
        grid=(T // tm, nk),
        in_specs=[
            pl.BlockSpec((tm, D), lambda i, k: (i, 0)),
            pl.BlockSpec((None, 8, D), lambda i, k: (i // tpb, 0, 0)),
            pl.BlockSpec((None, None, D, tf), lambda i, k: (k, j, 0, 0)),
            pl.BlockSpec((None, None, D, tf), lambda i, k: (nk + k, j, 0, 0)),
            pl.BlockSpec((None, tf, D), lambda i, k: (j, k, 0)),
        ],
        out_specs=[
            pl.BlockSpec((tm, D), lambda i, k: (i, 0)),
            pl.BlockSpec((tm, tf), lambda i, k: (i, k)),
            pl.BlockSpec((tm, tf), lambda i, k: (i, k)),
            pl.BlockSpec((tm, D), lambda i, k: (i, 0)),
        ],
        out_shape=[
            jax.ShapeDtypeStruct((T, D), F32),
            jax.ShapeDtypeStruct((T, nk * tf), BF16),
            jax.ShapeDtypeStruct((T, nk * tf), BF16),
            jax.ShapeDtypeStruct((T, D), BF16),
        ],
        scratch_shapes=[pltpu.VMEM((tm, D), BF16), pltpu.VMEM((tm, D), F32)],
        compiler_params=_cp(("arbitrary", "arbitrary"), VMEM_BIG),
    )(x, mod, wup5, wup5, wdn4)


def _ffn_bwd(x, dy, mod, f, g, u, wup5, wdn4, l, j, S):
    T = x.shape[0]
    tf = wup5.shape[-1]
    nk = 2
    tm = _tile(S, 256)
    tpb = S // tm

    def body(x_ref, dy_ref, mod_ref, f_ref, g_ref, u_ref, wup_ref, wd_ref,
             dx_ref, dmod_ref, wacc_ref, h_ref, df_ref, a_ref, dgu_ref):
        i = pl.program_id(0)

        @pl.when(i == 0)
        def _():
            wacc_ref[...] = jnp.zeros_like(wacc_ref)

        @pl.when(i % tpb == 0)
        def _():
            dmod_ref[...] = jnp.zeros_like(dmod_ref)

        dy_ = dy_ref[...]
        h, xn, rstd = _normmod(x_ref[...], mod_ref)
        h_ref[...] = h.astype(BF16)
        dfb = ((0.5 * mod_ref[2:3, :]) * dy_).astype(BF16)
        df_ref[...] = dfb
        dmod_ref[2:3, :] += _colsum(0.5 * f_ref[...].astype(F32) * dy_)
        dh = None
        for k in range(nk):
            cols = slice(k * tf, (k + 1) * tf)
            da = _dot_nt(dfb, wd_ref[cols, :])
            gg = g_ref[:, cols].astype(F32)
            uu = u_ref[:, cols].astype(F32)
            sig = _sigmoid(gg)
            s = gg * sig
            a_ref[:, cols] = (s * uu).astype(BF16)
            du = (da * s).astype(BF16)
            dg = (da * uu * (sig * (1.0 + gg * (1.0 - sig)))).astype(BF16)
            dgu_ref[0, :, cols] = dg
            dgu_ref[1, :, cols] = du
            part = _dot_nt(dg, wup_ref[k]) + _dot_nt(du, wup_ref[nk + k])
            dh = part if dh is None else dh + part
        dx_ref[...] = dy_ + _normmod_bwd(dh, xn, rstd, mod_ref, dmod_ref, wacc_ref)

    once = pl.Buffered(1)
    return pl.pallas_call(
        body, name=f"ffn_bwd_{l}_{j}",
        grid=(T // tm,),
        in_specs=[
            pl.BlockSpec((tm, D), lambda i: (i, 0)),
            pl.BlockSpec((tm, D), lambda i: (i, 0)),
            pl.BlockSpec((None, 8, D), lambda i: (i // tpb, 0, 0)),
            pl.BlockSpec((tm, D), lambda i: (i, 0)),
            pl.BlockSpec((tm, nk * tf), lambda i: (i, 0)),
            pl.BlockSpec((tm, nk * tf), lambda i: (i, 0)),
            pl.BlockSpec((2 * nk, None, D, tf), lambda i: (0, j, 0, 0), pipeline_mode=once),
            pl.BlockSpec((None, nk * tf, D), lambda i: (j, 0, 0), pipeline_mode=once),
        ],
        out_specs=[
            pl.BlockSpec((tm, D), lambda i: (i, 0)),
            pl.BlockSpec((None, 8, D), lambda i: (i // tpb, 0, 0)),
            pl.BlockSpec((8, D), lambda i: (0, 0)),
            pl.BlockSpec((tm, D), lambda i: (i, 0)),
            pl.BlockSpec((tm, D), lambda i: (i, 0)),
            pl.BlockSpec((tm, nk * tf), lambda i: (i, 0)),
            pl.BlockSpec((2, tm, nk * tf), lambda i: (0, i, 0)),
        ],
        out_shape=[
            jax.ShapeDtypeStruct((T, D), F32),
            jax.ShapeDtypeStruct((T // S, 8, D), F32),
            jax.ShapeDtypeStruct((8, D), F32),
            jax.ShapeDtypeStruct((T, D), BF16),
            jax.ShapeDtypeStruct((T, D), BF16),
            jax.ShapeDtypeStruct((T, nk * tf), BF16),
            jax.ShapeDtypeStruct((2, T, nk * tf), BF16),
        ],
        compiler_params=_cp(("arbitrary",), VMEM_BIG),
    )(x, dy, mod, f, g, u, wup5, wdn4)


def _mm_tn(a, b, name, tma=None, tnb=None, split_n=False, with_bf16=False):
    T, M = a.shape
    b3 = b if b.ndim == 3 else b[None]
    nb, _, N = b3.shape
    tma = tma or M
    tnb = tnb or N
    npb = N // tnb
    tt = _tile(T, 1024)
    nt = T // tt

    def body(a_ref, b_ref, o_ref, *ob_ref):
        @pl.when(pl.program_id(2) == 0)
        def _():
            o_ref[...] = jnp.zeros_like(o_ref)

        o_ref[...] += _dot_tn(a_ref[...], b_ref[...])

        if with_bf16:
            @pl.when(pl.program_id(2) == nt - 1)
            def _():
                ob_ref[0][...] = o_ref[...].astype(BF16)

    if split_n:
        shape = (nb * npb, M, tnb)
        out_spec = pl.BlockSpec((None, tma, tnb), lambda m, n, t: (n, m, 0))
    else:
        assert nb == 1
        shape = (M, N)
        out_spec = pl.BlockSpec((tma, tnb), lambda m, n, t: (m, n))
    dts = (F32, BF16) if with_bf16 else (F32,)
    out = pl.pallas_call(
        body, name=name,
        grid=(M // tma, nb * npb, nt),
        in_specs=[pl.BlockSpec((tt, tma), lambda m, n, t: (t, m)),
                  pl.BlockSpec((None, tt, tnb), lambda m, n, t: (n // npb, t, n % npb))],
        out_specs=[out_spec] * len(dts),
        out_shape=[jax.ShapeDtypeStruct(shape, dt) for dt in dts],
        compiler_params=_cp(("arbitrary", "arbitrary", "arbitrary"), VMEM_BIG),
    )(a, b3)
    return tuple(out) if with_bf16 else out[0]


def _mix_in_fwd(x, mod, winp, l, S):
    T = x.shape[0]
    tm = _tile(S, 512)
    tpb = S // tm

    def body(x_ref, mod_ref, w_ref, h_ref, p_ref):
        h, _, _ = _normmod(x_ref[...], mod_ref)
        hb = h.astype(BF16)
        h_ref[...] = hb
        p_ref[...] = _dot(hb, w_ref[...])

    return pl.pallas_call(
        body, name=f"mix_in_fwd_{l}",
        grid=(T // tm,),
        in_specs=[pl.BlockSpec((tm, D), lambda i: (i, 0)),
                  pl.BlockSpec((None, 8, D), lambda i: (i // tpb, 0, 0)),
                  pl.BlockSpec((D, N_INP), lambda i: (0, 0))],
        out_specs=[pl.BlockSpec((tm, D), lambda i: (i, 0)),
                   pl.BlockSpec((tm, N_INP), lambda i: (i, 0))],
        out_shape=[jax.ShapeDtypeStruct((T, D), BF16), jax.ShapeDtypeStruct((T, N_INP), F32)],
        compiler_params=_cp(("arbitrary",), VMEM_BIG),
    )(x, mod, winp)


def _mix_in_bwd(x, dres, mod, dproj, winp, l, S):
    T = x.shape[0]
    tm = _tile(S, 512)
    tpb = S // tm

    def body(x_ref, dr_ref, mod_ref, dp_ref, w_ref, dx_ref, dmod_ref, wacc_ref):
        i = pl.program_id(0)

        @pl.when(i == 0)
        def _():
            wacc_ref[...] = jnp.zeros_like(wacc_ref)

        @pl.when(i % tpb == 0)
        def _():
            dmod_ref[...] = jnp.zeros_like(dmod_ref)

        dh = _dot_nt(dp_ref[...], w_ref[...])
        _, xn, rstd = _normmod(x_ref[...], mod_ref)
        dx_ref[...] = dr_ref[...] + _normmod_bwd(dh, xn, rstd, mod_ref, dmod_ref, wacc_ref)

    return pl.pallas_call(
        body, name=f"mix_in_bwd_{l}",
        grid=(T // tm,),
        in_specs=[pl.BlockSpec((tm, D), lambda i: (i, 0)),
                  pl.BlockSpec((tm, D), lambda i: (i, 0)),
                  pl.BlockSpec((None, 8, D), lambda i: (i // tpb, 0, 0)),
                  pl.BlockSpec((tm, N_INP), lambda i: (i, 0)),
                  pl.BlockSpec((D, N_INP), lambda i: (0, 0))],
        out_specs=[pl.BlockSpec((tm, D), lambda i: (i, 0)),
                   pl.BlockSpec((None, 8, D), lambda i: (i // tpb, 0, 0)),
                   pl.BlockSpec((8, D), lambda i: (0, 0))],
        out_shape=[jax.ShapeDtypeStruct((T, D), F32),
                   jax.ShapeDtypeStruct((T // S, 8, D), F32),
                   jax.ShapeDtypeStruct((8, D), F32)],
        compiler_params=_cp(("arbitrary",), VMEM_BIG),
    )(x, dres, mod, dproj, winp)


_GROUPS = ((0, LW), (LW, LW + AW), (LW + AW, D))


def _mix_out_fwd(x, ylru, osb, ofox, mod, gmix, wout, l, S):
    T = x.shape[0]
    tm = _tile(S, 512)
    tpb = S // tm

    def body(x_ref, yl_ref, sb_ref, fx_ref, mod_ref, gm_ref, w_ref, xo_ref, y_ref, mo_ref):
        for src, (lo, hi) in zip((yl_ref, sb_ref, fx_ref), _GROUPS):
            vn, _ = _rms_rows(src[...])
            y_ref[:, lo:hi] = (vn * gm_ref[0:1, lo:hi]).astype(BF16)
        mo = _dot(y_ref[...], w_ref[...])
        mo_ref[...] = mo.astype(BF16)
        xo_ref[...] = x_ref[...] + mod_ref[2:3, :] * mo

    return pl.pallas_call(
        body, name=f"mix_out_fwd_{l}",
        grid=(T // tm,),
        in_specs=[pl.BlockSpec((tm, D), lambda i: (i, 0)),
                  pl.BlockSpec((tm, LW), lambda i: (i, 0)),
                  pl.BlockSpec((tm, AW), lambda i: (i, 0)),
                  pl.BlockSpec((tm, AW), lambda i: (i, 0)),
                  pl.BlockSpec((None, 8, D), lambda i: (i // tpb, 0, 0)),
                  pl.BlockSpec((None, 8, D), lambda i: (l, 0, 0)),
                  pl.BlockSpec((D, D), lambda i: (0, 0))],
        out_specs=[pl.BlockSpec((tm, D), lambda i: (i, 0)),
                   pl.BlockSpec((tm, D), lambda i: (i, 0)),
                   pl.BlockSpec((tm, D), lambda i: (i, 0))],
        out_shape=[jax.ShapeDtypeStruct((T, D), F32),
                   jax.ShapeDtypeStruct((T, D), BF16),
                   jax.ShapeDtypeStruct((T, D), BF16)],
        compiler_params=_cp(("arbitrary",), VMEM_BIG),
    )(x, ylru, osb, ofox, mod, gmix, wout)


def _mix_out_bwd(dx2, ylru, osb, ofox, mo, mod, gmix, wout, l, S):
    T = dx2.shape[0]
    tm = _tile(S, 512)
    tpb = S // tm

    def body(dx_ref, yl_ref, sb_ref, fx_ref, mo_ref, mod_ref, gm_ref, w_ref,
             dyl_ref, dsb_ref, dfx_ref, dmo_ref, dmod_ref, wacc_ref):
        i = pl.program_id(0)

        @pl.when(i == 0)
        def _():
            wacc_ref[...] = jnp.zeros_like(wacc_ref)

        @pl.when(i % tpb == 0)
        def _():
            dmod_ref[...] = jnp.zeros_like(dmod_ref)

        dx = dx_ref[...]
        dmod_ref[2:3, :] += _colsum(mo_ref[...].astype(F32) * dx)
        dmo = (mod_ref[2:3, :] * dx).astype(BF16)
        dmo_ref[...] = dmo
        dy = _dot_nt(dmo, w_ref[...])
        for src, dst, (lo, hi) in zip((yl_ref, sb_ref, fx_ref), (dyl_ref, dsb_ref, dfx_ref), _GROUPS):
            vn, rstd = _rms_rows(src[...])
            dyg = dy[:, lo:hi]
            wacc_ref[0:1, lo:hi] += _colsum(dyg * vn)
            dst[...] = _rms_bwd(vn, rstd, dyg * gm_ref[0:1, lo:hi])

    return pl.pallas_call(
        body, name=f"mix_out_bwd_{l}",
        grid=(T // tm,),
        in_specs=[pl.BlockSpec((tm, D), lambda i: (i, 0)),
                  pl.BlockSpec((tm, LW), lambda i: (i, 0)),
                  pl.BlockSpec((tm, AW), lambda i: (i, 0)),
                  pl.BlockSpec((tm, AW), lambda i: (i, 0)),
                  pl.BlockSpec((tm, D), lambda i: (i, 0)),
                  pl.BlockSpec((None, 8, D), lambda i: (i // tpb, 0, 0)),
                  pl.BlockSpec((None, 8, D), lambda i: (l, 0, 0)),
                  pl.BlockSpec((D, D), lambda i: (0, 0))],
        out_specs=[pl.BlockSpec((tm, LW), lambda i: (i, 0)),
                   pl.BlockSpec((tm, AW), lambda i: (i, 0)),
                   pl.BlockSpec((tm, AW), lambda i: (i, 0)),
                   pl.BlockSpec((tm, D), lambda i: (i, 0)),
                   pl.BlockSpec((None, 8, D), lambda i: (i // tpb, 0, 0)),
                   pl.BlockSpec((8, D), lambda i: (0, 0))],
        out_shape=[jax.ShapeDtypeStruct((T, LW), F32),
                   jax.ShapeDtypeStruct((T, AW), F32),
                   jax.ShapeDtypeStruct((T, AW), F32),
                   jax.ShapeDtypeStruct((T, D), BF16),
                   jax.ShapeDtypeStruct((T // S, 8, D), F32),
                   jax.ShapeDtypeStruct((8, D), F32)],
        compiler_params=_cp(("arbitrary",), VMEM_BIG),
    )(dx2, ylru, osb, ofox, mo, mod, gmix, wout)


def _loss_head(y, tgt, S):
    T = y.shape[0]
    tm = _tile(S, 512)

    def body(y_ref, t_ref, dy_ref, l_ref):
        @pl.when(pl.program_id(0) == 0)
        def _():
            l_ref[...] = jnp.zeros_like(l_ref)

        d = y_ref[...] - t_ref[...]
        dy_ref[...] = d * (1.0 / D)
        l_ref[...] += (0.5 / D) * _rowsum(_colsum(d * d))

    return pl.pallas_call(
        body, name="loss_head",
        grid=(T // tm,),
        in_specs=[pl.BlockSpec((tm, D), lambda i: (i, 0)), pl.BlockSpec((tm, D), lambda i: (i, 0))],
        out_specs=[pl.BlockSpec((tm, D), lambda i: (i, 0)), pl.BlockSpec((8, 128), lambda i: (0, 0))],
        out_shape=[jax.ShapeDtypeStruct((T, D), F32), jax.ShapeDtypeStruct((8, 128), F32)],
        compiler_params=_cp(("arbitrary",)),
    )(y, tgt)


def _lru_gates(u, vp_ref, wr_ref, wi_ref):
    ub = u.astype(BF16)
    r = _sigmoid(_dot(ub, wr_ref[...]) + vp_ref[1:2, :])
    ig = _sigmoid(_dot(ub, wi_ref[...]) + vp_ref[2:3, :])
    lam = vp_ref[3:4, :]
    sp = jnp.maximum(-lam, 0.0) + _log1p(jnp.exp(-jnp.abs(lam)))
    log_a = (-LRU_C) * r * sp
    a = jnp.exp(log_a)
    mult = jnp.sqrt(-_expm1_neg(2.0 * log_a))
    return ub, r, ig, sp, a, mult


def _conv_taps(x, xp, row, cw_ref):
    xs = [x]
    for d in (1, 2, 3):
        xs.append(jnp.where(row >= d, pltpu.roll(x, d, 0), pltpu.roll(xp, d, 0)))
    u = xs[0] * cw_ref[3:4, :]
    for d in (1, 2, 3):
        u = u + xs[d] * cw_ref[3 - d:4 - d, :]
    return xs, u


def _lru_fwd(proj, cw, vp, wr, wi, l, S):
    T = proj.shape[0]
    ts = _tile(S, 256)
    nb = S // ts

    def body(x_ref, lg_ref, cw_ref, vp_ref, wr_ref, wi_ref, y_ref, h_ref, xp_sc, hc_sc):
        @pl.when(pl.program_id(1) == 0)
        def _():
            xp_sc[...] = jnp.zeros_like(xp_sc)
            hc_sc[...] = jnp.zeros_like(hc_sc)

        row = lax.broadcasted_iota(jnp.int32, (ts, LW), 0)
        x = x_ref[...]
        _, u = _conv_taps(x, xp_sc[...], row, cw_ref)
        u = u + vp_ref[0:1, :]
        xp_sc[...] = x
        _, _, ig, _, a, mult = _lru_gates(u, vp_ref, wr_ref, wi_ref)
        bv = mult * (ig * u)
        av = a
        d = 1
        while d < ts:
            a_s = jnp.where(row >= d, pltpu.roll(av, d, 0), 1.0)
            b_s = jnp.where(row >= d, pltpu.roll(bv, d, 0), 0.0)
            bv = av * b_s + bv
            av = av * a_s
            d *= 2
        h = bv + av * hc_sc[7:8, :]
        hc_sc[...] = h[ts - 8:ts, :]
        h_ref[...] = h
        gl, _ = _gelu_and_grad(lg_ref[...])
        y_ref[...] = h * gl

    return pl.pallas_call(
        body, name=f"lru_fwd_{l}",
        grid=(T // S, nb),
        in_specs=[pl.BlockSpec((ts, LW), lambda b, j: (b * nb + j, 0)),
                  pl.BlockSpec((ts, LW), lambda b, j: (b * nb + j, 1)),
                  pl.BlockSpec((None, 8, LW), lambda b, j: (l, 0, 0)),
                  pl.BlockSpec((None, 8, LW), lambda b, j: (l, 0, 0)),
                  pl.BlockSpec((None, LW, LW), lambda b, j: (l, 0, 0)),
                  pl.BlockSpec((None, LW, LW), lambda b, j: (l, 0, 0))],
        out_specs=[pl.BlockSpec((ts, LW), lambda b, j: (b * nb + j, 0)),
                   pl.BlockSpec((ts, LW), lambda b, j: (b * nb + j, 0))],
        out_shape=[jax.ShapeDtypeStruct((T, LW), F32), jax.ShapeDtypeStruct((T, LW), F32)],
        scratch_shapes=[pltpu.VMEM((ts, LW), F32), pltpu.VMEM((8, LW), F32)],
        compiler_params=_cp(("arbitrary", "arbitrary")),
    )(proj, proj, cw, vp, wr, wi)


def _lru_bwd(dyl, proj, h, cw, vp, wr, wi, l, S):
    T = proj.shape[0]
    ts = _tile(S, 256)
    nb = S // ts

    def body(dy_ref, x_ref, xprev_ref, lg_ref, h_ref, hprev_ref, cw_ref, vp_ref, wr_ref, wi_ref,
             dx_ref, dlg_ref, dpr_ref, dpi_ref, ub_ref, wacc_ref, gc_sc, af_sc, dun_sc):
        b = pl.program_id(0)
        j = pl.program_id(1)
        first = j == nb - 1

        @pl.when((b == 0) & (j == 0))
        def _():
            wacc_ref[...] = jnp.zeros_like(wacc_ref)

        @pl.when(j == 0)
        def _():
            gc_sc[...] = jnp.zeros_like(gc_sc)
            af_sc[...] = jnp.ones_like(af_sc)
            dun_sc[...] = jnp.zeros_like(dun_sc)

        row = lax.broadcasted_iota(jnp.int32, (ts, LW), 0)
        keep = jnp.where(first, 0.0, 1.0)
        x = x_ref[...]
        xs, u = _conv_taps(x, xprev_ref[...] * keep, row, cw_ref)
        u = u + vp_ref[0:1, :]
        ub, r, ig, sp, a, mult = _lru_gates(u, vp_ref, wr_ref, wi_ref)
        ub_ref[...] = ub
        hh = h_ref[...]
        h_m1 = jnp.where(row >= 1, pltpu.roll(hh, 1, 0), pltpu.roll(hprev_ref[...] * keep, 1, 0))
        dy = dy_ref[...]
        gl, dgl = _gelu_and_grad(lg_ref[...])
        dlg_ref[...] = (dy * hh * dgl).astype(BF16)
        bv = dy * gl
        av = jnp.where(row < ts - 1, pltpu.roll(a, ts - 1, 0), af_sc[0:1, :])
        d = 1
        while d < ts:
            a_s = jnp.where(row < ts - d, pltpu.roll(av, ts - d, 0), 1.0)
            b_s = jnp.where(row < ts - d, pltpu.roll(bv, ts - d, 0), 0.0)
            bv = av * b_s + bv
            av = av * a_s
            d *= 2
        gt = bv + av * gc_sc[0:1, :]
        gc_sc[...] = gt[0:8, :]
        af_sc[...] = a[0:8, :]
        da = gt * h_m1
        d_ig = gt * mult * u
        d_mult = gt * ig * u
        du = gt * mult * ig
        dlog_a = da * a - d_mult * (a * a) / mult
        dpre_r = (dlog_a * ((-LRU_C) * sp)) * r * (1.0 - r)
        dpre_i = d_ig * ig * (1.0 - ig)
        lam = vp_ref[3:4, :]
        wacc_ref[7:8, :] += _colsum(dlog_a * r) * (LRU_C * _sigmoid(-lam))
        wacc_ref[5:6, :] += _colsum(dpre_r)
        wacc_ref[6:7, :] += _colsum(dpre_i)
        dprb = dpre_r.astype(BF16)
        dpib = dpre_i.astype(BF16)
        dpr_ref[...] = dprb
        dpi_ref[...] = dpib
        du = du + _dot_nt(dprb, wr_ref[...]) + _dot_nt(dpib, wi_ref[...])
        wacc_ref[4:5, :] += _colsum(du)
        dun = dun_sc[...]
        dx = du * cw_ref[3:4, :]
        wacc_ref[3:4, :] += _colsum(du * xs[0])
        for dd in (1, 2, 3):
            du_s = jnp.where(row < ts - dd, pltpu.roll(du, ts - dd, 0), pltpu.roll(dun, ts - dd, 0))
            dx = dx + du_s * cw_ref[3 - dd:4 - dd, :]
            wacc_ref[3 - dd:4 - dd, :] += _colsum(du * xs[dd])
        dun_sc[...] = du
        dx_ref[...] = dx.astype(BF16)

    def tb(b, j):
        return b * nb + (nb - 1 - j)

    def tbp(b, j):
        return b * nb + jnp.maximum(nb - 2 - j, 0)

    return pl.pallas_call(
        body, name=f"lru_bwd_{l}",
        grid=(T // S, nb),
        in_specs=[pl.BlockSpec((ts, LW), lambda b, j: (tb(b, j), 0)),
                  pl.BlockSpec((ts, LW), lambda b, j: (tb(b, j), 0)),
                  pl.BlockSpec((ts, LW), lambda b, j: (tbp(b, j), 0)),
                  pl.BlockSpec((ts, LW), lambda b, j: (tb(b, j), 1)),
                  pl.BlockSpec((ts, LW), lambda b, j: (tb(b, j), 0)),
                  pl.BlockSpec((ts, LW), lambda b, j: (tbp(b, j), 0)),
                  pl.BlockSpec((None, 8, LW), lambda b, j: (l, 0, 0)),
                  pl.BlockSpec((None, 8, LW), lambda b, j: (l, 0, 0)),
                  pl.BlockSpec((None, LW, LW), lambda b, j: (l, 0, 0)),
                  pl.BlockSpec((None, LW, LW), lambda b, j: (l, 0, 0))],
        out_specs=[pl.BlockSpec((ts, LW), lambda b, j: (tb(b, j), 0)),
                   pl.BlockSpec((ts, LW), lambda b, j: (tb(b, j), 0)),
                   pl.BlockSpec((ts, LW), lambda b, j: (tb(b, j), 0)),
                   pl.BlockSpec((ts, LW), lambda b, j: (tb(b, j), 0)),
                   pl.BlockSpec((ts, LW), lambda b, j: (tb(b, j), 0)),
                   pl.BlockSpec((8, LW), lambda b, j: (0, 0))],
        out_shape=[jax.ShapeDtypeStruct((T, LW), BF16),
                   jax.ShapeDtypeStruct((T, LW), BF16),
                   jax.ShapeDtypeStruct((T, LW), BF16),
                   jax.ShapeDtypeStruct((T, LW), BF16),
                   jax.ShapeDtypeStruct((T, LW), BF16),
                   jax.ShapeDtypeStruct((8, LW), F32)],
        scratch_shapes=[pltpu.VMEM((8, LW), F32), pltpu.VMEM((8, LW), F32), pltpu.VMEM((ts, LW), F32)],
        compiler_params=_cp(("arbitrary", "arbitrary")),
    )(dyl, proj, proj, proj, h, h, cw, vp, wr, wi)


def _fgate_fwd(proj, bfp, l, S):
    T = proj.shape[0]

    def body(x_ref, b_ref, o_ref):
        z = x_ref[...] + b_ref[0:1, :]
        v = jnp.minimum(z, 0.0) - _log1p(jnp.exp(-jnp.abs(z)))
        row = lax.broadcasted_iota(jnp.int32, (S, 128), 0)
        d = 1
        while d < S:
            v = v + jnp.where(row >= d, pltpu.roll(v, d, 0), 0.0)
            d *= 2
        o_ref[...] = v

    return pl.pallas_call(
        body, name=f"fgate_fwd_{l}",
        grid=(T // S,),
        in_specs=[pl.BlockSpec((S, 128), lambda b: (b, F_BLK)),
                  pl.BlockSpec((None, 8, 128), lambda b: (l, 0, 0))],
        out_specs=pl.BlockSpec((S, 128), lambda b: (b, 0)),
        out_shape=jax.ShapeDtypeStruct((T, 128), F32),
        compiler_params=_cp(("arbitrary",)),
    )(proj, bfp)


def _fgate_bwd(dcum, proj, bfp, l, S):
    T = proj.shape[0]

    def body(d_ref, x_ref, b_ref, o_ref, wacc_ref):
        @pl.when(pl.program_id(0) == 0)
        def _():
            wacc_ref[...] = jnp.zeros_like(wacc_ref)

        v = d_ref[...]
        row = lax.broadcasted_iota(jnp.int32, (S, 128), 0)
        d = 1
        while d < S:
            v = v + jnp.where(row < S - d, pltpu.roll(v, S - d, 0), 0.0)
            d *= 2
        z = x_ref[...] + b_ref[0:1, :]
        dz = v * _sigmoid(-z)
        o_ref[...] = dz.astype(BF16)
        wacc_ref[0:1, :] += _colsum(dz)

    return pl.pallas_call(
        body, name=f"fgate_bwd_{l}",
        grid=(T // S,),
        in_specs=[pl.BlockSpec((S, 128), lambda b: (b, 0)),
                  pl.BlockSpec((S, 128), lambda b: (b, F_BLK)),
                  pl.BlockSpec((None, 8, 128), lambda b: (l, 0, 0))],
        out_specs=[pl.BlockSpec((S, 128), lambda b: (b, 0)), pl.BlockSpec((8, 128), lambda b: (0, 0))],
        out_shape=[jax.ShapeDtypeStruct((T, 128), BF16), jax.ShapeDtypeStruct((8, 128), F32)],
        compiler_params=_cp(("arbitrary",)),
    )(dcum, proj, bfp)


def _logsig_parts(z):
    e = jnp.exp(-jnp.abs(z))
    l1p = jnp.log(1.0 + e)
    return e, jnp.minimum(z, 0.0) - l1p, -jnp.maximum(z, 0.0) - l1p


def _sb_fwd(q, k, v, l):
    B, H, nq, tq, _ = q.shape
    nk, tk = k.shape[2], k.shape[3]
    rr = tq // tk

    def body(q_ref, k_ref, v_ref, o_ref, t1_ref):
        tri = _tri(tk, "row_gt_col")
        ti = lax.broadcasted_iota(jnp.int32, (tq, 1), 0)
        si = lax.broadcasted_iota(jnp.int32, (1, tk), 1)

        def qloop(qb, carry):
            qq = q_ref[qb]
            tpos = qb * tq + ti
            nkb = (qb + 1) * rr

            def kloop(i, c):
                acc, run = c
                kb = nkb - 1 - i
                z = _dot_nt(qq, k_ref[kb]) * SCALE
                past = (kb * tk + si) < tpos
                _, lb, l1 = _logsig_parts(z)
                l1m = jnp.where(past, l1, 0.0)
                aft = _cumsum_mm(l1m, tri) + run
                w = jnp.where(past, jnp.exp(lb + aft), 0.0)
                acc = acc + _dot(w.astype(BF16), v_ref[kb])
                return acc, run + _rowsum(l1m)

            acc, run = lax.fori_loop(0, nkb, kloop, (jnp.zeros((tq, HD), F32), jnp.zeros((tq, 1), F32)))
            o_ref[qb] = acc
            t1_ref[qb] = run
            return carry

        lax.fori_loop(0, nq, qloop, 0)

    qs = pl.BlockSpec((None, None, nq, tq, HD), lambda b, h: (b, h, 0, 0, 0))
    ks = pl.BlockSpec((None, None, nk, tk, HD), lambda b, h: (b, h, 0, 0, 0))
    return pl.pallas_call(
        body, name=f"sb_fwd_{l}",
        grid=(B, H),
        in_specs=[qs, ks, ks],
        out_specs=[qs, pl.BlockSpec((None, None, nq, tq, 1), lambda b, h: (b, h, 0, 0, 0))],
        out_shape=[jax.ShapeDtypeStruct((B, H, nq, tq, HD), F32),
                   jax.ShapeDtypeStruct((B, H, nq, tq, 1), F32)],
        compiler_params=_cp(("arbitrary", "arbitrary"), VMEM_BIG),
    )(q, k, v)


def _sb_bwd(q, k, v, do, t1, l):
    B, H, nq, tq, _ = q.shape
    nk, tk = k.shape[2], k.shape[3]
    rr = tq // tk

    def body(q_ref, k_ref, v_ref, do_ref, t1_ref, dq_ref, dk_ref, dv_ref, dk_sc, dv_sc):
        dk_sc[...] = jnp.zeros_like(dk_sc)
        dv_sc[...] = jnp.zeros_like(dv_sc)
        tri_in = _tri(tk, "row_le_col")
        tri_ex = _tri(tk, "row_lt_col")
        ti = lax.broadcasted_iota(jnp.int32, (tq, 1), 0)
        si = lax.broadcasted_iota(jnp.int32, (1, tk), 1)

        def qloop(qb, carry):
            qq = q_ref[qb]
            dob = do_ref[qb].astype(BF16)
            tot = t1_ref[qb]
            tpos = qb * tq + ti
            nkb = (qb + 1) * rr

            def kloop(kb, c):
                dq, run1, rung = c
                kk = k_ref[kb]
                vv = v_ref[kb]
                z = _dot_nt(qq, kk) * SCALE
                past = (kb * tk + si) < tpos
                e, lb, l1 = _logsig_parts(z)
                l1m = jnp.where(past, l1, 0.0)
                aft = tot - (run1 + _cumsum_mm(l1m, tri_in))
                w = jnp.where(past, jnp.exp(lb + aft), 0.0)
                gm = w * _dot_nt(dob, vv)
                cpre = rung + _cumsum_mm(gm, tri_ex, parts=2)
                inv = 1.0 / (1.0 + e)
                sig = jnp.where(z >= 0.0, inv, e * inv)
                dz = jnp.where(past, gm * (1.0 - sig) - cpre * sig, 0.0).astype(BF16)
                dv_sc[kb] += _dot_tn(w.astype(BF16), dob)
                dk_sc[kb] += _dot_tn(dz, qq) * SCALE
                dq = dq + _dot(dz, kk) * SCALE
                return dq, run1 + _rowsum(l1m), rung + _rowsum(gm)

            z1 = jnp.zeros((tq, 1), F32)
            dq, _, _ = lax.fori_loop(0, nkb, kloop, (jnp.zeros((tq, HD), F32), z1, z1))
            dq_ref[qb] = dq.astype(BF16)
            return carry

        lax.fori_loop(0, nq, qloop, 0)
        dk_ref[...] = dk_sc[...].astype(BF16)
        dv_ref[...] = dv_sc[...].astype(BF16)

    qs = pl.BlockSpec((None, None, nq, tq, HD), lambda b, h: (b, h, 0, 0, 0))
    ks = pl.BlockSpec((None, None, nk, tk, HD), lambda b, h: (b, h, 0, 0, 0))
    return pl.pallas_call(
        body, name=f"sb_bwd_{l}",
        grid=(B, H),
        in_specs=[qs, ks, ks, qs, pl.BlockSpec((None, None, nq, tq, 1), lambda b, h: (b, h, 0, 0, 0))],
        out_specs=[qs, ks, ks],
        out_shape=[jax.ShapeDtypeStruct((B, H, nq, tq, HD), BF16),
                   jax.ShapeDtypeStruct((B, H, nk, tk, HD), BF16),
                   jax.ShapeDtypeStruct((B, H, nk, tk, HD), BF16)],
        scratch_shapes=[pltpu.VMEM((nk, tk, HD), F32), pltpu.VMEM((nk, tk, HD), F32)],
        compiler_params=_cp(("arbitrary", "arbitrary"), VMEM_BIG),
    )(q, k, v, do, t1)


def _fox_fwd(q, k, v, cq, ck, gqk, l):
    B, H, nq, tq, _ = q.shape
    nk, tk = k.shape[2], k.shape[3]
    rr = tq // tk

    def body(q_ref, k_ref, v_ref, cq_ref, ck_ref, g_ref, o_ref, lse_ref, fk_sc):
        g0 = g_ref[0:1, :]
        g1 = g_ref[1:2, :]

        def kprep(kb, c):
            kn, _ = _rms_rows(k_ref[kb])
            fk_sc[kb] = (kn * g1).astype(BF16)
            return c

        lax.fori_loop(0, nk, kprep, 0)
        ti = lax.broadcasted_iota(jnp.int32, (tq, 1), 0)
        si = lax.broadcasted_iota(jnp.int32, (1, tk), 1)

        def qloop(qb, carry):
            qn, _ = _rms_rows(q_ref[qb])
            fq = (qn * g0).astype(BF16)
            cqq = cq_ref[qb]
            tpos = qb * tq + ti

            def kloop(kb, c):
                m, lsum, acc = c
                s = _dot_nt(fq, fk_sc[kb]) * SCALE + cqq - ck_ref[kb]
                s = jnp.where((kb * tk + si) <= tpos, s, NEG)
                m2 = jnp.maximum(m, jnp.max(s, axis=1, keepdims=True))
                al = jnp.exp(m - m2)
                p = jnp.exp(s - m2)
                return m2, al * lsum + _rowsum(p), al * acc + _dot(p.astype(BF16), v_ref[kb])

            m, lsum, acc = lax.fori_loop(
                0, (qb + 1) * rr, kloop,
                (jnp.full((tq, 1), NEG, F32), jnp.zeros((tq, 1), F32), jnp.zeros((tq, HD), F32)))
            o_ref[qb] = acc / lsum
            lse_ref[qb] = m + jnp.log(lsum)
            return carry

        lax.fori_loop(0, nq, qloop, 0)

    qs = pl.BlockSpec((None, None, nq, tq, HD), lambda b, h: (b, h, 0, 0, 0))
    ks = pl.BlockSpec((None, None, nk, tk, HD), lambda b, h: (b, h, 0, 0, 0))
    cqs = pl.BlockSpec((None, None, nq, tq, 1), lambda b, h: (b, h, 0, 0, 0))
    cks = pl.BlockSpec((None, None, nk, 1, tk), lambda b, h: (b, h, 0, 0, 0))
    return pl.pallas_call(
        body, name=f"fox_fwd_{l}",
        grid=(B, H),
        in_specs=[qs, ks, ks, cqs, cks, pl.BlockSpec((None, 8, HD), lambda b, h: (l, 0, 0))],
        out_specs=[qs, cqs],
        out_shape=[jax.ShapeDtypeStruct((B, H, nq, tq, HD), F32),
                   jax.ShapeDtypeStruct((B, H, nq, tq, 1), F32)],
        scratch_shapes=[pltpu.VMEM((nk, tk, HD), BF16)],
        compiler_params=_cp(("arbitrary", "arbitrary"), VMEM_BIG),
    )(q, k, v, cq, ck, gqk)


def _fox_bwd(q, k, v, cq, ck, gqk, do, lse, l):
    B, H, nq, tq, _ = q.shape
    nk, tk = k.shape[2], k.shape[3]
    rr = tq // tk

    def body(q_ref, k_ref, v_ref, cq_ref, ck_ref, g_ref, do_ref, lse_ref,
             dq_ref, dk_ref, dv_ref, dc_ref, wacc_ref, fk_sc, dfk_sc, dv_sc):
        @pl.when((pl.program_id(0) == 0) & (pl.program_id(1) == 0))
        def _():
            wacc_ref[...] = jnp.zeros_like(wacc_ref)

        g0 = g_ref[0:1, :]
        g1 = g_ref[1:2, :]
        dfk_sc[...] = jnp.zeros_like(dfk_sc)
        dv_sc[...] = jnp.zeros_like(dv_sc)
        dc_ref[...] = jnp.zeros_like(dc_ref)

        def kprep(kb, c):
            kn, _ = _rms_rows(k_ref[kb])
            fk_sc[kb] = (kn * g1).astype(BF16)
            return c

        lax.fori_loop(0, nk, kprep, 0)
        ti = lax.broadcasted_iota(jnp.int32, (tq, 1), 0)
        si = lax.broadcasted_iota(jnp.int32, (1, tk), 1)

        def qloop(qb, carry):
            qn, qr = _rms_rows(q_ref[qb])
            fq = (qn * g0).astype(BF16)
            cqq = cq_ref[qb]
            lse = lse_ref[qb]
            dob = do_ref[qb].astype(BF16)
            tpos = qb * tq + ti

            def probs(kb):
                s = _dot_nt(fq, fk_sc[kb]) * SCALE + cqq - ck_ref[kb]
                p = jnp.where((kb * tk + si) <= tpos, jnp.exp(s - lse), 0.0)
                return p, _dot_nt(dob, v_ref[kb])

            def dloop(kb, acc):
                p, dp = probs(kb)
                return acc + _rowsum(p * dp)

            dlt = lax.fori_loop(0, (qb + 1) * rr, dloop, jnp.zeros((tq, 1), F32))

            def kloop(kb, dfq):
                fk = fk_sc[kb]
                p, dp = probs(kb)
                ds = p * (dp - dlt)
                dsb = ds.astype(BF16)
                dv_sc[kb] += _dot_tn(p.astype(BF16), dob)
                dfk_sc[kb] += _dot_tn(dsb, fq) * SCALE
                dc_ref[kb] += jnp.broadcast_to(-_colsum(ds), (8, tk))
                return dfq + _dot(dsb, fk) * SCALE

            dfq = lax.fori_loop(0, (qb + 1) * rr, kloop, jnp.zeros((tq, HD), F32))
            wacc_ref[0:1, :] += _colsum(dfq * qn)
            dq_ref[qb] = _rms_bwd(qn, qr, dfq * g0).astype(BF16)
            return carry

        lax.fori_loop(0, nq, qloop, 0)

        def kfin(kb, c):
            kn, kr = _rms_rows(k_ref[kb])
            dfk = dfk_sc[kb]
            wacc_ref[1:2, :] += _colsum(dfk * kn)
            dk_ref[kb] = _rms_bwd(kn, kr, dfk * g1).astype(BF16)
            return c

        lax.fori_loop(0, nk, kfin, 0)
        dv_ref[...] = dv_sc[...].astype(BF16)

    qs = pl.BlockSpec((None, None, nq, tq, HD), lambda b, h: (b, h, 0, 0, 0))
    ks = pl.BlockSpec((None, None, nk, tk, HD), lambda b, h: (b, h, 0, 0, 0))
    cqs = pl.BlockSpec((None, None, nq, tq, 1), lambda b, h: (b, h, 0, 0, 0))
    cks = pl.BlockSpec((None, None, nk, 1, tk), lambda b, h: (b, h, 0, 0, 0))
    return pl.pallas_call(
        body, name=f"fox_bwd_{l}",
        grid=(B, H),
        in_specs=[qs, ks, ks, cqs, cks, pl.BlockSpec((None, 8, HD), lambda b, h: (l, 0, 0)), qs, cqs],
        out_specs=[qs, ks, ks,
                   pl.BlockSpec((None, None, nk, 8, tk), lambda b, h: (b, h, 0, 0, 0)),
                   pl.BlockSpec((8, HD), lambda b, h: (0, 0))],
        out_shape=[jax.ShapeDtypeStruct((B, H, nq, tq, HD), BF16),
                   jax.ShapeDtypeStruct((B, H, nk, tk, HD), BF16),
                   jax.ShapeDtypeStruct((B, H, nk, tk, HD), BF16),
                   jax.ShapeDtypeStruct((B, H, nk, 8, tk), F32),
                   jax.ShapeDtypeStruct((8, HD), F32)],
        scratch_shapes=[pltpu.VMEM((nk, tk, HD), BF16), pltpu.VMEM((nk, tk, HD), F32),
                        pltpu.VMEM((nk, tk, HD), F32)],
        compiler_params=_cp(("arbitrary", "arbitrary"), VMEM_BIG),
    )(q, k, v, cq, ck, gqk, do, lse)


SBQ_BLK, SBK_BLK, SBV_BLK = 8, 10, 12
FXQ_BLK, FXK_BLK, FXV_BLK = 14, 16, 18
PAIR = 2 * HD


def _lane_masks():
    lane = lax.broadcasted_iota(jnp.int32, (1, PAIR), 1)
    return lane, lane < HD


def _pair_select(m0, a0, a1):
    return jnp.where(m0, a0, a1)


def _pair_split(x, m0):
    return jnp.where(m0, x, 0.0).astype(BF16), jnp.where(m0, 0.0, x).astype(BF16)


def _pair_mean(x, m0):
    s0 = _rowsum(jnp.where(m0, x, 0.0))
    s1 = _rowsum(x) - s0
    return jnp.where(m0, s0, s1) * (1.0 / HD)


def _pair_rms(x, m0):
    rstd = lax.rsqrt(_pair_mean(x * x, m0) + EPS)
    return x * rstd, rstd


def _pair_rms_bwd(xn, rstd, dyn, m0):
    return rstd * (dyn - xn * _pair_mean(dyn * xn, m0))


def _logsig2(z):
    l1p = jnp.log(1.0 + jnp.exp(-jnp.abs(z)))
    lb = jnp.minimum(z, 0.0) - l1p
    return lb, lb - z


def _rows(ref, blk, size):
    return ref[pl.ds(pl.multiple_of(blk * size, size), size), :]


def _sbp_fwd(proj, l, S):
    T = proj.shape[0]
    tq, tk = TQ_(S), TK_(S)
    assert tq == 2 * tk
    nq = S // tq

    def body(q_ref, k_ref, v_ref, o_ref, t1_ref, kb_sc, vb_sc):
        kb_sc[...] = k_ref[...].astype(BF16)
        vb_sc[...] = v_ref[...].astype(BF16)
        lane, m0 = _lane_masks()
        tri = _tri(tk, "row_gt_col")
        ti = lax.broadcasted_iota(jnp.int32, (tq, 1), 0)
        si = lax.broadcasted_iota(jnp.int32, (1, tk), 1)

        def qloop(qb, carry):
            qh = _pair_split(_rows(q_ref, qb, tq) * SCALE, m0)
            tpos = qb * tq + ti

            def step(kbs, c, masked):
                pre = []
                for h in range(2):
                    for kb in kbs:
                        z = _dot_nt(qh[h], _rows(kb_sc, kb, tk))
                        lb, l1 = _logsig2(z)
                        past = None
                        if masked:
                            past = (kb * tk + si) < tpos
                            l1 = jnp.where(past, l1, 0.0)
                        pre.append((lb, l1, _cumsum_mm(l1, tri, parts=2), past))
                out = []
                for h in range(2):
                    acc, run = c[h]
                    for n, kb in enumerate(kbs):
                        lb, l1, cs, past = pre[2 * h + n]
                        w = jnp.exp(lb + (cs + run))
                        if masked:
                            w = jnp.where(past, w, 0.0)
                        acc = acc + _dot(w.astype(BF16), _rows(vb_sc, kb, tk))
                        run = run + (cs[:, 0:1] + l1[:, 0:1])
                    out.append((acc, run))
                return tuple(out)

            zero = (jnp.zeros((tq, PAIR), F32), jnp.zeros((tq, 1), F32))
            c = step((2 * qb + 1, 2 * qb), (zero, zero), True)
            c = lax.fori_loop(0, qb, lambda i, cc: step((2 * (qb - i) - 1, 2 * (qb - i) - 2), cc, False), c)
            r0 = pl.multiple_of(qb * tq, tq)
            o_ref[pl.ds(r0, tq), :] = _pair_select(m0, c[0][0], c[1][0])
            t1_ref[pl.ds(r0, tq), :] = jnp.where(lane == 0, c[0][1], jnp.where(lane == 1, c[1][1], 0.0))
            return carry

        lax.fori_loop(0, nq, qloop, 0)

    def col(blk):
        return pl.BlockSpec((S, PAIR), lambda b, p: (b, blk + p))

    return pl.pallas_call(
        body, name=f"sb_fwd_{l}",
        grid=(T // S, 2),
        in_specs=[col(SBQ_BLK), col(SBK_BLK), col(SBV_BLK)],
        out_specs=[col(0), col(0)],
        out_shape=[jax.ShapeDtypeStruct((T, AW), F32), jax.ShapeDtypeStruct((T, AW), F32)],
        scratch_shapes=[pltpu.VMEM((S, PAIR), BF16), pltpu.VMEM((S, PAIR), BF16)],
        compiler_params=_cp(("arbitrary", "arbitrary"), VMEM_BIG),
    )(proj, proj, proj)


def _sbp_bwd(proj, do, t1, l, S):
    T = proj.shape[0]
    tq, tk = TQ_(S), TK_(S)
    assert tq == 2 * tk
    nq = S // tq

    def body(q_ref, k_ref, v_ref, do_ref, t1_ref, dq_ref, dk_ref, dv_ref, kb_sc, vb_sc, dk_sc, dv_sc):
        kb_sc[...] = k_ref[...].astype(BF16)
        vb_sc[...] = v_ref[...].astype(BF16)
        dk_sc[...] = jnp.zeros_like(dk_sc)
        dv_sc[...] = jnp.zeros_like(dv_sc)
        _, m0 = _lane_masks()
        tri_in = _tri(tk, "row_le_col")
        tri_ex = _tri(tk, "row_lt_col")
        ti = lax.broadcasted_iota(jnp.int32, (tq, 1), 0)
        si = lax.broadcasted_iota(jnp.int32, (1, tk), 1)

        def qloop(qb, carry):
            qh = _pair_split(_rows(q_ref, qb, tq) * SCALE, m0)
            doh = _pair_split(_rows(do_ref, qb, tq), m0)
            t1v = _rows(t1_ref, qb, tq)
            tot = (t1v[:, 0:1], t1v[:, 1:2])
            tpos = qb * tq + ti

            def step(kbs, c, masked):
                pre = []
                for h in range(2):
                    for kb in kbs:
                        kk = _rows(kb_sc, kb, tk)
                        z = _dot_nt(qh[h], kk)
                        lb, l1 = _logsig2(z)
                        past = None
                        if masked:
                            past = (kb * tk + si) < tpos
                            l1 = jnp.where(past, l1, 0.0)
                        sig = jnp.exp(lb)
                        pre.append((lb, sig, _cumsum_mm(l1, tri_in), _dot_nt(doh[h], _rows(vb_sc, kb, tk)), past, kk))
                out = []
                for h in range(2):
                    dq, run1, rung = c[h]
                    for n, kb in enumerate(kbs):
                        lb, sig, p1, dw, past, kk = pre[2 * h + n]
                        w = jnp.exp(lb + (tot[h] - (run1 + p1)))
                        if masked:
                            w = jnp.where(past, w, 0.0)
                        gm = w * dw
                        cx = _cumsum_mm(gm, tri_ex, parts=2)
                        dz = gm - (gm + (rung + cx)) * sig
                        if masked:
                            dz = jnp.where(past, dz, 0.0)
                        dz = dz.astype(BF16)
                        r = pl.ds(pl.multiple_of(kb * tk, tk), tk)
                        dv_sc[r, :] += _dot_tn(w.astype(BF16), doh[h])
                        dk_sc[r, :] += _dot_tn(dz, qh[h])
                        dq = dq + _dot(dz, kk)
                        run1 = run1 + p1[:, tk - 1:tk]
                        rung = rung + (cx[:, tk - 1:tk] + gm[:, tk - 1:tk])
                    out.append((dq, run1, rung))
                return tuple(out)

            z1 = jnp.zeros((tq, 1), F32)
            zero = (jnp.zeros((tq, PAIR), F32), z1, z1)
            c = lax.fori_loop(0, qb, lambda i, cc: step((2 * i, 2 * i + 1), cc, False), (zero, zero))
            c = step((2 * qb, 2 * qb + 1), c, True)
            r0 = pl.multiple_of(qb * tq, tq)
            dq_ref[pl.ds(r0, tq), :] = (_pair_select(m0, c[0][0], c[1][0]) * SCALE).astype(BF16)
            return carry

        lax.fori_loop(0, nq, qloop, 0)
        dk_ref[...] = dk_sc[...].astype(BF16)
        dv_ref[...] = dv_sc[...].astype(BF16)

    def col(blk):
        return pl.BlockSpec((S, PAIR), lambda b, p: (b, blk + p))

    sh = jax.ShapeDtypeStruct((T, AW), BF16)
    return pl.pallas_call(
        body, name=f"sb_bwd_{l}",
        grid=(T // S, 2),
        in_specs=[col(SBQ_BLK), col(SBK_BLK), col(SBV_BLK), col(0), col(0)],
        out_specs=[col(0), col(0), col(0)],
        out_shape=[sh, sh, sh],
        scratch_shapes=[pltpu.VMEM((S, PAIR), BF16), pltpu.VMEM((S, PAIR), BF16),
                        pltpu.VMEM((S, PAIR), F32), pltpu.VMEM((S, PAIR), F32)],
        compiler_params=_cp(("arbitrary", "arbitrary"), VMEM_BIG),
    )(proj, proj, proj, do, t1)


def _foxp_fwd(proj, cum, ck, gqk2, l, S):
    T = proj.shape[0]
    tq, tk = TQ_(S), TK_(S)
    assert tq == 2 * tk
    nq, nk = S // tq, S // tk

    def body(q_ref, k_ref, v_ref, cum_ref, ck_ref, g_ref, o_ref, nl_ref, fk_sc, vb_sc):
        lane, m0 = _lane_masks()
        p = pl.program_id(1)
        kn, _ = _pair_rms(k_ref[...], m0)
        fk_sc[...] = (kn * g_ref[1:2, :]).astype(BF16)
        vb_sc[...] = v_ref[...].astype(BF16)
        ti = lax.broadcasted_iota(jnp.int32, (tq, 1), 0)
        si = lax.broadcasted_iota(jnp.int32, (1, tk), 1)

        def qloop(qb, carry):
            qn, _ = _pair_rms(_rows(q_ref, qb, tq), m0)
            fqh = _pair_split(qn * (g_ref[0:1, :] * SCALE), m0)
            cumv = _rows(cum_ref, qb, tq)
            cq = [_rowsum(jnp.where(lane == 2 * p + h, cumv, 0.0)) for h in range(2)]
            tpos = qb * tq + ti

            def step(kbs, c, masked):
                out = []
                for h in range(2):
                    m, lsum, acc = c[h]
                    ss = []
                    for kb in kbs:
                        s = _dot_nt(fqh[h], _rows(fk_sc, kb, tk)) + (cq[h] - ck_ref[h, kb])
                        if masked:
                            s = jnp.where((kb * tk + si) <= tpos, s, NEG)
                        ss.append(s)
                    m2 = jnp.maximum(m, jnp.maximum(jnp.max(ss[0], axis=1, keepdims=True),
                                                    jnp.max(ss[1], axis=1, keepdims=True)))
                    al = jnp.exp(m - m2)
                    lsum = al * lsum
                    acc = al * acc
                    for s, kb in zip(ss, kbs):
                        pr = jnp.exp(s - m2)
                        lsum = lsum + _rowsum(pr)
                        acc = acc + _dot(pr.astype(BF16), _rows(vb_sc, kb, tk))
                    out.append((m2, lsum, acc))
                return tuple(out)

            zero = (jnp.full((tq, 1), NEG, F32), jnp.zeros((tq, 1), F32), jnp.zeros((tq, PAIR), F32))
            c = lax.fori_loop(0, qb, lambda i, cc: step((2 * i, 2 * i + 1), cc, False), (zero, zero))
            c = step((2 * qb, 2 * qb + 1), c, True)
            r0 = pl.multiple_of(qb * tq, tq)
            o_ref[pl.ds(r0, tq), :] = _pair_select(m0, c[0][2] / c[0][1], c[1][2] / c[1][1])
            nl = [cq[h] - (c[h][0] + jnp.log(c[h][1])) for h in range(2)]
            nl_ref[pl.ds(r0, tq), :] = jnp.where(lane == 0, nl[0], jnp.where(lane == 1, nl[1], 0.0))
            return carry

        lax.fori_loop(0, nq, qloop, 0)

    def col(blk):
        return pl.BlockSpec((S, PAIR), lambda b, p: (b, blk + p))

    return pl.pallas_call(
        body, name=f"fox_fwd_{l}",
        grid=(T // S, 2),
        in_specs=[col(FXQ_BLK), col(FXK_BLK), col(FXV_BLK),
                  pl.BlockSpec((S, 128), lambda b, p: (b, 0)),
                  pl.BlockSpec((None, 2, nk, 1, tk), lambda b, p: (b, p, 0, 0, 0)),
                  pl.BlockSpec((None, 8, PAIR), lambda b, p: (l, 0, 0))],
        out_specs=[col(0), col(0)],
        out_shape=[jax.ShapeDtypeStruct((T, AW), F32), jax.ShapeDtypeStruct((T, AW), F32)],
        scratch_shapes=[pltpu.VMEM((S, PAIR), BF16), pltpu.VMEM((S, PAIR), BF16)],
        compiler_params=_cp(("arbitrary", "arbitrary"), VMEM_BIG),
    )(proj, proj, proj, cum, ck, gqk2)


def _foxp_bwd(proj, do, nl, ck, gqk2, l, S):
    T = proj.shape[0]
    tq, tk = TQ_(S), TK_(S)
    assert tq == 2 * tk
    nq, nk = S // tq, S // tk

    def body(q_ref, k_ref, v_ref, do_ref, nl_ref, ck_ref, g_ref,
             dq_ref, dk_ref, dv_ref, dc_ref, wacc_ref, fk_sc, vb_sc, dfk_sc, dv_sc):
        @pl.when((pl.program_id(0) == 0) & (pl.program_id(1) == 0))
        def _():
            wacc_ref[...] = jnp.zeros_like(wacc_ref)

        _, m0 = _lane_masks()
        g0 = g_ref[0:1, :]
        g1 = g_ref[1:2, :]
        kn, kr = _pair_rms(k_ref[...], m0)
        fk_sc[...] = (kn * g1).astype(BF16)
        vb_sc[...] = v_ref[...].astype(BF16)
        dfk_sc[...] = jnp.zeros_like(dfk_sc)
        dv_sc[...] = jnp.zeros_like(dv_sc)
        dc_ref[...] = jnp.zeros_like(dc_ref)
        ti = lax.broadcasted_iota(jnp.int32, (tq, 1), 0)
        si = lax.broadcasted_iota(jnp.int32, (1, tk), 1)

        def qloop(qb, carry):
            qn, qr = _pair_rms(_rows(q_ref, qb, tq), m0)
            fqh = _pair_split(qn * (g0 * SCALE), m0)
            doh = _pair_split(_rows(do_ref, qb, tq), m0)
            nlv = _rows(nl_ref, qb, tq)
            cql = (nlv[:, 0:1], nlv[:, 1:2])
            tpos = qb * tq + ti

            def probs(h, kb, masked):
                s = _dot_nt(fqh[h], _rows(fk_sc, kb, tk)) + (cql[h] - ck_ref[h, kb])
                pr = jnp.exp(s)
                if masked:
                    pr = jnp.where((kb * tk + si) <= tpos, pr, 0.0)
                return pr, _dot_nt(doh[h], _rows(vb_sc, kb, tk))

            def dstep(kbs, c, masked):
                out = []
                for h in range(2):
                    acc = c[h]
                    for kb in kbs:
                        pr, dp = probs(h, kb, masked)
                        acc = acc + _rowsum(pr * dp)
                    out.append(acc)
                return tuple(out)

            z1 = jnp.zeros((tq, 1), F32)
            dlt = lax.fori_loop(0, qb, lambda i, cc: dstep((2 * i, 2 * i + 1), cc, False), (z1, z1))
            dlt = dstep((2 * qb, 2 * qb + 1), dlt, True)

            def step(kbs, c, masked):
                out = []
                for h in range(2):
                    dfq = c[h]
                    for kb in kbs:
                        pr, dp = probs(h, kb, masked)
                        ds = pr * (dp - dlt[h])
                        dsb = ds.astype(BF16)
                        r = pl.ds(pl.multiple_of(kb * tk, tk), tk)
                        dv_sc[r, :] += _dot_tn(pr.astype(BF16), doh[h])
                        dfk_sc[r, :] += _dot_tn(dsb, fqh[h])
                        dc_ref[h, kb] += jnp.broadcast_to(-_colsum(ds), (8, tk))
                        dfq = dfq + _dot(dsb, _rows(fk_sc, kb, tk))
                    out.append(dfq)
                return tuple(out)

            zq = jnp.zeros((tq, PAIR), F32)
            c = lax.fori_loop(0, qb, lambda i, cc: step((2 * i, 2 * i + 1), cc, False), (zq, zq))
            c = step((2 * qb, 2 * qb + 1), c, True)
            dfq = _pair_select(m0, c[0], c[1]) * SCALE
            wacc_ref[0:1, :] += _colsum(dfq * qn)
            r0 = pl.multiple_of(qb * tq, tq)
            dq_ref[pl.ds(r0, tq), :] = _pair_rms_bwd(qn, qr, dfq * g0, m0).astype(BF16)
            return carry

        lax.fori_loop(0, nq, qloop, 0)
        dfk = dfk_sc[...]
        wacc_ref[1:2, :] += _colsum(dfk * kn)
        dk_ref[...] = _pair_rms_bwd(kn, kr, dfk * g1, m0).astype(BF16)
        dv_ref[...] = dv_sc[...].astype(BF16)

    def col(blk):
        return pl.BlockSpec((S, PAIR), lambda b, p: (b, blk + p))

    sh = jax.ShapeDtypeStruct((T, AW), BF16)
    return pl.pallas_call(
        body, name=f"fox_bwd_{l}",
        grid=(T // S, 2),
        in_specs=[col(FXQ_BLK), col(FXK_BLK), col(FXV_BLK), col(0), col(0),
                  pl.BlockSpec((None, 2, nk, 1, tk), lambda b, p: (b, p, 0, 0, 0)),
                  pl.BlockSpec((None, 8, PAIR), lambda b, p: (l, 0, 0))],
        out_specs=[col(0), col(0), col(0),
                   pl.BlockSpec((None, 2, nk, 8, tk), lambda b, p: (b, p, 0, 0, 0)),
                   pl.BlockSpec((8, PAIR), lambda b, p: (0, 0))],
        out_shape=[sh, sh, sh,
                   jax.ShapeDtypeStruct((T // S, NH, nk, 8, tk), F32),
                   jax.ShapeDtypeStruct((8, PAIR), F32)],
        scratch_shapes=[pltpu.VMEM((S, PAIR), BF16), pltpu.VMEM((S, PAIR), BF16),
                        pltpu.VMEM((S, PAIR), F32), pltpu.VMEM((S, PAIR), F32)],
        compiler_params=_cp(("arbitrary", "arbitrary"), VMEM_BIG),
    )(proj, proj, proj, do, nl, ck, gqk2)


def _transpose_blocks(src_ref, dst_sc, nblk, blk):
    for kb in range(nblk):
        dst_sc[kb] = src_ref[kb * blk:(kb + 1) * blk, :].astype(F32).T.astype(BF16)


def _sbq_fwd(proj, l, S, comm=None):
    T = proj.shape[0]
    tb = TQ_(S)
    nb = S // tb
    nbat = T // S
    c_args, c_specs, c_outs, c_scr = _hosted(comm)
    n_ci, n_co = len(c_args), len(c_outs)

    def body(*refs):
        q_ref, k_ref, v_ref = refs[:3]
        c_in = refs[3:3 + n_ci]
        o_ref, t1_ref = refs[3 + n_ci:5 + n_ci]
        c_out = refs[5 + n_ci:5 + n_ci + n_co]
        kt_sc, vb_sc = refs[5 + n_ci + n_co:7 + n_ci + n_co]
        c_sems = refs[7 + n_ci + n_co:]
        step = pl.program_id(0) * 2 + pl.program_id(1)
        if comm is not None:
            @pl.when(step == 0)
            def _():
                comm["start"](c_in, c_out, c_sems)

        _transpose_blocks(k_ref, kt_sc, nb, tb)
        vb_sc[...] = v_ref[...].astype(BF16)
        lane, m0 = _lane_masks()
        tri = _tri(tb, "row_gt_col")
        past = lax.broadcasted_iota(jnp.int32, (tb, tb), 1) < lax.broadcasted_iota(jnp.int32, (tb, tb), 0)

        def qloop(qb, carry):
            qh = _pair_split(_rows(q_ref, qb, tb) * SCALE, m0)

            def scores(kb):
                return tuple(_dot(qh[h], kt_sc[kb]) for h in range(2))

            def block(kb, kb_next, z, c, masked):
                mid = []
                for h in range(2):
                    lb, l1 = _logsig2(z[h])
                    if masked:
                        l1 = jnp.where(past, l1, 0.0)
                    mid.append((lb, l1, _cumsum_mm(l1, tri, parts=2)))
                z_next = scores(kb_next)
                pv, runs = [], []
                for h in range(2):
                    lb, l1, cs = mid[h]
                    w = jnp.exp(lb + (cs + c[h][1]))
                    if masked:
                        w = jnp.where(past, w, 0.0)
                    pv.append(_dot(w.astype(BF16), _rows(vb_sc, kb, tb)))
                    runs.append(c[h][1] + (cs[:, 0:1] + l1[:, 0:1]))
                return z_next, tuple((c[h][0] + pv[h], runs[h]) for h in range(2))

            zero = (jnp.zeros((tb, PAIR), F32), jnp.zeros((tb, 1), F32))
            z, c = block(qb, jnp.maximum(qb - 1, 0), scores(qb), (zero, zero), True)

            def off_diag(i, zc):
                kb = qb - 1 - i
                return block(kb, jnp.maximum(kb - 1, 0), zc[0], zc[1], False)

            _, c = lax.fori_loop(0, qb, off_diag, (z, c))
            r0 = pl.multiple_of(qb * tb, tb)
            o_ref[pl.ds(r0, tb), :] = _pair_select(m0, c[0][0], c[1][0])
            t1_ref[pl.ds(r0, tb), :] = jnp.where(lane == 0, c[0][1], jnp.where(lane == 1, c[1][1], 0.0))
            return carry

        lax.fori_loop(0, nb, qloop, 0)
        if comm is not None:
            @pl.when(step == 2 * nbat - 1)
            def _():
                comm["finish"](c_in, c_out, c_sems)

    def col(blk):
        return pl.BlockSpec((S, PAIR), lambda b, p: (b, blk + p))

    anyspec = pl.BlockSpec(memory_space=pl.ANY)
    out = pl.pallas_call(
        body, name=f"sb_fwd_{l}",
        grid=(nbat, 2),
        in_specs=[col(SBQ_BLK), col(SBK_BLK), col(SBV_BLK)] + c_specs,
        out_specs=[col(0), col(0)] + [anyspec] * n_co,
        out_shape=[jax.ShapeDtypeStruct((T, AW), F32), jax.ShapeDtypeStruct((T, AW), F32)] + c_outs,
        scratch_shapes=[pltpu.VMEM((nb, PAIR, tb), BF16), pltpu.VMEM((S, PAIR), BF16)] + c_scr,
        compiler_params=_cp(("arbitrary", "arbitrary"), VMEM_BIG),
    )(proj, proj, proj, *c_args)
    return out[0], out[1], list(out[2:])


def _sbq_bwd(proj, do, t1, l, S, comm=None):
    T = proj.shape[0]
    tb = TQ_(S)
    nb = S // tb
    nbat = T // S
    c_args, c_specs, c_outs, c_scr = _hosted(comm)
    n_ci, n_co = len(c_args), len(c_outs)

    def body(*refs):
        q_ref, k_ref, v_ref, do_ref, t1_ref = refs[:5]
        c_in = refs[5:5 + n_ci]
        dq_ref, dk_ref, dv_ref = refs[5 + n_ci:8 + n_ci]
        c_out = refs[8 + n_ci:8 + n_ci + n_co]
        kb_sc, kt_sc, vt_sc, dkt_sc, dvt_sc = refs[8 + n_ci + n_co:13 + n_ci + n_co]
        c_sems = refs[13 + n_ci + n_co:]
        step = pl.program_id(0) * 2 + pl.program_id(1)
        if comm is not None:
            @pl.when(step == 0)
            def _():
                comm["start"](c_in, c_out, c_sems)

        kb_sc[...] = k_ref[...].astype(BF16)
        _transpose_blocks(k_ref, kt_sc, nb, tb)
        _transpose_blocks(v_ref, vt_sc, nb, tb)
        dkt_sc[...] = jnp.zeros_like(dkt_sc)
        dvt_sc[...] = jnp.zeros_like(dvt_sc)
        _, m0 = _lane_masks()
        mt0 = lax.broadcasted_iota(jnp.int32, (PAIR, 1), 0) < HD
        tri_in = _tri(tb, "row_le_col")
        tri_ex = _tri(tb, "row_lt_col")
        past = lax.broadcasted_iota(jnp.int32, (tb, tb), 1) < lax.broadcasted_iota(jnp.int32, (tb, tb), 0)

        def qloop(qb, carry):
            qf = _rows(q_ref, qb, tb) * SCALE
            dof = _rows(do_ref, qb, tb)
            qh = _pair_split(qf, m0)
            doh = _pair_split(dof, m0)
            qth = _pair_split(qf.T, mt0)
            doth = _pair_split(dof.T, mt0)
            t1v = _rows(t1_ref, qb, tb)
            tot = (t1v[:, 0:1], t1v[:, 1:2])

            def block(kb, c, masked):
                hs = range(2)
                z = [_dot(qh[h], kt_sc[kb]) for h in hs]
                dw = [_dot(doh[h], vt_sc[kb]) for h in hs]
                st = []
                for h in hs:
                    lb, l1 = _logsig2(z[h])
                    if masked:
                        l1 = jnp.where(past, l1, 0.0)
                    st.append((lb, _cumsum_mm(l1, tri_in, parts=2)))
                mid = []
                for h in hs:
                    lb, p1 = st[h]
                    w = jnp.exp(lb + (tot[h] - (c[h][1] + p1)))
                    if masked:
                        w = jnp.where(past, w, 0.0)
                    gm = w * dw[h]
                    mid.append((w.astype(BF16), gm, _cumsum_mm(gm, tri_ex, parts=1)))
                out = []
                for h in hs:
                    dq, run1, rung = c[h]
                    wb, gm, cx = mid[h]
                    dz = gm - (gm + (rung + cx)) * jnp.exp(st[h][0])
                    if masked:
                        dz = jnp.where(past, dz, 0.0)
                    dz = dz.astype(BF16)
                    dvt_sc[kb] += _dot(doth[h], wb)
                    dkt_sc[kb] += _dot(qth[h], dz)
                    dq = dq + _dot(dz, _rows(kb_sc, kb, tb))
                    p1 = st[h][1]
                    out.append((dq, run1 + p1[:, tb - 1:tb], rung + (cx[:, tb - 1:tb] + gm[:, tb - 1:tb])))
                return tuple(out)

            z1 = jnp.zeros((tb, 1), F32)
            zero = (jnp.zeros((tb, PAIR), F32), z1, z1)
            c = lax.fori_loop(0, qb, lambda i, cc: block(i, cc, False), (zero, zero))
            c = block(qb, c, True)
            r0 = pl.multiple_of(qb * tb, tb)
            dq_ref[pl.ds(r0, tb), :] = (_pair_select(m0, c[0][0], c[1][0]) * SCALE).astype(BF16)
            return carry

        lax.fori_loop(0, nb, qloop, 0)
        for kb in range(nb):
            dk_ref[kb * tb:(kb + 1) * tb, :] = dkt_sc[kb].T.astype(BF16)
            dv_ref[kb * tb:(kb + 1) * tb, :] = dvt_sc[kb].T.astype(BF16)
        if comm is not None:
            @pl.when(step == 2 * nbat - 1)
            def _():
                comm["finish"](c_in, c_out, c_sems)

    def col(blk):
        return pl.BlockSpec((S, PAIR), lambda b, p: (b, blk + p))

    sh = jax.ShapeDtypeStruct((T, AW), BF16)
    anyspec = pl.BlockSpec(memory_space=pl.ANY)
    out = pl.pallas_call(
        body, name=f"sb_bwd_{l}",
        grid=(nbat, 2),
        in_specs=[col(SBQ_BLK), col(SBK_BLK), col(SBV_BLK), col(0), col(0)] + c_specs,
        out_specs=[col(0), col(0), col(0)] + [anyspec] * n_co,
        out_shape=[sh, sh, sh] + c_outs,
        scratch_shapes=[pltpu.VMEM((S, PAIR), BF16), pltpu.VMEM((nb, PAIR, tb), BF16), pltpu.VMEM((nb, PAIR, tb), BF16),
                        pltpu.VMEM((nb, PAIR, tb), F32), pltpu.VMEM((nb, PAIR, tb), F32)] + c_scr,
        compiler_params=_cp(("arbitrary", "arbitrary"), VMEM_BIG),
    )(proj, proj, proj, do, t1, *c_args)
    return out[0], out[1], out[2], list(out[3:])


def _foxq_fwd(proj, cum, ck, gqk2, l, S):
    T = proj.shape[0]
    tb = TQ_(S)
    nb = S // tb

    def body(q_ref, k_ref, v_ref, cum_ref, ck_ref, g_ref, o_ref, nl_ref, ox_ref, fk_sc, fkt_sc, vb_sc):
        lane, m0 = _lane_masks()
        p = pl.program_id(1)
        kn, _ = _pair_rms(k_ref[...], m0)
        fk_sc[...] = kn * g_ref[1:2, :]
        _transpose_blocks(fk_sc, fkt_sc, nb, tb)
        vb_sc[...] = v_ref[...].astype(BF16)
        causal = lax.broadcasted_iota(jnp.int32, (tb, tb), 1) <= lax.broadcasted_iota(jnp.int32, (tb, tb), 0)

        def qloop(qb, carry):
            qn, _ = _pair_rms(_rows(q_ref, qb, tb), m0)
            fqh = _pair_split(qn * (g_ref[0:1, :] * SCALE), m0)
            cumv = _rows(cum_ref, qb, tb)
            cq = [_rowsum(jnp.where(lane == 2 * p + h, cumv, 0.0)) for h in range(2)]

            def scores(kb):
                return tuple(_dot(fqh[h], fkt_sc[kb]) for h in range(2))

            def block(kb, kb_next, qk, c, masked):
                st = []
                for h in range(2):
                    s = qk[h] + (cq[h] - ck_ref[h, kb])
                    if masked:
                        s = jnp.where(causal, s, NEG)
                    m2 = jnp.maximum(c[h][0], jnp.max(s, axis=1, keepdims=True))
                    pr = jnp.exp(s - m2)
                    hi = pr.astype(BF16)
                    lo = (pr - hi.astype(F32)).astype(BF16)
                    vv = _rows(vb_sc, kb, tb)
                    st.append((m2, pr, _dot(hi, vv), _dot(lo, vv)))
                qk_next = scores(kb_next)
                out = []
                for h in range(2):
                    m, lsum, acc, rest = c[h]
                    m2, pr, pv, pv_lo = st[h]
                    al = jnp.exp(m - m2)
                    out.append((m2, al * lsum + _rowsum(pr), al * acc + pv, al * rest + pv_lo))
                return qk_next, tuple(out)

            zacc = jnp.zeros((tb, PAIR), F32)
            zero = (jnp.full((tb, 1), NEG, F32), jnp.zeros((tb, 1), F32), zacc, zacc)

            def off_diag(i, sc):
                return block(i, i + 1, sc[0], sc[1], False)

            qk, c = lax.fori_loop(0, qb, off_diag, (scores(0), (zero, zero)))
            _, c = block(qb, qb, qk, c, True)
            r0 = pl.multiple_of(qb * tb, tb)
            o_ref[pl.ds(r0, tb), :] = _pair_select(m0, c[0][2] / c[0][1], c[1][2] / c[1][1])
            ox_ref[pl.ds(r0, tb), :] = _pair_select(m0, (c[0][2] + c[0][3]) / c[0][1], (c[1][2] + c[1][3]) / c[1][1])
            nl = [cq[h] - (c[h][0] + jnp.log(c[h][1])) for h in range(2)]
            nl_ref[pl.ds(r0, tb), :] = jnp.where(lane == 0, nl[0], jnp.where(lane == 1, nl[1], 0.0))
            return carry

        lax.fori_loop(0, nb, qloop, 0)

    def col(blk):
        return pl.BlockSpec((S, PAIR), lambda b, p: (b, blk + p))

    return pl.pallas_call(
        body, name=f"fox_fwd_{l}",
        grid=(T // S, 2),
        in_specs=[col(FXQ_BLK), col(FXK_BLK), col(FXV_BLK),
                  pl.BlockSpec((S, 128), lambda b, p: (b, 0)),
                  pl.BlockSpec((None, 2, nb, 1, tb), lambda b, p: (b, p, 0, 0, 0)),
                  pl.BlockSpec((None, 8, PAIR), lambda b, p: (l, 0, 0))],
        out_specs=[col(0), col(0), col(0)],
        out_shape=[jax.ShapeDtypeStruct((T, AW), F32)] * 3,
        scratch_shapes=[pltpu.VMEM((S, PAIR), F32), pltpu.VMEM((nb, PAIR, tb), BF16), pltpu.VMEM((S, PAIR), BF16)],
        compiler_params=_cp(("arbitrary", "arbitrary"), VMEM_BIG),
    )(proj, proj, proj, cum, ck, gqk2)


def _foxq_bwd(proj, do, nl, ox, ck, gqk2, l, S):
    T = proj.shape[0]
    tb = TQ_(S)
    nb = S // tb

    def body(q_ref, k_ref, v_ref, do_ref, nl_ref, ox_ref, ck_ref, g_ref,
             dq_ref, dk_ref, dv_ref, dc_ref, wacc_ref, fk_sc, fkt_sc, vt_sc, dfkt_sc, dvt_sc):
        @pl.when((pl.program_id(0) == 0) & (pl.program_id(1) == 0))
        def _():
            wacc_ref[...] = jnp.zeros_like(wacc_ref)

        _, m0 = _lane_masks()
        mt0 = lax.broadcasted_iota(jnp.int32, (PAIR, 1), 0) < HD
        g0 = g_ref[0:1, :]
        g1 = g_ref[1:2, :]
        fk_sc[...] = (_pair_rms(k_ref[...], m0)[0] * g1).astype(BF16)
        _transpose_blocks(fk_sc, fkt_sc, nb, tb)
        _transpose_blocks(v_ref, vt_sc, nb, tb)
        dfkt_sc[...] = jnp.zeros_like(dfkt_sc)
        dvt_sc[...] = jnp.zeros_like(dvt_sc)
        dc_ref[...] = jnp.zeros_like(dc_ref)
        causal = lax.broadcasted_iota(jnp.int32, (tb, tb), 1) <= lax.broadcasted_iota(jnp.int32, (tb, tb), 0)

        def qloop(qb, carry):
            qn, qr = _pair_rms(_rows(q_ref, qb, tb), m0)
            fqf = qn * (g0 * SCALE)
            dof = _rows(do_ref, qb, tb)
            fqh = _pair_split(fqf, m0)
            doh = _pair_split(dof, m0)
            fqth = _pair_split(fqf.T, mt0)
            doth = _pair_split(dof.T, mt0)
            nlv = _rows(nl_ref, qb, tb)
            cql = (nlv[:, 0:1], nlv[:, 1:2])

            def probs(kb, masked):
                qk = [_dot(fqh[h], fkt_sc[kb]) for h in range(2)]
                dp = [_dot(doh[h], vt_sc[kb]) for h in range(2)]
                pr = []
                for h in range(2):
                    e = jnp.exp(qk[h] + (cql[h] - ck_ref[h, kb]))
                    pr.append(jnp.where(causal, e, 0.0) if masked else e)
                return pr, dp

            oxv = _rows(ox_ref, qb, tb)
            dlt = [_rowsum(doh[h].astype(F32) * oxv) for h in range(2)]

            def block(kb, c, masked):
                pr, dp = probs(kb, masked)
                out = []
                for h in range(2):
                    ds = pr[h] * (dp[h] - dlt[h])
                    dsb = ds.astype(BF16)
                    dvt_sc[kb] += _dot(doth[h], pr[h].astype(BF16))
                    dfkt_sc[kb] += _dot(fqth[h], dsb)
                    dc_ref[h, kb] += jnp.broadcast_to(-_colsum(ds), (8, tb))
                    out.append(c[h] + _dot(dsb, _rows(fk_sc, kb, tb)))
                return tuple(out)

            zq = jnp.zeros((tb, PAIR), F32)
            c = lax.fori_loop(0, qb, lambda i, cc: block(i, cc, False), (zq, zq))
            c = block(qb, c, True)
            dfq = _pair_select(m0, c[0], c[1]) * SCALE
            wacc_ref[0:1, :] += _colsum(dfq * qn)
            r0 = pl.multiple_of(qb * tb, tb)
            dq_ref[pl.ds(r0, tb), :] = _pair_rms_bwd(qn, qr, dfq * g0, m0).astype(BF16)
            return carry

        lax.fori_loop(0, nb, qloop, 0)
        for kb in range(nb):
            rows = slice(kb * tb, (kb + 1) * tb)
            dfk = dfkt_sc[kb].T
            knb, krb = _pair_rms(k_ref[rows, :], m0)
            wacc_ref[1:2, :] += _colsum(dfk * knb)
            dk_ref[rows, :] = _pair_rms_bwd(knb, krb, dfk * g1, m0).astype(BF16)
            dv_ref[rows, :] = dvt_sc[kb].T.astype(BF16)

    def col(blk):
        return pl.BlockSpec((S, PAIR), lambda b, p: (b, blk + p))

    sh = jax.ShapeDtypeStruct((T, AW), BF16)
    return pl.pallas_call(
        body, name=f"fox_bwd_{l}",
        grid=(T // S, 2),
        in_specs=[col(FXQ_BLK), col(FXK_BLK), col(FXV_BLK), col(0), col(0), col(0),
                  pl.BlockSpec((None, 2, nb, 1, tb), lambda b, p: (b, p, 0, 0, 0)),
                  pl.BlockSpec((None, 8, PAIR), lambda b, p: (l, 0, 0))],
        out_specs=[col(0), col(0), col(0),
                   pl.BlockSpec((None, 2, nb, 8, tb), lambda b, p: (b, p, 0, 0, 0)),
                   pl.BlockSpec((8, PAIR), lambda b, p: (0, 0))],
        out_shape=[sh, sh, sh,
                   jax.ShapeDtypeStruct((T // S, NH, nb, 8, tb), F32),
                   jax.ShapeDtypeStruct((8, PAIR), F32)],
        scratch_shapes=[pltpu.VMEM((S, PAIR), BF16), pltpu.VMEM((nb, PAIR, tb), BF16), pltpu.VMEM((nb, PAIR, tb), BF16),
                        pltpu.VMEM((nb, PAIR, tb), F32), pltpu.VMEM((nb, PAIR, tb), F32)],
        compiler_params=_cp(("arbitrary", "arbitrary"), VMEM_BIG),
    )(proj, proj, proj, do, nl, ox, ck, gqk2)


def _ada_fwd(c_all, w_ada, b_cols):
    nb, ncol = c_all.shape[0], w_ada.shape[2]
    tn = _tile(ncol, 768)

    def body(c_ref, w_ref, b_ref, o_ref):
        c = c_ref[...]
        ca = (c * _sigmoid(c)).astype(BF16)
        o_ref[...] = _dot(ca, w_ref[...].astype(BF16)) + b_ref[...]

    return pl.pallas_call(
        body, name="ada_fwd",
        grid=(2, ncol // tn),
        in_specs=[pl.BlockSpec((nb, D), lambda l, n: (0, 0)),
                  pl.BlockSpec((None, D, tn), lambda l, n: (l, 0, n)),
                  pl.BlockSpec((None, 1, tn), lambda l, n: (l, 0, n))],
        out_specs=pl.BlockSpec((None, nb, tn), lambda l, n: (l, 0, n)),
        out_shape=jax.ShapeDtypeStruct((2, nb, ncol), F32),
        compiler_params=_cp(("arbitrary", "arbitrary")),
    )(c_all, w_ada, b_cols)


def _ada_bwd(c_all, dmod_cols):
    nb, ncol = c_all.shape[0], dmod_cols.shape[2]
    tn = _tile(ncol, 768)

    def body(c_ref, d_ref, o_ref):
        c = c_ref[...]
        ca = (c * _sigmoid(c)).astype(BF16)
        o_ref[...] = _dot_tn(ca, d_ref[...].astype(BF16))

    return pl.pallas_call(
        body, name="ada_bwd",
        grid=(2, ncol // tn),
        in_specs=[pl.BlockSpec((nb, D), lambda l, n: (0, 0)),
                  pl.BlockSpec((None, nb, tn), lambda l, n: (l, 0, n))],
        out_specs=pl.BlockSpec((None, D, tn), lambda l, n: (l, 0, n)),
        out_shape=jax.ShapeDtypeStruct((2, D, ncol), F32),
        compiler_params=_cp(("arbitrary", "arbitrary")),
    )(c_all, dmod_cols)


def _sum_lead(a, name):
    n, R, C = a.shape
    tr = _tile_div8(R, 256)

    def body(a_ref, o_ref):
        acc = a_ref[0]
        for i in range(1, n):
            acc = acc + a_ref[i]
        o_ref[...] = acc

    return pl.pallas_call(
        body, name=name,
        grid=(R // tr,),
        in_specs=[pl.BlockSpec((n, tr, C), lambda i: (0, i, 0))],
        out_specs=pl.BlockSpec((tr, C), lambda i: (i, 0)),
        out_shape=jax.ShapeDtypeStruct((R, C), F32),
        compiler_params=_cp(("arbitrary",)),
    )(a)


def _adamw(w, g, m, v, name):
    R, C = w.shape
    tr = _tile_div8(R, max(8, (1 << 18) // C))
    c1 = 1.0 / (1.0 - ADAM_B1 ** ADAM_STEP)
    c2 = 1.0 / (1.0 - ADAM_B2 ** ADAM_STEP)

    def body(w_ref, g_ref, m_ref, v_ref, d_ref, mo_ref, vo_ref):
        gg = g_ref[...]
        mn = ADAM_B1 * m_ref[...] + (1.0 - ADAM_B1) * gg
        vn = ADAM_B2 * v_ref[...] + (1.0 - ADAM_B2) * (gg * gg)
        mo_ref[...] = mn
        vo_ref[...] = vn
        d_ref[...] = (-ADAM_LR) * ((mn * c1) / (jnp.sqrt(vn * c2) + ADAM_EPS) + ADAM_WD * w_ref[...])

    spec = pl.BlockSpec((tr, C), lambda i: (i, 0))
    sh = jax.ShapeDtypeStruct((R, C), F32)
    return pl.pallas_call(
        body, name=name, grid=(R // tr,),
        in_specs=[spec] * 4, out_specs=[spec] * 3, out_shape=[sh] * 3,
        compiler_params=_cp(("arbitrary",)),
    )(w, g, m, v)


def _coords():
    return lax.axis_index("x"), lax.axis_index("y"), lax.axis_index("c")


def _all_gather8(blk, name, vmem):
    m_per, n = blk.shape
    space = pltpu.VMEM if vmem else pl.ANY

    def body(x_ref, out_ref, send_sems, recv_sems, local_sem):
        x, y, c = _coords()
        me, sibling = (x, y, c), (x, y, 1 - c)
        chips = [(1 - x, y), (x, 1 - y), (1 - x, 1 - y)]

        def rows(px, py, pc):
            return out_ref.at[4 * px + 2 * py + pc]

        def copy(k, block, to, src=None):
            return pltpu.make_async_remote_copy(
                src_ref=rows(*block) if src is None else src, dst_ref=rows(*block),
                send_sem=send_sems.at[k], recv_sem=recv_sems.at[k], device_id=to, device_id_type=MESH)

        mine = pltpu.make_async_copy(x_ref, rows(*me), local_sem)
        mine.start()
        first = [copy(0, me, sibling, src=x_ref)]
        first += [copy(1 + j, me, (*chip, c), src=x_ref) for j, chip in enumerate(chips)]
        for cp in first:
            cp.start()
        passed = [copy(4 + j, (*chip, c), sibling) for j, chip in enumerate(chips)]
        for j, chip in enumerate(chips):
            copy(1 + j, (*chip, c), me).wait_recv()
            passed[j].start()
        copy(0, sibling, me).wait_recv()
        for j, chip in enumerate(chips):
            copy(4 + j, (*chip, 1 - c), me).wait_recv()
        for cp in first + passed:
            cp.wait_send()
        mine.wait()

    return pl.pallas_call(
        body, name=name,
        out_shape=jax.ShapeDtypeStruct((N_DEV, m_per, n), blk.dtype),
        in_specs=[pl.BlockSpec(memory_space=space)],
        out_specs=pl.BlockSpec(memory_space=space),
        scratch_shapes=[pltpu.SemaphoreType.DMA((7,)), pltpu.SemaphoreType.DMA((7,)), pltpu.SemaphoreType.DMA],
        compiler_params=pltpu.CompilerParams(vmem_limit_bytes=VMEM_BIG if vmem else None),
    )(blk)


def _run_comm(comm, name):
    n_in, n_out = len(comm["args"]), len(comm["out_shapes"])

    def body(*refs):
        parts = (refs[:n_in], refs[n_in:n_in + n_out], refs[n_in + n_out:])
        comm["start"](*parts)
        comm["finish"](*parts)

    anyspec = pl.BlockSpec(memory_space=pl.ANY)
    return pl.pallas_call(
        body, name=name, out_shape=comm["out_shapes"],
        in_specs=[anyspec] * n_in, out_specs=[anyspec] * n_out, scratch_shapes=comm["scratch"],
    )(*comm["args"])


def _hosted(comm):
    if comm is None:
        return [], [], [], []
    anyspec = pl.BlockSpec(memory_space=pl.ANY)
    return list(comm["args"]), [anyspec] * len(comm["args"]), list(comm["out_shapes"]), list(comm["scratch"])


def _ag_layer_comm(up_b, dn_b, in_b, out_b, l):
    dn_rows = dn_b.shape[2]
    in_half, out_half = in_b.shape[1] // 2, out_b.shape[1] // 2
    n_piece = 4

    def plan(ins, outs, sems):
        up_ref, dn_ref, in_ref, out_ref = ins
        gup_ref, gdn_ref, gin_ref, gout_ref = outs
        send_sems, recv_sems, local_sems = sems
        x, y, c = _coords()
        me, sibling = (x, y, c), (x, y, 1 - c)
        chips = [(1 - x, y), (x, 1 - y), (1 - x, 1 - y)]

        def dsts(px, py, pc):
            s = 2 * px + py
            return [gup_ref.at[s, pc], gdn_ref.at[pc, pl.ds(s * dn_rows, dn_rows)],
                    gin_ref.at[s, pl.ds(pc * in_half, in_half)],
                    gout_ref.at[pl.ds(s * 2 * out_half + pc * out_half, out_half)]]

        srcs = [up_ref.at[l, c], dn_ref.at[l, c], in_ref.at[l, pl.ds(c * in_half, in_half)],
                out_ref.at[l, pl.ds(c * out_half, out_half)]]

        def copies(k, block, to, own=False):
            d = dsts(*block)
            return [pltpu.make_async_remote_copy(
                src_ref=srcs[p] if own else d[p], dst_ref=d[p], send_sem=send_sems.at[k, p],
                recv_sem=recv_sems.at[k, p], device_id=to, device_id_type=MESH) for p in range(n_piece)]

        mine = [pltpu.make_async_copy(srcs[p], d, local_sems.at[p]) for p, d in enumerate(dsts(*me))]
        first = copies(0, me, sibling, own=True)
        for j, chip in enumerate(chips):
            first += copies(1 + j, me, (*chip, c), own=True)
        return me, sibling, chips, c, copies, mine, first

    def start(ins, outs, sems):
        *_, mine, first = plan(ins, outs, sems)
        for cp in mine + first:
            cp.start()

    def finish(ins, outs, sems):
        me, sibling, chips, c, copies, mine, first = plan(ins, outs, sems)
        passed = []
        for j, chip in enumerate(chips):
            for cp in copies(1 + j, (*chip, c), me):
                cp.wait_recv()
            fwd = copies(4 + j, (*chip, c), sibling)
            for cp in fwd:
                cp.start()
            passed += fwd
        for cp in copies(0, sibling, me):
            cp.wait_recv()
        for j, chip in enumerate(chips):
            for cp in copies(4 + j, (*chip, 1 - c), me):
                cp.wait_recv()
        for cp in first + passed:
            cp.wait_send()
        for cp in mine:
            cp.wait()

    return dict(
        args=[up_b, dn_b, in_b, out_b],
        out_shapes=[jax.ShapeDtypeStruct((N_SHARD,) + up_b.shape[1:], BF16),
                    jax.ShapeDtypeStruct((dn_b.shape[1], N_SHARD * dn_rows, D), BF16),
                    jax.ShapeDtypeStruct((N_SHARD,) + in_b.shape[1:], BF16),
                    jax.ShapeDtypeStruct((N_SHARD * out_b.shape[1], D), BF16)],
        scratch=[pltpu.SemaphoreType.DMA((7, n_piece)), pltpu.SemaphoreType.DMA((7, n_piece)),
                 pltpu.SemaphoreType.DMA((n_piece,))],
        start=start, finish=finish)


def _rs_to_chips_comm(hs):
    n = len(hs)

    def copies(h, r, sems):
        send_sems, recv_sems = sems
        x, y, c = _coords()
        chips = [(1 - x, y), (x, 1 - y), (1 - x, 1 - y)]
        return [pltpu.make_async_remote_copy(
            src_ref=h[p].at[2 * px + py], dst_ref=r[p].at[k], send_sem=send_sems.at[k, p], recv_sem=recv_sems.at[k, p],
            device_id=(px, py, c), device_id_type=MESH) for k, (px, py) in enumerate(chips) for p in range(n)]

    def start(h, r, sems):
        for cp in copies(h, r, sems):
            cp.start()

    def finish(h, r, sems):
        for cp in copies(h, r, sems):
            cp.wait()

    return dict(args=list(hs), out_shapes=[jax.ShapeDtypeStruct((3,) + h.shape[1:], h.dtype) for h in hs],
                scratch=[pltpu.SemaphoreType.DMA((3, n)), pltpu.SemaphoreType.DMA((3, n))],
                start=start, finish=finish)


def _rs_to_sibling(pieces, name):
    n = len(pieces)

    def body(*refs):
        g, r, (send_sems, recv_sems) = refs[:n], refs[n:2 * n], refs[2 * n:]
        x, y, c = _coords()
        cps = []
        for p in range(n):
            r2 = g[p].shape[1] // 2
            cps.append(pltpu.make_async_remote_copy(
                src_ref=g[p].at[:, pl.ds((1 - c) * r2, r2)], dst_ref=r[p], send_sem=send_sems.at[p],
                recv_sem=recv_sems.at[p], device_id=(x, y, 1 - c), device_id_type=MESH))
        for cp in cps:
            cp.start()
        for cp in cps:
            cp.wait()

    anyspec = pl.BlockSpec(memory_space=pl.ANY)
    return pl.pallas_call(
        body, name=name,
        out_shape=[jax.ShapeDtypeStruct((N_SHARD, g.shape[1] // 2, g.shape[2]), g.dtype) for g in pieces],
        in_specs=[anyspec] * n, out_specs=[anyspec] * n,
        scratch_shapes=[pltpu.SemaphoreType.DMA((n,)), pltpu.SemaphoreType.DMA((n,))],
    )(*pieces)


def _share_halves(tensors, places, r2s):
    n, no = len(places), len(tensors)

    def body(*refs):
        o, (send_sems, recv_sems) = refs[no:2 * no], refs[2 * no:]
        x, y, c = _coords()

        def half(p, hc):
            oi, lead = places[p]
            return o[oi].at[(*lead, pl.ds(hc * r2s[p], r2s[p]))]

        outs = [pltpu.make_async_remote_copy(
            src_ref=half(p, c), dst_ref=half(p, c), send_sem=send_sems.at[p], recv_sem=recv_sems.at[p],
            device_id=(x, y, 1 - c), device_id_type=MESH) for p in range(n)]
        for cp in outs:
            cp.start()
        for p in range(n):
            pltpu.make_async_remote_copy(
                src_ref=half(p, 1 - c), dst_ref=half(p, 1 - c), send_sem=send_sems.at[p], recv_sem=recv_sems.at[p],
                device_id=(x, y, 1 - c), device_id_type=MESH).wait_recv()
        for cp in outs:
            cp.wait_send()

    anyspec = pl.BlockSpec(memory_space=pl.ANY)
    return pl.pallas_call(
        body, name="share_halves",
        out_shape=[jax.ShapeDtypeStruct(t.shape, t.dtype) for t in tensors],
        in_specs=[anyspec] * no, out_specs=[anyspec] * no,
        input_output_aliases={i: i for i in range(no)},
        scratch_shapes=[pltpu.SemaphoreType.DMA((n,)), pltpu.SemaphoreType.DMA((n,))],
    )(*tensors)


def _add_rows(r2, cols, n_arrays):
    lanes = -(-cols // 128) * 128
    return _tile_div8(r2, max(16, (24 << 20) // (2 * n_arrays * lanes * 4)), mult=16)


def _add_sibling(pieces, recvs, cidx, name):
    n = len(pieces)
    _, R, C = pieces[0].shape
    r2 = R // 2
    tr = _add_rows(r2, C, 2 * n)
    nt = r2 // tr

    def body(c_ref, *refs):
        for p in range(n):
            refs[2 * n + p][...] = (refs[p][...] + refs[n + p][...].astype(F32)).astype(BF16)

    return pl.pallas_call(
        body, name=name,
        grid_spec=pltpu.PrefetchScalarGridSpec(
            num_scalar_prefetch=1, grid=(N_SHARD, nt),
            in_specs=[pl.BlockSpec((None, tr, C), lambda s, i, c_ref: (s, c_ref[0] * nt + i, 0))] * n
            + [pl.BlockSpec((None, tr, C), lambda s, i, c_ref: (s, i, 0))] * n,
            out_specs=[pl.BlockSpec((None, tr, C), lambda s, i, c_ref: (s, i, 0))] * n),
        out_shape=[jax.ShapeDtypeStruct((N_SHARD, r2, C), BF16)] * n,
        compiler_params=_cp(("arbitrary", "arbitrary"), VMEM_BIG),
    )(cidx, *pieces, *recvs)


def _add_chips_into(piece, recv_a, recv_b, sc, prev, shape, lead, name):
    _, R, C = piece.shape
    r2 = R // 2
    tr = _add_rows(r2, C, 4)
    nt = r2 // tr
    nl = len(lead)

    def body(sc_ref, p_ref, a_ref, b_ref, *rest):
        o_ref = rest[-1]
        acc = p_ref[...] + a_ref[...].astype(F32)
        for k in range(3):
            acc = acc + b_ref[k].astype(F32)
        o_ref[...] = acc

    in_specs = [pl.BlockSpec((None, tr, C), lambda i, sc_ref: (sc_ref[0], sc_ref[1] * nt + i, 0)),
                pl.BlockSpec((None, tr, C), lambda i, sc_ref: (sc_ref[0], i, 0)),
                pl.BlockSpec((3, tr, C), lambda i, sc_ref: (0, i, 0))]
    args = [sc, piece, recv_a, recv_b]
    aliases = {}
    if prev is not None:
        in_specs.append(pl.BlockSpec(memory_space=pl.ANY))
        args.append(prev)
        aliases = {4: 0}
    return pl.pallas_call(
        body, name=name,
        grid_spec=pltpu.PrefetchScalarGridSpec(
            num_scalar_prefetch=1, grid=(nt,), in_specs=in_specs,
            out_specs=pl.BlockSpec((None,) * nl + (tr, C), lambda i, sc_ref: (*lead, sc_ref[1] * nt + i, 0))),
        out_shape=jax.ShapeDtypeStruct(shape, F32),
        input_output_aliases=aliases,
        compiler_params=_cp(("arbitrary",), VMEM_BIG),
    )(*args)


def _pack_rows(parts, rows, dtype):
    flat = jnp.concatenate([p.reshape(-1).astype(dtype) for p in parts])
    return jnp.pad(flat, (0, rows * ROW - flat.shape[0])).reshape(rows, ROW)


def _unpack(flat, shapes):
    out, off = [], 0
    for sh in shapes:
        n = math.prod(sh)
        out.append(flat[off:off + n].reshape(sh))
        off += n
    return out


def _heads(t, B, S, blk):
    return t.reshape(B, S, NH, HD).transpose(0, 2, 1, 3).reshape(B, NH, S // blk, blk, HD)


def _unheads(t, B, S):
    return t.reshape(B, NH, S, HD).transpose(0, 2, 1, 3).reshape(B * S, AW)


def _block_diag(w):
    eye = jnp.eye(LW // HD, dtype=w.dtype)
    return jnp.einsum("lhij,hg->lhigj", w, eye).reshape(w.shape[0], LW, LW)


def _diag_blocks(w):
    nbk = LW // HD
    w4 = w.reshape(nbk, HD, nbk, HD)
    return jnp.stack([w4[h, :, h, :] for h in range(nbk)])


def _rows8(rows, width):
    z = jnp.zeros((width,), F32)
    return jnp.stack(list(rows) + [z] * (8 - len(rows)))


def kernel(x, c, w_ada, b_ada, g_norm, w_ffn_up, w_ffn_down, w_in, b_fgate, conv_w, conv_b, w_rgate, b_rgate, w_igate, b_igate, lru_lambda, g_qk, g_mix_out, w_out, loss_target, m_w_ada, m_b_ada, m_g_norm, m_w_ffn_up, m_w_ffn_down, m_w_in, m_b_fgate, m_conv_w, m_conv_b, m_w_rgate, m_b_rgate, m_w_igate, m_b_igate, m_lru_lambda, m_g_qk, m_g_mix_out, m_w_out, v_w_ada, v_b_ada, v_g_norm, v_w_ffn_up, v_w_ffn_down, v_w_in, v_b_fgate, v_conv_w, v_conv_b, v_w_rgate, v_b_rgate, v_w_igate, v_b_igate, v_lru_lambda, v_g_qk, v_g_mix_out, v_w_out):
    B, S, _ = x.shape
    T = B * S
    xi, yi, ci = _coords()
    sidx = 2 * xi + yi
    didx = 4 * xi + 2 * yi + ci
    ada_cols = w_ada.shape[2]
    gn_cols = g_norm.shape[2]
    cw_cols = conv_w.shape[2]
    n_all = B * N_DEV

    blk1 = _pack_rows([c, jnp.pad(g_norm.reshape(-1), (0, 2 * ROW - g_norm.size)), conv_w], 8, F32)
    ag1 = _all_gather8(blk1, "ag_small_in", True)
    c_all = ag1[:, 0:B].reshape(n_all, D)
    chip_rows = ag1[0::2]
    g_norm_full = chip_rows[:, 2:4].reshape(N_SHARD, 2 * ROW)[:, :g_norm.size] \
        .reshape(N_SHARD, 2, 3, gn_cols).transpose(1, 2, 0, 3).reshape(2, 3, D)
    conv_w_full = chip_rows[:, 4].reshape(N_SHARD, 2, 4, cw_cols).transpose(1, 2, 0, 3).reshape(2, 4, LW)

    b_cols = lax.dynamic_slice(b_ada, (0, sidx * ada_cols), (2, ada_cols)).reshape(2, 1, ada_cols)
    mod_cols = _ada_fwd(c_all, w_ada, b_cols)
    mrows = (2 * n_all * ada_cols) // ROW
    ag2 = _all_gather8(mod_cols.reshape(mrows, ROW), "ag_mod", True)
    mod_sh = ag2[0::2].reshape(N_SHARD, 2, n_all, ada_cols)
    mod_me = lax.dynamic_slice(mod_sh, (0, 0, didx * B, 0), (N_SHARD, 2, B, ada_cols))
    mod_me = mod_me.transpose(1, 2, 0, 3).reshape(2, B, 3, 3, D)
    zrow = jnp.zeros((B, D), F32)
    mods = [[jnp.stack([mod_me[l, :, j, 0], 1.0 + mod_me[l, :, j, 1], 1.0 + mod_me[l, :, j, 2],
                        jnp.broadcast_to(g_norm_full[l, j], (B, D)), zrow, zrow, zrow, zrow], axis=1)
             for j in range(3)] for l in range(2)]

    wshards = (w_ffn_up.astype(BF16), w_ffn_down.astype(BF16), w_in.astype(BF16), w_out.astype(BF16))

    def layer_weights(gathered):
        g_up, g_dn, g_in, g_out = gathered
        winp = jnp.pad(g_in.transpose(1, 0, 2).reshape(D, N_IN), ((0, 0), (0, N_INP - N_IN)))
        return dict(up=g_up, dn=g_dn, inp=winp, out=g_out)

    wl = [layer_weights(_run_comm(_ag_layer_comm(*wshards, 0), "ag_weights_0")), None]

    wr_d = _block_diag(w_rgate).astype(BF16)
    wi_d = _block_diag(w_igate).astype(BF16)
    cw8 = jnp.pad(conv_w_full, ((0, 0), (0, 4), (0, 0)))
    vp8 = jnp.stack([_rows8([conv_b[l], b_rgate[l], b_igate[l], lru_lambda[l]], LW) for l in range(2)])
    bfp = jnp.pad(b_fgate, ((0, 0), (0, 128 - NH)))[:, None, :] * jnp.ones((1, 8, 1), F32)
    gqk2 = jnp.tile(jnp.pad(g_qk, ((0, 0), (0, 6), (0, 0))), (1, 1, 2))
    gmix8 = jnp.pad(g_mix_out[:, None, :], ((0, 0), (0, 7), (0, 0)))

    x2 = x.reshape(T, D)
    tgt = loss_target.reshape(T, D)

    saved = []
    xc = x2
    for l in range(2):
        sv = {}
        sv["x0"] = xc
        w = wl[l]
        xc, sv["g0"], sv["u0"], sv["f0"] = _ffn_fwd(xc, mods[l][0], w["up"], w["dn"], l, 0, S)
        sv["x1"] = xc
        sv["h1"], proj = _mix_in_fwd(xc, mods[l][1], w["inp"], l, S)
        sv["proj"] = proj
        sv["ylru"], sv["hl"] = _lru_fwd(proj, cw8, vp8, wr_d, wi_d, l, S)
        sv["osb"], sv["t1"], gathered = _sbq_fwd(proj, l, S, _ag_layer_comm(*wshards, 1) if l == 0 else None)
        if l == 0:
            wl[1] = layer_weights(gathered)
        cum = _fgate_fwd(proj, bfp, l, S)
        sv["ck"] = cum[:, :NH].reshape(B, S, NH).transpose(0, 2, 1).reshape(B, NH, S // TK_(S), 1, TK_(S))
        sv["ofx"], sv["nl"], sv["ox"] = _foxq_fwd(proj, cum, sv["ck"], gqk2, l, S)
        xc, sv["y"], sv["mo"] = _mix_out_fwd(xc, sv["ylru"], sv["osb"], sv["ofx"], mods[l][1], gmix8, w["out"], l, S)
        sv["x2"] = xc
        xc, sv["g2"], sv["u2"], sv["f2"] = _ffn_fwd(xc, mods[l][2], w["up"], w["dn"], l, 1, S)
        saved.append(sv)

    dxc, lpart = _loss_head(xc, tgt, S)
    loss = lax.psum(lpart[0, 0], ("x", "y", "c"))

    tf = wl[0]["up"].shape[-1]
    g_up_l = [[None, None], [None, None]]
    g_dn_l = [[None, None], [None, None]]
    g_in_l, g_out_l = [None, None], [None, None]
    dmods = [[None] * 3 for _ in range(2)]
    small = [dict() for _ in range(2)]
    cvec = jnp.reshape(ci, (1,)).astype(jnp.int32)
    scvec = jnp.stack([sidx, ci]).astype(jnp.int32)

    def layer_groups(l):
        return [("up", [g_up_l[l][j] for j in range(2)], [(l, 0), (l, 1)]),
                ("dn", [g_dn_l[l][j] for j in range(2)], [(l, 0), (l, 1)]),
                ("in", [g_in_l[l]], [(l,)]),
                ("out", [g_out_l[l]], [(l,)])]

    def rs_sibling_phase(l):
        groups = layer_groups(l)
        recv_a = _rs_to_sibling([pb for _, ps, _ in groups for _, pb in ps], f"rs_to_sibling_{l}")
        hs, off = [], 0
        for gname, ps, _ in groups:
            hs += _add_sibling([pf for pf, _ in ps], recv_a[off:off + len(ps)], cvec, f"rs_add_sibling_{gname}_{l}")
            off += len(ps)
        return recv_a, hs

    def ffn_back(l, j, xin, dy, sv, sub):
        dx, dmod, wacc, hb, dfb, ab, dgub = _ffn_bwd(
            xin, dy, mods[l][sub], sv[f"f{sub}"], sv[f"g{sub}"], sv[f"u{sub}"], wl[l]["up"], wl[l]["dn"], l, j, S)
        g_up_l[l][j] = _mm_tn(hb, dgub, f"dw_up_{l}_{j}", tnb=tf, split_n=True, with_bf16=True)
        g_dn_l[l][j] = tuple(g.reshape(N_SHARD, -1, D)
                             for g in _mm_tn(ab, dfb, f"dw_dn_{l}_{j}", tma=tf, with_bf16=True))
        dmods[l][sub] = dmod
        small[l][f"gn{sub}"] = wacc[0]
        return dx

    for l in (1, 0):
        sv = saved[l]
        dxc = ffn_back(l, 1, sv["x2"], dxc, sv, 2)
        dyl, dsb, dfx, dmo, dmod1, wacc_mo = _mix_out_bwd(
            dxc, sv["ylru"], sv["osb"], sv["ofx"], sv["mo"], mods[l][1], gmix8, wl[l]["out"], l, S)
        small[l]["gmix"] = wacc_mo[0]
        g_out_l[l] = tuple(g.reshape(N_SHARD, -1, D) for g in _mm_tn(sv["y"], dmo, f"dw_out_{l}", with_bf16=True))
        dsq, dsk, dsv, recv_b1 = _sbq_bwd(sv["proj"], dsb, sv["t1"], l, S,
                                           _rs_to_chips_comm(rs1[1]) if l == 0 else None)
        dfq, dfk, dfv, dck, wacc_fx = _foxq_bwd(sv["proj"], dfx, sv["nl"], sv["ox"], sv["ck"], gqk2, l, S)
        small[l]["gqk"] = wacc_fx[0:2, :HD] + wacc_fx[0:2, HD:]
        dcum = dck[:, :, :, 0, :].reshape(B, NH, S).transpose(0, 2, 1).reshape(T, NH)
        dff_, wacc_fg = _fgate_bwd(jnp.pad(dcum, ((0, 0), (0, 128 - NH))), sv["proj"], bfp, l, S)
        small[l]["bf"] = wacc_fg[0, :NH]
        dlx, dlg, dpr, dpi, ub, wacc_lru = _lru_bwd(dyl, sv["proj"], sv["hl"], cw8, vp8, wr_d, wi_d, l, S)
        small[l]["lru"] = wacc_lru
        small[l]["wr"] = _diag_blocks(_mm_tn(ub, dpr, f"dw_rgate_{l}"))
        small[l]["wi"] = _diag_blocks(_mm_tn(ub, dpi, f"dw_igate_{l}"))
        dproj = jnp.concatenate(
            [dlx, dlg, dsq, dsk, dsv, dfq, dfk, dfv, dff_], axis=1)
        g_in = _mm_tn(sv["h1"], dproj, f"dw_in_{l}", tnb=N_INP // 3)[:, :N_IN]
        g_in = g_in.reshape(D, N_SHARD, -1).transpose(1, 0, 2)
        g_in_l[l] = (g_in, g_in.astype(BF16))
        dxc, dmod_in, wacc_in = _mix_in_bwd(sv["x1"], dxc, mods[l][1], dproj, wl[l]["inp"], l, S)
        dmods[l][1] = dmod_in + dmod1
        small[l]["gn1"] = wacc_in[0]
        dxc = ffn_back(l, 0, sv["x0"], dxc, sv, 0)
        if l == 1:
            rs1 = rs_sibling_phase(1)
    grad_x = dxc.reshape(B, S, D)

    dmod_loc = jnp.stack([jnp.stack([dmods[l][j][:, 0:3, :] for j in range(3)], axis=1) for l in range(2)])
    drows = 2 * B * 9
    blk3 = _pack_rows([dmod_loc], -(-drows // 8) * 8, F32)
    ag3 = _all_gather8(blk3, "ag_dmod", True)
    dmod_all = ag3[:, :drows].reshape(N_DEV, 2, B, 9 * D).transpose(1, 0, 2, 3).reshape(2, n_all, 9 * D)
    dmod_mine = lax.dynamic_slice(dmod_all, (0, 0, sidx * ada_cols), (2, n_all, ada_cols))
    grad_w_ada = _ada_bwd(c_all, dmod_mine)
    dmod_rows = jnp.pad(dmod_all.transpose(1, 0, 2).reshape(n_all, 2 * 9, D), ((0, 0), (0, 6), (0, 0)))
    grad_b_ada = _sum_lead(dmod_rows, "grad_b_ada")[:2 * 9].reshape(2, 9 * D)

    sm_parts = [
        jnp.stack([small[l]["bf"] for l in range(2)]),
        jnp.stack([small[l]["lru"][4] for l in range(2)]),
        jnp.stack([small[l]["wr"] for l in range(2)]),
        jnp.stack([small[l]["lru"][5] for l in range(2)]),
        jnp.stack([small[l]["wi"] for l in range(2)]),
        jnp.stack([small[l]["lru"][6] for l in range(2)]),
        jnp.stack([small[l]["lru"][7] for l in range(2)]),
        jnp.stack([small[l]["gqk"] for l in range(2)]),
        jnp.stack([small[l]["gmix"] for l in range(2)]),
        jnp.stack([jnp.stack([small[l][f"gn{j}"] for j in range(3)]) for l in range(2)]),
        jnp.stack([small[l]["lru"][0:4] for l in range(2)]),
    ]
    sm_shapes = [p.shape for p in sm_parts]
    sm_rows = -(-sum(p.size for p in sm_parts) // (8 * ROW)) * 8
    ag4 = _all_gather8(_pack_rows(sm_parts, sm_rows, F32), "ag_small_grads", True)
    sm_sum = _sum_lead(ag4, "sum_small_grads").reshape(-1)
    (g_bf, g_cb, g_wr, g_br, g_wi, g_bi, g_lam, g_gqk, g_gmix, g_gn_full, g_cw_full) = _unpack(sm_sum, sm_shapes)
    g_gn = lax.dynamic_slice(g_gn_full, (0, 0, sidx * gn_cols), (2, 3, gn_cols))
    g_cw = lax.dynamic_slice(g_cw_full, (0, 0, sidx * cw_cols), (2, 4, cw_cols))

    rs0 = rs_sibling_phase(0)
    recv_b0 = _run_comm(_rs_to_chips_comm(rs0[1]), "rs_to_chips_0")
    recv = [(rs0[0], recv_b0), (rs1[0], recv_b1)]
    shapes4 = [w_ffn_up.shape, w_ffn_down.shape, w_in.shape, w_out.shape]
    tensors, places, r2s = [None] * 4, [], []
    for l in range(2):
        k = 0
        for gi, (gname, ps, leads) in enumerate(layer_groups(l)):
            for (pf, _), lead in zip(ps, leads):
                tensors[gi] = _add_chips_into(pf, recv[l][0][k], recv[l][1][k], scvec, tensors[gi], shapes4[gi], lead,
                                              f"rs_add_chips_{gname}_{'_'.join(map(str, lead))}")
                places.append((gi, lead))
                r2s.append(pf.shape[1] // 2)
                k += 1
    gw_up, gw_dn, gw_in, gw_out = _share_halves(tensors, places, r2s)

    def upd(w, g, m, v, name):
        sh = w.shape
        two = (w.size // sh[-1], sh[-1])
        dlt, mn, vn = _adamw(w.reshape(two), g.reshape(two), m.reshape(two), v.reshape(two), name)
        return dlt.reshape(sh), mn.reshape(sh), vn.reshape(sh)

    big = {
        "w_ada": (w_ada, grad_w_ada, m_w_ada, v_w_ada),
        "w_ffn_up": (w_ffn_up, gw_up, m_w_ffn_up, v_w_ffn_up),
        "w_ffn_down": (w_ffn_down, gw_dn, m_w_ffn_down, v_w_ffn_down),
        "w_in": (w_in, gw_in, m_w_in, v_w_in),
        "w_out": (w_out, gw_out, m_w_out, v_w_out),
    }
    res = {n: (t[1],) + upd(*t, f"adamw_{n}") for n, t in big.items()}

    smalls = {
        "b_ada": (b_ada, grad_b_ada, m_b_ada, v_b_ada),
        "g_norm": (g_norm, g_gn, m_g_norm, v_g_norm),
        "b_fgate": (b_fgate, g_bf, m_b_fgate, v_b_fgate),
        "conv_w": (conv_w, g_cw, m_conv_w, v_conv_w),
        "conv_b": (conv_b, g_cb, m_conv_b, v_conv_b),
        "w_rgate": (w_rgate, g_wr, m_w_rgate, v_w_rgate),
        "b_rgate": (b_rgate, g_br, m_b_rgate, v_b_rgate),
        "w_igate": (w_igate, g_wi, m_w_igate, v_w_igate),
        "b_igate": (b_igate, g_bi, m_b_igate, v_b_igate),
        "lru_lambda": (lru_lambda, g_lam, m_lru_lambda, v_lru_lambda),
        "g_qk": (g_qk, g_gqk, m_g_qk, v_g_qk),
        "g_mix_out": (g_mix_out, g_gmix, m_g_mix_out, v_g_mix_out),
    }
    names = list(smalls)
    shapes = [smalls[n][0].shape for n in names]
    prow = -(-sum(math.prod(s) for s in shapes) // (8 * ROW)) * 8
    packed = [_pack_rows([smalls[n][i].reshape(shapes[k]) for k, n in enumerate(names)], prow, F32) for i in range(4)]
    outs = _adamw(packed[0], packed[1], packed[2], packed[3], "adamw_small")
    un = [_unpack(o.reshape(-1), shapes) for o in outs]
    for k, n in enumerate(names):
        res[n] = (smalls[n][1].reshape(shapes[k]), un[0][k], un[1][k], un[2][k])

    order = ["w_ada", "b_ada", "g_norm", "w_ffn_up", "w_ffn_down", "w_in", "b_fgate", "conv_w", "conv_b",
             "w_rgate", "b_rgate", "w_igate", "b_igate", "lru_lambda", "g_qk", "g_mix_out", "w_out"]
    return (loss, grad_x, *[res[n][0] for n in order], *[res[n][1] for n in order],
            *[res[n][2] for n in order], *[res[n][3] for n in order])


def TQ_(S):
    return min(TQ, S)


def TK_(S):
    return min(TK, S)


def _unpack_shards(wg, shapes):
    out, off = [], 0
    for sh in shapes:
        n = math.prod(sh)
        out.append(wg[:, off:off + n].reshape((N_SHARD,) + tuple(sh)))
        off += n
    return out
```

```python
import math

import jax
import jax.numpy as jnp
from jax import lax
from jax.experimental import pallas as pl
from jax.experimental.pallas import tpu as pltpu

F32 = jnp.float32
BF16 = jnp.bfloat16
MESH = pl.DeviceIdType.MESH

D = 1024
HD = 64
LW = 512
NH = 4
AW = NH * HD
N_IN = 2564
N_INP = 2688
F_BLK = 2560 // 128
EPS = 1e-6
LRU_C = 8.0
SCALE = HD ** -0.5
NEG = -1e30
TQ = 256
TK = 256

ADAM_LR, ADAM_B1, ADAM_B2, ADAM_EPS, ADAM_WD, ADAM_STEP = 0.001, 0.9, 0.999, 1e-08, 0.01, 10

VMEM_BIG = 56 * 1024 * 1024
N_DEV = 8
N_SHARD = 4
ROW = 1024


def _cp(sem, vmem=None):
    return pltpu.CompilerParams(dimension_semantics=sem, vmem_limit_bytes=vmem)


def _dot(a, b):
    return jnp.dot(a, b, preferred_element_type=F32)


def _dot_nt(a, b):
    return lax.dot_general(a, b, (((1,), (1,)), ((), ())), preferred_element_type=F32)


def _dot_tn(a, b):
    return lax.dot_general(a, b, (((0,), (0,)), ((), ())), preferred_element_type=F32)


def _log1p(e):
    small = e * (1.0 - e * (0.5 - e * (1.0 / 3.0 - e * 0.25)))
    return jnp.where(e < 0.01, small, jnp.log(1.0 + e))


def _expm1_neg(x):
    small = x * (1.0 + x * 0.5 * (1.0 + x * (1.0 / 3.0) * (1.0 + x * 0.25 * (1.0 + x * 0.2))))
    return jnp.where(x > -0.05, small, jnp.exp(x) - 1.0)


def _sigmoid(x):
    return 1.0 / (1.0 + jnp.exp(-x))


_GELU_C = math.sqrt(2.0 / math.pi)


def _gelu_and_grad(x):
    x2 = x * x
    th = jnp.tanh(_GELU_C * (x + 0.044715 * x * x2))
    g = 0.5 * x * (1.0 + th)
    dg = 0.5 * (1.0 + th) + 0.5 * x * (1.0 - th * th) * _GELU_C * (1.0 + 3.0 * 0.044715 * x2)
    return g, dg


def _rms_rows(x):
    rstd = lax.rsqrt(jnp.mean(x * x, axis=-1, keepdims=True) + EPS)
    return x * rstd, rstd


def _rms_bwd(xn, rstd, dyn):
    return rstd * (dyn - xn * jnp.mean(dyn * xn, axis=-1, keepdims=True))


def _colsum(x):
    return jnp.sum(x, axis=0, keepdims=True)


def _rowsum(x):
    return jnp.sum(x, axis=1, keepdims=True)


def _split3(x):
    hi = x.astype(BF16)
    r = x - hi.astype(F32)
    mid = r.astype(BF16)
    lo = (r - mid.astype(F32)).astype(BF16)
    return hi, mid, lo


def _cumsum_mm(x, ones_tri, parts=3):
    ps = _split3(x)[:parts]
    acc = _dot(ps[0], ones_tri)
    for p in ps[1:]:
        acc = acc + _dot(p, ones_tri)
    return acc


def _tri(n, kind):
    r = lax.broadcasted_iota(jnp.int32, (n, n), 0)
    c = lax.broadcasted_iota(jnp.int32, (n, n), 1)
    m = {"row_gt_col": r > c, "row_le_col": r <= c, "row_lt_col": r < c}[kind]
    return jnp.where(m, 1.0, 0.0).astype(BF16)


def _normmod(x, mod_ref):
    xn, rstd = _rms_rows(x)
    h = xn * mod_ref[3:4, :] * mod_ref[1:2, :] + mod_ref[0:1, :]
    return h, xn, rstd


def _normmod_bwd(dh, xn, rstd, mod_ref, dmod_ref, wacc_ref):
    gn = mod_ref[3:4, :]
    sc = mod_ref[1:2, :]
    dmod_ref[0:1, :] += _colsum(dh)
    t = _colsum(dh * xn)
    dmod_ref[1:2, :] += t * gn
    wacc_ref[0:1, :] += t * sc
    return _rms_bwd(xn, rstd, dh * (gn * sc))


def _tile(n, want):
    t = min(n, want)
    while n % t:
        t //= 2
    return t


def _tile_div8(n, cap, mult=8):
    best = mult
    for t in range(mult, min(n, cap) + 1, mult):
        if n % t == 0:
            best = t
    assert n % best == 0
    return best


def _ffn_fwd(x, mod, wup, wdn, l, j, S, comm=None):
    T = x.shape[0]
    tf = wup.shape[-1]
    nk = 2
    tm = _tile(S, 512)
    tpb = S // tm
    nt = T // tm
    c_args, c_specs, c_outs, c_scr = _hosted(comm)
    n_ci, n_co = len(c_args), len(c_outs)

    def body(*refs):
        x_ref, mod_ref, wg_ref, wu_ref, wd_ref = refs[:5]
        c_in = refs[5:5 + n_ci]
        xo_ref, g_ref, u_ref, f_ref = refs[5 + n_ci:9 + n_ci]
        c_out = refs[9 + n_ci:9 + n_ci + n_co]
        h_sc, acc_sc = refs[9 + n_ci + n_co:11 + n_ci + n_co]
        c_sems = refs[11 + n_ci + n_co:]
        i = pl.program_id(0)
        k = pl.program_id(1)
        if comm is not None:
            @pl.when((i == 0) & (k == 0))
            def _():
                comm["start"](c_in, c_out, c_sems)

        @pl.when(k == 0)
        def _():
            h, _, _ = _normmod(x_ref[...], mod_ref)
            h_sc[...] = h.astype(BF16)
            acc_sc[...] = jnp.zeros_like(acc_sc)

        h = h_sc[...]
        g = _dot(h, wg_ref[...])
        u = _dot(h, wu_ref[...])
        g_ref[...] = g.astype(BF16)
        u_ref[...] = u.astype(BF16)
        a = (g * _sigmoid(g)) * u
        acc_sc[...] += _dot(a.astype(BF16), wd_ref[...])

        @pl.when(k == nk - 1)
        def _():
            f = acc_sc[...]
            f_ref[...] = f.astype(BF16)
            xo_ref[...] = x_ref[...] + (0.5 * mod_ref[2:3, :]) * f

        if comm is not None:
            @pl.when((i == nt - 1) & (k == nk - 1))
            def _():
                comm["finish"](c_in, c_out, c_sems)

    anyspec = pl.BlockSpec(memory_space=pl.ANY)
    out = pl.pallas_call(
        body, name=f"ffn_fwd_{l}_{j}",
        grid=(nt, nk),
        in_specs=[
            pl.BlockSpec((tm, D), lambda i, k: (i, 0)),
            pl.BlockSpec((None, 8, D), lambda i, k: (i // tpb, 0, 0)),
            pl.BlockSpec((None, D, tf), lambda i, k: (k, 0, 0)),
            pl.BlockSpec((None, D, tf), lambda i, k: (nk + k, 0, 0)),
            pl.BlockSpec((tf, D), lambda i, k: (k, 0)),
        ] + c_specs,
        out_specs=[
            pl.BlockSpec((tm, D), lambda i, k: (i, 0)),
            pl.BlockSpec((tm, tf), lambda i, k: (i, k)),
            pl.BlockSpec((tm, tf), lambda i, k: (i, k)),
            pl.BlockSpec((tm, D), lambda i, k: (i, 0)),
        ] + [anyspec] * n_co,
        out_shape=[
            jax.ShapeDtypeStruct((T, D), F32),
            jax.ShapeDtypeStruct((T, nk * tf), BF16),
            jax.ShapeDtypeStruct((T, nk * tf), BF16),
            jax.ShapeDtypeStruct((T, D), BF16),
        ] + c_outs,
        scratch_shapes=[pltpu.VMEM((tm, D), BF16), pltpu.VMEM((tm, D), F32)] + c_scr,
        compiler_params=_cp(("arbitrary", "arbitrary"), VMEM_BIG),
    )(x, mod, wup, wup, wdn, *c_args)
    return out[0], out[1], out[2], out[3], list(out[4:])


def _ffn_bwd(x, dy, mod, f, g, u, wup_t, wdn_t, l, j, S, comm=None):
    T = x.shape[0]
    tf = wup_t.shape[1]
    nk = 2
    tm = _tile(S, 256)
    tpb = S // tm
    nt = T // tm
    c_args, c_specs, c_outs, c_scr = _hosted(comm)
    n_ci, n_co = len(c_args), len(c_outs)

    def body(*refs):
        x_ref, dy_ref, mod_ref, f_ref, g_ref, u_ref, wup_ref, wd_ref = refs[:8]
        c_in = refs[8:8 + n_ci]
        dx_ref, dmod_ref, wacc_ref, h_ref, df_ref, a_ref, dgu_ref = refs[8 + n_ci:15 + n_ci]
        c_out = refs[15 + n_ci:15 + n_ci + n_co]
        c_sems = refs[15 + n_ci + n_co:]
        i = pl.program_id(0)

        @pl.when(i == 0)
        def _():
            wacc_ref[...] = jnp.zeros_like(wacc_ref)
            if comm is not None:
                comm["start"](c_in, c_out, c_sems)

        @pl.when(i % tpb == 0)
        def _():
            dmod_ref[...] = jnp.zeros_like(dmod_ref)

        dy_ = dy_ref[...]
        h, xn, rstd = _normmod(x_ref[...], mod_ref)
        h_ref[...] = h.astype(BF16)
        dfb = ((0.5 * mod_ref[2:3, :]) * dy_).astype(BF16)
        df_ref[...] = dfb
        dmod_ref[2:3, :] += _colsum(0.5 * f_ref[...].astype(F32) * dy_)
        dh = None
        for k in range(nk):
            cols = slice(k * tf, (k + 1) * tf)
            da = _dot(dfb, wd_ref[:, cols])
            gg = g_ref[:, cols].astype(F32)
            uu = u_ref[:, cols].astype(F32)
            sig = _sigmoid(gg)
            s = gg * sig
            a_ref[:, cols] = (s * uu).astype(BF16)
            du = (da * s).astype(BF16)
            dg = (da * uu * (sig * (1.0 + gg * (1.0 - sig)))).astype(BF16)
            dgu_ref[0, :, cols] = dg
            dgu_ref[1, :, cols] = du
            part = _dot(dg, wup_ref[k]) + _dot(du, wup_ref[nk + k])
            dh = part if dh is None else dh + part
        dx_ref[...] = dy_ + _normmod_bwd(dh, xn, rstd, mod_ref, dmod_ref, wacc_ref)

        if comm is not None:
            @pl.when(i == nt - 1)
            def _():
                comm["finish"](c_in, c_out, c_sems)

    once = pl.Buffered(1)
    anyspec = pl.BlockSpec(memory_space=pl.ANY)
    out = pl.pallas_call(
        body, name=f"ffn_bwd_{l}_{j}",
        grid=(nt,),
        in_specs=[
            pl.BlockSpec((tm, D), lambda i: (i, 0)),
            pl.BlockSpec((tm, D), lambda i: (i, 0)),
            pl.BlockSpec((None, 8, D), lambda i: (i // tpb, 0, 0)),
            pl.BlockSpec((tm, D), lambda i: (i, 0)),
            pl.BlockSpec((tm, nk * tf), lambda i: (i, 0)),
            pl.BlockSpec((tm, nk * tf), lambda i: (i, 0)),
            pl.BlockSpec((2 * nk, tf, D), lambda i: (0, 0, 0), pipeline_mode=once),
            pl.BlockSpec((D, nk * tf), lambda i: (0, 0), pipeline_mode=once),
        ] + c_specs,
        out_specs=[
            pl.BlockSpec((tm, D), lambda i: (i, 0)),
            pl.BlockSpec((None, 8, D), lambda i: (i // tpb, 0, 0)),
            pl.BlockSpec((8, D), lambda i: (0, 0)),
            pl.BlockSpec((tm, D), lambda i: (i, 0)),
            pl.BlockSpec((tm, D), lambda i: (i, 0)),
            pl.BlockSpec((tm, nk * tf), lambda i: (i, 0)),
            pl.BlockSpec((2, tm, nk * tf), lambda i: (0, i, 0)),
        ] + [anyspec] * n_co,
        out_shape=[
            jax.ShapeDtypeStruct((T, D), F32),
            jax.ShapeDtypeStruct((T // S, 8, D), F32),
            jax.ShapeDtypeStruct((8, D), F32),
            jax.ShapeDtypeStruct((T, D), BF16),
            jax.ShapeDtypeStruct((T, D), BF16),
            jax.ShapeDtypeStruct((T, nk * tf), BF16),
            jax.ShapeDtypeStruct((2, T, nk * tf), BF16),
        ] + c_outs,
        scratch_shapes=c_scr,
        compiler_params=_cp(("arbitrary",), VMEM_BIG),
    )(x, dy, mod, f, g, u, wup_t, wdn_t, *c_args)
    return tuple(out[:7]) + (list(out[7:]),)


def _mm_tn(a, b, name, tma=None, tnb=None, split_n=False, with_bf16=False):
    T, M = a.shape
    b3 = b if b.ndim == 3 else b[None]
    nb, _, N = b3.shape
    tma = tma or M
    tnb = tnb or N
    npb = N // tnb
    tt = _tile(T, 1024)
    nt = T // tt

    def body(a_ref, b_ref, o_ref, *ob_ref):
        @pl.when(pl.program_id(2) == 0)
        def _():
            o_ref[...] = jnp.zeros_like(o_ref)

        o_ref[...] += _dot_tn(a_ref[...], b_ref[...])

        if with_bf16:
            @pl.when(pl.program_id(2) == nt - 1)
            def _():
                ob_ref[0][...] = o_ref[...].astype(BF16)

    if split_n:
        shape = (nb * npb, M, tnb)
        out_spec = pl.BlockSpec((None, tma, tnb), lambda m, n, t: (n, m, 0))
    else:
        assert nb == 1
        shape = (M, N)
        out_spec = pl.BlockSpec((tma, tnb), lambda m, n, t: (m, n))
    dts = (F32, BF16) if with_bf16 else (F32,)
    out = pl.pallas_call(
        body, name=name,
        grid=(M // tma, nb * npb, nt),
        in_specs=[pl.BlockSpec((tt, tma), lambda m, n, t: (t, m)),
                  pl.BlockSpec((None, tt, tnb), lambda m, n, t: (n // npb, t, n % npb))],
        out_specs=[out_spec] * len(dts),
        out_shape=[jax.ShapeDtypeStruct(shape, dt) for dt in dts],
        compiler_params=_cp(("arbitrary", "arbitrary", "arbitrary"), VMEM_BIG),
    )(a, b3)
    return tuple(out) if with_bf16 else out[0]


def _mix_in_fwd(x, mod, winp, l, S):
    T = x.shape[0]
    tm = _tile(S, 512)
    tpb = S // tm

    def body(x_ref, mod_ref, w_ref, h_ref, p_ref):
        h, _, _ = _normmod(x_ref[...], mod_ref)
        hb = h.astype(BF16)
        h_ref[...] = hb
        p_ref[...] = _dot(hb, w_ref[...])

    return pl.pallas_call(
        body, name=f"mix_in_fwd_{l}",
        grid=(T // tm,),
        in_specs=[pl.BlockSpec((tm, D), lambda i: (i, 0)),
                  pl.BlockSpec((None, 8, D), lambda i: (i // tpb, 0, 0)),
                  pl.BlockSpec((D, N_INP), lambda i: (0, 0))],
        out_specs=[pl.BlockSpec((tm, D), lambda i: (i, 0)),
                   pl.BlockSpec((tm, N_INP), lambda i: (i, 0))],
        out_shape=[jax.ShapeDtypeStruct((T, D), BF16), jax.ShapeDtypeStruct((T, N_INP), F32)],
        compiler_params=_cp(("arbitrary",), VMEM_BIG),
    )(x, mod, winp)


def _mix_in_bwd(x, dres, mod, dproj, winp, l, S):
    T = x.shape[0]
    tm = _tile(S, 512)
    tpb = S // tm

    def body(x_ref, dr_ref, mod_ref, dp_ref, w_ref, dx_ref, dmod_ref, wacc_ref):
        i = pl.program_id(0)

        @pl.when(i == 0)
        def _():
            wacc_ref[...] = jnp.zeros_like(wacc_ref)

        @pl.when(i % tpb == 0)
        def _():
            dmod_ref[...] = jnp.zeros_like(dmod_ref)

        dh = _dot_nt(dp_ref[...], w_ref[...])
        _, xn, rstd = _normmod(x_ref[...], mod_ref)
        dx_ref[...] = dr_ref[...] + _normmod_bwd(dh, xn, rstd, mod_ref, dmod_ref, wacc_ref)

    return pl.pallas_call(
        body, name=f"mix_in_bwd_{l}",
        grid=(T // tm,),
        in_specs=[pl.BlockSpec((tm, D), lambda i: (i, 0)),
                  pl.BlockSpec((tm, D), lambda i: (i, 0)),
                  pl.BlockSpec((None, 8, D), lambda i: (i // tpb, 0, 0)),
                  pl.BlockSpec((tm, N_INP), lambda i: (i, 0)),
                  pl.BlockSpec((D, N_INP), lambda i: (0, 0))],
        out_specs=[pl.BlockSpec((tm, D), lambda i: (i, 0)),
                   pl.BlockSpec((None, 8, D), lambda i: (i // tpb, 0, 0)),
                   pl.BlockSpec((8, D), lambda i: (0, 0))],
        out_shape=[jax.ShapeDtypeStruct((T, D), F32),
                   jax.ShapeDtypeStruct((T // S, 8, D), F32),
                   jax.ShapeDtypeStruct((8, D), F32)],
        compiler_params=_cp(("arbitrary",), VMEM_BIG),
    )(x, dres, mod, dproj, winp)


_GROUPS = ((0, LW), (LW, LW + AW), (LW + AW, D))


def _mix_out_fwd(x, ylru, osb, ofox, mod, gmix, wout, l, S):
    T = x.shape[0]
    tm = _tile(S, 512)
    tpb = S // tm

    def body(x_ref, yl_ref, sb_ref, fx_ref, mod_ref, gm_ref, w_ref, xo_ref, y_ref, mo_ref):
        for src, (lo, hi) in zip((yl_ref, sb_ref, fx_ref), _GROUPS):
            vn, _ = _rms_rows(src[...])
            y_ref[:, lo:hi] = (vn * gm_ref[0:1, lo:hi]).astype(BF16)
        mo = _dot(y_ref[...], w_ref[...])
        mo_ref[...] = mo.astype(BF16)
        xo_ref[...] = x_ref[...] + mod_ref[2:3, :] * mo

    return pl.pallas_call(
        body, name=f"mix_out_fwd_{l}",
        grid=(T // tm,),
        in_specs=[pl.BlockSpec((tm, D), lambda i: (i, 0)),
                  pl.BlockSpec((tm, LW), lambda i: (i, 0)),
                  pl.BlockSpec((tm, AW), lambda i: (i, 0)),
                  pl.BlockSpec((tm, AW), lambda i: (i, 0)),
                  pl.BlockSpec((None, 8, D), lambda i: (i // tpb, 0, 0)),
                  pl.BlockSpec((None, 8, D), lambda i: (l, 0, 0)),
                  pl.BlockSpec((D, D), lambda i: (0, 0))],
        out_specs=[pl.BlockSpec((tm, D), lambda i: (i, 0)),
                   pl.BlockSpec((tm, D), lambda i: (i, 0)),
                   pl.BlockSpec((tm, D), lambda i: (i, 0))],
        out_shape=[jax.ShapeDtypeStruct((T, D), F32),
                   jax.ShapeDtypeStruct((T, D), BF16),
                   jax.ShapeDtypeStruct((T, D), BF16)],
        compiler_params=_cp(("arbitrary",), VMEM_BIG),
    )(x, ylru, osb, ofox, mod, gmix, wout)


def _mix_out_bwd(dx2, ylru, osb, ofox, mo, mod, gmix, wout, l, S):
    T = dx2.shape[0]
    tm = _tile(S, 512)
    tpb = S // tm

    def body(dx_ref, yl_ref, sb_ref, fx_ref, mo_ref, mod_ref, gm_ref, w_ref,
             dyl_ref, dsb_ref, dfx_ref, dmo_ref, dmod_ref, wacc_ref):
        i = pl.program_id(0)

        @pl.when(i == 0)
        def _():
            wacc_ref[...] = jnp.zeros_like(wacc_ref)

        @pl.when(i % tpb == 0)
        def _():
            dmod_ref[...] = jnp.zeros_like(dmod_ref)

        dx = dx_ref[...]
        dmod_ref[2:3, :] += _colsum(mo_ref[...].astype(F32) * dx)
        dmo = (mod_ref[2:3, :] * dx).astype(BF16)
        dmo_ref[...] = dmo
        dy = _dot_nt(dmo, w_ref[...])
        for src, dst, (lo, hi) in zip((yl_ref, sb_ref, fx_ref), (dyl_ref, dsb_ref, dfx_ref), _GROUPS):
            vn, rstd = _rms_rows(src[...])
            dyg = dy[:, lo:hi]
            wacc_ref[0:1, lo:hi] += _colsum(dyg * vn)
            dst[...] = _rms_bwd(vn, rstd, dyg * gm_ref[0:1, lo:hi])

    return pl.pallas_call(
        body, name=f"mix_out_bwd_{l}",
        grid=(T // tm,),
        in_specs=[pl.BlockSpec((tm, D), lambda i: (i, 0)),
                  pl.BlockSpec((tm, LW), lambda i: (i, 0)),
                  pl.BlockSpec((tm, AW), lambda i: (i, 0)),
                  pl.BlockSpec((tm, AW), lambda i: (i, 0)),
                  pl.BlockSpec((tm, D), lambda i: (i, 0)),
                  pl.BlockSpec((None, 8, D), lambda i: (i // tpb, 0, 0)),
                  pl.BlockSpec((None, 8, D), lambda i: (l, 0, 0)),
                  pl.BlockSpec((D, D), lambda i: (0, 0))],
        out_specs=[pl.BlockSpec((tm, LW), lambda i: (i, 0)),
                   pl.BlockSpec((tm, AW), lambda i: (i, 0)),
                   pl.BlockSpec((tm, AW), lambda i: (i, 0)),
                   pl.BlockSpec((tm, D), lambda i: (i, 0)),
                   pl.BlockSpec((None, 8, D), lambda i: (i // tpb, 0, 0)),
                   pl.BlockSpec((8, D), lambda i: (0, 0))],
        out_shape=[jax.ShapeDtypeStruct((T, LW), F32),
                   jax.ShapeDtypeStruct((T, AW), F32),
                   jax.ShapeDtypeStruct((T, AW), F32),
                   jax.ShapeDtypeStruct((T, D), BF16),
                   jax.ShapeDtypeStruct((T // S, 8, D), F32),
                   jax.ShapeDtypeStruct((8, D), F32)],
        compiler_params=_cp(("arbitrary",), VMEM_BIG),
    )(dx2, ylru, osb, ofox, mo, mod, gmix, wout)


def _loss_head(y, tgt, S):
    T = y.shape[0]
    tm = _tile(S, 512)

    def body(y_ref, t_ref, dy_ref, l_ref):
        @pl.when(pl.program_id(0) == 0)
        def _():
            l_ref[...] = jnp.zeros_like(l_ref)

        d = y_ref[...] - t_ref[...]
        dy_ref[...] = d * (1.0 / D)
        l_ref[...] += (0.5 / D) * _rowsum(_colsum(d * d))

    return pl.pallas_call(
        body, name="loss_head",
        grid=(T // tm,),
        in_specs=[pl.BlockSpec((tm, D), lambda i: (i, 0)), pl.BlockSpec((tm, D), lambda i: (i, 0))],
        out_specs=[pl.BlockSpec((tm, D), lambda i: (i, 0)), pl.BlockSpec((8, 128), lambda i: (0, 0))],
        out_shape=[jax.ShapeDtypeStruct((T, D), F32), jax.ShapeDtypeStruct((8, 128), F32)],
        compiler_params=_cp(("arbitrary",)),
    )(y, tgt)


def _lru_gates(u, vp_ref, wr_ref, wi_ref):
    ub = u.astype(BF16)
    r = _sigmoid(_dot(ub, wr_ref[...]) + vp_ref[1:2, :])
    ig = _sigmoid(_dot(ub, wi_ref[...]) + vp_ref[2:3, :])
    lam = vp_ref[3:4, :]
    sp = jnp.maximum(-lam, 0.0) + _log1p(jnp.exp(-jnp.abs(lam)))
    log_a = (-LRU_C) * r * sp
    a = jnp.exp(log_a)
    mult = jnp.sqrt(-_expm1_neg(2.0 * log_a))
    return ub, r, ig, sp, a, mult


def _conv_taps(x, xp, row, cw_ref):
    xs = [x]
    for d in (1, 2, 3):
        xs.append(jnp.where(row >= d, pltpu.roll(x, d, 0), pltpu.roll(xp, d, 0)))
    u = xs[0] * cw_ref[3:4, :]
    for d in (1, 2, 3):
        u = u + xs[d] * cw_ref[3 - d:4 - d, :]
    return xs, u


def _lru_fwd(proj, cw, vp, wr, wi, l, S):
    T = proj.shape[0]
    ts = _tile(S, 256)
    nb = S // ts

    def body(x_ref, lg_ref, cw_ref, vp_ref, wr_ref, wi_ref, y_ref, h_ref, xp_sc, hc_sc):
        @pl.when(pl.program_id(1) == 0)
        def _():
            xp_sc[...] = jnp.zeros_like(xp_sc)
            hc_sc[...] = jnp.zeros_like(hc_sc)

        row = lax.broadcasted_iota(jnp.int32, (ts, LW), 0)
        x = x_ref[...]
        _, u = _conv_taps(x, xp_sc[...], row, cw_ref)
        u = u + vp_ref[0:1, :]
        xp_sc[...] = x
        _, _, ig, _, a, mult = _lru_gates(u, vp_ref, wr_ref, wi_ref)
        bv = mult * (ig * u)
        av = a
        d = 1
        while d < ts:
            a_s = jnp.where(row >= d, pltpu.roll(av, d, 0), 1.0)
            b_s = jnp.where(row >= d, pltpu.roll(bv, d, 0), 0.0)
            bv = av * b_s + bv
            av = av * a_s
            d *= 2
        h = bv + av * hc_sc[7:8, :]
        hc_sc[...] = h[ts - 8:ts, :]
        h_ref[...] = h
        gl, _ = _gelu_and_grad(lg_ref[...])
        y_ref[...] = h * gl

    return pl.pallas_call(
        body, name=f"lru_fwd_{l}",
        grid=(T // S, nb),
        in_specs=[pl.BlockSpec((ts, LW), lambda b, j: (b * nb + j, 0)),
                  pl.BlockSpec((ts, LW), lambda b, j: (b * nb + j, 1)),
                  pl.BlockSpec((None, 8, LW), lambda b, j: (l, 0, 0)),
                  pl.BlockSpec((None, 8, LW), lambda b, j: (l, 0, 0)),
                  pl.BlockSpec((None, LW, LW), lambda b, j: (l, 0, 0)),
                  pl.BlockSpec((None, LW, LW), lambda b, j: (l, 0, 0))],
        out_specs=[pl.BlockSpec((ts, LW), lambda b, j: (b * nb + j, 0)),
                   pl.BlockSpec((ts, LW), lambda b, j: (b * nb + j, 0))],
        out_shape=[jax.ShapeDtypeStruct((T, LW), F32), jax.ShapeDtypeStruct((T, LW), F32)],
        scratch_shapes=[pltpu.VMEM((ts, LW), F32), pltpu.VMEM((8, LW), F32)],
        compiler_params=_cp(("arbitrary", "arbitrary")),
    )(proj, proj, cw, vp, wr, wi)


def _lru_bwd(dyl, proj, h, cw, vp, wr, wi, l, S):
    T = proj.shape[0]
    ts = _tile(S, 256)
    nb = S // ts

    def body(dy_ref, x_ref, xprev_ref, lg_ref, h_ref, hprev_ref, cw_ref, vp_ref, wr_ref, wi_ref,
             dx_ref, dlg_ref, dpr_ref, dpi_ref, ub_ref, wacc_ref, gc_sc, af_sc, dun_sc):
        b = pl.program_id(0)
        j = pl.program_id(1)
        first = j == nb - 1

        @pl.when((b == 0) & (j == 0))
        def _():
            wacc_ref[...] = jnp.zeros_like(wacc_ref)

        @pl.when(j == 0)
        def _():
            gc_sc[...] = jnp.zeros_like(gc_sc)
            af_sc[...] = jnp.ones_like(af_sc)
            dun_sc[...] = jnp.zeros_like(dun_sc)

        row = lax.broadcasted_iota(jnp.int32, (ts, LW), 0)
        keep = jnp.where(first, 0.0, 1.0)
        x = x_ref[...]
        xs, u = _conv_taps(x, xprev_ref[...] * keep, row, cw_ref)
        u = u + vp_ref[0:1, :]
        ub, r, ig, sp, a, mult = _lru_gates(u, vp_ref, wr_ref, wi_ref)
        ub_ref[...] = ub
        hh = h_ref[...]
        h_m1 = jnp.where(row >= 1, pltpu.roll(hh, 1, 0), pltpu.roll(hprev_ref[...] * keep, 1, 0))
        dy = dy_ref[...]
        gl, dgl = _gelu_and_grad(lg_ref[...])
        dlg_ref[...] = (dy * hh * dgl).astype(BF16)
        bv = dy * gl
        av = jnp.where(row < ts - 1, pltpu.roll(a, ts - 1, 0), af_sc[0:1, :])
        d = 1
        while d < ts:
            a_s = jnp.where(row < ts - d, pltpu.roll(av, ts - d, 0), 1.0)
            b_s = jnp.where(row < ts - d, pltpu.roll(bv, ts - d, 0), 0.0)
            bv = av * b_s + bv
            av = av * a_s
            d *= 2
        gt = bv + av * gc_sc[0:1, :]
        gc_sc[...] = gt[0:8, :]
        af_sc[...] = a[0:8, :]
        da = gt * h_m1
        d_ig = gt * mult * u
        d_mult = gt * ig * u
        du = gt * mult * ig
        dlog_a = da * a - d_mult * (a * a) / mult
        dpre_r = (dlog_a * ((-LRU_C) * sp)) * r * (1.0 - r)
        dpre_i = d_ig * ig * (1.0 - ig)
        lam = vp_ref[3:4, :]
        wacc_ref[7:8, :] += _colsum(dlog_a * r) * (LRU_C * _sigmoid(-lam))
        wacc_ref[5:6, :] += _colsum(dpre_r)
        wacc_ref[6:7, :] += _colsum(dpre_i)
        dprb = dpre_r.astype(BF16)
        dpib = dpre_i.astype(BF16)
        dpr_ref[...] = dprb
        dpi_ref[...] = dpib
        du = du + _dot_nt(dprb, wr_ref[...]) + _dot_nt(dpib, wi_ref[...])
        wacc_ref[4:5, :] += _colsum(du)
        dun = dun_sc[...]
        dx = du * cw_ref[3:4, :]
        wacc_ref[3:4, :] += _colsum(du * xs[0])
        for dd in (1, 2, 3):
            du_s = jnp.where(row < ts - dd, pltpu.roll(du, ts - dd, 0), pltpu.roll(dun, ts - dd, 0))
            dx = dx + du_s * cw_ref[3 - dd:4 - dd, :]
            wacc_ref[3 - dd:4 - dd, :] += _colsum(du * xs[dd])
        dun_sc[...] = du
        dx_ref[...] = dx.astype(BF16)

    def tb(b, j):
        return b * nb + (nb - 1 - j)

    def tbp(b, j):
        return b * nb + jnp.maximum(nb - 2 - j, 0)

    return pl.pallas_call(
        body, name=f"lru_bwd_{l}",
        grid=(T // S, nb),
        in_specs=[pl.BlockSpec((ts, LW), lambda b, j: (tb(b, j), 0)),
                  pl.BlockSpec((ts, LW), lambda b, j: (tb(b, j), 0)),
                  pl.BlockSpec((ts, LW), lambda b, j: (tbp(b, j), 0)),
                  pl.BlockSpec((ts, LW), lambda b, j: (tb(b, j), 1)),
                  pl.BlockSpec((ts, LW), lambda b, j: (tb(b, j), 0)),
                  pl.BlockSpec((ts, LW), lambda b, j: (tbp(b, j), 0)),
                  pl.BlockSpec((None, 8, LW), lambda b, j: (l, 0, 0)),
                  pl.BlockSpec((None, 8, LW), lambda b, j: (l, 0, 0)),
                  pl.BlockSpec((None, LW, LW), lambda b, j: (l, 0, 0)),
                  pl.BlockSpec((None, LW, LW), lambda b, j: (l, 0, 0))],
        out_specs=[pl.BlockSpec((ts, LW), lambda b, j: (tb(b, j), 0)),
                   pl.BlockSpec((ts, LW), lambda b, j: (tb(b, j), 0)),
                   pl.BlockSpec((ts, LW), lambda b, j: (tb(b, j), 0)),
                   pl.BlockSpec((ts, LW), lambda b, j: (tb(b, j), 0)),
                   pl.BlockSpec((ts, LW), lambda b, j: (tb(b, j), 0)),
                   pl.BlockSpec((8, LW), lambda b, j: (0, 0))],
        out_shape=[jax.ShapeDtypeStruct((T, LW), BF16),
                   jax.ShapeDtypeStruct((T, LW), BF16),
                   jax.ShapeDtypeStruct((T, LW), BF16),
                   jax.ShapeDtypeStruct((T, LW), BF16),
                   jax.ShapeDtypeStruct((T, LW), BF16),
                   jax.ShapeDtypeStruct((8, LW), F32)],
        scratch_shapes=[pltpu.VMEM((8, LW), F32), pltpu.VMEM((8, LW), F32), pltpu.VMEM((ts, LW), F32)],
        compiler_params=_cp(("arbitrary", "arbitrary")),
    )(dyl, proj, proj, proj, h, h, cw, vp, wr, wi)


def _fgate_fwd(proj, bfp, l, S):
    T = proj.shape[0]

    def body(x_ref, b_ref, o_ref):
        z = x_ref[...] + b_ref[0:1, :]
        v = jnp.minimum(z, 0.0) - _log1p(jnp.exp(-jnp.abs(z)))
        row = lax.broadcasted_iota(jnp.int32, (S, 128), 0)
        d = 1
        while d < S:
            v = v + jnp.where(row >= d, pltpu.roll(v, d, 0), 0.0)
            d *= 2
        o_ref[...] = v

    return pl.pallas_call(
        body, name=f"fgate_fwd_{l}",
        grid=(T // S,),
        in_specs=[pl.BlockSpec((S, 128), lambda b: (b, F_BLK)),
                  pl.BlockSpec((None, 8, 128), lambda b: (l, 0, 0))],
        out_specs=pl.BlockSpec((S, 128), lambda b: (b, 0)),
        out_shape=jax.ShapeDtypeStruct((T, 128), F32),
        compiler_params=_cp(("arbitrary",)),
    )(proj, bfp)


def _fgate_bwd(dcum, proj, bfp, l, S):
    T = proj.shape[0]

    def body(d_ref, x_ref, b_ref, o_ref, wacc_ref):
        @pl.when(pl.program_id(0) == 0)
        def _():
            wacc_ref[...] = jnp.zeros_like(wacc_ref)

        v = d_ref[...]
        row = lax.broadcasted_iota(jnp.int32, (S, 128), 0)
        d = 1
        while d < S:
            v = v + jnp.where(row < S - d, pltpu.roll(v, S - d, 0), 0.0)
            d *= 2
        z = x_ref[...] + b_ref[0:1, :]
        dz = v * _sigmoid(-z)
        o_ref[...] = dz.astype(BF16)
        wacc_ref[0:1, :] += _colsum(dz)

    return pl.pallas_call(
        body, name=f"fgate_bwd_{l}",
        grid=(T // S,),
        in_specs=[pl.BlockSpec((S, 128), lambda b: (b, 0)),
                  pl.BlockSpec((S, 128), lambda b: (b, F_BLK)),
                  pl.BlockSpec((None, 8, 128), lambda b: (l, 0, 0))],
        out_specs=[pl.BlockSpec((S, 128), lambda b: (b, 0)), pl.BlockSpec((8, 128), lambda b: (0, 0))],
        out_shape=[jax.ShapeDtypeStruct((T, 128), BF16), jax.ShapeDtypeStruct((8, 128), F32)],
        compiler_params=_cp(("arbitrary",)),
    )(dcum, proj, bfp)


def _logsig_parts(z):
    e = jnp.exp(-jnp.abs(z))
    l1p = jnp.log(1.0 + e)
    return e, jnp.minimum(z, 0.0) - l1p, -jnp.maximum(z, 0.0) - l1p


def _sb_fwd(q, k, v, l):
    B, H, nq, tq, _ = q.shape
    nk, tk = k.shape[2], k.shape[3]
    rr = tq // tk

    def body(q_ref, k_ref, v_ref, o_ref, t1_ref):
        tri = _tri(tk, "row_gt_col")
        ti = lax.broadcasted_iota(jnp.int32, (tq, 1), 0)
        si = lax.broadcasted_iota(jnp.int32, (1, tk), 1)

        def qloop(qb, carry):
            qq = q_ref[qb]
            tpos = qb * tq + ti
            nkb = (qb + 1) * rr

            def kloop(i, c):
                acc, run = c
                kb = nkb - 1 - i
                z = _dot_nt(qq, k_ref[kb]) * SCALE
                past = (kb * tk + si) < tpos
                _, lb, l1 = _logsig_parts(z)
                l1m = jnp.where(past, l1, 0.0)
                aft = _cumsum_mm(l1m, tri) + run
                w = jnp.where(past, jnp.exp(lb + aft), 0.0)
                acc = acc + _dot(w.astype(BF16), v_ref[kb])
                return acc, run + _rowsum(l1m)

            acc, run = lax.fori_loop(0, nkb, kloop, (jnp.zeros((tq, HD), F32), jnp.zeros((tq, 1), F32)))
            o_ref[qb] = acc
            t1_ref[qb] = run
            return carry

        lax.fori_loop(0, nq, qloop, 0)

    qs = pl.BlockSpec((None, None, nq, tq, HD), lambda b, h: (b, h, 0, 0, 0))
    ks = pl.BlockSpec((None, None, nk, tk, HD), lambda b, h: (b, h, 0, 0, 0))
    return pl.pallas_call(
        body, name=f"sb_fwd_{l}",
        grid=(B, H),
        in_specs=[qs, ks, ks],
        out_specs=[qs, pl.BlockSpec((None, None, nq, tq, 1), lambda b, h: (b, h, 0, 0, 0))],
        out_shape=[jax.ShapeDtypeStruct((B, H, nq, tq, HD), F32),
                   jax.ShapeDtypeStruct((B, H, nq, tq, 1), F32)],
        compiler_params=_cp(("arbitrary", "arbitrary"), VMEM_BIG),
    )(q, k, v)


def _sb_bwd(q, k, v, do, t1, l):
    B, H, nq, tq, _ = q.shape
    nk, tk = k.shape[2], k.shape[3]
    rr = tq // tk

    def body(q_ref, k_ref, v_ref, do_ref, t1_ref, dq_ref, dk_ref, dv_ref, dk_sc, dv_sc):
        dk_sc[...] = jnp.zeros_like(dk_sc)
        dv_sc[...] = jnp.zeros_like(dv_sc)
        tri_in = _tri(tk, "row_le_col")
        tri_ex = _tri(tk, "row_lt_col")
        ti = lax.broadcasted_iota(jnp.int32, (tq, 1), 0)
        si = lax.broadcasted_iota(jnp.int32, (1, tk), 1)

        def qloop(qb, carry):
            qq = q_ref[qb]
            dob = do_ref[qb].astype(BF16)
            tot = t1_ref[qb]
            tpos = qb * tq + ti
            nkb = (qb + 1) * rr

            def kloop(kb, c):
                dq, run1, rung = c
                kk = k_ref[kb]
                vv = v_ref[kb]
                z = _dot_nt(qq, kk) * SCALE
                past = (kb * tk + si) < tpos
                e, lb, l1 = _logsig_parts(z)
                l1m = jnp.where(past, l1, 0.0)
                aft = tot - (run1 + _cumsum_mm(l1m, tri_in))
                w = jnp.where(past, jnp.exp(lb + aft), 0.0)
                gm = w * _dot_nt(dob, vv)
                cpre = rung + _cumsum_mm(gm, tri_ex, parts=2)
                inv = 1.0 / (1.0 + e)
                sig = jnp.where(z >= 0.0, inv, e * inv)
                dz = jnp.where(past, gm * (1.0 - sig) - cpre * sig, 0.0).astype(BF16)
                dv_sc[kb] += _dot_tn(w.astype(BF16), dob)
                dk_sc[kb] += _dot_tn(dz, qq) * SCALE
                dq = dq + _dot(dz, kk) * SCALE
                return dq, run1 + _rowsum(l1m), rung + _rowsum(gm)

            z1 = jnp.zeros((tq, 1), F32)
            dq, _, _ = lax.fori_loop(0, nkb, kloop, (jnp.zeros((tq, HD), F32), z1, z1))
            dq_ref[qb] = dq.astype(BF16)
            return carry

        lax.fori_loop(0, nq, qloop, 0)
        dk_ref[...] = dk_sc[...].astype(BF16)
        dv_ref[...] = dv_sc[...].astype(BF16)

    qs = pl.BlockSpec((None, None, nq, tq, HD), lambda b, h: (b, h, 0, 0, 0))
    ks = pl.BlockSpec((None, None, nk, tk, HD), lambda b, h: (b, h, 0, 0, 0))
    return pl.pallas_call(
        body, name=f"sb_bwd_{l}",
        grid=(B, H),
        in_specs=[qs, ks, ks, qs, pl.BlockSpec((None, None, nq, tq, 1), lambda b, h: (b, h, 0, 0, 0))],
        out_specs=[qs, ks, ks],
        out_shape=[jax.ShapeDtypeStruct((B, H, nq, tq, HD), BF16),
                   jax.ShapeDtypeStruct((B, H, nk, tk, HD), BF16),
                   jax.ShapeDtypeStruct((B, H, nk, tk, HD), BF16)],
        scratch_shapes=[pltpu.VMEM((nk, tk, HD), F32), pltpu.VMEM((nk, tk, HD), F32)],
        compiler_params=_cp(("arbitrary", "arbitrary"), VMEM_BIG),
    )(q, k, v, do, t1)


def _fox_fwd(q, k, v, cq, ck, gqk, l):
    B, H, nq, tq, _ = q.shape
    nk, tk = k.shape[2], k.shape[3]
    rr = tq // tk

    def body(q_ref, k_ref, v_ref, cq_ref, ck_ref, g_ref, o_ref, lse_ref, fk_sc):
        g0 = g_ref[0:1, :]
        g1 = g_ref[1:2, :]

        def kprep(kb, c):
            kn, _ = _rms_rows(k_ref[kb])
            fk_sc[kb] = (kn * g1).astype(BF16)
            return c

        lax.fori_loop(0, nk, kprep, 0)
        ti = lax.broadcasted_iota(jnp.int32, (tq, 1), 0)
        si = lax.broadcasted_iota(jnp.int32, (1, tk), 1)

        def qloop(qb, carry):
            qn, _ = _rms_rows(q_ref[qb])
            fq = (qn * g0).astype(BF16)
            cqq = cq_ref[qb]
            tpos = qb * tq + ti

            def kloop(kb, c):
                m, lsum, acc = c
                s = _dot_nt(fq, fk_sc[kb]) * SCALE + cqq - ck_ref[kb]
                s = jnp.where((kb * tk + si) <= tpos, s, NEG)
                m2 = jnp.maximum(m, jnp.max(s, axis=1, keepdims=True))
                al = jnp.exp(m - m2)
                p = jnp.exp(s - m2)
                return m2, al * lsum + _rowsum(p), al * acc + _dot(p.astype(BF16), v_ref[kb])

            m, lsum, acc = lax.fori_loop(
                0, (qb + 1) * rr, kloop,
                (jnp.full((tq, 1), NEG, F32), jnp.zeros((tq, 1), F32), jnp.zeros((tq, HD), F32)))
            o_ref[qb] = acc / lsum
            lse_ref[qb] = m + jnp.log(lsum)
            return carry

        lax.fori_loop(0, nq, qloop, 0)

    qs = pl.BlockSpec((None, None, nq, tq, HD), lambda b, h: (b, h, 0, 0, 0))
    ks = pl.BlockSpec((None, None, nk, tk, HD), lambda b, h: (b, h, 0, 0, 0))
    cqs = pl.BlockSpec((None, None, nq, tq, 1), lambda b, h: (b, h, 0, 0, 0))
    cks = pl.BlockSpec((None, None, nk, 1, tk), lambda b, h: (b, h, 0, 0, 0))
    return pl.pallas_call(
        body, name=f"fox_fwd_{l}",
        grid=(B, H),
        in_specs=[qs, ks, ks, cqs, cks, pl.BlockSpec((None, 8, HD), lambda b, h: (l, 0, 0))],
        out_specs=[qs, cqs],
        out_shape=[jax.ShapeDtypeStruct((B, H, nq, tq, HD), F32),
                   jax.ShapeDtypeStruct((B, H, nq, tq, 1), F32)],
        scratch_shapes=[pltpu.VMEM((nk, tk, HD), BF16)],
        compiler_params=_cp(("arbitrary", "arbitrary"), VMEM_BIG),
    )(q, k, v, cq, ck, gqk)


def _fox_bwd(q, k, v, cq, ck, gqk, do, lse, l):
    B, H, nq, tq, _ = q.shape
    nk, tk = k.shape[2], k.shape[3]
    rr = tq // tk

    def body(q_ref, k_ref, v_ref, cq_ref, ck_ref, g_ref, do_ref, lse_ref,
             dq_ref, dk_ref, dv_ref, dc_ref, wacc_ref, fk_sc, dfk_sc, dv_sc):
        @pl.when((pl.program_id(0) == 0) & (pl.program_id(1) == 0))
        def _():
            wacc_ref[...] = jnp.zeros_like(wacc_ref)

        g0 = g_ref[0:1, :]
        g1 = g_ref[1:2, :]
        dfk_sc[...] = jnp.zeros_like(dfk_sc)
        dv_sc[...] = jnp.zeros_like(dv_sc)
        dc_ref[...] = jnp.zeros_like(dc_ref)

        def kprep(kb, c):
            kn, _ = _rms_rows(k_ref[kb])
            fk_sc[kb] = (kn * g1).astype(BF16)
            return c

        lax.fori_loop(0, nk, kprep, 0)
        ti = lax.broadcasted_iota(jnp.int32, (tq, 1), 0)
        si = lax.broadcasted_iota(jnp.int32, (1, tk), 1)

        def qloop(qb, carry):
            qn, qr = _rms_rows(q_ref[qb])
            fq = (qn * g0).astype(BF16)
            cqq = cq_ref[qb]
            lse = lse_ref[qb]
            dob = do_ref[qb].astype(BF16)
            tpos = qb * tq + ti

            def probs(kb):
                s = _dot_nt(fq, fk_sc[kb]) * SCALE + cqq - ck_ref[kb]
                p = jnp.where((kb * tk + si) <= tpos, jnp.exp(s - lse), 0.0)
                return p, _dot_nt(dob, v_ref[kb])

            def dloop(kb, acc):
                p, dp = probs(kb)
                return acc + _rowsum(p * dp)

            dlt = lax.fori_loop(0, (qb + 1) * rr, dloop, jnp.zeros((tq, 1), F32))

            def kloop(kb, dfq):
                fk = fk_sc[kb]
                p, dp = probs(kb)
                ds = p * (dp - dlt)
                dsb = ds.astype(BF16)
                dv_sc[kb] += _dot_tn(p.astype(BF16), dob)
                dfk_sc[kb] += _dot_tn(dsb, fq) * SCALE
                dc_ref[kb] += jnp.broadcast_to(-_colsum(ds), (8, tk))
                return dfq + _dot(dsb, fk) * SCALE

            dfq = lax.fori_loop(0, (qb + 1) * rr, kloop, jnp.zeros((tq, HD), F32))
            wacc_ref[0:1, :] += _colsum(dfq * qn)
            dq_ref[qb] = _rms_bwd(qn, qr, dfq * g0).astype(BF16)
            return carry

        lax.fori_loop(0, nq, qloop, 0)

        def kfin(kb, c):
            kn, kr = _rms_rows(k_ref[kb])
            dfk = dfk_sc[kb]
            wacc_ref[1:2, :] += _colsum(dfk * kn)
            dk_ref[kb] = _rms_bwd(kn, kr, dfk * g1).astype(BF16)
            return c

        lax.fori_loop(0, nk, kfin, 0)
        dv_ref[...] = dv_sc[...].astype(BF16)

    qs = pl.BlockSpec((None, None, nq, tq, HD), lambda b, h: (b, h, 0, 0, 0))
    ks = pl.BlockSpec((None, None, nk, tk, HD), lambda b, h: (b, h, 0, 0, 0))
    cqs = pl.BlockSpec((None, None, nq, tq, 1), lambda b, h: (b, h, 0, 0, 0))
    cks = pl.BlockSpec((None, None, nk, 1, tk), lambda b, h: (b, h, 0, 0, 0))
    return pl.pallas_call(
        body, name=f"fox_bwd_{l}",
        grid=(B, H),
        in_specs=[qs, ks, ks, cqs, cks, pl.BlockSpec((None, 8, HD), lambda b, h: (l, 0, 0)), qs, cqs],
        out_specs=[qs, ks, ks,
                   pl.BlockSpec((None, None, nk, 8, tk), lambda b, h: (b, h, 0, 0, 0)),
                   pl.BlockSpec((8, HD), lambda b, h: (0, 0))],
        out_shape=[jax.ShapeDtypeStruct((B, H, nq, tq, HD), BF16),
                   jax.ShapeDtypeStruct((B, H, nk, tk, HD), BF16),
                   jax.ShapeDtypeStruct((B, H, nk, tk, HD), BF16),
                   jax.ShapeDtypeStruct((B, H, nk, 8, tk), F32),
                   jax.ShapeDtypeStruct((8, HD), F32)],
        scratch_shapes=[pltpu.VMEM((nk, tk, HD), BF16), pltpu.VMEM((nk, tk, HD), F32),
                        pltpu.VMEM((nk, tk, HD), F32)],
        compiler_params=_cp(("arbitrary", "arbitrary"), VMEM_BIG),
    )(q, k, v, cq, ck, gqk, do, lse)


SBQ_BLK, SBK_BLK, SBV_BLK = 8, 10, 12
FXQ_BLK, FXK_BLK, FXV_BLK = 14, 16, 18
PAIR = 2 * HD


def _lane_masks():
    lane = lax.broadcasted_iota(jnp.int32, (1, PAIR), 1)
    return lane, lane < HD


def _pair_select(m0, a0, a1):
    return jnp.where(m0, a0, a1)


def _pair_split(x, m0):
    return jnp.where(m0, x, 0.0).astype(BF16), jnp.where(m0, 0.0, x).astype(BF16)


def _pair_mean(x, m0):
    s0 = _rowsum(jnp.where(m0, x, 0.0))
    s1 = _rowsum(x) - s0
    return jnp.where(m0, s0, s1) * (1.0 / HD)


def _pair_rms(x, m0):
    rstd = lax.rsqrt(_pair_mean(x * x, m0) + EPS)
    return x * rstd, rstd


def _pair_rms_bwd(xn, rstd, dyn, m0):
    return rstd * (dyn - xn * _pair_mean(dyn * xn, m0))


def _logsig2(z):
    l1p = jnp.log(1.0 + jnp.exp(-jnp.abs(z)))
    lb = jnp.minimum(z, 0.0) - l1p
    return lb, lb - z


def _rows(ref, blk, size):
    return ref[pl.ds(pl.multiple_of(blk * size, size), size), :]


def _sbp_fwd(proj, l, S):
    T = proj.shape[0]
    tq, tk = TQ_(S), TK_(S)
    assert tq == 2 * tk
    nq = S // tq

    def body(q_ref, k_ref, v_ref, o_ref, t1_ref, kb_sc, vb_sc):
        kb_sc[...] = k_ref[...].astype(BF16)
        vb_sc[...] = v_ref[...].astype(BF16)
        lane, m0 = _lane_masks()
        tri = _tri(tk, "row_gt_col")
        ti = lax.broadcasted_iota(jnp.int32, (tq, 1), 0)
        si = lax.broadcasted_iota(jnp.int32, (1, tk), 1)

        def qloop(qb, carry):
            qh = _pair_split(_rows(q_ref, qb, tq) * SCALE, m0)
            tpos = qb * tq + ti

            def step(kbs, c, masked):
                pre = []
                for h in range(2):
                    for kb in kbs:
                        z = _dot_nt(qh[h], _rows(kb_sc, kb, tk))
                        lb, l1 = _logsig2(z)
                        past = None
                        if masked:
                            past = (kb * tk + si) < tpos
                            l1 = jnp.where(past, l1, 0.0)
                        pre.append((lb, l1, _cumsum_mm(l1, tri, parts=2), past))
                out = []
                for h in range(2):
                    acc, run = c[h]
                    for n, kb in enumerate(kbs):
                        lb, l1, cs, past = pre[2 * h + n]
                        w = jnp.exp(lb + (cs + run))
                        if masked:
                            w = jnp.where(past, w, 0.0)
                        acc = acc + _dot(w.astype(BF16), _rows(vb_sc, kb, tk))
                        run = run + (cs[:, 0:1] + l1[:, 0:1])
                    out.append((acc, run))
                return tuple(out)

            zero = (jnp.zeros((tq, PAIR), F32), jnp.zeros((tq, 1), F32))
            c = step((2 * qb + 1, 2 * qb), (zero, zero), True)
            c = lax.fori_loop(0, qb, lambda i, cc: step((2 * (qb - i) - 1, 2 * (qb - i) - 2), cc, False), c)
            r0 = pl.multiple_of(qb * tq, tq)
            o_ref[pl.ds(r0, tq), :] = _pair_select(m0, c[0][0], c[1][0])
            t1_ref[pl.ds(r0, tq), :] = jnp.where(lane == 0, c[0][1], jnp.where(lane == 1, c[1][1], 0.0))
            return carry

        lax.fori_loop(0, nq, qloop, 0)

    def col(blk):
        return pl.BlockSpec((S, PAIR), lambda b, p: (b, blk + p))

    return pl.pallas_call(
        body, name=f"sb_fwd_{l}",
        grid=(T // S, 2),
        in_specs=[col(SBQ_BLK), col(SBK_BLK), col(SBV_BLK)],
        out_specs=[col(0), col(0)],
        out_shape=[jax.ShapeDtypeStruct((T, AW), F32), jax.ShapeDtypeStruct((T, AW), F32)],
        scratch_shapes=[pltpu.VMEM((S, PAIR), BF16), pltpu.VMEM((S, PAIR), BF16)],
        compiler_params=_cp(("arbitrary", "arbitrary"), VMEM_BIG),
    )(proj, proj, proj)


def _sbp_bwd(proj, do, t1, l, S):
    T = proj.shape[0]
    tq, tk = TQ_(S), TK_(S)
    assert tq == 2 * tk
    nq = S // tq

    def body(q_ref, k_ref, v_ref, do_ref, t1_ref, dq_ref, dk_ref, dv_ref, kb_sc, vb_sc, dk_sc, dv_sc):
        kb_sc[...] = k_ref[...].astype(BF16)
        vb_sc[...] = v_ref[...].astype(BF16)
        dk_sc[...] = jnp.zeros_like(dk_sc)
        dv_sc[...] = jnp.zeros_like(dv_sc)
        _, m0 = _lane_masks()
        tri_in = _tri(tk, "row_le_col")
        tri_ex = _tri(tk, "row_lt_col")
        ti = lax.broadcasted_iota(jnp.int32, (tq, 1), 0)
        si = lax.broadcasted_iota(jnp.int32, (1, tk), 1)

        def qloop(qb, carry):
            qh = _pair_split(_rows(q_ref, qb, tq) * SCALE, m0)
            doh = _pair_split(_rows(do_ref, qb, tq), m0)
            t1v = _rows(t1_ref, qb, tq)
            tot = (t1v[:, 0:1], t1v[:, 1:2])
            tpos = qb * tq + ti

            def step(kbs, c, masked):
                pre = []
                for h in range(2):
                    for kb in kbs:
                        kk = _rows(kb_sc, kb, tk)
                        z = _dot_nt(qh[h], kk)
                        lb, l1 = _logsig2(z)
                        past = None
                        if masked:
                            past = (kb * tk + si) < tpos
                            l1 = jnp.where(past, l1, 0.0)
                        sig = jnp.exp(lb)
                        pre.append((lb, sig, _cumsum_mm(l1, tri_in), _dot_nt(doh[h], _rows(vb_sc, kb, tk)), past, kk))
                out = []
                for h in range(2):
                    dq, run1, rung = c[h]
                    for n, kb in enumerate(kbs):
                        lb, sig, p1, dw, past, kk = pre[2 * h + n]
                        w = jnp.exp(lb + (tot[h] - (run1 + p1)))
                        if masked:
                            w = jnp.where(past, w, 0.0)
                        gm = w * dw
                        cx = _cumsum_mm(gm, tri_ex, parts=2)
                        dz = gm - (gm + (rung + cx)) * sig
                        if masked:
                            dz = jnp.where(past, dz, 0.0)
                        dz = dz.astype(BF16)
                        r = pl.ds(pl.multiple_of(kb * tk, tk), tk)
                        dv_sc[r, :] += _dot_tn(w.astype(BF16), doh[h])
                        dk_sc[r, :] += _dot_tn(dz, qh[h])
                        dq = dq + _dot(dz, kk)
                        run1 = run1 + p1[:, tk - 1:tk]
                        rung = rung + (cx[:, tk - 1:tk] + gm[:, tk - 1:tk])
                    out.append((dq, run1, rung))
                return tuple(out)

            z1 = jnp.zeros((tq, 1), F32)
            zero = (jnp.zeros((tq, PAIR), F32), z1, z1)
            c = lax.fori_loop(0, qb, lambda i, cc: step((2 * i, 2 * i + 1), cc, False), (zero, zero))
            c = step((2 * qb, 2 * qb + 1), c, True)
            r0 = pl.multiple_of(qb * tq, tq)
            dq_ref[pl.ds(r0, tq), :] = (_pair_select(m0, c[0][0], c[1][0]) * SCALE).astype(BF16)
            return carry

        lax.fori_loop(0, nq, qloop, 0)
        dk_ref[...] = dk_sc[...].astype(BF16)
        dv_ref[...] = dv_sc[...].astype(BF16)

    def col(blk):
        return pl.BlockSpec((S, PAIR), lambda b, p: (b, blk + p))

    sh = jax.ShapeDtypeStruct((T, AW), BF16)
    return pl.pallas_call(
        body, name=f"sb_bwd_{l}",
        grid=(T // S, 2),
        in_specs=[col(SBQ_BLK), col(SBK_BLK), col(SBV_BLK), col(0), col(0)],
        out_specs=[col(0), col(0), col(0)],
        out_shape=[sh, sh, sh],
        scratch_shapes=[pltpu.VMEM((S, PAIR), BF16), pltpu.VMEM((S, PAIR), BF16),
                        pltpu.VMEM((S, PAIR), F32), pltpu.VMEM((S, PAIR), F32)],
        compiler_params=_cp(("arbitrary", "arbitrary"), VMEM_BIG),
    )(proj, proj, proj, do, t1)


def _foxp_fwd(proj, cum, ck, gqk2, l, S):
    T = proj.shape[0]
    tq, tk = TQ_(S), TK_(S)
    assert tq == 2 * tk
    nq, nk = S // tq, S // tk

    def body(q_ref, k_ref, v_ref, cum_ref, ck_ref, g_ref, o_ref, nl_ref, fk_sc, vb_sc):
        lane, m0 = _lane_masks()
        p = pl.program_id(1)
        kn, _ = _pair_rms(k_ref[...], m0)
        fk_sc[...] = (kn * g_ref[1:2, :]).astype(BF16)
        vb_sc[...] = v_ref[...].astype(BF16)
        ti = lax.broadcasted_iota(jnp.int32, (tq, 1), 0)
        si = lax.broadcasted_iota(jnp.int32, (1, tk), 1)

        def qloop(qb, carry):
            qn, _ = _pair_rms(_rows(q_ref, qb, tq), m0)
            fqh = _pair_split(qn * (g_ref[0:1, :] * SCALE), m0)
            cumv = _rows(cum_ref, qb, tq)
            cq = [_rowsum(jnp.where(lane == 2 * p + h, cumv, 0.0)) for h in range(2)]
            tpos = qb * tq + ti

            def step(kbs, c, masked):
                out = []
                for h in range(2):
                    m, lsum, acc = c[h]
                    ss = []
                    for kb in kbs:
                        s = _dot_nt(fqh[h], _rows(fk_sc, kb, tk)) + (cq[h] - ck_ref[h, kb])
                        if masked:
                            s = jnp.where((kb * tk + si) <= tpos, s, NEG)
                        ss.append(s)
                    m2 = jnp.maximum(m, jnp.maximum(jnp.max(ss[0], axis=1, keepdims=True),
                                                    jnp.max(ss[1], axis=1, keepdims=True)))
                    al = jnp.exp(m - m2)
                    lsum = al * lsum
                    acc = al * acc
                    for s, kb in zip(ss, kbs):
                        pr = jnp.exp(s - m2)
                        lsum = lsum + _rowsum(pr)
                        acc = acc + _dot(pr.astype(BF16), _rows(vb_sc, kb, tk))
                    out.append((m2, lsum, acc))
                return tuple(out)

            zero = (jnp.full((tq, 1), NEG, F32), jnp.zeros((tq, 1), F32), jnp.zeros((tq, PAIR), F32))
            c = lax.fori_loop(0, qb, lambda i, cc: step((2 * i, 2 * i + 1), cc, False), (zero, zero))
            c = step((2 * qb, 2 * qb + 1), c, True)
            r0 = pl.multiple_of(qb * tq, tq)
            o_ref[pl.ds(r0, tq), :] = _pair_select(m0, c[0][2] / c[0][1], c[1][2] / c[1][1])
            nl = [cq[h] - (c[h][0] + jnp.log(c[h][1])) for h in range(2)]
            nl_ref[pl.ds(r0, tq), :] = jnp.where(lane == 0, nl[0], jnp.where(lane == 1, nl[1], 0.0))
            return carry

        lax.fori_loop(0, nq, qloop, 0)

    def col(blk):
        return pl.BlockSpec((S, PAIR), lambda b, p: (b, blk + p))

    return pl.pallas_call(
        body, name=f"fox_fwd_{l}",
        grid=(T // S, 2),
        in_specs=[col(FXQ_BLK), col(FXK_BLK), col(FXV_BLK),
                  pl.BlockSpec((S, 128), lambda b, p: (b, 0)),
                  pl.BlockSpec((None, 2, nk, 1, tk), lambda b, p: (b, p, 0, 0, 0)),
                  pl.BlockSpec((None, 8, PAIR), lambda b, p: (l, 0, 0))],
        out_specs=[col(0), col(0)],
        out_shape=[jax.ShapeDtypeStruct((T, AW), F32), jax.ShapeDtypeStruct((T, AW), F32)],
        scratch_shapes=[pltpu.VMEM((S, PAIR), BF16), pltpu.VMEM((S, PAIR), BF16)],
        compiler_params=_cp(("arbitrary", "arbitrary"), VMEM_BIG),
    )(proj, proj, proj, cum, ck, gqk2)


def _foxp_bwd(proj, do, nl, ck, gqk2, l, S):
    T = proj.shape[0]
    tq, tk = TQ_(S), TK_(S)
    assert tq == 2 * tk
    nq, nk = S // tq, S // tk

    def body(q_ref, k_ref, v_ref, do_ref, nl_ref, ck_ref, g_ref,
             dq_ref, dk_ref, dv_ref, dc_ref, wacc_ref, fk_sc, vb_sc, dfk_sc, dv_sc):
        @pl.when((pl.program_id(0) == 0) & (pl.program_id(1) == 0))
        def _():
            wacc_ref[...] = jnp.zeros_like(wacc_ref)

        _, m0 = _lane_masks()
        g0 = g_ref[0:1, :]
        g1 = g_ref[1:2, :]
        kn, kr = _pair_rms(k_ref[...], m0)
        fk_sc[...] = (kn * g1).astype(BF16)
        vb_sc[...] = v_ref[...].astype(BF16)
        dfk_sc[...] = jnp.zeros_like(dfk_sc)
        dv_sc[...] = jnp.zeros_like(dv_sc)
        dc_ref[...] = jnp.zeros_like(dc_ref)
        ti = lax.broadcasted_iota(jnp.int32, (tq, 1), 0)
        si = lax.broadcasted_iota(jnp.int32, (1, tk), 1)

        def qloop(qb, carry):
            qn, qr = _pair_rms(_rows(q_ref, qb, tq), m0)
            fqh = _pair_split(qn * (g0 * SCALE), m0)
            doh = _pair_split(_rows(do_ref, qb, tq), m0)
            nlv = _rows(nl_ref, qb, tq)
            cql = (nlv[:, 0:1], nlv[:, 1:2])
            tpos = qb * tq + ti

            def probs(h, kb, masked):
                s = _dot_nt(fqh[h], _rows(fk_sc, kb, tk)) + (cql[h] - ck_ref[h, kb])
                pr = jnp.exp(s)
                if masked:
                    pr = jnp.where((kb * tk + si) <= tpos, pr, 0.0)
                return pr, _dot_nt(doh[h], _rows(vb_sc, kb, tk))

            def dstep(kbs, c, masked):
                out = []
                for h in range(2):
                    acc = c[h]
                    for kb in kbs:
                        pr, dp = probs(h, kb, masked)
                        acc = acc + _rowsum(pr * dp)
                    out.append(acc)
                return tuple(out)

            z1 = jnp.zeros((tq, 1), F32)
            dlt = lax.fori_loop(0, qb, lambda i, cc: dstep((2 * i, 2 * i + 1), cc, False), (z1, z1))
            dlt = dstep((2 * qb, 2 * qb + 1), dlt, True)

            def step(kbs, c, masked):
                out = []
                for h in range(2):
                    dfq = c[h]
                    for kb in kbs:
                        pr, dp = probs(h, kb, masked)
                        ds = pr * (dp - dlt[h])
                        dsb = ds.astype(BF16)
                        r = pl.ds(pl.multiple_of(kb * tk, tk), tk)
                        dv_sc[r, :] += _dot_tn(pr.astype(BF16), doh[h])
                        dfk_sc[r, :] += _dot_tn(dsb, fqh[h])
                        dc_ref[h, kb] += jnp.broadcast_to(-_colsum(ds), (8, tk))
                        dfq = dfq + _dot(dsb, _rows(fk_sc, kb, tk))
                    out.append(dfq)
                return tuple(out)

            zq = jnp.zeros((tq, PAIR), F32)
            c = lax.fori_loop(0, qb, lambda i, cc: step((2 * i, 2 * i + 1), cc, False), (zq, zq))
            c = step((2 * qb, 2 * qb + 1), c, True)
            dfq = _pair_select(m0, c[0], c[1]) * SCALE
            wacc_ref[0:1, :] += _colsum(dfq * qn)
            r0 = pl.multiple_of(qb * tq, tq)
            dq_ref[pl.ds(r0, tq), :] = _pair_rms_bwd(qn, qr, dfq * g0, m0).astype(BF16)
            return carry

        lax.fori_loop(0, nq, qloop, 0)
        dfk = dfk_sc[...]
        wacc_ref[1:2, :] += _colsum(dfk * kn)
        dk_ref[...] = _pair_rms_bwd(kn, kr, dfk * g1, m0).astype(BF16)
        dv_ref[...] = dv_sc[...].astype(BF16)

    def col(blk):
        return pl.BlockSpec((S, PAIR), lambda b, p: (b, blk + p))

    sh = jax.ShapeDtypeStruct((T, AW), BF16)
    return pl.pallas_call(
        body, name=f"fox_bwd_{l}",
        grid=(T // S, 2),
        in_specs=[col(FXQ_BLK), col(FXK_BLK), col(FXV_BLK), col(0), col(0),
                  pl.BlockSpec((None, 2, nk, 1, tk), lambda b, p: (b, p, 0, 0, 0)),
                  pl.BlockSpec((None, 8, PAIR), lambda b, p: (l, 0, 0))],
        out_specs=[col(0), col(0), col(0),
                   pl.BlockSpec((None, 2, nk, 8, tk), lambda b, p: (b, p, 0, 0, 0)),
                   pl.BlockSpec((8, PAIR), lambda b, p: (0, 0))],
        out_shape=[sh, sh, sh,
                   jax.ShapeDtypeStruct((T // S, NH, nk, 8, tk), F32),
                   jax.ShapeDtypeStruct((8, PAIR), F32)],
        scratch_shapes=[pltpu.VMEM((S, PAIR), BF16), pltpu.VMEM((S, PAIR), BF16),
                        pltpu.VMEM((S, PAIR), F32), pltpu.VMEM((S, PAIR), F32)],
        compiler_params=_cp(("arbitrary", "arbitrary"), VMEM_BIG),
    )(proj, proj, proj, do, nl, ck, gqk2)


def _transpose_blocks(src_ref, dst_sc, nblk, blk):
    for kb in range(nblk):
        dst_sc[kb] = src_ref[kb * blk:(kb + 1) * blk, :].astype(F32).T.astype(BF16)


def _sbq_fwd(proj, l, S, comm=None):
    T = proj.shape[0]
    tb = TQ_(S)
    nb = S // tb
    nbat = T // S
    c_args, c_specs, c_outs, c_scr = _hosted(comm)
    n_ci, n_co = len(c_args), len(c_outs)

    def body(*refs):
        q_ref, k_ref, v_ref = refs[:3]
        c_in = refs[3:3 + n_ci]
        o_ref, t1_ref = refs[3 + n_ci:5 + n_ci]
        c_out = refs[5 + n_ci:5 + n_ci + n_co]
        kt_sc, vb_sc = refs[5 + n_ci + n_co:7 + n_ci + n_co]
        c_sems = refs[7 + n_ci + n_co:]
        step = pl.program_id(0) * 2 + pl.program_id(1)
        if comm is not None:
            @pl.when(step == 0)
            def _():
                comm["start"](c_in, c_out, c_sems)

        _transpose_blocks(k_ref, kt_sc, nb, tb)
        vb_sc[...] = v_ref[...].astype(BF16)
        lane, m0 = _lane_masks()
        tri = _tri(tb, "row_gt_col")
        past = lax.broadcasted_iota(jnp.int32, (tb, tb), 1) < lax.broadcasted_iota(jnp.int32, (tb, tb), 0)

        def qloop(qb, carry):
            qh = _pair_split(_rows(q_ref, qb, tb) * SCALE, m0)

            def scores(kb):
                return tuple(_dot(qh[h], kt_sc[kb]) for h in range(2))

            def block(kb, kb_next, z, c, masked):
                mid = []
                for h in range(2):
                    lb, l1 = _logsig2(z[h])
                    if masked:
                        l1 = jnp.where(past, l1, 0.0)
                    mid.append((lb, l1, _cumsum_mm(l1, tri, parts=2)))
                z_next = scores(kb_next)
                pv, runs = [], []
                for h in range(2):
                    lb, l1, cs = mid[h]
                    w = jnp.exp(lb + (cs + c[h][1]))
                    if masked:
                        w = jnp.where(past, w, 0.0)
                    pv.append(_dot(w.astype(BF16), _rows(vb_sc, kb, tb)))
                    runs.append(c[h][1] + (cs[:, 0:1] + l1[:, 0:1]))
                return z_next, tuple((c[h][0] + pv[h], runs[h]) for h in range(2))

            zero = (jnp.zeros((tb, PAIR), F32), jnp.zeros((tb, 1), F32))
            z, c = block(qb, jnp.maximum(qb - 1, 0), scores(qb), (zero, zero), True)

            def off_diag(i, zc):
                kb = qb - 1 - i
                return block(kb, jnp.maximum(kb - 1, 0), zc[0], zc[1], False)

            _, c = lax.fori_loop(0, qb, off_diag, (z, c))
            r0 = pl.multiple_of(qb * tb, tb)
            o_ref[pl.ds(r0, tb), :] = _pair_select(m0, c[0][0], c[1][0])
            t1_ref[pl.ds(r0, tb), :] = jnp.where(lane == 0, c[0][1], jnp.where(lane == 1, c[1][1], 0.0))
            return carry

        lax.fori_loop(0, nb, qloop, 0)
        if comm is not None:
            @pl.when(step == 2 * nbat - 1)
            def _():
                comm["finish"](c_in, c_out, c_sems)

    def col(blk):
        return pl.BlockSpec((S, PAIR), lambda b, p: (b, blk + p))

    anyspec = pl.BlockSpec(memory_space=pl.ANY)
    out = pl.pallas_call(
        body, name=f"sb_fwd_{l}",
        grid=(nbat, 2),
        in_specs=[col(SBQ_BLK), col(SBK_BLK), col(SBV_BLK)] + c_specs,
        out_specs=[col(0), col(0)] + [anyspec] * n_co,
        out_shape=[jax.ShapeDtypeStruct((T, AW), F32), jax.ShapeDtypeStruct((T, AW), F32)] + c_outs,
        scratch_shapes=[pltpu.VMEM((nb, PAIR, tb), BF16), pltpu.VMEM((S, PAIR), BF16)] + c_scr,
        compiler_params=_cp(("arbitrary", "arbitrary"), VMEM_BIG),
    )(proj, proj, proj, *c_args)
    return out[0], out[1], list(out[2:])


def _sbq_bwd(proj, do, t1, l, S, comm=None):
    T = proj.shape[0]
    tb = TQ_(S)
    nb = S // tb
    nbat = T // S
    c_args, c_specs, c_outs, c_scr = _hosted(comm)
    n_ci, n_co = len(c_args), len(c_outs)

    def body(*refs):
        q_ref, k_ref, v_ref, do_ref, t1_ref = refs[:5]
        c_in = refs[5:5 + n_ci]
        dq_ref, dk_ref, dv_ref = refs[5 + n_ci:8 + n_ci]
        c_out = refs[8 + n_ci:8 + n_ci + n_co]
        kb_sc, kt_sc, vt_sc, dkt_sc, dvt_sc = refs[8 + n_ci + n_co:13 + n_ci + n_co]
        c_sems = refs[13 + n_ci + n_co:]
        step = pl.program_id(0) * 2 + pl.program_id(1)
        if comm is not None:
            @pl.when(step == 0)
            def _():
                comm["start"](c_in, c_out, c_sems)

        kb_sc[...] = k_ref[...].astype(BF16)
        _transpose_blocks(k_ref, kt_sc, nb, tb)
        _transpose_blocks(v_ref, vt_sc, nb, tb)
        dkt_sc[...] = jnp.zeros_like(dkt_sc)
        dvt_sc[...] = jnp.zeros_like(dvt_sc)
        _, m0 = _lane_masks()
        mt0 = lax.broadcasted_iota(jnp.int32, (PAIR, 1), 0) < HD
        tri_in = _tri(tb, "row_le_col")
        tri_ex = _tri(tb, "row_lt_col")
        past = lax.broadcasted_iota(jnp.int32, (tb, tb), 1) < lax.broadcasted_iota(jnp.int32, (tb, tb), 0)

        def qloop(qb, carry):
            qf = _rows(q_ref, qb, tb) * SCALE
            dof = _rows(do_ref, qb, tb)
            qh = _pair_split(qf, m0)
            doh = _pair_split(dof, m0)
            qth = _pair_split(qf.T, mt0)
            doth = _pair_split(dof.T, mt0)
            t1v = _rows(t1_ref, qb, tb)
            tot = (t1v[:, 0:1], t1v[:, 1:2])

            def block(kb, c, masked):
                hs = range(2)
                z = [_dot(qh[h], kt_sc[kb]) for h in hs]
                dw = [_dot(doh[h], vt_sc[kb]) for h in hs]
                st = []
                for h in hs:
                    lb, l1 = _logsig2(z[h])
                    if masked:
                        l1 = jnp.where(past, l1, 0.0)
                    st.append((lb, _cumsum_mm(l1, tri_in, parts=2)))
                mid = []
                for h in hs:
                    lb, p1 = st[h]
                    w = jnp.exp(lb + (tot[h] - (c[h][1] + p1)))
                    if masked:
                        w = jnp.where(past, w, 0.0)
                    gm = w * dw[h]
                    mid.append((w.astype(BF16), gm, _cumsum_mm(gm, tri_ex, parts=1)))
                out = []
                for h in hs:
                    dq, run1, rung = c[h]
                    wb, gm, cx = mid[h]
                    dz = gm - (gm + (rung + cx)) * jnp.exp(st[h][0])
                    if masked:
                        dz = jnp.where(past, dz, 0.0)
                    dz = dz.astype(BF16)
                    dvt_sc[kb] += _dot(doth[h], wb)
                    dkt_sc[kb] += _dot(qth[h], dz)
                    dq = dq + _dot(dz, _rows(kb_sc, kb, tb))
                    p1 = st[h][1]
                    out.append((dq, run1 + p1[:, tb - 1:tb], rung + (cx[:, tb - 1:tb] + gm[:, tb - 1:tb])))
                return tuple(out)

            z1 = jnp.zeros((tb, 1), F32)
            zero = (jnp.zeros((tb, PAIR), F32), z1, z1)
            c = lax.fori_loop(0, qb, lambda i, cc: block(i, cc, False), (zero, zero))
            c = block(qb, c, True)
            r0 = pl.multiple_of(qb * tb, tb)
            dq_ref[pl.ds(r0, tb), :] = (_pair_select(m0, c[0][0], c[1][0]) * SCALE).astype(BF16)
            return carry

        lax.fori_loop(0, nb, qloop, 0)
        for kb in range(nb):
            dk_ref[kb * tb:(kb + 1) * tb, :] = dkt_sc[kb].T.astype(BF16)
            dv_ref[kb * tb:(kb + 1) * tb, :] = dvt_sc[kb].T.astype(BF16)
        if comm is not None:
            @pl.when(step == 2 * nbat - 1)
            def _():
                comm["finish"](c_in, c_out, c_sems)

    def col(blk):
        return pl.BlockSpec((S, PAIR), lambda b, p: (b, blk + p))

    sh = jax.ShapeDtypeStruct((T, AW), BF16)
    anyspec = pl.BlockSpec(memory_space=pl.ANY)
    out = pl.pallas_call(
        body, name=f"sb_bwd_{l}",
        grid=(nbat, 2),
        in_specs=[col(SBQ_BLK), col(SBK_BLK), col(SBV_BLK), col(0), col(0)] + c_specs,
        out_specs=[col(0), col(0), col(0)] + [anyspec] * n_co,
        out_shape=[sh, sh, sh] + c_outs,
        scratch_shapes=[pltpu.VMEM((S, PAIR), BF16), pltpu.VMEM((nb, PAIR, tb), BF16), pltpu.VMEM((nb, PAIR, tb), BF16),
                        pltpu.VMEM((nb, PAIR, tb), F32), pltpu.VMEM((nb, PAIR, tb), F32)] + c_scr,
        compiler_params=_cp(("arbitrary", "arbitrary"), VMEM_BIG),
    )(proj, proj, proj, do, t1, *c_args)
    return out[0], out[1], out[2], list(out[3:])


def _foxq_fwd(proj, cum, ck, gqk2, l, S):
    T = proj.shape[0]
    tb = TQ_(S)
    nb = S // tb

    def body(q_ref, k_ref, v_ref, cum_ref, ck_ref, g_ref, o_ref, nl_ref, ox_ref, fk_sc, fkt_sc, vb_sc):
        lane, m0 = _lane_masks()
        p = pl.program_id(1)
        kn, _ = _pair_rms(k_ref[...], m0)
        fk_sc[...] = kn * g_ref[1:2, :]
        _transpose_blocks(fk_sc, fkt_sc, nb, tb)
        vb_sc[...] = v_ref[...].astype(BF16)
        causal = lax.broadcasted_iota(jnp.int32, (tb, tb), 1) <= lax.broadcasted_iota(jnp.int32, (tb, tb), 0)

        def qloop(qb, carry):
            qn, _ = _pair_rms(_rows(q_ref, qb, tb), m0)
            fqh = _pair_split(qn * (g_ref[0:1, :] * SCALE), m0)
            cumv = _rows(cum_ref, qb, tb)
            cq = [_rowsum(jnp.where(lane == 2 * p + h, cumv, 0.0)) for h in range(2)]

            def scores(kb):
                return tuple(_dot(fqh[h], fkt_sc[kb]) for h in range(2))

            def block(kb, kb_next, qk, c, masked):
                st = []
                for h in range(2):
                    s = qk[h] + (cq[h] - ck_ref[h, kb])
                    if masked:
                        s = jnp.where(causal, s, NEG)
                    m2 = jnp.maximum(c[h][0], jnp.max(s, axis=1, keepdims=True))
                    pr = jnp.exp(s - m2)
                    hi = pr.astype(BF16)
                    lo = (pr - hi.astype(F32)).astype(BF16)
                    vv = _rows(vb_sc, kb, tb)
                    st.append((m2, pr, _dot(hi, vv), _dot(lo, vv)))
                qk_next = scores(kb_next)
                out = []
                for h in range(2):
                    m, lsum, acc, rest = c[h]
                    m2, pr, pv, pv_lo = st[h]
                    al = jnp.exp(m - m2)
                    out.append((m2, al * lsum + _rowsum(pr), al * acc + pv, al * rest + pv_lo))
                return qk_next, tuple(out)

            zacc = jnp.zeros((tb, PAIR), F32)
            zero = (jnp.full((tb, 1), NEG, F32), jnp.zeros((tb, 1), F32), zacc, zacc)

            def off_diag(i, sc):
                return block(i, i + 1, sc[0], sc[1], False)

            qk, c = lax.fori_loop(0, qb, off_diag, (scores(0), (zero, zero)))
            _, c = block(qb, qb, qk, c, True)
            r0 = pl.multiple_of(qb * tb, tb)
            o_ref[pl.ds(r0, tb), :] = _pair_select(m0, c[0][2] / c[0][1], c[1][2] / c[1][1])
            ox_ref[pl.ds(r0, tb), :] = _pair_select(m0, (c[0][2] + c[0][3]) / c[0][1], (c[1][2] + c[1][3]) / c[1][1])
            nl = [cq[h] - (c[h][0] + jnp.log(c[h][1])) for h in range(2)]
            nl_ref[pl.ds(r0, tb), :] = jnp.where(lane == 0, nl[0], jnp.where(lane == 1, nl[1], 0.0))
            return carry

        lax.fori_loop(0, nb, qloop, 0)

    def col(blk):
        return pl.BlockSpec((S, PAIR), lambda b, p: (b, blk + p))

    return pl.pallas_call(
        body, name=f"fox_fwd_{l}",
        grid=(T // S, 2),
        in_specs=[col(FXQ_BLK), col(FXK_BLK), col(FXV_BLK),
                  pl.BlockSpec((S, 128), lambda b, p: (b, 0)),
                  pl.BlockSpec((None, 2, nb, 1, tb), lambda b, p: (b, p, 0, 0, 0)),
                  pl.BlockSpec((None, 8, PAIR), lambda b, p: (l, 0, 0))],
        out_specs=[col(0), col(0), col(0)],
        out_shape=[jax.ShapeDtypeStruct((T, AW), F32)] * 3,
        scratch_shapes=[pltpu.VMEM((S, PAIR), F32), pltpu.VMEM((nb, PAIR, tb), BF16), pltpu.VMEM((S, PAIR), BF16)],
        compiler_params=_cp(("arbitrary", "arbitrary"), VMEM_BIG),
    )(proj, proj, proj, cum, ck, gqk2)


def _foxq_bwd(proj, do, nl, ox, ck, gqk2, l, S):
    T = proj.shape[0]
    tb = TQ_(S)
    nb = S // tb

    def body(q_ref, k_ref, v_ref, do_ref, nl_ref, ox_ref, ck_ref, g_ref,
             dq_ref, dk_ref, dv_ref, dc_ref, wacc_ref, fk_sc, fkt_sc, vt_sc, dfkt_sc, dvt_sc):
        @pl.when((pl.program_id(0) == 0) & (pl.program_id(1) == 0))
        def _():
            wacc_ref[...] = jnp.zeros_like(wacc_ref)

        _, m0 = _lane_masks()
        mt0 = lax.broadcasted_iota(jnp.int32, (PAIR, 1), 0) < HD
        g0 = g_ref[0:1, :]
        g1 = g_ref[1:2, :]
        fk_sc[...] = (_pair_rms(k_ref[...], m0)[0] * g1).astype(BF16)
        _transpose_blocks(fk_sc, fkt_sc, nb, tb)
        _transpose_blocks(v_ref, vt_sc, nb, tb)
        dfkt_sc[...] = jnp.zeros_like(dfkt_sc)
        dvt_sc[...] = jnp.zeros_like(dvt_sc)
        dc_ref[...] = jnp.zeros_like(dc_ref)
        causal = lax.broadcasted_iota(jnp.int32, (tb, tb), 1) <= lax.broadcasted_iota(jnp.int32, (tb, tb), 0)

        def qloop(qb, carry):
            qn, qr = _pair_rms(_rows(q_ref, qb, tb), m0)
            fqf = qn * (g0 * SCALE)
            dof = _rows(do_ref, qb, tb)
            fqh = _pair_split(fqf, m0)
            doh = _pair_split(dof, m0)
            fqth = _pair_split(fqf.T, mt0)
            doth = _pair_split(dof.T, mt0)
            nlv = _rows(nl_ref, qb, tb)
            cql = (nlv[:, 0:1], nlv[:, 1:2])

            def probs(kb, masked):
                qk = [_dot(fqh[h], fkt_sc[kb]) for h in range(2)]
                dp = [_dot(doh[h], vt_sc[kb]) for h in range(2)]
                pr = []
                for h in range(2):
                    e = jnp.exp(qk[h] + (cql[h] - ck_ref[h, kb]))
                    pr.append(jnp.where(causal, e, 0.0) if masked else e)
                return pr, dp

            oxv = _rows(ox_ref, qb, tb)
            dlt = [_rowsum(doh[h].astype(F32) * oxv) for h in range(2)]

            def block(kb, c, masked):
                pr, dp = probs(kb, masked)
                out = []
                for h in range(2):
                    ds = pr[h] * (dp[h] - dlt[h])
                    dsb = ds.astype(BF16)
                    dvt_sc[kb] += _dot(doth[h], pr[h].astype(BF16))
                    dfkt_sc[kb] += _dot(fqth[h], dsb)
                    dc_ref[h, kb] += jnp.broadcast_to(-_colsum(ds), (8, tb))
                    out.append(c[h] + _dot(dsb, _rows(fk_sc, kb, tb)))
                return tuple(out)

            zq = jnp.zeros((tb, PAIR), F32)
            c = lax.fori_loop(0, qb, lambda i, cc: block(i, cc, False), (zq, zq))
            c = block(qb, c, True)
            dfq = _pair_select(m0, c[0], c[1]) * SCALE
            wacc_ref[0:1, :] += _colsum(dfq * qn)
            r0 = pl.multiple_of(qb * tb, tb)
            dq_ref[pl.ds(r0, tb), :] = _pair_rms_bwd(qn, qr, dfq * g0, m0).astype(BF16)
            return carry

        lax.fori_loop(0, nb, qloop, 0)
        for kb in range(nb):
            rows = slice(kb * tb, (kb + 1) * tb)
            dfk = dfkt_sc[kb].T
            knb, krb = _pair_rms(k_ref[rows, :], m0)
            wacc_ref[1:2, :] += _colsum(dfk * knb)
            dk_ref[rows, :] = _pair_rms_bwd(knb, krb, dfk * g1, m0).astype(BF16)
            dv_ref[rows, :] = dvt_sc[kb].T.astype(BF16)

    def col(blk):
        return pl.BlockSpec((S, PAIR), lambda b, p: (b, blk + p))

    sh = jax.ShapeDtypeStruct((T, AW), BF16)
    return pl.pallas_call(
        body, name=f"fox_bwd_{l}",
        grid=(T // S, 2),
        in_specs=[col(FXQ_BLK), col(FXK_BLK), col(FXV_BLK), col(0), col(0), col(0),
                  pl.BlockSpec((None, 2, nb, 1, tb), lambda b, p: (b, p, 0, 0, 0)),
                  pl.BlockSpec((None, 8, PAIR), lambda b, p: (l, 0, 0))],
        out_specs=[col(0), col(0), col(0),
                   pl.BlockSpec((None, 2, nb, 8, tb), lambda b, p: (b, p, 0, 0, 0)),
                   pl.BlockSpec((8, PAIR), lambda b, p: (0, 0))],
        out_shape=[sh, sh, sh,
                   jax.ShapeDtypeStruct((T // S, NH, nb, 8, tb), F32),
                   jax.ShapeDtypeStruct((8, PAIR), F32)],
        scratch_shapes=[pltpu.VMEM((S, PAIR), BF16), pltpu.VMEM((nb, PAIR, tb), BF16), pltpu.VMEM((nb, PAIR, tb), BF16),
                        pltpu.VMEM((nb, PAIR, tb), F32), pltpu.VMEM((nb, PAIR, tb), F32)],
        compiler_params=_cp(("arbitrary", "arbitrary"), VMEM_BIG),
    )(proj, proj, proj, do, nl, ox, ck, gqk2)


def _ada_fwd(c_all, w_ada, b_cols):
    nb, ncol = c_all.shape[0], w_ada.shape[2]
    tn = _tile(ncol, 768)

    def body(c_ref, w_ref, b_ref, o_ref):
        c = c_ref[...]
        ca = (c * _sigmoid(c)).astype(BF16)
        o_ref[...] = _dot(ca, w_ref[...].astype(BF16)) + b_ref[...]

    return pl.pallas_call(
        body, name="ada_fwd",
        grid=(2, ncol // tn),
        in_specs=[pl.BlockSpec((nb, D), lambda l, n: (0, 0)),
                  pl.BlockSpec((None, D, tn), lambda l, n: (l, 0, n)),
                  pl.BlockSpec((None, 1, tn), lambda l, n: (l, 0, n))],
        out_specs=pl.BlockSpec((None, nb, tn), lambda l, n: (l, 0, n)),
        out_shape=jax.ShapeDtypeStruct((2, nb, ncol), F32),
        compiler_params=_cp(("arbitrary", "arbitrary")),
    )(c_all, w_ada, b_cols)


def _ada_bwd(c_all, dmod_cols):
    nb, ncol = c_all.shape[0], dmod_cols.shape[2]
    tn = _tile(ncol, 768)

    def body(c_ref, d_ref, o_ref):
        c = c_ref[...]
        ca = (c * _sigmoid(c)).astype(BF16)
        o_ref[...] = _dot_tn(ca, d_ref[...].astype(BF16))

    return pl.pallas_call(
        body, name="ada_bwd",
        grid=(2, ncol // tn),
        in_specs=[pl.BlockSpec((nb, D), lambda l, n: (0, 0)),
                  pl.BlockSpec((None, nb, tn), lambda l, n: (l, 0, n))],
        out_specs=pl.BlockSpec((None, D, tn), lambda l, n: (l, 0, n)),
        out_shape=jax.ShapeDtypeStruct((2, D, ncol), F32),
        compiler_params=_cp(("arbitrary", "arbitrary")),
    )(c_all, dmod_cols)


def _sum_lead(a, name):
    n, R, C = a.shape
    tr = _tile_div8(R, 256)

    def body(a_ref, o_ref):
        acc = a_ref[0]
        for i in range(1, n):
            acc = acc + a_ref[i]
        o_ref[...] = acc

    return pl.pallas_call(
        body, name=name,
        grid=(R // tr,),
        in_specs=[pl.BlockSpec((n, tr, C), lambda i: (0, i, 0))],
        out_specs=pl.BlockSpec((tr, C), lambda i: (i, 0)),
        out_shape=jax.ShapeDtypeStruct((R, C), F32),
        compiler_params=_cp(("arbitrary",)),
    )(a)


def _adamw(w, g, m, v, name):
    R, C = w.shape
    tr = _tile_div8(R, max(8, (1 << 18) // C))
    c1 = 1.0 / (1.0 - ADAM_B1 ** ADAM_STEP)
    c2 = 1.0 / (1.0 - ADAM_B2 ** ADAM_STEP)

    def body(w_ref, g_ref, m_ref, v_ref, d_ref, mo_ref, vo_ref):
        gg = g_ref[...]
        mn = ADAM_B1 * m_ref[...] + (1.0 - ADAM_B1) * gg
        vn = ADAM_B2 * v_ref[...] + (1.0 - ADAM_B2) * (gg * gg)
        mo_ref[...] = mn
        vo_ref[...] = vn
        d_ref[...] = (-ADAM_LR) * ((mn * c1) / (jnp.sqrt(vn * c2) + ADAM_EPS) + ADAM_WD * w_ref[...])

    spec = pl.BlockSpec((tr, C), lambda i: (i, 0))
    sh = jax.ShapeDtypeStruct((R, C), F32)
    return pl.pallas_call(
        body, name=name, grid=(R // tr,),
        in_specs=[spec] * 4, out_specs=[spec] * 3, out_shape=[sh] * 3,
        compiler_params=_cp(("arbitrary",)),
    )(w, g, m, v)


def _coords():
    return lax.axis_index("x"), lax.axis_index("y"), lax.axis_index("c")


def _all_gather8(blk, name, vmem):
    m_per, n = blk.shape
    space = pltpu.VMEM if vmem else pl.ANY

    def body(x_ref, out_ref, send_sems, recv_sems, local_sem):
        x, y, c = _coords()
        me, sibling = (x, y, c), (x, y, 1 - c)
        chips = [(1 - x, y), (x, 1 - y), (1 - x, 1 - y)]

        def rows(px, py, pc):
            return out_ref.at[4 * px + 2 * py + pc]

        def copy(k, block, to, src=None):
            return pltpu.make_async_remote_copy(
                src_ref=rows(*block) if src is None else src, dst_ref=rows(*block),
                send_sem=send_sems.at[k], recv_sem=recv_sems.at[k], device_id=to, device_id_type=MESH)

        mine = pltpu.make_async_copy(x_ref, rows(*me), local_sem)
        mine.start()
        first = [copy(0, me, sibling, src=x_ref)]
        first += [copy(1 + j, me, (*chip, c), src=x_ref) for j, chip in enumerate(chips)]
        for cp in first:
            cp.start()
        passed = [copy(4 + j, (*chip, c), sibling) for j, chip in enumerate(chips)]
        for j, chip in enumerate(chips):
            copy(1 + j, (*chip, c), me).wait_recv()
            passed[j].start()
        copy(0, sibling, me).wait_recv()
        for j, chip in enumerate(chips):
            copy(4 + j, (*chip, 1 - c), me).wait_recv()
        for cp in first + passed:
            cp.wait_send()
        mine.wait()

    return pl.pallas_call(
        body, name=name,
        out_shape=jax.ShapeDtypeStruct((N_DEV, m_per, n), blk.dtype),
        in_specs=[pl.BlockSpec(memory_space=space)],
        out_specs=pl.BlockSpec(memory_space=space),
        scratch_shapes=[pltpu.SemaphoreType.DMA((7,)), pltpu.SemaphoreType.DMA((7,)), pltpu.SemaphoreType.DMA],
        compiler_params=pltpu.CompilerParams(vmem_limit_bytes=VMEM_BIG if vmem else None),
    )(blk)


def _run_comm(comm, name):
    n_in, n_out = len(comm["args"]), len(comm["out_shapes"])

    def body(*refs):
        parts = (refs[:n_in], refs[n_in:n_in + n_out], refs[n_in + n_out:])
        comm["start"](*parts)
        comm["finish"](*parts)

    anyspec = pl.BlockSpec(memory_space=pl.ANY)
    return pl.pallas_call(
        body, name=name, out_shape=comm["out_shapes"],
        in_specs=[anyspec] * n_in, out_specs=[anyspec] * n_out, scratch_shapes=comm["scratch"],
    )(*comm["args"])


def _hosted(comm):
    if comm is None:
        return [], [], [], []
    anyspec = pl.BlockSpec(memory_space=pl.ANY)
    return list(comm["args"]), [anyspec] * len(comm["args"]), list(comm["out_shapes"]), list(comm["scratch"])


def _ag_comm(wshards, pieces):
    n_piece = len(pieces)
    n_src = len(wshards)
    halves = [wshards[i].shape[len(lead)] // 2 for i, lead, _ in pieces]

    def plan(ins, outs, sems):
        send_sems, recv_sems, local_sems = sems
        x, y, c = _coords()
        me, sibling = (x, y, c), (x, y, 1 - c)
        chips = [(1 - x, y), (x, 1 - y), (1 - x, 1 - y)]

        def dsts(px, py, pc):
            s = 2 * px + py
            return [outs[p].at[s, pl.ds(pc * r2, r2)] if stacked else outs[p].at[pl.ds((2 * s + pc) * r2, r2)]
                    for p, ((_, _, stacked), r2) in enumerate(zip(pieces, halves))]

        srcs = [ins[i].at[(*lead, pl.ds(c * r2, r2))] for (i, lead, _), r2 in zip(pieces, halves)]

        def copies(k, block, to, own=False):
            d = dsts(*block)
            return [pltpu.make_async_remote_copy(
                src_ref=srcs[p] if own else d[p], dst_ref=d[p], send_sem=send_sems.at[k, p],
                recv_sem=recv_sems.at[k, p], device_id=to, device_id_type=MESH) for p in range(n_piece)]

        mine = [pltpu.make_async_copy(srcs[p], d, local_sems.at[p]) for p, d in enumerate(dsts(*me))]
        first = copies(0, me, sibling, own=True)
        for j, chip in enumerate(chips):
            first += copies(1 + j, me, (*chip, c), own=True)
        return me, sibling, chips, c, copies, mine, first

    def start(ins, outs, sems):
        *_, mine, first = plan(ins, outs, sems)
        for cp in mine + first:
            cp.start()

    def finish(ins, outs, sems):
        me, sibling, chips, c, copies, mine, first = plan(ins, outs, sems)
        passed = []
        for j, chip in enumerate(chips):
            for cp in copies(1 + j, (*chip, c), me):
                cp.wait_recv()
            fwd = copies(4 + j, (*chip, c), sibling)
            for cp in fwd:
                cp.start()
            passed += fwd
        for cp in copies(0, sibling, me):
            cp.wait_recv()
        for j, chip in enumerate(chips):
            for cp in copies(4 + j, (*chip, 1 - c), me):
                cp.wait_recv()
        for cp in first + passed:
            cp.wait_send()
        for cp in mine:
            cp.wait()

    out_shapes = []
    for (i, lead, stacked), r2 in zip(pieces, halves):
        cols = wshards[i].shape[-1]
        out_shapes.append(jax.ShapeDtypeStruct((N_SHARD, 2 * r2, cols) if stacked else (N_SHARD * 2 * r2, cols), BF16))
    assert n_src == 4
    return dict(
        args=list(wshards), out_shapes=out_shapes,
        scratch=[pltpu.SemaphoreType.DMA((7, n_piece)), pltpu.SemaphoreType.DMA((7, n_piece)),
                 pltpu.SemaphoreType.DMA((n_piece,))],
        start=start, finish=finish)


def _rs_to_chips_comm(hs):
    n = len(hs)

    def copies(h, r, sems):
        send_sems, recv_sems = sems
        x, y, c = _coords()
        chips = [(1 - x, y), (x, 1 - y), (1 - x, 1 - y)]
        return [pltpu.make_async_remote_copy(
            src_ref=h[p].at[2 * px + py], dst_ref=r[p].at[k], send_sem=send_sems.at[k, p], recv_sem=recv_sems.at[k, p],
            device_id=(px, py, c), device_id_type=MESH) for k, (px, py) in enumerate(chips) for p in range(n)]

    def start(h, r, sems):
        for cp in copies(h, r, sems):
            cp.start()

    def finish(h, r, sems):
        for cp in copies(h, r, sems):
            cp.wait()

    return dict(args=list(hs), out_shapes=[jax.ShapeDtypeStruct((3,) + h.shape[1:], h.dtype) for h in hs],
                scratch=[pltpu.SemaphoreType.DMA((3, n)), pltpu.SemaphoreType.DMA((3, n))],
                start=start, finish=finish)


def _rs_to_sibling(pieces, name):
    n = len(pieces)

    def body(*refs):
        g, r, (send_sems, recv_sems) = refs[:n], refs[n:2 * n], refs[2 * n:]
        x, y, c = _coords()
        cps = []
        for p in range(n):
            r2 = g[p].shape[1] // 2
            cps.append(pltpu.make_async_remote_copy(
                src_ref=g[p].at[:, pl.ds((1 - c) * r2, r2)], dst_ref=r[p], send_sem=send_sems.at[p],
                recv_sem=recv_sems.at[p], device_id=(x, y, 1 - c), device_id_type=MESH))
        for cp in cps:
            cp.start()
        for cp in cps:
            cp.wait()

    anyspec = pl.BlockSpec(memory_space=pl.ANY)
    return pl.pallas_call(
        body, name=name,
        out_shape=[jax.ShapeDtypeStruct((N_SHARD, g.shape[1] // 2, g.shape[2]), g.dtype) for g in pieces],
        in_specs=[anyspec] * n, out_specs=[anyspec] * n,
        scratch_shapes=[pltpu.SemaphoreType.DMA((n,)), pltpu.SemaphoreType.DMA((n,))],
    )(*pieces)


def _share_halves(tensors, places, r2s):
    n, no = len(places), len(tensors)

    def body(*refs):
        o, (send_sems, recv_sems) = refs[no:2 * no], refs[2 * no:]
        x, y, c = _coords()

        def half(p, hc):
            oi, lead = places[p]
            return o[oi].at[(*lead, pl.ds(hc * r2s[p], r2s[p]))]

        outs = [pltpu.make_async_remote_copy(
            src_ref=half(p, c), dst_ref=half(p, c), send_sem=send_sems.at[p], recv_sem=recv_sems.at[p],
            device_id=(x, y, 1 - c), device_id_type=MESH) for p in range(n)]
        for cp in outs:
            cp.start()
        for p in range(n):
            pltpu.make_async_remote_copy(
                src_ref=half(p, 1 - c), dst_ref=half(p, 1 - c), send_sem=send_sems.at[p], recv_sem=recv_sems.at[p],
                device_id=(x, y, 1 - c), device_id_type=MESH).wait_recv()
        for cp in outs:
            cp.wait_send()

    anyspec = pl.BlockSpec(memory_space=pl.ANY)
    return pl.pallas_call(
        body, name="share_halves",
        out_shape=[jax.ShapeDtypeStruct(t.shape, t.dtype) for t in tensors],
        in_specs=[anyspec] * no, out_specs=[anyspec] * no,
        input_output_aliases={i: i for i in range(no)},
        scratch_shapes=[pltpu.SemaphoreType.DMA((n,)), pltpu.SemaphoreType.DMA((n,))],
    )(*tensors)


def _add_rows(r2, cols, n_arrays):
    lanes = -(-cols // 128) * 128
    return _tile_div8(r2, max(16, (24 << 20) // (2 * n_arrays * lanes * 4)), mult=16)


def _add_sibling(pieces, recvs, cidx, name):
    n = len(pieces)
    _, R, C = pieces[0].shape
    r2 = R // 2
    tr = _add_rows(r2, C, 2 * n)
    nt = r2 // tr

    def body(c_ref, *refs):
        for p in range(n):
            refs[2 * n + p][...] = (refs[p][...] + refs[n + p][...].astype(F32)).astype(BF16)

    return pl.pallas_call(
        body, name=name,
        grid_spec=pltpu.PrefetchScalarGridSpec(
            num_scalar_prefetch=1, grid=(N_SHARD, nt),
            in_specs=[pl.BlockSpec((None, tr, C), lambda s, i, c_ref: (s, c_ref[0] * nt + i, 0))] * n
            + [pl.BlockSpec((None, tr, C), lambda s, i, c_ref: (s, i, 0))] * n,
            out_specs=[pl.BlockSpec((None, tr, C), lambda s, i, c_ref: (s, i, 0))] * n),
        out_shape=[jax.ShapeDtypeStruct((N_SHARD, r2, C), BF16)] * n,
        compiler_params=_cp(("arbitrary", "arbitrary"), VMEM_BIG),
    )(cidx, *pieces, *recvs)


def _add_chips_into(piece, recv_a, recv_b, sc, prev, shape, lead, name):
    _, R, C = piece.shape
    r2 = R // 2
    tr = _add_rows(r2, C, 4)
    nt = r2 // tr
    nl = len(lead)

    def body(sc_ref, p_ref, a_ref, b_ref, *rest):
        o_ref = rest[-1]
        acc = p_ref[...] + a_ref[...].astype(F32)
        for k in range(3):
            acc = acc + b_ref[k].astype(F32)
        o_ref[...] = acc

    in_specs = [pl.BlockSpec((None, tr, C), lambda i, sc_ref: (sc_ref[0], sc_ref[1] * nt + i, 0)),
                pl.BlockSpec((None, tr, C), lambda i, sc_ref: (sc_ref[0], i, 0)),
                pl.BlockSpec((3, tr, C), lambda i, sc_ref: (0, i, 0))]
    args = [sc, piece, recv_a, recv_b]
    aliases = {}
    if prev is not None:
        in_specs.append(pl.BlockSpec(memory_space=pl.ANY))
        args.append(prev)
        aliases = {4: 0}
    return pl.pallas_call(
        body, name=name,
        grid_spec=pltpu.PrefetchScalarGridSpec(
            num_scalar_prefetch=1, grid=(nt,), in_specs=in_specs,
            out_specs=pl.BlockSpec((None,) * nl + (tr, C), lambda i, sc_ref: (*lead, sc_ref[1] * nt + i, 0))),
        out_shape=jax.ShapeDtypeStruct(shape, F32),
        input_output_aliases=aliases,
        compiler_params=_cp(("arbitrary",), VMEM_BIG),
    )(*args)


def _pack_rows(parts, rows, dtype):
    flat = jnp.concatenate([p.reshape(-1).astype(dtype) for p in parts])
    return jnp.pad(flat, (0, rows * ROW - flat.shape[0])).reshape(rows, ROW)


def _unpack(flat, shapes):
    out, off = [], 0
    for sh in shapes:
        n = math.prod(sh)
        out.append(flat[off:off + n].reshape(sh))
        off += n
    return out


def _heads(t, B, S, blk):
    return t.reshape(B, S, NH, HD).transpose(0, 2, 1, 3).reshape(B, NH, S // blk, blk, HD)


def _unheads(t, B, S):
    return t.reshape(B, NH, S, HD).transpose(0, 2, 1, 3).reshape(B * S, AW)


def _block_diag(w):
    eye = jnp.eye(LW // HD, dtype=w.dtype)
    return jnp.einsum("lhij,hg->lhigj", w, eye).reshape(w.shape[0], LW, LW)


def _diag_blocks(w):
    nbk = LW // HD
    w4 = w.reshape(nbk, HD, nbk, HD)
    return jnp.stack([w4[h, :, h, :] for h in range(nbk)])


def _rows8(rows, width):
    z = jnp.zeros((width,), F32)
    return jnp.stack(list(rows) + [z] * (8 - len(rows)))


def kernel(x, c, w_ada, b_ada, g_norm, w_ffn_up, w_ffn_down, w_in, b_fgate, conv_w, conv_b, w_rgate, b_rgate, w_igate, b_igate, lru_lambda, g_qk, g_mix_out, w_out, loss_target, m_w_ada, m_b_ada, m_g_norm, m_w_ffn_up, m_w_ffn_down, m_w_in, m_b_fgate, m_conv_w, m_conv_b, m_w_rgate, m_b_rgate, m_w_igate, m_b_igate, m_lru_lambda, m_g_qk, m_g_mix_out, m_w_out, v_w_ada, v_b_ada, v_g_norm, v_w_ffn_up, v_w_ffn_down, v_w_in, v_b_fgate, v_conv_w, v_conv_b, v_w_rgate, v_b_rgate, v_w_igate, v_b_igate, v_lru_lambda, v_g_qk, v_g_mix_out, v_w_out):
    B, S, _ = x.shape
    T = B * S
    xi, yi, ci = _coords()
    sidx = 2 * xi + yi
    didx = 4 * xi + 2 * yi + ci
    ada_cols = w_ada.shape[2]
    gn_cols = g_norm.shape[2]
    cw_cols = conv_w.shape[2]
    n_all = B * N_DEV

    blk1 = _pack_rows([c, jnp.pad(g_norm.reshape(-1), (0, 2 * ROW - g_norm.size)), conv_w], 8, F32)
    ag1 = _all_gather8(blk1, "ag_small_in", True)
    c_all = ag1[:, 0:B].reshape(n_all, D)
    chip_rows = ag1[0::2]
    g_norm_full = chip_rows[:, 2:4].reshape(N_SHARD, 2 * ROW)[:, :g_norm.size] \
        .reshape(N_SHARD, 2, 3, gn_cols).transpose(1, 2, 0, 3).reshape(2, 3, D)
    conv_w_full = chip_rows[:, 4].reshape(N_SHARD, 2, 4, cw_cols).transpose(1, 2, 0, 3).reshape(2, 4, LW)

    b_cols = lax.dynamic_slice(b_ada, (0, sidx * ada_cols), (2, ada_cols)).reshape(2, 1, ada_cols)
    mod_cols = _ada_fwd(c_all, w_ada, b_cols)
    mrows = (2 * n_all * ada_cols) // ROW
    ag2 = _all_gather8(mod_cols.reshape(mrows, ROW), "ag_mod", True)
    mod_sh = ag2[0::2].reshape(N_SHARD, 2, n_all, ada_cols)
    mod_me = lax.dynamic_slice(mod_sh, (0, 0, didx * B, 0), (N_SHARD, 2, B, ada_cols))
    mod_me = mod_me.transpose(1, 2, 0, 3).reshape(2, B, 3, 3, D)
    zrow = jnp.zeros((B, D), F32)
    mods = [[jnp.stack([mod_me[l, :, j, 0], 1.0 + mod_me[l, :, j, 1], 1.0 + mod_me[l, :, j, 2],
                        jnp.broadcast_to(g_norm_full[l, j], (B, D)), zrow, zrow, zrow, zrow], axis=1)
             for j in range(3)] for l in range(2)]

    wshards = (w_ffn_up.astype(BF16), w_ffn_down.astype(BF16), w_in.astype(BF16), w_out.astype(BF16))

    def ffn_pieces(l, j):
        return [(0, (l, j), True), (1, (l, j), False)]

    def mixer_pieces(l):
        return [(2, (l,), True), (3, (l,), False)]

    def ffn_weights(up, dn):
        return dict(up=up, dn=dn, up_t=jnp.swapaxes(up, 1, 2), dn_t=dn.T)

    def mixer_weights(g_in, g_out):
        return dict(inp=jnp.pad(g_in.transpose(1, 0, 2).reshape(D, N_IN), ((0, 0), (0, N_INP - N_IN))), out=g_out)

    wl = [dict(), dict()]
    wl[0][0] = ffn_weights(*_run_comm(_ag_comm(wshards, ffn_pieces(0, 0)), "ag_weights_0_0"))

    wr_d = _block_diag(w_rgate).astype(BF16)
    wi_d = _block_diag(w_igate).astype(BF16)
    cw8 = jnp.pad(conv_w_full, ((0, 0), (0, 4), (0, 0)))
    vp8 = jnp.stack([_rows8([conv_b[l], b_rgate[l], b_igate[l], lru_lambda[l]], LW) for l in range(2)])
    bfp = jnp.pad(b_fgate, ((0, 0), (0, 128 - NH)))[:, None, :] * jnp.ones((1, 8, 1), F32)
    gqk2 = jnp.tile(jnp.pad(g_qk, ((0, 0), (0, 6), (0, 0))), (1, 1, 2))
    gmix8 = jnp.pad(g_mix_out[:, None, :], ((0, 0), (0, 7), (0, 0)))

    x2 = x.reshape(T, D)
    tgt = loss_target.reshape(T, D)

    saved = []
    xc = x2
    for l in range(2):
        sv = {}
        sv["x0"] = xc
        w = wl[l]
        rest0 = _ag_comm(wshards, mixer_pieces(0) + ffn_pieces(0, 1)) if l == 0 else None
        xc, sv["g0"], sv["u0"], sv["f0"], got = _ffn_fwd(xc, mods[l][0], w[0]["up"], w[0]["dn"], l, 0, S, rest0)
        if l == 0:
            w["mix"] = mixer_weights(got[0], got[1])
            w[1] = ffn_weights(got[2], got[3])
        sv["x1"] = xc
        sv["h1"], proj = _mix_in_fwd(xc, mods[l][1], w["mix"]["inp"], l, S)
        sv["proj"] = proj
        sv["ylru"], sv["hl"] = _lru_fwd(proj, cw8, vp8, wr_d, wi_d, l, S)
        all1 = _ag_comm(wshards, ffn_pieces(1, 0) + mixer_pieces(1) + ffn_pieces(1, 1)) if l == 0 else None
        sv["osb"], sv["t1"], got = _sbq_fwd(proj, l, S, all1)
        if l == 0:
            wl[1][0] = ffn_weights(got[0], got[1])
            wl[1]["mix"] = mixer_weights(got[2], got[3])
            wl[1][1] = ffn_weights(got[4], got[5])
        cum = _fgate_fwd(proj, bfp, l, S)
        sv["ck"] = cum[:, :NH].reshape(B, S, NH).transpose(0, 2, 1).reshape(B, NH, S // TK_(S), 1, TK_(S))
        sv["ofx"], sv["nl"], sv["ox"] = _foxq_fwd(proj, cum, sv["ck"], gqk2, l, S)
        xc, sv["y"], sv["mo"] = _mix_out_fwd(xc, sv["ylru"], sv["osb"], sv["ofx"], mods[l][1], gmix8, w["mix"]["out"], l, S)
        sv["x2"] = xc
        xc, sv["g2"], sv["u2"], sv["f2"], _ = _ffn_fwd(xc, mods[l][2], w[1]["up"], w[1]["dn"], l, 1, S)
        saved.append(sv)

    dxc, lpart = _loss_head(xc, tgt, S)
    loss = lax.psum(lpart[0, 0], ("x", "y", "c"))

    tf = wl[0][0]["up"].shape[-1]
    g_up_l = [[None, None], [None, None]]
    g_dn_l = [[None, None], [None, None]]
    g_in_l, g_out_l = [None, None], [None, None]
    dmods = [[None] * 3 for _ in range(2)]
    small = [dict() for _ in range(2)]
    cvec = jnp.reshape(ci, (1,)).astype(jnp.int32)
    scvec = jnp.stack([sidx, ci]).astype(jnp.int32)

    def ffn_groups(l, j):
        return [(0, "up", [g_up_l[l][j]], [(l, j)]), (1, "dn", [g_dn_l[l][j]], [(l, j)])]

    def mixer_groups(l):
        return [(2, "in", [g_in_l[l]], [(l,)]), (3, "out", [g_out_l[l]], [(l,)])]

    def rs_sibling_phase(groups, tag):
        recv_a = _rs_to_sibling([pb for _, _, ps, _ in groups for _, pb in ps], f"rs_to_sibling_{tag}")
        hs, off = [], 0
        for _, gname, ps, leads in groups:
            hs += _add_sibling([pf for pf, _ in ps], recv_a[off:off + len(ps)], cvec,
                               f"rs_add_sibling_{gname}_{'_'.join(map(str, leads[0]))}")
            off += len(ps)
        return groups, recv_a, hs

    def ffn_back(l, j, xin, dy, sv, sub, comm=None):
        dx, dmod, wacc, hb, dfb, ab, dgub, got = _ffn_bwd(
            xin, dy, mods[l][sub], sv[f"f{sub}"], sv[f"g{sub}"], sv[f"u{sub}"],
            wl[l][j]["up_t"], wl[l][j]["dn_t"], l, j, S, comm)
        g_up_l[l][j] = _mm_tn(hb, dgub, f"dw_up_{l}_{j}", tnb=tf, split_n=True, with_bf16=True)
        g_dn_l[l][j] = tuple(g.reshape(N_SHARD, -1, D)
                             for g in _mm_tn(ab, dfb, f"dw_dn_{l}_{j}", tma=tf, with_bf16=True))
        dmods[l][sub] = dmod
        small[l][f"gn{sub}"] = wacc[0]
        return dx, got

    batches = []
    for l in (1, 0):
        sv = saved[l]
        dxc, _ = ffn_back(l, 1, sv["x2"], dxc, sv, 2)
        dyl, dsb, dfx, dmo, dmod1, wacc_mo = _mix_out_bwd(
            dxc, sv["ylru"], sv["osb"], sv["ofx"], sv["mo"], mods[l][1], gmix8, wl[l]["mix"]["out"], l, S)
        small[l]["gmix"] = wacc_mo[0]
        g_out_l[l] = tuple(g.reshape(N_SHARD, -1, D) for g in _mm_tn(sv["y"], dmo, f"dw_out_{l}", with_bf16=True))
        dsq, dsk, dsv, got = _sbq_bwd(sv["proj"], dsb, sv["t1"], l, S,
                                       _rs_to_chips_comm(rs1[2]) if l == 0 else None)
        if l == 0:
            batches.append((rs1[0], rs1[1], got))
        dfq, dfk, dfv, dck, wacc_fx = _foxq_bwd(sv["proj"], dfx, sv["nl"], sv["ox"], sv["ck"], gqk2, l, S)
        small[l]["gqk"] = wacc_fx[0:2, :HD] + wacc_fx[0:2, HD:]
        dcum = dck[:, :, :, 0, :].reshape(B, NH, S).transpose(0, 2, 1).reshape(T, NH)
        dff_, wacc_fg = _fgate_bwd(jnp.pad(dcum, ((0, 0), (0, 128 - NH))), sv["proj"], bfp, l, S)
        small[l]["bf"] = wacc_fg[0, :NH]
        dlx, dlg, dpr, dpi, ub, wacc_lru = _lru_bwd(dyl, sv["proj"], sv["hl"], cw8, vp8, wr_d, wi_d, l, S)
        small[l]["lru"] = wacc_lru
        small[l]["wr"] = _diag_blocks(_mm_tn(ub, dpr, f"dw_rgate_{l}"))
        small[l]["wi"] = _diag_blocks(_mm_tn(ub, dpi, f"dw_igate_{l}"))
        dproj = jnp.concatenate(
            [dlx, dlg, dsq, dsk, dsv, dfq, dfk, dfv, dff_], axis=1)
        g_in = _mm_tn(sv["h1"], dproj, f"dw_in_{l}", tnb=N_INP // 3)[:, :N_IN]
        g_in = g_in.reshape(D, N_SHARD, -1).transpose(1, 0, 2)
        g_in_l[l] = (g_in, g_in.astype(BF16))
        dxc, dmod_in, wacc_in = _mix_in_bwd(sv["x1"], dxc, mods[l][1], dproj, wl[l]["mix"]["inp"], l, S)
        dmods[l][1] = dmod_in + dmod1
        small[l]["gn1"] = wacc_in[0]
        if l == 1:
            dxc, _ = ffn_back(l, 0, sv["x0"], dxc, sv, 0)
            rs1 = rs_sibling_phase(ffn_groups(1, 0) + mixer_groups(1) + ffn_groups(1, 1), "1")
        else:
            late = rs_sibling_phase(mixer_groups(0) + ffn_groups(0, 1), "0_late")
            dxc, got = ffn_back(l, 0, sv["x0"], dxc, sv, 0, _rs_to_chips_comm(late[2]))
            batches.append((late[0], late[1], got))
    grad_x = dxc.reshape(B, S, D)

    dmod_loc = jnp.stack([jnp.stack([dmods[l][j][:, 0:3, :] for j in range(3)], axis=1) for l in range(2)])
    drows = 2 * B * 9
    blk3 = _pack_rows([dmod_loc], -(-drows // 8) * 8, F32)
    ag3 = _all_gather8(blk3, "ag_dmod", True)
    dmod_all = ag3[:, :drows].reshape(N_DEV, 2, B, 9 * D).transpose(1, 0, 2, 3).reshape(2, n_all, 9 * D)
    dmod_mine = lax.dynamic_slice(dmod_all, (0, 0, sidx * ada_cols), (2, n_all, ada_cols))
    grad_w_ada = _ada_bwd(c_all, dmod_mine)
    dmod_rows = jnp.pad(dmod_all.transpose(1, 0, 2).reshape(n_all, 2 * 9, D), ((0, 0), (0, 6), (0, 0)))
    grad_b_ada = _sum_lead(dmod_rows, "grad_b_ada")[:2 * 9].reshape(2, 9 * D)

    sm_parts = [
        jnp.stack([small[l]["bf"] for l in range(2)]),
        jnp.stack([small[l]["lru"][4] for l in range(2)]),
        jnp.stack([small[l]["wr"] for l in range(2)]),
        jnp.stack([small[l]["lru"][5] for l in range(2)]),
        jnp.stack([small[l]["wi"] for l in range(2)]),
        jnp.stack([small[l]["lru"][6] for l in range(2)]),
        jnp.stack([small[l]["lru"][7] for l in range(2)]),
        jnp.stack([small[l]["gqk"] for l in range(2)]),
        jnp.stack([small[l]["gmix"] for l in range(2)]),
        jnp.stack([jnp.stack([small[l][f"gn{j}"] for j in range(3)]) for l in range(2)]),
        jnp.stack([small[l]["lru"][0:4] for l in range(2)]),
    ]
    sm_shapes = [p.shape for p in sm_parts]
    sm_rows = -(-sum(p.size for p in sm_parts) // (8 * ROW)) * 8
    ag4 = _all_gather8(_pack_rows(sm_parts, sm_rows, F32), "ag_small_grads", True)
    sm_sum = _sum_lead(ag4, "sum_small_grads").reshape(-1)
    (g_bf, g_cb, g_wr, g_br, g_wi, g_bi, g_lam, g_gqk, g_gmix, g_gn_full, g_cw_full) = _unpack(sm_sum, sm_shapes)
    g_gn = lax.dynamic_slice(g_gn_full, (0, 0, sidx * gn_cols), (2, 3, gn_cols))
    g_cw = lax.dynamic_slice(g_cw_full, (0, 0, sidx * cw_cols), (2, 4, cw_cols))

    last = rs_sibling_phase(ffn_groups(0, 0), "0_first")
    batches.append((last[0], last[1], _run_comm(_rs_to_chips_comm(last[2]), "rs_to_chips_0_first")))
    shapes4 = [w_ffn_up.shape, w_ffn_down.shape, w_in.shape, w_out.shape]
    tensors, places, r2s = [None] * 4, [], []
    for groups, recv_a, recv_b in batches:
        k = 0
        for gi, gname, ps, leads in groups:
            for (pf, _), lead in zip(ps, leads):
                tensors[gi] = _add_chips_into(pf, recv_a[k], recv_b[k], scvec, tensors[gi], shapes4[gi], lead,
                                              f"rs_add_chips_{gname}_{'_'.join(map(str, lead))}")
                places.append((gi, lead))
                r2s.append(pf.shape[1] // 2)
                k += 1
    gw_up, gw_dn, gw_in, gw_out = _share_halves(tensors, places, r2s)

    def upd(w, g, m, v, name):
        sh = w.shape
        two = (w.size // sh[-1], sh[-1])
        dlt, mn, vn = _adamw(w.reshape(two), g.reshape(two), m.reshape(two), v.reshape(two), name)
        return dlt.reshape(sh), mn.reshape(sh), vn.reshape(sh)

    big = {
        "w_ada": (w_ada, grad_w_ada, m_w_ada, v_w_ada),
        "w_ffn_up": (w_ffn_up, gw_up, m_w_ffn_up, v_w_ffn_up),
        "w_ffn_down": (w_ffn_down, gw_dn, m_w_ffn_down, v_w_ffn_down),
        "w_in": (w_in, gw_in, m_w_in, v_w_in),
        "w_out": (w_out, gw_out, m_w_out, v_w_out),
    }
    res = {n: (t[1],) + upd(*t, f"adamw_{n}") for n, t in big.items()}

    smalls = {
        "b_ada": (b_ada, grad_b_ada, m_b_ada, v_b_ada),
        "g_norm": (g_norm, g_gn, m_g_norm, v_g_norm),
        "b_fgate": (b_fgate, g_bf, m_b_fgate, v_b_fgate),
        "conv_w": (conv_w, g_cw, m_conv_w, v_conv_w),
        "conv_b": (conv_b, g_cb, m_conv_b, v_conv_b),
        "w_rgate": (w_rgate, g_wr, m_w_rgate, v_w_rgate),
        "b_rgate": (b_rgate, g_br, m_b_rgate, v_b_rgate),
        "w_igate": (w_igate, g_wi, m_w_igate, v_w_igate),
        "b_igate": (b_igate, g_bi, m_b_igate, v_b_igate),
        "lru_lambda": (lru_lambda, g_lam, m_lru_lambda, v_lru_lambda),
        "g_qk": (g_qk, g_gqk, m_g_qk, v_g_qk),
        "g_mix_out": (g_mix_out, g_gmix, m_g_mix_out, v_g_mix_out),
    }
    names = list(smalls)
    shapes = [smalls[n][0].shape for n in names]
    prow = -(-sum(math.prod(s) for s in shapes) // (8 * ROW)) * 8
    packed = [_pack_rows([smalls[n][i].reshape(shapes[k]) for k, n in enumerate(names)], prow, F32) for i in range(4)]
    outs = _adamw(packed[0], packed[1], packed[2], packed[3], "adamw_small")
    un = [_unpack(o.reshape(-1), shapes) for o in outs]
    for k, n in enumerate(names):
        res[n] = (smalls[n][1].reshape(shapes[k]), un[0][k], un[1][k], un[2][k])

    order = ["w_ada", "b_ada", "g_norm", "w_ffn_up", "w_ffn_down", "w_in", "b_fgate", "conv_w", "conv_b",
             "w_rgate", "b_rgate", "w_igate", "b_igate", "lru_lambda", "g_qk", "g_mix_out", "w_out"]
    return (loss, grad_x, *[res[n][0] for n in order], *[res[n][1] for n in order],
            *[res[n][2] for n in order], *[res[n][3] for n in order])


def TQ_(S):
    return min(TQ, S)


def TK_(S):
    return min(TK, S)


def _unpack_shards(wg, shapes):
    out, off = [], 0
    for sh in shapes:
        n = math.prod(sh)
        out.append(wg[:, off:off + n].reshape((N_SHARD,) + tuple(sh)))
        off += n
    return out
```

```python
import math

import jax
import jax.numpy as jnp
from jax import lax
from jax.experimental import pallas as pl
from jax.experimental.pallas import tpu as pltpu

F32 = jnp.float32
BF16 = jnp.bfloat16
MESH = pl.DeviceIdType.MESH

D = 1024
HD = 64
LW = 512
NH = 4
AW = NH * HD
N_IN = 2564
N_INP = 2688
F_BLK = 2560 // 128
EPS = 1e-6
LRU_C = 8.0
SCALE = HD ** -0.5
NEG = -1e30
TQ = 256
TK = 256

ADAM_LR, ADAM_B1, ADAM_B2, ADAM_EPS, ADAM_WD, ADAM_STEP = 0.001, 0.9, 0.999, 1e-08, 0.01, 10

VMEM_BIG = 56 * 1024 * 1024
N_DEV = 8
N_SHARD = 4
ROW = 1024


def _cp(sem, vmem=None):
    return pltpu.CompilerParams(dimension_semantics=sem, vmem_limit_bytes=vmem)


def _dot(a, b):
    return jnp.dot(a, b, preferred_element_type=F32)


def _dot_nt(a, b):
    return lax.dot_general(a, b, (((1,), (1,)), ((), ())), preferred_element_type=F32)


def _dot_tn(a, b):
    return lax.dot_general(a, b, (((0,), (0,)), ((), ())), preferred_element_type=F32)


def _log1p(e):
    small = e * (1.0 - e * (0.5 - e * (1.0 / 3.0 - e * 0.25)))
    return jnp.where(e < 0.01, small, jnp.log(1.0 + e))


def _expm1_neg(x):
    small = x * (1.0 + x * 0.5 * (1.0 + x * (1.0 / 3.0) * (1.0 + x * 0.25 * (1.0 + x * 0.2))))
    return jnp.where(x > -0.05, small, jnp.exp(x) - 1.0)


def _sigmoid(x):
    return 1.0 / (1.0 + jnp.exp(-x))


_GELU_C = math.sqrt(2.0 / math.pi)


def _gelu_and_grad(x):
    x2 = x * x
    th = jnp.tanh(_GELU_C * (x + 0.044715 * x * x2))
    g = 0.5 * x * (1.0 + th)
    dg = 0.5 * (1.0 + th) + 0.5 * x * (1.0 - th * th) * _GELU_C * (1.0 + 3.0 * 0.044715 * x2)
    return g, dg


def _rms_rows(x):
    rstd = lax.rsqrt(jnp.mean(x * x, axis=-1, keepdims=True) + EPS)
    return x * rstd, rstd


def _rms_bwd(xn, rstd, dyn):
    return rstd * (dyn - xn * jnp.mean(dyn * xn, axis=-1, keepdims=True))


def _colsum(x):
    return jnp.sum(x, axis=0, keepdims=True)


def _rowsum(x):
    return jnp.sum(x, axis=1, keepdims=True)


def _split3(x):
    hi = x.astype(BF16)
    r = x - hi.astype(F32)
    mid = r.astype(BF16)
    lo = (r - mid.astype(F32)).astype(BF16)
    return hi, mid, lo


def _cumsum_mm(x, ones_tri, parts=3):
    ps = _split3(x)[:parts]
    acc = _dot(ps[0], ones_tri)
    for p in ps[1:]:
        acc = acc + _dot(p, ones_tri)
    return acc


def _tri(n, kind):
    r = lax.broadcasted_iota(jnp.int32, (n, n), 0)
    c = lax.broadcasted_iota(jnp.int32, (n, n), 1)
    m = {"row_gt_col": r > c, "row_le_col": r <= c, "row_lt_col": r < c}[kind]
    return jnp.where(m, 1.0, 0.0).astype(BF16)


def _normmod(x, mod_ref):
    xn, rstd = _rms_rows(x)
    h = xn * mod_ref[3:4, :] * mod_ref[1:2, :] + mod_ref[0:1, :]
    return h, xn, rstd


def _normmod_bwd(dh, xn, rstd, mod_ref, dmod_ref, wacc_ref):
    gn = mod_ref[3:4, :]
    sc = mod_ref[1:2, :]
    dmod_ref[0:1, :] += _colsum(dh)
    t = _colsum(dh * xn)
    dmod_ref[1:2, :] += t * gn
    wacc_ref[0:1, :] += t * sc
    return _rms_bwd(xn, rstd, dh * (gn * sc))


def _tile(n, want):
    t = min(n, want)
    while n % t:
        t //= 2
    return t


def _tile_div8(n, cap, mult=8):
    best = mult
    for t in range(mult, min(n, cap) + 1, mult):
        if n % t == 0:
            best = t
    assert n % best == 0
    return best


def _ffn_fwd(x, mod, wup, wdn, l, j, S, comm=None):
    T = x.shape[0]
    tf = wup.shape[-1]
    nk = 2
    tm = _tile(S, 512)
    tpb = S // tm
    nt = T // tm
    c_args, c_specs, c_outs, c_scr = _hosted(comm)
    n_ci, n_co = len(c_args), len(c_outs)

    def body(*refs):
        x_ref, mod_ref, wg_ref, wu_ref, wd_ref = refs[:5]
        c_in = refs[5:5 + n_ci]
        xo_ref, g_ref, u_ref, f_ref = refs[5 + n_ci:9 + n_ci]
        c_out = refs[9 + n_ci:9 + n_ci + n_co]
        h_sc, acc_sc = refs[9 + n_ci + n_co:11 + n_ci + n_co]
        c_sems = refs[11 + n_ci + n_co:]
        i = pl.program_id(0)
        k = pl.program_id(1)
        if comm is not None:
            @pl.when((i == 0) & (k == 0))
            def _():
                comm["start"](c_in, c_out, c_sems)

        @pl.when(k == 0)
        def _():
            h, _, _ = _normmod(x_ref[...], mod_ref)
            h_sc[...] = h.astype(BF16)
            acc_sc[...] = jnp.zeros_like(acc_sc)

        h = h_sc[...]
        g = _dot(h, wg_ref[...])
        u = _dot(h, wu_ref[...])
        g_ref[...] = g.astype(BF16)
        u_ref[...] = u.astype(BF16)
        a = (g * _sigmoid(g)) * u
        acc_sc[...] += _dot(a.astype(BF16), wd_ref[...])

        @pl.when(k == nk - 1)
        def _():
            f = acc_sc[...]
            f_ref[...] = f.astype(BF16)
            xo_ref[...] = x_ref[...] + (0.5 * mod_ref[2:3, :]) * f

        if comm is not None:
            @pl.when((i == nt - 1) & (k == nk - 1))
            def _():
                comm["finish"](c_in, c_out, c_sems)

    anyspec = pl.BlockSpec(memory_space=pl.ANY)
    out = pl.pallas_call(
        body, name=f"ffn_fwd_{l}_{j}",
        grid=(nt, nk),
        in_specs=[
            pl.BlockSpec((tm, D), lambda i, k: (i, 0)),
            pl.BlockSpec((None, 8, D), lambda i, k: (i // tpb, 0, 0)),
            pl.BlockSpec((None, D, tf), lambda i, k: (k, 0, 0)),
            pl.BlockSpec((None, D, tf), lambda i, k: (nk + k, 0, 0)),
            pl.BlockSpec((tf, D), lambda i, k: (k, 0)),
        ] + c_specs,
        out_specs=[
            pl.BlockSpec((tm, D), lambda i, k: (i, 0)),
            pl.BlockSpec((tm, tf), lambda i, k: (i, k)),
            pl.BlockSpec((tm, tf), lambda i, k: (i, k)),
            pl.BlockSpec((tm, D), lambda i, k: (i, 0)),
        ] + [anyspec] * n_co,
        out_shape=[
            jax.ShapeDtypeStruct((T, D), F32),
            jax.ShapeDtypeStruct((T, nk * tf), BF16),
            jax.ShapeDtypeStruct((T, nk * tf), BF16),
            jax.ShapeDtypeStruct((T, D), BF16),
        ] + c_outs,
        scratch_shapes=[pltpu.VMEM((tm, D), BF16), pltpu.VMEM((tm, D), F32)] + c_scr,
        compiler_params=_cp(("arbitrary", "arbitrary"), VMEM_BIG),
    )(x, mod, wup, wup, wdn, *c_args)
    return out[0], out[1], out[2], out[3], list(out[4:])


def _ffn_bwd(x, dy, mod, f, g, u, wup, wdn, l, j, S, comm=None):
    T = x.shape[0]
    tf = wup.shape[-1]
    nk = 2
    tm = _tile(S, 256)
    tpb = S // tm
    nt = T // tm
    c_args, c_specs, c_outs, c_scr = _hosted(comm)
    n_ci, n_co = len(c_args), len(c_outs)

    def body(*refs):
        x_ref, dy_ref, mod_ref, f_ref, g_ref, u_ref, wup_ref, wd_ref = refs[:8]
        c_in = refs[8:8 + n_ci]
        dx_ref, dmod_ref, wacc_ref, h_ref, df_ref, a_ref, dgu_ref = refs[8 + n_ci:15 + n_ci]
        c_out = refs[15 + n_ci:15 + n_ci + n_co]
        c_sems = refs[15 + n_ci + n_co:]
        i = pl.program_id(0)

        @pl.when(i == 0)
        def _():
            wacc_ref[...] = jnp.zeros_like(wacc_ref)
            if comm is not None:
                comm["start"](c_in, c_out, c_sems)

        @pl.when(i % tpb == 0)
        def _():
            dmod_ref[...] = jnp.zeros_like(dmod_ref)

        dy_ = dy_ref[...]
        h, xn, rstd = _normmod(x_ref[...], mod_ref)
        h_ref[...] = h.astype(BF16)
        dfb = ((0.5 * mod_ref[2:3, :]) * dy_).astype(BF16)
        df_ref[...] = dfb
        dmod_ref[2:3, :] += _colsum(0.5 * f_ref[...].astype(F32) * dy_)
        dh = None
        for k in range(nk):
            cols = slice(k * tf, (k + 1) * tf)
            da = _dot_nt(dfb, wd_ref[cols, :])
            gg = g_ref[:, cols].astype(F32)
            uu = u_ref[:, cols].astype(F32)
            sig = _sigmoid(gg)
            s = gg * sig
            a_ref[:, cols] = (s * uu).astype(BF16)
            du = (da * s).astype(BF16)
            dg = (da * uu * (sig * (1.0 + gg * (1.0 - sig)))).astype(BF16)
            dgu_ref[0, :, cols] = dg
            dgu_ref[1, :, cols] = du
            part = _dot_nt(dg, wup_ref[k]) + _dot_nt(du, wup_ref[nk + k])
            dh = part if dh is None else dh + part
        dx_ref[...] = dy_ + _normmod_bwd(dh, xn, rstd, mod_ref, dmod_ref, wacc_ref)

        if comm is not None:
            @pl.when(i == nt - 1)
            def _():
                comm["finish"](c_in, c_out, c_sems)

    once = pl.Buffered(1)
    anyspec = pl.BlockSpec(memory_space=pl.ANY)
    out = pl.pallas_call(
        body, name=f"ffn_bwd_{l}_{j}",
        grid=(nt,),
        in_specs=[
            pl.BlockSpec((tm, D), lambda i: (i, 0)),
            pl.BlockSpec((tm, D), lambda i: (i, 0)),
            pl.BlockSpec((None, 8, D), lambda i: (i // tpb, 0, 0)),
            pl.BlockSpec((tm, D), lambda i: (i, 0)),
            pl.BlockSpec((tm, nk * tf), lambda i: (i, 0)),
            pl.BlockSpec((tm, nk * tf), lambda i: (i, 0)),
            pl.BlockSpec((2 * nk, D, tf), lambda i: (0, 0, 0), pipeline_mode=once),
            pl.BlockSpec((nk * tf, D), lambda i: (0, 0), pipeline_mode=once),
        ] + c_specs,
        out_specs=[
            pl.BlockSpec((tm, D), lambda i: (i, 0)),
            pl.BlockSpec((None, 8, D), lambda i: (i // tpb, 0, 0)),
            pl.BlockSpec((8, D), lambda i: (0, 0)),
            pl.BlockSpec((tm, D), lambda i: (i, 0)),
            pl.BlockSpec((tm, D), lambda i: (i, 0)),
            pl.BlockSpec((tm, nk * tf), lambda i: (i, 0)),
            pl.BlockSpec((2, tm, nk * tf), lambda i: (0, i, 0)),
        ] + [anyspec] * n_co,
        out_shape=[
            jax.ShapeDtypeStruct((T, D), F32),
            jax.ShapeDtypeStruct((T // S, 8, D), F32),
            jax.ShapeDtypeStruct((8, D), F32),
            jax.ShapeDtypeStruct((T, D), BF16),
            jax.ShapeDtypeStruct((T, D), BF16),
            jax.ShapeDtypeStruct((T, nk * tf), BF16),
            jax.ShapeDtypeStruct((2, T, nk * tf), BF16),
        ] + c_outs,
        scratch_shapes=c_scr,
        compiler_params=_cp(("arbitrary",), VMEM_BIG),
    )(x, dy, mod, f, g, u, wup, wdn, *c_args)
    return tuple(out[:7]) + (list(out[7:]),)


def _mm_tn(a, b, name, tma=None, tnb=None, split_n=False, with_bf16=False):
    T, M = a.shape
    b3 = b if b.ndim == 3 else b[None]
    nb, _, N = b3.shape
    tma = tma or M
    tnb = tnb or N
    npb = N // tnb
    tt = _tile(T, 1024)
    nt = T // tt

    def body(a_ref, b_ref, o_ref, *ob_ref):
        @pl.when(pl.program_id(2) == 0)
        def _():
            o_ref[...] = jnp.zeros_like(o_ref)

        o_ref[...] += _dot_tn(a_ref[...], b_ref[...])

        if with_bf16:
            @pl.when(pl.program_id(2) == nt - 1)
            def _():
                ob_ref[0][...] = o_ref[...].astype(BF16)

    if split_n:
        shape = (nb * npb, M, tnb)
        out_spec = pl.BlockSpec((None, tma, tnb), lambda m, n, t: (n, m, 0))
    else:
        assert nb == 1
        shape = (M, N)
        out_spec = pl.BlockSpec((tma, tnb), lambda m, n, t: (m, n))
    dts = (F32, BF16) if with_bf16 else (F32,)
    out = pl.pallas_call(
        body, name=name,
        grid=(M // tma, nb * npb, nt),
        in_specs=[pl.BlockSpec((tt, tma), lambda m, n, t: (t, m)),
                  pl.BlockSpec((None, tt, tnb), lambda m, n, t: (n // npb, t, n % npb))],
        out_specs=[out_spec] * len(dts),
        out_shape=[jax.ShapeDtypeStruct(shape, dt) for dt in dts],
        compiler_params=_cp(("arbitrary", "arbitrary", "arbitrary"), VMEM_BIG),
    )(a, b3)
    return tuple(out) if with_bf16 else out[0]


def _mix_in_fwd(x, mod, winp, l, S):
    T = x.shape[0]
    tm = _tile(S, 512)
    tpb = S // tm

    def body(x_ref, mod_ref, w_ref, h_ref, p_ref):
        h, _, _ = _normmod(x_ref[...], mod_ref)
        hb = h.astype(BF16)
        h_ref[...] = hb
        p_ref[...] = _dot(hb, w_ref[...])

    return pl.pallas_call(
        body, name=f"mix_in_fwd_{l}",
        grid=(T // tm,),
        in_specs=[pl.BlockSpec((tm, D), lambda i: (i, 0)),
                  pl.BlockSpec((None, 8, D), lambda i: (i // tpb, 0, 0)),
                  pl.BlockSpec((D, N_INP), lambda i: (0, 0))],
        out_specs=[pl.BlockSpec((tm, D), lambda i: (i, 0)),
                   pl.BlockSpec((tm, N_INP), lambda i: (i, 0))],
        out_shape=[jax.ShapeDtypeStruct((T, D), BF16), jax.ShapeDtypeStruct((T, N_INP), F32)],
        compiler_params=_cp(("arbitrary",), VMEM_BIG),
    )(x, mod, winp)


def _mix_in_bwd(x, dres, mod, dproj, winp, l, S):
    T = x.shape[0]
    tm = _tile(S, 512)
    tpb = S // tm

    def body(x_ref, dr_ref, mod_ref, dp_ref, w_ref, dx_ref, dmod_ref, wacc_ref):
        i = pl.program_id(0)

        @pl.when(i == 0)
        def _():
            wacc_ref[...] = jnp.zeros_like(wacc_ref)

        @pl.when(i % tpb == 0)
        def _():
            dmod_ref[...] = jnp.zeros_like(dmod_ref)

        dh = _dot_nt(dp_ref[...], w_ref[...])
        _, xn, rstd = _normmod(x_ref[...], mod_ref)
        dx_ref[...] = dr_ref[...] + _normmod_bwd(dh, xn, rstd, mod_ref, dmod_ref, wacc_ref)

    return pl.pallas_call(
        body, name=f"mix_in_bwd_{l}",
        grid=(T // tm,),
        in_specs=[pl.BlockSpec((tm, D), lambda i: (i, 0)),
                  pl.BlockSpec((tm, D), lambda i: (i, 0)),
                  pl.BlockSpec((None, 8, D), lambda i: (i // tpb, 0, 0)),
                  pl.BlockSpec((tm, N_INP), lambda i: (i, 0)),
                  pl.BlockSpec((D, N_INP), lambda i: (0, 0))],
        out_specs=[pl.BlockSpec((tm, D), lambda i: (i, 0)),
                   pl.BlockSpec((None, 8, D), lambda i: (i // tpb, 0, 0)),
                   pl.BlockSpec((8, D), lambda i: (0, 0))],
        out_shape=[jax.ShapeDtypeStruct((T, D), F32),
                   jax.ShapeDtypeStruct((T // S, 8, D), F32),
                   jax.ShapeDtypeStruct((8, D), F32)],
        compiler_params=_cp(("arbitrary",), VMEM_BIG),
    )(x, dres, mod, dproj, winp)


_GROUPS = ((0, LW), (LW, LW + AW), (LW + AW, D))


def _mix_out_fwd(x, ylru, osb, ofox, mod, gmix, wout, l, S):
    T = x.shape[0]
    tm = _tile(S, 512)
    tpb = S // tm

    def body(x_ref, yl_ref, sb_ref, fx_ref, mod_ref, gm_ref, w_ref, xo_ref, y_ref, mo_ref):
        for src, (lo, hi) in zip((yl_ref, sb_ref, fx_ref), _GROUPS):
            vn, _ = _rms_rows(src[...])
            y_ref[:, lo:hi] = (vn * gm_ref[0:1, lo:hi]).astype(BF16)
        mo = _dot(y_ref[...], w_ref[...])
        mo_ref[...] = mo.astype(BF16)
        xo_ref[...] = x_ref[...] + mod_ref[2:3, :] * mo

    return pl.pallas_call(
        body, name=f"mix_out_fwd_{l}",
        grid=(T // tm,),
        in_specs=[pl.BlockSpec((tm, D), lambda i: (i, 0)),
                  pl.BlockSpec((tm, LW), lambda i: (i, 0)),
                  pl.BlockSpec((tm, AW), lambda i: (i, 0)),
                  pl.BlockSpec((tm, AW), lambda i: (i, 0)),
                  pl.BlockSpec((None, 8, D), lambda i: (i // tpb, 0, 0)),
                  pl.BlockSpec((None, 8, D), lambda i: (l, 0, 0)),
                  pl.BlockSpec((D, D), lambda i: (0, 0))],
        out_specs=[pl.BlockSpec((tm, D), lambda i: (i, 0)),
                   pl.BlockSpec((tm, D), lambda i: (i, 0)),
                   pl.BlockSpec((tm, D), lambda i: (i, 0))],
        out_shape=[jax.ShapeDtypeStruct((T, D), F32),
                   jax.ShapeDtypeStruct((T, D), BF16),
                   jax.ShapeDtypeStruct((T, D), BF16)],
        compiler_params=_cp(("arbitrary",), VMEM_BIG),
    )(x, ylru, osb, ofox, mod, gmix, wout)


def _mix_out_bwd(dx2, ylru, osb, ofox, mo, mod, gmix, wout, l, S):
    T = dx2.shape[0]
    tm = _tile(S, 512)
    tpb = S // tm

    def body(dx_ref, yl_ref, sb_ref, fx_ref, mo_ref, mod_ref, gm_ref, w_ref,
             dyl_ref, dsb_ref, dfx_ref, dmo_ref, dmod_ref, wacc_ref):
        i = pl.program_id(0)

        @pl.when(i == 0)
        def _():
            wacc_ref[...] = jnp.zeros_like(wacc_ref)

        @pl.when(i % tpb == 0)
        def _():
            dmod_ref[...] = jnp.zeros_like(dmod_ref)

        dx = dx_ref[...]
        dmod_ref[2:3, :] += _colsum(mo_ref[...].astype(F32) * dx)
        dmo = (mod_ref[2:3, :] * dx).astype(BF16)
        dmo_ref[...] = dmo
        dy = _dot_nt(dmo, w_ref[...])
        for src, dst, (lo, hi) in zip((yl_ref, sb_ref, fx_ref), (dyl_ref, dsb_ref, dfx_ref), _GROUPS):
            vn, rstd = _rms_rows(src[...])
            dyg = dy[:, lo:hi]
            wacc_ref[0:1, lo:hi] += _colsum(dyg * vn)
            dst[...] = _rms_bwd(vn, rstd, dyg * gm_ref[0:1, lo:hi])

    return pl.pallas_call(
        body, name=f"mix_out_bwd_{l}",
        grid=(T // tm,),
        in_specs=[pl.BlockSpec((tm, D), lambda i: (i, 0)),
                  pl.BlockSpec((tm, LW), lambda i: (i, 0)),
                  pl.BlockSpec((tm, AW), lambda i: (i, 0)),
                  pl.BlockSpec((tm, AW), lambda i: (i, 0)),
                  pl.BlockSpec((tm, D), lambda i: (i, 0)),
                  pl.BlockSpec((None, 8, D), lambda i: (i // tpb, 0, 0)),
                  pl.BlockSpec((None, 8, D), lambda i: (l, 0, 0)),
                  pl.BlockSpec((D, D), lambda i: (0, 0))],
        out_specs=[pl.BlockSpec((tm, LW), lambda i: (i, 0)),
                   pl.BlockSpec((tm, AW), lambda i: (i, 0)),
                   pl.BlockSpec((tm, AW), lambda i: (i, 0)),
                   pl.BlockSpec((tm, D), lambda i: (i, 0)),
                   pl.BlockSpec((None, 8, D), lambda i: (i // tpb, 0, 0)),
                   pl.BlockSpec((8, D), lambda i: (0, 0))],
        out_shape=[jax.ShapeDtypeStruct((T, LW), F32),
                   jax.ShapeDtypeStruct((T, AW), F32),
                   jax.ShapeDtypeStruct((T, AW), F32),
                   jax.ShapeDtypeStruct((T, D), BF16),
                   jax.ShapeDtypeStruct((T // S, 8, D), F32),
                   jax.ShapeDtypeStruct((8, D), F32)],
        compiler_params=_cp(("arbitrary",), VMEM_BIG),
    )(dx2, ylru, osb, ofox, mo, mod, gmix, wout)


def _loss_head(y, tgt, S):
    T = y.shape[0]
    tm = _tile(S, 512)

    def body(y_ref, t_ref, dy_ref, l_ref):
        @pl.when(pl.program_id(0) == 0)
        def _():
            l_ref[...] = jnp.zeros_like(l_ref)

        d = y_ref[...] - t_ref[...]
        dy_ref[...] = d * (1.0 / D)
        l_ref[...] += (0.5 / D) * _rowsum(_colsum(d * d))

    return pl.pallas_call(
        body, name="loss_head",
        grid=(T // tm,),
        in_specs=[pl.BlockSpec((tm, D), lambda i: (i, 0)), pl.BlockSpec((tm, D), lambda i: (i, 0))],
        out_specs=[pl.BlockSpec((tm, D), lambda i: (i, 0)), pl.BlockSpec((8, 128), lambda i: (0, 0))],
        out_shape=[jax.ShapeDtypeStruct((T, D), F32), jax.ShapeDtypeStruct((8, 128), F32)],
        compiler_params=_cp(("arbitrary",)),
    )(y, tgt)


def _lru_gates(u, vp_ref, wr_ref, wi_ref):
    ub = u.astype(BF16)
    r = _sigmoid(_dot(ub, wr_ref[...]) + vp_ref[1:2, :])
    ig = _sigmoid(_dot(ub, wi_ref[...]) + vp_ref[2:3, :])
    lam = vp_ref[3:4, :]
    sp = jnp.maximum(-lam, 0.0) + _log1p(jnp.exp(-jnp.abs(lam)))
    log_a = (-LRU_C) * r * sp
    a = jnp.exp(log_a)
    mult = jnp.sqrt(-_expm1_neg(2.0 * log_a))
    return ub, r, ig, sp, a, mult


def _conv_taps(x, xp, row, cw_ref):
    xs = [x]
    for d in (1, 2, 3):
        xs.append(jnp.where(row >= d, pltpu.roll(x, d, 0), pltpu.roll(xp, d, 0)))
    u = xs[0] * cw_ref[3:4, :]
    for d in (1, 2, 3):
        u = u + xs[d] * cw_ref[3 - d:4 - d, :]
    return xs, u


def _lru_fwd(proj, cw, vp, wr, wi, l, S):
    T = proj.shape[0]
    ts = _tile(S, 256)
    nb = S // ts

    def body(x_ref, lg_ref, cw_ref, vp_ref, wr_ref, wi_ref, y_ref, h_ref, xp_sc, hc_sc):
        @pl.when(pl.program_id(1) == 0)
        def _():
            xp_sc[...] = jnp.zeros_like(xp_sc)
            hc_sc[...] = jnp.zeros_like(hc_sc)

        row = lax.broadcasted_iota(jnp.int32, (ts, LW), 0)
        x = x_ref[...]
        _, u = _conv_taps(x, xp_sc[...], row, cw_ref)
        u = u + vp_ref[0:1, :]
        xp_sc[...] = x
        _, _, ig, _, a, mult = _lru_gates(u, vp_ref, wr_ref, wi_ref)
        bv = mult * (ig * u)
        av = a
        d = 1
        while d < ts:
            a_s = jnp.where(row >= d, pltpu.roll(av, d, 0), 1.0)
            b_s = jnp.where(row >= d, pltpu.roll(bv, d, 0), 0.0)
            bv = av * b_s + bv
            av = av * a_s
            d *= 2
        h = bv + av * hc_sc[7:8, :]
        hc_sc[...] = h[ts - 8:ts, :]
        h_ref[...] = h
        gl, _ = _gelu_and_grad(lg_ref[...])
        y_ref[...] = h * gl

    return pl.pallas_call(
        body, name=f"lru_fwd_{l}",
        grid=(T // S, nb),
        in_specs=[pl.BlockSpec((ts, LW), lambda b, j: (b * nb + j, 0)),
                  pl.BlockSpec((ts, LW), lambda b, j: (b * nb + j, 1)),
                  pl.BlockSpec((None, 8, LW), lambda b, j: (l, 0, 0)),
                  pl.BlockSpec((None, 8, LW), lambda b, j: (l, 0, 0)),
                  pl.BlockSpec((None, LW, LW), lambda b, j: (l, 0, 0)),
                  pl.BlockSpec((None, LW, LW), lambda b, j: (l, 0, 0))],
        out_specs=[pl.BlockSpec((ts, LW), lambda b, j: (b * nb + j, 0)),
                   pl.BlockSpec((ts, LW), lambda b, j: (b * nb + j, 0))],
        out_shape=[jax.ShapeDtypeStruct((T, LW), F32), jax.ShapeDtypeStruct((T, LW), F32)],
        scratch_shapes=[pltpu.VMEM((ts, LW), F32), pltpu.VMEM((8, LW), F32)],
        compiler_params=_cp(("arbitrary", "arbitrary")),
    )(proj, proj, cw, vp, wr, wi)


def _lru_bwd(dyl, proj, h, cw, vp, wr, wi, l, S):
    T = proj.shape[0]
    ts = _tile(S, 256)
    nb = S // ts

    def body(dy_ref, x_ref, xprev_ref, lg_ref, h_ref, hprev_ref, cw_ref, vp_ref, wr_ref, wi_ref,
             dx_ref, dlg_ref, dpr_ref, dpi_ref, ub_ref, wacc_ref, gc_sc, af_sc, dun_sc):
        b = pl.program_id(0)
        j = pl.program_id(1)
        first = j == nb - 1

        @pl.when((b == 0) & (j == 0))
        def _():
            wacc_ref[...] = jnp.zeros_like(wacc_ref)

        @pl.when(j == 0)
        def _():
            gc_sc[...] = jnp.zeros_like(gc_sc)
            af_sc[...] = jnp.ones_like(af_sc)
            dun_sc[...] = jnp.zeros_like(dun_sc)

        row = lax.broadcasted_iota(jnp.int32, (ts, LW), 0)
        keep = jnp.where(first, 0.0, 1.0)
        x = x_ref[...]
        xs, u = _conv_taps(x, xprev_ref[...] * keep, row, cw_ref)
        u = u + vp_ref[0:1, :]
        ub, r, ig, sp, a, mult = _lru_gates(u, vp_ref, wr_ref, wi_ref)
        ub_ref[...] = ub
        hh = h_ref[...]
        h_m1 = jnp.where(row >= 1, pltpu.roll(hh, 1, 0), pltpu.roll(hprev_ref[...] * keep, 1, 0))
        dy = dy_ref[...]
        gl, dgl = _gelu_and_grad(lg_ref[...])
        dlg_ref[...] = (dy * hh * dgl).astype(BF16)
        bv = dy * gl
        av = jnp.where(row < ts - 1, pltpu.roll(a, ts - 1, 0), af_sc[0:1, :])
        d = 1
        while d < ts:
            a_s = jnp.where(row < ts - d, pltpu.roll(av, ts - d, 0), 1.0)
            b_s = jnp.where(row < ts - d, pltpu.roll(bv, ts - d, 0), 0.0)
            bv = av * b_s + bv
            av = av * a_s
            d *= 2
        gt = bv + av * gc_sc[0:1, :]
        gc_sc[...] = gt[0:8, :]
        af_sc[...] = a[0:8, :]
        da = gt * h_m1
        d_ig = gt * mult * u
        d_mult = gt * ig * u
        du = gt * mult * ig
        dlog_a = da * a - d_mult * (a * a) / mult
        dpre_r = (dlog_a * ((-LRU_C) * sp)) * r * (1.0 - r)
        dpre_i = d_ig * ig * (1.0 - ig)
        lam = vp_ref[3:4, :]
        wacc_ref[7:8, :] += _colsum(dlog_a * r) * (LRU_C * _sigmoid(-lam))
        wacc_ref[5:6, :] += _colsum(dpre_r)
        wacc_ref[6:7, :] += _colsum(dpre_i)
        dprb = dpre_r.astype(BF16)
        dpib = dpre_i.astype(BF16)
        dpr_ref[...] = dprb
        dpi_ref[...] = dpib
        du = du + _dot_nt(dprb, wr_ref[...]) + _dot_nt(dpib, wi_ref[...])
        wacc_ref[4:5, :] += _colsum(du)
        dun = dun_sc[...]
        dx = du * cw_ref[3:4, :]
        wacc_ref[3:4, :] += _colsum(du * xs[0])
        for dd in (1, 2, 3):
            du_s = jnp.where(row < ts - dd, pltpu.roll(du, ts - dd, 0), pltpu.roll(dun, ts - dd, 0))
            dx = dx + du_s * cw_ref[3 - dd:4 - dd, :]
            wacc_ref[3 - dd:4 - dd, :] += _colsum(du * xs[dd])
        dun_sc[...] = du
        dx_ref[...] = dx.astype(BF16)

    def tb(b, j):
        return b * nb + (nb - 1 - j)

    def tbp(b, j):
        return b * nb + jnp.maximum(nb - 2 - j, 0)

    return pl.pallas_call(
        body, name=f"lru_bwd_{l}",
        grid=(T // S, nb),
        in_specs=[pl.BlockSpec((ts, LW), lambda b, j: (tb(b, j), 0)),
                  pl.BlockSpec((ts, LW), lambda b, j: (tb(b, j), 0)),
                  pl.BlockSpec((ts, LW), lambda b, j: (tbp(b, j), 0)),
                  pl.BlockSpec((ts, LW), lambda b, j: (tb(b, j), 1)),
                  pl.BlockSpec((ts, LW), lambda b, j: (tb(b, j), 0)),
                  pl.BlockSpec((ts, LW), lambda b, j: (tbp(b, j), 0)),
                  pl.BlockSpec((None, 8, LW), lambda b, j: (l, 0, 0)),
                  pl.BlockSpec((None, 8, LW), lambda b, j: (l, 0, 0)),
                  pl.BlockSpec((None, LW, LW), lambda b, j: (l, 0, 0)),
                  pl.BlockSpec((None, LW, LW), lambda b, j: (l, 0, 0))],
        out_specs=[pl.BlockSpec((ts, LW), lambda b, j: (tb(b, j), 0)),
                   pl.BlockSpec((ts, LW), lambda b, j: (tb(b, j), 0)),
                   pl.BlockSpec((ts, LW), lambda b, j: (tb(b, j), 0)),
                   pl.BlockSpec((ts, LW), lambda b, j: (tb(b, j), 0)),
                   pl.BlockSpec((ts, LW), lambda b, j: (tb(b, j), 0)),
                   pl.BlockSpec((8, LW), lambda b, j: (0, 0))],
        out_shape=[jax.ShapeDtypeStruct((T, LW), BF16),
                   jax.ShapeDtypeStruct((T, LW), BF16),
                   jax.ShapeDtypeStruct((T, LW), BF16),
                   jax.ShapeDtypeStruct((T, LW), BF16),
                   jax.ShapeDtypeStruct((T, LW), BF16),
                   jax.ShapeDtypeStruct((8, LW), F32)],
        scratch_shapes=[pltpu.VMEM((8, LW), F32), pltpu.VMEM((8, LW), F32), pltpu.VMEM((ts, LW), F32)],
        compiler_params=_cp(("arbitrary", "arbitrary")),
    )(dyl, proj, proj, proj, h, h, cw, vp, wr, wi)


def _fgate_fwd(proj, bfp, l, S):
    T = proj.shape[0]

    def body(x_ref, b_ref, o_ref):
        z = x_ref[...] + b_ref[0:1, :]
        v = jnp.minimum(z, 0.0) - _log1p(jnp.exp(-jnp.abs(z)))
        row = lax.broadcasted_iota(jnp.int32, (S, 128), 0)
        d = 1
        while d < S:
            v = v + jnp.where(row >= d, pltpu.roll(v, d, 0), 0.0)
            d *= 2
        o_ref[...] = v

    return pl.pallas_call(
        body, name=f"fgate_fwd_{l}",
        grid=(T // S,),
        in_specs=[pl.BlockSpec((S, 128), lambda b: (b, F_BLK)),
                  pl.BlockSpec((None, 8, 128), lambda b: (l, 0, 0))],
        out_specs=pl.BlockSpec((S, 128), lambda b: (b, 0)),
        out_shape=jax.ShapeDtypeStruct((T, 128), F32),
        compiler_params=_cp(("arbitrary",)),
    )(proj, bfp)


def _fgate_bwd(dcum, proj, bfp, l, S):
    T = proj.shape[0]

    def body(d_ref, x_ref, b_ref, o_ref, wacc_ref):
        @pl.when(pl.program_id(0) == 0)
        def _():
            wacc_ref[...] = jnp.zeros_like(wacc_ref)

        v = d_ref[...]
        row = lax.broadcasted_iota(jnp.int32, (S, 128), 0)
        d = 1
        while d < S:
            v = v + jnp.where(row < S - d, pltpu.roll(v, S - d, 0), 0.0)
            d *= 2
        z = x_ref[...] + b_ref[0:1, :]
        dz = v * _sigmoid(-z)
        o_ref[...] = dz.astype(BF16)
        wacc_ref[0:1, :] += _colsum(dz)

    return pl.pallas_call(
        body, name=f"fgate_bwd_{l}",
        grid=(T // S,),
        in_specs=[pl.BlockSpec((S, 128), lambda b: (b, 0)),
                  pl.BlockSpec((S, 128), lambda b: (b, F_BLK)),
                  pl.BlockSpec((None, 8, 128), lambda b: (l, 0, 0))],
        out_specs=[pl.BlockSpec((S, 128), lambda b: (b, 0)), pl.BlockSpec((8, 128), lambda b: (0, 0))],
        out_shape=[jax.ShapeDtypeStruct((T, 128), BF16), jax.ShapeDtypeStruct((8, 128), F32)],
        compiler_params=_cp(("arbitrary",)),
    )(dcum, proj, bfp)


def _logsig_parts(z):
    e = jnp.exp(-jnp.abs(z))
    l1p = jnp.log(1.0 + e)
    return e, jnp.minimum(z, 0.0) - l1p, -jnp.maximum(z, 0.0) - l1p


def _sb_fwd(q, k, v, l):
    B, H, nq, tq, _ = q.shape
    nk, tk = k.shape[2], k.shape[3]
    rr = tq // tk

    def body(q_ref, k_ref, v_ref, o_ref, t1_ref):
        tri = _tri(tk, "row_gt_col")
        ti = lax.broadcasted_iota(jnp.int32, (tq, 1), 0)
        si = lax.broadcasted_iota(jnp.int32, (1, tk), 1)

        def qloop(qb, carry):
            qq = q_ref[qb]
            tpos = qb * tq + ti
            nkb = (qb + 1) * rr

            def kloop(i, c):
                acc, run = c
                kb = nkb - 1 - i
                z = _dot_nt(qq, k_ref[kb]) * SCALE
                past = (kb * tk + si) < tpos
                _, lb, l1 = _logsig_parts(z)
                l1m = jnp.where(past, l1, 0.0)
                aft = _cumsum_mm(l1m, tri) + run
                w = jnp.where(past, jnp.exp(lb + aft), 0.0)
                acc = acc + _dot(w.astype(BF16), v_ref[kb])
                return acc, run + _rowsum(l1m)

            acc, run = lax.fori_loop(0, nkb, kloop, (jnp.zeros((tq, HD), F32), jnp.zeros((tq, 1), F32)))
            o_ref[qb] = acc
            t1_ref[qb] = run
            return carry

        lax.fori_loop(0, nq, qloop, 0)

    qs = pl.BlockSpec((None, None, nq, tq, HD), lambda b, h: (b, h, 0, 0, 0))
    ks = pl.BlockSpec((None, None, nk, tk, HD), lambda b, h: (b, h, 0, 0, 0))
    return pl.pallas_call(
        body, name=f"sb_fwd_{l}",
        grid=(B, H),
        in_specs=[qs, ks, ks],
        out_specs=[qs, pl.BlockSpec((None, None, nq, tq, 1), lambda b, h: (b, h, 0, 0, 0))],
        out_shape=[jax.ShapeDtypeStruct((B, H, nq, tq, HD), F32),
                   jax.ShapeDtypeStruct((B, H, nq, tq, 1), F32)],
        compiler_params=_cp(("arbitrary", "arbitrary"), VMEM_BIG),
    )(q, k, v)


def _sb_bwd(q, k, v, do, t1, l):
    B, H, nq, tq, _ = q.shape
    nk, tk = k.shape[2], k.shape[3]
    rr = tq // tk

    def body(q_ref, k_ref, v_ref, do_ref, t1_ref, dq_ref, dk_ref, dv_ref, dk_sc, dv_sc):
        dk_sc[...] = jnp.zeros_like(dk_sc)
        dv_sc[...] = jnp.zeros_like(dv_sc)
        tri_in = _tri(tk, "row_le_col")
        tri_ex = _tri(tk, "row_lt_col")
        ti = lax.broadcasted_iota(jnp.int32, (tq, 1), 0)
        si = lax.broadcasted_iota(jnp.int32, (1, tk), 1)

        def qloop(qb, carry):
            qq = q_ref[qb]
            dob = do_ref[qb].astype(BF16)
            tot = t1_ref[qb]
            tpos = qb * tq + ti
            nkb = (qb + 1) * rr

            def kloop(kb, c):
                dq, run1, rung = c
                kk = k_ref[kb]
                vv = v_ref[kb]
                z = _dot_nt(qq, kk) * SCALE
                past = (kb * tk + si) < tpos
                e, lb, l1 = _logsig_parts(z)
                l1m = jnp.where(past, l1, 0.0)
                aft = tot - (run1 + _cumsum_mm(l1m, tri_in))
                w = jnp.where(past, jnp.exp(lb + aft), 0.0)
                gm = w * _dot_nt(dob, vv)
                cpre = rung + _cumsum_mm(gm, tri_ex, parts=2)
                inv = 1.0 / (1.0 + e)
                sig = jnp.where(z >= 0.0, inv, e * inv)
                dz = jnp.where(past, gm * (1.0 - sig) - cpre * sig, 0.0).astype(BF16)
                dv_sc[kb] += _dot_tn(w.astype(BF16), dob)
                dk_sc[kb] += _dot_tn(dz, qq) * SCALE
                dq = dq + _dot(dz, kk) * SCALE
                return dq, run1 + _rowsum(l1m), rung + _rowsum(gm)

            z1 = jnp.zeros((tq, 1), F32)
            dq, _, _ = lax.fori_loop(0, nkb, kloop, (jnp.zeros((tq, HD), F32), z1, z1))
            dq_ref[qb] = dq.astype(BF16)
            return carry

        lax.fori_loop(0, nq, qloop, 0)
        dk_ref[...] = dk_sc[...].astype(BF16)
        dv_ref[...] = dv_sc[...].astype(BF16)

    qs = pl.BlockSpec((None, None, nq, tq, HD), lambda b, h: (b, h, 0, 0, 0))
    ks = pl.BlockSpec((None, None, nk, tk, HD), lambda b, h: (b, h, 0, 0, 0))
    return pl.pallas_call(
        body, name=f"sb_bwd_{l}",
        grid=(B, H),
        in_specs=[qs, ks, ks, qs, pl.BlockSpec((None, None, nq, tq, 1), lambda b, h: (b, h, 0, 0, 0))],
        out_specs=[qs, ks, ks],
        out_shape=[jax.ShapeDtypeStruct((B, H, nq, tq, HD), BF16),
                   jax.ShapeDtypeStruct((B, H, nk, tk, HD), BF16),
                   jax.ShapeDtypeStruct((B, H, nk, tk, HD), BF16)],
        scratch_shapes=[pltpu.VMEM((nk, tk, HD), F32), pltpu.VMEM((nk, tk, HD), F32)],
        compiler_params=_cp(("arbitrary", "arbitrary"), VMEM_BIG),
    )(q, k, v, do, t1)


def _fox_fwd(q, k, v, cq, ck, gqk, l):
    B, H, nq, tq, _ = q.shape
    nk, tk = k.shape[2], k.shape[3]
    rr = tq // tk

    def body(q_ref, k_ref, v_ref, cq_ref, ck_ref, g_ref, o_ref, lse_ref, fk_sc):
        g0 = g_ref[0:1, :]
        g1 = g_ref[1:2, :]

        def kprep(kb, c):
            kn, _ = _rms_rows(k_ref[kb])
            fk_sc[kb] = (kn * g1).astype(BF16)
            return c

        lax.fori_loop(0, nk, kprep, 0)
        ti = lax.broadcasted_iota(jnp.int32, (tq, 1), 0)
        si = lax.broadcasted_iota(jnp.int32, (1, tk), 1)

        def qloop(qb, carry):
            qn, _ = _rms_rows(q_ref[qb])
            fq = (qn * g0).astype(BF16)
            cqq = cq_ref[qb]
            tpos = qb * tq + ti

            def kloop(kb, c):
                m, lsum, acc = c
                s = _dot_nt(fq, fk_sc[kb]) * SCALE + cqq - ck_ref[kb]
                s = jnp.where((kb * tk + si) <= tpos, s, NEG)
                m2 = jnp.maximum(m, jnp.max(s, axis=1, keepdims=True))
                al = jnp.exp(m - m2)
                p = jnp.exp(s - m2)
                return m2, al * lsum + _rowsum(p), al * acc + _dot(p.astype(BF16), v_ref[kb])

            m, lsum, acc = lax.fori_loop(
                0, (qb + 1) * rr, kloop,
                (jnp.full((tq, 1), NEG, F32), jnp.zeros((tq, 1), F32), jnp.zeros((tq, HD), F32)))
            o_ref[qb] = acc / lsum
            lse_ref[qb] = m + jnp.log(lsum)
            return carry

        lax.fori_loop(0, nq, qloop, 0)

    qs = pl.BlockSpec((None, None, nq, tq, HD), lambda b, h: (b, h, 0, 0, 0))
    ks = pl.BlockSpec((None, None, nk, tk, HD), lambda b, h: (b, h, 0, 0, 0))
    cqs = pl.BlockSpec((None, None, nq, tq, 1), lambda b, h: (b, h, 0, 0, 0))
    cks = pl.BlockSpec((None, None, nk, 1, tk), lambda b, h: (b, h, 0, 0, 0))
    return pl.pallas_call(
        body, name=f"fox_fwd_{l}",
        grid=(B, H),
        in_specs=[qs, ks, ks, cqs, cks, pl.BlockSpec((None, 8, HD), lambda b, h: (l, 0, 0))],
        out_specs=[qs, cqs],
        out_shape=[jax.ShapeDtypeStruct((B, H, nq, tq, HD), F32),
                   jax.ShapeDtypeStruct((B, H, nq, tq, 1), F32)],
        scratch_shapes=[pltpu.VMEM((nk, tk, HD), BF16)],
        compiler_params=_cp(("arbitrary", "arbitrary"), VMEM_BIG),
    )(q, k, v, cq, ck, gqk)


def _fox_bwd(q, k, v, cq, ck, gqk, do, lse, l):
    B, H, nq, tq, _ = q.shape
    nk, tk = k.shape[2], k.shape[3]
    rr = tq // tk

    def body(q_ref, k_ref, v_ref, cq_ref, ck_ref, g_ref, do_ref, lse_ref,
             dq_ref, dk_ref, dv_ref, dc_ref, wacc_ref, fk_sc, dfk_sc, dv_sc):
        @pl.when((pl.program_id(0) == 0) & (pl.program_id(1) == 0))
        def _():
            wacc_ref[...] = jnp.zeros_like(wacc_ref)

        g0 = g_ref[0:1, :]
        g1 = g_ref[1:2, :]
        dfk_sc[...] = jnp.zeros_like(dfk_sc)
        dv_sc[...] = jnp.zeros_like(dv_sc)
        dc_ref[...] = jnp.zeros_like(dc_ref)

        def kprep(kb, c):
            kn, _ = _rms_rows(k_ref[kb])
            fk_sc[kb] = (kn * g1).astype(BF16)
            return c

        lax.fori_loop(0, nk, kprep, 0)
        ti = lax.broadcasted_iota(jnp.int32, (tq, 1), 0)
        si = lax.broadcasted_iota(jnp.int32, (1, tk), 1)

        def qloop(qb, carry):
            qn, qr = _rms_rows(q_ref[qb])
            fq = (qn * g0).astype(BF16)
            cqq = cq_ref[qb]
            lse = lse_ref[qb]
            dob = do_ref[qb].astype(BF16)
            tpos = qb * tq + ti

            def probs(kb):
                s = _dot_nt(fq, fk_sc[kb]) * SCALE + cqq - ck_ref[kb]
                p = jnp.where((kb * tk + si) <= tpos, jnp.exp(s - lse), 0.0)
                return p, _dot_nt(dob, v_ref[kb])

            def dloop(kb, acc):
                p, dp = probs(kb)
                return acc + _rowsum(p * dp)

            dlt = lax.fori_loop(0, (qb + 1) * rr, dloop, jnp.zeros((tq, 1), F32))

            def kloop(kb, dfq):
                fk = fk_sc[kb]
                p, dp = probs(kb)
                ds = p * (dp - dlt)
                dsb = ds.astype(BF16)
                dv_sc[kb] += _dot_tn(p.astype(BF16), dob)
                dfk_sc[kb] += _dot_tn(dsb, fq) * SCALE
                dc_ref[kb] += jnp.broadcast_to(-_colsum(ds), (8, tk))
                return dfq + _dot(dsb, fk) * SCALE

            dfq = lax.fori_loop(0, (qb + 1) * rr, kloop, jnp.zeros((tq, HD), F32))
            wacc_ref[0:1, :] += _colsum(dfq * qn)
            dq_ref[qb] = _rms_bwd(qn, qr, dfq * g0).astype(BF16)
            return carry

        lax.fori_loop(0, nq, qloop, 0)

        def kfin(kb, c):
            kn, kr = _rms_rows(k_ref[kb])
            dfk = dfk_sc[kb]
            wacc_ref[1:2, :] += _colsum(dfk * kn)
            dk_ref[kb] = _rms_bwd(kn, kr, dfk * g1).astype(BF16)
            return c

        lax.fori_loop(0, nk, kfin, 0)
        dv_ref[...] = dv_sc[...].astype(BF16)

    qs = pl.BlockSpec((None, None, nq, tq, HD), lambda b, h: (b, h, 0, 0, 0))
    ks = pl.BlockSpec((None, None, nk, tk, HD), lambda b, h: (b, h, 0, 0, 0))
    cqs = pl.BlockSpec((None, None, nq, tq, 1), lambda b, h: (b, h, 0, 0, 0))
    cks = pl.BlockSpec((None, None, nk, 1, tk), lambda b, h: (b, h, 0, 0, 0))
    return pl.pallas_call(
        body, name=f"fox_bwd_{l}",
        grid=(B, H),
        in_specs=[qs, ks, ks, cqs, cks, pl.BlockSpec((None, 8, HD), lambda b, h: (l, 0, 0)), qs, cqs],
        out_specs=[qs, ks, ks,
                   pl.BlockSpec((None, None, nk, 8, tk), lambda b, h: (b, h, 0, 0, 0)),
                   pl.BlockSpec((8, HD), lambda b, h: (0, 0))],
        out_shape=[jax.ShapeDtypeStruct((B, H, nq, tq, HD), BF16),
                   jax.ShapeDtypeStruct((B, H, nk, tk, HD), BF16),
                   jax.ShapeDtypeStruct((B, H, nk, tk, HD), BF16),
                   jax.ShapeDtypeStruct((B, H, nk, 8, tk), F32),
                   jax.ShapeDtypeStruct((8, HD), F32)],
        scratch_shapes=[pltpu.VMEM((nk, tk, HD), BF16), pltpu.VMEM((nk, tk, HD), F32),
                        pltpu.VMEM((nk, tk, HD), F32)],
        compiler_params=_cp(("arbitrary", "arbitrary"), VMEM_BIG),
    )(q, k, v, cq, ck, gqk, do, lse)


SBQ_BLK, SBK_BLK, SBV_BLK = 8, 10, 12
FXQ_BLK, FXK_BLK, FXV_BLK = 14, 16, 18
PAIR = 2 * HD


def _lane_masks():
    lane = lax.broadcasted_iota(jnp.int32, (1, PAIR), 1)
    return lane, lane < HD


def _pair_select(m0, a0, a1):
    return jnp.where(m0, a0, a1)


def _pair_split(x, m0):
    return jnp.where(m0, x, 0.0).astype(BF16), jnp.where(m0, 0.0, x).astype(BF16)


def _pair_mean(x, m0):
    s0 = _rowsum(jnp.where(m0, x, 0.0))
    s1 = _rowsum(x) - s0
    return jnp.where(m0, s0, s1) * (1.0 / HD)


def _pair_rms(x, m0):
    rstd = lax.rsqrt(_pair_mean(x * x, m0) + EPS)
    return x * rstd, rstd


def _pair_rms_bwd(xn, rstd, dyn, m0):
    return rstd * (dyn - xn * _pair_mean(dyn * xn, m0))


def _logsig2(z):
    l1p = jnp.log(1.0 + jnp.exp(-jnp.abs(z)))
    lb = jnp.minimum(z, 0.0) - l1p
    return lb, lb - z


def _rows(ref, blk, size):
    return ref[pl.ds(pl.multiple_of(blk * size, size), size), :]


def _sbp_fwd(proj, l, S):
    T = proj.shape[0]
    tq, tk = TQ_(S), TK_(S)
    assert tq == 2 * tk
    nq = S // tq

    def body(q_ref, k_ref, v_ref, o_ref, t1_ref, kb_sc, vb_sc):
        kb_sc[...] = k_ref[...].astype(BF16)
        vb_sc[...] = v_ref[...].astype(BF16)
        lane, m0 = _lane_masks()
        tri = _tri(tk, "row_gt_col")
        ti = lax.broadcasted_iota(jnp.int32, (tq, 1), 0)
        si = lax.broadcasted_iota(jnp.int32, (1, tk), 1)

        def qloop(qb, carry):
            qh = _pair_split(_rows(q_ref, qb, tq) * SCALE, m0)
            tpos = qb * tq + ti

            def step(kbs, c, masked):
                pre = []
                for h in range(2):
                    for kb in kbs:
                        z = _dot_nt(qh[h], _rows(kb_sc, kb, tk))
                        lb, l1 = _logsig2(z)
                        past = None
                        if masked:
                            past = (kb * tk + si) < tpos
                            l1 = jnp.where(past, l1, 0.0)
                        pre.append((lb, l1, _cumsum_mm(l1, tri, parts=2), past))
                out = []
                for h in range(2):
                    acc, run = c[h]
                    for n, kb in enumerate(kbs):
                        lb, l1, cs, past = pre[2 * h + n]
                        w = jnp.exp(lb + (cs + run))
                        if masked:
                            w = jnp.where(past, w, 0.0)
                        acc = acc + _dot(w.astype(BF16), _rows(vb_sc, kb, tk))
                        run = run + (cs[:, 0:1] + l1[:, 0:1])
                    out.append((acc, run))
                return tuple(out)

            zero = (jnp.zeros((tq, PAIR), F32), jnp.zeros((tq, 1), F32))
            c = step((2 * qb + 1, 2 * qb), (zero, zero), True)
            c = lax.fori_loop(0, qb, lambda i, cc: step((2 * (qb - i) - 1, 2 * (qb - i) - 2), cc, False), c)
            r0 = pl.multiple_of(qb * tq, tq)
            o_ref[pl.ds(r0, tq), :] = _pair_select(m0, c[0][0], c[1][0])
            t1_ref[pl.ds(r0, tq), :] = jnp.where(lane == 0, c[0][1], jnp.where(lane == 1, c[1][1], 0.0))
            return carry

        lax.fori_loop(0, nq, qloop, 0)

    def col(blk):
        return pl.BlockSpec((S, PAIR), lambda b, p: (b, blk + p))

    return pl.pallas_call(
        body, name=f"sb_fwd_{l}",
        grid=(T // S, 2),
        in_specs=[col(SBQ_BLK), col(SBK_BLK), col(SBV_BLK)],
        out_specs=[col(0), col(0)],
        out_shape=[jax.ShapeDtypeStruct((T, AW), F32), jax.ShapeDtypeStruct((T, AW), F32)],
        scratch_shapes=[pltpu.VMEM((S, PAIR), BF16), pltpu.VMEM((S, PAIR), BF16)],
        compiler_params=_cp(("arbitrary", "arbitrary"), VMEM_BIG),
    )(proj, proj, proj)


def _sbp_bwd(proj, do, t1, l, S):
    T = proj.shape[0]
    tq, tk = TQ_(S), TK_(S)
    assert tq == 2 * tk
    nq = S // tq

    def body(q_ref, k_ref, v_ref, do_ref, t1_ref, dq_ref, dk_ref, dv_ref, kb_sc, vb_sc, dk_sc, dv_sc):
        kb_sc[...] = k_ref[...].astype(BF16)
        vb_sc[...] = v_ref[...].astype(BF16)
        dk_sc[...] = jnp.zeros_like(dk_sc)
        dv_sc[...] = jnp.zeros_like(dv_sc)
        _, m0 = _lane_masks()
        tri_in = _tri(tk, "row_le_col")
        tri_ex = _tri(tk, "row_lt_col")
        ti = lax.broadcasted_iota(jnp.int32, (tq, 1), 0)
        si = lax.broadcasted_iota(jnp.int32, (1, tk), 1)

        def qloop(qb, carry):
            qh = _pair_split(_rows(q_ref, qb, tq) * SCALE, m0)
            doh = _pair_split(_rows(do_ref, qb, tq), m0)
            t1v = _rows(t1_ref, qb, tq)
            tot = (t1v[:, 0:1], t1v[:, 1:2])
            tpos = qb * tq + ti

            def step(kbs, c, masked):
                pre = []
                for h in range(2):
                    for kb in kbs:
                        kk = _rows(kb_sc, kb, tk)
                        z = _dot_nt(qh[h], kk)
                        lb, l1 = _logsig2(z)
                        past = None
                        if masked:
                            past = (kb * tk + si) < tpos
                            l1 = jnp.where(past, l1, 0.0)
                        sig = jnp.exp(lb)
                        pre.append((lb, sig, _cumsum_mm(l1, tri_in), _dot_nt(doh[h], _rows(vb_sc, kb, tk)), past, kk))
                out = []
                for h in range(2):
                    dq, run1, rung = c[h]
                    for n, kb in enumerate(kbs):
                        lb, sig, p1, dw, past, kk = pre[2 * h + n]
                        w = jnp.exp(lb + (tot[h] - (run1 + p1)))
                        if masked:
                            w = jnp.where(past, w, 0.0)
                        gm = w * dw
                        cx = _cumsum_mm(gm, tri_ex, parts=2)
                        dz = gm - (gm + (rung + cx)) * sig
                        if masked:
                            dz = jnp.where(past, dz, 0.0)
                        dz = dz.astype(BF16)
                        r = pl.ds(pl.multiple_of(kb * tk, tk), tk)
                        dv_sc[r, :] += _dot_tn(w.astype(BF16), doh[h])
                        dk_sc[r, :] += _dot_tn(dz, qh[h])
                        dq = dq + _dot(dz, kk)
                        run1 = run1 + p1[:, tk - 1:tk]
                        rung = rung + (cx[:, tk - 1:tk] + gm[:, tk - 1:tk])
                    out.append((dq, run1, rung))
                return tuple(out)

            z1 = jnp.zeros((tq, 1), F32)
            zero = (jnp.zeros((tq, PAIR), F32), z1, z1)
            c = lax.fori_loop(0, qb, lambda i, cc: step((2 * i, 2 * i + 1), cc, False), (zero, zero))
            c = step((2 * qb, 2 * qb + 1), c, True)
            r0 = pl.multiple_of(qb * tq, tq)
            dq_ref[pl.ds(r0, tq), :] = (_pair_select(m0, c[0][0], c[1][0]) * SCALE).astype(BF16)
            return carry

        lax.fori_loop(0, nq, qloop, 0)
        dk_ref[...] = dk_sc[...].astype(BF16)
        dv_ref[...] = dv_sc[...].astype(BF16)

    def col(blk):
        return pl.BlockSpec((S, PAIR), lambda b, p: (b, blk + p))

    sh = jax.ShapeDtypeStruct((T, AW), BF16)
    return pl.pallas_call(
        body, name=f"sb_bwd_{l}",
        grid=(T // S, 2),
        in_specs=[col(SBQ_BLK), col(SBK_BLK), col(SBV_BLK), col(0), col(0)],
        out_specs=[col(0), col(0), col(0)],
        out_shape=[sh, sh, sh],
        scratch_shapes=[pltpu.VMEM((S, PAIR), BF16), pltpu.VMEM((S, PAIR), BF16),
                        pltpu.VMEM((S, PAIR), F32), pltpu.VMEM((S, PAIR), F32)],
        compiler_params=_cp(("arbitrary", "arbitrary"), VMEM_BIG),
    )(proj, proj, proj, do, t1)


def _foxp_fwd(proj, cum, ck, gqk2, l, S):
    T = proj.shape[0]
    tq, tk = TQ_(S), TK_(S)
    assert tq == 2 * tk
    nq, nk = S // tq, S // tk

    def body(q_ref, k_ref, v_ref, cum_ref, ck_ref, g_ref, o_ref, nl_ref, fk_sc, vb_sc):
        lane, m0 = _lane_masks()
        p = pl.program_id(1)
        kn, _ = _pair_rms(k_ref[...], m0)
        fk_sc[...] = (kn * g_ref[1:2, :]).astype(BF16)
        vb_sc[...] = v_ref[...].astype(BF16)
        ti = lax.broadcasted_iota(jnp.int32, (tq, 1), 0)
        si = lax.broadcasted_iota(jnp.int32, (1, tk), 1)

        def qloop(qb, carry):
            qn, _ = _pair_rms(_rows(q_ref, qb, tq), m0)
            fqh = _pair_split(qn * (g_ref[0:1, :] * SCALE), m0)
            cumv = _rows(cum_ref, qb, tq)
            cq = [_rowsum(jnp.where(lane == 2 * p + h, cumv, 0.0)) for h in range(2)]
            tpos = qb * tq + ti

            def step(kbs, c, masked):
                out = []
                for h in range(2):
                    m, lsum, acc = c[h]
                    ss = []
                    for kb in kbs:
                        s = _dot_nt(fqh[h], _rows(fk_sc, kb, tk)) + (cq[h] - ck_ref[h, kb])
                        if masked:
                            s = jnp.where((kb * tk + si) <= tpos, s, NEG)
                        ss.append(s)
                    m2 = jnp.maximum(m, jnp.maximum(jnp.max(ss[0], axis=1, keepdims=True),
                                                    jnp.max(ss[1], axis=1, keepdims=True)))
                    al = jnp.exp(m - m2)
                    lsum = al * lsum
                    acc = al * acc
                    for s, kb in zip(ss, kbs):
                        pr = jnp.exp(s - m2)
                        lsum = lsum + _rowsum(pr)
                        acc = acc + _dot(pr.astype(BF16), _rows(vb_sc, kb, tk))
                    out.append((m2, lsum, acc))
                return tuple(out)

            zero = (jnp.full((tq, 1), NEG, F32), jnp.zeros((tq, 1), F32), jnp.zeros((tq, PAIR), F32))
            c = lax.fori_loop(0, qb, lambda i, cc: step((2 * i, 2 * i + 1), cc, False), (zero, zero))
            c = step((2 * qb, 2 * qb + 1), c, True)
            r0 = pl.multiple_of(qb * tq, tq)
            o_ref[pl.ds(r0, tq), :] = _pair_select(m0, c[0][2] / c[0][1], c[1][2] / c[1][1])
            nl = [cq[h] - (c[h][0] + jnp.log(c[h][1])) for h in range(2)]
            nl_ref[pl.ds(r0, tq), :] = jnp.where(lane == 0, nl[0], jnp.where(lane == 1, nl[1], 0.0))
            return carry

        lax.fori_loop(0, nq, qloop, 0)

    def col(blk):
        return pl.BlockSpec((S, PAIR), lambda b, p: (b, blk + p))

    return pl.pallas_call(
        body, name=f"fox_fwd_{l}",
        grid=(T // S, 2),
        in_specs=[col(FXQ_BLK), col(FXK_BLK), col(FXV_BLK),
                  pl.BlockSpec((S, 128), lambda b, p: (b, 0)),
                  pl.BlockSpec((None, 2, nk, 1, tk), lambda b, p: (b, p, 0, 0, 0)),
                  pl.BlockSpec((None, 8, PAIR), lambda b, p: (l, 0, 0))],
        out_specs=[col(0), col(0)],
        out_shape=[jax.ShapeDtypeStruct((T, AW), F32), jax.ShapeDtypeStruct((T, AW), F32)],
        scratch_shapes=[pltpu.VMEM((S, PAIR), BF16), pltpu.VMEM((S, PAIR), BF16)],
        compiler_params=_cp(("arbitrary", "arbitrary"), VMEM_BIG),
    )(proj, proj, proj, cum, ck, gqk2)


def _foxp_bwd(proj, do, nl, ck, gqk2, l, S):
    T = proj.shape[0]
    tq, tk = TQ_(S), TK_(S)
    assert tq == 2 * tk
    nq, nk = S // tq, S // tk

    def body(q_ref, k_ref, v_ref, do_ref, nl_ref, ck_ref, g_ref,
             dq_ref, dk_ref, dv_ref, dc_ref, wacc_ref, fk_sc, vb_sc, dfk_sc, dv_sc):
        @pl.when((pl.program_id(0) == 0) & (pl.program_id(1) == 0))
        def _():
            wacc_ref[...] = jnp.zeros_like(wacc_ref)

        _, m0 = _lane_masks()
        g0 = g_ref[0:1, :]
        g1 = g_ref[1:2, :]
        kn, kr = _pair_rms(k_ref[...], m0)
        fk_sc[...] = (kn * g1).astype(BF16)
        vb_sc[...] = v_ref[...].astype(BF16)
        dfk_sc[...] = jnp.zeros_like(dfk_sc)
        dv_sc[...] = jnp.zeros_like(dv_sc)
        dc_ref[...] = jnp.zeros_like(dc_ref)
        ti = lax.broadcasted_iota(jnp.int32, (tq, 1), 0)
        si = lax.broadcasted_iota(jnp.int32, (1, tk), 1)

        def qloop(qb, carry):
            qn, qr = _pair_rms(_rows(q_ref, qb, tq), m0)
            fqh = _pair_split(qn * (g0 * SCALE), m0)
            doh = _pair_split(_rows(do_ref, qb, tq), m0)
            nlv = _rows(nl_ref, qb, tq)
            cql = (nlv[:, 0:1], nlv[:, 1:2])
            tpos = qb * tq + ti

            def probs(h, kb, masked):
                s = _dot_nt(fqh[h], _rows(fk_sc, kb, tk)) + (cql[h] - ck_ref[h, kb])
                pr = jnp.exp(s)
                if masked:
                    pr = jnp.where((kb * tk + si) <= tpos, pr, 0.0)
                return pr, _dot_nt(doh[h], _rows(vb_sc, kb, tk))

            def dstep(kbs, c, masked):
                out = []
                for h in range(2):
                    acc = c[h]
                    for kb in kbs:
                        pr, dp = probs(h, kb, masked)
                        acc = acc + _rowsum(pr * dp)
                    out.append(acc)
                return tuple(out)

            z1 = jnp.zeros((tq, 1), F32)
            dlt = lax.fori_loop(0, qb, lambda i, cc: dstep((2 * i, 2 * i + 1), cc, False), (z1, z1))
            dlt = dstep((2 * qb, 2 * qb + 1), dlt, True)

            def step(kbs, c, masked):
                out = []
                for h in range(2):
                    dfq = c[h]
                    for kb in kbs:
                        pr, dp = probs(h, kb, masked)
                        ds = pr * (dp - dlt[h])
                        dsb = ds.astype(BF16)
                        r = pl.ds(pl.multiple_of(kb * tk, tk), tk)
                        dv_sc[r, :] += _dot_tn(pr.astype(BF16), doh[h])
                        dfk_sc[r, :] += _dot_tn(dsb, fqh[h])
                        dc_ref[h, kb] += jnp.broadcast_to(-_colsum(ds), (8, tk))
                        dfq = dfq + _dot(dsb, _rows(fk_sc, kb, tk))
                    out.append(dfq)
                return tuple(out)

            zq = jnp.zeros((tq, PAIR), F32)
            c = lax.fori_loop(0, qb, lambda i, cc: step((2 * i, 2 * i + 1), cc, False), (zq, zq))
            c = step((2 * qb, 2 * qb + 1), c, True)
            dfq = _pair_select(m0, c[0], c[1]) * SCALE
            wacc_ref[0:1, :] += _colsum(dfq * qn)
            r0 = pl.multiple_of(qb * tq, tq)
            dq_ref[pl.ds(r0, tq), :] = _pair_rms_bwd(qn, qr, dfq * g0, m0).astype(BF16)
            return carry

        lax.fori_loop(0, nq, qloop, 0)
        dfk = dfk_sc[...]
        wacc_ref[1:2, :] += _colsum(dfk * kn)
        dk_ref[...] = _pair_rms_bwd(kn, kr, dfk * g1, m0).astype(BF16)
        dv_ref[...] = dv_sc[...].astype(BF16)

    def col(blk):
        return pl.BlockSpec((S, PAIR), lambda b, p: (b, blk + p))

    sh = jax.ShapeDtypeStruct((T, AW), BF16)
    return pl.pallas_call(
        body, name=f"fox_bwd_{l}",
        grid=(T // S, 2),
        in_specs=[col(FXQ_BLK), col(FXK_BLK), col(FXV_BLK), col(0), col(0),
                  pl.BlockSpec((None, 2, nk, 1, tk), lambda b, p: (b, p, 0, 0, 0)),
                  pl.BlockSpec((None, 8, PAIR), lambda b, p: (l, 0, 0))],
        out_specs=[col(0), col(0), col(0),
                   pl.BlockSpec((None, 2, nk, 8, tk), lambda b, p: (b, p, 0, 0, 0)),
                   pl.BlockSpec((8, PAIR), lambda b, p: (0, 0))],
        out_shape=[sh, sh, sh,
                   jax.ShapeDtypeStruct((T // S, NH, nk, 8, tk), F32),
                   jax.ShapeDtypeStruct((8, PAIR), F32)],
        scratch_shapes=[pltpu.VMEM((S, PAIR), BF16), pltpu.VMEM((S, PAIR), BF16),
                        pltpu.VMEM((S, PAIR), F32), pltpu.VMEM((S, PAIR), F32)],
        compiler_params=_cp(("arbitrary", "arbitrary"), VMEM_BIG),
    )(proj, proj, proj, do, nl, ck, gqk2)


def _transpose_blocks(src_ref, dst_sc, nblk, blk):
    for kb in range(nblk):
        dst_sc[kb] = src_ref[kb * blk:(kb + 1) * blk, :].astype(F32).T.astype(BF16)


def _sbq_fwd(proj, l, S, comm=None):
    T = proj.shape[0]
    tb = TQ_(S)
    nb = S // tb
    nbat = T // S
    c_args, c_specs, c_outs, c_scr = _hosted(comm)
    n_ci, n_co = len(c_args), len(c_outs)

    def body(*refs):
        q_ref, k_ref, v_ref = refs[:3]
        c_in = refs[3:3 + n_ci]
        o_ref, t1_ref = refs[3 + n_ci:5 + n_ci]
        c_out = refs[5 + n_ci:5 + n_ci + n_co]
        kt_sc, vb_sc = refs[5 + n_ci + n_co:7 + n_ci + n_co]
        c_sems = refs[7 + n_ci + n_co:]
        step = pl.program_id(0) * 2 + pl.program_id(1)
        if comm is not None:
            @pl.when(step == 0)
            def _():
                comm["start"](c_in, c_out, c_sems)

        _transpose_blocks(k_ref, kt_sc, nb, tb)
        vb_sc[...] = v_ref[...].astype(BF16)
        lane, m0 = _lane_masks()
        tri = _tri(tb, "row_gt_col")
        past = lax.broadcasted_iota(jnp.int32, (tb, tb), 1) < lax.broadcasted_iota(jnp.int32, (tb, tb), 0)

        def qloop(qb, carry):
            qh = _pair_split(_rows(q_ref, qb, tb) * SCALE, m0)

            def scores(kb):
                return tuple(_dot(qh[h], kt_sc[kb]) for h in range(2))

            def block(kb, kb_next, z, c, masked):
                mid = []
                for h in range(2):
                    lb, l1 = _logsig2(z[h])
                    if masked:
                        l1 = jnp.where(past, l1, 0.0)
                    mid.append((lb, l1, _cumsum_mm(l1, tri, parts=2)))
                z_next = scores(kb_next)
                pv, runs = [], []
                for h in range(2):
                    lb, l1, cs = mid[h]
                    w = jnp.exp(lb + (cs + c[h][1]))
                    if masked:
                        w = jnp.where(past, w, 0.0)
                    pv.append(_dot(w.astype(BF16), _rows(vb_sc, kb, tb)))
                    runs.append(c[h][1] + (cs[:, 0:1] + l1[:, 0:1]))
                return z_next, tuple((c[h][0] + pv[h], runs[h]) for h in range(2))

            zero = (jnp.zeros((tb, PAIR), F32), jnp.zeros((tb, 1), F32))
            z, c = block(qb, jnp.maximum(qb - 1, 0), scores(qb), (zero, zero), True)

            def off_diag(i, zc):
                kb = qb - 1 - i
                return block(kb, jnp.maximum(kb - 1, 0), zc[0], zc[1], False)

            _, c = lax.fori_loop(0, qb, off_diag, (z, c))
            r0 = pl.multiple_of(qb * tb, tb)
            o_ref[pl.ds(r0, tb), :] = _pair_select(m0, c[0][0], c[1][0])
            t1_ref[pl.ds(r0, tb), :] = jnp.where(lane == 0, c[0][1], jnp.where(lane == 1, c[1][1], 0.0))
            return carry

        lax.fori_loop(0, nb, qloop, 0)
        if comm is not None:
            @pl.when(step == 2 * nbat - 1)
            def _():
                comm["finish"](c_in, c_out, c_sems)

    def col(blk):
        return pl.BlockSpec((S, PAIR), lambda b, p: (b, blk + p))

    anyspec = pl.BlockSpec(memory_space=pl.ANY)
    out = pl.pallas_call(
        body, name=f"sb_fwd_{l}",
        grid=(nbat, 2),
        in_specs=[col(SBQ_BLK), col(SBK_BLK), col(SBV_BLK)] + c_specs,
        out_specs=[col(0), col(0)] + [anyspec] * n_co,
        out_shape=[jax.ShapeDtypeStruct((T, AW), F32), jax.ShapeDtypeStruct((T, AW), F32)] + c_outs,
        scratch_shapes=[pltpu.VMEM((nb, PAIR, tb), BF16), pltpu.VMEM((S, PAIR), BF16)] + c_scr,
        compiler_params=_cp(("arbitrary", "arbitrary"), VMEM_BIG),
    )(proj, proj, proj, *c_args)
    return out[0], out[1], list(out[2:])


def _sbq_bwd(proj, do, t1, l, S, comm=None):
    T = proj.shape[0]
    tb = TQ_(S)
    nb = S // tb
    nbat = T // S
    c_args, c_specs, c_outs, c_scr = _hosted(comm)
    n_ci, n_co = len(c_args), len(c_outs)

    def body(*refs):
        q_ref, k_ref, v_ref, do_ref, t1_ref = refs[:5]
        c_in = refs[5:5 + n_ci]
        dq_ref, dk_ref, dv_ref = refs[5 + n_ci:8 + n_ci]
        c_out = refs[8 + n_ci:8 + n_ci + n_co]
        kb_sc, kt_sc, vt_sc, dkt_sc, dvt_sc = refs[8 + n_ci + n_co:13 + n_ci + n_co]
        c_sems = refs[13 + n_ci + n_co:]
        step = pl.program_id(0) * 2 + pl.program_id(1)
        if comm is not None:
            @pl.when(step == 0)
            def _():
                comm["start"](c_in, c_out, c_sems)

        kb_sc[...] = k_ref[...].astype(BF16)
        _transpose_blocks(k_ref, kt_sc, nb, tb)
        _transpose_blocks(v_ref, vt_sc, nb, tb)
        dkt_sc[...] = jnp.zeros_like(dkt_sc)
        dvt_sc[...] = jnp.zeros_like(dvt_sc)
        _, m0 = _lane_masks()
        mt0 = lax.broadcasted_iota(jnp.int32, (PAIR, 1), 0) < HD
        tri_in = _tri(tb, "row_le_col")
        tri_ex = _tri(tb, "row_lt_col")
        past = lax.broadcasted_iota(jnp.int32, (tb, tb), 1) < lax.broadcasted_iota(jnp.int32, (tb, tb), 0)

        def qloop(qb, carry):
            qf = _rows(q_ref, qb, tb) * SCALE
            dof = _rows(do_ref, qb, tb)
            qh = _pair_split(qf, m0)
            doh = _pair_split(dof, m0)
            qth = _pair_split(qf.T, mt0)
            doth = _pair_split(dof.T, mt0)
            t1v = _rows(t1_ref, qb, tb)
            tot = (t1v[:, 0:1], t1v[:, 1:2])

            def accumulate(kb, wz, dqs):
                out = []
                for h in range(2):
                    wb, dz = wz[h]
                    dvt_sc[kb] += _dot(doth[h], wb)
                    dkt_sc[kb] += _dot(qth[h], dz)
                    out.append(dqs[h] + _dot(dz, _rows(kb_sc, kb, tb)))
                return tuple(out)

            def block(kb, c, masked):
                hs = range(2)
                runs, dqs, (kb_prev, wz_prev) = c
                z = [_dot(qh[h], kt_sc[kb]) for h in hs]
                dw = [_dot(doh[h], vt_sc[kb]) for h in hs]
                st = []
                for h in hs:
                    lb, l1 = _logsig2(z[h])
                    if masked:
                        l1 = jnp.where(past, l1, 0.0)
                    st.append((lb, _cumsum_mm(l1, tri_in, parts=2)))
                dqs = accumulate(kb_prev, wz_prev, dqs)
                mid = []
                for h in hs:
                    lb, p1 = st[h]
                    w = jnp.exp(lb + (tot[h] - (runs[h][0] + p1)))
                    if masked:
                        w = jnp.where(past, w, 0.0)
                    gm = w * dw[h]
                    mid.append((w.astype(BF16), gm, _cumsum_mm(gm, tri_ex, parts=1)))
                new_runs, wz = [], []
                for h in hs:
                    run1, rung = runs[h]
                    wb, gm, cx = mid[h]
                    dz = gm - (gm + (rung + cx)) * jnp.exp(st[h][0])
                    if masked:
                        dz = jnp.where(past, dz, 0.0)
                    wz.append((wb, dz.astype(BF16)))
                    p1 = st[h][1]
                    new_runs.append((run1 + p1[:, tb - 1:tb], rung + (cx[:, tb - 1:tb] + gm[:, tb - 1:tb])))
                return tuple(new_runs), dqs, (kb, tuple(wz))

            z1 = jnp.zeros((tb, 1), F32)
            zq = jnp.zeros((tb, PAIR), F32)
            zb = jnp.zeros((tb, tb), BF16)
            none = (jnp.int32(0), ((zb, zb), (zb, zb)))
            c = lax.fori_loop(0, qb, lambda i, cc: block(i, cc, False), (((z1, z1), (z1, z1)), (zq, zq), none))
            _, dqs, (kb_last, wz_last) = block(qb, c, True)
            dqs = accumulate(kb_last, wz_last, dqs)
            r0 = pl.multiple_of(qb * tb, tb)
            dq_ref[pl.ds(r0, tb), :] = (_pair_select(m0, dqs[0], dqs[1]) * SCALE).astype(BF16)
            return carry

        lax.fori_loop(0, nb, qloop, 0)
        for kb in range(nb):
            dk_ref[kb * tb:(kb + 1) * tb, :] = dkt_sc[kb].T.astype(BF16)
            dv_ref[kb * tb:(kb + 1) * tb, :] = dvt_sc[kb].T.astype(BF16)
        if comm is not None:
            @pl.when(step == 2 * nbat - 1)
            def _():
                comm["finish"](c_in, c_out, c_sems)

    def col(blk):
        return pl.BlockSpec((S, PAIR), lambda b, p: (b, blk + p))

    sh = jax.ShapeDtypeStruct((T, AW), BF16)
    anyspec = pl.BlockSpec(memory_space=pl.ANY)
    out = pl.pallas_call(
        body, name=f"sb_bwd_{l}",
        grid=(nbat, 2),
        in_specs=[col(SBQ_BLK), col(SBK_BLK), col(SBV_BLK), col(0), col(0)] + c_specs,
        out_specs=[col(0), col(0), col(0)] + [anyspec] * n_co,
        out_shape=[sh, sh, sh] + c_outs,
        scratch_shapes=[pltpu.VMEM((S, PAIR), BF16), pltpu.VMEM((nb, PAIR, tb), BF16), pltpu.VMEM((nb, PAIR, tb), BF16),
                        pltpu.VMEM((nb, PAIR, tb), F32), pltpu.VMEM((nb, PAIR, tb), F32)] + c_scr,
        compiler_params=_cp(("arbitrary", "arbitrary"), VMEM_BIG),
    )(proj, proj, proj, do, t1, *c_args)
    return out[0], out[1], out[2], list(out[3:])


def _foxq_fwd(proj, cum, ck, gqk2, l, S):
    T = proj.shape[0]
    tb = TQ_(S)
    nb = S // tb

    def body(q_ref, k_ref, v_ref, cum_ref, ck_ref, g_ref, o_ref, nl_ref, ox_ref, fk_sc, fkt_sc, vb_sc):
        lane, m0 = _lane_masks()
        p = pl.program_id(1)
        kn, _ = _pair_rms(k_ref[...], m0)
        fk_sc[...] = kn * g_ref[1:2, :]
        _transpose_blocks(fk_sc, fkt_sc, nb, tb)
        vb_sc[...] = v_ref[...].astype(BF16)
        causal = lax.broadcasted_iota(jnp.int32, (tb, tb), 1) <= lax.broadcasted_iota(jnp.int32, (tb, tb), 0)

        def qloop(qb, carry):
            qn, _ = _pair_rms(_rows(q_ref, qb, tb), m0)
            fqh = _pair_split(qn * (g_ref[0:1, :] * SCALE), m0)
            cumv = _rows(cum_ref, qb, tb)
            cq = [_rowsum(jnp.where(lane == 2 * p + h, cumv, 0.0)) for h in range(2)]

            def scores(kb):
                return tuple(_dot(fqh[h], fkt_sc[kb]) for h in range(2))

            def block(kb, kb_next, qk, c, masked):
                st = []
                for h in range(2):
                    s = qk[h] + (cq[h] - ck_ref[h, kb])
                    if masked:
                        s = jnp.where(causal, s, NEG)
                    m2 = jnp.maximum(c[h][0], jnp.max(s, axis=1, keepdims=True))
                    pr = jnp.exp(s - m2)
                    hi = pr.astype(BF16)
                    lo = (pr - hi.astype(F32)).astype(BF16)
                    vv = _rows(vb_sc, kb, tb)
                    st.append((m2, pr, _dot(hi, vv), _dot(lo, vv)))
                qk_next = scores(kb_next)
                out = []
                for h in range(2):
                    m, lsum, acc, rest = c[h]
                    m2, pr, pv, pv_lo = st[h]
                    al = jnp.exp(m - m2)
                    out.append((m2, al * lsum + _rowsum(pr), al * acc + pv, al * rest + pv_lo))
                return qk_next, tuple(out)

            zacc = jnp.zeros((tb, PAIR), F32)
            zero = (jnp.full((tb, 1), NEG, F32), jnp.zeros((tb, 1), F32), zacc, zacc)

            def off_diag(i, sc):
                return block(i, i + 1, sc[0], sc[1], False)

            qk, c = lax.fori_loop(0, qb, off_diag, (scores(0), (zero, zero)))
            _, c = block(qb, qb, qk, c, True)
            r0 = pl.multiple_of(qb * tb, tb)
            o_ref[pl.ds(r0, tb), :] = _pair_select(m0, c[0][2] / c[0][1], c[1][2] / c[1][1])
            ox_ref[pl.ds(r0, tb), :] = _pair_select(m0, (c[0][2] + c[0][3]) / c[0][1], (c[1][2] + c[1][3]) / c[1][1])
            nl = [cq[h] - (c[h][0] + jnp.log(c[h][1])) for h in range(2)]
            nl_ref[pl.ds(r0, tb), :] = jnp.where(lane == 0, nl[0], jnp.where(lane == 1, nl[1], 0.0))
            return carry

        lax.fori_loop(0, nb, qloop, 0)

    def col(blk):
        return pl.BlockSpec((S, PAIR), lambda b, p: (b, blk + p))

    return pl.pallas_call(
        body, name=f"fox_fwd_{l}",
        grid=(T // S, 2),
        in_specs=[col(FXQ_BLK), col(FXK_BLK), col(FXV_BLK),
                  pl.BlockSpec((S, 128), lambda b, p: (b, 0)),
                  pl.BlockSpec((None, 2, nb, 1, tb), lambda b, p: (b, p, 0, 0, 0)),
                  pl.BlockSpec((None, 8, PAIR), lambda b, p: (l, 0, 0))],
        out_specs=[col(0), col(0), col(0)],
        out_shape=[jax.ShapeDtypeStruct((T, AW), F32)] * 3,
        scratch_shapes=[pltpu.VMEM((S, PAIR), F32), pltpu.VMEM((nb, PAIR, tb), BF16), pltpu.VMEM((S, PAIR), BF16)],
        compiler_params=_cp(("arbitrary", "arbitrary"), VMEM_BIG),
    )(proj, proj, proj, cum, ck, gqk2)


def _foxq_bwd(proj, do, nl, ox, ck, gqk2, l, S):
    T = proj.shape[0]
    tb = TQ_(S)
    nb = S // tb

    def body(q_ref, k_ref, v_ref, do_ref, nl_ref, ox_ref, ck_ref, g_ref,
             dq_ref, dk_ref, dv_ref, dc_ref, wacc_ref, fk_sc, fkt_sc, vt_sc, dfkt_sc, dvt_sc):
        @pl.when((pl.program_id(0) == 0) & (pl.program_id(1) == 0))
        def _():
            wacc_ref[...] = jnp.zeros_like(wacc_ref)

        _, m0 = _lane_masks()
        mt0 = lax.broadcasted_iota(jnp.int32, (PAIR, 1), 0) < HD
        g0 = g_ref[0:1, :]
        g1 = g_ref[1:2, :]
        fk_sc[...] = (_pair_rms(k_ref[...], m0)[0] * g1).astype(BF16)
        _transpose_blocks(fk_sc, fkt_sc, nb, tb)
        _transpose_blocks(v_ref, vt_sc, nb, tb)
        dfkt_sc[...] = jnp.zeros_like(dfkt_sc)
        dvt_sc[...] = jnp.zeros_like(dvt_sc)
        dc_ref[...] = jnp.zeros_like(dc_ref)
        causal = lax.broadcasted_iota(jnp.int32, (tb, tb), 1) <= lax.broadcasted_iota(jnp.int32, (tb, tb), 0)

        def qloop(qb, carry):
            qn, qr = _pair_rms(_rows(q_ref, qb, tb), m0)
            fqf = qn * (g0 * SCALE)
            dof = _rows(do_ref, qb, tb)
            fqh = _pair_split(fqf, m0)
            doh = _pair_split(dof, m0)
            fqth = _pair_split(fqf.T, mt0)
            doth = _pair_split(dof.T, mt0)
            nlv = _rows(nl_ref, qb, tb)
            cql = (nlv[:, 0:1], nlv[:, 1:2])

            def probs(kb, masked):
                qk = [_dot(fqh[h], fkt_sc[kb]) for h in range(2)]
                dp = [_dot(doh[h], vt_sc[kb]) for h in range(2)]
                pr = []
                for h in range(2):
                    e = jnp.exp(qk[h] + (cql[h] - ck_ref[h, kb]))
                    pr.append(jnp.where(causal, e, 0.0) if masked else e)
                return pr, dp

            oxv = _rows(ox_ref, qb, tb)
            dlt = [_rowsum(doh[h].astype(F32) * oxv) for h in range(2)]

            def accumulate(kb, pd, dfqs):
                out = []
                for h in range(2):
                    prb, dsb = pd[h]
                    dvt_sc[kb] += _dot(doth[h], prb)
                    dfkt_sc[kb] += _dot(fqth[h], dsb)
                    out.append(dfqs[h] + _dot(dsb, _rows(fk_sc, kb, tb)))
                return tuple(out)

            def block(kb, c, masked):
                dfqs, (kb_prev, pd_prev) = c
                pr, dp = probs(kb, masked)
                dfqs = accumulate(kb_prev, pd_prev, dfqs)
                pd = []
                for h in range(2):
                    ds = pr[h] * (dp[h] - dlt[h])
                    dc_ref[h, kb] += jnp.broadcast_to(-_colsum(ds), (8, tb))
                    pd.append((pr[h].astype(BF16), ds.astype(BF16)))
                return dfqs, (kb, tuple(pd))

            zq = jnp.zeros((tb, PAIR), F32)
            zb = jnp.zeros((tb, tb), BF16)
            none = (jnp.int32(0), ((zb, zb), (zb, zb)))
            c = lax.fori_loop(0, qb, lambda i, cc: block(i, cc, False), ((zq, zq), none))
            dfqs, (kb_last, pd_last) = block(qb, c, True)
            c = accumulate(kb_last, pd_last, dfqs)
            dfq = _pair_select(m0, c[0], c[1]) * SCALE
            wacc_ref[0:1, :] += _colsum(dfq * qn)
            r0 = pl.multiple_of(qb * tb, tb)
            dq_ref[pl.ds(r0, tb), :] = _pair_rms_bwd(qn, qr, dfq * g0, m0).astype(BF16)
            return carry

        lax.fori_loop(0, nb, qloop, 0)
        for kb in range(nb):
            rows = slice(kb * tb, (kb + 1) * tb)
            dfk = dfkt_sc[kb].T
            knb, krb = _pair_rms(k_ref[rows, :], m0)
            wacc_ref[1:2, :] += _colsum(dfk * knb)
            dk_ref[rows, :] = _pair_rms_bwd(knb, krb, dfk * g1, m0).astype(BF16)
            dv_ref[rows, :] = dvt_sc[kb].T.astype(BF16)

    def col(blk):
        return pl.BlockSpec((S, PAIR), lambda b, p: (b, blk + p))

    sh = jax.ShapeDtypeStruct((T, AW), BF16)
    return pl.pallas_call(
        body, name=f"fox_bwd_{l}",
        grid=(T // S, 2),
        in_specs=[col(FXQ_BLK), col(FXK_BLK), col(FXV_BLK), col(0), col(0), col(0),
                  pl.BlockSpec((None, 2, nb, 1, tb), lambda b, p: (b, p, 0, 0, 0)),
                  pl.BlockSpec((None, 8, PAIR), lambda b, p: (l, 0, 0))],
        out_specs=[col(0), col(0), col(0),
                   pl.BlockSpec((None, 2, nb, 8, tb), lambda b, p: (b, p, 0, 0, 0)),
                   pl.BlockSpec((8, PAIR), lambda b, p: (0, 0))],
        out_shape=[sh, sh, sh,
                   jax.ShapeDtypeStruct((T // S, NH, nb, 8, tb), F32),
                   jax.ShapeDtypeStruct((8, PAIR), F32)],
        scratch_shapes=[pltpu.VMEM((S, PAIR), BF16), pltpu.VMEM((nb, PAIR, tb), BF16), pltpu.VMEM((nb, PAIR, tb), BF16),
                        pltpu.VMEM((nb, PAIR, tb), F32), pltpu.VMEM((nb, PAIR, tb), F32)],
        compiler_params=_cp(("arbitrary", "arbitrary"), VMEM_BIG),
    )(proj, proj, proj, do, nl, ox, ck, gqk2)


def _ada_fwd(c_all, w_ada, b_cols):
    nb, ncol = c_all.shape[0], w_ada.shape[2]
    tn = _tile(ncol, 768)

    def body(c_ref, w_ref, b_ref, o_ref):
        c = c_ref[...]
        ca = (c * _sigmoid(c)).astype(BF16)
        o_ref[...] = _dot(ca, w_ref[...].astype(BF16)) + b_ref[...]

    return pl.pallas_call(
        body, name="ada_fwd",
        grid=(2, ncol // tn),
        in_specs=[pl.BlockSpec((nb, D), lambda l, n: (0, 0)),
                  pl.BlockSpec((None, D, tn), lambda l, n: (l, 0, n)),
                  pl.BlockSpec((None, 1, tn), lambda l, n: (l, 0, n))],
        out_specs=pl.BlockSpec((None, nb, tn), lambda l, n: (l, 0, n)),
        out_shape=jax.ShapeDtypeStruct((2, nb, ncol), F32),
        compiler_params=_cp(("arbitrary", "arbitrary")),
    )(c_all, w_ada, b_cols)


def _ada_bwd(c_all, dmod_cols):
    nb, ncol = c_all.shape[0], dmod_cols.shape[2]
    tn = _tile(ncol, 768)

    def body(c_ref, d_ref, o_ref):
        c = c_ref[...]
        ca = (c * _sigmoid(c)).astype(BF16)
        o_ref[...] = _dot_tn(ca, d_ref[...].astype(BF16))

    return pl.pallas_call(
        body, name="ada_bwd",
        grid=(2, ncol // tn),
        in_specs=[pl.BlockSpec((nb, D), lambda l, n: (0, 0)),
                  pl.BlockSpec((None, nb, tn), lambda l, n: (l, 0, n))],
        out_specs=pl.BlockSpec((None, D, tn), lambda l, n: (l, 0, n)),
        out_shape=jax.ShapeDtypeStruct((2, D, ncol), F32),
        compiler_params=_cp(("arbitrary", "arbitrary")),
    )(c_all, dmod_cols)


def _sum_lead(a, name):
    n, R, C = a.shape
    tr = _tile_div8(R, 256)

    def body(a_ref, o_ref):
        acc = a_ref[0]
        for i in range(1, n):
            acc = acc + a_ref[i]
        o_ref[...] = acc

    return pl.pallas_call(
        body, name=name,
        grid=(R // tr,),
        in_specs=[pl.BlockSpec((n, tr, C), lambda i: (0, i, 0))],
        out_specs=pl.BlockSpec((tr, C), lambda i: (i, 0)),
        out_shape=jax.ShapeDtypeStruct((R, C), F32),
        compiler_params=_cp(("arbitrary",)),
    )(a)


def _adamw(w, g, m, v, name):
    R, C = w.shape
    tr = _tile_div8(R, max(8, (1 << 18) // C))
    c1 = 1.0 / (1.0 - ADAM_B1 ** ADAM_STEP)
    c2 = 1.0 / (1.0 - ADAM_B2 ** ADAM_STEP)

    def body(w_ref, g_ref, m_ref, v_ref, d_ref, mo_ref, vo_ref):
        gg = g_ref[...]
        mn = ADAM_B1 * m_ref[...] + (1.0 - ADAM_B1) * gg
        vn = ADAM_B2 * v_ref[...] + (1.0 - ADAM_B2) * (gg * gg)
        mo_ref[...] = mn
        vo_ref[...] = vn
        d_ref[...] = (-ADAM_LR) * ((mn * c1) / (jnp.sqrt(vn * c2) + ADAM_EPS) + ADAM_WD * w_ref[...])

    spec = pl.BlockSpec((tr, C), lambda i: (i, 0))
    sh = jax.ShapeDtypeStruct((R, C), F32)
    return pl.pallas_call(
        body, name=name, grid=(R // tr,),
        in_specs=[spec] * 4, out_specs=[spec] * 3, out_shape=[sh] * 3,
        compiler_params=_cp(("arbitrary",)),
    )(w, g, m, v)


def _coords():
    return lax.axis_index("x"), lax.axis_index("y"), lax.axis_index("c")


def _all_gather8(blk, name, vmem):
    m_per, n = blk.shape
    space = pltpu.VMEM if vmem else pl.ANY

    def body(x_ref, out_ref, send_sems, recv_sems, local_sem):
        x, y, c = _coords()
        me, sibling = (x, y, c), (x, y, 1 - c)
        chips = [(1 - x, y), (x, 1 - y), (1 - x, 1 - y)]

        def rows(px, py, pc):
            return out_ref.at[4 * px + 2 * py + pc]

        def copy(k, block, to, src=None):
            return pltpu.make_async_remote_copy(
                src_ref=rows(*block) if src is None else src, dst_ref=rows(*block),
                send_sem=send_sems.at[k], recv_sem=recv_sems.at[k], device_id=to, device_id_type=MESH)

        mine = pltpu.make_async_copy(x_ref, rows(*me), local_sem)
        mine.start()
        first = [copy(0, me, sibling, src=x_ref)]
        first += [copy(1 + j, me, (*chip, c), src=x_ref) for j, chip in enumerate(chips)]
        for cp in first:
            cp.start()
        passed = [copy(4 + j, (*chip, c), sibling) for j, chip in enumerate(chips)]
        for j, chip in enumerate(chips):
            copy(1 + j, (*chip, c), me).wait_recv()
            passed[j].start()
        copy(0, sibling, me).wait_recv()
        for j, chip in enumerate(chips):
            copy(4 + j, (*chip, 1 - c), me).wait_recv()
        for cp in first + passed:
            cp.wait_send()
        mine.wait()

    return pl.pallas_call(
        body, name=name,
        out_shape=jax.ShapeDtypeStruct((N_DEV, m_per, n), blk.dtype),
        in_specs=[pl.BlockSpec(memory_space=space)],
        out_specs=pl.BlockSpec(memory_space=space),
        scratch_shapes=[pltpu.SemaphoreType.DMA((7,)), pltpu.SemaphoreType.DMA((7,)), pltpu.SemaphoreType.DMA],
        compiler_params=pltpu.CompilerParams(vmem_limit_bytes=VMEM_BIG if vmem else None),
    )(blk)


def _run_comm(comm, name):
    n_in, n_out = len(comm["args"]), len(comm["out_shapes"])

    def body(*refs):
        parts = (refs[:n_in], refs[n_in:n_in + n_out], refs[n_in + n_out:])
        comm["start"](*parts)
        comm["finish"](*parts)

    anyspec = pl.BlockSpec(memory_space=pl.ANY)
    return pl.pallas_call(
        body, name=name, out_shape=comm["out_shapes"],
        in_specs=[anyspec] * n_in, out_specs=[anyspec] * n_out, scratch_shapes=comm["scratch"],
    )(*comm["args"])


def _hosted(comm):
    if comm is None:
        return [], [], [], []
    anyspec = pl.BlockSpec(memory_space=pl.ANY)
    return list(comm["args"]), [anyspec] * len(comm["args"]), list(comm["out_shapes"]), list(comm["scratch"])


def _ag_comm(wshards, pieces):
    n_piece = len(pieces)
    n_src = len(wshards)
    halves = [wshards[i].shape[len(lead)] // 2 for i, lead, _ in pieces]

    def plan(ins, outs, sems):
        send_sems, recv_sems, local_sems = sems
        x, y, c = _coords()
        me, sibling = (x, y, c), (x, y, 1 - c)
        chips = [(1 - x, y), (x, 1 - y), (1 - x, 1 - y)]

        def dsts(px, py, pc):
            s = 2 * px + py
            return [outs[p].at[s, pl.ds(pc * r2, r2)] if stacked else outs[p].at[pl.ds((2 * s + pc) * r2, r2)]
                    for p, ((_, _, stacked), r2) in enumerate(zip(pieces, halves))]

        srcs = [ins[i].at[(*lead, pl.ds(c * r2, r2))] for (i, lead, _), r2 in zip(pieces, halves)]

        def copies(k, block, to, own=False):
            d = dsts(*block)
            return [pltpu.make_async_remote_copy(
                src_ref=srcs[p] if own else d[p], dst_ref=d[p], send_sem=send_sems.at[k, p],
                recv_sem=recv_sems.at[k, p], device_id=to, device_id_type=MESH) for p in range(n_piece)]

        mine = [pltpu.make_async_copy(srcs[p], d, local_sems.at[p]) for p, d in enumerate(dsts(*me))]
        first = copies(0, me, sibling, own=True)
        for j, chip in enumerate(chips):
            first += copies(1 + j, me, (*chip, c), own=True)
        return me, sibling, chips, c, copies, mine, first

    def start(ins, outs, sems):
        *_, mine, first = plan(ins, outs, sems)
        for cp in mine + first:
            cp.start()

    def finish(ins, outs, sems):
        me, sibling, chips, c, copies, mine, first = plan(ins, outs, sems)
        passed = []
        for j, chip in enumerate(chips):
            for cp in copies(1 + j, (*chip, c), me):
                cp.wait_recv()
            fwd = copies(4 + j, (*chip, c), sibling)
            for cp in fwd:
                cp.start()
            passed += fwd
        for cp in copies(0, sibling, me):
            cp.wait_recv()
        for j, chip in enumerate(chips):
            for cp in copies(4 + j, (*chip, 1 - c), me):
                cp.wait_recv()
        for cp in first + passed:
            cp.wait_send()
        for cp in mine:
            cp.wait()

    out_shapes = []
    for (i, lead, stacked), r2 in zip(pieces, halves):
        cols = wshards[i].shape[-1]
        out_shapes.append(jax.ShapeDtypeStruct((N_SHARD, 2 * r2, cols) if stacked else (N_SHARD * 2 * r2, cols), BF16))
    assert n_src == 4
    return dict(
        args=list(wshards), out_shapes=out_shapes,
        scratch=[pltpu.SemaphoreType.DMA((7, n_piece)), pltpu.SemaphoreType.DMA((7, n_piece)),
                 pltpu.SemaphoreType.DMA((n_piece,))],
        start=start, finish=finish)


def _rs_to_chips_comm(hs):
    n = len(hs)

    def copies(h, r, sems):
        send_sems, recv_sems = sems
        x, y, c = _coords()
        chips = [(1 - x, y), (x, 1 - y), (1 - x, 1 - y)]
        return [pltpu.make_async_remote_copy(
            src_ref=h[p].at[2 * px + py], dst_ref=r[p].at[k], send_sem=send_sems.at[k, p], recv_sem=recv_sems.at[k, p],
            device_id=(px, py, c), device_id_type=MESH) for k, (px, py) in enumerate(chips) for p in range(n)]

    def start(h, r, sems):
        for cp in copies(h, r, sems):
            cp.start()

    def finish(h, r, sems):
        for cp in copies(h, r, sems):
            cp.wait()

    return dict(args=list(hs), out_shapes=[jax.ShapeDtypeStruct((3,) + h.shape[1:], h.dtype) for h in hs],
                scratch=[pltpu.SemaphoreType.DMA((3, n)), pltpu.SemaphoreType.DMA((3, n))],
                start=start, finish=finish)


def _rs_to_sibling(pieces, name):
    n = len(pieces)

    def body(*refs):
        g, r, (send_sems, recv_sems) = refs[:n], refs[n:2 * n], refs[2 * n:]
        x, y, c = _coords()
        cps = []
        for p in range(n):
            r2 = g[p].shape[1] // 2
            cps.append(pltpu.make_async_remote_copy(
                src_ref=g[p].at[:, pl.ds((1 - c) * r2, r2)], dst_ref=r[p], send_sem=send_sems.at[p],
                recv_sem=recv_sems.at[p], device_id=(x, y, 1 - c), device_id_type=MESH))
        for cp in cps:
            cp.start()
        for cp in cps:
            cp.wait()

    anyspec = pl.BlockSpec(memory_space=pl.ANY)
    return pl.pallas_call(
        body, name=name,
        out_shape=[jax.ShapeDtypeStruct((N_SHARD, g.shape[1] // 2, g.shape[2]), g.dtype) for g in pieces],
        in_specs=[anyspec] * n, out_specs=[anyspec] * n,
        scratch_shapes=[pltpu.SemaphoreType.DMA((n,)), pltpu.SemaphoreType.DMA((n,))],
    )(*pieces)


def _share_halves(tensors, places, r2s):
    n, no = len(places), len(tensors)

    def body(*refs):
        o, (send_sems, recv_sems) = refs[no:2 * no], refs[2 * no:]
        x, y, c = _coords()

        def half(p, hc):
            oi, lead = places[p]
            return o[oi].at[(*lead, pl.ds(hc * r2s[p], r2s[p]))]

        outs = [pltpu.make_async_remote_copy(
            src_ref=half(p, c), dst_ref=half(p, c), send_sem=send_sems.at[p], recv_sem=recv_sems.at[p],
            device_id=(x, y, 1 - c), device_id_type=MESH) for p in range(n)]
        for cp in outs:
            cp.start()
        for p in range(n):
            pltpu.make_async_remote_copy(
                src_ref=half(p, 1 - c), dst_ref=half(p, 1 - c), send_sem=send_sems.at[p], recv_sem=recv_sems.at[p],
                device_id=(x, y, 1 - c), device_id_type=MESH).wait_recv()
        for cp in outs:
            cp.wait_send()

    anyspec = pl.BlockSpec(memory_space=pl.ANY)
    return pl.pallas_call(
        body, name="share_halves",
        out_shape=[jax.ShapeDtypeStruct(t.shape, t.dtype) for t in tensors],
        in_specs=[anyspec] * no, out_specs=[anyspec] * no,
        input_output_aliases={i: i for i in range(no)},
        scratch_shapes=[pltpu.SemaphoreType.DMA((n,)), pltpu.SemaphoreType.DMA((n,))],
    )(*tensors)


def _add_rows(r2, cols, n_arrays):
    lanes = -(-cols // 128) * 128
    return _tile_div8(r2, max(16, (24 << 20) // (2 * n_arrays * lanes * 4)), mult=16)


def _add_sibling(pieces, recvs, cidx, name):
    n = len(pieces)
    _, R, C = pieces[0].shape
    r2 = R // 2
    tr = _add_rows(r2, C, 2 * n)
    nt = r2 // tr

    def body(c_ref, *refs):
        for p in range(n):
            refs[2 * n + p][...] = (refs[p][...] + refs[n + p][...].astype(F32)).astype(BF16)

    return pl.pallas_call(
        body, name=name,
        grid_spec=pltpu.PrefetchScalarGridSpec(
            num_scalar_prefetch=1, grid=(N_SHARD, nt),
            in_specs=[pl.BlockSpec((None, tr, C), lambda s, i, c_ref: (s, c_ref[0] * nt + i, 0))] * n
            + [pl.BlockSpec((None, tr, C), lambda s, i, c_ref: (s, i, 0))] * n,
            out_specs=[pl.BlockSpec((None, tr, C), lambda s, i, c_ref: (s, i, 0))] * n),
        out_shape=[jax.ShapeDtypeStruct((N_SHARD, r2, C), BF16)] * n,
        compiler_params=_cp(("arbitrary", "arbitrary"), VMEM_BIG),
    )(cidx, *pieces, *recvs)


def _add_chips_into(piece, recv_a, recv_b, sc, prev, shape, lead, name):
    _, R, C = piece.shape
    r2 = R // 2
    tr = _add_rows(r2, C, 4)
    nt = r2 // tr
    nl = len(lead)

    def body(sc_ref, p_ref, a_ref, b_ref, *rest):
        o_ref = rest[-1]
        acc = p_ref[...] + a_ref[...].astype(F32)
        for k in range(3):
            acc = acc + b_ref[k].astype(F32)
        o_ref[...] = acc

    in_specs = [pl.BlockSpec((None, tr, C), lambda i, sc_ref: (sc_ref[0], sc_ref[1] * nt + i, 0)),
                pl.BlockSpec((None, tr, C), lambda i, sc_ref: (sc_ref[0], i, 0)),
                pl.BlockSpec((3, tr, C), lambda i, sc_ref: (0, i, 0))]
    args = [sc, piece, recv_a, recv_b]
    aliases = {}
    if prev is not None:
        in_specs.append(pl.BlockSpec(memory_space=pl.ANY))
        args.append(prev)
        aliases = {4: 0}
    return pl.pallas_call(
        body, name=name,
        grid_spec=pltpu.PrefetchScalarGridSpec(
            num_scalar_prefetch=1, grid=(nt,), in_specs=in_specs,
            out_specs=pl.BlockSpec((None,) * nl + (tr, C), lambda i, sc_ref: (*lead, sc_ref[1] * nt + i, 0))),
        out_shape=jax.ShapeDtypeStruct(shape, F32),
        input_output_aliases=aliases,
        compiler_params=_cp(("arbitrary",), VMEM_BIG),
    )(*args)


def _pack_rows(parts, rows, dtype):
    flat = jnp.concatenate([p.reshape(-1).astype(dtype) for p in parts])
    return jnp.pad(flat, (0, rows * ROW - flat.shape[0])).reshape(rows, ROW)


def _unpack(flat, shapes):
    out, off = [], 0
    for sh in shapes:
        n = math.prod(sh)
        out.append(flat[off:off + n].reshape(sh))
        off += n
    return out


def _heads(t, B, S, blk):
    return t.reshape(B, S, NH, HD).transpose(0, 2, 1, 3).reshape(B, NH, S // blk, blk, HD)


def _unheads(t, B, S):
    return t.reshape(B, NH, S, HD).transpose(0, 2, 1, 3).reshape(B * S, AW)


def _block_diag(w):
    eye = jnp.eye(LW // HD, dtype=w.dtype)
    return jnp.einsum("lhij,hg->lhigj", w, eye).reshape(w.shape[0], LW, LW)


def _diag_blocks(w):
    nbk = LW // HD
    w4 = w.reshape(nbk, HD, nbk, HD)
    return jnp.stack([w4[h, :, h, :] for h in range(nbk)])


def _rows8(rows, width):
    z = jnp.zeros((width,), F32)
    return jnp.stack(list(rows) + [z] * (8 - len(rows)))


def kernel(x, c, w_ada, b_ada, g_norm, w_ffn_up, w_ffn_down, w_in, b_fgate, conv_w, conv_b, w_rgate, b_rgate, w_igate, b_igate, lru_lambda, g_qk, g_mix_out, w_out, loss_target, m_w_ada, m_b_ada, m_g_norm, m_w_ffn_up, m_w_ffn_down, m_w_in, m_b_fgate, m_conv_w, m_conv_b, m_w_rgate, m_b_rgate, m_w_igate, m_b_igate, m_lru_lambda, m_g_qk, m_g_mix_out, m_w_out, v_w_ada, v_b_ada, v_g_norm, v_w_ffn_up, v_w_ffn_down, v_w_in, v_b_fgate, v_conv_w, v_conv_b, v_w_rgate, v_b_rgate, v_w_igate, v_b_igate, v_lru_lambda, v_g_qk, v_g_mix_out, v_w_out):
    B, S, _ = x.shape
    T = B * S
    xi, yi, ci = _coords()
    sidx = 2 * xi + yi
    didx = 4 * xi + 2 * yi + ci
    ada_cols = w_ada.shape[2]
    gn_cols = g_norm.shape[2]
    cw_cols = conv_w.shape[2]
    n_all = B * N_DEV

    blk1 = _pack_rows([c, jnp.pad(g_norm.reshape(-1), (0, 2 * ROW - g_norm.size)), conv_w], 8, F32)
    ag1 = _all_gather8(blk1, "ag_small_in", True)
    c_all = ag1[:, 0:B].reshape(n_all, D)
    chip_rows = ag1[0::2]
    g_norm_full = chip_rows[:, 2:4].reshape(N_SHARD, 2 * ROW)[:, :g_norm.size] \
        .reshape(N_SHARD, 2, 3, gn_cols).transpose(1, 2, 0, 3).reshape(2, 3, D)
    conv_w_full = chip_rows[:, 4].reshape(N_SHARD, 2, 4, cw_cols).transpose(1, 2, 0, 3).reshape(2, 4, LW)

    b_cols = lax.dynamic_slice(b_ada, (0, sidx * ada_cols), (2, ada_cols)).reshape(2, 1, ada_cols)
    mod_cols = _ada_fwd(c_all, w_ada, b_cols)
    mrows = (2 * n_all * ada_cols) // ROW
    ag2 = _all_gather8(mod_cols.reshape(mrows, ROW), "ag_mod", True)
    mod_sh = ag2[0::2].reshape(N_SHARD, 2, n_all, ada_cols)
    mod_me = lax.dynamic_slice(mod_sh, (0, 0, didx * B, 0), (N_SHARD, 2, B, ada_cols))
    mod_me = mod_me.transpose(1, 2, 0, 3).reshape(2, B, 3, 3, D)
    zrow = jnp.zeros((B, D), F32)
    mods = [[jnp.stack([mod_me[l, :, j, 0], 1.0 + mod_me[l, :, j, 1], 1.0 + mod_me[l, :, j, 2],
                        jnp.broadcast_to(g_norm_full[l, j], (B, D)), zrow, zrow, zrow, zrow], axis=1)
             for j in range(3)] for l in range(2)]

    wshards = (w_ffn_up.astype(BF16), w_ffn_down.astype(BF16), w_in.astype(BF16), w_out.astype(BF16))

    def ffn_pieces(l, j):
        return [(0, (l, j), True), (1, (l, j), False)]

    def mixer_pieces(l):
        return [(2, (l,), True), (3, (l,), False)]

    def ffn_weights(up, dn):
        return dict(up=up, dn=dn)

    def mixer_weights(g_in, g_out):
        return dict(inp=jnp.pad(g_in.transpose(1, 0, 2).reshape(D, N_IN), ((0, 0), (0, N_INP - N_IN))), out=g_out)

    wl = [dict(), dict()]
    wl[0][0] = ffn_weights(*_run_comm(_ag_comm(wshards, ffn_pieces(0, 0)), "ag_weights_0_0"))

    wr_d = _block_diag(w_rgate).astype(BF16)
    wi_d = _block_diag(w_igate).astype(BF16)
    cw8 = jnp.pad(conv_w_full, ((0, 0), (0, 4), (0, 0)))
    vp8 = jnp.stack([_rows8([conv_b[l], b_rgate[l], b_igate[l], lru_lambda[l]], LW) for l in range(2)])
    bfp = jnp.pad(b_fgate, ((0, 0), (0, 128 - NH)))[:, None, :] * jnp.ones((1, 8, 1), F32)
    gqk2 = jnp.tile(jnp.pad(g_qk, ((0, 0), (0, 6), (0, 0))), (1, 1, 2))
    gmix8 = jnp.pad(g_mix_out[:, None, :], ((0, 0), (0, 7), (0, 0)))

    x2 = x.reshape(T, D)
    tgt = loss_target.reshape(T, D)

    saved = []
    xc = x2
    for l in range(2):
        sv = {}
        sv["x0"] = xc
        w = wl[l]
        rest0 = _ag_comm(wshards, mixer_pieces(0) + ffn_pieces(0, 1)) if l == 0 else None
        xc, sv["g0"], sv["u0"], sv["f0"], got = _ffn_fwd(xc, mods[l][0], w[0]["up"], w[0]["dn"], l, 0, S, rest0)
        if l == 0:
            w["mix"] = mixer_weights(got[0], got[1])
            w[1] = ffn_weights(got[2], got[3])
        sv["x1"] = xc
        sv["h1"], proj = _mix_in_fwd(xc, mods[l][1], w["mix"]["inp"], l, S)
        sv["proj"] = proj
        sv["ylru"], sv["hl"] = _lru_fwd(proj, cw8, vp8, wr_d, wi_d, l, S)
        all1 = _ag_comm(wshards, ffn_pieces(1, 0) + mixer_pieces(1) + ffn_pieces(1, 1)) if l == 0 else None
        sv["osb"], sv["t1"], got = _sbq_fwd(proj, l, S, all1)
        if l == 0:
            wl[1][0] = ffn_weights(got[0], got[1])
            wl[1]["mix"] = mixer_weights(got[2], got[3])
            wl[1][1] = ffn_weights(got[4], got[5])
        cum = _fgate_fwd(proj, bfp, l, S)
        sv["ck"] = cum[:, :NH].reshape(B, S, NH).transpose(0, 2, 1).reshape(B, NH, S // TK_(S), 1, TK_(S))
        sv["ofx"], sv["nl"], sv["ox"] = _foxq_fwd(proj, cum, sv["ck"], gqk2, l, S)
        xc, sv["y"], sv["mo"] = _mix_out_fwd(xc, sv["ylru"], sv["osb"], sv["ofx"], mods[l][1], gmix8, w["mix"]["out"], l, S)
        sv["x2"] = xc
        xc, sv["g2"], sv["u2"], sv["f2"], _ = _ffn_fwd(xc, mods[l][2], w[1]["up"], w[1]["dn"], l, 1, S)
        saved.append(sv)

    dxc, lpart = _loss_head(xc, tgt, S)
    loss = lax.psum(lpart[0, 0], ("x", "y", "c"))

    tf = wl[0][0]["up"].shape[-1]
    g_up_l = [[None, None], [None, None]]
    g_dn_l = [[None, None], [None, None]]
    g_in_l, g_out_l = [None, None], [None, None]
    dmods = [[None] * 3 for _ in range(2)]
    small = [dict() for _ in range(2)]
    cvec = jnp.reshape(ci, (1,)).astype(jnp.int32)
    scvec = jnp.stack([sidx, ci]).astype(jnp.int32)

    def ffn_groups(l, j):
        return [(0, "up", [g_up_l[l][j]], [(l, j)]), (1, "dn", [g_dn_l[l][j]], [(l, j)])]

    def mixer_groups(l):
        return [(2, "in", [g_in_l[l]], [(l,)]), (3, "out", [g_out_l[l]], [(l,)])]

    def rs_sibling_phase(groups, tag):
        recv_a = _rs_to_sibling([pb for _, _, ps, _ in groups for _, pb in ps], f"rs_to_sibling_{tag}")
        hs, off = [], 0
        for _, gname, ps, leads in groups:
            hs += _add_sibling([pf for pf, _ in ps], recv_a[off:off + len(ps)], cvec,
                               f"rs_add_sibling_{gname}_{'_'.join(map(str, leads[0]))}")
            off += len(ps)
        return groups, recv_a, hs

    def ffn_back(l, j, xin, dy, sv, sub, comm=None):
        dx, dmod, wacc, hb, dfb, ab, dgub, got = _ffn_bwd(
            xin, dy, mods[l][sub], sv[f"f{sub}"], sv[f"g{sub}"], sv[f"u{sub}"],
            wl[l][j]["up"], wl[l][j]["dn"], l, j, S, comm)
        g_up_l[l][j] = _mm_tn(hb, dgub, f"dw_up_{l}_{j}", tnb=tf, split_n=True, with_bf16=True)
        g_dn_l[l][j] = tuple(g.reshape(N_SHARD, -1, D)
                             for g in _mm_tn(ab, dfb, f"dw_dn_{l}_{j}", tma=tf, with_bf16=True))
        dmods[l][sub] = dmod
        small[l][f"gn{sub}"] = wacc[0]
        return dx, got

    batches = []
    for l in (1, 0):
        sv = saved[l]
        dxc, _ = ffn_back(l, 1, sv["x2"], dxc, sv, 2)
        dyl, dsb, dfx, dmo, dmod1, wacc_mo = _mix_out_bwd(
            dxc, sv["ylru"], sv["osb"], sv["ofx"], sv["mo"], mods[l][1], gmix8, wl[l]["mix"]["out"], l, S)
        small[l]["gmix"] = wacc_mo[0]
        g_out_l[l] = tuple(g.reshape(N_SHARD, -1, D) for g in _mm_tn(sv["y"], dmo, f"dw_out_{l}", with_bf16=True))
        dsq, dsk, dsv, got = _sbq_bwd(sv["proj"], dsb, sv["t1"], l, S,
                                       _rs_to_chips_comm(rs1[2]) if l == 0 else None)
        if l == 0:
            batches.append((rs1[0], rs1[1], got))
        dfq, dfk, dfv, dck, wacc_fx = _foxq_bwd(sv["proj"], dfx, sv["nl"], sv["ox"], sv["ck"], gqk2, l, S)
        small[l]["gqk"] = wacc_fx[0:2, :HD] + wacc_fx[0:2, HD:]
        dcum = dck[:, :, :, 0, :].reshape(B, NH, S).transpose(0, 2, 1).reshape(T, NH)
        dff_, wacc_fg = _fgate_bwd(jnp.pad(dcum, ((0, 0), (0, 128 - NH))), sv["proj"], bfp, l, S)
        small[l]["bf"] = wacc_fg[0, :NH]
        dlx, dlg, dpr, dpi, ub, wacc_lru = _lru_bwd(dyl, sv["proj"], sv["hl"], cw8, vp8, wr_d, wi_d, l, S)
        small[l]["lru"] = wacc_lru
        small[l]["wr"] = _diag_blocks(_mm_tn(ub, dpr, f"dw_rgate_{l}"))
        small[l]["wi"] = _diag_blocks(_mm_tn(ub, dpi, f"dw_igate_{l}"))
        dproj = jnp.concatenate(
            [dlx, dlg, dsq, dsk, dsv, dfq, dfk, dfv, dff_], axis=1)
        g_in = _mm_tn(sv["h1"], dproj, f"dw_in_{l}", tnb=N_INP // 3)[:, :N_IN]
        g_in = g_in.reshape(D, N_SHARD, -1).transpose(1, 0, 2)
        g_in_l[l] = (g_in, g_in.astype(BF16))
        dxc, dmod_in, wacc_in = _mix_in_bwd(sv["x1"], dxc, mods[l][1], dproj, wl[l]["mix"]["inp"], l, S)
        dmods[l][1] = dmod_in + dmod1
        small[l]["gn1"] = wacc_in[0]
        if l == 1:
            dxc, _ = ffn_back(l, 0, sv["x0"], dxc, sv, 0)
            rs1 = rs_sibling_phase(ffn_groups(1, 0) + mixer_groups(1) + ffn_groups(1, 1), "1")
        else:
            late = rs_sibling_phase(mixer_groups(0) + ffn_groups(0, 1), "0_late")
            dxc, got = ffn_back(l, 0, sv["x0"], dxc, sv, 0, _rs_to_chips_comm(late[2]))
            batches.append((late[0], late[1], got))
    grad_x = dxc.reshape(B, S, D)

    dmod_loc = jnp.stack([jnp.stack([dmods[l][j][:, 0:3, :] for j in range(3)], axis=1) for l in range(2)])
    drows = 2 * B * 9
    blk3 = _pack_rows([dmod_loc], -(-drows // 8) * 8, F32)
    ag3 = _all_gather8(blk3, "ag_dmod", True)
    dmod_all = ag3[:, :drows].reshape(N_DEV, 2, B, 9 * D).transpose(1, 0, 2, 3).reshape(2, n_all, 9 * D)
    dmod_mine = lax.dynamic_slice(dmod_all, (0, 0, sidx * ada_cols), (2, n_all, ada_cols))
    grad_w_ada = _ada_bwd(c_all, dmod_mine)
    dmod_rows = jnp.pad(dmod_all.transpose(1, 0, 2).reshape(n_all, 2 * 9, D), ((0, 0), (0, 6), (0, 0)))
    grad_b_ada = _sum_lead(dmod_rows, "grad_b_ada")[:2 * 9].reshape(2, 9 * D)

    sm_parts = [
        jnp.stack([small[l]["bf"] for l in range(2)]),
        jnp.stack([small[l]["lru"][4] for l in range(2)]),
        jnp.stack([small[l]["wr"] for l in range(2)]),
        jnp.stack([small[l]["lru"][5] for l in range(2)]),
        jnp.stack([small[l]["wi"] for l in range(2)]),
        jnp.stack([small[l]["lru"][6] for l in range(2)]),
        jnp.stack([small[l]["lru"][7] for l in range(2)]),
        jnp.stack([small[l]["gqk"] for l in range(2)]),
        jnp.stack([small[l]["gmix"] for l in range(2)]),
        jnp.stack([jnp.stack([small[l][f"gn{j}"] for j in range(3)]) for l in range(2)]),
        jnp.stack([small[l]["lru"][0:4] for l in range(2)]),
    ]
    sm_shapes = [p.shape for p in sm_parts]
    sm_rows = -(-sum(p.size for p in sm_parts) // (8 * ROW)) * 8
    ag4 = _all_gather8(_pack_rows(sm_parts, sm_rows, F32), "ag_small_grads", True)
    sm_sum = _sum_lead(ag4, "sum_small_grads").reshape(-1)
    (g_bf, g_cb, g_wr, g_br, g_wi, g_bi, g_lam, g_gqk, g_gmix, g_gn_full, g_cw_full) = _unpack(sm_sum, sm_shapes)
    g_gn = lax.dynamic_slice(g_gn_full, (0, 0, sidx * gn_cols), (2, 3, gn_cols))
    g_cw = lax.dynamic_slice(g_cw_full, (0, 0, sidx * cw_cols), (2, 4, cw_cols))

    last = rs_sibling_phase(ffn_groups(0, 0), "0_first")
    batches.append((last[0], last[1], _run_comm(_rs_to_chips_comm(last[2]), "rs_to_chips_0_first")))
    shapes4 = [w_ffn_up.shape, w_ffn_down.shape, w_in.shape, w_out.shape]
    tensors, places, r2s = [None] * 4, [], []
    for groups, recv_a, recv_b in batches:
        k = 0
        for gi, gname, ps, leads in groups:
            for (pf, _), lead in zip(ps, leads):
                tensors[gi] = _add_chips_into(pf, recv_a[k], recv_b[k], scvec, tensors[gi], shapes4[gi], lead,
                                              f"rs_add_chips_{gname}_{'_'.join(map(str, lead))}")
                places.append((gi, lead))
                r2s.append(pf.shape[1] // 2)
                k += 1
    gw_up, gw_dn, gw_in, gw_out = _share_halves(tensors, places, r2s)

    def upd(w, g, m, v, name):
        sh = w.shape
        two = (w.size // sh[-1], sh[-1])
        dlt, mn, vn = _adamw(w.reshape(two), g.reshape(two), m.reshape(two), v.reshape(two), name)
        return dlt.reshape(sh), mn.reshape(sh), vn.reshape(sh)

    big = {
        "w_ada": (w_ada, grad_w_ada, m_w_ada, v_w_ada),
        "w_ffn_up": (w_ffn_up, gw_up, m_w_ffn_up, v_w_ffn_up),
        "w_ffn_down": (w_ffn_down, gw_dn, m_w_ffn_down, v_w_ffn_down),
        "w_in": (w_in, gw_in, m_w_in, v_w_in),
        "w_out": (w_out, gw_out, m_w_out, v_w_out),
    }
    res = {n: (t[1],) + upd(*t, f"adamw_{n}") for n, t in big.items()}

    smalls = {
        "b_ada": (b_ada, grad_b_ada, m_b_ada, v_b_ada),
        "g_norm": (g_norm, g_gn, m_g_norm, v_g_norm),
        "b_fgate": (b_fgate, g_bf, m_b_fgate, v_b_fgate),
        "conv_w": (conv_w, g_cw, m_conv_w, v_conv_w),
        "conv_b": (conv_b, g_cb, m_conv_b, v_conv_b),
        "w_rgate": (w_rgate, g_wr, m_w_rgate, v_w_rgate),
        "b_rgate": (b_rgate, g_br, m_b_rgate, v_b_rgate),
        "w_igate": (w_igate, g_wi, m_w_igate, v_w_igate),
        "b_igate": (b_igate, g_bi, m_b_igate, v_b_igate),
        "lru_lambda": (lru_lambda, g_lam, m_lru_lambda, v_lru_lambda),
        "g_qk": (g_qk, g_gqk, m_g_qk, v_g_qk),
        "g_mix_out": (g_mix_out, g_gmix, m_g_mix_out, v_g_mix_out),
    }
    names = list(smalls)
    shapes = [smalls[n][0].shape for n in names]
    prow = -(-sum(math.prod(s) for s in shapes) // (8 * ROW)) * 8
    packed = [_pack_rows([smalls[n][i].reshape(shapes[k]) for k, n in enumerate(names)], prow, F32) for i in range(4)]
    outs = _adamw(packed[0], packed[1], packed[2], packed[3], "adamw_small")
    un = [_unpack(o.reshape(-1), shapes) for o in outs]
    for k, n in enumerate(names):
        res[n] = (smalls[n][1].reshape(shapes[k]), un[0][k], un[1][k], un[2][k])

    order = ["w_ada", "b_ada", "g_norm", "w_ffn_up", "w_ffn_down", "w_in", "b_fgate", "conv_w", "conv_b",
             "w_rgate", "b_rgate", "w_igate", "b_igate", "lru_lambda", "g_qk", "g_mix_out", "w_out"]
    return (loss, grad_x, *[res[n][0] for n in order], *[res[n][1] for n in order],
            *[res[n][2] for n in order], *[res[n][3] for n in order])


def TQ_(S):
    return min(TQ, S)


def TK_(S):
    return min(TK, S)


def _unpack_shards(wg, shapes):
    out, off = [], 0
    for sh in shapes:
        n = math.prod(sh)
        out.append(wg[:, off:off + n].reshape((N_SHARD,) + tuple(sh)))
        off += n
    return out
```

```python
import math

import jax
import jax.numpy as jnp
from jax import lax
from jax.experimental import pallas as pl
from jax.experimental.pallas import tpu as pltpu

F32 = jnp.float32
BF16 = jnp.bfloat16
MESH = pl.DeviceIdType.MESH

D = 1024
HD = 64
LW = 512
NH = 4
AW = NH * HD
N_IN = 2564
N_INP = 2688
F_BLK = 2560 // 128
EPS = 1e-6
LRU_C = 8.0
SCALE = HD ** -0.5
NEG = -1e30
TQ = 256

ADAM_LR, ADAM_B1, ADAM_B2, ADAM_EPS, ADAM_WD, ADAM_STEP = 0.001, 0.9, 0.999, 1e-08, 0.01, 10

VMEM_BIG = 56 * 1024 * 1024
N_DEV = 8
N_SHARD = 4
ROW = 1024


def _cp(sem, vmem=None):
    return pltpu.CompilerParams(dimension_semantics=sem, vmem_limit_bytes=vmem)


def _dot(a, b):
    return jnp.dot(a, b, preferred_element_type=F32)


def _dot_nt(a, b):
    return lax.dot_general(a, b, (((1,), (1,)), ((), ())), preferred_element_type=F32)


def _dot_tn(a, b):
    return lax.dot_general(a, b, (((0,), (0,)), ((), ())), preferred_element_type=F32)


def _log1p(e):
    small = e * (1.0 - e * (0.5 - e * (1.0 / 3.0 - e * 0.25)))
    return jnp.where(e < 0.01, small, jnp.log(1.0 + e))


def _expm1_neg(x):
    small = x * (1.0 + x * 0.5 * (1.0 + x * (1.0 / 3.0) * (1.0 + x * 0.25 * (1.0 + x * 0.2))))
    return jnp.where(x > -0.05, small, jnp.exp(x) - 1.0)


def _sigmoid(x):
    return 1.0 / (1.0 + jnp.exp(-x))


_GELU_C = math.sqrt(2.0 / math.pi)


def _gelu_and_grad(x):
    x2 = x * x
    th = jnp.tanh(_GELU_C * (x + 0.044715 * x * x2))
    g = 0.5 * x * (1.0 + th)
    dg = 0.5 * (1.0 + th) + 0.5 * x * (1.0 - th * th) * _GELU_C * (1.0 + 3.0 * 0.044715 * x2)
    return g, dg


def _rms_rows(x):
    rstd = lax.rsqrt(jnp.mean(x * x, axis=-1, keepdims=True) + EPS)
    return x * rstd, rstd


def _rms_bwd(xn, rstd, dyn):
    return rstd * (dyn - xn * jnp.mean(dyn * xn, axis=-1, keepdims=True))


def _colsum(x):
    return jnp.sum(x, axis=0, keepdims=True)


def _rowsum(x):
    return jnp.sum(x, axis=1, keepdims=True)


def _split3(x):
    hi = x.astype(BF16)
    r = x - hi.astype(F32)
    mid = r.astype(BF16)
    lo = (r - mid.astype(F32)).astype(BF16)
    return hi, mid, lo


def _cumsum_mm(x, ones_tri, parts=3):
    ps = _split3(x)[:parts]
    acc = _dot(ps[0], ones_tri)
    for p in ps[1:]:
        acc = acc + _dot(p, ones_tri)
    return acc


def _tri(n, kind):
    r = lax.broadcasted_iota(jnp.int32, (n, n), 0)
    c = lax.broadcasted_iota(jnp.int32, (n, n), 1)
    m = {"row_gt_col": r > c, "row_le_col": r <= c, "row_lt_col": r < c}[kind]
    return jnp.where(m, 1.0, 0.0).astype(BF16)


def _normmod(x, mod_ref):
    xn, rstd = _rms_rows(x)
    h = xn * mod_ref[3:4, :] * mod_ref[1:2, :] + mod_ref[0:1, :]
    return h, xn, rstd


def _normmod_bwd(dh, xn, rstd, mod_ref, dmod_ref, wacc_ref):
    gn = mod_ref[3:4, :]
    sc = mod_ref[1:2, :]
    dmod_ref[0:1, :] += _colsum(dh)
    t = _colsum(dh * xn)
    dmod_ref[1:2, :] += t * gn
    wacc_ref[0:1, :] += t * sc
    return _rms_bwd(xn, rstd, dh * (gn * sc))


def _tile(n, want):
    t = min(n, want)
    while n % t:
        t //= 2
    return t


def _tile_div8(n, cap, mult=8):
    best = mult
    for t in range(mult, min(n, cap) + 1, mult):
        if n % t == 0:
            best = t
    assert n % best == 0
    return best


def _ffn_fwd(x, mod, wup, wdn, l, j, S, comm=None):
    T = x.shape[0]
    tf = wup.shape[-1]
    nk = 2
    tm = _tile(S, 512)
    tpb = S // tm
    nt = T // tm
    c_args, c_specs, c_outs, c_scr = _hosted(comm)
    n_ci, n_co = len(c_args), len(c_outs)

    def body(*refs):
        x_ref, mod_ref, wg_ref, wu_ref, wd_ref = refs[:5]
        c_in = refs[5:5 + n_ci]
        xo_ref, g_ref, u_ref, f_ref = refs[5 + n_ci:9 + n_ci]
        c_out = refs[9 + n_ci:9 + n_ci + n_co]
        h_sc, acc_sc = refs[9 + n_ci + n_co:11 + n_ci + n_co]
        c_sems = refs[11 + n_ci + n_co:]
        i = pl.program_id(0)
        k = pl.program_id(1)
        if comm is not None:
            @pl.when((i == 0) & (k == 0))
            def _():
                comm["start"](c_in, c_out, c_sems)

        @pl.when(k == 0)
        def _():
            h, _, _ = _normmod(x_ref[...], mod_ref)
            h_sc[...] = h.astype(BF16)
            acc_sc[...] = jnp.zeros_like(acc_sc)

        h = h_sc[...]
        g = _dot(h, wg_ref[...])
        u = _dot(h, wu_ref[...])
        g_ref[...] = g.astype(BF16)
        u_ref[...] = u.astype(BF16)
        a = (g * _sigmoid(g)) * u
        acc_sc[...] += _dot(a.astype(BF16), wd_ref[...])

        @pl.when(k == nk - 1)
        def _():
            f = acc_sc[...]
            f_ref[...] = f.astype(BF16)
            xo_ref[...] = x_ref[...] + (0.5 * mod_ref[2:3, :]) * f

        if comm is not None:
            @pl.when((i == nt - 1) & (k == nk - 1))
            def _():
                comm["finish"](c_in, c_out, c_sems)

    anyspec = pl.BlockSpec(memory_space=pl.ANY)
    out = pl.pallas_call(
        body, name=f"ffn_fwd_{l}_{j}",
        grid=(nt, nk),
        in_specs=[
            pl.BlockSpec((tm, D), lambda i, k: (i, 0)),
            pl.BlockSpec((None, 8, D), lambda i, k: (i // tpb, 0, 0)),
            pl.BlockSpec((None, D, tf), lambda i, k: (k, 0, 0)),
            pl.BlockSpec((None, D, tf), lambda i, k: (nk + k, 0, 0)),
            pl.BlockSpec((tf, D), lambda i, k: (k, 0)),
        ] + c_specs,
        out_specs=[
            pl.BlockSpec((tm, D), lambda i, k: (i, 0)),
            pl.BlockSpec((tm, tf), lambda i, k: (i, k)),
            pl.BlockSpec((tm, tf), lambda i, k: (i, k)),
            pl.BlockSpec((tm, D), lambda i, k: (i, 0)),
        ] + [anyspec] * n_co,
        out_shape=[
            jax.ShapeDtypeStruct((T, D), F32),
            jax.ShapeDtypeStruct((T, nk * tf), BF16),
            jax.ShapeDtypeStruct((T, nk * tf), BF16),
            jax.ShapeDtypeStruct((T, D), BF16),
        ] + c_outs,
        scratch_shapes=[pltpu.VMEM((tm, D), BF16), pltpu.VMEM((tm, D), F32)] + c_scr,
        compiler_params=_cp(("arbitrary", "arbitrary"), VMEM_BIG),
    )(x, mod, wup, wup, wdn, *c_args)
    return out[0], out[1], out[2], out[3], list(out[4:])


def _ffn_bwd(x, dy, mod, f, g, u, wup, wdn, l, j, S, comm=None):
    T = x.shape[0]
    tf = wup.shape[-1]
    nk = 2
    tm = _tile(S, 256)
    tpb = S // tm
    nt = T // tm
    c_args, c_specs, c_outs, c_scr = _hosted(comm)
    n_ci, n_co = len(c_args), len(c_outs)

    def body(*refs):
        x_ref, dy_ref, mod_ref, f_ref, g_ref, u_ref, wup_ref, wd_ref = refs[:8]
        c_in = refs[8:8 + n_ci]
        dx_ref, dmod_ref, wacc_ref, h_ref, df_ref, a_ref, dgu_ref = refs[8 + n_ci:15 + n_ci]
        c_out = refs[15 + n_ci:15 + n_ci + n_co]
        c_sems = refs[15 + n_ci + n_co:]
        i = pl.program_id(0)

        @pl.when(i == 0)
        def _():
            wacc_ref[...] = jnp.zeros_like(wacc_ref)
            if comm is not None:
                comm["start"](c_in, c_out, c_sems)

        @pl.when(i % tpb == 0)
        def _():
            dmod_ref[...] = jnp.zeros_like(dmod_ref)

        dy_ = dy_ref[...]
        h, xn, rstd = _normmod(x_ref[...], mod_ref)
        h_ref[...] = h.astype(BF16)
        dfb = ((0.5 * mod_ref[2:3, :]) * dy_).astype(BF16)
        df_ref[...] = dfb
        dmod_ref[2:3, :] += _colsum(0.5 * f_ref[...].astype(F32) * dy_)
        dh = None
        for k in range(nk):
            cols = slice(k * tf, (k + 1) * tf)
            da = _dot_nt(dfb, wd_ref[cols, :])
            gg = g_ref[:, cols].astype(F32)
            uu = u_ref[:, cols].astype(F32)
            sig = _sigmoid(gg)
            s = gg * sig
            a_ref[:, cols] = (s * uu).astype(BF16)
            du = (da * s).astype(BF16)
            dg = (da * uu * (sig * (1.0 + gg * (1.0 - sig)))).astype(BF16)
            dgu_ref[0, :, cols] = dg
            dgu_ref[1, :, cols] = du
            part = _dot_nt(dg, wup_ref[k]) + _dot_nt(du, wup_ref[nk + k])
            dh = part if dh is None else dh + part
        dx_ref[...] = dy_ + _normmod_bwd(dh, xn, rstd, mod_ref, dmod_ref, wacc_ref)

        if comm is not None:
            @pl.when(i == nt - 1)
            def _():
                comm["finish"](c_in, c_out, c_sems)

    once = pl.Buffered(1)
    anyspec = pl.BlockSpec(memory_space=pl.ANY)
    out = pl.pallas_call(
        body, name=f"ffn_bwd_{l}_{j}",
        grid=(nt,),
        in_specs=[
            pl.BlockSpec((tm, D), lambda i: (i, 0)),
            pl.BlockSpec((tm, D), lambda i: (i, 0)),
            pl.BlockSpec((None, 8, D), lambda i: (i // tpb, 0, 0)),
            pl.BlockSpec((tm, D), lambda i: (i, 0)),
            pl.BlockSpec((tm, nk * tf), lambda i: (i, 0)),
            pl.BlockSpec((tm, nk * tf), lambda i: (i, 0)),
            pl.BlockSpec((2 * nk, D, tf), lambda i: (0, 0, 0), pipeline_mode=once),
            pl.BlockSpec((nk * tf, D), lambda i: (0, 0), pipeline_mode=once),
        ] + c_specs,
        out_specs=[
            pl.BlockSpec((tm, D), lambda i: (i, 0)),
            pl.BlockSpec((None, 8, D), lambda i: (i // tpb, 0, 0)),
            pl.BlockSpec((8, D), lambda i: (0, 0)),
            pl.BlockSpec((tm, D), lambda i: (i, 0)),
            pl.BlockSpec((tm, D), lambda i: (i, 0)),
            pl.BlockSpec((tm, nk * tf), lambda i: (i, 0)),
            pl.BlockSpec((2, tm, nk * tf), lambda i: (0, i, 0)),
        ] + [anyspec] * n_co,
        out_shape=[
            jax.ShapeDtypeStruct((T, D), F32),
            jax.ShapeDtypeStruct((T // S, 8, D), F32),
            jax.ShapeDtypeStruct((8, D), F32),
            jax.ShapeDtypeStruct((T, D), BF16),
            jax.ShapeDtypeStruct((T, D), BF16),
            jax.ShapeDtypeStruct((T, nk * tf), BF16),
            jax.ShapeDtypeStruct((2, T, nk * tf), BF16),
        ] + c_outs,
        scratch_shapes=c_scr,
        compiler_params=_cp(("arbitrary",), VMEM_BIG),
    )(x, dy, mod, f, g, u, wup, wdn, *c_args)
    return tuple(out[:7]) + (list(out[7:]),)


def _mm_tn(a, b, name, tma=None, tnb=None, split_n=False, with_bf16=False):
    T, M = a.shape
    b3 = b if b.ndim == 3 else b[None]
    nb, _, N = b3.shape
    tma = tma or M
    tnb = tnb or N
    npb = N // tnb
    tt = _tile(T, 1024)
    nt = T // tt

    def body(a_ref, b_ref, o_ref, *ob_ref):
        @pl.when(pl.program_id(2) == 0)
        def _():
            o_ref[...] = jnp.zeros_like(o_ref)

        o_ref[...] += _dot_tn(a_ref[...], b_ref[...])

        if with_bf16:
            @pl.when(pl.program_id(2) == nt - 1)
            def _():
                ob_ref[0][...] = o_ref[...].astype(BF16)

    if split_n:
        shape = (nb * npb, M, tnb)
        out_spec = pl.BlockSpec((None, tma, tnb), lambda m, n, t: (n, m, 0))
    else:
        assert nb == 1
        shape = (M, N)
        out_spec = pl.BlockSpec((tma, tnb), lambda m, n, t: (m, n))
    dts = (F32, BF16) if with_bf16 else (F32,)
    out = pl.pallas_call(
        body, name=name,
        grid=(M // tma, nb * npb, nt),
        in_specs=[pl.BlockSpec((tt, tma), lambda m, n, t: (t, m)),
                  pl.BlockSpec((None, tt, tnb), lambda m, n, t: (n // npb, t, n % npb))],
        out_specs=[out_spec] * len(dts),
        out_shape=[jax.ShapeDtypeStruct(shape, dt) for dt in dts],
        compiler_params=_cp(("arbitrary", "arbitrary", "arbitrary"), VMEM_BIG),
    )(a, b3)
    return tuple(out) if with_bf16 else out[0]


def _mix_in_fwd(x, mod, winp, l, S):
    T = x.shape[0]
    tm = _tile(S, 512)
    tpb = S // tm

    def body(x_ref, mod_ref, w_ref, h_ref, p_ref):
        h, _, _ = _normmod(x_ref[...], mod_ref)
        hb = h.astype(BF16)
        h_ref[...] = hb
        p_ref[...] = _dot(hb, w_ref[...])

    return pl.pallas_call(
        body, name=f"mix_in_fwd_{l}",
        grid=(T // tm,),
        in_specs=[pl.BlockSpec((tm, D), lambda i: (i, 0)),
                  pl.BlockSpec((None, 8, D), lambda i: (i // tpb, 0, 0)),
                  pl.BlockSpec((D, N_INP), lambda i: (0, 0))],
        out_specs=[pl.BlockSpec((tm, D), lambda i: (i, 0)),
                   pl.BlockSpec((tm, N_INP), lambda i: (i, 0))],
        out_shape=[jax.ShapeDtypeStruct((T, D), BF16), jax.ShapeDtypeStruct((T, N_INP), F32)],
        compiler_params=_cp(("arbitrary",), VMEM_BIG),
    )(x, mod, winp)


def _mix_in_bwd(x, dres, mod, dproj, winp, l, S):
    T = x.shape[0]
    tm = _tile(S, 512)
    tpb = S // tm

    def body(x_ref, dr_ref, mod_ref, dp_ref, w_ref, dx_ref, dmod_ref, wacc_ref):
        i = pl.program_id(0)

        @pl.when(i == 0)
        def _():
            wacc_ref[...] = jnp.zeros_like(wacc_ref)

        @pl.when(i % tpb == 0)
        def _():
            dmod_ref[...] = jnp.zeros_like(dmod_ref)

        dh = _dot_nt(dp_ref[...], w_ref[...])
        _, xn, rstd = _normmod(x_ref[...], mod_ref)
        dx_ref[...] = dr_ref[...] + _normmod_bwd(dh, xn, rstd, mod_ref, dmod_ref, wacc_ref)

    return pl.pallas_call(
        body, name=f"mix_in_bwd_{l}",
        grid=(T // tm,),
        in_specs=[pl.BlockSpec((tm, D), lambda i: (i, 0)),
                  pl.BlockSpec((tm, D), lambda i: (i, 0)),
                  pl.BlockSpec((None, 8, D), lambda i: (i // tpb, 0, 0)),
                  pl.BlockSpec((tm, N_INP), lambda i: (i, 0)),
                  pl.BlockSpec((D, N_INP), lambda i: (0, 0))],
        out_specs=[pl.BlockSpec((tm, D), lambda i: (i, 0)),
                   pl.BlockSpec((None, 8, D), lambda i: (i // tpb, 0, 0)),
                   pl.BlockSpec((8, D), lambda i: (0, 0))],
        out_shape=[jax.ShapeDtypeStruct((T, D), F32),
                   jax.ShapeDtypeStruct((T // S, 8, D), F32),
                   jax.ShapeDtypeStruct((8, D), F32)],
        compiler_params=_cp(("arbitrary",), VMEM_BIG),
    )(x, dres, mod, dproj, winp)


_GROUPS = ((0, LW), (LW, LW + AW), (LW + AW, D))


def _mix_out_fwd(x, ylru, osb, ofox, mod, gmix, wout, l, S):
    T = x.shape[0]
    tm = _tile(S, 512)
    tpb = S // tm

    def body(x_ref, yl_ref, sb_ref, fx_ref, mod_ref, gm_ref, w_ref, xo_ref, y_ref, mo_ref):
        for src, (lo, hi) in zip((yl_ref, sb_ref, fx_ref), _GROUPS):
            vn, _ = _rms_rows(src[...])
            y_ref[:, lo:hi] = (vn * gm_ref[0:1, lo:hi]).astype(BF16)
        mo = _dot(y_ref[...], w_ref[...])
        mo_ref[...] = mo.astype(BF16)
        xo_ref[...] = x_ref[...] + mod_ref[2:3, :] * mo

    return pl.pallas_call(
        body, name=f"mix_out_fwd_{l}",
        grid=(T // tm,),
        in_specs=[pl.BlockSpec((tm, D), lambda i: (i, 0)),
                  pl.BlockSpec((tm, LW), lambda i: (i, 0)),
                  pl.BlockSpec((tm, AW), lambda i: (i, 0)),
                  pl.BlockSpec((tm, AW), lambda i: (i, 0)),
                  pl.BlockSpec((None, 8, D), lambda i: (i // tpb, 0, 0)),
                  pl.BlockSpec((None, 8, D), lambda i: (l, 0, 0)),
                  pl.BlockSpec((D, D), lambda i: (0, 0))],
        out_specs=[pl.BlockSpec((tm, D), lambda i: (i, 0)),
                   pl.BlockSpec((tm, D), lambda i: (i, 0)),
                   pl.BlockSpec((tm, D), lambda i: (i, 0))],
        out_shape=[jax.ShapeDtypeStruct((T, D), F32),
                   jax.ShapeDtypeStruct((T, D), BF16),
                   jax.ShapeDtypeStruct((T, D), BF16)],
        compiler_params=_cp(("arbitrary",), VMEM_BIG),
    )(x, ylru, osb, ofox, mod, gmix, wout)


def _mix_out_bwd(dx2, ylru, osb, ofox, mo, mod, gmix, wout, l, S):
    T = dx2.shape[0]
    tm = _tile(S, 512)
    tpb = S // tm

    def body(dx_ref, yl_ref, sb_ref, fx_ref, mo_ref, mod_ref, gm_ref, w_ref,
             dyl_ref, dsb_ref, dfx_ref, dmo_ref, dmod_ref, wacc_ref):
        i = pl.program_id(0)

        @pl.when(i == 0)
        def _():
            wacc_ref[...] = jnp.zeros_like(wacc_ref)

        @pl.when(i % tpb == 0)
        def _():
            dmod_ref[...] = jnp.zeros_like(dmod_ref)

        dx = dx_ref[...]
        dmod_ref[2:3, :] += _colsum(mo_ref[...].astype(F32) * dx)
        dmo = (mod_ref[2:3, :] * dx).astype(BF16)
        dmo_ref[...] = dmo
        dy = _dot_nt(dmo, w_ref[...])
        for src, dst, (lo, hi) in zip((yl_ref, sb_ref, fx_ref), (dyl_ref, dsb_ref, dfx_ref), _GROUPS):
            vn, rstd = _rms_rows(src[...])
            dyg = dy[:, lo:hi]
            wacc_ref[0:1, lo:hi] += _colsum(dyg * vn)
            dst[...] = _rms_bwd(vn, rstd, dyg * gm_ref[0:1, lo:hi])

    return pl.pallas_call(
        body, name=f"mix_out_bwd_{l}",
        grid=(T // tm,),
        in_specs=[pl.BlockSpec((tm, D), lambda i: (i, 0)),
                  pl.BlockSpec((tm, LW), lambda i: (i, 0)),
                  pl.BlockSpec((tm, AW), lambda i: (i, 0)),
                  pl.BlockSpec((tm, AW), lambda i: (i, 0)),
                  pl.BlockSpec((tm, D), lambda i: (i, 0)),
                  pl.BlockSpec((None, 8, D), lambda i: (i // tpb, 0, 0)),
                  pl.BlockSpec((None, 8, D), lambda i: (l, 0, 0)),
                  pl.BlockSpec((D, D), lambda i: (0, 0))],
        out_specs=[pl.BlockSpec((tm, LW), lambda i: (i, 0)),
                   pl.BlockSpec((tm, AW), lambda i: (i, 0)),
                   pl.BlockSpec((tm, AW), lambda i: (i, 0)),
                   pl.BlockSpec((tm, D), lambda i: (i, 0)),
                   pl.BlockSpec((None, 8, D), lambda i: (i // tpb, 0, 0)),
                   pl.BlockSpec((8, D), lambda i: (0, 0))],
        out_shape=[jax.ShapeDtypeStruct((T, LW), F32),
                   jax.ShapeDtypeStruct((T, AW), F32),
                   jax.ShapeDtypeStruct((T, AW), F32),
                   jax.ShapeDtypeStruct((T, D), BF16),
                   jax.ShapeDtypeStruct((T // S, 8, D), F32),
                   jax.ShapeDtypeStruct((8, D), F32)],
        compiler_params=_cp(("arbitrary",), VMEM_BIG),
    )(dx2, ylru, osb, ofox, mo, mod, gmix, wout)


def _loss_head(y, tgt, S):
    T = y.shape[0]
    tm = _tile(S, 512)

    def body(y_ref, t_ref, dy_ref, l_ref):
        @pl.when(pl.program_id(0) == 0)
        def _():
            l_ref[...] = jnp.zeros_like(l_ref)

        d = y_ref[...] - t_ref[...]
        dy_ref[...] = d * (1.0 / D)
        l_ref[...] += (0.5 / D) * _rowsum(_colsum(d * d))

    return pl.pallas_call(
        body, name="loss_head",
        grid=(T // tm,),
        in_specs=[pl.BlockSpec((tm, D), lambda i: (i, 0)), pl.BlockSpec((tm, D), lambda i: (i, 0))],
        out_specs=[pl.BlockSpec((tm, D), lambda i: (i, 0)), pl.BlockSpec((8, 128), lambda i: (0, 0))],
        out_shape=[jax.ShapeDtypeStruct((T, D), F32), jax.ShapeDtypeStruct((8, 128), F32)],
        compiler_params=_cp(("arbitrary",)),
    )(y, tgt)


def _lru_gates(u, vp_ref, wr_ref, wi_ref):
    ub = u.astype(BF16)
    r = _sigmoid(_dot(ub, wr_ref[...]) + vp_ref[1:2, :])
    ig = _sigmoid(_dot(ub, wi_ref[...]) + vp_ref[2:3, :])
    lam = vp_ref[3:4, :]
    sp = jnp.maximum(-lam, 0.0) + _log1p(jnp.exp(-jnp.abs(lam)))
    log_a = (-LRU_C) * r * sp
    a = jnp.exp(log_a)
    mult = jnp.sqrt(-_expm1_neg(2.0 * log_a))
    return ub, r, ig, sp, a, mult


def _conv_taps(x, xp, row, cw_ref):
    xs = [x]
    for d in (1, 2, 3):
        xs.append(jnp.where(row >= d, pltpu.roll(x, d, 0), pltpu.roll(xp, d, 0)))
    u = xs[0] * cw_ref[3:4, :]
    for d in (1, 2, 3):
        u = u + xs[d] * cw_ref[3 - d:4 - d, :]
    return xs, u


def _lru_fwd(proj, cw, vp, wr, wi, l, S):
    T = proj.shape[0]
    ts = _tile(S, 256)
    nb = S // ts

    def body(x_ref, lg_ref, cw_ref, vp_ref, wr_ref, wi_ref, y_ref, h_ref, xp_sc, hc_sc):
        @pl.when(pl.program_id(1) == 0)
        def _():
            xp_sc[...] = jnp.zeros_like(xp_sc)
            hc_sc[...] = jnp.zeros_like(hc_sc)

        row = lax.broadcasted_iota(jnp.int32, (ts, LW), 0)
        x = x_ref[...]
        _, u = _conv_taps(x, xp_sc[...], row, cw_ref)
        u = u + vp_ref[0:1, :]
        xp_sc[...] = x
        _, _, ig, _, a, mult = _lru_gates(u, vp_ref, wr_ref, wi_ref)
        bv = mult * (ig * u)
        av = a
        d = 1
        while d < ts:
            a_s = jnp.where(row >= d, pltpu.roll(av, d, 0), 1.0)
            b_s = jnp.where(row >= d, pltpu.roll(bv, d, 0), 0.0)
            bv = av * b_s + bv
            av = av * a_s
            d *= 2
        h = bv + av * hc_sc[7:8, :]
        hc_sc[...] = h[ts - 8:ts, :]
        h_ref[...] = h
        gl, _ = _gelu_and_grad(lg_ref[...])
        y_ref[...] = h * gl

    return pl.pallas_call(
        body, name=f"lru_fwd_{l}",
        grid=(T // S, nb),
        in_specs=[pl.BlockSpec((ts, LW), lambda b, j: (b * nb + j, 0)),
                  pl.BlockSpec((ts, LW), lambda b, j: (b * nb + j, 1)),
                  pl.BlockSpec((None, 8, LW), lambda b, j: (l, 0, 0)),
                  pl.BlockSpec((None, 8, LW), lambda b, j: (l, 0, 0)),
                  pl.BlockSpec((None, LW, LW), lambda b, j: (l, 0, 0)),
                  pl.BlockSpec((None, LW, LW), lambda b, j: (l, 0, 0))],
        out_specs=[pl.BlockSpec((ts, LW), lambda b, j: (b * nb + j, 0)),
                   pl.BlockSpec((ts, LW), lambda b, j: (b * nb + j, 0))],
        out_shape=[jax.ShapeDtypeStruct((T, LW), F32), jax.ShapeDtypeStruct((T, LW), F32)],
        scratch_shapes=[pltpu.VMEM((ts, LW), F32), pltpu.VMEM((8, LW), F32)],
        compiler_params=_cp(("arbitrary", "arbitrary")),
    )(proj, proj, cw, vp, wr, wi)


def _lru_bwd(dyl, proj, h, cw, vp, wr, wi, l, S):
    T = proj.shape[0]
    ts = _tile(S, 256)
    nb = S // ts

    def body(dy_ref, x_ref, xprev_ref, lg_ref, h_ref, hprev_ref, cw_ref, vp_ref, wr_ref, wi_ref,
             dx_ref, dlg_ref, dpr_ref, dpi_ref, ub_ref, wacc_ref, gc_sc, af_sc, dun_sc):
        b = pl.program_id(0)
        j = pl.program_id(1)
        first = j == nb - 1

        @pl.when((b == 0) & (j == 0))
        def _():
            wacc_ref[...] = jnp.zeros_like(wacc_ref)

        @pl.when(j == 0)
        def _():
            gc_sc[...] = jnp.zeros_like(gc_sc)
            af_sc[...] = jnp.ones_like(af_sc)
            dun_sc[...] = jnp.zeros_like(dun_sc)

        row = lax.broadcasted_iota(jnp.int32, (ts, LW), 0)
        keep = jnp.where(first, 0.0, 1.0)
        x = x_ref[...]
        xs, u = _conv_taps(x, xprev_ref[...] * keep, row, cw_ref)
        u = u + vp_ref[0:1, :]
        ub, r, ig, sp, a, mult = _lru_gates(u, vp_ref, wr_ref, wi_ref)
        ub_ref[...] = ub
        hh = h_ref[...]
        h_m1 = jnp.where(row >= 1, pltpu.roll(hh, 1, 0), pltpu.roll(hprev_ref[...] * keep, 1, 0))
        dy = dy_ref[...]
        gl, dgl = _gelu_and_grad(lg_ref[...])
        dlg_ref[...] = (dy * hh * dgl).astype(BF16)
        bv = dy * gl
        av = jnp.where(row < ts - 1, pltpu.roll(a, ts - 1, 0), af_sc[0:1, :])
        d = 1
        while d < ts:
            a_s = jnp.where(row < ts - d, pltpu.roll(av, ts - d, 0), 1.0)
            b_s = jnp.where(row < ts - d, pltpu.roll(bv, ts - d, 0), 0.0)
            bv = av * b_s + bv
            av = av * a_s
            d *= 2
        gt = bv + av * gc_sc[0:1, :]
        gc_sc[...] = gt[0:8, :]
        af_sc[...] = a[0:8, :]
        da = gt * h_m1
        d_ig = gt * mult * u
        d_mult = gt * ig * u
        du = gt * mult * ig
        dlog_a = da * a - d_mult * (a * a) / mult
        dpre_r = (dlog_a * ((-LRU_C) * sp)) * r * (1.0 - r)
        dpre_i = d_ig * ig * (1.0 - ig)
        lam = vp_ref[3:4, :]
        wacc_ref[7:8, :] += _colsum(dlog_a * r) * (LRU_C * _sigmoid(-lam))
        wacc_ref[5:6, :] += _colsum(dpre_r)
        wacc_ref[6:7, :] += _colsum(dpre_i)
        dprb = dpre_r.astype(BF16)
        dpib = dpre_i.astype(BF16)
        dpr_ref[...] = dprb
        dpi_ref[...] = dpib
        du = du + _dot_nt(dprb, wr_ref[...]) + _dot_nt(dpib, wi_ref[...])
        wacc_ref[4:5, :] += _colsum(du)
        dun = dun_sc[...]
        dx = du * cw_ref[3:4, :]
        wacc_ref[3:4, :] += _colsum(du * xs[0])
        for dd in (1, 2, 3):
            du_s = jnp.where(row < ts - dd, pltpu.roll(du, ts - dd, 0), pltpu.roll(dun, ts - dd, 0))
            dx = dx + du_s * cw_ref[3 - dd:4 - dd, :]
            wacc_ref[3 - dd:4 - dd, :] += _colsum(du * xs[dd])
        dun_sc[...] = du
        dx_ref[...] = dx.astype(BF16)

    def tb(b, j):
        return b * nb + (nb - 1 - j)

    def tbp(b, j):
        return b * nb + jnp.maximum(nb - 2 - j, 0)

    return pl.pallas_call(
        body, name=f"lru_bwd_{l}",
        grid=(T // S, nb),
        in_specs=[pl.BlockSpec((ts, LW), lambda b, j: (tb(b, j), 0)),
                  pl.BlockSpec((ts, LW), lambda b, j: (tb(b, j), 0)),
                  pl.BlockSpec((ts, LW), lambda b, j: (tbp(b, j), 0)),
                  pl.BlockSpec((ts, LW), lambda b, j: (tb(b, j), 1)),
                  pl.BlockSpec((ts, LW), lambda b, j: (tb(b, j), 0)),
                  pl.BlockSpec((ts, LW), lambda b, j: (tbp(b, j), 0)),
                  pl.BlockSpec((None, 8, LW), lambda b, j: (l, 0, 0)),
                  pl.BlockSpec((None, 8, LW), lambda b, j: (l, 0, 0)),
                  pl.BlockSpec((None, LW, LW), lambda b, j: (l, 0, 0)),
                  pl.BlockSpec((None, LW, LW), lambda b, j: (l, 0, 0))],
        out_specs=[pl.BlockSpec((ts, LW), lambda b, j: (tb(b, j), 0)),
                   pl.BlockSpec((ts, LW), lambda b, j: (tb(b, j), 0)),
                   pl.BlockSpec((ts, LW), lambda b, j: (tb(b, j), 0)),
                   pl.BlockSpec((ts, LW), lambda b, j: (tb(b, j), 0)),
                   pl.BlockSpec((ts, LW), lambda b, j: (tb(b, j), 0)),
                   pl.BlockSpec((8, LW), lambda b, j: (0, 0))],
        out_shape=[jax.ShapeDtypeStruct((T, LW), BF16),
                   jax.ShapeDtypeStruct((T, LW), BF16),
                   jax.ShapeDtypeStruct((T, LW), BF16),
                   jax.ShapeDtypeStruct((T, LW), BF16),
                   jax.ShapeDtypeStruct((T, LW), BF16),
                   jax.ShapeDtypeStruct((8, LW), F32)],
        scratch_shapes=[pltpu.VMEM((8, LW), F32), pltpu.VMEM((8, LW), F32), pltpu.VMEM((ts, LW), F32)],
        compiler_params=_cp(("arbitrary", "arbitrary")),
    )(dyl, proj, proj, proj, h, h, cw, vp, wr, wi)


def _fgate_fwd(proj, bfp, l, S):
    T = proj.shape[0]

    def body(x_ref, b_ref, o_ref):
        z = x_ref[...] + b_ref[0:1, :]
        v = jnp.minimum(z, 0.0) - _log1p(jnp.exp(-jnp.abs(z)))
        row = lax.broadcasted_iota(jnp.int32, (S, 128), 0)
        d = 1
        while d < S:
            v = v + jnp.where(row >= d, pltpu.roll(v, d, 0), 0.0)
            d *= 2
        o_ref[...] = v

    return pl.pallas_call(
        body, name=f"fgate_fwd_{l}",
        grid=(T // S,),
        in_specs=[pl.BlockSpec((S, 128), lambda b: (b, F_BLK)),
                  pl.BlockSpec((None, 8, 128), lambda b: (l, 0, 0))],
        out_specs=pl.BlockSpec((S, 128), lambda b: (b, 0)),
        out_shape=jax.ShapeDtypeStruct((T, 128), F32),
        compiler_params=_cp(("arbitrary",)),
    )(proj, bfp)


def _fgate_bwd(dcum, proj, bfp, l, S):
    T = proj.shape[0]

    def body(d_ref, x_ref, b_ref, o_ref, wacc_ref):
        @pl.when(pl.program_id(0) == 0)
        def _():
            wacc_ref[...] = jnp.zeros_like(wacc_ref)

        v = d_ref[...]
        row = lax.broadcasted_iota(jnp.int32, (S, 128), 0)
        d = 1
        while d < S:
            v = v + jnp.where(row < S - d, pltpu.roll(v, S - d, 0), 0.0)
            d *= 2
        z = x_ref[...] + b_ref[0:1, :]
        dz = v * _sigmoid(-z)
        o_ref[...] = dz.astype(BF16)
        wacc_ref[0:1, :] += _colsum(dz)

    return pl.pallas_call(
        body, name=f"fgate_bwd_{l}",
        grid=(T // S,),
        in_specs=[pl.BlockSpec((S, 128), lambda b: (b, 0)),
                  pl.BlockSpec((S, 128), lambda b: (b, F_BLK)),
                  pl.BlockSpec((None, 8, 128), lambda b: (l, 0, 0))],
        out_specs=[pl.BlockSpec((S, 128), lambda b: (b, 0)), pl.BlockSpec((8, 128), lambda b: (0, 0))],
        out_shape=[jax.ShapeDtypeStruct((T, 128), BF16), jax.ShapeDtypeStruct((8, 128), F32)],
        compiler_params=_cp(("arbitrary",)),
    )(dcum, proj, bfp)


SBQ_BLK, SBK_BLK, SBV_BLK = 8, 10, 12
FXQ_BLK, FXK_BLK, FXV_BLK = 14, 16, 18
PAIR = 2 * HD


def _lane_masks():
    lane = lax.broadcasted_iota(jnp.int32, (1, PAIR), 1)
    return lane, lane < HD


def _pair_select(m0, a0, a1):
    return jnp.where(m0, a0, a1)


def _pair_split(x, m0):
    return jnp.where(m0, x, 0.0).astype(BF16), jnp.where(m0, 0.0, x).astype(BF16)


def _pair_mean(x, m0):
    s0 = _rowsum(jnp.where(m0, x, 0.0))
    s1 = _rowsum(x) - s0
    return jnp.where(m0, s0, s1) * (1.0 / HD)


def _pair_rms(x, m0):
    rstd = lax.rsqrt(_pair_mean(x * x, m0) + EPS)
    return x * rstd, rstd


def _pair_rms_bwd(xn, rstd, dyn, m0):
    return rstd * (dyn - xn * _pair_mean(dyn * xn, m0))


def _logsig2(z):
    l1p = jnp.log(1.0 + jnp.exp(-jnp.abs(z)))
    lb = jnp.minimum(z, 0.0) - l1p
    return lb, lb - z


def _rows(ref, blk, size):
    return ref[pl.ds(pl.multiple_of(blk * size, size), size), :]


def _transpose_blocks(src_ref, dst_sc, nblk, blk):
    for kb in range(nblk):
        dst_sc[kb] = src_ref[kb * blk:(kb + 1) * blk, :].astype(F32).T.astype(BF16)


def _sbq_fwd(proj, l, S, comm=None):
    T = proj.shape[0]
    tb = TQ_(S)
    nb = S // tb
    nbat = T // S
    c_args, c_specs, c_outs, c_scr = _hosted(comm)
    n_ci, n_co = len(c_args), len(c_outs)

    def body(*refs):
        q_ref, k_ref, v_ref = refs[:3]
        c_in = refs[3:3 + n_ci]
        o_ref, t1_ref = refs[3 + n_ci:5 + n_ci]
        c_out = refs[5 + n_ci:5 + n_ci + n_co]
        kt_sc, vb_sc = refs[5 + n_ci + n_co:7 + n_ci + n_co]
        c_sems = refs[7 + n_ci + n_co:]
        step = pl.program_id(0) * 2 + pl.program_id(1)
        if comm is not None:
            @pl.when(step == 0)
            def _():
                comm["start"](c_in, c_out, c_sems)

        _transpose_blocks(k_ref, kt_sc, nb, tb)
        vb_sc[...] = v_ref[...].astype(BF16)
        lane, m0 = _lane_masks()
        tri = _tri(tb, "row_gt_col")
        past = lax.broadcasted_iota(jnp.int32, (tb, tb), 1) < lax.broadcasted_iota(jnp.int32, (tb, tb), 0)

        def qloop(qb, carry):
            qh = _pair_split(_rows(q_ref, qb, tb) * SCALE, m0)

            def scores(kb):
                return tuple(_dot(qh[h], kt_sc[kb]) for h in range(2))

            def block(kb, kb_next, z, c, masked):
                mid = []
                for h in range(2):
                    lb, l1 = _logsig2(z[h])
                    if masked:
                        l1 = jnp.where(past, l1, 0.0)
                    mid.append((lb, l1, _cumsum_mm(l1, tri, parts=2)))
                z_next = scores(kb_next)
                pv, runs = [], []
                for h in range(2):
                    lb, l1, cs = mid[h]
                    w = jnp.exp(lb + (cs + c[h][1]))
                    if masked:
                        w = jnp.where(past, w, 0.0)
                    pv.append(_dot(w.astype(BF16), _rows(vb_sc, kb, tb)))
                    runs.append(c[h][1] + (cs[:, 0:1] + l1[:, 0:1]))
                return z_next, tuple((c[h][0] + pv[h], runs[h]) for h in range(2))

            zero = (jnp.zeros((tb, PAIR), F32), jnp.zeros((tb, 1), F32))
            z, c = block(qb, jnp.maximum(qb - 1, 0), scores(qb), (zero, zero), True)

            def off_diag(i, zc):
                kb = qb - 1 - i
                return block(kb, jnp.maximum(kb - 1, 0), zc[0], zc[1], False)

            _, c = lax.fori_loop(0, qb, off_diag, (z, c))
            r0 = pl.multiple_of(qb * tb, tb)
            o_ref[pl.ds(r0, tb), :] = _pair_select(m0, c[0][0], c[1][0])
            t1_ref[pl.ds(r0, tb), :] = jnp.where(lane == 0, c[0][1], jnp.where(lane == 1, c[1][1], 0.0))
            return carry

        lax.fori_loop(0, nb, qloop, 0)
        if comm is not None:
            @pl.when(step == 2 * nbat - 1)
            def _():
                comm["finish"](c_in, c_out, c_sems)

    def col(blk):
        return pl.BlockSpec((S, PAIR), lambda b, p: (b, blk + p))

    anyspec = pl.BlockSpec(memory_space=pl.ANY)
    out = pl.pallas_call(
        body, name=f"sb_fwd_{l}",
        grid=(nbat, 2),
        in_specs=[col(SBQ_BLK), col(SBK_BLK), col(SBV_BLK)] + c_specs,
        out_specs=[col(0), col(0)] + [anyspec] * n_co,
        out_shape=[jax.ShapeDtypeStruct((T, AW), F32), jax.ShapeDtypeStruct((T, AW), F32)] + c_outs,
        scratch_shapes=[pltpu.VMEM((nb, PAIR, tb), BF16), pltpu.VMEM((S, PAIR), BF16)] + c_scr,
        compiler_params=_cp(("arbitrary", "arbitrary"), VMEM_BIG),
    )(proj, proj, proj, *c_args)
    return out[0], out[1], list(out[2:])


def _sbq_bwd(proj, do, t1, l, S, comm=None):
    T = proj.shape[0]
    tb = TQ_(S)
    nb = S // tb
    nbat = T // S
    c_args, c_specs, c_outs, c_scr = _hosted(comm)
    n_ci, n_co = len(c_args), len(c_outs)

    def body(*refs):
        q_ref, k_ref, v_ref, do_ref, t1_ref = refs[:5]
        c_in = refs[5:5 + n_ci]
        dq_ref, dk_ref, dv_ref = refs[5 + n_ci:8 + n_ci]
        c_out = refs[8 + n_ci:8 + n_ci + n_co]
        kb_sc, kt_sc, vt_sc, dkt_sc, dvt_sc = refs[8 + n_ci + n_co:13 + n_ci + n_co]
        c_sems = refs[13 + n_ci + n_co:]
        step = pl.program_id(0) * 2 + pl.program_id(1)
        if comm is not None:
            @pl.when(step == 0)
            def _():
                comm["start"](c_in, c_out, c_sems)

        kb_sc[...] = k_ref[...].astype(BF16)
        _transpose_blocks(k_ref, kt_sc, nb, tb)
        _transpose_blocks(v_ref, vt_sc, nb, tb)
        dkt_sc[...] = jnp.zeros_like(dkt_sc)
        dvt_sc[...] = jnp.zeros_like(dvt_sc)
        _, m0 = _lane_masks()
        mt0 = lax.broadcasted_iota(jnp.int32, (PAIR, 1), 0) < HD
        tri_in = _tri(tb, "row_le_col")
        tri_ex = _tri(tb, "row_lt_col")
        past = lax.broadcasted_iota(jnp.int32, (tb, tb), 1) < lax.broadcasted_iota(jnp.int32, (tb, tb), 0)

        def qloop(qb, carry):
            qf = _rows(q_ref, qb, tb) * SCALE
            dof = _rows(do_ref, qb, tb)
            qh = _pair_split(qf, m0)
            doh = _pair_split(dof, m0)
            qth = _pair_split(qf.T, mt0)
            doth = _pair_split(dof.T, mt0)
            t1v = _rows(t1_ref, qb, tb)
            tot = (t1v[:, 0:1], t1v[:, 1:2])

            def accumulate(kb, wz, dqs):
                out = []
                for h in range(2):
                    wb, dz = wz[h]
                    dvt_sc[kb] += _dot(doth[h], wb)
                    dkt_sc[kb] += _dot(qth[h], dz)
                    out.append(dqs[h] + _dot(dz, _rows(kb_sc, kb, tb)))
                return tuple(out)

            def block(kb, c, masked):
                hs = range(2)
                runs, dqs, (kb_prev, wz_prev) = c
                z = [_dot(qh[h], kt_sc[kb]) for h in hs]
                dw = [_dot(doh[h], vt_sc[kb]) for h in hs]
                st = []
                for h in hs:
                    lb, l1 = _logsig2(z[h])
                    if masked:
                        l1 = jnp.where(past, l1, 0.0)
                    st.append((lb, _cumsum_mm(l1, tri_in, parts=2)))
                dqs = accumulate(kb_prev, wz_prev, dqs)
                mid = []
                for h in hs:
                    lb, p1 = st[h]
                    w = jnp.exp(lb + (tot[h] - (runs[h][0] + p1)))
                    if masked:
                        w = jnp.where(past, w, 0.0)
                    gm = w * dw[h]
                    mid.append((w.astype(BF16), gm, _cumsum_mm(gm, tri_ex, parts=1)))
                new_runs, wz = [], []
                for h in hs:
                    run1, rung = runs[h]
                    wb, gm, cx = mid[h]
                    dz = gm - (gm + (rung + cx)) * jnp.exp(st[h][0])
                    if masked:
                        dz = jnp.where(past, dz, 0.0)
                    wz.append((wb, dz.astype(BF16)))
                    p1 = st[h][1]
                    new_runs.append((run1 + p1[:, tb - 1:tb], rung + (cx[:, tb - 1:tb] + gm[:, tb - 1:tb])))
                return tuple(new_runs), dqs, (kb, tuple(wz))

            z1 = jnp.zeros((tb, 1), F32)
            zq = jnp.zeros((tb, PAIR), F32)
            zb = jnp.zeros((tb, tb), BF16)
            none = (jnp.int32(0), ((zb, zb), (zb, zb)))
            c = lax.fori_loop(0, qb, lambda i, cc: block(i, cc, False), (((z1, z1), (z1, z1)), (zq, zq), none))
            _, dqs, (kb_last, wz_last) = block(qb, c, True)
            dqs = accumulate(kb_last, wz_last, dqs)
            r0 = pl.multiple_of(qb * tb, tb)
            dq_ref[pl.ds(r0, tb), :] = (_pair_select(m0, dqs[0], dqs[1]) * SCALE).astype(BF16)
            return carry

        lax.fori_loop(0, nb, qloop, 0)
        for kb in range(nb):
            dk_ref[kb * tb:(kb + 1) * tb, :] = dkt_sc[kb].T.astype(BF16)
            dv_ref[kb * tb:(kb + 1) * tb, :] = dvt_sc[kb].T.astype(BF16)
        if comm is not None:
            @pl.when(step == 2 * nbat - 1)
            def _():
                comm["finish"](c_in, c_out, c_sems)

    def col(blk):
        return pl.BlockSpec((S, PAIR), lambda b, p: (b, blk + p))

    sh = jax.ShapeDtypeStruct((T, AW), BF16)
    anyspec = pl.BlockSpec(memory_space=pl.ANY)
    out = pl.pallas_call(
        body, name=f"sb_bwd_{l}",
        grid=(nbat, 2),
        in_specs=[col(SBQ_BLK), col(SBK_BLK), col(SBV_BLK), col(0), col(0)] + c_specs,
        out_specs=[col(0), col(0), col(0)] + [anyspec] * n_co,
        out_shape=[sh, sh, sh] + c_outs,
        scratch_shapes=[pltpu.VMEM((S, PAIR), BF16), pltpu.VMEM((nb, PAIR, tb), BF16), pltpu.VMEM((nb, PAIR, tb), BF16),
                        pltpu.VMEM((nb, PAIR, tb), F32), pltpu.VMEM((nb, PAIR, tb), F32)] + c_scr,
        compiler_params=_cp(("arbitrary", "arbitrary"), VMEM_BIG),
    )(proj, proj, proj, do, t1, *c_args)
    return out[0], out[1], out[2], list(out[3:])


def _foxq_fwd(proj, cum, ck, gqk2, l, S):
    T = proj.shape[0]
    tb = TQ_(S)
    nb = S // tb

    def body(q_ref, k_ref, v_ref, cum_ref, ck_ref, g_ref, o_ref, nl_ref, fk_sc, fkt_sc, vb_sc):
        lane, m0 = _lane_masks()
        p = pl.program_id(1)
        kn, _ = _pair_rms(k_ref[...], m0)
        fk_sc[...] = kn * g_ref[1:2, :]
        _transpose_blocks(fk_sc, fkt_sc, nb, tb)
        vb_sc[...] = v_ref[...].astype(BF16)
        causal = lax.broadcasted_iota(jnp.int32, (tb, tb), 1) <= lax.broadcasted_iota(jnp.int32, (tb, tb), 0)

        def qloop(qb, carry):
            qn, _ = _pair_rms(_rows(q_ref, qb, tb), m0)
            fqh = _pair_split(qn * (g_ref[0:1, :] * SCALE), m0)
            cumv = _rows(cum_ref, qb, tb)
            cq = [_rowsum(jnp.where(lane == 2 * p + h, cumv, 0.0)) for h in range(2)]

            def scores(kb):
                return tuple(_dot(fqh[h], fkt_sc[kb]) for h in range(2))

            def block(kb, kb_next, qk, c, masked):
                st = []
                for h in range(2):
                    s = qk[h] + (cq[h] - ck_ref[h, kb])
                    if masked:
                        s = jnp.where(causal, s, NEG)
                    m2 = jnp.maximum(c[h][0], jnp.max(s, axis=1, keepdims=True))
                    pr = jnp.exp(s - m2)
                    hi = pr.astype(BF16)
                    lo = (pr - hi.astype(F32)).astype(BF16)
                    vv = _rows(vb_sc, kb, tb)
                    st.append((m2, pr, _dot(hi, vv) + _dot(lo, vv)))
                qk_next = scores(kb_next)
                out = []
                for h in range(2):
                    m, lsum, acc = c[h]
                    m2, pr, pv = st[h]
                    al = jnp.exp(m - m2)
                    out.append((m2, al * lsum + _rowsum(pr), al * acc + pv))
                return qk_next, tuple(out)

            zero = (jnp.full((tb, 1), NEG, F32), jnp.zeros((tb, 1), F32), jnp.zeros((tb, PAIR), F32))

            def off_diag(i, sc):
                return block(i, i + 1, sc[0], sc[1], False)

            qk, c = lax.fori_loop(0, qb, off_diag, (scores(0), (zero, zero)))
            _, c = block(qb, qb, qk, c, True)
            r0 = pl.multiple_of(qb * tb, tb)
            o_ref[pl.ds(r0, tb), :] = _pair_select(m0, c[0][2] / c[0][1], c[1][2] / c[1][1])
            nl = [cq[h] - (c[h][0] + jnp.log(c[h][1])) for h in range(2)]
            nl_ref[pl.ds(r0, tb), :] = jnp.where(lane == 0, nl[0], jnp.where(lane == 1, nl[1], 0.0))
            return carry

        lax.fori_loop(0, nb, qloop, 0)

    def col(blk):
        return pl.BlockSpec((S, PAIR), lambda b, p: (b, blk + p))

    return pl.pallas_call(
        body, name=f"fox_fwd_{l}",
        grid=(T // S, 2),
        in_specs=[col(FXQ_BLK), col(FXK_BLK), col(FXV_BLK),
                  pl.BlockSpec((S, 128), lambda b, p: (b, 0)),
                  pl.BlockSpec((None, 2, nb, 1, tb), lambda b, p: (b, p, 0, 0, 0)),
                  pl.BlockSpec((None, 8, PAIR), lambda b, p: (l, 0, 0))],
        out_specs=[col(0), col(0)],
        out_shape=[jax.ShapeDtypeStruct((T, AW), F32)] * 2,
        scratch_shapes=[pltpu.VMEM((S, PAIR), F32), pltpu.VMEM((nb, PAIR, tb), BF16), pltpu.VMEM((S, PAIR), BF16)],
        compiler_params=_cp(("arbitrary", "arbitrary"), VMEM_BIG),
    )(proj, proj, proj, cum, ck, gqk2)


def _foxq_bwd(proj, do, nl, ox, ck, gqk2, l, S):
    T = proj.shape[0]
    tb = TQ_(S)
    nb = S // tb

    def body(q_ref, k_ref, v_ref, do_ref, nl_ref, ox_ref, ck_ref, g_ref,
             dq_ref, dk_ref, dv_ref, dc_ref, wacc_ref, fk_sc, fkt_sc, vt_sc, dfkt_sc, dvt_sc):
        @pl.when((pl.program_id(0) == 0) & (pl.program_id(1) == 0))
        def _():
            wacc_ref[...] = jnp.zeros_like(wacc_ref)

        _, m0 = _lane_masks()
        mt0 = lax.broadcasted_iota(jnp.int32, (PAIR, 1), 0) < HD
        g0 = g_ref[0:1, :]
        g1 = g_ref[1:2, :]
        fk_sc[...] = (_pair_rms(k_ref[...], m0)[0] * g1).astype(BF16)
        _transpose_blocks(fk_sc, fkt_sc, nb, tb)
        _transpose_blocks(v_ref, vt_sc, nb, tb)
        dfkt_sc[...] = jnp.zeros_like(dfkt_sc)
        dvt_sc[...] = jnp.zeros_like(dvt_sc)
        dc_ref[...] = jnp.zeros_like(dc_ref)
        causal = lax.broadcasted_iota(jnp.int32, (tb, tb), 1) <= lax.broadcasted_iota(jnp.int32, (tb, tb), 0)

        def qloop(qb, carry):
            qn, qr = _pair_rms(_rows(q_ref, qb, tb), m0)
            fqf = qn * (g0 * SCALE)
            dof = _rows(do_ref, qb, tb)
            fqh = _pair_split(fqf, m0)
            doh = _pair_split(dof, m0)
            fqth = _pair_split(fqf.T, mt0)
            doth = _pair_split(dof.T, mt0)
            nlv = _rows(nl_ref, qb, tb)
            cql = (nlv[:, 0:1], nlv[:, 1:2])

            def probs(kb, masked):
                qk = [_dot(fqh[h], fkt_sc[kb]) for h in range(2)]
                dp = [_dot(doh[h], vt_sc[kb]) for h in range(2)]
                pr = []
                for h in range(2):
                    e = jnp.exp(qk[h] + (cql[h] - ck_ref[h, kb]))
                    pr.append(jnp.where(causal, e, 0.0) if masked else e)
                return pr, dp

            oxv = _rows(ox_ref, qb, tb)
            dlt = [_rowsum(doh[h].astype(F32) * oxv) for h in range(2)]

            def accumulate(kb, pd, dfqs):
                out = []
                for h in range(2):
                    prb, dsb = pd[h]
                    dvt_sc[kb] += _dot(doth[h], prb)
                    dfkt_sc[kb] += _dot(fqth[h], dsb)
                    out.append(dfqs[h] + _dot(dsb, _rows(fk_sc, kb, tb)))
                return tuple(out)

            def block(kb, c, masked):
                dfqs, (kb_prev, pd_prev) = c
                pr, dp = probs(kb, masked)
                dfqs = accumulate(kb_prev, pd_prev, dfqs)
                pd = []
                for h in range(2):
                    ds = pr[h] * (dp[h] - dlt[h])
                    dc_ref[h, kb] += jnp.broadcast_to(-_colsum(ds), (8, tb))
                    pd.append((pr[h].astype(BF16), ds.astype(BF16)))
                return dfqs, (kb, tuple(pd))

            zq = jnp.zeros((tb, PAIR), F32)
            zb = jnp.zeros((tb, tb), BF16)
            none = (jnp.int32(0), ((zb, zb), (zb, zb)))
            c = lax.fori_loop(0, qb, lambda i, cc: block(i, cc, False), ((zq, zq), none))
            dfqs, (kb_last, pd_last) = block(qb, c, True)
            c = accumulate(kb_last, pd_last, dfqs)
            dfq = _pair_select(m0, c[0], c[1]) * SCALE
            wacc_ref[0:1, :] += _colsum(dfq * qn)
            r0 = pl.multiple_of(qb * tb, tb)
            dq_ref[pl.ds(r0, tb), :] = _pair_rms_bwd(qn, qr, dfq * g0, m0).astype(BF16)
            return carry

        lax.fori_loop(0, nb, qloop, 0)
        for kb in range(nb):
            rows = slice(kb * tb, (kb + 1) * tb)
            dfk = dfkt_sc[kb].T
            knb, krb = _pair_rms(k_ref[rows, :], m0)
            wacc_ref[1:2, :] += _colsum(dfk * knb)
            dk_ref[rows, :] = _pair_rms_bwd(knb, krb, dfk * g1, m0).astype(BF16)
            dv_ref[rows, :] = dvt_sc[kb].T.astype(BF16)

    def col(blk):
        return pl.BlockSpec((S, PAIR), lambda b, p: (b, blk + p))

    sh = jax.ShapeDtypeStruct((T, AW), BF16)
    return pl.pallas_call(
        body, name=f"fox_bwd_{l}",
        grid=(T // S, 2),
        in_specs=[col(FXQ_BLK), col(FXK_BLK), col(FXV_BLK), col(0), col(0), col(0),
                  pl.BlockSpec((None, 2, nb, 1, tb), lambda b, p: (b, p, 0, 0, 0)),
                  pl.BlockSpec((None, 8, PAIR), lambda b, p: (l, 0, 0))],
        out_specs=[col(0), col(0), col(0),
                   pl.BlockSpec((None, 2, nb, 8, tb), lambda b, p: (b, p, 0, 0, 0)),
                   pl.BlockSpec((8, PAIR), lambda b, p: (0, 0))],
        out_shape=[sh, sh, sh,
                   jax.ShapeDtypeStruct((T // S, NH, nb, 8, tb), F32),
                   jax.ShapeDtypeStruct((8, PAIR), F32)],
        scratch_shapes=[pltpu.VMEM((S, PAIR), BF16), pltpu.VMEM((nb, PAIR, tb), BF16), pltpu.VMEM((nb, PAIR, tb), BF16),
                        pltpu.VMEM((nb, PAIR, tb), F32), pltpu.VMEM((nb, PAIR, tb), F32)],
        compiler_params=_cp(("arbitrary", "arbitrary"), VMEM_BIG),
    )(proj, proj, proj, do, nl, ox, ck, gqk2)


def _ada_fwd(c_all, w_ada, b_cols):
    nb, ncol = c_all.shape[0], w_ada.shape[2]
    tn = _tile(ncol, 768)

    def body(c_ref, w_ref, b_ref, o_ref):
        c = c_ref[...]
        ca = (c * _sigmoid(c)).astype(BF16)
        o_ref[...] = _dot(ca, w_ref[...].astype(BF16)) + b_ref[...]

    return pl.pallas_call(
        body, name="ada_fwd",
        grid=(2, ncol // tn),
        in_specs=[pl.BlockSpec((nb, D), lambda l, n: (0, 0)),
                  pl.BlockSpec((None, D, tn), lambda l, n: (l, 0, n)),
                  pl.BlockSpec((None, 1, tn), lambda l, n: (l, 0, n))],
        out_specs=pl.BlockSpec((None, nb, tn), lambda l, n: (l, 0, n)),
        out_shape=jax.ShapeDtypeStruct((2, nb, ncol), F32),
        compiler_params=_cp(("arbitrary", "arbitrary")),
    )(c_all, w_ada, b_cols)


def _ada_bwd(c_all, dmod_cols):
    nb, ncol = c_all.shape[0], dmod_cols.shape[2]
    tn = _tile(ncol, 768)

    def body(c_ref, d_ref, o_ref):
        c = c_ref[...]
        ca = (c * _sigmoid(c)).astype(BF16)
        o_ref[...] = _dot_tn(ca, d_ref[...].astype(BF16))

    return pl.pallas_call(
        body, name="ada_bwd",
        grid=(2, ncol // tn),
        in_specs=[pl.BlockSpec((nb, D), lambda l, n: (0, 0)),
                  pl.BlockSpec((None, nb, tn), lambda l, n: (l, 0, n))],
        out_specs=pl.BlockSpec((None, D, tn), lambda l, n: (l, 0, n)),
        out_shape=jax.ShapeDtypeStruct((2, D, ncol), F32),
        compiler_params=_cp(("arbitrary", "arbitrary")),
    )(c_all, dmod_cols)


def _sum_lead(a, name):
    n, R, C = a.shape
    tr = _tile_div8(R, 256)

    def body(a_ref, o_ref):
        acc = a_ref[0]
        for i in range(1, n):
            acc = acc + a_ref[i]
        o_ref[...] = acc

    return pl.pallas_call(
        body, name=name,
        grid=(R // tr,),
        in_specs=[pl.BlockSpec((n, tr, C), lambda i: (0, i, 0))],
        out_specs=pl.BlockSpec((tr, C), lambda i: (i, 0)),
        out_shape=jax.ShapeDtypeStruct((R, C), F32),
        compiler_params=_cp(("arbitrary",)),
    )(a)


def _adamw(w, g, m, v, name):
    R, C = w.shape
    tr = _tile_div8(R, max(8, (1 << 18) // C))
    c1 = 1.0 / (1.0 - ADAM_B1 ** ADAM_STEP)
    c2 = 1.0 / (1.0 - ADAM_B2 ** ADAM_STEP)

    def body(w_ref, g_ref, m_ref, v_ref, d_ref, mo_ref, vo_ref):
        gg = g_ref[...]
        mn = ADAM_B1 * m_ref[...] + (1.0 - ADAM_B1) * gg
        vn = ADAM_B2 * v_ref[...] + (1.0 - ADAM_B2) * (gg * gg)
        mo_ref[...] = mn
        vo_ref[...] = vn
        d_ref[...] = (-ADAM_LR) * ((mn * c1) / (jnp.sqrt(vn * c2) + ADAM_EPS) + ADAM_WD * w_ref[...])

    spec = pl.BlockSpec((tr, C), lambda i: (i, 0))
    sh = jax.ShapeDtypeStruct((R, C), F32)
    return pl.pallas_call(
        body, name=name, grid=(R // tr,),
        in_specs=[spec] * 4, out_specs=[spec] * 3, out_shape=[sh] * 3,
        compiler_params=_cp(("arbitrary",)),
    )(w, g, m, v)


def _coords():
    return lax.axis_index("x"), lax.axis_index("y"), lax.axis_index("c")


def _all_gather8(blk, name, vmem):
    m_per, n = blk.shape
    space = pltpu.VMEM if vmem else pl.ANY

    def body(x_ref, out_ref, send_sems, recv_sems, local_sem):
        x, y, c = _coords()
        me, sibling = (x, y, c), (x, y, 1 - c)
        chips = [(1 - x, y), (x, 1 - y), (1 - x, 1 - y)]

        def rows(px, py, pc):
            return out_ref.at[4 * px + 2 * py + pc]

        def copy(k, block, to, src=None):
            return pltpu.make_async_remote_copy(
                src_ref=rows(*block) if src is None else src, dst_ref=rows(*block),
                send_sem=send_sems.at[k], recv_sem=recv_sems.at[k], device_id=to, device_id_type=MESH)

        mine = pltpu.make_async_copy(x_ref, rows(*me), local_sem)
        mine.start()
        first = [copy(0, me, sibling, src=x_ref)]
        first += [copy(1 + j, me, (*chip, c), src=x_ref) for j, chip in enumerate(chips)]
        for cp in first:
            cp.start()
        passed = [copy(4 + j, (*chip, c), sibling) for j, chip in enumerate(chips)]
        for j, chip in enumerate(chips):
            copy(1 + j, (*chip, c), me).wait_recv()
            passed[j].start()
        copy(0, sibling, me).wait_recv()
        for j, chip in enumerate(chips):
            copy(4 + j, (*chip, 1 - c), me).wait_recv()
        for cp in first + passed:
            cp.wait_send()
        mine.wait()

    return pl.pallas_call(
        body, name=name,
        out_shape=jax.ShapeDtypeStruct((N_DEV, m_per, n), blk.dtype),
        in_specs=[pl.BlockSpec(memory_space=space)],
        out_specs=pl.BlockSpec(memory_space=space),
        scratch_shapes=[pltpu.SemaphoreType.DMA((7,)), pltpu.SemaphoreType.DMA((7,)), pltpu.SemaphoreType.DMA],
        compiler_params=pltpu.CompilerParams(vmem_limit_bytes=VMEM_BIG if vmem else None),
    )(blk)


def _run_comm(comm, name):
    n_in, n_out = len(comm["args"]), len(comm["out_shapes"])

    def body(*refs):
        parts = (refs[:n_in], refs[n_in:n_in + n_out], refs[n_in + n_out:])
        comm["start"](*parts)
        comm["finish"](*parts)

    anyspec = pl.BlockSpec(memory_space=pl.ANY)
    return pl.pallas_call(
        body, name=name, out_shape=comm["out_shapes"],
        in_specs=[anyspec] * n_in, out_specs=[anyspec] * n_out, scratch_shapes=comm["scratch"],
    )(*comm["args"])


def _hosted(comm):
    if comm is None:
        return [], [], [], []
    anyspec = pl.BlockSpec(memory_space=pl.ANY)
    return list(comm["args"]), [anyspec] * len(comm["args"]), list(comm["out_shapes"]), list(comm["scratch"])


def _ag_comm(wshards, pieces):
    n_piece = len(pieces)
    halves = [wshards[i].shape[len(lead)] // 2 for i, lead, _ in pieces]

    def plan(ins, outs, sems):
        send_sems, recv_sems, local_sems = sems
        x, y, c = _coords()
        me, sibling = (x, y, c), (x, y, 1 - c)
        chips = [(1 - x, y), (x, 1 - y), (1 - x, 1 - y)]

        def dsts(px, py, pc):
            s = 2 * px + py
            return [outs[p].at[s, pl.ds(pc * r2, r2)] if stacked else outs[p].at[pl.ds((2 * s + pc) * r2, r2)]
                    for p, ((_, _, stacked), r2) in enumerate(zip(pieces, halves))]

        srcs = [ins[i].at[(*lead, pl.ds(c * r2, r2))] for (i, lead, _), r2 in zip(pieces, halves)]

        def copies(k, block, to, own=False):
            d = dsts(*block)
            return [pltpu.make_async_remote_copy(
                src_ref=srcs[p] if own else d[p], dst_ref=d[p], send_sem=send_sems.at[k, p],
                recv_sem=recv_sems.at[k, p], device_id=to, device_id_type=MESH) for p in range(n_piece)]

        mine = [pltpu.make_async_copy(srcs[p], d, local_sems.at[p]) for p, d in enumerate(dsts(*me))]
        first = copies(0, me, sibling, own=True)
        for j, chip in enumerate(chips):
            first += copies(1 + j, me, (*chip, c), own=True)
        return me, sibling, chips, c, copies, mine, first

    def start(ins, outs, sems):
        *_, mine, first = plan(ins, outs, sems)
        for cp in mine + first:
            cp.start()

    def finish(ins, outs, sems):
        me, sibling, chips, c, copies, mine, first = plan(ins, outs, sems)
        passed = []
        for j, chip in enumerate(chips):
            for cp in copies(1 + j, (*chip, c), me):
                cp.wait_recv()
            fwd = copies(4 + j, (*chip, c), sibling)
            for cp in fwd:
                cp.start()
            passed += fwd
        for cp in copies(0, sibling, me):
            cp.wait_recv()
        for j, chip in enumerate(chips):
            for cp in copies(4 + j, (*chip, 1 - c), me):
                cp.wait_recv()
        for cp in first + passed:
            cp.wait_send()
        for cp in mine:
            cp.wait()

    out_shapes = []
    for (i, lead, stacked), r2 in zip(pieces, halves):
        cols = wshards[i].shape[-1]
        out_shapes.append(jax.ShapeDtypeStruct((N_SHARD, 2 * r2, cols) if stacked else (N_SHARD * 2 * r2, cols), BF16))
    return dict(
        args=list(wshards), out_shapes=out_shapes,
        scratch=[pltpu.SemaphoreType.DMA((7, n_piece)), pltpu.SemaphoreType.DMA((7, n_piece)),
                 pltpu.SemaphoreType.DMA((n_piece,))],
        start=start, finish=finish)


def _rs_to_chips_comm(hs):
    n = len(hs)

    def copies(h, r, sems):
        send_sems, recv_sems = sems
        x, y, c = _coords()
        chips = [(1 - x, y), (x, 1 - y), (1 - x, 1 - y)]
        return [pltpu.make_async_remote_copy(
            src_ref=h[p].at[2 * px + py], dst_ref=r[p].at[k], send_sem=send_sems.at[k, p], recv_sem=recv_sems.at[k, p],
            device_id=(px, py, c), device_id_type=MESH) for k, (px, py) in enumerate(chips) for p in range(n)]

    def start(h, r, sems):
        for cp in copies(h, r, sems):
            cp.start()

    def finish(h, r, sems):
        for cp in copies(h, r, sems):
            cp.wait()

    return dict(args=list(hs), out_shapes=[jax.ShapeDtypeStruct((3,) + h.shape[1:], h.dtype) for h in hs],
                scratch=[pltpu.SemaphoreType.DMA((3, n)), pltpu.SemaphoreType.DMA((3, n))],
                start=start, finish=finish)


def _rs_to_sibling(pieces, name):
    n = len(pieces)

    def body(*refs):
        g, r, (send_sems, recv_sems) = refs[:n], refs[n:2 * n], refs[2 * n:]
        x, y, c = _coords()
        cps = []
        for p in range(n):
            r2 = g[p].shape[1] // 2
            cps.append(pltpu.make_async_remote_copy(
                src_ref=g[p].at[:, pl.ds((1 - c) * r2, r2)], dst_ref=r[p], send_sem=send_sems.at[p],
                recv_sem=recv_sems.at[p], device_id=(x, y, 1 - c), device_id_type=MESH))
        for cp in cps:
            cp.start()
        for cp in cps:
            cp.wait()

    anyspec = pl.BlockSpec(memory_space=pl.ANY)
    return pl.pallas_call(
        body, name=name,
        out_shape=[jax.ShapeDtypeStruct((N_SHARD, g.shape[1] // 2, g.shape[2]), g.dtype) for g in pieces],
        in_specs=[anyspec] * n, out_specs=[anyspec] * n,
        scratch_shapes=[pltpu.SemaphoreType.DMA((n,)), pltpu.SemaphoreType.DMA((n,))],
    )(*pieces)


def _share_halves(tensors, places, r2s):
    n, no = len(places), len(tensors)

    def body(*refs):
        o, (send_sems, recv_sems) = refs[no:2 * no], refs[2 * no:]
        x, y, c = _coords()

        def half(p, hc):
            oi, lead = places[p]
            return o[oi].at[(*lead, pl.ds(hc * r2s[p], r2s[p]))]

        outs = [pltpu.make_async_remote_copy(
            src_ref=half(p, c), dst_ref=half(p, c), send_sem=send_sems.at[p], recv_sem=recv_sems.at[p],
            device_id=(x, y, 1 - c), device_id_type=MESH) for p in range(n)]
        for cp in outs:
            cp.start()
        for p in range(n):
            pltpu.make_async_remote_copy(
                src_ref=half(p, 1 - c), dst_ref=half(p, 1 - c), send_sem=send_sems.at[p], recv_sem=recv_sems.at[p],
                device_id=(x, y, 1 - c), device_id_type=MESH).wait_recv()
        for cp in outs:
            cp.wait_send()

    anyspec = pl.BlockSpec(memory_space=pl.ANY)
    return pl.pallas_call(
        body, name="share_halves",
        out_shape=[jax.ShapeDtypeStruct(t.shape, t.dtype) for t in tensors],
        in_specs=[anyspec] * no, out_specs=[anyspec] * no,
        input_output_aliases={i: i for i in range(no)},
        scratch_shapes=[pltpu.SemaphoreType.DMA((n,)), pltpu.SemaphoreType.DMA((n,))],
    )(*tensors)


def _add_rows(r2, cols, n_arrays):
    lanes = -(-cols // 128) * 128
    return _tile_div8(r2, max(16, (24 << 20) // (2 * n_arrays * lanes * 4)), mult=16)


def _add_sibling(pieces, recvs, cidx, name):
    n = len(pieces)
    _, R, C = pieces[0].shape
    r2 = R // 2
    tr = _add_rows(r2, C, 2 * n)
    nt = r2 // tr

    def body(c_ref, *refs):
        for p in range(n):
            refs[2 * n + p][...] = (refs[p][...] + refs[n + p][...].astype(F32)).astype(BF16)

    return pl.pallas_call(
        body, name=name,
        grid_spec=pltpu.PrefetchScalarGridSpec(
            num_scalar_prefetch=1, grid=(N_SHARD, nt),
            in_specs=[pl.BlockSpec((None, tr, C), lambda s, i, c_ref: (s, c_ref[0] * nt + i, 0))] * n
            + [pl.BlockSpec((None, tr, C), lambda s, i, c_ref: (s, i, 0))] * n,
            out_specs=[pl.BlockSpec((None, tr, C), lambda s, i, c_ref: (s, i, 0))] * n),
        out_shape=[jax.ShapeDtypeStruct((N_SHARD, r2, C), BF16)] * n,
        compiler_params=_cp(("arbitrary", "arbitrary"), VMEM_BIG),
    )(cidx, *pieces, *recvs)


def _add_chips_into(piece, recv_a, recv_b, sc, prev, shape, lead, name):
    _, R, C = piece.shape
    r2 = R // 2
    tr = _add_rows(r2, C, 4)
    nt = r2 // tr
    nl = len(lead)

    def body(sc_ref, p_ref, a_ref, b_ref, *rest):
        o_ref = rest[-1]
        acc = p_ref[...] + a_ref[...].astype(F32)
        for k in range(3):
            acc = acc + b_ref[k].astype(F32)
        o_ref[...] = acc

    in_specs = [pl.BlockSpec((None, tr, C), lambda i, sc_ref: (sc_ref[0], sc_ref[1] * nt + i, 0)),
                pl.BlockSpec((None, tr, C), lambda i, sc_ref: (sc_ref[0], i, 0)),
                pl.BlockSpec((3, tr, C), lambda i, sc_ref: (0, i, 0))]
    args = [sc, piece, recv_a, recv_b]
    aliases = {}
    if prev is not None:
        in_specs.append(pl.BlockSpec(memory_space=pl.ANY))
        args.append(prev)
        aliases = {4: 0}
    return pl.pallas_call(
        body, name=name,
        grid_spec=pltpu.PrefetchScalarGridSpec(
            num_scalar_prefetch=1, grid=(nt,), in_specs=in_specs,
            out_specs=pl.BlockSpec((None,) * nl + (tr, C), lambda i, sc_ref: (*lead, sc_ref[1] * nt + i, 0))),
        out_shape=jax.ShapeDtypeStruct(shape, F32),
        input_output_aliases=aliases,
        compiler_params=_cp(("arbitrary",), VMEM_BIG),
    )(*args)


def _pack_rows(parts, rows, dtype):
    flat = jnp.concatenate([p.reshape(-1).astype(dtype) for p in parts])
    return jnp.pad(flat, (0, rows * ROW - flat.shape[0])).reshape(rows, ROW)


def _unpack(flat, shapes):
    out, off = [], 0
    for sh in shapes:
        n = math.prod(sh)
        out.append(flat[off:off + n].reshape(sh))
        off += n
    return out


def _block_diag(w):
    eye = jnp.eye(LW // HD, dtype=w.dtype)
    return jnp.einsum("lhij,hg->lhigj", w, eye).reshape(w.shape[0], LW, LW)


def _diag_blocks(w):
    nbk = LW // HD
    w4 = w.reshape(nbk, HD, nbk, HD)
    return jnp.stack([w4[h, :, h, :] for h in range(nbk)])


def _rows8(rows, width):
    z = jnp.zeros((width,), F32)
    return jnp.stack(list(rows) + [z] * (8 - len(rows)))


def kernel(x, c, w_ada, b_ada, g_norm, w_ffn_up, w_ffn_down, w_in, b_fgate, conv_w, conv_b, w_rgate, b_rgate, w_igate, b_igate, lru_lambda, g_qk, g_mix_out, w_out, loss_target, m_w_ada, m_b_ada, m_g_norm, m_w_ffn_up, m_w_ffn_down, m_w_in, m_b_fgate, m_conv_w, m_conv_b, m_w_rgate, m_b_rgate, m_w_igate, m_b_igate, m_lru_lambda, m_g_qk, m_g_mix_out, m_w_out, v_w_ada, v_b_ada, v_g_norm, v_w_ffn_up, v_w_ffn_down, v_w_in, v_b_fgate, v_conv_w, v_conv_b, v_w_rgate, v_b_rgate, v_w_igate, v_b_igate, v_lru_lambda, v_g_qk, v_g_mix_out, v_w_out):
    B, S, _ = x.shape
    T = B * S
    xi, yi, ci = _coords()
    sidx = 2 * xi + yi
    didx = 4 * xi + 2 * yi + ci
    ada_cols = w_ada.shape[2]
    gn_cols = g_norm.shape[2]
    cw_cols = conv_w.shape[2]
    n_all = B * N_DEV

    blk1 = _pack_rows([c, jnp.pad(g_norm.reshape(-1), (0, 2 * ROW - g_norm.size)), conv_w], 8, F32)
    ag1 = _all_gather8(blk1, "ag_small_in", True)
    c_all = ag1[:, 0:B].reshape(n_all, D)
    chip_rows = ag1[0::2]
    g_norm_full = chip_rows[:, 2:4].reshape(N_SHARD, 2 * ROW)[:, :g_norm.size] \
        .reshape(N_SHARD, 2, 3, gn_cols).transpose(1, 2, 0, 3).reshape(2, 3, D)
    conv_w_full = chip_rows[:, 4].reshape(N_SHARD, 2, 4, cw_cols).transpose(1, 2, 0, 3).reshape(2, 4, LW)

    b_cols = lax.dynamic_slice(b_ada, (0, sidx * ada_cols), (2, ada_cols)).reshape(2, 1, ada_cols)
    mod_cols = _ada_fwd(c_all, w_ada, b_cols)
    mrows = (2 * n_all * ada_cols) // ROW
    ag2 = _all_gather8(mod_cols.reshape(mrows, ROW), "ag_mod", True)
    mod_sh = ag2[0::2].reshape(N_SHARD, 2, n_all, ada_cols)
    mod_me = lax.dynamic_slice(mod_sh, (0, 0, didx * B, 0), (N_SHARD, 2, B, ada_cols))
    mod_me = mod_me.transpose(1, 2, 0, 3).reshape(2, B, 3, 3, D)
    zrow = jnp.zeros((B, D), F32)
    mods = [[jnp.stack([mod_me[l, :, j, 0], 1.0 + mod_me[l, :, j, 1], 1.0 + mod_me[l, :, j, 2],
                        jnp.broadcast_to(g_norm_full[l, j], (B, D)), zrow, zrow, zrow, zrow], axis=1)
             for j in range(3)] for l in range(2)]

    wshards = (w_ffn_up.astype(BF16), w_ffn_down.astype(BF16), w_in.astype(BF16), w_out.astype(BF16))

    def ffn_pieces(l, j):
        return [(0, (l, j), True), (1, (l, j), False)]

    def mixer_pieces(l):
        return [(2, (l,), True), (3, (l,), False)]

    def ffn_weights(up, dn):
        return dict(up=up, dn=dn)

    def mixer_weights(g_in, g_out):
        return dict(inp=jnp.pad(g_in.transpose(1, 0, 2).reshape(D, N_IN), ((0, 0), (0, N_INP - N_IN))), out=g_out)

    wl = [dict(), dict()]
    wl[0][0] = ffn_weights(*_run_comm(_ag_comm(wshards, ffn_pieces(0, 0)), "ag_weights_0_0"))

    wr_d = _block_diag(w_rgate).astype(BF16)
    wi_d = _block_diag(w_igate).astype(BF16)
    cw8 = jnp.pad(conv_w_full, ((0, 0), (0, 4), (0, 0)))
    vp8 = jnp.stack([_rows8([conv_b[l], b_rgate[l], b_igate[l], lru_lambda[l]], LW) for l in range(2)])
    bfp = jnp.pad(b_fgate, ((0, 0), (0, 128 - NH)))[:, None, :] * jnp.ones((1, 8, 1), F32)
    gqk2 = jnp.tile(jnp.pad(g_qk, ((0, 0), (0, 6), (0, 0))), (1, 1, 2))
    gmix8 = jnp.pad(g_mix_out[:, None, :], ((0, 0), (0, 7), (0, 0)))

    x2 = x.reshape(T, D)
    tgt = loss_target.reshape(T, D)

    saved = []
    xc = x2
    for l in range(2):
        sv = {}
        sv["x0"] = xc
        w = wl[l]
        rest0 = _ag_comm(wshards, mixer_pieces(0) + ffn_pieces(0, 1)) if l == 0 else None
        xc, sv["g0"], sv["u0"], sv["f0"], got = _ffn_fwd(xc, mods[l][0], w[0]["up"], w[0]["dn"], l, 0, S, rest0)
        if l == 0:
            w["mix"] = mixer_weights(got[0], got[1])
            w[1] = ffn_weights(got[2], got[3])
        sv["x1"] = xc
        sv["h1"], proj = _mix_in_fwd(xc, mods[l][1], w["mix"]["inp"], l, S)
        sv["proj"] = proj
        sv["ylru"], sv["hl"] = _lru_fwd(proj, cw8, vp8, wr_d, wi_d, l, S)
        all1 = _ag_comm(wshards, ffn_pieces(1, 0) + mixer_pieces(1) + ffn_pieces(1, 1)) if l == 0 else None
        sv["osb"], sv["t1"], got = _sbq_fwd(proj, l, S, all1)
        if l == 0:
            wl[1][0] = ffn_weights(got[0], got[1])
            wl[1]["mix"] = mixer_weights(got[2], got[3])
            wl[1][1] = ffn_weights(got[4], got[5])
        cum = _fgate_fwd(proj, bfp, l, S)
        sv["ck"] = cum[:, :NH].reshape(B, S, NH).transpose(0, 2, 1).reshape(B, NH, S // TQ_(S), 1, TQ_(S))
        sv["ofx"], sv["nl"] = _foxq_fwd(proj, cum, sv["ck"], gqk2, l, S)
        xc, sv["y"], sv["mo"] = _mix_out_fwd(xc, sv["ylru"], sv["osb"], sv["ofx"], mods[l][1], gmix8, w["mix"]["out"], l, S)
        sv["x2"] = xc
        xc, sv["g2"], sv["u2"], sv["f2"], _ = _ffn_fwd(xc, mods[l][2], w[1]["up"], w[1]["dn"], l, 1, S)
        saved.append(sv)

    dxc, lpart = _loss_head(xc, tgt, S)
    loss = lax.psum(lpart[0, 0], ("x", "y", "c"))

    tf = wl[0][0]["up"].shape[-1]
    g_up_l = [[None, None], [None, None]]
    g_dn_l = [[None, None], [None, None]]
    g_in_l, g_out_l = [None, None], [None, None]
    dmods = [[None] * 3 for _ in range(2)]
    small = [dict() for _ in range(2)]
    cvec = jnp.reshape(ci, (1,)).astype(jnp.int32)
    scvec = jnp.stack([sidx, ci]).astype(jnp.int32)

    def ffn_groups(l, j):
        return [(0, "up", [g_up_l[l][j]], [(l, j)]), (1, "dn", [g_dn_l[l][j]], [(l, j)])]

    def mixer_groups(l):
        return [(2, "in", [g_in_l[l]], [(l,)]), (3, "out", [g_out_l[l]], [(l,)])]

    def rs_sibling_phase(groups, tag):
        recv_a = _rs_to_sibling([pb for _, _, ps, _ in groups for _, pb in ps], f"rs_to_sibling_{tag}")
        hs, off = [], 0
        for _, gname, ps, leads in groups:
            hs += _add_sibling([pf for pf, _ in ps], recv_a[off:off + len(ps)], cvec,
                               f"rs_add_sibling_{gname}_{'_'.join(map(str, leads[0]))}")
            off += len(ps)
        return groups, recv_a, hs

    def ffn_back(l, j, xin, dy, sv, sub, comm=None):
        dx, dmod, wacc, hb, dfb, ab, dgub, got = _ffn_bwd(
            xin, dy, mods[l][sub], sv[f"f{sub}"], sv[f"g{sub}"], sv[f"u{sub}"],
            wl[l][j]["up"], wl[l][j]["dn"], l, j, S, comm)
        g_up_l[l][j] = _mm_tn(hb, dgub, f"dw_up_{l}_{j}", tnb=tf, split_n=True, with_bf16=True)
        g_dn_l[l][j] = tuple(g.reshape(N_SHARD, -1, D)
                             for g in _mm_tn(ab, dfb, f"dw_dn_{l}_{j}", tma=tf, with_bf16=True))
        dmods[l][sub] = dmod
        small[l][f"gn{sub}"] = wacc[0]
        return dx, got

    batches = []
    for l in (1, 0):
        sv = saved[l]
        dxc, _ = ffn_back(l, 1, sv["x2"], dxc, sv, 2)
        dyl, dsb, dfx, dmo, dmod1, wacc_mo = _mix_out_bwd(
            dxc, sv["ylru"], sv["osb"], sv["ofx"], sv["mo"], mods[l][1], gmix8, wl[l]["mix"]["out"], l, S)
        small[l]["gmix"] = wacc_mo[0]
        g_out_l[l] = tuple(g.reshape(N_SHARD, -1, D) for g in _mm_tn(sv["y"], dmo, f"dw_out_{l}", with_bf16=True))
        dsq, dsk, dsv, got = _sbq_bwd(sv["proj"], dsb, sv["t1"], l, S,
                                       _rs_to_chips_comm(rs1[2]) if l == 0 else None)
        if l == 0:
            batches.append((rs1[0], rs1[1], got))
        dfq, dfk, dfv, dck, wacc_fx = _foxq_bwd(sv["proj"], dfx, sv["nl"], sv["ofx"], sv["ck"], gqk2, l, S)
        small[l]["gqk"] = wacc_fx[0:2, :HD] + wacc_fx[0:2, HD:]
        dcum = dck[:, :, :, 0, :].reshape(B, NH, S).transpose(0, 2, 1).reshape(T, NH)
        dff_, wacc_fg = _fgate_bwd(jnp.pad(dcum, ((0, 0), (0, 128 - NH))), sv["proj"], bfp, l, S)
        small[l]["bf"] = wacc_fg[0, :NH]
        dlx, dlg, dpr, dpi, ub, wacc_lru = _lru_bwd(dyl, sv["proj"], sv["hl"], cw8, vp8, wr_d, wi_d, l, S)
        small[l]["lru"] = wacc_lru
        small[l]["wr"] = _diag_blocks(_mm_tn(ub, dpr, f"dw_rgate_{l}"))
        small[l]["wi"] = _diag_blocks(_mm_tn(ub, dpi, f"dw_igate_{l}"))
        dproj = jnp.concatenate(
            [dlx, dlg, dsq, dsk, dsv, dfq, dfk, dfv, dff_], axis=1)
        g_in = _mm_tn(sv["h1"], dproj, f"dw_in_{l}", tnb=N_INP // 3)[:, :N_IN]
        g_in = g_in.reshape(D, N_SHARD, -1).transpose(1, 0, 2)
        g_in_l[l] = (g_in, g_in.astype(BF16))
        dxc, dmod_in, wacc_in = _mix_in_bwd(sv["x1"], dxc, mods[l][1], dproj, wl[l]["mix"]["inp"], l, S)
        dmods[l][1] = dmod_in + dmod1
        small[l]["gn1"] = wacc_in[0]
        if l == 1:
            dxc, _ = ffn_back(l, 0, sv["x0"], dxc, sv, 0)
            rs1 = rs_sibling_phase(ffn_groups(1, 0) + mixer_groups(1) + ffn_groups(1, 1), "1")
        else:
            late = rs_sibling_phase(mixer_groups(0) + ffn_groups(0, 1), "0_late")
            dxc, got = ffn_back(l, 0, sv["x0"], dxc, sv, 0, _rs_to_chips_comm(late[2]))
            batches.append((late[0], late[1], got))
    grad_x = dxc.reshape(B, S, D)

    dmod_loc = jnp.stack([jnp.stack([dmods[l][j][:, 0:3, :] for j in range(3)], axis=1) for l in range(2)])
    drows = 2 * B * 9
    blk3 = _pack_rows([dmod_loc], -(-drows // 8) * 8, F32)
    ag3 = _all_gather8(blk3, "ag_dmod", True)
    dmod_all = ag3[:, :drows].reshape(N_DEV, 2, B, 9 * D).transpose(1, 0, 2, 3).reshape(2, n_all, 9 * D)
    dmod_mine = lax.dynamic_slice(dmod_all, (0, 0, sidx * ada_cols), (2, n_all, ada_cols))
    grad_w_ada = _ada_bwd(c_all, dmod_mine)
    dmod_rows = jnp.pad(dmod_all.transpose(1, 0, 2).reshape(n_all, 2 * 9, D), ((0, 0), (0, 6), (0, 0)))
    grad_b_ada = _sum_lead(dmod_rows, "grad_b_ada")[:2 * 9].reshape(2, 9 * D)

    sm_parts = [
        jnp.stack([small[l]["bf"] for l in range(2)]),
        jnp.stack([small[l]["lru"][4] for l in range(2)]),
        jnp.stack([small[l]["wr"] for l in range(2)]),
        jnp.stack([small[l]["lru"][5] for l in range(2)]),
        jnp.stack([small[l]["wi"] for l in range(2)]),
        jnp.stack([small[l]["lru"][6] for l in range(2)]),
        jnp.stack([small[l]["lru"][7] for l in range(2)]),
        jnp.stack([small[l]["gqk"] for l in range(2)]),
        jnp.stack([small[l]["gmix"] for l in range(2)]),
        jnp.stack([jnp.stack([small[l][f"gn{j}"] for j in range(3)]) for l in range(2)]),
        jnp.stack([small[l]["lru"][0:4] for l in range(2)]),
    ]
    sm_shapes = [p.shape for p in sm_parts]
    sm_rows = -(-sum(p.size for p in sm_parts) // (8 * ROW)) * 8
    ag4 = _all_gather8(_pack_rows(sm_parts, sm_rows, F32), "ag_small_grads", True)
    sm_sum = _sum_lead(ag4, "sum_small_grads").reshape(-1)
    (g_bf, g_cb, g_wr, g_br, g_wi, g_bi, g_lam, g_gqk, g_gmix, g_gn_full, g_cw_full) = _unpack(sm_sum, sm_shapes)
    g_gn = lax.dynamic_slice(g_gn_full, (0, 0, sidx * gn_cols), (2, 3, gn_cols))
    g_cw = lax.dynamic_slice(g_cw_full, (0, 0, sidx * cw_cols), (2, 4, cw_cols))

    last = rs_sibling_phase(ffn_groups(0, 0), "0_first")
    batches.append((last[0], last[1], _run_comm(_rs_to_chips_comm(last[2]), "rs_to_chips_0_first")))
    shapes4 = [w_ffn_up.shape, w_ffn_down.shape, w_in.shape, w_out.shape]
    tensors, places, r2s = [None] * 4, [], []
    for groups, recv_a, recv_b in batches:
        k = 0
        for gi, gname, ps, leads in groups:
            for (pf, _), lead in zip(ps, leads):
                tensors[gi] = _add_chips_into(pf, recv_a[k], recv_b[k], scvec, tensors[gi], shapes4[gi], lead,
                                              f"rs_add_chips_{gname}_{'_'.join(map(str, lead))}")
                places.append((gi, lead))
                r2s.append(pf.shape[1] // 2)
                k += 1
    gw_up, gw_dn, gw_in, gw_out = _share_halves(tensors, places, r2s)

    def upd(w, g, m, v, name):
        sh = w.shape
        two = (w.size // sh[-1], sh[-1])
        dlt, mn, vn = _adamw(w.reshape(two), g.reshape(two), m.reshape(two), v.reshape(two), name)
        return dlt.reshape(sh), mn.reshape(sh), vn.reshape(sh)

    big = {
        "w_ada": (w_ada, grad_w_ada, m_w_ada, v_w_ada),
        "w_ffn_up": (w_ffn_up, gw_up, m_w_ffn_up, v_w_ffn_up),
        "w_ffn_down": (w_ffn_down, gw_dn, m_w_ffn_down, v_w_ffn_down),
        "w_in": (w_in, gw_in, m_w_in, v_w_in),
        "w_out": (w_out, gw_out, m_w_out, v_w_out),
    }
    res = {n: (t[1],) + upd(*t, f"adamw_{n}") for n, t in big.items()}

    smalls = {
        "b_ada": (b_ada, grad_b_ada, m_b_ada, v_b_ada),
        "g_norm": (g_norm, g_gn, m_g_norm, v_g_norm),
        "b_fgate": (b_fgate, g_bf, m_b_fgate, v_b_fgate),
        "conv_w": (conv_w, g_cw, m_conv_w, v_conv_w),
        "conv_b": (conv_b, g_cb, m_conv_b, v_conv_b),
        "w_rgate": (w_rgate, g_wr, m_w_rgate, v_w_rgate),
        "b_rgate": (b_rgate, g_br, m_b_rgate, v_b_rgate),
        "w_igate": (w_igate, g_wi, m_w_igate, v_w_igate),
        "b_igate": (b_igate, g_bi, m_b_igate, v_b_igate),
        "lru_lambda": (lru_lambda, g_lam, m_lru_lambda, v_lru_lambda),
        "g_qk": (g_qk, g_gqk, m_g_qk, v_g_qk),
        "g_mix_out": (g_mix_out, g_gmix, m_g_mix_out, v_g_mix_out),
    }
    names = list(smalls)
    shapes = [smalls[n][0].shape for n in names]
    prow = -(-sum(math.prod(s) for s in shapes) // (8 * ROW)) * 8
    packed = [_pack_rows([smalls[n][i].reshape(shapes[k]) for k, n in enumerate(names)], prow, F32) for i in range(4)]
    outs = _adamw(packed[0], packed[1], packed[2], packed[3], "adamw_small")
    un = [_unpack(o.reshape(-1), shapes) for o in outs]
    for k, n in enumerate(names):
        res[n] = (smalls[n][1].reshape(shapes[k]), un[0][k], un[1][k], un[2][k])

    order = ["w_ada", "b_ada", "g_norm", "w_ffn_up", "w_ffn_down", "w_in", "b_fgate", "conv_w", "conv_b",
             "w_rgate", "b_rgate", "w_igate", "b_igate", "lru_lambda", "g_qk", "g_mix_out", "w_out"]
    return (loss, grad_x, *[res[n][0] for n in order], *[res[n][1] for n in order],
            *[res[n][2] for n in order], *[res[n][3] for n in order])


def TQ_(S):
    return min(TQ, S)
```

```python
import math

import jax
import jax.numpy as jnp
from jax import lax
from jax.experimental import pallas as pl
from jax.experimental.pallas import tpu as pltpu

F32 = jnp.float32
BF16 = jnp.bfloat16
MESH = pl.DeviceIdType.MESH

D = 1024
HD = 64
LW = 512
NH = 4
AW = NH * HD
N_IN = 2564
N_INP = 2688
F_BLK = 2560 // 128
EPS = 1e-6
LRU_C = 8.0
SCALE = HD ** -0.5
NEG = -1e30
TQ = 256

ADAM_LR, ADAM_B1, ADAM_B2, ADAM_EPS, ADAM_WD, ADAM_STEP = 0.001, 0.9, 0.999, 1e-08, 0.01, 10

VMEM_BIG = 56 * 1024 * 1024
N_DEV = 8
N_SHARD = 4
ROW = 1024


def _cp(sem, vmem=None):
    return pltpu.CompilerParams(dimension_semantics=sem, vmem_limit_bytes=vmem)


def _dot(a, b):
    return jnp.dot(a, b, preferred_element_type=F32)


def _dot_nt(a, b):
    return lax.dot_general(a, b, (((1,), (1,)), ((), ())), preferred_element_type=F32)


def _dot_tn(a, b):
    return lax.dot_general(a, b, (((0,), (0,)), ((), ())), preferred_element_type=F32)


def _log1p(e):
    small = e * (1.0 - e * (0.5 - e * (1.0 / 3.0 - e * 0.25)))
    return jnp.where(e < 0.01, small, jnp.log(1.0 + e))


def _expm1_neg(x):
    small = x * (1.0 + x * 0.5 * (1.0 + x * (1.0 / 3.0) * (1.0 + x * 0.25 * (1.0 + x * 0.2))))
    return jnp.where(x > -0.05, small, jnp.exp(x) - 1.0)


def _sigmoid(x):
    return 1.0 / (1.0 + jnp.exp(-x))


_GELU_C = math.sqrt(2.0 / math.pi)


def _gelu_and_grad(x):
    x2 = x * x
    th = jnp.tanh(_GELU_C * (x + 0.044715 * x * x2))
    g = 0.5 * x * (1.0 + th)
    dg = 0.5 * (1.0 + th) + 0.5 * x * (1.0 - th * th) * _GELU_C * (1.0 + 3.0 * 0.044715 * x2)
    return g, dg


def _rms_rows(x):
    rstd = lax.rsqrt(jnp.mean(x * x, axis=-1, keepdims=True) + EPS)
    return x * rstd, rstd


def _rms_bwd(xn, rstd, dyn):
    return rstd * (dyn - xn * jnp.mean(dyn * xn, axis=-1, keepdims=True))


def _colsum(x):
    return jnp.sum(x, axis=0, keepdims=True)


def _rowsum(x):
    return jnp.sum(x, axis=1, keepdims=True)


def _split3(x):
    hi = x.astype(BF16)
    r = x - hi.astype(F32)
    mid = r.astype(BF16)
    lo = (r - mid.astype(F32)).astype(BF16)
    return hi, mid, lo


def _cumsum_mm(x, ones_tri, parts=3):
    ps = _split3(x)[:parts]
    acc = _dot(ps[0], ones_tri)
    for p in ps[1:]:
        acc = acc + _dot(p, ones_tri)
    return acc


def _tri(n, kind):
    r = lax.broadcasted_iota(jnp.int32, (n, n), 0)
    c = lax.broadcasted_iota(jnp.int32, (n, n), 1)
    m = {"row_gt_col": r > c, "row_le_col": r <= c, "row_lt_col": r < c}[kind]
    return jnp.where(m, 1.0, 0.0).astype(BF16)


def _normmod(x, mod_ref):
    xn, rstd = _rms_rows(x)
    h = xn * mod_ref[3:4, :] * mod_ref[1:2, :] + mod_ref[0:1, :]
    return h, xn, rstd


def _normmod_bwd(dh, xn, rstd, mod_ref, dmod_ref, wacc_ref):
    gn = mod_ref[3:4, :]
    sc = mod_ref[1:2, :]
    dmod_ref[0:1, :] += _colsum(dh)
    t = _colsum(dh * xn)
    dmod_ref[1:2, :] += t * gn
    wacc_ref[0:1, :] += t * sc
    return _rms_bwd(xn, rstd, dh * (gn * sc))


def _tile(n, want):
    t = min(n, want)
    while n % t:
        t //= 2
    return t


def _tile_div8(n, cap, mult=8):
    best = mult
    for t in range(mult, min(n, cap) + 1, mult):
        if n % t == 0:
            best = t
    assert n % best == 0
    return best


def _ffn_fwd(x, mod, wup, wdn, l, j, S, comm=None):
    T = x.shape[0]
    tf = wup.shape[-1]
    nk = 2
    tm = _tile(S, 512)
    tpb = S // tm
    nt = T // tm
    c_args, c_specs, c_outs, c_scr = _hosted(comm)
    n_ci, n_co = len(c_args), len(c_outs)

    def body(*refs):
        x_ref, mod_ref, wg_ref, wu_ref, wd_ref = refs[:5]
        c_in = refs[5:5 + n_ci]
        xo_ref, g_ref, u_ref, f_ref = refs[5 + n_ci:9 + n_ci]
        c_out = refs[9 + n_ci:9 + n_ci + n_co]
        h_sc, acc_sc = refs[9 + n_ci + n_co:11 + n_ci + n_co]
        c_sems = refs[11 + n_ci + n_co:]
        i = pl.program_id(0)
        k = pl.program_id(1)
        if comm is not None:
            @pl.when((i == 0) & (k == 0))
            def _():
                comm["start"](c_in, c_out, c_sems)

        @pl.when(k == 0)
        def _():
            h, _, _ = _normmod(x_ref[...], mod_ref)
            h_sc[...] = h.astype(BF16)
            acc_sc[...] = jnp.zeros_like(acc_sc)

        h = h_sc[...]
        g = _dot(h, wg_ref[...])
        u = _dot(h, wu_ref[...])
        g_ref[...] = g.astype(BF16)
        u_ref[...] = u.astype(BF16)
        a = (g * _sigmoid(g)) * u
        acc_sc[...] += _dot(a.astype(BF16), wd_ref[...])

        @pl.when(k == nk - 1)
        def _():
            f = acc_sc[...]
            f_ref[...] = f.astype(BF16)
            xo_ref[...] = x_ref[...] + (0.5 * mod_ref[2:3, :]) * f

        if comm is not None:
            @pl.when((i == nt - 1) & (k == nk - 1))
            def _():
                comm["finish"](c_in, c_out, c_sems)

    anyspec = pl.BlockSpec(memory_space=pl.ANY)
    out = pl.pallas_call(
        body, name=f"ffn_fwd_{l}_{j}",
        grid=(nt, nk),
        in_specs=[
            pl.BlockSpec((tm, D), lambda i, k: (i, 0)),
            pl.BlockSpec((None, 8, D), lambda i, k: (i // tpb, 0, 0)),
            pl.BlockSpec((None, D, tf), lambda i, k: (k, 0, 0)),
            pl.BlockSpec((None, D, tf), lambda i, k: (nk + k, 0, 0)),
            pl.BlockSpec((tf, D), lambda i, k: (k, 0)),
        ] + c_specs,
        out_specs=[
            pl.BlockSpec((tm, D), lambda i, k: (i, 0)),
            pl.BlockSpec((tm, tf), lambda i, k: (i, k)),
            pl.BlockSpec((tm, tf), lambda i, k: (i, k)),
            pl.BlockSpec((tm, D), lambda i, k: (i, 0)),
        ] + [anyspec] * n_co,
        out_shape=[
            jax.ShapeDtypeStruct((T, D), F32),
            jax.ShapeDtypeStruct((T, nk * tf), BF16),
            jax.ShapeDtypeStruct((T, nk * tf), BF16),
            jax.ShapeDtypeStruct((T, D), BF16),
        ] + c_outs,
        scratch_shapes=[pltpu.VMEM((tm, D), BF16), pltpu.VMEM((tm, D), F32)] + c_scr,
        compiler_params=_cp(("arbitrary", "arbitrary"), VMEM_BIG),
    )(x, mod, wup, wup, wdn, *c_args)
    return out[0], out[1], out[2], out[3], list(out[4:])


def _ffn_bwd(x, dy, mod, f, g, u, wup, wdn, l, j, S, comm=None):
    T = x.shape[0]
    tf = wup.shape[-1]
    nk = 2
    tm = _tile(S, 256)
    tpb = S // tm
    nt = T // tm
    c_args, c_specs, c_outs, c_scr = _hosted(comm)
    n_ci, n_co = len(c_args), len(c_outs)

    def body(*refs):
        x_ref, dy_ref, mod_ref, f_ref, g_ref, u_ref, wup_ref, wd_ref = refs[:8]
        c_in = refs[8:8 + n_ci]
        dx_ref, dmod_ref, wacc_ref, h_ref, df_ref, a_ref, dgu_ref = refs[8 + n_ci:15 + n_ci]
        c_out = refs[15 + n_ci:15 + n_ci + n_co]
        c_sems = refs[15 + n_ci + n_co:]
        i = pl.program_id(0)

        @pl.when(i == 0)
        def _():
            wacc_ref[...] = jnp.zeros_like(wacc_ref)
            if comm is not None:
                comm["start"](c_in, c_out, c_sems)

        @pl.when(i % tpb == 0)
        def _():
            dmod_ref[...] = jnp.zeros_like(dmod_ref)

        dy_ = dy_ref[...]
        h, xn, rstd = _normmod(x_ref[...], mod_ref)
        h_ref[...] = h.astype(BF16)
        dfb = ((0.5 * mod_ref[2:3, :]) * dy_).astype(BF16)
        df_ref[...] = dfb
        dmod_ref[2:3, :] += _colsum(0.5 * f_ref[...].astype(F32) * dy_)
        dh = None
        for k in range(nk):
            cols = slice(k * tf, (k + 1) * tf)
            da = _dot_nt(dfb, wd_ref[cols, :])
            gg = g_ref[:, cols].astype(F32)
            uu = u_ref[:, cols].astype(F32)
            sig = _sigmoid(gg)
            s = gg * sig
            a_ref[:, cols] = (s * uu).astype(BF16)
            du = (da * s).astype(BF16)
            dg = (da * uu * (sig * (1.0 + gg * (1.0 - sig)))).astype(BF16)
            dgu_ref[0, :, cols] = dg
            dgu_ref[1, :, cols] = du
            part = _dot_nt(dg, wup_ref[k]) + _dot_nt(du, wup_ref[nk + k])
            dh = part if dh is None else dh + part
        dx_ref[...] = dy_ + _normmod_bwd(dh, xn, rstd, mod_ref, dmod_ref, wacc_ref)

        if comm is not None:
            @pl.when(i == nt - 1)
            def _():
                comm["finish"](c_in, c_out, c_sems)

    once = pl.Buffered(1)
    anyspec = pl.BlockSpec(memory_space=pl.ANY)
    out = pl.pallas_call(
        body, name=f"ffn_bwd_{l}_{j}",
        grid=(nt,),
        in_specs=[
            pl.BlockSpec((tm, D), lambda i: (i, 0)),
            pl.BlockSpec((tm, D), lambda i: (i, 0)),
            pl.BlockSpec((None, 8, D), lambda i: (i // tpb, 0, 0)),
            pl.BlockSpec((tm, D), lambda i: (i, 0)),
            pl.BlockSpec((tm, nk * tf), lambda i: (i, 0)),
            pl.BlockSpec((tm, nk * tf), lambda i: (i, 0)),
            pl.BlockSpec((2 * nk, D, tf), lambda i: (0, 0, 0), pipeline_mode=once),
            pl.BlockSpec((nk * tf, D), lambda i: (0, 0), pipeline_mode=once),
        ] + c_specs,
        out_specs=[
            pl.BlockSpec((tm, D), lambda i: (i, 0)),
            pl.BlockSpec((None, 8, D), lambda i: (i // tpb, 0, 0)),
            pl.BlockSpec((8, D), lambda i: (0, 0)),
            pl.BlockSpec((tm, D), lambda i: (i, 0)),
            pl.BlockSpec((tm, D), lambda i: (i, 0)),
            pl.BlockSpec((tm, nk * tf), lambda i: (i, 0)),
            pl.BlockSpec((2, tm, nk * tf), lambda i: (0, i, 0)),
        ] + [anyspec] * n_co,
        out_shape=[
            jax.ShapeDtypeStruct((T, D), F32),
            jax.ShapeDtypeStruct((T // S, 8, D), F32),
            jax.ShapeDtypeStruct((8, D), F32),
            jax.ShapeDtypeStruct((T, D), BF16),
            jax.ShapeDtypeStruct((T, D), BF16),
            jax.ShapeDtypeStruct((T, nk * tf), BF16),
            jax.ShapeDtypeStruct((2, T, nk * tf), BF16),
        ] + c_outs,
        scratch_shapes=c_scr,
        compiler_params=_cp(("arbitrary",), VMEM_BIG),
    )(x, dy, mod, f, g, u, wup, wdn, *c_args)
    return tuple(out[:7]) + (list(out[7:]),)


def _mm_tn(a, b, name, tma=None, tnb=None, split_n=False, with_bf16=False):
    T, M = a.shape
    b3 = b if b.ndim == 3 else b[None]
    nb, _, N = b3.shape
    tma = tma or M
    tnb = tnb or N
    npb = N // tnb
    tt = _tile(T, 1024)
    nt = T // tt

    def body(a_ref, b_ref, o_ref, *ob_ref):
        @pl.when(pl.program_id(2) == 0)
        def _():
            o_ref[...] = jnp.zeros_like(o_ref)

        o_ref[...] += _dot_tn(a_ref[...], b_ref[...])

        if with_bf16:
            @pl.when(pl.program_id(2) == nt - 1)
            def _():
                ob_ref[0][...] = o_ref[...].astype(BF16)

    if split_n:
        shape = (nb * npb, M, tnb)
        out_spec = pl.BlockSpec((None, tma, tnb), lambda m, n, t: (n, m, 0))
    else:
        assert nb == 1
        shape = (M, N)
        out_spec = pl.BlockSpec((tma, tnb), lambda m, n, t: (m, n))
    dts = (F32, BF16) if with_bf16 else (F32,)
    out = pl.pallas_call(
        body, name=name,
        grid=(M // tma, nb * npb, nt),
        in_specs=[pl.BlockSpec((tt, tma), lambda m, n, t: (t, m)),
                  pl.BlockSpec((None, tt, tnb), lambda m, n, t: (n // npb, t, n % npb))],
        out_specs=[out_spec] * len(dts),
        out_shape=[jax.ShapeDtypeStruct(shape, dt) for dt in dts],
        compiler_params=_cp(("arbitrary", "arbitrary", "arbitrary"), VMEM_BIG),
    )(a, b3)
    return tuple(out) if with_bf16 else out[0]


def _mix_in_fwd(x, mod, winp, l, S):
    T = x.shape[0]
    tm = _tile(S, 512)
    tpb = S // tm

    def body(x_ref, mod_ref, w_ref, h_ref, p_ref):
        h, _, _ = _normmod(x_ref[...], mod_ref)
        hb = h.astype(BF16)
        h_ref[...] = hb
        p_ref[...] = _dot(hb, w_ref[...])

    return pl.pallas_call(
        body, name=f"mix_in_fwd_{l}",
        grid=(T // tm,),
        in_specs=[pl.BlockSpec((tm, D), lambda i: (i, 0)),
                  pl.BlockSpec((None, 8, D), lambda i: (i // tpb, 0, 0)),
                  pl.BlockSpec((D, N_INP), lambda i: (0, 0))],
        out_specs=[pl.BlockSpec((tm, D), lambda i: (i, 0)),
                   pl.BlockSpec((tm, N_INP), lambda i: (i, 0))],
        out_shape=[jax.ShapeDtypeStruct((T, D), BF16), jax.ShapeDtypeStruct((T, N_INP), F32)],
        compiler_params=_cp(("arbitrary",), VMEM_BIG),
    )(x, mod, winp)


def _mix_in_bwd(x, dres, mod, dproj, winp, l, S):
    T = x.shape[0]
    tm = _tile(S, 512)
    tpb = S // tm

    def body(x_ref, dr_ref, mod_ref, dp_ref, w_ref, dx_ref, dmod_ref, wacc_ref):
        i = pl.program_id(0)

        @pl.when(i == 0)
        def _():
            wacc_ref[...] = jnp.zeros_like(wacc_ref)

        @pl.when(i % tpb == 0)
        def _():
            dmod_ref[...] = jnp.zeros_like(dmod_ref)

        dh = _dot_nt(dp_ref[...], w_ref[...])
        _, xn, rstd = _normmod(x_ref[...], mod_ref)
        dx_ref[...] = dr_ref[...] + _normmod_bwd(dh, xn, rstd, mod_ref, dmod_ref, wacc_ref)

    return pl.pallas_call(
        body, name=f"mix_in_bwd_{l}",
        grid=(T // tm,),
        in_specs=[pl.BlockSpec((tm, D), lambda i: (i, 0)),
                  pl.BlockSpec((tm, D), lambda i: (i, 0)),
                  pl.BlockSpec((None, 8, D), lambda i: (i // tpb, 0, 0)),
                  pl.BlockSpec((tm, N_INP), lambda i: (i, 0)),
                  pl.BlockSpec((D, N_INP), lambda i: (0, 0))],
        out_specs=[pl.BlockSpec((tm, D), lambda i: (i, 0)),
                   pl.BlockSpec((None, 8, D), lambda i: (i // tpb, 0, 0)),
                   pl.BlockSpec((8, D), lambda i: (0, 0))],
        out_shape=[jax.ShapeDtypeStruct((T, D), F32),
                   jax.ShapeDtypeStruct((T // S, 8, D), F32),
                   jax.ShapeDtypeStruct((8, D), F32)],
        compiler_params=_cp(("arbitrary",), VMEM_BIG),
    )(x, dres, mod, dproj, winp)


_GROUPS = ((0, LW), (LW, LW + AW), (LW + AW, D))


def _mix_out_fwd(x, ylru, osb, ofox, mod, gmix, wout, l, S):
    T = x.shape[0]
    tm = _tile(S, 512)
    tpb = S // tm

    def body(x_ref, yl_ref, sb_ref, fx_ref, mod_ref, gm_ref, w_ref, xo_ref, y_ref, mo_ref):
        for src, (lo, hi) in zip((yl_ref, sb_ref, fx_ref), _GROUPS):
            vn, _ = _rms_rows(src[...])
            y_ref[:, lo:hi] = (vn * gm_ref[0:1, lo:hi]).astype(BF16)
        mo = _dot(y_ref[...], w_ref[...])
        mo_ref[...] = mo.astype(BF16)
        xo_ref[...] = x_ref[...] + mod_ref[2:3, :] * mo

    return pl.pallas_call(
        body, name=f"mix_out_fwd_{l}",
        grid=(T // tm,),
        in_specs=[pl.BlockSpec((tm, D), lambda i: (i, 0)),
                  pl.BlockSpec((tm, LW), lambda i: (i, 0)),
                  pl.BlockSpec((tm, AW), lambda i: (i, 0)),
                  pl.BlockSpec((tm, AW), lambda i: (i, 0)),
                  pl.BlockSpec((None, 8, D), lambda i: (i // tpb, 0, 0)),
                  pl.BlockSpec((None, 8, D), lambda i: (l, 0, 0)),
                  pl.BlockSpec((D, D), lambda i: (0, 0))],
        out_specs=[pl.BlockSpec((tm, D), lambda i: (i, 0)),
                   pl.BlockSpec((tm, D), lambda i: (i, 0)),
                   pl.BlockSpec((tm, D), lambda i: (i, 0))],
        out_shape=[jax.ShapeDtypeStruct((T, D), F32),
                   jax.ShapeDtypeStruct((T, D), BF16),
                   jax.ShapeDtypeStruct((T, D), BF16)],
        compiler_params=_cp(("arbitrary",), VMEM_BIG),
    )(x, ylru, osb, ofox, mod, gmix, wout)


def _mix_out_bwd(dx2, ylru, osb, ofox, mo, mod, gmix, wout, l, S):
    T = dx2.shape[0]
    tm = _tile(S, 512)
    tpb = S // tm

    def body(dx_ref, yl_ref, sb_ref, fx_ref, mo_ref, mod_ref, gm_ref, w_ref,
             dyl_ref, dsb_ref, dfx_ref, dmo_ref, dmod_ref, wacc_ref):
        i = pl.program_id(0)

        @pl.when(i == 0)
        def _():
            wacc_ref[...] = jnp.zeros_like(wacc_ref)

        @pl.when(i % tpb == 0)
        def _():
            dmod_ref[...] = jnp.zeros_like(dmod_ref)

        dx = dx_ref[...]
        dmod_ref[2:3, :] += _colsum(mo_ref[...].astype(F32) * dx)
        dmo = (mod_ref[2:3, :] * dx).astype(BF16)
        dmo_ref[...] = dmo
        dy = _dot_nt(dmo, w_ref[...])
        for src, dst, (lo, hi) in zip((yl_ref, sb_ref, fx_ref), (dyl_ref, dsb_ref, dfx_ref), _GROUPS):
            vn, rstd = _rms_rows(src[...])
            dyg = dy[:, lo:hi]
            wacc_ref[0:1, lo:hi] += _colsum(dyg * vn)
            dst[...] = _rms_bwd(vn, rstd, dyg * gm_ref[0:1, lo:hi])

    return pl.pallas_call(
        body, name=f"mix_out_bwd_{l}",
        grid=(T // tm,),
        in_specs=[pl.BlockSpec((tm, D), lambda i: (i, 0)),
                  pl.BlockSpec((tm, LW), lambda i: (i, 0)),
                  pl.BlockSpec((tm, AW), lambda i: (i, 0)),
                  pl.BlockSpec((tm, AW), lambda i: (i, 0)),
                  pl.BlockSpec((tm, D), lambda i: (i, 0)),
                  pl.BlockSpec((None, 8, D), lambda i: (i // tpb, 0, 0)),
                  pl.BlockSpec((None, 8, D), lambda i: (l, 0, 0)),
                  pl.BlockSpec((D, D), lambda i: (0, 0))],
        out_specs=[pl.BlockSpec((tm, LW), lambda i: (i, 0)),
                   pl.BlockSpec((tm, AW), lambda i: (i, 0)),
                   pl.BlockSpec((tm, AW), lambda i: (i, 0)),
                   pl.BlockSpec((tm, D), lambda i: (i, 0)),
                   pl.BlockSpec((None, 8, D), lambda i: (i // tpb, 0, 0)),
                   pl.BlockSpec((8, D), lambda i: (0, 0))],
        out_shape=[jax.ShapeDtypeStruct((T, LW), F32),
                   jax.ShapeDtypeStruct((T, AW), F32),
                   jax.ShapeDtypeStruct((T, AW), F32),
                   jax.ShapeDtypeStruct((T, D), BF16),
                   jax.ShapeDtypeStruct((T // S, 8, D), F32),
                   jax.ShapeDtypeStruct((8, D), F32)],
        compiler_params=_cp(("arbitrary",), VMEM_BIG),
    )(dx2, ylru, osb, ofox, mo, mod, gmix, wout)


def _loss_head(y, tgt, S):
    T = y.shape[0]
    tm = _tile(S, 512)

    def body(y_ref, t_ref, dy_ref, l_ref):
        @pl.when(pl.program_id(0) == 0)
        def _():
            l_ref[...] = jnp.zeros_like(l_ref)

        d = y_ref[...] - t_ref[...]
        dy_ref[...] = d * (1.0 / D)
        l_ref[...] += (0.5 / D) * _rowsum(_colsum(d * d))

    return pl.pallas_call(
        body, name="loss_head",
        grid=(T // tm,),
        in_specs=[pl.BlockSpec((tm, D), lambda i: (i, 0)), pl.BlockSpec((tm, D), lambda i: (i, 0))],
        out_specs=[pl.BlockSpec((tm, D), lambda i: (i, 0)), pl.BlockSpec((8, 128), lambda i: (0, 0))],
        out_shape=[jax.ShapeDtypeStruct((T, D), F32), jax.ShapeDtypeStruct((8, 128), F32)],
        compiler_params=_cp(("arbitrary",)),
    )(y, tgt)


def _lru_gates(u, vp_ref, wr_ref, wi_ref):
    ub = u.astype(BF16)
    r = _sigmoid(_dot(ub, wr_ref[...]) + vp_ref[1:2, :])
    ig = _sigmoid(_dot(ub, wi_ref[...]) + vp_ref[2:3, :])
    lam = vp_ref[3:4, :]
    sp = jnp.maximum(-lam, 0.0) + _log1p(jnp.exp(-jnp.abs(lam)))
    log_a = (-LRU_C) * r * sp
    a = jnp.exp(log_a)
    mult = jnp.sqrt(-_expm1_neg(2.0 * log_a))
    return ub, r, ig, sp, a, mult


def _conv_taps(x, xp, row, cw_ref):
    xs = [x]
    for d in (1, 2, 3):
        xs.append(jnp.where(row >= d, pltpu.roll(x, d, 0), pltpu.roll(xp, d, 0)))
    u = xs[0] * cw_ref[3:4, :]
    for d in (1, 2, 3):
        u = u + xs[d] * cw_ref[3 - d:4 - d, :]
    return xs, u


def _lru_fwd(proj, cw, vp, wr, wi, l, S):
    T = proj.shape[0]
    ts = _tile(S, 256)
    nb = S // ts

    def body(x_ref, lg_ref, cw_ref, vp_ref, wr_ref, wi_ref, y_ref, h_ref, xp_sc, hc_sc):
        @pl.when(pl.program_id(1) == 0)
        def _():
            xp_sc[...] = jnp.zeros_like(xp_sc)
            hc_sc[...] = jnp.zeros_like(hc_sc)

        row = lax.broadcasted_iota(jnp.int32, (ts, LW), 0)
        x = x_ref[...]
        _, u = _conv_taps(x, xp_sc[...], row, cw_ref)
        u = u + vp_ref[0:1, :]
        xp_sc[...] = x
        _, _, ig, _, a, mult = _lru_gates(u, vp_ref, wr_ref, wi_ref)
        bv = mult * (ig * u)
        av = a
        d = 1
        while d < ts:
            a_s = jnp.where(row >= d, pltpu.roll(av, d, 0), 1.0)
            b_s = jnp.where(row >= d, pltpu.roll(bv, d, 0), 0.0)
            bv = av * b_s + bv
            av = av * a_s
            d *= 2
        h = bv + av * hc_sc[7:8, :]
        hc_sc[...] = h[ts - 8:ts, :]
        h_ref[...] = h
        gl, _ = _gelu_and_grad(lg_ref[...])
        y_ref[...] = h * gl

    return pl.pallas_call(
        body, name=f"lru_fwd_{l}",
        grid=(T // S, nb),
        in_specs=[pl.BlockSpec((ts, LW), lambda b, j: (b * nb + j, 0)),
                  pl.BlockSpec((ts, LW), lambda b, j: (b * nb + j, 1)),
                  pl.BlockSpec((None, 8, LW), lambda b, j: (l, 0, 0)),
                  pl.BlockSpec((None, 8, LW), lambda b, j: (l, 0, 0)),
                  pl.BlockSpec((None, LW, LW), lambda b, j: (l, 0, 0)),
                  pl.BlockSpec((None, LW, LW), lambda b, j: (l, 0, 0))],
        out_specs=[pl.BlockSpec((ts, LW), lambda b, j: (b * nb + j, 0)),
                   pl.BlockSpec((ts, LW), lambda b, j: (b * nb + j, 0))],
        out_shape=[jax.ShapeDtypeStruct((T, LW), F32), jax.ShapeDtypeStruct((T, LW), F32)],
        scratch_shapes=[pltpu.VMEM((ts, LW), F32), pltpu.VMEM((8, LW), F32)],
        compiler_params=_cp(("arbitrary", "arbitrary")),
    )(proj, proj, cw, vp, wr, wi)


def _lru_bwd(dyl, proj, h, cw, vp, wr, wi, l, S):
    T = proj.shape[0]
    ts = _tile(S, 256)
    nb = S // ts

    def body(dy_ref, x_ref, xprev_ref, lg_ref, h_ref, hprev_ref, cw_ref, vp_ref, wr_ref, wi_ref,
             dx_ref, dlg_ref, dpr_ref, dpi_ref, ub_ref, wacc_ref, gc_sc, af_sc, dun_sc):
        b = pl.program_id(0)
        j = pl.program_id(1)
        first = j == nb - 1

        @pl.when((b == 0) & (j == 0))
        def _():
            wacc_ref[...] = jnp.zeros_like(wacc_ref)

        @pl.when(j == 0)
        def _():
            gc_sc[...] = jnp.zeros_like(gc_sc)
            af_sc[...] = jnp.ones_like(af_sc)
            dun_sc[...] = jnp.zeros_like(dun_sc)

        row = lax.broadcasted_iota(jnp.int32, (ts, LW), 0)
        keep = jnp.where(first, 0.0, 1.0)
        x = x_ref[...]
        xs, u = _conv_taps(x, xprev_ref[...] * keep, row, cw_ref)
        u = u + vp_ref[0:1, :]
        ub, r, ig, sp, a, mult = _lru_gates(u, vp_ref, wr_ref, wi_ref)
        ub_ref[...] = ub
        hh = h_ref[...]
        h_m1 = jnp.where(row >= 1, pltpu.roll(hh, 1, 0), pltpu.roll(hprev_ref[...] * keep, 1, 0))
        dy = dy_ref[...]
        gl, dgl = _gelu_and_grad(lg_ref[...])
        dlg_ref[...] = (dy * hh * dgl).astype(BF16)
        bv = dy * gl
        av = jnp.where(row < ts - 1, pltpu.roll(a, ts - 1, 0), af_sc[0:1, :])
        d = 1
        while d < ts:
            a_s = jnp.where(row < ts - d, pltpu.roll(av, ts - d, 0), 1.0)
            b_s = jnp.where(row < ts - d, pltpu.roll(bv, ts - d, 0), 0.0)
            bv = av * b_s + bv
            av = av * a_s
            d *= 2
        gt = bv + av * gc_sc[0:1, :]
        gc_sc[...] = gt[0:8, :]
        af_sc[...] = a[0:8, :]
        da = gt * h_m1
        d_ig = gt * mult * u
        d_mult = gt * ig * u
        du = gt * mult * ig
        dlog_a = da * a - d_mult * (a * a) / mult
        dpre_r = (dlog_a * ((-LRU_C) * sp)) * r * (1.0 - r)
        dpre_i = d_ig * ig * (1.0 - ig)
        lam = vp_ref[3:4, :]
        wacc_ref[7:8, :] += _colsum(dlog_a * r) * (LRU_C * _sigmoid(-lam))
        wacc_ref[5:6, :] += _colsum(dpre_r)
        wacc_ref[6:7, :] += _colsum(dpre_i)
        dprb = dpre_r.astype(BF16)
        dpib = dpre_i.astype(BF16)
        dpr_ref[...] = dprb
        dpi_ref[...] = dpib
        du = du + _dot_nt(dprb, wr_ref[...]) + _dot_nt(dpib, wi_ref[...])
        wacc_ref[4:5, :] += _colsum(du)
        dun = dun_sc[...]
        dx = du * cw_ref[3:4, :]
        wacc_ref[3:4, :] += _colsum(du * xs[0])
        for dd in (1, 2, 3):
            du_s = jnp.where(row < ts - dd, pltpu.roll(du, ts - dd, 0), pltpu.roll(dun, ts - dd, 0))
            dx = dx + du_s * cw_ref[3 - dd:4 - dd, :]
            wacc_ref[3 - dd:4 - dd, :] += _colsum(du * xs[dd])
        dun_sc[...] = du
        dx_ref[...] = dx.astype(BF16)

    def tb(b, j):
        return b * nb + (nb - 1 - j)

    def tbp(b, j):
        return b * nb + jnp.maximum(nb - 2 - j, 0)

    return pl.pallas_call(
        body, name=f"lru_bwd_{l}",
        grid=(T // S, nb),
        in_specs=[pl.BlockSpec((ts, LW), lambda b, j: (tb(b, j), 0)),
                  pl.BlockSpec((ts, LW), lambda b, j: (tb(b, j), 0)),
                  pl.BlockSpec((ts, LW), lambda b, j: (tbp(b, j), 0)),
                  pl.BlockSpec((ts, LW), lambda b, j: (tb(b, j), 1)),
                  pl.BlockSpec((ts, LW), lambda b, j: (tb(b, j), 0)),
                  pl.BlockSpec((ts, LW), lambda b, j: (tbp(b, j), 0)),
                  pl.BlockSpec((None, 8, LW), lambda b, j: (l, 0, 0)),
                  pl.BlockSpec((None, 8, LW), lambda b, j: (l, 0, 0)),
                  pl.BlockSpec((None, LW, LW), lambda b, j: (l, 0, 0)),
                  pl.BlockSpec((None, LW, LW), lambda b, j: (l, 0, 0))],
        out_specs=[pl.BlockSpec((ts, LW), lambda b, j: (tb(b, j), 0)),
                   pl.BlockSpec((ts, LW), lambda b, j: (tb(b, j), 0)),
                   pl.BlockSpec((ts, LW), lambda b, j: (tb(b, j), 0)),
                   pl.BlockSpec((ts, LW), lambda b, j: (tb(b, j), 0)),
                   pl.BlockSpec((ts, LW), lambda b, j: (tb(b, j), 0)),
                   pl.BlockSpec((8, LW), lambda b, j: (0, 0))],
        out_shape=[jax.ShapeDtypeStruct((T, LW), BF16),
                   jax.ShapeDtypeStruct((T, LW), BF16),
                   jax.ShapeDtypeStruct((T, LW), BF16),
                   jax.ShapeDtypeStruct((T, LW), BF16),
                   jax.ShapeDtypeStruct((T, LW), BF16),
                   jax.ShapeDtypeStruct((8, LW), F32)],
        scratch_shapes=[pltpu.VMEM((8, LW), F32), pltpu.VMEM((8, LW), F32), pltpu.VMEM((ts, LW), F32)],
        compiler_params=_cp(("arbitrary", "arbitrary")),
    )(dyl, proj, proj, proj, h, h, cw, vp, wr, wi)


def _fgate_fwd(proj, bfp, l, S):
    T = proj.shape[0]

    def body(x_ref, b_ref, o_ref):
        z = x_ref[...] + b_ref[0:1, :]
        v = jnp.minimum(z, 0.0) - _log1p(jnp.exp(-jnp.abs(z)))
        row = lax.broadcasted_iota(jnp.int32, (S, 128), 0)
        d = 1
        while d < S:
            v = v + jnp.where(row >= d, pltpu.roll(v, d, 0), 0.0)
            d *= 2
        o_ref[...] = v

    return pl.pallas_call(
        body, name=f"fgate_fwd_{l}",
        grid=(T // S,),
        in_specs=[pl.BlockSpec((S, 128), lambda b: (b, F_BLK)),
                  pl.BlockSpec((None, 8, 128), lambda b: (l, 0, 0))],
        out_specs=pl.BlockSpec((S, 128), lambda b: (b, 0)),
        out_shape=jax.ShapeDtypeStruct((T, 128), F32),
        compiler_params=_cp(("arbitrary",)),
    )(proj, bfp)


def _fgate_bwd(dcum, proj, bfp, l, S):
    T = proj.shape[0]

    def body(d_ref, x_ref, b_ref, o_ref, wacc_ref):
        @pl.when(pl.program_id(0) == 0)
        def _():
            wacc_ref[...] = jnp.zeros_like(wacc_ref)

        v = d_ref[...]
        row = lax.broadcasted_iota(jnp.int32, (S, 128), 0)
        d = 1
        while d < S:
            v = v + jnp.where(row < S - d, pltpu.roll(v, S - d, 0), 0.0)
            d *= 2
        z = x_ref[...] + b_ref[0:1, :]
        dz = v * _sigmoid(-z)
        o_ref[...] = dz.astype(BF16)
        wacc_ref[0:1, :] += _colsum(dz)

    return pl.pallas_call(
        body, name=f"fgate_bwd_{l}",
        grid=(T // S,),
        in_specs=[pl.BlockSpec((S, 128), lambda b: (b, 0)),
                  pl.BlockSpec((S, 128), lambda b: (b, F_BLK)),
                  pl.BlockSpec((None, 8, 128), lambda b: (l, 0, 0))],
        out_specs=[pl.BlockSpec((S, 128), lambda b: (b, 0)), pl.BlockSpec((8, 128), lambda b: (0, 0))],
        out_shape=[jax.ShapeDtypeStruct((T, 128), BF16), jax.ShapeDtypeStruct((8, 128), F32)],
        compiler_params=_cp(("arbitrary",)),
    )(dcum, proj, bfp)


SBQ_BLK, SBK_BLK, SBV_BLK = 8, 10, 12
FXQ_BLK, FXK_BLK, FXV_BLK = 14, 16, 18
PAIR = 2 * HD


def _lane_masks():
    lane = lax.broadcasted_iota(jnp.int32, (1, PAIR), 1)
    return lane, lane < HD


def _pair_select(m0, a0, a1):
    return jnp.where(m0, a0, a1)


def _pair_split(x, m0):
    return jnp.where(m0, x, 0.0).astype(BF16), jnp.where(m0, 0.0, x).astype(BF16)


def _pair_mean(x, m0):
    s0 = _rowsum(jnp.where(m0, x, 0.0))
    s1 = _rowsum(x) - s0
    return jnp.where(m0, s0, s1) * (1.0 / HD)


def _pair_rms(x, m0):
    rstd = lax.rsqrt(_pair_mean(x * x, m0) + EPS)
    return x * rstd, rstd


def _pair_rms_bwd(xn, rstd, dyn, m0):
    return rstd * (dyn - xn * _pair_mean(dyn * xn, m0))


def _logsig2(z):
    l1p = jnp.log(1.0 + jnp.exp(-jnp.abs(z)))
    lb = jnp.minimum(z, 0.0) - l1p
    return lb, lb - z


def _rows(ref, blk, size):
    return ref[pl.ds(pl.multiple_of(blk * size, size), size), :]


def _loop_by_two(n, body, init):
    c = lax.fori_loop(0, n // 2, lambda i, cc: body(2 * i + 1, body(2 * i, cc)), init)
    return lax.cond(n % 2 == 1, lambda cc: body(n - 1, cc), lambda cc: cc, c)


def _transpose_blocks(src_ref, dst_sc, nblk, blk):
    for kb in range(nblk):
        dst_sc[kb] = src_ref[kb * blk:(kb + 1) * blk, :].astype(F32).T.astype(BF16)


def _sbq_fwd(proj, l, S, comm=None):
    T = proj.shape[0]
    tb = TQ_(S)
    nb = S // tb
    nbat = T // S
    c_args, c_specs, c_outs, c_scr = _hosted(comm)
    n_ci, n_co = len(c_args), len(c_outs)

    def body(*refs):
        q_ref, k_ref, v_ref = refs[:3]
        c_in = refs[3:3 + n_ci]
        o_ref, t1_ref = refs[3 + n_ci:5 + n_ci]
        c_out = refs[5 + n_ci:5 + n_ci + n_co]
        kt_sc, vb_sc = refs[5 + n_ci + n_co:7 + n_ci + n_co]
        c_sems = refs[7 + n_ci + n_co:]
        step = pl.program_id(0) * 2 + pl.program_id(1)
        if comm is not None:
            @pl.when(step == 0)
            def _():
                comm["start"](c_in, c_out, c_sems)

        _transpose_blocks(k_ref, kt_sc, nb, tb)
        vb_sc[...] = v_ref[...].astype(BF16)
        lane, m0 = _lane_masks()
        tri = _tri(tb, "row_gt_col")
        past = lax.broadcasted_iota(jnp.int32, (tb, tb), 1) < lax.broadcasted_iota(jnp.int32, (tb, tb), 0)

        def qloop(qb, carry):
            qh = _pair_split(_rows(q_ref, qb, tb) * SCALE, m0)

            def scores(kb):
                return tuple(_dot(qh[h], kt_sc[kb]) for h in range(2))

            def block(kb, kb_next, z, c, masked):
                mid = []
                for h in range(2):
                    lb, l1 = _logsig2(z[h])
                    if masked:
                        l1 = jnp.where(past, l1, 0.0)
                    mid.append((lb, l1, _cumsum_mm(l1, tri, parts=2)))
                z_next = scores(kb_next)
                pv, runs = [], []
                for h in range(2):
                    lb, l1, cs = mid[h]
                    w = jnp.exp(lb + (cs + c[h][1]))
                    if masked:
                        w = jnp.where(past, w, 0.0)
                    pv.append(_dot(w.astype(BF16), _rows(vb_sc, kb, tb)))
                    runs.append(c[h][1] + (cs[:, 0:1] + l1[:, 0:1]))
                return z_next, tuple((c[h][0] + pv[h], runs[h]) for h in range(2))

            zero = (jnp.zeros((tb, PAIR), F32), jnp.zeros((tb, 1), F32))
            z, c = block(qb, jnp.maximum(qb - 1, 0), scores(qb), (zero, zero), True)

            def off_diag(i, zc):
                kb = qb - 1 - i
                return block(kb, jnp.maximum(kb - 1, 0), zc[0], zc[1], False)

            _, c = _loop_by_two(qb, off_diag, (z, c))
            r0 = pl.multiple_of(qb * tb, tb)
            o_ref[pl.ds(r0, tb), :] = _pair_select(m0, c[0][0], c[1][0])
            t1_ref[pl.ds(r0, tb), :] = jnp.where(lane == 0, c[0][1], jnp.where(lane == 1, c[1][1], 0.0))
            return carry

        lax.fori_loop(0, nb, qloop, 0)
        if comm is not None:
            @pl.when(step == 2 * nbat - 1)
            def _():
                comm["finish"](c_in, c_out, c_sems)

    def col(blk):
        return pl.BlockSpec((S, PAIR), lambda b, p: (b, blk + p))

    anyspec = pl.BlockSpec(memory_space=pl.ANY)
    out = pl.pallas_call(
        body, name=f"sb_fwd_{l}",
        grid=(nbat, 2),
        in_specs=[col(SBQ_BLK), col(SBK_BLK), col(SBV_BLK)] + c_specs,
        out_specs=[col(0), col(0)] + [anyspec] * n_co,
        out_shape=[jax.ShapeDtypeStruct((T, AW), F32), jax.ShapeDtypeStruct((T, AW), F32)] + c_outs,
        scratch_shapes=[pltpu.VMEM((nb, PAIR, tb), BF16), pltpu.VMEM((S, PAIR), BF16)] + c_scr,
        compiler_params=_cp(("arbitrary", "arbitrary"), VMEM_BIG),
    )(proj, proj, proj, *c_args)
    return out[0], out[1], list(out[2:])


def _sbq_bwd(proj, do, t1, l, S, comm=None):
    T = proj.shape[0]
    tb = TQ_(S)
    nb = S // tb
    nbat = T // S
    c_args, c_specs, c_outs, c_scr = _hosted(comm)
    n_ci, n_co = len(c_args), len(c_outs)

    def body(*refs):
        q_ref, k_ref, v_ref, do_ref, t1_ref = refs[:5]
        c_in = refs[5:5 + n_ci]
        dq_ref, dk_ref, dv_ref = refs[5 + n_ci:8 + n_ci]
        c_out = refs[8 + n_ci:8 + n_ci + n_co]
        kb_sc, kt_sc, vt_sc, dkt_sc, dvt_sc = refs[8 + n_ci + n_co:13 + n_ci + n_co]
        c_sems = refs[13 + n_ci + n_co:]
        step = pl.program_id(0) * 2 + pl.program_id(1)
        if comm is not None:
            @pl.when(step == 0)
            def _():
                comm["start"](c_in, c_out, c_sems)

        kb_sc[...] = k_ref[...].astype(BF16)
        _transpose_blocks(k_ref, kt_sc, nb, tb)
        _transpose_blocks(v_ref, vt_sc, nb, tb)
        dkt_sc[...] = jnp.zeros_like(dkt_sc)
        dvt_sc[...] = jnp.zeros_like(dvt_sc)
        _, m0 = _lane_masks()
        mt0 = lax.broadcasted_iota(jnp.int32, (PAIR, 1), 0) < HD
        tri_in = _tri(tb, "row_le_col")
        tri_ex = _tri(tb, "row_lt_col")
        past = lax.broadcasted_iota(jnp.int32, (tb, tb), 1) < lax.broadcasted_iota(jnp.int32, (tb, tb), 0)

        def qloop(qb, carry):
            qf = _rows(q_ref, qb, tb) * SCALE
            dof = _rows(do_ref, qb, tb)
            qh = _pair_split(qf, m0)
            doh = _pair_split(dof, m0)
            qth = _pair_split(qf.T, mt0)
            doth = _pair_split(dof.T, mt0)
            t1v = _rows(t1_ref, qb, tb)
            tot = (t1v[:, 0:1], t1v[:, 1:2])

            def accumulate(kb, wz, dqs):
                out = []
                for h in range(2):
                    wb, dz = wz[h]
                    dvt_sc[kb] += _dot(doth[h], wb)
                    dkt_sc[kb] += _dot(qth[h], dz)
                    out.append(dqs[h] + _dot(dz, _rows(kb_sc, kb, tb)))
                return tuple(out)

            def block(kb, c, masked):
                hs = range(2)
                runs, dqs, (kb_prev, wz_prev) = c
                z = [_dot(qh[h], kt_sc[kb]) for h in hs]
                dw = [_dot(doh[h], vt_sc[kb]) for h in hs]
                st = []
                for h in hs:
                    lb, l1 = _logsig2(z[h])
                    if masked:
                        l1 = jnp.where(past, l1, 0.0)
                    st.append((lb, _cumsum_mm(l1, tri_in, parts=2)))
                dqs = accumulate(kb_prev, wz_prev, dqs)
                mid = []
                for h in hs:
                    lb, p1 = st[h]
                    w = jnp.exp(lb + (tot[h] - (runs[h][0] + p1)))
                    if masked:
                        w = jnp.where(past, w, 0.0)
                    gm = w * dw[h]
                    mid.append((w.astype(BF16), gm, _cumsum_mm(gm, tri_ex, parts=1)))
                new_runs, wz = [], []
                for h in hs:
                    run1, rung = runs[h]
                    wb, gm, cx = mid[h]
                    dz = gm - (gm + (rung + cx)) * jnp.exp(st[h][0])
                    if masked:
                        dz = jnp.where(past, dz, 0.0)
                    wz.append((wb, dz.astype(BF16)))
                    p1 = st[h][1]
                    new_runs.append((run1 + p1[:, tb - 1:tb], rung + (cx[:, tb - 1:tb] + gm[:, tb - 1:tb])))
                return tuple(new_runs), dqs, (kb, tuple(wz))

            z1 = jnp.zeros((tb, 1), F32)
            zq = jnp.zeros((tb, PAIR), F32)
            zb = jnp.zeros((tb, tb), BF16)
            none = (jnp.int32(0), ((zb, zb), (zb, zb)))
            c = _loop_by_two(qb, lambda i, cc: block(i, cc, False), (((z1, z1), (z1, z1)), (zq, zq), none))
            _, dqs, (kb_last, wz_last) = block(qb, c, True)
            dqs = accumulate(kb_last, wz_last, dqs)
            r0 = pl.multiple_of(qb * tb, tb)
            dq_ref[pl.ds(r0, tb), :] = (_pair_select(m0, dqs[0], dqs[1]) * SCALE).astype(BF16)
            return carry

        lax.fori_loop(0, nb, qloop, 0)
        for kb in range(nb):
            dk_ref[kb * tb:(kb + 1) * tb, :] = dkt_sc[kb].T.astype(BF16)
            dv_ref[kb * tb:(kb + 1) * tb, :] = dvt_sc[kb].T.astype(BF16)
        if comm is not None:
            @pl.when(step == 2 * nbat - 1)
            def _():
                comm["finish"](c_in, c_out, c_sems)

    def col(blk):
        return pl.BlockSpec((S, PAIR), lambda b, p: (b, blk + p))

    sh = jax.ShapeDtypeStruct((T, AW), BF16)
    anyspec = pl.BlockSpec(memory_space=pl.ANY)
    out = pl.pallas_call(
        body, name=f"sb_bwd_{l}",
        grid=(nbat, 2),
        in_specs=[col(SBQ_BLK), col(SBK_BLK), col(SBV_BLK), col(0), col(0)] + c_specs,
        out_specs=[col(0), col(0), col(0)] + [anyspec] * n_co,
        out_shape=[sh, sh, sh] + c_outs,
        scratch_shapes=[pltpu.VMEM((S, PAIR), BF16), pltpu.VMEM((nb, PAIR, tb), BF16), pltpu.VMEM((nb, PAIR, tb), BF16),
                        pltpu.VMEM((nb, PAIR, tb), F32), pltpu.VMEM((nb, PAIR, tb), F32)] + c_scr,
        compiler_params=_cp(("arbitrary", "arbitrary"), VMEM_BIG),
    )(proj, proj, proj, do, t1, *c_args)
    return out[0], out[1], out[2], list(out[3:])


def _foxq_fwd(proj, cum, ck, gqk2, l, S):
    T = proj.shape[0]
    tb = TQ_(S)
    nb = S // tb

    def body(q_ref, k_ref, v_ref, cum_ref, ck_ref, g_ref, o_ref, nl_ref, fk_sc, fkt_sc, vb_sc):
        lane, m0 = _lane_masks()
        p = pl.program_id(1)
        kn, _ = _pair_rms(k_ref[...], m0)
        fk_sc[...] = kn * g_ref[1:2, :]
        _transpose_blocks(fk_sc, fkt_sc, nb, tb)
        vb_sc[...] = v_ref[...].astype(BF16)
        causal = lax.broadcasted_iota(jnp.int32, (tb, tb), 1) <= lax.broadcasted_iota(jnp.int32, (tb, tb), 0)

        def qloop(qb, carry):
            qn, _ = _pair_rms(_rows(q_ref, qb, tb), m0)
            fqh = _pair_split(qn * (g_ref[0:1, :] * SCALE), m0)
            cumv = _rows(cum_ref, qb, tb)
            cq = [_rowsum(jnp.where(lane == 2 * p + h, cumv, 0.0)) for h in range(2)]

            def scores(kb):
                return tuple(_dot(fqh[h], fkt_sc[kb]) for h in range(2))

            def block(kb, kb_next, qk, c, masked):
                st = []
                for h in range(2):
                    s = qk[h] + (cq[h] - ck_ref[h, kb])
                    if masked:
                        s = jnp.where(causal, s, NEG)
                    m2 = jnp.maximum(c[h][0], jnp.max(s, axis=1, keepdims=True))
                    pr = jnp.exp(s - m2)
                    hi = pr.astype(BF16)
                    lo = (pr - hi.astype(F32)).astype(BF16)
                    vv = _rows(vb_sc, kb, tb)
                    st.append((m2, pr, _dot(hi, vv) + _dot(lo, vv)))
                qk_next = scores(kb_next)
                out = []
                for h in range(2):
                    m, lsum, acc = c[h]
                    m2, pr, pv = st[h]
                    al = jnp.exp(m - m2)
                    out.append((m2, al * lsum + _rowsum(pr), al * acc + pv))
                return qk_next, tuple(out)

            zero = (jnp.full((tb, 1), NEG, F32), jnp.zeros((tb, 1), F32), jnp.zeros((tb, PAIR), F32))

            def off_diag(i, sc):
                return block(i, i + 1, sc[0], sc[1], False)

            qk, c = _loop_by_two(qb, off_diag, (scores(0), (zero, zero)))
            _, c = block(qb, qb, qk, c, True)
            r0 = pl.multiple_of(qb * tb, tb)
            o_ref[pl.ds(r0, tb), :] = _pair_select(m0, c[0][2] / c[0][1], c[1][2] / c[1][1])
            nl = [cq[h] - (c[h][0] + jnp.log(c[h][1])) for h in range(2)]
            nl_ref[pl.ds(r0, tb), :] = jnp.where(lane == 0, nl[0], jnp.where(lane == 1, nl[1], 0.0))
            return carry

        lax.fori_loop(0, nb, qloop, 0)

    def col(blk):
        return pl.BlockSpec((S, PAIR), lambda b, p: (b, blk + p))

    return pl.pallas_call(
        body, name=f"fox_fwd_{l}",
        grid=(T // S, 2),
        in_specs=[col(FXQ_BLK), col(FXK_BLK), col(FXV_BLK),
                  pl.BlockSpec((S, 128), lambda b, p: (b, 0)),
                  pl.BlockSpec((None, 2, nb, 1, tb), lambda b, p: (b, p, 0, 0, 0)),
                  pl.BlockSpec((None, 8, PAIR), lambda b, p: (l, 0, 0))],
        out_specs=[col(0), col(0)],
        out_shape=[jax.ShapeDtypeStruct((T, AW), F32)] * 2,
        scratch_shapes=[pltpu.VMEM((S, PAIR), F32), pltpu.VMEM((nb, PAIR, tb), BF16), pltpu.VMEM((S, PAIR), BF16)],
        compiler_params=_cp(("arbitrary", "arbitrary"), VMEM_BIG),
    )(proj, proj, proj, cum, ck, gqk2)


def _foxq_bwd(proj, do, nl, ox, ck, gqk2, l, S):
    T = proj.shape[0]
    tb = TQ_(S)
    nb = S // tb

    def body(q_ref, k_ref, v_ref, do_ref, nl_ref, ox_ref, ck_ref, g_ref,
             dq_ref, dk_ref, dv_ref, dc_ref, wacc_ref, fk_sc, fkt_sc, vt_sc, dfkt_sc, dvt_sc):
        @pl.when((pl.program_id(0) == 0) & (pl.program_id(1) == 0))
        def _():
            wacc_ref[...] = jnp.zeros_like(wacc_ref)

        _, m0 = _lane_masks()
        mt0 = lax.broadcasted_iota(jnp.int32, (PAIR, 1), 0) < HD
        g0 = g_ref[0:1, :]
        g1 = g_ref[1:2, :]
        fk_sc[...] = (_pair_rms(k_ref[...], m0)[0] * g1).astype(BF16)
        _transpose_blocks(fk_sc, fkt_sc, nb, tb)
        _transpose_blocks(v_ref, vt_sc, nb, tb)
        dfkt_sc[...] = jnp.zeros_like(dfkt_sc)
        dvt_sc[...] = jnp.zeros_like(dvt_sc)
        dc_ref[...] = jnp.zeros_like(dc_ref)
        causal = lax.broadcasted_iota(jnp.int32, (tb, tb), 1) <= lax.broadcasted_iota(jnp.int32, (tb, tb), 0)

        def qloop(qb, carry):
            qn, qr = _pair_rms(_rows(q_ref, qb, tb), m0)
            fqf = qn * (g0 * SCALE)
            dof = _rows(do_ref, qb, tb)
            fqh = _pair_split(fqf, m0)
            doh = _pair_split(dof, m0)
            fqth = _pair_split(fqf.T, mt0)
            doth = _pair_split(dof.T, mt0)
            nlv = _rows(nl_ref, qb, tb)
            cql = (nlv[:, 0:1], nlv[:, 1:2])

            def probs(kb, masked):
                qk = [_dot(fqh[h], fkt_sc[kb]) for h in range(2)]
                dp = [_dot(doh[h], vt_sc[kb]) for h in range(2)]
                pr = []
                for h in range(2):
                    e = jnp.exp(qk[h] + (cql[h] - ck_ref[h, kb]))
                    pr.append(jnp.where(causal, e, 0.0) if masked else e)
                return pr, dp

            oxv = _rows(ox_ref, qb, tb)
            dlt = [_rowsum(doh[h].astype(F32) * oxv) for h in range(2)]

            def accumulate(kb, pd, dfqs):
                out = []
                for h in range(2):
                    prb, dsb = pd[h]
                    dvt_sc[kb] += _dot(doth[h], prb)
                    dfkt_sc[kb] += _dot(fqth[h], dsb)
                    out.append(dfqs[h] + _dot(dsb, _rows(fk_sc, kb, tb)))
                return tuple(out)

            def block(kb, c, masked):
                dfqs, (kb_prev, pd_prev) = c
                pr, dp = probs(kb, masked)
                dfqs = accumulate(kb_prev, pd_prev, dfqs)
                pd = []
                for h in range(2):
                    ds = pr[h] * (dp[h] - dlt[h])
                    dc_ref[h, kb] += jnp.broadcast_to(-_colsum(ds), (8, tb))
                    pd.append((pr[h].astype(BF16), ds.astype(BF16)))
                return dfqs, (kb, tuple(pd))

            zq = jnp.zeros((tb, PAIR), F32)
            zb = jnp.zeros((tb, tb), BF16)
            none = (jnp.int32(0), ((zb, zb), (zb, zb)))
            c = _loop_by_two(qb, lambda i, cc: block(i, cc, False), ((zq, zq), none))
            dfqs, (kb_last, pd_last) = block(qb, c, True)
            c = accumulate(kb_last, pd_last, dfqs)
            dfq = _pair_select(m0, c[0], c[1]) * SCALE
            wacc_ref[0:1, :] += _colsum(dfq * qn)
            r0 = pl.multiple_of(qb * tb, tb)
            dq_ref[pl.ds(r0, tb), :] = _pair_rms_bwd(qn, qr, dfq * g0, m0).astype(BF16)
            return carry

        lax.fori_loop(0, nb, qloop, 0)
        for kb in range(nb):
            rows = slice(kb * tb, (kb + 1) * tb)
            dfk = dfkt_sc[kb].T
            knb, krb = _pair_rms(k_ref[rows, :], m0)
            wacc_ref[1:2, :] += _colsum(dfk * knb)
            dk_ref[rows, :] = _pair_rms_bwd(knb, krb, dfk * g1, m0).astype(BF16)
            dv_ref[rows, :] = dvt_sc[kb].T.astype(BF16)

    def col(blk):
        return pl.BlockSpec((S, PAIR), lambda b, p: (b, blk + p))

    sh = jax.ShapeDtypeStruct((T, AW), BF16)
    return pl.pallas_call(
        body, name=f"fox_bwd_{l}",
        grid=(T // S, 2),
        in_specs=[col(FXQ_BLK), col(FXK_BLK), col(FXV_BLK), col(0), col(0), col(0),
                  pl.BlockSpec((None, 2, nb, 1, tb), lambda b, p: (b, p, 0, 0, 0)),
                  pl.BlockSpec((None, 8, PAIR), lambda b, p: (l, 0, 0))],
        out_specs=[col(0), col(0), col(0),
                   pl.BlockSpec((None, 2, nb, 8, tb), lambda b, p: (b, p, 0, 0, 0)),
                   pl.BlockSpec((8, PAIR), lambda b, p: (0, 0))],
        out_shape=[sh, sh, sh,
                   jax.ShapeDtypeStruct((T // S, NH, nb, 8, tb), F32),
                   jax.ShapeDtypeStruct((8, PAIR), F32)],
        scratch_shapes=[pltpu.VMEM((S, PAIR), BF16), pltpu.VMEM((nb, PAIR, tb), BF16), pltpu.VMEM((nb, PAIR, tb), BF16),
                        pltpu.VMEM((nb, PAIR, tb), F32), pltpu.VMEM((nb, PAIR, tb), F32)],
        compiler_params=_cp(("arbitrary", "arbitrary"), VMEM_BIG),
    )(proj, proj, proj, do, nl, ox, ck, gqk2)


def _ada_fwd(c_all, w_ada, b_cols):
    nb, ncol = c_all.shape[0], w_ada.shape[2]
    tn = _tile(ncol, 768)

    def body(c_ref, w_ref, b_ref, o_ref):
        c = c_ref[...]
        ca = (c * _sigmoid(c)).astype(BF16)
        o_ref[...] = _dot(ca, w_ref[...].astype(BF16)) + b_ref[...]

    return pl.pallas_call(
        body, name="ada_fwd",
        grid=(2, ncol // tn),
        in_specs=[pl.BlockSpec((nb, D), lambda l, n: (0, 0)),
                  pl.BlockSpec((None, D, tn), lambda l, n: (l, 0, n)),
                  pl.BlockSpec((None, 1, tn), lambda l, n: (l, 0, n))],
        out_specs=pl.BlockSpec((None, nb, tn), lambda l, n: (l, 0, n)),
        out_shape=jax.ShapeDtypeStruct((2, nb, ncol), F32),
        compiler_params=_cp(("arbitrary", "arbitrary")),
    )(c_all, w_ada, b_cols)


def _ada_bwd(c_all, dmod_cols):
    nb, ncol = c_all.shape[0], dmod_cols.shape[2]
    tn = _tile(ncol, 768)

    def body(c_ref, d_ref, o_ref):
        c = c_ref[...]
        ca = (c * _sigmoid(c)).astype(BF16)
        o_ref[...] = _dot_tn(ca, d_ref[...].astype(BF16))

    return pl.pallas_call(
        body, name="ada_bwd",
        grid=(2, ncol // tn),
        in_specs=[pl.BlockSpec((nb, D), lambda l, n: (0, 0)),
                  pl.BlockSpec((None, nb, tn), lambda l, n: (l, 0, n))],
        out_specs=pl.BlockSpec((None, D, tn), lambda l, n: (l, 0, n)),
        out_shape=jax.ShapeDtypeStruct((2, D, ncol), F32),
        compiler_params=_cp(("arbitrary", "arbitrary")),
    )(c_all, dmod_cols)


def _sum_lead(a, name):
    n, R, C = a.shape
    tr = _tile_div8(R, 256)

    def body(a_ref, o_ref):
        acc = a_ref[0]
        for i in range(1, n):
            acc = acc + a_ref[i]
        o_ref[...] = acc

    return pl.pallas_call(
        body, name=name,
        grid=(R // tr,),
        in_specs=[pl.BlockSpec((n, tr, C), lambda i: (0, i, 0))],
        out_specs=pl.BlockSpec((tr, C), lambda i: (i, 0)),
        out_shape=jax.ShapeDtypeStruct((R, C), F32),
        compiler_params=_cp(("arbitrary",)),
    )(a)


def _adamw(w, g, m, v, name):
    R, C = w.shape
    tr = _tile_div8(R, max(8, (1 << 18) // C))
    c1 = 1.0 / (1.0 - ADAM_B1 ** ADAM_STEP)
    c2 = 1.0 / (1.0 - ADAM_B2 ** ADAM_STEP)

    def body(w_ref, g_ref, m_ref, v_ref, d_ref, mo_ref, vo_ref):
        gg = g_ref[...]
        mn = ADAM_B1 * m_ref[...] + (1.0 - ADAM_B1) * gg
        vn = ADAM_B2 * v_ref[...] + (1.0 - ADAM_B2) * (gg * gg)
        mo_ref[...] = mn
        vo_ref[...] = vn
        d_ref[...] = (-ADAM_LR) * ((mn * c1) / (jnp.sqrt(vn * c2) + ADAM_EPS) + ADAM_WD * w_ref[...])

    spec = pl.BlockSpec((tr, C), lambda i: (i, 0))
    sh = jax.ShapeDtypeStruct((R, C), F32)
    return pl.pallas_call(
        body, name=name, grid=(R // tr,),
        in_specs=[spec] * 4, out_specs=[spec] * 3, out_shape=[sh] * 3,
        compiler_params=_cp(("arbitrary",)),
    )(w, g, m, v)


def _coords():
    return lax.axis_index("x"), lax.axis_index("y"), lax.axis_index("c")


def _all_gather8(blk, name, vmem):
    m_per, n = blk.shape
    space = pltpu.VMEM if vmem else pl.ANY

    def body(x_ref, out_ref, send_sems, recv_sems, local_sem):
        x, y, c = _coords()
        me, sibling = (x, y, c), (x, y, 1 - c)
        chips = [(1 - x, y), (x, 1 - y), (1 - x, 1 - y)]

        def rows(px, py, pc):
            return out_ref.at[4 * px + 2 * py + pc]

        def copy(k, block, to, src=None):
            return pltpu.make_async_remote_copy(
                src_ref=rows(*block) if src is None else src, dst_ref=rows(*block),
                send_sem=send_sems.at[k], recv_sem=recv_sems.at[k], device_id=to, device_id_type=MESH)

        mine = pltpu.make_async_copy(x_ref, rows(*me), local_sem)
        mine.start()
        first = [copy(0, me, sibling, src=x_ref)]
        first += [copy(1 + j, me, (*chip, c), src=x_ref) for j, chip in enumerate(chips)]
        for cp in first:
            cp.start()
        passed = [copy(4 + j, (*chip, c), sibling) for j, chip in enumerate(chips)]
        for j, chip in enumerate(chips):
            copy(1 + j, (*chip, c), me).wait_recv()
            passed[j].start()
        copy(0, sibling, me).wait_recv()
        for j, chip in enumerate(chips):
            copy(4 + j, (*chip, 1 - c), me).wait_recv()
        for cp in first + passed:
            cp.wait_send()
        mine.wait()

    return pl.pallas_call(
        body, name=name,
        out_shape=jax.ShapeDtypeStruct((N_DEV, m_per, n), blk.dtype),
        in_specs=[pl.BlockSpec(memory_space=space)],
        out_specs=pl.BlockSpec(memory_space=space),
        scratch_shapes=[pltpu.SemaphoreType.DMA((7,)), pltpu.SemaphoreType.DMA((7,)), pltpu.SemaphoreType.DMA],
        compiler_params=pltpu.CompilerParams(vmem_limit_bytes=VMEM_BIG if vmem else None),
    )(blk)


def _run_comm(comm, name):
    n_in, n_out = len(comm["args"]), len(comm["out_shapes"])

    def body(*refs):
        parts = (refs[:n_in], refs[n_in:n_in + n_out], refs[n_in + n_out:])
        comm["start"](*parts)
        comm["finish"](*parts)

    anyspec = pl.BlockSpec(memory_space=pl.ANY)
    return pl.pallas_call(
        body, name=name, out_shape=comm["out_shapes"],
        in_specs=[anyspec] * n_in, out_specs=[anyspec] * n_out, scratch_shapes=comm["scratch"],
    )(*comm["args"])


def _hosted(comm):
    if comm is None:
        return [], [], [], []
    anyspec = pl.BlockSpec(memory_space=pl.ANY)
    return list(comm["args"]), [anyspec] * len(comm["args"]), list(comm["out_shapes"]), list(comm["scratch"])


def _ag_comm(wshards, pieces):
    n_piece = len(pieces)
    halves = [wshards[i].shape[len(lead)] // 2 for i, lead, _ in pieces]

    def plan(ins, outs, sems):
        send_sems, recv_sems, local_sems = sems
        x, y, c = _coords()
        me, sibling = (x, y, c), (x, y, 1 - c)
        chips = [(1 - x, y), (x, 1 - y), (1 - x, 1 - y)]

        def dsts(px, py, pc):
            s = 2 * px + py
            return [outs[p].at[s, pl.ds(pc * r2, r2)] if stacked else outs[p].at[pl.ds((2 * s + pc) * r2, r2)]
                    for p, ((_, _, stacked), r2) in enumerate(zip(pieces, halves))]

        srcs = [ins[i].at[(*lead, pl.ds(c * r2, r2))] for (i, lead, _), r2 in zip(pieces, halves)]

        def copies(k, block, to, own=False):
            d = dsts(*block)
            return [pltpu.make_async_remote_copy(
                src_ref=srcs[p] if own else d[p], dst_ref=d[p], send_sem=send_sems.at[k, p],
                recv_sem=recv_sems.at[k, p], device_id=to, device_id_type=MESH) for p in range(n_piece)]

        mine = [pltpu.make_async_copy(srcs[p], d, local_sems.at[p]) for p, d in enumerate(dsts(*me))]
        first = copies(0, me, sibling, own=True)
        for j, chip in enumerate(chips):
            first += copies(1 + j, me, (*chip, c), own=True)
        return me, sibling, chips, c, copies, mine, first

    def start(ins, outs, sems):
        *_, mine, first = plan(ins, outs, sems)
        for cp in mine + first:
            cp.start()

    def finish(ins, outs, sems):
        me, sibling, chips, c, copies, mine, first = plan(ins, outs, sems)
        passed = []
        for j, chip in enumerate(chips):
            for cp in copies(1 + j, (*chip, c), me):
                cp.wait_recv()
            fwd = copies(4 + j, (*chip, c), sibling)
            for cp in fwd:
                cp.start()
            passed += fwd
        for cp in copies(0, sibling, me):
            cp.wait_recv()
        for j, chip in enumerate(chips):
            for cp in copies(4 + j, (*chip, 1 - c), me):
                cp.wait_recv()
        for cp in first + passed:
            cp.wait_send()
        for cp in mine:
            cp.wait()

    out_shapes = []
    for (i, lead, stacked), r2 in zip(pieces, halves):
        cols = wshards[i].shape[-1]
        out_shapes.append(jax.ShapeDtypeStruct((N_SHARD, 2 * r2, cols) if stacked else (N_SHARD * 2 * r2, cols), BF16))
    return dict(
        args=list(wshards), out_shapes=out_shapes,
        scratch=[pltpu.SemaphoreType.DMA((7, n_piece)), pltpu.SemaphoreType.DMA((7, n_piece)),
                 pltpu.SemaphoreType.DMA((n_piece,))],
        start=start, finish=finish)


def _rs_to_chips_comm(hs):
    n = len(hs)

    def copies(h, r, sems):
        send_sems, recv_sems = sems
        x, y, c = _coords()
        chips = [(1 - x, y), (x, 1 - y), (1 - x, 1 - y)]
        return [pltpu.make_async_remote_copy(
            src_ref=h[p].at[2 * px + py], dst_ref=r[p].at[k], send_sem=send_sems.at[k, p], recv_sem=recv_sems.at[k, p],
            device_id=(px, py, c), device_id_type=MESH) for k, (px, py) in enumerate(chips) for p in range(n)]

    def start(h, r, sems):
        for cp in copies(h, r, sems):
            cp.start()

    def finish(h, r, sems):
        for cp in copies(h, r, sems):
            cp.wait()

    return dict(args=list(hs), out_shapes=[jax.ShapeDtypeStruct((3,) + h.shape[1:], h.dtype) for h in hs],
                scratch=[pltpu.SemaphoreType.DMA((3, n)), pltpu.SemaphoreType.DMA((3, n))],
                start=start, finish=finish)


def _rs_to_sibling(pieces, name):
    n = len(pieces)

    def body(*refs):
        g, r, (send_sems, recv_sems) = refs[:n], refs[n:2 * n], refs[2 * n:]
        x, y, c = _coords()
        cps = []
        for p in range(n):
            r2 = g[p].shape[1] // 2
            cps.append(pltpu.make_async_remote_copy(
                src_ref=g[p].at[:, pl.ds((1 - c) * r2, r2)], dst_ref=r[p], send_sem=send_sems.at[p],
                recv_sem=recv_sems.at[p], device_id=(x, y, 1 - c), device_id_type=MESH))
        for cp in cps:
            cp.start()
        for cp in cps:
            cp.wait()

    anyspec = pl.BlockSpec(memory_space=pl.ANY)
    return pl.pallas_call(
        body, name=name,
        out_shape=[jax.ShapeDtypeStruct((N_SHARD, g.shape[1] // 2, g.shape[2]), g.dtype) for g in pieces],
        in_specs=[anyspec] * n, out_specs=[anyspec] * n,
        scratch_shapes=[pltpu.SemaphoreType.DMA((n,)), pltpu.SemaphoreType.DMA((n,))],
    )(*pieces)


def _share_halves(tensors, places, r2s):
    n, no = len(places), len(tensors)

    def body(*refs):
        o, (send_sems, recv_sems) = refs[no:2 * no], refs[2 * no:]
        x, y, c = _coords()

        def half(p, hc):
            oi, lead = places[p]
            return o[oi].at[(*lead, pl.ds(hc * r2s[p], r2s[p]))]

        outs = [pltpu.make_async_remote_copy(
            src_ref=half(p, c), dst_ref=half(p, c), send_sem=send_sems.at[p], recv_sem=recv_sems.at[p],
            device_id=(x, y, 1 - c), device_id_type=MESH) for p in range(n)]
        for cp in outs:
            cp.start()
        for p in range(n):
            pltpu.make_async_remote_copy(
                src_ref=half(p, 1 - c), dst_ref=half(p, 1 - c), send_sem=send_sems.at[p], recv_sem=recv_sems.at[p],
                device_id=(x, y, 1 - c), device_id_type=MESH).wait_recv()
        for cp in outs:
            cp.wait_send()

    anyspec = pl.BlockSpec(memory_space=pl.ANY)
    return pl.pallas_call(
        body, name="share_halves",
        out_shape=[jax.ShapeDtypeStruct(t.shape, t.dtype) for t in tensors],
        in_specs=[anyspec] * no, out_specs=[anyspec] * no,
        input_output_aliases={i: i for i in range(no)},
        scratch_shapes=[pltpu.SemaphoreType.DMA((n,)), pltpu.SemaphoreType.DMA((n,))],
    )(*tensors)


def _add_rows(r2, cols, n_arrays):
    lanes = -(-cols // 128) * 128
    return _tile_div8(r2, max(16, (24 << 20) // (2 * n_arrays * lanes * 4)), mult=16)


def _add_sibling(pieces, recvs, cidx, name):
    n = len(pieces)
    _, R, C = pieces[0].shape
    r2 = R // 2
    tr = _add_rows(r2, C, 2 * n)
    nt = r2 // tr

    def body(c_ref, *refs):
        for p in range(n):
            refs[2 * n + p][...] = (refs[p][...] + refs[n + p][...].astype(F32)).astype(BF16)

    return pl.pallas_call(
        body, name=name,
        grid_spec=pltpu.PrefetchScalarGridSpec(
            num_scalar_prefetch=1, grid=(N_SHARD, nt),
            in_specs=[pl.BlockSpec((None, tr, C), lambda s, i, c_ref: (s, c_ref[0] * nt + i, 0))] * n
            + [pl.BlockSpec((None, tr, C), lambda s, i, c_ref: (s, i, 0))] * n,
            out_specs=[pl.BlockSpec((None, tr, C), lambda s, i, c_ref: (s, i, 0))] * n),
        out_shape=[jax.ShapeDtypeStruct((N_SHARD, r2, C), BF16)] * n,
        compiler_params=_cp(("arbitrary", "arbitrary"), VMEM_BIG),
    )(cidx, *pieces, *recvs)


def _add_chips_into(piece, recv_a, recv_b, sc, prev, shape, lead, name):
    _, R, C = piece.shape
    r2 = R // 2
    tr = _add_rows(r2, C, 4)
    nt = r2 // tr
    nl = len(lead)

    def body(sc_ref, p_ref, a_ref, b_ref, *rest):
        o_ref = rest[-1]
        acc = p_ref[...] + a_ref[...].astype(F32)
        for k in range(3):
            acc = acc + b_ref[k].astype(F32)
        o_ref[...] = acc

    in_specs = [pl.BlockSpec((None, tr, C), lambda i, sc_ref: (sc_ref[0], sc_ref[1] * nt + i, 0)),
                pl.BlockSpec((None, tr, C), lambda i, sc_ref: (sc_ref[0], i, 0)),
                pl.BlockSpec((3, tr, C), lambda i, sc_ref: (0, i, 0))]
    args = [sc, piece, recv_a, recv_b]
    aliases = {}
    if prev is not None:
        in_specs.append(pl.BlockSpec(memory_space=pl.ANY))
        args.append(prev)
        aliases = {4: 0}
    return pl.pallas_call(
        body, name=name,
        grid_spec=pltpu.PrefetchScalarGridSpec(
            num_scalar_prefetch=1, grid=(nt,), in_specs=in_specs,
            out_specs=pl.BlockSpec((None,) * nl + (tr, C), lambda i, sc_ref: (*lead, sc_ref[1] * nt + i, 0))),
        out_shape=jax.ShapeDtypeStruct(shape, F32),
        input_output_aliases=aliases,
        compiler_params=_cp(("arbitrary",), VMEM_BIG),
    )(*args)


def _pack_rows(parts, rows, dtype):
    flat = jnp.concatenate([p.reshape(-1).astype(dtype) for p in parts])
    return jnp.pad(flat, (0, rows * ROW - flat.shape[0])).reshape(rows, ROW)


def _unpack(flat, shapes):
    out, off = [], 0
    for sh in shapes:
        n = math.prod(sh)
        out.append(flat[off:off + n].reshape(sh))
        off += n
    return out


def _block_diag(w):
    eye = jnp.eye(LW // HD, dtype=w.dtype)
    return jnp.einsum("lhij,hg->lhigj", w, eye).reshape(w.shape[0], LW, LW)


def _diag_blocks(w):
    nbk = LW // HD
    w4 = w.reshape(nbk, HD, nbk, HD)
    return jnp.stack([w4[h, :, h, :] for h in range(nbk)])


def _rows8(rows, width):
    z = jnp.zeros((width,), F32)
    return jnp.stack(list(rows) + [z] * (8 - len(rows)))


def kernel(x, c, w_ada, b_ada, g_norm, w_ffn_up, w_ffn_down, w_in, b_fgate, conv_w, conv_b, w_rgate, b_rgate, w_igate, b_igate, lru_lambda, g_qk, g_mix_out, w_out, loss_target, m_w_ada, m_b_ada, m_g_norm, m_w_ffn_up, m_w_ffn_down, m_w_in, m_b_fgate, m_conv_w, m_conv_b, m_w_rgate, m_b_rgate, m_w_igate, m_b_igate, m_lru_lambda, m_g_qk, m_g_mix_out, m_w_out, v_w_ada, v_b_ada, v_g_norm, v_w_ffn_up, v_w_ffn_down, v_w_in, v_b_fgate, v_conv_w, v_conv_b, v_w_rgate, v_b_rgate, v_w_igate, v_b_igate, v_lru_lambda, v_g_qk, v_g_mix_out, v_w_out):
    B, S, _ = x.shape
    T = B * S
    xi, yi, ci = _coords()
    sidx = 2 * xi + yi
    didx = 4 * xi + 2 * yi + ci
    ada_cols = w_ada.shape[2]
    gn_cols = g_norm.shape[2]
    cw_cols = conv_w.shape[2]
    n_all = B * N_DEV

    blk1 = _pack_rows([c, jnp.pad(g_norm.reshape(-1), (0, 2 * ROW - g_norm.size)), conv_w], 8, F32)
    ag1 = _all_gather8(blk1, "ag_small_in", True)
    c_all = ag1[:, 0:B].reshape(n_all, D)
    chip_rows = ag1[0::2]
    g_norm_full = chip_rows[:, 2:4].reshape(N_SHARD, 2 * ROW)[:, :g_norm.size] \
        .reshape(N_SHARD, 2, 3, gn_cols).transpose(1, 2, 0, 3).reshape(2, 3, D)
    conv_w_full = chip_rows[:, 4].reshape(N_SHARD, 2, 4, cw_cols).transpose(1, 2, 0, 3).reshape(2, 4, LW)

    b_cols = lax.dynamic_slice(b_ada, (0, sidx * ada_cols), (2, ada_cols)).reshape(2, 1, ada_cols)
    mod_cols = _ada_fwd(c_all, w_ada, b_cols)
    mrows = (2 * n_all * ada_cols) // ROW
    ag2 = _all_gather8(mod_cols.reshape(mrows, ROW), "ag_mod", True)
    mod_sh = ag2[0::2].reshape(N_SHARD, 2, n_all, ada_cols)
    mod_me = lax.dynamic_slice(mod_sh, (0, 0, didx * B, 0), (N_SHARD, 2, B, ada_cols))
    mod_me = mod_me.transpose(1, 2, 0, 3).reshape(2, B, 3, 3, D)
    zrow = jnp.zeros((B, D), F32)
    mods = [[jnp.stack([mod_me[l, :, j, 0], 1.0 + mod_me[l, :, j, 1], 1.0 + mod_me[l, :, j, 2],
                        jnp.broadcast_to(g_norm_full[l, j], (B, D)), zrow, zrow, zrow, zrow], axis=1)
             for j in range(3)] for l in range(2)]

    wshards = (w_ffn_up.astype(BF16), w_ffn_down.astype(BF16), w_in.astype(BF16), w_out.astype(BF16))

    def ffn_pieces(l, j):
        return [(0, (l, j), True), (1, (l, j), False)]

    def mixer_pieces(l):
        return [(2, (l,), True), (3, (l,), False)]

    def ffn_weights(up, dn):
        return dict(up=up, dn=dn)

    def mixer_weights(g_in, g_out):
        return dict(inp=jnp.pad(g_in.transpose(1, 0, 2).reshape(D, N_IN), ((0, 0), (0, N_INP - N_IN))), out=g_out)

    wl = [dict(), dict()]
    wl[0][0] = ffn_weights(*_run_comm(_ag_comm(wshards, ffn_pieces(0, 0)), "ag_weights_0_0"))

    wr_d = _block_diag(w_rgate).astype(BF16)
    wi_d = _block_diag(w_igate).astype(BF16)
    cw8 = jnp.pad(conv_w_full, ((0, 0), (0, 4), (0, 0)))
    vp8 = jnp.stack([_rows8([conv_b[l], b_rgate[l], b_igate[l], lru_lambda[l]], LW) for l in range(2)])
    bfp = jnp.pad(b_fgate, ((0, 0), (0, 128 - NH)))[:, None, :] * jnp.ones((1, 8, 1), F32)
    gqk2 = jnp.tile(jnp.pad(g_qk, ((0, 0), (0, 6), (0, 0))), (1, 1, 2))
    gmix8 = jnp.pad(g_mix_out[:, None, :], ((0, 0), (0, 7), (0, 0)))

    x2 = x.reshape(T, D)
    tgt = loss_target.reshape(T, D)

    saved = []
    xc = x2
    for l in range(2):
        sv = {}
        sv["x0"] = xc
        w = wl[l]
        rest0 = _ag_comm(wshards, mixer_pieces(0) + ffn_pieces(0, 1)) if l == 0 else None
        xc, sv["g0"], sv["u0"], sv["f0"], got = _ffn_fwd(xc, mods[l][0], w[0]["up"], w[0]["dn"], l, 0, S, rest0)
        if l == 0:
            w["mix"] = mixer_weights(got[0], got[1])
            w[1] = ffn_weights(got[2], got[3])
        sv["x1"] = xc
        sv["h1"], proj = _mix_in_fwd(xc, mods[l][1], w["mix"]["inp"], l, S)
        sv["proj"] = proj
        sv["ylru"], sv["hl"] = _lru_fwd(proj, cw8, vp8, wr_d, wi_d, l, S)
        all1 = _ag_comm(wshards, ffn_pieces(1, 0) + mixer_pieces(1) + ffn_pieces(1, 1)) if l == 0 else None
        sv["osb"], sv["t1"], got = _sbq_fwd(proj, l, S, all1)
        if l == 0:
            wl[1][0] = ffn_weights(got[0], got[1])
            wl[1]["mix"] = mixer_weights(got[2], got[3])
            wl[1][1] = ffn_weights(got[4], got[5])
        cum = _fgate_fwd(proj, bfp, l, S)
        sv["ck"] = cum[:, :NH].reshape(B, S, NH).transpose(0, 2, 1).reshape(B, NH, S // TQ_(S), 1, TQ_(S))
        sv["ofx"], sv["nl"] = _foxq_fwd(proj, cum, sv["ck"], gqk2, l, S)
        xc, sv["y"], sv["mo"] = _mix_out_fwd(xc, sv["ylru"], sv["osb"], sv["ofx"], mods[l][1], gmix8, w["mix"]["out"], l, S)
        sv["x2"] = xc
        xc, sv["g2"], sv["u2"], sv["f2"], _ = _ffn_fwd(xc, mods[l][2], w[1]["up"], w[1]["dn"], l, 1, S)
        saved.append(sv)

    dxc, lpart = _loss_head(xc, tgt, S)
    loss = lax.psum(lpart[0, 0], ("x", "y", "c"))

    tf = wl[0][0]["up"].shape[-1]
    g_up_l = [[None, None], [None, None]]
    g_dn_l = [[None, None], [None, None]]
    g_in_l, g_out_l = [None, None], [None, None]
    dmods = [[None] * 3 for _ in range(2)]
    small = [dict() for _ in range(2)]
    cvec = jnp.reshape(ci, (1,)).astype(jnp.int32)
    scvec = jnp.stack([sidx, ci]).astype(jnp.int32)

    def ffn_groups(l, j):
        return [(0, "up", [g_up_l[l][j]], [(l, j)]), (1, "dn", [g_dn_l[l][j]], [(l, j)])]

    def mixer_groups(l):
        return [(2, "in", [g_in_l[l]], [(l,)]), (3, "out", [g_out_l[l]], [(l,)])]

    def rs_sibling_phase(groups, tag):
        recv_a = _rs_to_sibling([pb for _, _, ps, _ in groups for _, pb in ps], f"rs_to_sibling_{tag}")
        hs, off = [], 0
        for _, gname, ps, leads in groups:
            hs += _add_sibling([pf for pf, _ in ps], recv_a[off:off + len(ps)], cvec,
                               f"rs_add_sibling_{gname}_{'_'.join(map(str, leads[0]))}")
            off += len(ps)
        return groups, recv_a, hs

    def ffn_back(l, j, xin, dy, sv, sub, comm=None):
        dx, dmod, wacc, hb, dfb, ab, dgub, got = _ffn_bwd(
            xin, dy, mods[l][sub], sv[f"f{sub}"], sv[f"g{sub}"], sv[f"u{sub}"],
            wl[l][j]["up"], wl[l][j]["dn"], l, j, S, comm)
        g_up_l[l][j] = _mm_tn(hb, dgub, f"dw_up_{l}_{j}", tnb=tf, split_n=True, with_bf16=True)
        g_dn_l[l][j] = tuple(g.reshape(N_SHARD, -1, D)
                             for g in _mm_tn(ab, dfb, f"dw_dn_{l}_{j}", tma=tf, with_bf16=True))
        dmods[l][sub] = dmod
        small[l][f"gn{sub}"] = wacc[0]
        return dx, got

    batches = []
    for l in (1, 0):
        sv = saved[l]
        dxc, _ = ffn_back(l, 1, sv["x2"], dxc, sv, 2)
        dyl, dsb, dfx, dmo, dmod1, wacc_mo = _mix_out_bwd(
            dxc, sv["ylru"], sv["osb"], sv["ofx"], sv["mo"], mods[l][1], gmix8, wl[l]["mix"]["out"], l, S)
        small[l]["gmix"] = wacc_mo[0]
        g_out_l[l] = tuple(g.reshape(N_SHARD, -1, D) for g in _mm_tn(sv["y"], dmo, f"dw_out_{l}", with_bf16=True))
        dsq, dsk, dsv, got = _sbq_bwd(sv["proj"], dsb, sv["t1"], l, S,
                                       _rs_to_chips_comm(rs1[2]) if l == 0 else None)
        if l == 0:
            batches.append((rs1[0], rs1[1], got))
        dfq, dfk, dfv, dck, wacc_fx = _foxq_bwd(sv["proj"], dfx, sv["nl"], sv["ofx"], sv["ck"], gqk2, l, S)
        small[l]["gqk"] = wacc_fx[0:2, :HD] + wacc_fx[0:2, HD:]
        dcum = dck[:, :, :, 0, :].reshape(B, NH, S).transpose(0, 2, 1).reshape(T, NH)
        dff_, wacc_fg = _fgate_bwd(jnp.pad(dcum, ((0, 0), (0, 128 - NH))), sv["proj"], bfp, l, S)
        small[l]["bf"] = wacc_fg[0, :NH]
        dlx, dlg, dpr, dpi, ub, wacc_lru = _lru_bwd(dyl, sv["proj"], sv["hl"], cw8, vp8, wr_d, wi_d, l, S)
        small[l]["lru"] = wacc_lru
        small[l]["wr"] = _diag_blocks(_mm_tn(ub, dpr, f"dw_rgate_{l}"))
        small[l]["wi"] = _diag_blocks(_mm_tn(ub, dpi, f"dw_igate_{l}"))
        dproj = jnp.concatenate(
            [dlx, dlg, dsq, dsk, dsv, dfq, dfk, dfv, dff_], axis=1)
        g_in = _mm_tn(sv["h1"], dproj, f"dw_in_{l}", tnb=N_INP // 3)[:, :N_IN]
        g_in = g_in.reshape(D, N_SHARD, -1).transpose(1, 0, 2)
        g_in_l[l] = (g_in, g_in.astype(BF16))
        dxc, dmod_in, wacc_in = _mix_in_bwd(sv["x1"], dxc, mods[l][1], dproj, wl[l]["mix"]["inp"], l, S)
        dmods[l][1] = dmod_in + dmod1
        small[l]["gn1"] = wacc_in[0]
        if l == 1:
            dxc, _ = ffn_back(l, 0, sv["x0"], dxc, sv, 0)
            rs1 = rs_sibling_phase(ffn_groups(1, 0) + mixer_groups(1) + ffn_groups(1, 1), "1")
        else:
            late = rs_sibling_phase(mixer_groups(0) + ffn_groups(0, 1), "0_late")
            dxc, got = ffn_back(l, 0, sv["x0"], dxc, sv, 0, _rs_to_chips_comm(late[2]))
            batches.append((late[0], late[1], got))
    grad_x = dxc.reshape(B, S, D)

    dmod_loc = jnp.stack([jnp.stack([dmods[l][j][:, 0:3, :] for j in range(3)], axis=1) for l in range(2)])
    drows = 2 * B * 9
    blk3 = _pack_rows([dmod_loc], -(-drows // 8) * 8, F32)
    ag3 = _all_gather8(blk3, "ag_dmod", True)
    dmod_all = ag3[:, :drows].reshape(N_DEV, 2, B, 9 * D).transpose(1, 0, 2, 3).reshape(2, n_all, 9 * D)
    dmod_mine = lax.dynamic_slice(dmod_all, (0, 0, sidx * ada_cols), (2, n_all, ada_cols))
    grad_w_ada = _ada_bwd(c_all, dmod_mine)
    dmod_rows = jnp.pad(dmod_all.transpose(1, 0, 2).reshape(n_all, 2 * 9, D), ((0, 0), (0, 6), (0, 0)))
    grad_b_ada = _sum_lead(dmod_rows, "grad_b_ada")[:2 * 9].reshape(2, 9 * D)

    sm_parts = [
        jnp.stack([small[l]["bf"] for l in range(2)]),
        jnp.stack([small[l]["lru"][4] for l in range(2)]),
        jnp.stack([small[l]["wr"] for l in range(2)]),
        jnp.stack([small[l]["lru"][5] for l in range(2)]),
        jnp.stack([small[l]["wi"] for l in range(2)]),
        jnp.stack([small[l]["lru"][6] for l in range(2)]),
        jnp.stack([small[l]["lru"][7] for l in range(2)]),
        jnp.stack([small[l]["gqk"] for l in range(2)]),
        jnp.stack([small[l]["gmix"] for l in range(2)]),
        jnp.stack([jnp.stack([small[l][f"gn{j}"] for j in range(3)]) for l in range(2)]),
        jnp.stack([small[l]["lru"][0:4] for l in range(2)]),
    ]
    sm_shapes = [p.shape for p in sm_parts]
    sm_rows = -(-sum(p.size for p in sm_parts) // (8 * ROW)) * 8
    ag4 = _all_gather8(_pack_rows(sm_parts, sm_rows, F32), "ag_small_grads", True)
    sm_sum = _sum_lead(ag4, "sum_small_grads").reshape(-1)
    (g_bf, g_cb, g_wr, g_br, g_wi, g_bi, g_lam, g_gqk, g_gmix, g_gn_full, g_cw_full) = _unpack(sm_sum, sm_shapes)
    g_gn = lax.dynamic_slice(g_gn_full, (0, 0, sidx * gn_cols), (2, 3, gn_cols))
    g_cw = lax.dynamic_slice(g_cw_full, (0, 0, sidx * cw_cols), (2, 4, cw_cols))

    last = rs_sibling_phase(ffn_groups(0, 0), "0_first")
    batches.append((last[0], last[1], _run_comm(_rs_to_chips_comm(last[2]), "rs_to_chips_0_first")))
    shapes4 = [w_ffn_up.shape, w_ffn_down.shape, w_in.shape, w_out.shape]
    tensors, places, r2s = [None] * 4, [], []
    for groups, recv_a, recv_b in batches:
        k = 0
        for gi, gname, ps, leads in groups:
            for (pf, _), lead in zip(ps, leads):
                tensors[gi] = _add_chips_into(pf, recv_a[k], recv_b[k], scvec, tensors[gi], shapes4[gi], lead,
                                              f"rs_add_chips_{gname}_{'_'.join(map(str, lead))}")
                places.append((gi, lead))
                r2s.append(pf.shape[1] // 2)
                k += 1
    gw_up, gw_dn, gw_in, gw_out = _share_halves(tensors, places, r2s)

    def upd(w, g, m, v, name):
        sh = w.shape
        two = (w.size // sh[-1], sh[-1])
        dlt, mn, vn = _adamw(w.reshape(two), g.reshape(two), m.reshape(two), v.reshape(two), name)
        return dlt.reshape(sh), mn.reshape(sh), vn.reshape(sh)

    big = {
        "w_ada": (w_ada, grad_w_ada, m_w_ada, v_w_ada),
        "w_ffn_up": (w_ffn_up, gw_up, m_w_ffn_up, v_w_ffn_up),
        "w_ffn_down": (w_ffn_down, gw_dn, m_w_ffn_down, v_w_ffn_down),
        "w_in": (w_in, gw_in, m_w_in, v_w_in),
        "w_out": (w_out, gw_out, m_w_out, v_w_out),
    }
    res = {n: (t[1],) + upd(*t, f"adamw_{n}") for n, t in big.items()}

    smalls = {
        "b_ada": (b_ada, grad_b_ada, m_b_ada, v_b_ada),
        "g_norm": (g_norm, g_gn, m_g_norm, v_g_norm),
        "b_fgate": (b_fgate, g_bf, m_b_fgate, v_b_fgate),
        "conv_w": (conv_w, g_cw, m_conv_w, v_conv_w),
        "conv_b": (conv_b, g_cb, m_conv_b, v_conv_b),
        "w_rgate": (w_rgate, g_wr, m_w_rgate, v_w_rgate),
        "b_rgate": (b_rgate, g_br, m_b_rgate, v_b_rgate),
        "w_igate": (w_igate, g_wi, m_w_igate, v_w_igate),
        "b_igate": (b_igate, g_bi, m_b_igate, v_b_igate),
        "lru_lambda": (lru_lambda, g_lam, m_lru_lambda, v_lru_lambda),
        "g_qk": (g_qk, g_gqk, m_g_qk, v_g_qk),
        "g_mix_out": (g_mix_out, g_gmix, m_g_mix_out, v_g_mix_out),
    }
    names = list(smalls)
    shapes = [smalls[n][0].shape for n in names]
    prow = -(-sum(math.prod(s) for s in shapes) // (8 * ROW)) * 8
    packed = [_pack_rows([smalls[n][i].reshape(shapes[k]) for k, n in enumerate(names)], prow, F32) for i in range(4)]
    outs = _adamw(packed[0], packed[1], packed[2], packed[3], "adamw_small")
    un = [_unpack(o.reshape(-1), shapes) for o in outs]
    for k, n in enumerate(names):
        res[n] = (smalls[n][1].reshape(shapes[k]), un[0][k], un[1][k], un[2][k])

    order = ["w_ada", "b_ada", "g_norm", "w_ffn_up", "w_ffn_down", "w_in", "b_fgate", "conv_w", "conv_b",
             "w_rgate", "b_rgate", "w_igate", "b_igate", "lru_lambda", "g_qk", "g_mix_out", "w_out"]
    return (loss, grad_x, *[res[n][0] for n in order], *[res[n][1] for n in order],
            *[res[n][2] for n in order], *[res[n][3] for n in order])


def TQ_(S):
    return min(TQ, S)
```

```python
import math

import jax
import jax.numpy as jnp
from jax import lax
from jax.experimental import pallas as pl
from jax.experimental.pallas import tpu as pltpu

F32 = jnp.float32
BF16 = jnp.bfloat16
MESH = pl.DeviceIdType.MESH

D = 1024
HD = 64
LW = 512
NH = 4
AW = NH * HD
N_IN = 2564
N_INP = 2688
F_BLK = 2560 // 128
EPS = 1e-6
LRU_C = 8.0
SCALE = HD ** -0.5
NEG = -1e30
TQ = 256

ADAM_LR, ADAM_B1, ADAM_B2, ADAM_EPS, ADAM_WD, ADAM_STEP = 0.001, 0.9, 0.999, 1e-08, 0.01, 10

VMEM_BIG = 56 * 1024 * 1024
N_DEV = 8
N_SHARD = 4
ROW = 1024


def _cp(sem, vmem=None):
    return pltpu.CompilerParams(dimension_semantics=sem, vmem_limit_bytes=vmem)


def _dot(a, b):
    return jnp.dot(a, b, preferred_element_type=F32)


def _dot_nt(a, b):
    return lax.dot_general(a, b, (((1,), (1,)), ((), ())), preferred_element_type=F32)


def _dot_tn(a, b):
    return lax.dot_general(a, b, (((0,), (0,)), ((), ())), preferred_element_type=F32)


def _log1p(e):
    small = e * (1.0 - e * (0.5 - e * (1.0 / 3.0 - e * 0.25)))
    return jnp.where(e < 0.01, small, jnp.log(1.0 + e))


def _expm1_neg(x):
    small = x * (1.0 + x * 0.5 * (1.0 + x * (1.0 / 3.0) * (1.0 + x * 0.25 * (1.0 + x * 0.2))))
    return jnp.where(x > -0.05, small, jnp.exp(x) - 1.0)


def _sigmoid(x):
    return 1.0 / (1.0 + jnp.exp(-x))


_GELU_C = math.sqrt(2.0 / math.pi)


def _gelu_and_grad(x):
    x2 = x * x
    th = jnp.tanh(_GELU_C * (x + 0.044715 * x * x2))
    g = 0.5 * x * (1.0 + th)
    dg = 0.5 * (1.0 + th) + 0.5 * x * (1.0 - th * th) * _GELU_C * (1.0 + 3.0 * 0.044715 * x2)
    return g, dg


def _rms_rows(x):
    rstd = lax.rsqrt(jnp.mean(x * x, axis=-1, keepdims=True) + EPS)
    return x * rstd, rstd


def _rms_bwd(xn, rstd, dyn):
    return rstd * (dyn - xn * jnp.mean(dyn * xn, axis=-1, keepdims=True))


def _colsum(x):
    return jnp.sum(x, axis=0, keepdims=True)


def _rowsum(x):
    return jnp.sum(x, axis=1, keepdims=True)


def _split3(x):
    hi = x.astype(BF16)
    r = x - hi.astype(F32)
    mid = r.astype(BF16)
    lo = (r - mid.astype(F32)).astype(BF16)
    return hi, mid, lo


def _cumsum_mm(x, ones_tri, parts=3):
    ps = _split3(x)[:parts]
    acc = _dot(ps[0], ones_tri)
    for p in ps[1:]:
        acc = acc + _dot(p, ones_tri)
    return acc


def _tri(n, kind):
    r = lax.broadcasted_iota(jnp.int32, (n, n), 0)
    c = lax.broadcasted_iota(jnp.int32, (n, n), 1)
    m = {"row_gt_col": r > c, "row_le_col": r <= c, "row_lt_col": r < c}[kind]
    return jnp.where(m, 1.0, 0.0).astype(BF16)


def _normmod(x, mod_ref):
    xn, rstd = _rms_rows(x)
    h = xn * mod_ref[3:4, :] * mod_ref[1:2, :] + mod_ref[0:1, :]
    return h, xn, rstd


def _normmod_bwd(dh, xn, rstd, mod_ref, dmod_ref, wacc_ref):
    gn = mod_ref[3:4, :]
    sc = mod_ref[1:2, :]
    dmod_ref[0:1, :] += _colsum(dh)
    t = _colsum(dh * xn)
    dmod_ref[1:2, :] += t * gn
    wacc_ref[0:1, :] += t * sc
    return _rms_bwd(xn, rstd, dh * (gn * sc))


def _tile(n, want):
    t = min(n, want)
    while n % t:
        t //= 2
    return t


def _tile_div8(n, cap, mult=8):
    best = mult
    for t in range(mult, min(n, cap) + 1, mult):
        if n % t == 0:
            best = t
    assert n % best == 0
    return best


def _ffn_fwd(x, mod, wup, wdn, l, j, S, comm=None):
    T = x.shape[0]
    tf = wup.shape[-1]
    nk = 2
    tm = _tile(S, 512)
    tpb = S // tm
    nt = T // tm
    c_args, c_specs, c_outs, c_scr = _hosted(comm)
    n_ci, n_co = len(c_args), len(c_outs)

    def body(*refs):
        x_ref, mod_ref, wg_ref, wu_ref, wd_ref = refs[:5]
        c_in = refs[5:5 + n_ci]
        xo_ref, g_ref, u_ref, f_ref = refs[5 + n_ci:9 + n_ci]
        c_out = refs[9 + n_ci:9 + n_ci + n_co]
        h_sc, acc_sc = refs[9 + n_ci + n_co:11 + n_ci + n_co]
        c_sems = refs[11 + n_ci + n_co:]
        i = pl.program_id(0)
        k = pl.program_id(1)
        if comm is not None:
            @pl.when((i == 0) & (k == 0))
            def _():
                comm["start"](c_in, c_out, c_sems)

        @pl.when(k == 0)
        def _():
            h, _, _ = _normmod(x_ref[...], mod_ref)
            h_sc[...] = h.astype(BF16)
            acc_sc[...] = jnp.zeros_like(acc_sc)

        h = h_sc[...]
        g = _dot(h, wg_ref[...])
        u = _dot(h, wu_ref[...])
        g_ref[...] = g.astype(BF16)
        u_ref[...] = u.astype(BF16)
        a = (g * _sigmoid(g)) * u
        acc_sc[...] += _dot(a.astype(BF16), wd_ref[...])

        @pl.when(k == nk - 1)
        def _():
            f = acc_sc[...]
            f_ref[...] = f.astype(BF16)
            xo_ref[...] = x_ref[...] + (0.5 * mod_ref[2:3, :]) * f

        if comm is not None:
            @pl.when((i == nt - 1) & (k == nk - 1))
            def _():
                comm["finish"](c_in, c_out, c_sems)

    anyspec = pl.BlockSpec(memory_space=pl.ANY)
    out = pl.pallas_call(
        body, name=f"ffn_fwd_{l}_{j}",
        grid=(nt, nk),
        in_specs=[
            pl.BlockSpec((tm, D), lambda i, k: (i, 0)),
            pl.BlockSpec((None, 8, D), lambda i, k: (i // tpb, 0, 0)),
            pl.BlockSpec((None, D, tf), lambda i, k: (k, 0, 0)),
            pl.BlockSpec((None, D, tf), lambda i, k: (nk + k, 0, 0)),
            pl.BlockSpec((tf, D), lambda i, k: (k, 0)),
        ] + c_specs,
        out_specs=[
            pl.BlockSpec((tm, D), lambda i, k: (i, 0)),
            pl.BlockSpec((tm, tf), lambda i, k: (i, k)),
            pl.BlockSpec((tm, tf), lambda i, k: (i, k)),
            pl.BlockSpec((tm, D), lambda i, k: (i, 0)),
        ] + [anyspec] * n_co,
        out_shape=[
            jax.ShapeDtypeStruct((T, D), F32),
            jax.ShapeDtypeStruct((T, nk * tf), BF16),
            jax.ShapeDtypeStruct((T, nk * tf), BF16),
            jax.ShapeDtypeStruct((T, D), BF16),
        ] + c_outs,
        scratch_shapes=[pltpu.VMEM((tm, D), BF16), pltpu.VMEM((tm, D), F32)] + c_scr,
        compiler_params=_cp(("arbitrary", "arbitrary"), VMEM_BIG),
    )(x, mod, wup, wup, wdn, *c_args)
    return out[0], out[1], out[2], out[3], list(out[4:])


def _ffn_bwd(x, dy, mod, f, g, u, wup, wdn, l, j, S, comm=None):
    T = x.shape[0]
    tf = wup.shape[-1]
    nk = 2
    tm = _tile(S, 256)
    tpb = S // tm
    nt = T // tm
    c_args, c_specs, c_outs, c_scr = _hosted(comm)
    n_ci, n_co = len(c_args), len(c_outs)

    def body(*refs):
        x_ref, dy_ref, mod_ref, f_ref, g_ref, u_ref, wup_ref, wd_ref = refs[:8]
        c_in = refs[8:8 + n_ci]
        dx_ref, dmod_ref, wacc_ref, h_ref, df_ref, a_ref, dgu_ref = refs[8 + n_ci:15 + n_ci]
        c_out = refs[15 + n_ci:15 + n_ci + n_co]
        c_sems = refs[15 + n_ci + n_co:]
        i = pl.program_id(0)

        @pl.when(i == 0)
        def _():
            wacc_ref[...] = jnp.zeros_like(wacc_ref)
            if comm is not None:
                comm["start"](c_in, c_out, c_sems)

        @pl.when(i % tpb == 0)
        def _():
            dmod_ref[...] = jnp.zeros_like(dmod_ref)

        dy_ = dy_ref[...]
        h, xn, rstd = _normmod(x_ref[...], mod_ref)
        h_ref[...] = h.astype(BF16)
        dfb = ((0.5 * mod_ref[2:3, :]) * dy_).astype(BF16)
        df_ref[...] = dfb
        dmod_ref[2:3, :] += _colsum(0.5 * f_ref[...].astype(F32) * dy_)
        dh = None
        for k in range(nk):
            cols = slice(k * tf, (k + 1) * tf)
            da = _dot_nt(dfb, wd_ref[cols, :])
            gg = g_ref[:, cols].astype(F32)
            uu = u_ref[:, cols].astype(F32)
            sig = _sigmoid(gg)
            s = gg * sig
            a_ref[:, cols] = (s * uu).astype(BF16)
            du = (da * s).astype(BF16)
            dg = (da * uu * (sig * (1.0 + gg * (1.0 - sig)))).astype(BF16)
            dgu_ref[0, :, cols] = dg
            dgu_ref[1, :, cols] = du
            part = _dot_nt(dg, wup_ref[k]) + _dot_nt(du, wup_ref[nk + k])
            dh = part if dh is None else dh + part
        dx_ref[...] = dy_ + _normmod_bwd(dh, xn, rstd, mod_ref, dmod_ref, wacc_ref)

        if comm is not None:
            @pl.when(i == nt - 1)
            def _():
                comm["finish"](c_in, c_out, c_sems)

    once = pl.Buffered(1)
    anyspec = pl.BlockSpec(memory_space=pl.ANY)
    out = pl.pallas_call(
        body, name=f"ffn_bwd_{l}_{j}",
        grid=(nt,),
        in_specs=[
            pl.BlockSpec((tm, D), lambda i: (i, 0)),
            pl.BlockSpec((tm, D), lambda i: (i, 0)),
            pl.BlockSpec((None, 8, D), lambda i: (i // tpb, 0, 0)),
            pl.BlockSpec((tm, D), lambda i: (i, 0)),
            pl.BlockSpec((tm, nk * tf), lambda i: (i, 0)),
            pl.BlockSpec((tm, nk * tf), lambda i: (i, 0)),
            pl.BlockSpec((2 * nk, D, tf), lambda i: (0, 0, 0), pipeline_mode=once),
            pl.BlockSpec((nk * tf, D), lambda i: (0, 0), pipeline_mode=once),
        ] + c_specs,
        out_specs=[
            pl.BlockSpec((tm, D), lambda i: (i, 0)),
            pl.BlockSpec((None, 8, D), lambda i: (i // tpb, 0, 0)),
            pl.BlockSpec((8, D), lambda i: (0, 0)),
            pl.BlockSpec((tm, D), lambda i: (i, 0)),
            pl.BlockSpec((tm, D), lambda i: (i, 0)),
            pl.BlockSpec((tm, nk * tf), lambda i: (i, 0)),
            pl.BlockSpec((2, tm, nk * tf), lambda i: (0, i, 0)),
        ] + [anyspec] * n_co,
        out_shape=[
            jax.ShapeDtypeStruct((T, D), F32),
            jax.ShapeDtypeStruct((T // S, 8, D), F32),
            jax.ShapeDtypeStruct((8, D), F32),
            jax.ShapeDtypeStruct((T, D), BF16),
            jax.ShapeDtypeStruct((T, D), BF16),
            jax.ShapeDtypeStruct((T, nk * tf), BF16),
            jax.ShapeDtypeStruct((2, T, nk * tf), BF16),
        ] + c_outs,
        scratch_shapes=c_scr,
        compiler_params=_cp(("arbitrary",), VMEM_BIG),
    )(x, dy, mod, f, g, u, wup, wdn, *c_args)
    return tuple(out[:7]) + (list(out[7:]),)


def _mm_tn(a, b, name, tma=None, tnb=None, split_n=False, with_bf16=False):
    T, M = a.shape
    b3 = b if b.ndim == 3 else b[None]
    nb, _, N = b3.shape
    tma = tma or M
    tnb = tnb or N
    npb = N // tnb
    tt = _tile(T, 1024)
    nt = T // tt

    def body(a_ref, b_ref, o_ref, *ob_ref):
        @pl.when(pl.program_id(2) == 0)
        def _():
            o_ref[...] = jnp.zeros_like(o_ref)

        o_ref[...] += _dot_tn(a_ref[...], b_ref[...])

        if with_bf16:
            @pl.when(pl.program_id(2) == nt - 1)
            def _():
                ob_ref[0][...] = o_ref[...].astype(BF16)

    if split_n:
        shape = (nb * npb, M, tnb)
        out_spec = pl.BlockSpec((None, tma, tnb), lambda m, n, t: (n, m, 0))
    else:
        assert nb == 1
        shape = (M, N)
        out_spec = pl.BlockSpec((tma, tnb), lambda m, n, t: (m, n))
    dts = (F32, BF16) if with_bf16 else (F32,)
    out = pl.pallas_call(
        body, name=name,
        grid=(M // tma, nb * npb, nt),
        in_specs=[pl.BlockSpec((tt, tma), lambda m, n, t: (t, m)),
                  pl.BlockSpec((None, tt, tnb), lambda m, n, t: (n // npb, t, n % npb))],
        out_specs=[out_spec] * len(dts),
        out_shape=[jax.ShapeDtypeStruct(shape, dt) for dt in dts],
        compiler_params=_cp(("arbitrary", "arbitrary", "arbitrary"), VMEM_BIG),
    )(a, b3)
    return tuple(out) if with_bf16 else out[0]


def _mix_in_fwd(x, mod, winp, l, S):
    T = x.shape[0]
    tm = _tile(S, 512)
    tpb = S // tm

    def body(x_ref, mod_ref, w_ref, h_ref, p_ref):
        h, _, _ = _normmod(x_ref[...], mod_ref)
        hb = h.astype(BF16)
        h_ref[...] = hb
        p_ref[...] = _dot(hb, w_ref[...])

    return pl.pallas_call(
        body, name=f"mix_in_fwd_{l}",
        grid=(T // tm,),
        in_specs=[pl.BlockSpec((tm, D), lambda i: (i, 0)),
                  pl.BlockSpec((None, 8, D), lambda i: (i // tpb, 0, 0)),
                  pl.BlockSpec((D, N_INP), lambda i: (0, 0))],
        out_specs=[pl.BlockSpec((tm, D), lambda i: (i, 0)),
                   pl.BlockSpec((tm, N_INP), lambda i: (i, 0))],
        out_shape=[jax.ShapeDtypeStruct((T, D), BF16), jax.ShapeDtypeStruct((T, N_INP), F32)],
        compiler_params=_cp(("arbitrary",), VMEM_BIG),
    )(x, mod, winp)


def _mix_in_bwd(x, dres, mod, dproj, winp, l, S):
    T = x.shape[0]
    tm = _tile(S, 512)
    tpb = S // tm

    def body(x_ref, dr_ref, mod_ref, dp_ref, w_ref, dx_ref, dmod_ref, wacc_ref):
        i = pl.program_id(0)

        @pl.when(i == 0)
        def _():
            wacc_ref[...] = jnp.zeros_like(wacc_ref)

        @pl.when(i % tpb == 0)
        def _():
            dmod_ref[...] = jnp.zeros_like(dmod_ref)

        dh = _dot_nt(dp_ref[...], w_ref[...])
        _, xn, rstd = _normmod(x_ref[...], mod_ref)
        dx_ref[...] = dr_ref[...] + _normmod_bwd(dh, xn, rstd, mod_ref, dmod_ref, wacc_ref)

    return pl.pallas_call(
        body, name=f"mix_in_bwd_{l}",
        grid=(T // tm,),
        in_specs=[pl.BlockSpec((tm, D), lambda i: (i, 0)),
                  pl.BlockSpec((tm, D), lambda i: (i, 0)),
                  pl.BlockSpec((None, 8, D), lambda i: (i // tpb, 0, 0)),
                  pl.BlockSpec((tm, N_INP), lambda i: (i, 0)),
                  pl.BlockSpec((D, N_INP), lambda i: (0, 0))],
        out_specs=[pl.BlockSpec((tm, D), lambda i: (i, 0)),
                   pl.BlockSpec((None, 8, D), lambda i: (i // tpb, 0, 0)),
                   pl.BlockSpec((8, D), lambda i: (0, 0))],
        out_shape=[jax.ShapeDtypeStruct((T, D), F32),
                   jax.ShapeDtypeStruct((T // S, 8, D), F32),
                   jax.ShapeDtypeStruct((8, D), F32)],
        compiler_params=_cp(("arbitrary",), VMEM_BIG),
    )(x, dres, mod, dproj, winp)


_GROUPS = ((0, LW), (LW, LW + AW), (LW + AW, D))


def _mix_out_fwd(x, ylru, osb, ofox, mod, gmix, wout, l, S):
    T = x.shape[0]
    tm = _tile(S, 512)
    tpb = S // tm

    def body(x_ref, yl_ref, sb_ref, fx_ref, mod_ref, gm_ref, w_ref, xo_ref, y_ref, mo_ref):
        for src, (lo, hi) in zip((yl_ref, sb_ref, fx_ref), _GROUPS):
            vn, _ = _rms_rows(src[...])
            y_ref[:, lo:hi] = (vn * gm_ref[0:1, lo:hi]).astype(BF16)
        mo = _dot(y_ref[...], w_ref[...])
        mo_ref[...] = mo.astype(BF16)
        xo_ref[...] = x_ref[...] + mod_ref[2:3, :] * mo

    return pl.pallas_call(
        body, name=f"mix_out_fwd_{l}",
        grid=(T // tm,),
        in_specs=[pl.BlockSpec((tm, D), lambda i: (i, 0)),
                  pl.BlockSpec((tm, LW), lambda i: (i, 0)),
                  pl.BlockSpec((tm, AW), lambda i: (i, 0)),
                  pl.BlockSpec((tm, AW), lambda i: (i, 0)),
                  pl.BlockSpec((None, 8, D), lambda i: (i // tpb, 0, 0)),
                  pl.BlockSpec((None, 8, D), lambda i: (l, 0, 0)),
                  pl.BlockSpec((D, D), lambda i: (0, 0))],
        out_specs=[pl.BlockSpec((tm, D), lambda i: (i, 0)),
                   pl.BlockSpec((tm, D), lambda i: (i, 0)),
                   pl.BlockSpec((tm, D), lambda i: (i, 0))],
        out_shape=[jax.ShapeDtypeStruct((T, D), F32),
                   jax.ShapeDtypeStruct((T, D), BF16),
                   jax.ShapeDtypeStruct((T, D), BF16)],
        compiler_params=_cp(("arbitrary",), VMEM_BIG),
    )(x, ylru, osb, ofox, mod, gmix, wout)


def _mix_out_bwd(dx2, ylru, osb, ofox, mo, mod, gmix, wout, l, S):
    T = dx2.shape[0]
    tm = _tile(S, 512)
    tpb = S // tm

    def body(dx_ref, yl_ref, sb_ref, fx_ref, mo_ref, mod_ref, gm_ref, w_ref,
             dyl_ref, dsb_ref, dfx_ref, dmo_ref, dmod_ref, wacc_ref):
        i = pl.program_id(0)

        @pl.when(i == 0)
        def _():
            wacc_ref[...] = jnp.zeros_like(wacc_ref)

        @pl.when(i % tpb == 0)
        def _():
            dmod_ref[...] = jnp.zeros_like(dmod_ref)

        dx = dx_ref[...]
        dmod_ref[2:3, :] += _colsum(mo_ref[...].astype(F32) * dx)
        dmo = (mod_ref[2:3, :] * dx).astype(BF16)
        dmo_ref[...] = dmo
        dy = _dot_nt(dmo, w_ref[...])
        for src, dst, (lo, hi) in zip((yl_ref, sb_ref, fx_ref), (dyl_ref, dsb_ref, dfx_ref), _GROUPS):
            vn, rstd = _rms_rows(src[...])
            dyg = dy[:, lo:hi]
            wacc_ref[0:1, lo:hi] += _colsum(dyg * vn)
            dst[...] = _rms_bwd(vn, rstd, dyg * gm_ref[0:1, lo:hi])

    return pl.pallas_call(
        body, name=f"mix_out_bwd_{l}",
        grid=(T // tm,),
        in_specs=[pl.BlockSpec((tm, D), lambda i: (i, 0)),
                  pl.BlockSpec((tm, LW), lambda i: (i, 0)),
                  pl.BlockSpec((tm, AW), lambda i: (i, 0)),
                  pl.BlockSpec((tm, AW), lambda i: (i, 0)),
                  pl.BlockSpec((tm, D), lambda i: (i, 0)),
                  pl.BlockSpec((None, 8, D), lambda i: (i // tpb, 0, 0)),
                  pl.BlockSpec((None, 8, D), lambda i: (l, 0, 0)),
                  pl.BlockSpec((D, D), lambda i: (0, 0))],
        out_specs=[pl.BlockSpec((tm, LW), lambda i: (i, 0)),
                   pl.BlockSpec((tm, AW), lambda i: (i, 0)),
                   pl.BlockSpec((tm, AW), lambda i: (i, 0)),
                   pl.BlockSpec((tm, D), lambda i: (i, 0)),
                   pl.BlockSpec((None, 8, D), lambda i: (i // tpb, 0, 0)),
                   pl.BlockSpec((8, D), lambda i: (0, 0))],
        out_shape=[jax.ShapeDtypeStruct((T, LW), F32),
                   jax.ShapeDtypeStruct((T, AW), F32),
                   jax.ShapeDtypeStruct((T, AW), F32),
                   jax.ShapeDtypeStruct((T, D), BF16),
                   jax.ShapeDtypeStruct((T // S, 8, D), F32),
                   jax.ShapeDtypeStruct((8, D), F32)],
        compiler_params=_cp(("arbitrary",), VMEM_BIG),
    )(dx2, ylru, osb, ofox, mo, mod, gmix, wout)


def _loss_head(y, tgt, S):
    T = y.shape[0]
    tm = _tile(S, 512)

    def body(y_ref, t_ref, dy_ref, l_ref):
        @pl.when(pl.program_id(0) == 0)
        def _():
            l_ref[...] = jnp.zeros_like(l_ref)

        d = y_ref[...] - t_ref[...]
        dy_ref[...] = d * (1.0 / D)
        l_ref[...] += (0.5 / D) * _rowsum(_colsum(d * d))

    return pl.pallas_call(
        body, name="loss_head",
        grid=(T // tm,),
        in_specs=[pl.BlockSpec((tm, D), lambda i: (i, 0)), pl.BlockSpec((tm, D), lambda i: (i, 0))],
        out_specs=[pl.BlockSpec((tm, D), lambda i: (i, 0)), pl.BlockSpec((8, 128), lambda i: (0, 0))],
        out_shape=[jax.ShapeDtypeStruct((T, D), F32), jax.ShapeDtypeStruct((8, 128), F32)],
        compiler_params=_cp(("arbitrary",)),
    )(y, tgt)


def _lru_gates(u, vp_ref, wr_ref, wi_ref):
    ub = u.astype(BF16)
    r = _sigmoid(_dot(ub, wr_ref[...]) + vp_ref[1:2, :])
    ig = _sigmoid(_dot(ub, wi_ref[...]) + vp_ref[2:3, :])
    lam = vp_ref[3:4, :]
    sp = jnp.maximum(-lam, 0.0) + _log1p(jnp.exp(-jnp.abs(lam)))
    log_a = (-LRU_C) * r * sp
    a = jnp.exp(log_a)
    mult = jnp.sqrt(-_expm1_neg(2.0 * log_a))
    return ub, r, ig, sp, a, mult


def _conv_taps(x, xp, row, cw_ref):
    xs = [x]
    for d in (1, 2, 3):
        xs.append(jnp.where(row >= d, pltpu.roll(x, d, 0), pltpu.roll(xp, d, 0)))
    u = xs[0] * cw_ref[3:4, :]
    for d in (1, 2, 3):
        u = u + xs[d] * cw_ref[3 - d:4 - d, :]
    return xs, u


def _lru_fwd(proj, cw, vp, wr, wi, l, S):
    T = proj.shape[0]
    ts = _tile(S, 256)
    nb = S // ts

    def body(x_ref, lg_ref, cw_ref, vp_ref, wr_ref, wi_ref, y_ref, h_ref, xp_sc, hc_sc):
        @pl.when(pl.program_id(1) == 0)
        def _():
            xp_sc[...] = jnp.zeros_like(xp_sc)
            hc_sc[...] = jnp.zeros_like(hc_sc)

        row = lax.broadcasted_iota(jnp.int32, (ts, LW), 0)
        x = x_ref[...]
        _, u = _conv_taps(x, xp_sc[...], row, cw_ref)
        u = u + vp_ref[0:1, :]
        xp_sc[...] = x
        _, _, ig, _, a, mult = _lru_gates(u, vp_ref, wr_ref, wi_ref)
        bv = mult * (ig * u)
        av = a
        d = 1
        while d < ts:
            a_s = jnp.where(row >= d, pltpu.roll(av, d, 0), 1.0)
            b_s = jnp.where(row >= d, pltpu.roll(bv, d, 0), 0.0)
            bv = av * b_s + bv
            av = av * a_s
            d *= 2
        h = bv + av * hc_sc[7:8, :]
        hc_sc[...] = h[ts - 8:ts, :]
        h_ref[...] = h
        gl, _ = _gelu_and_grad(lg_ref[...])
        y_ref[...] = h * gl

    return pl.pallas_call(
        body, name=f"lru_fwd_{l}",
        grid=(T // S, nb),
        in_specs=[pl.BlockSpec((ts, LW), lambda b, j: (b * nb + j, 0)),
                  pl.BlockSpec((ts, LW), lambda b, j: (b * nb + j, 1)),
                  pl.BlockSpec((None, 8, LW), lambda b, j: (l, 0, 0)),
                  pl.BlockSpec((None, 8, LW), lambda b, j: (l, 0, 0)),
                  pl.BlockSpec((None, LW, LW), lambda b, j: (l, 0, 0)),
                  pl.BlockSpec((None, LW, LW), lambda b, j: (l, 0, 0))],
        out_specs=[pl.BlockSpec((ts, LW), lambda b, j: (b * nb + j, 0)),
                   pl.BlockSpec((ts, LW), lambda b, j: (b * nb + j, 0))],
        out_shape=[jax.ShapeDtypeStruct((T, LW), F32), jax.ShapeDtypeStruct((T, LW), F32)],
        scratch_shapes=[pltpu.VMEM((ts, LW), F32), pltpu.VMEM((8, LW), F32)],
        compiler_params=_cp(("arbitrary", "arbitrary")),
    )(proj, proj, cw, vp, wr, wi)


def _lru_bwd(dyl, proj, h, cw, vp, wr, wi, l, S):
    T = proj.shape[0]
    ts = _tile(S, 256)
    nb = S // ts

    def body(dy_ref, x_ref, xprev_ref, lg_ref, h_ref, hprev_ref, cw_ref, vp_ref, wr_ref, wi_ref,
             dx_ref, dlg_ref, dpr_ref, dpi_ref, ub_ref, wacc_ref, gc_sc, af_sc, dun_sc):
        b = pl.program_id(0)
        j = pl.program_id(1)
        first = j == nb - 1

        @pl.when((b == 0) & (j == 0))
        def _():
            wacc_ref[...] = jnp.zeros_like(wacc_ref)

        @pl.when(j == 0)
        def _():
            gc_sc[...] = jnp.zeros_like(gc_sc)
            af_sc[...] = jnp.ones_like(af_sc)
            dun_sc[...] = jnp.zeros_like(dun_sc)

        row = lax.broadcasted_iota(jnp.int32, (ts, LW), 0)
        keep = jnp.where(first, 0.0, 1.0)
        x = x_ref[...]
        xs, u = _conv_taps(x, xprev_ref[...] * keep, row, cw_ref)
        u = u + vp_ref[0:1, :]
        ub, r, ig, sp, a, mult = _lru_gates(u, vp_ref, wr_ref, wi_ref)
        ub_ref[...] = ub
        hh = h_ref[...]
        h_m1 = jnp.where(row >= 1, pltpu.roll(hh, 1, 0), pltpu.roll(hprev_ref[...] * keep, 1, 0))
        dy = dy_ref[...]
        gl, dgl = _gelu_and_grad(lg_ref[...])
        dlg_ref[...] = (dy * hh * dgl).astype(BF16)
        bv = dy * gl
        av = jnp.where(row < ts - 1, pltpu.roll(a, ts - 1, 0), af_sc[0:1, :])
        d = 1
        while d < ts:
            a_s = jnp.where(row < ts - d, pltpu.roll(av, ts - d, 0), 1.0)
            b_s = jnp.where(row < ts - d, pltpu.roll(bv, ts - d, 0), 0.0)
            bv = av * b_s + bv
            av = av * a_s
            d *= 2
        gt = bv + av * gc_sc[0:1, :]
        gc_sc[...] = gt[0:8, :]
        af_sc[...] = a[0:8, :]
        da = gt * h_m1
        d_ig = gt * mult * u
        d_mult = gt * ig * u
        du = gt * mult * ig
        dlog_a = da * a - d_mult * (a * a) / mult
        dpre_r = (dlog_a * ((-LRU_C) * sp)) * r * (1.0 - r)
        dpre_i = d_ig * ig * (1.0 - ig)
        lam = vp_ref[3:4, :]
        wacc_ref[7:8, :] += _colsum(dlog_a * r) * (LRU_C * _sigmoid(-lam))
        wacc_ref[5:6, :] += _colsum(dpre_r)
        wacc_ref[6:7, :] += _colsum(dpre_i)
        dprb = dpre_r.astype(BF16)
        dpib = dpre_i.astype(BF16)
        dpr_ref[...] = dprb
        dpi_ref[...] = dpib
        du = du + _dot_nt(dprb, wr_ref[...]) + _dot_nt(dpib, wi_ref[...])
        wacc_ref[4:5, :] += _colsum(du)
        dun = dun_sc[...]
        dx = du * cw_ref[3:4, :]
        wacc_ref[3:4, :] += _colsum(du * xs[0])
        for dd in (1, 2, 3):
            du_s = jnp.where(row < ts - dd, pltpu.roll(du, ts - dd, 0), pltpu.roll(dun, ts - dd, 0))
            dx = dx + du_s * cw_ref[3 - dd:4 - dd, :]
            wacc_ref[3 - dd:4 - dd, :] += _colsum(du * xs[dd])
        dun_sc[...] = du
        dx_ref[...] = dx.astype(BF16)

    def tb(b, j):
        return b * nb + (nb - 1 - j)

    def tbp(b, j):
        return b * nb + jnp.maximum(nb - 2 - j, 0)

    return pl.pallas_call(
        body, name=f"lru_bwd_{l}",
        grid=(T // S, nb),
        in_specs=[pl.BlockSpec((ts, LW), lambda b, j: (tb(b, j), 0)),
                  pl.BlockSpec((ts, LW), lambda b, j: (tb(b, j), 0)),
                  pl.BlockSpec((ts, LW), lambda b, j: (tbp(b, j), 0)),
                  pl.BlockSpec((ts, LW), lambda b, j: (tb(b, j), 1)),
                  pl.BlockSpec((ts, LW), lambda b, j: (tb(b, j), 0)),
                  pl.BlockSpec((ts, LW), lambda b, j: (tbp(b, j), 0)),
                  pl.BlockSpec((None, 8, LW), lambda b, j: (l, 0, 0)),
                  pl.BlockSpec((None, 8, LW), lambda b, j: (l, 0, 0)),
                  pl.BlockSpec((None, LW, LW), lambda b, j: (l, 0, 0)),
                  pl.BlockSpec((None, LW, LW), lambda b, j: (l, 0, 0))],
        out_specs=[pl.BlockSpec((ts, LW), lambda b, j: (tb(b, j), 0)),
                   pl.BlockSpec((ts, LW), lambda b, j: (tb(b, j), 0)),
                   pl.BlockSpec((ts, LW), lambda b, j: (tb(b, j), 0)),
                   pl.BlockSpec((ts, LW), lambda b, j: (tb(b, j), 0)),
                   pl.BlockSpec((ts, LW), lambda b, j: (tb(b, j), 0)),
                   pl.BlockSpec((8, LW), lambda b, j: (0, 0))],
        out_shape=[jax.ShapeDtypeStruct((T, LW), BF16),
                   jax.ShapeDtypeStruct((T, LW), BF16),
                   jax.ShapeDtypeStruct((T, LW), BF16),
                   jax.ShapeDtypeStruct((T, LW), BF16),
                   jax.ShapeDtypeStruct((T, LW), BF16),
                   jax.ShapeDtypeStruct((8, LW), F32)],
        scratch_shapes=[pltpu.VMEM((8, LW), F32), pltpu.VMEM((8, LW), F32), pltpu.VMEM((ts, LW), F32)],
        compiler_params=_cp(("arbitrary", "arbitrary")),
    )(dyl, proj, proj, proj, h, h, cw, vp, wr, wi)


def _fgate_fwd(proj, bfp, l, S):
    T = proj.shape[0]

    def body(x_ref, b_ref, o_ref):
        z = x_ref[...] + b_ref[0:1, :]
        v = jnp.minimum(z, 0.0) - _log1p(jnp.exp(-jnp.abs(z)))
        row = lax.broadcasted_iota(jnp.int32, (S, 128), 0)
        d = 1
        while d < S:
            v = v + jnp.where(row >= d, pltpu.roll(v, d, 0), 0.0)
            d *= 2
        o_ref[...] = v

    return pl.pallas_call(
        body, name=f"fgate_fwd_{l}",
        grid=(T // S,),
        in_specs=[pl.BlockSpec((S, 128), lambda b: (b, F_BLK)),
                  pl.BlockSpec((None, 8, 128), lambda b: (l, 0, 0))],
        out_specs=pl.BlockSpec((S, 128), lambda b: (b, 0)),
        out_shape=jax.ShapeDtypeStruct((T, 128), F32),
        compiler_params=_cp(("arbitrary",)),
    )(proj, bfp)


def _fgate_bwd(dcum, proj, bfp, l, S):
    T = proj.shape[0]

    def body(d_ref, x_ref, b_ref, o_ref, wacc_ref):
        @pl.when(pl.program_id(0) == 0)
        def _():
            wacc_ref[...] = jnp.zeros_like(wacc_ref)

        v = d_ref[...]
        row = lax.broadcasted_iota(jnp.int32, (S, 128), 0)
        d = 1
        while d < S:
            v = v + jnp.where(row < S - d, pltpu.roll(v, S - d, 0), 0.0)
            d *= 2
        z = x_ref[...] + b_ref[0:1, :]
        dz = v * _sigmoid(-z)
        o_ref[...] = dz.astype(BF16)
        wacc_ref[0:1, :] += _colsum(dz)

    return pl.pallas_call(
        body, name=f"fgate_bwd_{l}",
        grid=(T // S,),
        in_specs=[pl.BlockSpec((S, 128), lambda b: (b, 0)),
                  pl.BlockSpec((S, 128), lambda b: (b, F_BLK)),
                  pl.BlockSpec((None, 8, 128), lambda b: (l, 0, 0))],
        out_specs=[pl.BlockSpec((S, 128), lambda b: (b, 0)), pl.BlockSpec((8, 128), lambda b: (0, 0))],
        out_shape=[jax.ShapeDtypeStruct((T, 128), BF16), jax.ShapeDtypeStruct((8, 128), F32)],
        compiler_params=_cp(("arbitrary",)),
    )(dcum, proj, bfp)


SBQ_BLK, SBK_BLK, SBV_BLK = 8, 10, 12
FXQ_BLK, FXK_BLK, FXV_BLK = 14, 16, 18
PAIR = 2 * HD


def _lane_masks():
    lane = lax.broadcasted_iota(jnp.int32, (1, PAIR), 1)
    return lane, lane < HD


def _pair_select(m0, a0, a1):
    return jnp.where(m0, a0, a1)


def _pair_split(x, m0):
    return jnp.where(m0, x, 0.0).astype(BF16), jnp.where(m0, 0.0, x).astype(BF16)


def _pair_mean(x, m0):
    s0 = _rowsum(jnp.where(m0, x, 0.0))
    s1 = _rowsum(x) - s0
    return jnp.where(m0, s0, s1) * (1.0 / HD)


def _pair_rms(x, m0):
    rstd = lax.rsqrt(_pair_mean(x * x, m0) + EPS)
    return x * rstd, rstd


def _pair_rms_bwd(xn, rstd, dyn, m0):
    return rstd * (dyn - xn * _pair_mean(dyn * xn, m0))


def _logsig2(z):
    l1p = jnp.log(1.0 + jnp.exp(-jnp.abs(z)))
    lb = jnp.minimum(z, 0.0) - l1p
    return lb, lb - z


def _rows(ref, blk, size):
    return ref[pl.ds(pl.multiple_of(blk * size, size), size), :]


def _loop_by_two(n, body, init):
    c = lax.fori_loop(0, n // 2, lambda i, cc: body(2 * i + 1, body(2 * i, cc)), init)
    return lax.cond(n % 2 == 1, lambda cc: body(n - 1, cc), lambda cc: cc, c)


def _transpose_blocks(src_ref, dst_sc, nblk, blk):
    for kb in range(nblk):
        dst_sc[kb] = src_ref[kb * blk:(kb + 1) * blk, :].astype(F32).T.astype(BF16)


def _sbq_fwd(proj, l, S, comm=None):
    T = proj.shape[0]
    tb = TQ_(S)
    nb = S // tb
    nbat = T // S
    c_args, c_specs, c_outs, c_scr = _hosted(comm)
    n_ci, n_co = len(c_args), len(c_outs)

    def body(*refs):
        q_ref, k_ref, v_ref = refs[:3]
        c_in = refs[3:3 + n_ci]
        o_ref, t1_ref = refs[3 + n_ci:5 + n_ci]
        c_out = refs[5 + n_ci:5 + n_ci + n_co]
        kt_sc, vb_sc = refs[5 + n_ci + n_co:7 + n_ci + n_co]
        c_sems = refs[7 + n_ci + n_co:]
        step = pl.program_id(0) * 2 + pl.program_id(1)
        if comm is not None:
            @pl.when(step == 0)
            def _():
                comm["start"](c_in, c_out, c_sems)

        _transpose_blocks(k_ref, kt_sc, nb, tb)
        vb_sc[...] = v_ref[...].astype(BF16)
        lane, m0 = _lane_masks()
        tri = _tri(tb, "row_gt_col")
        past = lax.broadcasted_iota(jnp.int32, (tb, tb), 1) < lax.broadcasted_iota(jnp.int32, (tb, tb), 0)

        def qloop(qb, carry):
            qh = _pair_split(_rows(q_ref, qb, tb) * SCALE, m0)

            def scores(kb):
                return tuple(_dot(qh[h], kt_sc[kb]) for h in range(2))

            def block(kb, kb_next, z, c, masked):
                mid = []
                for h in range(2):
                    lb, l1 = _logsig2(z[h])
                    if masked:
                        l1 = jnp.where(past, l1, 0.0)
                    mid.append((lb, l1, _cumsum_mm(l1, tri, parts=2)))
                z_next = scores(kb_next)
                pv, runs = [], []
                for h in range(2):
                    lb, l1, cs = mid[h]
                    w = jnp.exp(lb + (cs + c[h][1]))
                    if masked:
                        w = jnp.where(past, w, 0.0)
                    pv.append(_dot(w.astype(BF16), _rows(vb_sc, kb, tb)))
                    runs.append(c[h][1] + (cs[:, 0:1] + l1[:, 0:1]))
                return z_next, tuple((c[h][0] + pv[h], runs[h]) for h in range(2))

            zero = (jnp.zeros((tb, PAIR), F32), jnp.zeros((tb, 1), F32))
            z, c = block(qb, jnp.maximum(qb - 1, 0), scores(qb), (zero, zero), True)

            def off_diag(i, zc):
                kb = qb - 1 - i
                return block(kb, jnp.maximum(kb - 1, 0), zc[0], zc[1], False)

            _, c = _loop_by_two(qb, off_diag, (z, c))
            r0 = pl.multiple_of(qb * tb, tb)
            o_ref[pl.ds(r0, tb), :] = _pair_select(m0, c[0][0], c[1][0])
            t1_ref[pl.ds(r0, tb), :] = jnp.where(lane == 0, c[0][1], jnp.where(lane == 1, c[1][1], 0.0))
            return carry

        lax.fori_loop(0, nb, qloop, 0)
        if comm is not None:
            @pl.when(step == 2 * nbat - 1)
            def _():
                comm["finish"](c_in, c_out, c_sems)

    def col(blk):
        return pl.BlockSpec((S, PAIR), lambda b, p: (b, blk + p))

    anyspec = pl.BlockSpec(memory_space=pl.ANY)
    out = pl.pallas_call(
        body, name=f"sb_fwd_{l}",
        grid=(nbat, 2),
        in_specs=[col(SBQ_BLK), col(SBK_BLK), col(SBV_BLK)] + c_specs,
        out_specs=[col(0), col(0)] + [anyspec] * n_co,
        out_shape=[jax.ShapeDtypeStruct((T, AW), F32), jax.ShapeDtypeStruct((T, AW), F32)] + c_outs,
        scratch_shapes=[pltpu.VMEM((nb, PAIR, tb), BF16), pltpu.VMEM((S, PAIR), BF16)] + c_scr,
        compiler_params=_cp(("arbitrary", "arbitrary"), VMEM_BIG),
    )(proj, proj, proj, *c_args)
    return out[0], out[1], list(out[2:])


def _sbq_bwd(proj, do, t1, l, S, comm=None):
    T = proj.shape[0]
    tb = TQ_(S)
    nb = S // tb
    nbat = T // S
    c_args, c_specs, c_outs, c_scr = _hosted(comm)
    n_ci, n_co = len(c_args), len(c_outs)

    def body(*refs):
        q_ref, k_ref, v_ref, do_ref, t1_ref = refs[:5]
        c_in = refs[5:5 + n_ci]
        dq_ref, dk_ref, dv_ref = refs[5 + n_ci:8 + n_ci]
        c_out = refs[8 + n_ci:8 + n_ci + n_co]
        kb_sc, kt_sc, vt_sc, dkt_sc, dvt_sc = refs[8 + n_ci + n_co:13 + n_ci + n_co]
        c_sems = refs[13 + n_ci + n_co:]
        step = pl.program_id(0) * 2 + pl.program_id(1)
        if comm is not None:
            @pl.when(step == 0)
            def _():
                comm["start"](c_in, c_out, c_sems)

        kb_sc[...] = k_ref[...].astype(BF16)
        _transpose_blocks(k_ref, kt_sc, nb, tb)
        _transpose_blocks(v_ref, vt_sc, nb, tb)
        dkt_sc[...] = jnp.zeros_like(dkt_sc)
        dvt_sc[...] = jnp.zeros_like(dvt_sc)
        _, m0 = _lane_masks()
        mt0 = lax.broadcasted_iota(jnp.int32, (PAIR, 1), 0) < HD
        tri_in = _tri(tb, "row_le_col")
        tri_ex = _tri(tb, "row_lt_col")
        past = lax.broadcasted_iota(jnp.int32, (tb, tb), 1) < lax.broadcasted_iota(jnp.int32, (tb, tb), 0)

        def qloop(qb, carry):
            qf = _rows(q_ref, qb, tb) * SCALE
            dof = _rows(do_ref, qb, tb)
            qh = _pair_split(qf, m0)
            doh = _pair_split(dof, m0)
            qth = _pair_split(qf.T, mt0)
            doth = _pair_split(dof.T, mt0)
            t1v = _rows(t1_ref, qb, tb)
            tot = (t1v[:, 0:1], t1v[:, 1:2])

            def accumulate(kb, wz, dqs):
                out = []
                for h in range(2):
                    wb, dz = wz[h]
                    dvt_sc[kb] += _dot(doth[h], wb)
                    dkt_sc[kb] += _dot(qth[h], dz)
                    out.append(dqs[h] + _dot(dz, _rows(kb_sc, kb, tb)))
                return tuple(out)

            def block(kb, c, masked):
                hs = range(2)
                runs, dqs, (kb_prev, wz_prev) = c
                z = [_dot(qh[h], kt_sc[kb]) for h in hs]
                dw = [_dot(doh[h], vt_sc[kb]) for h in hs]
                st = []
                for h in hs:
                    lb, l1 = _logsig2(z[h])
                    if masked:
                        l1 = jnp.where(past, l1, 0.0)
                    st.append((lb, _cumsum_mm(l1, tri_in, parts=2)))
                dqs = accumulate(kb_prev, wz_prev, dqs)
                mid = []
                for h in hs:
                    lb, p1 = st[h]
                    w = jnp.exp(lb + (tot[h] - (runs[h][0] + p1)))
                    if masked:
                        w = jnp.where(past, w, 0.0)
                    gm = w * dw[h]
                    mid.append((w.astype(BF16), gm, _cumsum_mm(gm, tri_ex, parts=1)))
                new_runs, wz = [], []
                for h in hs:
                    run1, rung = runs[h]
                    wb, gm, cx = mid[h]
                    dz = gm - (gm + (rung + cx)) * jnp.exp(st[h][0])
                    if masked:
                        dz = jnp.where(past, dz, 0.0)
                    wz.append((wb, dz.astype(BF16)))
                    p1 = st[h][1]
                    new_runs.append((run1 + p1[:, tb - 1:tb], rung + (cx[:, tb - 1:tb] + gm[:, tb - 1:tb])))
                return tuple(new_runs), dqs, (kb, tuple(wz))

            z1 = jnp.zeros((tb, 1), F32)
            zq = jnp.zeros((tb, PAIR), F32)
            zb = jnp.zeros((tb, tb), BF16)
            none = (jnp.int32(0), ((zb, zb), (zb, zb)))
            c = _loop_by_two(qb, lambda i, cc: block(i, cc, False), (((z1, z1), (z1, z1)), (zq, zq), none))
            _, dqs, (kb_last, wz_last) = block(qb, c, True)
            dqs = accumulate(kb_last, wz_last, dqs)
            r0 = pl.multiple_of(qb * tb, tb)
            dq_ref[pl.ds(r0, tb), :] = (_pair_select(m0, dqs[0], dqs[1]) * SCALE).astype(BF16)
            return carry

        lax.fori_loop(0, nb, qloop, 0)
        for kb in range(nb):
            dk_ref[kb * tb:(kb + 1) * tb, :] = dkt_sc[kb].T.astype(BF16)
            dv_ref[kb * tb:(kb + 1) * tb, :] = dvt_sc[kb].T.astype(BF16)
        if comm is not None:
            @pl.when(step == 2 * nbat - 1)
            def _():
                comm["finish"](c_in, c_out, c_sems)

    def col(blk):
        return pl.BlockSpec((S, PAIR), lambda b, p: (b, blk + p))

    sh = jax.ShapeDtypeStruct((T, AW), BF16)
    anyspec = pl.BlockSpec(memory_space=pl.ANY)
    out = pl.pallas_call(
        body, name=f"sb_bwd_{l}",
        grid=(nbat, 2),
        in_specs=[col(SBQ_BLK), col(SBK_BLK), col(SBV_BLK), col(0), col(0)] + c_specs,
        out_specs=[col(0), col(0), col(0)] + [anyspec] * n_co,
        out_shape=[sh, sh, sh] + c_outs,
        scratch_shapes=[pltpu.VMEM((S, PAIR), BF16), pltpu.VMEM((nb, PAIR, tb), BF16), pltpu.VMEM((nb, PAIR, tb), BF16),
                        pltpu.VMEM((nb, PAIR, tb), F32), pltpu.VMEM((nb, PAIR, tb), F32)] + c_scr,
        compiler_params=_cp(("arbitrary", "arbitrary"), VMEM_BIG),
    )(proj, proj, proj, do, t1, *c_args)
    return out[0], out[1], out[2], list(out[3:])


def _foxq_fwd(proj, cum, ck, gqk2, l, S):
    T = proj.shape[0]
    tb = TQ_(S)
    nb = S // tb

    def body(q_ref, k_ref, v_ref, cum_ref, ck_ref, g_ref, o_ref, nl_ref, fk_sc, fkt_sc, vb_sc):
        lane, m0 = _lane_masks()
        p = pl.program_id(1)
        kn, _ = _pair_rms(k_ref[...], m0)
        fk_sc[...] = kn * g_ref[1:2, :]
        _transpose_blocks(fk_sc, fkt_sc, nb, tb)
        vb_sc[...] = v_ref[...].astype(BF16)
        causal = lax.broadcasted_iota(jnp.int32, (tb, tb), 1) <= lax.broadcasted_iota(jnp.int32, (tb, tb), 0)

        def qloop(qb, carry):
            qn, _ = _pair_rms(_rows(q_ref, qb, tb), m0)
            fqh = _pair_split(qn * (g_ref[0:1, :] * SCALE), m0)
            cumv = _rows(cum_ref, qb, tb)
            cq = [_rowsum(jnp.where(lane == 2 * p + h, cumv, 0.0)) for h in range(2)]

            def scores(kb):
                return tuple(_dot(fqh[h], fkt_sc[kb]) for h in range(2))

            def block(kb, kb_next, qk, c, masked):
                st = []
                for h in range(2):
                    s = qk[h] + (cq[h] - ck_ref[h, kb])
                    if masked:
                        s = jnp.where(causal, s, NEG)
                    m2 = jnp.maximum(c[h][0], jnp.max(s, axis=1, keepdims=True))
                    pr = jnp.exp(s - m2)
                    hi = pr.astype(BF16)
                    lo = (pr - hi.astype(F32)).astype(BF16)
                    vv = _rows(vb_sc, kb, tb)
                    st.append((m2, pr, _dot(hi, vv) + _dot(lo, vv)))
                qk_next = scores(kb_next)
                out = []
                for h in range(2):
                    m, lsum, acc = c[h]
                    m2, pr, pv = st[h]
                    al = jnp.exp(m - m2)
                    out.append((m2, al * lsum + _rowsum(pr), al * acc + pv))
                return qk_next, tuple(out)

            zero = (jnp.full((tb, 1), NEG, F32), jnp.zeros((tb, 1), F32), jnp.zeros((tb, PAIR), F32))

            def off_diag(i, sc):
                return block(i, i + 1, sc[0], sc[1], False)

            qk, c = lax.fori_loop(0, qb, off_diag, (scores(0), (zero, zero)))
            _, c = block(qb, qb, qk, c, True)
            r0 = pl.multiple_of(qb * tb, tb)
            o_ref[pl.ds(r0, tb), :] = _pair_select(m0, c[0][2] / c[0][1], c[1][2] / c[1][1])
            nl = [cq[h] - (c[h][0] + jnp.log(c[h][1])) for h in range(2)]
            nl_ref[pl.ds(r0, tb), :] = jnp.where(lane == 0, nl[0], jnp.where(lane == 1, nl[1], 0.0))
            return carry

        lax.fori_loop(0, nb, qloop, 0)

    def col(blk):
        return pl.BlockSpec((S, PAIR), lambda b, p: (b, blk + p))

    return pl.pallas_call(
        body, name=f"fox_fwd_{l}",
        grid=(T // S, 2),
        in_specs=[col(FXQ_BLK), col(FXK_BLK), col(FXV_BLK),
                  pl.BlockSpec((S, 128), lambda b, p: (b, 0)),
                  pl.BlockSpec((None, 2, nb, 1, tb), lambda b, p: (b, p, 0, 0, 0)),
                  pl.BlockSpec((None, 8, PAIR), lambda b, p: (l, 0, 0))],
        out_specs=[col(0), col(0)],
        out_shape=[jax.ShapeDtypeStruct((T, AW), F32)] * 2,
        scratch_shapes=[pltpu.VMEM((S, PAIR), F32), pltpu.VMEM((nb, PAIR, tb), BF16), pltpu.VMEM((S, PAIR), BF16)],
        compiler_params=_cp(("arbitrary", "arbitrary"), VMEM_BIG),
    )(proj, proj, proj, cum, ck, gqk2)


def _foxq_bwd(proj, do, nl, ox, ck, gqk2, l, S):
    T = proj.shape[0]
    tb = TQ_(S)
    nb = S // tb

    def body(q_ref, k_ref, v_ref, do_ref, nl_ref, ox_ref, ck_ref, g_ref,
             dq_ref, dk_ref, dv_ref, dc_ref, wacc_ref, fk_sc, fkt_sc, vt_sc, dfkt_sc, dvt_sc):
        @pl.when((pl.program_id(0) == 0) & (pl.program_id(1) == 0))
        def _():
            wacc_ref[...] = jnp.zeros_like(wacc_ref)

        _, m0 = _lane_masks()
        mt0 = lax.broadcasted_iota(jnp.int32, (PAIR, 1), 0) < HD
        g0 = g_ref[0:1, :]
        g1 = g_ref[1:2, :]
        fk_sc[...] = (_pair_rms(k_ref[...], m0)[0] * g1).astype(BF16)
        _transpose_blocks(fk_sc, fkt_sc, nb, tb)
        _transpose_blocks(v_ref, vt_sc, nb, tb)
        dfkt_sc[...] = jnp.zeros_like(dfkt_sc)
        dvt_sc[...] = jnp.zeros_like(dvt_sc)
        dc_ref[...] = jnp.zeros_like(dc_ref)
        causal = lax.broadcasted_iota(jnp.int32, (tb, tb), 1) <= lax.broadcasted_iota(jnp.int32, (tb, tb), 0)

        def qloop(qb, carry):
            qn, qr = _pair_rms(_rows(q_ref, qb, tb), m0)
            fqf = qn * (g0 * SCALE)
            dof = _rows(do_ref, qb, tb)
            fqh = _pair_split(fqf, m0)
            doh = _pair_split(dof, m0)
            fqth = _pair_split(fqf.T, mt0)
            doth = _pair_split(dof.T, mt0)
            nlv = _rows(nl_ref, qb, tb)
            cql = (nlv[:, 0:1], nlv[:, 1:2])

            def probs(kb, masked):
                qk = [_dot(fqh[h], fkt_sc[kb]) for h in range(2)]
                dp = [_dot(doh[h], vt_sc[kb]) for h in range(2)]
                pr = []
                for h in range(2):
                    e = jnp.exp(qk[h] + (cql[h] - ck_ref[h, kb]))
                    pr.append(jnp.where(causal, e, 0.0) if masked else e)
                return pr, dp

            oxv = _rows(ox_ref, qb, tb)
            dlt = [_rowsum(doh[h].astype(F32) * oxv) for h in range(2)]

            def accumulate(kb, pd, dfqs):
                out = []
                for h in range(2):
                    prb, dsb = pd[h]
                    dvt_sc[kb] += _dot(doth[h], prb)
                    dfkt_sc[kb] += _dot(fqth[h], dsb)
                    out.append(dfqs[h] + _dot(dsb, _rows(fk_sc, kb, tb)))
                return tuple(out)

            def block(kb, c, masked):
                dfqs, (kb_prev, pd_prev) = c
                pr, dp = probs(kb, masked)
                dfqs = accumulate(kb_prev, pd_prev, dfqs)
                pd = []
                for h in range(2):
                    ds = pr[h] * (dp[h] - dlt[h])
                    dc_ref[h, kb] += jnp.broadcast_to(-_colsum(ds), (8, tb))
                    pd.append((pr[h].astype(BF16), ds.astype(BF16)))
                return dfqs, (kb, tuple(pd))

            zq = jnp.zeros((tb, PAIR), F32)
            zb = jnp.zeros((tb, tb), BF16)
            none = (jnp.int32(0), ((zb, zb), (zb, zb)))
            c = _loop_by_two(qb, lambda i, cc: block(i, cc, False), ((zq, zq), none))
            dfqs, (kb_last, pd_last) = block(qb, c, True)
            c = accumulate(kb_last, pd_last, dfqs)
            dfq = _pair_select(m0, c[0], c[1]) * SCALE
            wacc_ref[0:1, :] += _colsum(dfq * qn)
            r0 = pl.multiple_of(qb * tb, tb)
            dq_ref[pl.ds(r0, tb), :] = _pair_rms_bwd(qn, qr, dfq * g0, m0).astype(BF16)
            return carry

        lax.fori_loop(0, nb, qloop, 0)
        for kb in range(nb):
            rows = slice(kb * tb, (kb + 1) * tb)
            dfk = dfkt_sc[kb].T
            knb, krb = _pair_rms(k_ref[rows, :], m0)
            wacc_ref[1:2, :] += _colsum(dfk * knb)
            dk_ref[rows, :] = _pair_rms_bwd(knb, krb, dfk * g1, m0).astype(BF16)
            dv_ref[rows, :] = dvt_sc[kb].T.astype(BF16)

    def col(blk):
        return pl.BlockSpec((S, PAIR), lambda b, p: (b, blk + p))

    sh = jax.ShapeDtypeStruct((T, AW), BF16)
    return pl.pallas_call(
        body, name=f"fox_bwd_{l}",
        grid=(T // S, 2),
        in_specs=[col(FXQ_BLK), col(FXK_BLK), col(FXV_BLK), col(0), col(0), col(0),
                  pl.BlockSpec((None, 2, nb, 1, tb), lambda b, p: (b, p, 0, 0, 0)),
                  pl.BlockSpec((None, 8, PAIR), lambda b, p: (l, 0, 0))],
        out_specs=[col(0), col(0), col(0),
                   pl.BlockSpec((None, 2, nb, 8, tb), lambda b, p: (b, p, 0, 0, 0)),
                   pl.BlockSpec((8, PAIR), lambda b, p: (0, 0))],
        out_shape=[sh, sh, sh,
                   jax.ShapeDtypeStruct((T // S, NH, nb, 8, tb), F32),
                   jax.ShapeDtypeStruct((8, PAIR), F32)],
        scratch_shapes=[pltpu.VMEM((S, PAIR), BF16), pltpu.VMEM((nb, PAIR, tb), BF16), pltpu.VMEM((nb, PAIR, tb), BF16),
                        pltpu.VMEM((nb, PAIR, tb), F32), pltpu.VMEM((nb, PAIR, tb), F32)],
        compiler_params=_cp(("arbitrary", "arbitrary"), VMEM_BIG),
    )(proj, proj, proj, do, nl, ox, ck, gqk2)


def _ada_fwd(c_all, w_ada, b_cols):
    nb, ncol = c_all.shape[0], w_ada.shape[2]
    tn = _tile(ncol, 768)

    def body(c_ref, w_ref, b_ref, o_ref):
        c = c_ref[...]
        ca = (c * _sigmoid(c)).astype(BF16)
        o_ref[...] = _dot(ca, w_ref[...].astype(BF16)) + b_ref[...]

    return pl.pallas_call(
        body, name="ada_fwd",
        grid=(2, ncol // tn),
        in_specs=[pl.BlockSpec((nb, D), lambda l, n: (0, 0)),
                  pl.BlockSpec((None, D, tn), lambda l, n: (l, 0, n)),
                  pl.BlockSpec((None, 1, tn), lambda l, n: (l, 0, n))],
        out_specs=pl.BlockSpec((None, nb, tn), lambda l, n: (l, 0, n)),
        out_shape=jax.ShapeDtypeStruct((2, nb, ncol), F32),
        compiler_params=_cp(("arbitrary", "arbitrary")),
    )(c_all, w_ada, b_cols)


def _ada_bwd(c_all, dmod_cols):
    nb, ncol = c_all.shape[0], dmod_cols.shape[2]
    tn = _tile(ncol, 768)

    def body(c_ref, d_ref, o_ref):
        c = c_ref[...]
        ca = (c * _sigmoid(c)).astype(BF16)
        o_ref[...] = _dot_tn(ca, d_ref[...].astype(BF16))

    return pl.pallas_call(
        body, name="ada_bwd",
        grid=(2, ncol // tn),
        in_specs=[pl.BlockSpec((nb, D), lambda l, n: (0, 0)),
                  pl.BlockSpec((None, nb, tn), lambda l, n: (l, 0, n))],
        out_specs=pl.BlockSpec((None, D, tn), lambda l, n: (l, 0, n)),
        out_shape=jax.ShapeDtypeStruct((2, D, ncol), F32),
        compiler_params=_cp(("arbitrary", "arbitrary")),
    )(c_all, dmod_cols)


def _sum_lead(a, name):
    n, R, C = a.shape
    tr = _tile_div8(R, 256)

    def body(a_ref, o_ref):
        acc = a_ref[0]
        for i in range(1, n):
            acc = acc + a_ref[i]
        o_ref[...] = acc

    return pl.pallas_call(
        body, name=name,
        grid=(R // tr,),
        in_specs=[pl.BlockSpec((n, tr, C), lambda i: (0, i, 0))],
        out_specs=pl.BlockSpec((tr, C), lambda i: (i, 0)),
        out_shape=jax.ShapeDtypeStruct((R, C), F32),
        compiler_params=_cp(("arbitrary",)),
    )(a)


def _adamw(w, g, m, v, name):
    R, C = w.shape
    tr = _tile_div8(R, max(8, (1 << 18) // C))
    c1 = 1.0 / (1.0 - ADAM_B1 ** ADAM_STEP)
    c2 = 1.0 / (1.0 - ADAM_B2 ** ADAM_STEP)

    def body(w_ref, g_ref, m_ref, v_ref, d_ref, mo_ref, vo_ref):
        gg = g_ref[...]
        mn = ADAM_B1 * m_ref[...] + (1.0 - ADAM_B1) * gg
        vn = ADAM_B2 * v_ref[...] + (1.0 - ADAM_B2) * (gg * gg)
        mo_ref[...] = mn
        vo_ref[...] = vn
        d_ref[...] = (-ADAM_LR) * ((mn * c1) / (jnp.sqrt(vn * c2) + ADAM_EPS) + ADAM_WD * w_ref[...])

    spec = pl.BlockSpec((tr, C), lambda i: (i, 0))
    sh = jax.ShapeDtypeStruct((R, C), F32)
    return pl.pallas_call(
        body, name=name, grid=(R // tr,),
        in_specs=[spec] * 4, out_specs=[spec] * 3, out_shape=[sh] * 3,
        compiler_params=_cp(("arbitrary",)),
    )(w, g, m, v)


def _coords():
    return lax.axis_index("x"), lax.axis_index("y"), lax.axis_index("c")


def _all_gather8(blk, name, vmem):
    m_per, n = blk.shape
    space = pltpu.VMEM if vmem else pl.ANY

    def body(x_ref, out_ref, send_sems, recv_sems, local_sem):
        x, y, c = _coords()
        me, sibling = (x, y, c), (x, y, 1 - c)
        chips = [(1 - x, y), (x, 1 - y), (1 - x, 1 - y)]

        def rows(px, py, pc):
            return out_ref.at[4 * px + 2 * py + pc]

        def copy(k, block, to, src=None):
            return pltpu.make_async_remote_copy(
                src_ref=rows(*block) if src is None else src, dst_ref=rows(*block),
                send_sem=send_sems.at[k], recv_sem=recv_sems.at[k], device_id=to, device_id_type=MESH)

        mine = pltpu.make_async_copy(x_ref, rows(*me), local_sem)
        mine.start()
        first = [copy(0, me, sibling, src=x_ref)]
        first += [copy(1 + j, me, (*chip, c), src=x_ref) for j, chip in enumerate(chips)]
        for cp in first:
            cp.start()
        passed = [copy(4 + j, (*chip, c), sibling) for j, chip in enumerate(chips)]
        for j, chip in enumerate(chips):
            copy(1 + j, (*chip, c), me).wait_recv()
            passed[j].start()
        copy(0, sibling, me).wait_recv()
        for j, chip in enumerate(chips):
            copy(4 + j, (*chip, 1 - c), me).wait_recv()
        for cp in first + passed:
            cp.wait_send()
        mine.wait()

    return pl.pallas_call(
        body, name=name,
        out_shape=jax.ShapeDtypeStruct((N_DEV, m_per, n), blk.dtype),
        in_specs=[pl.BlockSpec(memory_space=space)],
        out_specs=pl.BlockSpec(memory_space=space),
        scratch_shapes=[pltpu.SemaphoreType.DMA((7,)), pltpu.SemaphoreType.DMA((7,)), pltpu.SemaphoreType.DMA],
        compiler_params=pltpu.CompilerParams(vmem_limit_bytes=VMEM_BIG if vmem else None),
    )(blk)


def _run_comm(comm, name):
    n_in, n_out = len(comm["args"]), len(comm["out_shapes"])

    def body(*refs):
        parts = (refs[:n_in], refs[n_in:n_in + n_out], refs[n_in + n_out:])
        comm["start"](*parts)
        comm["finish"](*parts)

    anyspec = pl.BlockSpec(memory_space=pl.ANY)
    return pl.pallas_call(
        body, name=name, out_shape=comm["out_shapes"],
        in_specs=[anyspec] * n_in, out_specs=[anyspec] * n_out, scratch_shapes=comm["scratch"],
    )(*comm["args"])


def _hosted(comm):
    if comm is None:
        return [], [], [], []
    anyspec = pl.BlockSpec(memory_space=pl.ANY)
    return list(comm["args"]), [anyspec] * len(comm["args"]), list(comm["out_shapes"]), list(comm["scratch"])


def _ag_comm(wshards, pieces):
    n_piece = len(pieces)
    halves = [wshards[i].shape[len(lead)] // 2 for i, lead, _ in pieces]

    def plan(ins, outs, sems):
        send_sems, recv_sems, local_sems = sems
        x, y, c = _coords()
        me, sibling = (x, y, c), (x, y, 1 - c)
        chips = [(1 - x, y), (x, 1 - y), (1 - x, 1 - y)]

        def dsts(px, py, pc):
            s = 2 * px + py
            return [outs[p].at[s, pl.ds(pc * r2, r2)] if stacked else outs[p].at[pl.ds((2 * s + pc) * r2, r2)]
                    for p, ((_, _, stacked), r2) in enumerate(zip(pieces, halves))]

        srcs = [ins[i].at[(*lead, pl.ds(c * r2, r2))] for (i, lead, _), r2 in zip(pieces, halves)]

        def copies(k, block, to, own=False):
            d = dsts(*block)
            return [pltpu.make_async_remote_copy(
                src_ref=srcs[p] if own else d[p], dst_ref=d[p], send_sem=send_sems.at[k, p],
                recv_sem=recv_sems.at[k, p], device_id=to, device_id_type=MESH) for p in range(n_piece)]

        mine = [pltpu.make_async_copy(srcs[p], d, local_sems.at[p]) for p, d in enumerate(dsts(*me))]
        first = copies(0, me, sibling, own=True)
        for j, chip in enumerate(chips):
            first += copies(1 + j, me, (*chip, c), own=True)
        return me, sibling, chips, c, copies, mine, first

    def start(ins, outs, sems):
        *_, mine, first = plan(ins, outs, sems)
        for cp in mine + first:
            cp.start()

    def finish(ins, outs, sems):
        me, sibling, chips, c, copies, mine, first = plan(ins, outs, sems)
        passed = []
        for j, chip in enumerate(chips):
            for cp in copies(1 + j, (*chip, c), me):
                cp.wait_recv()
            fwd = copies(4 + j, (*chip, c), sibling)
            for cp in fwd:
                cp.start()
            passed += fwd
        for cp in copies(0, sibling, me):
            cp.wait_recv()
        for j, chip in enumerate(chips):
            for cp in copies(4 + j, (*chip, 1 - c), me):
                cp.wait_recv()
        for cp in first + passed:
            cp.wait_send()
        for cp in mine:
            cp.wait()

    out_shapes = []
    for (i, lead, stacked), r2 in zip(pieces, halves):
        cols = wshards[i].shape[-1]
        out_shapes.append(jax.ShapeDtypeStruct((N_SHARD, 2 * r2, cols) if stacked else (N_SHARD * 2 * r2, cols), BF16))
    return dict(
        args=list(wshards), out_shapes=out_shapes,
        scratch=[pltpu.SemaphoreType.DMA((7, n_piece)), pltpu.SemaphoreType.DMA((7, n_piece)),
                 pltpu.SemaphoreType.DMA((n_piece,))],
        start=start, finish=finish)


def _rs_to_chips_comm(hs):
    n = len(hs)

    def copies(h, r, sems):
        send_sems, recv_sems = sems
        x, y, c = _coords()
        chips = [(1 - x, y), (x, 1 - y), (1 - x, 1 - y)]
        return [pltpu.make_async_remote_copy(
            src_ref=h[p].at[2 * px + py], dst_ref=r[p].at[k], send_sem=send_sems.at[k, p], recv_sem=recv_sems.at[k, p],
            device_id=(px, py, c), device_id_type=MESH) for k, (px, py) in enumerate(chips) for p in range(n)]

    def start(h, r, sems):
        for cp in copies(h, r, sems):
            cp.start()

    def finish(h, r, sems):
        for cp in copies(h, r, sems):
            cp.wait()

    return dict(args=list(hs), out_shapes=[jax.ShapeDtypeStruct((3,) + h.shape[1:], h.dtype) for h in hs],
                scratch=[pltpu.SemaphoreType.DMA((3, n)), pltpu.SemaphoreType.DMA((3, n))],
                start=start, finish=finish)


def _rs_to_sibling(pieces, name):
    n = len(pieces)

    def body(*refs):
        g, r, (send_sems, recv_sems) = refs[:n], refs[n:2 * n], refs[2 * n:]
        x, y, c = _coords()
        cps = []
        for p in range(n):
            r2 = g[p].shape[1] // 2
            cps.append(pltpu.make_async_remote_copy(
                src_ref=g[p].at[:, pl.ds((1 - c) * r2, r2)], dst_ref=r[p], send_sem=send_sems.at[p],
                recv_sem=recv_sems.at[p], device_id=(x, y, 1 - c), device_id_type=MESH))
        for cp in cps:
            cp.start()
        for cp in cps:
            cp.wait()

    anyspec = pl.BlockSpec(memory_space=pl.ANY)
    return pl.pallas_call(
        body, name=name,
        out_shape=[jax.ShapeDtypeStruct((N_SHARD, g.shape[1] // 2, g.shape[2]), g.dtype) for g in pieces],
        in_specs=[anyspec] * n, out_specs=[anyspec] * n,
        scratch_shapes=[pltpu.SemaphoreType.DMA((n,)), pltpu.SemaphoreType.DMA((n,))],
    )(*pieces)


def _share_halves(tensors, places, r2s):
    n, no = len(places), len(tensors)

    def body(*refs):
        o, (send_sems, recv_sems) = refs[no:2 * no], refs[2 * no:]
        x, y, c = _coords()

        def half(p, hc):
            oi, lead = places[p]
            return o[oi].at[(*lead, pl.ds(hc * r2s[p], r2s[p]))]

        outs = [pltpu.make_async_remote_copy(
            src_ref=half(p, c), dst_ref=half(p, c), send_sem=send_sems.at[p], recv_sem=recv_sems.at[p],
            device_id=(x, y, 1 - c), device_id_type=MESH) for p in range(n)]
        for cp in outs:
            cp.start()
        for p in range(n):
            pltpu.make_async_remote_copy(
                src_ref=half(p, 1 - c), dst_ref=half(p, 1 - c), send_sem=send_sems.at[p], recv_sem=recv_sems.at[p],
                device_id=(x, y, 1 - c), device_id_type=MESH).wait_recv()
        for cp in outs:
            cp.wait_send()

    anyspec = pl.BlockSpec(memory_space=pl.ANY)
    return pl.pallas_call(
        body, name="share_halves",
        out_shape=[jax.ShapeDtypeStruct(t.shape, t.dtype) for t in tensors],
        in_specs=[anyspec] * no, out_specs=[anyspec] * no,
        input_output_aliases={i: i for i in range(no)},
        scratch_shapes=[pltpu.SemaphoreType.DMA((n,)), pltpu.SemaphoreType.DMA((n,))],
    )(*tensors)


def _add_rows(r2, cols, n_arrays):
    lanes = -(-cols // 128) * 128
    return _tile_div8(r2, max(16, (24 << 20) // (2 * n_arrays * lanes * 4)), mult=16)


def _add_sibling(pieces, recvs, cidx, name):
    n = len(pieces)
    _, R, C = pieces[0].shape
    r2 = R // 2
    tr = _add_rows(r2, C, 2 * n)
    nt = r2 // tr

    def body(c_ref, *refs):
        for p in range(n):
            refs[2 * n + p][...] = (refs[p][...] + refs[n + p][...].astype(F32)).astype(BF16)

    return pl.pallas_call(
        body, name=name,
        grid_spec=pltpu.PrefetchScalarGridSpec(
            num_scalar_prefetch=1, grid=(N_SHARD, nt),
            in_specs=[pl.BlockSpec((None, tr, C), lambda s, i, c_ref: (s, c_ref[0] * nt + i, 0))] * n
            + [pl.BlockSpec((None, tr, C), lambda s, i, c_ref: (s, i, 0))] * n,
            out_specs=[pl.BlockSpec((None, tr, C), lambda s, i, c_ref: (s, i, 0))] * n),
        out_shape=[jax.ShapeDtypeStruct((N_SHARD, r2, C), BF16)] * n,
        compiler_params=_cp(("arbitrary", "arbitrary"), VMEM_BIG),
    )(cidx, *pieces, *recvs)


def _add_chips_into(piece, recv_a, recv_b, sc, prev, shape, lead, name):
    _, R, C = piece.shape
    r2 = R // 2
    tr = _add_rows(r2, C, 4)
    nt = r2 // tr
    nl = len(lead)

    def body(sc_ref, p_ref, a_ref, b_ref, *rest):
        o_ref = rest[-1]
        acc = p_ref[...] + a_ref[...].astype(F32)
        for k in range(3):
            acc = acc + b_ref[k].astype(F32)
        o_ref[...] = acc

    in_specs = [pl.BlockSpec((None, tr, C), lambda i, sc_ref: (sc_ref[0], sc_ref[1] * nt + i, 0)),
                pl.BlockSpec((None, tr, C), lambda i, sc_ref: (sc_ref[0], i, 0)),
                pl.BlockSpec((3, tr, C), lambda i, sc_ref: (0, i, 0))]
    args = [sc, piece, recv_a, recv_b]
    aliases = {}
    if prev is not None:
        in_specs.append(pl.BlockSpec(memory_space=pl.ANY))
        args.append(prev)
        aliases = {4: 0}
    return pl.pallas_call(
        body, name=name,
        grid_spec=pltpu.PrefetchScalarGridSpec(
            num_scalar_prefetch=1, grid=(nt,), in_specs=in_specs,
            out_specs=pl.BlockSpec((None,) * nl + (tr, C), lambda i, sc_ref: (*lead, sc_ref[1] * nt + i, 0))),
        out_shape=jax.ShapeDtypeStruct(shape, F32),
        input_output_aliases=aliases,
        compiler_params=_cp(("arbitrary",), VMEM_BIG),
    )(*args)


def _pack_rows(parts, rows, dtype):
    flat = jnp.concatenate([p.reshape(-1).astype(dtype) for p in parts])
    return jnp.pad(flat, (0, rows * ROW - flat.shape[0])).reshape(rows, ROW)


def _unpack(flat, shapes):
    out, off = [], 0
    for sh in shapes:
        n = math.prod(sh)
        out.append(flat[off:off + n].reshape(sh))
        off += n
    return out


def _block_diag(w):
    eye = jnp.eye(LW // HD, dtype=w.dtype)
    return jnp.einsum("lhij,hg->lhigj", w, eye).reshape(w.shape[0], LW, LW)


def _diag_blocks(w):
    nbk = LW // HD
    w4 = w.reshape(nbk, HD, nbk, HD)
    return jnp.stack([w4[h, :, h, :] for h in range(nbk)])


def _rows8(rows, width):
    z = jnp.zeros((width,), F32)
    return jnp.stack(list(rows) + [z] * (8 - len(rows)))


def kernel(x, c, w_ada, b_ada, g_norm, w_ffn_up, w_ffn_down, w_in, b_fgate, conv_w, conv_b, w_rgate, b_rgate, w_igate, b_igate, lru_lambda, g_qk, g_mix_out, w_out, loss_target, m_w_ada, m_b_ada, m_g_norm, m_w_ffn_up, m_w_ffn_down, m_w_in, m_b_fgate, m_conv_w, m_conv_b, m_w_rgate, m_b_rgate, m_w_igate, m_b_igate, m_lru_lambda, m_g_qk, m_g_mix_out, m_w_out, v_w_ada, v_b_ada, v_g_norm, v_w_ffn_up, v_w_ffn_down, v_w_in, v_b_fgate, v_conv_w, v_conv_b, v_w_rgate, v_b_rgate, v_w_igate, v_b_igate, v_lru_lambda, v_g_qk, v_g_mix_out, v_w_out):
    B, S, _ = x.shape
    T = B * S
    xi, yi, ci = _coords()
    sidx = 2 * xi + yi
    didx = 4 * xi + 2 * yi + ci
    ada_cols = w_ada.shape[2]
    gn_cols = g_norm.shape[2]
    cw_cols = conv_w.shape[2]
    n_all = B * N_DEV

    blk1 = _pack_rows([c, jnp.pad(g_norm.reshape(-1), (0, 2 * ROW - g_norm.size)), conv_w], 8, F32)
    ag1 = _all_gather8(blk1, "ag_small_in", True)
    c_all = ag1[:, 0:B].reshape(n_all, D)
    chip_rows = ag1[0::2]
    g_norm_full = chip_rows[:, 2:4].reshape(N_SHARD, 2 * ROW)[:, :g_norm.size] \
        .reshape(N_SHARD, 2, 3, gn_cols).transpose(1, 2, 0, 3).reshape(2, 3, D)
    conv_w_full = chip_rows[:, 4].reshape(N_SHARD, 2, 4, cw_cols).transpose(1, 2, 0, 3).reshape(2, 4, LW)

    b_cols = lax.dynamic_slice(b_ada, (0, sidx * ada_cols), (2, ada_cols)).reshape(2, 1, ada_cols)
    mod_cols = _ada_fwd(c_all, w_ada, b_cols)
    mrows = (2 * n_all * ada_cols) // ROW
    ag2 = _all_gather8(mod_cols.reshape(mrows, ROW), "ag_mod", True)
    mod_sh = ag2[0::2].reshape(N_SHARD, 2, n_all, ada_cols)
    mod_me = lax.dynamic_slice(mod_sh, (0, 0, didx * B, 0), (N_SHARD, 2, B, ada_cols))
    mod_me = mod_me.transpose(1, 2, 0, 3).reshape(2, B, 3, 3, D)
    zrow = jnp.zeros((B, D), F32)
    mods = [[jnp.stack([mod_me[l, :, j, 0], 1.0 + mod_me[l, :, j, 1], 1.0 + mod_me[l, :, j, 2],
                        jnp.broadcast_to(g_norm_full[l, j], (B, D)), zrow, zrow, zrow, zrow], axis=1)
             for j in range(3)] for l in range(2)]

    wshards = (w_ffn_up.astype(BF16), w_ffn_down.astype(BF16), w_in.astype(BF16), w_out.astype(BF16))

    def ffn_pieces(l, j):
        return [(0, (l, j), True), (1, (l, j), False)]

    def mixer_pieces(l):
        return [(2, (l,), True), (3, (l,), False)]

    def ffn_weights(up, dn):
        return dict(up=up, dn=dn)

    def mixer_weights(g_in, g_out):
        return dict(inp=jnp.pad(g_in.transpose(1, 0, 2).reshape(D, N_IN), ((0, 0), (0, N_INP - N_IN))), out=g_out)

    wl = [dict(), dict()]
    wl[0][0] = ffn_weights(*_run_comm(_ag_comm(wshards, ffn_pieces(0, 0)), "ag_weights_0_0"))

    wr_d = _block_diag(w_rgate).astype(BF16)
    wi_d = _block_diag(w_igate).astype(BF16)
    cw8 = jnp.pad(conv_w_full, ((0, 0), (0, 4), (0, 0)))
    vp8 = jnp.stack([_rows8([conv_b[l], b_rgate[l], b_igate[l], lru_lambda[l]], LW) for l in range(2)])
    bfp = jnp.pad(b_fgate, ((0, 0), (0, 128 - NH)))[:, None, :] * jnp.ones((1, 8, 1), F32)
    gqk2 = jnp.tile(jnp.pad(g_qk, ((0, 0), (0, 6), (0, 0))), (1, 1, 2))
    gmix8 = jnp.pad(g_mix_out[:, None, :], ((0, 0), (0, 7), (0, 0)))

    x2 = x.reshape(T, D)
    tgt = loss_target.reshape(T, D)

    saved = []
    xc = x2
    for l in range(2):
        sv = {}
        sv["x0"] = xc
        w = wl[l]
        rest0 = _ag_comm(wshards, mixer_pieces(0) + ffn_pieces(0, 1)) if l == 0 else None
        xc, sv["g0"], sv["u0"], sv["f0"], got = _ffn_fwd(xc, mods[l][0], w[0]["up"], w[0]["dn"], l, 0, S, rest0)
        if l == 0:
            w["mix"] = mixer_weights(got[0], got[1])
            w[1] = ffn_weights(got[2], got[3])
        sv["x1"] = xc
        sv["h1"], proj = _mix_in_fwd(xc, mods[l][1], w["mix"]["inp"], l, S)
        sv["proj"] = proj
        sv["ylru"], sv["hl"] = _lru_fwd(proj, cw8, vp8, wr_d, wi_d, l, S)
        all1 = _ag_comm(wshards, ffn_pieces(1, 0) + mixer_pieces(1) + ffn_pieces(1, 1)) if l == 0 else None
        sv["osb"], sv["t1"], got = _sbq_fwd(proj, l, S, all1)
        if l == 0:
            wl[1][0] = ffn_weights(got[0], got[1])
            wl[1]["mix"] = mixer_weights(got[2], got[3])
            wl[1][1] = ffn_weights(got[4], got[5])
        cum = _fgate_fwd(proj, bfp, l, S)
        sv["ck"] = cum[:, :NH].reshape(B, S, NH).transpose(0, 2, 1).reshape(B, NH, S // TQ_(S), 1, TQ_(S))
        sv["ofx"], sv["nl"] = _foxq_fwd(proj, cum, sv["ck"], gqk2, l, S)
        xc, sv["y"], sv["mo"] = _mix_out_fwd(xc, sv["ylru"], sv["osb"], sv["ofx"], mods[l][1], gmix8, w["mix"]["out"], l, S)
        sv["x2"] = xc
        xc, sv["g2"], sv["u2"], sv["f2"], _ = _ffn_fwd(xc, mods[l][2], w[1]["up"], w[1]["dn"], l, 1, S)
        saved.append(sv)

    dxc, lpart = _loss_head(xc, tgt, S)
    loss = lax.psum(lpart[0, 0], ("x", "y", "c"))

    tf = wl[0][0]["up"].shape[-1]
    g_up_l = [[None, None], [None, None]]
    g_dn_l = [[None, None], [None, None]]
    g_in_l, g_out_l = [None, None], [None, None]
    dmods = [[None] * 3 for _ in range(2)]
    small = [dict() for _ in range(2)]
    cvec = jnp.reshape(ci, (1,)).astype(jnp.int32)
    scvec = jnp.stack([sidx, ci]).astype(jnp.int32)

    def ffn_groups(l, j):
        return [(0, "up", [g_up_l[l][j]], [(l, j)]), (1, "dn", [g_dn_l[l][j]], [(l, j)])]

    def mixer_groups(l):
        return [(2, "in", [g_in_l[l]], [(l,)]), (3, "out", [g_out_l[l]], [(l,)])]

    def rs_sibling_phase(groups, tag):
        recv_a = _rs_to_sibling([pb for _, _, ps, _ in groups for _, pb in ps], f"rs_to_sibling_{tag}")
        hs, off = [], 0
        for _, gname, ps, leads in groups:
            hs += _add_sibling([pf for pf, _ in ps], recv_a[off:off + len(ps)], cvec,
                               f"rs_add_sibling_{gname}_{'_'.join(map(str, leads[0]))}")
            off += len(ps)
        return groups, recv_a, hs

    def ffn_back(l, j, xin, dy, sv, sub, comm=None):
        dx, dmod, wacc, hb, dfb, ab, dgub, got = _ffn_bwd(
            xin, dy, mods[l][sub], sv[f"f{sub}"], sv[f"g{sub}"], sv[f"u{sub}"],
            wl[l][j]["up"], wl[l][j]["dn"], l, j, S, comm)
        g_up_l[l][j] = _mm_tn(hb, dgub, f"dw_up_{l}_{j}", tnb=tf, split_n=True, with_bf16=True)
        g_dn_l[l][j] = tuple(g.reshape(N_SHARD, -1, D)
                             for g in _mm_tn(ab, dfb, f"dw_dn_{l}_{j}", tma=tf, with_bf16=True))
        dmods[l][sub] = dmod
        small[l][f"gn{sub}"] = wacc[0]
        return dx, got

    batches = []
    for l in (1, 0):
        sv = saved[l]
        dxc, _ = ffn_back(l, 1, sv["x2"], dxc, sv, 2)
        dyl, dsb, dfx, dmo, dmod1, wacc_mo = _mix_out_bwd(
            dxc, sv["ylru"], sv["osb"], sv["ofx"], sv["mo"], mods[l][1], gmix8, wl[l]["mix"]["out"], l, S)
        small[l]["gmix"] = wacc_mo[0]
        g_out_l[l] = tuple(g.reshape(N_SHARD, -1, D) for g in _mm_tn(sv["y"], dmo, f"dw_out_{l}", with_bf16=True))
        dsq, dsk, dsv, got = _sbq_bwd(sv["proj"], dsb, sv["t1"], l, S,
                                       _rs_to_chips_comm(rs1[2]) if l == 0 else None)
        if l == 0:
            batches.append((rs1[0], rs1[1], got))
        dfq, dfk, dfv, dck, wacc_fx = _foxq_bwd(sv["proj"], dfx, sv["nl"], sv["ofx"], sv["ck"], gqk2, l, S)
        small[l]["gqk"] = wacc_fx[0:2, :HD] + wacc_fx[0:2, HD:]
        dcum = dck[:, :, :, 0, :].reshape(B, NH, S).transpose(0, 2, 1).reshape(T, NH)
        dff_, wacc_fg = _fgate_bwd(jnp.pad(dcum, ((0, 0), (0, 128 - NH))), sv["proj"], bfp, l, S)
        small[l]["bf"] = wacc_fg[0, :NH]
        dlx, dlg, dpr, dpi, ub, wacc_lru = _lru_bwd(dyl, sv["proj"], sv["hl"], cw8, vp8, wr_d, wi_d, l, S)
        small[l]["lru"] = wacc_lru
        small[l]["wr"] = _diag_blocks(_mm_tn(ub, dpr, f"dw_rgate_{l}"))
        small[l]["wi"] = _diag_blocks(_mm_tn(ub, dpi, f"dw_igate_{l}"))
        dproj = jnp.concatenate(
            [dlx, dlg, dsq, dsk, dsv, dfq, dfk, dfv, dff_], axis=1)
        g_in = _mm_tn(sv["h1"], dproj, f"dw_in_{l}", tnb=N_INP // 3)[:, :N_IN]
        g_in = g_in.reshape(D, N_SHARD, -1).transpose(1, 0, 2)
        g_in_l[l] = (g_in, g_in.astype(BF16))
        dxc, dmod_in, wacc_in = _mix_in_bwd(sv["x1"], dxc, mods[l][1], dproj, wl[l]["mix"]["inp"], l, S)
        dmods[l][1] = dmod_in + dmod1
        small[l]["gn1"] = wacc_in[0]
        if l == 1:
            dxc, _ = ffn_back(l, 0, sv["x0"], dxc, sv, 0)
            rs1 = rs_sibling_phase(ffn_groups(1, 0) + mixer_groups(1) + ffn_groups(1, 1), "1")
        else:
            late = rs_sibling_phase(mixer_groups(0) + ffn_groups(0, 1), "0_late")
            dxc, got = ffn_back(l, 0, sv["x0"], dxc, sv, 0, _rs_to_chips_comm(late[2]))
            batches.append((late[0], late[1], got))
    grad_x = dxc.reshape(B, S, D)

    dmod_loc = jnp.stack([jnp.stack([dmods[l][j][:, 0:3, :] for j in range(3)], axis=1) for l in range(2)])
    drows = 2 * B * 9
    blk3 = _pack_rows([dmod_loc], -(-drows // 8) * 8, F32)
    ag3 = _all_gather8(blk3, "ag_dmod", True)
    dmod_all = ag3[:, :drows].reshape(N_DEV, 2, B, 9 * D).transpose(1, 0, 2, 3).reshape(2, n_all, 9 * D)
    dmod_mine = lax.dynamic_slice(dmod_all, (0, 0, sidx * ada_cols), (2, n_all, ada_cols))
    grad_w_ada = _ada_bwd(c_all, dmod_mine)
    dmod_rows = jnp.pad(dmod_all.transpose(1, 0, 2).reshape(n_all, 2 * 9, D), ((0, 0), (0, 6), (0, 0)))
    grad_b_ada = _sum_lead(dmod_rows, "grad_b_ada")[:2 * 9].reshape(2, 9 * D)

    sm_parts = [
        jnp.stack([small[l]["bf"] for l in range(2)]),
        jnp.stack([small[l]["lru"][4] for l in range(2)]),
        jnp.stack([small[l]["wr"] for l in range(2)]),
        jnp.stack([small[l]["lru"][5] for l in range(2)]),
        jnp.stack([small[l]["wi"] for l in range(2)]),
        jnp.stack([small[l]["lru"][6] for l in range(2)]),
        jnp.stack([small[l]["lru"][7] for l in range(2)]),
        jnp.stack([small[l]["gqk"] for l in range(2)]),
        jnp.stack([small[l]["gmix"] for l in range(2)]),
        jnp.stack([jnp.stack([small[l][f"gn{j}"] for j in range(3)]) for l in range(2)]),
        jnp.stack([small[l]["lru"][0:4] for l in range(2)]),
    ]
    sm_shapes = [p.shape for p in sm_parts]
    sm_rows = -(-sum(p.size for p in sm_parts) // (8 * ROW)) * 8
    ag4 = _all_gather8(_pack_rows(sm_parts, sm_rows, F32), "ag_small_grads", True)
    sm_sum = _sum_lead(ag4, "sum_small_grads").reshape(-1)
    (g_bf, g_cb, g_wr, g_br, g_wi, g_bi, g_lam, g_gqk, g_gmix, g_gn_full, g_cw_full) = _unpack(sm_sum, sm_shapes)
    g_gn = lax.dynamic_slice(g_gn_full, (0, 0, sidx * gn_cols), (2, 3, gn_cols))
    g_cw = lax.dynamic_slice(g_cw_full, (0, 0, sidx * cw_cols), (2, 4, cw_cols))

    last = rs_sibling_phase(ffn_groups(0, 0), "0_first")
    batches.append((last[0], last[1], _run_comm(_rs_to_chips_comm(last[2]), "rs_to_chips_0_first")))
    shapes4 = [w_ffn_up.shape, w_ffn_down.shape, w_in.shape, w_out.shape]
    tensors, places, r2s = [None] * 4, [], []
    for groups, recv_a, recv_b in batches:
        k = 0
        for gi, gname, ps, leads in groups:
            for (pf, _), lead in zip(ps, leads):
                tensors[gi] = _add_chips_into(pf, recv_a[k], recv_b[k], scvec, tensors[gi], shapes4[gi], lead,
                                              f"rs_add_chips_{gname}_{'_'.join(map(str, lead))}")
                places.append((gi, lead))
                r2s.append(pf.shape[1] // 2)
                k += 1
    gw_up, gw_dn, gw_in, gw_out = _share_halves(tensors, places, r2s)

    def upd(w, g, m, v, name):
        sh = w.shape
        two = (w.size // sh[-1], sh[-1])
        dlt, mn, vn = _adamw(w.reshape(two), g.reshape(two), m.reshape(two), v.reshape(two), name)
        return dlt.reshape(sh), mn.reshape(sh), vn.reshape(sh)

    big = {
        "w_ada": (w_ada, grad_w_ada, m_w_ada, v_w_ada),
        "w_ffn_up": (w_ffn_up, gw_up, m_w_ffn_up, v_w_ffn_up),
        "w_ffn_down": (w_ffn_down, gw_dn, m_w_ffn_down, v_w_ffn_down),
        "w_in": (w_in, gw_in, m_w_in, v_w_in),
        "w_out": (w_out, gw_out, m_w_out, v_w_out),
    }
    res = {n: (t[1],) + upd(*t, f"adamw_{n}") for n, t in big.items()}

    smalls = {
        "b_ada": (b_ada, grad_b_ada, m_b_ada, v_b_ada),
        "g_norm": (g_norm, g_gn, m_g_norm, v_g_norm),
        "b_fgate": (b_fgate, g_bf, m_b_fgate, v_b_fgate),
        "conv_w": (conv_w, g_cw, m_conv_w, v_conv_w),
        "conv_b": (conv_b, g_cb, m_conv_b, v_conv_b),
        "w_rgate": (w_rgate, g_wr, m_w_rgate, v_w_rgate),
        "b_rgate": (b_rgate, g_br, m_b_rgate, v_b_rgate),
        "w_igate": (w_igate, g_wi, m_w_igate, v_w_igate),
        "b_igate": (b_igate, g_bi, m_b_igate, v_b_igate),
        "lru_lambda": (lru_lambda, g_lam, m_lru_lambda, v_lru_lambda),
        "g_qk": (g_qk, g_gqk, m_g_qk, v_g_qk),
        "g_mix_out": (g_mix_out, g_gmix, m_g_mix_out, v_g_mix_out),
    }
    names = list(smalls)
    shapes = [smalls[n][0].shape for n in names]
    prow = -(-sum(math.prod(s) for s in shapes) // (8 * ROW)) * 8
    packed = [_pack_rows([smalls[n][i].reshape(shapes[k]) for k, n in enumerate(names)], prow, F32) for i in range(4)]
    outs = _adamw(packed[0], packed[1], packed[2], packed[3], "adamw_small")
    un = [_unpack(o.reshape(-1), shapes) for o in outs]
    for k, n in enumerate(names):
        res[n] = (smalls[n][1].reshape(shapes[k]), un[0][k], un[1][k], un[2][k])

    order = ["w_ada", "b_ada", "g_norm", "w_ffn_up", "w_ffn_down", "w_in", "b_fgate", "conv_w", "conv_b",
             "w_rgate", "b_rgate", "w_igate", "b_igate", "lru_lambda", "g_qk", "g_mix_out", "w_out"]
    return (loss, grad_x, *[res[n][0] for n in order], *[res[n][1] for n in order],
            *[res[n][2] for n in order], *[res[n][3] for n in order])


def TQ_(S):
    return min(TQ, S)
```

```python
import math

import jax
import jax.numpy as jnp
from jax import lax
from jax.experimental import pallas as pl
from jax.experimental.pallas import tpu as pltpu

F32 = jnp.float32
BF16 = jnp.bfloat16
MESH = pl.DeviceIdType.MESH

D = 1024
HD = 64
LW = 512
NH = 4
AW = NH * HD
N_IN = 2564
N_INP = 2688
F_BLK = 2560 // 128
EPS = 1e-6
LRU_C = 8.0
SCALE = HD ** -0.5
NEG = -1e30
TQ = 256

ADAM_LR, ADAM_B1, ADAM_B2, ADAM_EPS, ADAM_WD, ADAM_STEP = 0.001, 0.9, 0.999, 1e-08, 0.01, 10

VMEM_BIG = 56 * 1024 * 1024
N_DEV = 8
N_SHARD = 4
ROW = 1024


def _cp(sem, vmem=None):
    return pltpu.CompilerParams(dimension_semantics=sem, vmem_limit_bytes=vmem)


def _dot(a, b):
    return jnp.dot(a, b, preferred_element_type=F32)


def _dot_nt(a, b):
    return lax.dot_general(a, b, (((1,), (1,)), ((), ())), preferred_element_type=F32)


def _dot_tn(a, b):
    return lax.dot_general(a, b, (((0,), (0,)), ((), ())), preferred_element_type=F32)


def _log1p(e):
    small = e * (1.0 - e * (0.5 - e * (1.0 / 3.0 - e * 0.25)))
    return jnp.where(e < 0.01, small, jnp.log(1.0 + e))


def _expm1_neg(x):
    small = x * (1.0 + x * 0.5 * (1.0 + x * (1.0 / 3.0) * (1.0 + x * 0.25 * (1.0 + x * 0.2))))
    return jnp.where(x > -0.05, small, jnp.exp(x) - 1.0)


def _sigmoid(x):
    return 1.0 / (1.0 + jnp.exp(-x))


_GELU_C = math.sqrt(2.0 / math.pi)


def _gelu_and_grad(x):
    x2 = x * x
    th = jnp.tanh(_GELU_C * (x + 0.044715 * x * x2))
    g = 0.5 * x * (1.0 + th)
    dg = 0.5 * (1.0 + th) + 0.5 * x * (1.0 - th * th) * _GELU_C * (1.0 + 3.0 * 0.044715 * x2)
    return g, dg


def _rms_rows(x):
    rstd = lax.rsqrt(jnp.mean(x * x, axis=-1, keepdims=True) + EPS)
    return x * rstd, rstd


def _rms_bwd(xn, rstd, dyn):
    return rstd * (dyn - xn * jnp.mean(dyn * xn, axis=-1, keepdims=True))


def _colsum(x):
    return jnp.sum(x, axis=0, keepdims=True)


def _rowsum(x):
    return jnp.sum(x, axis=1, keepdims=True)


def _split3(x):
    hi = x.astype(BF16)
    r = x - hi.astype(F32)
    mid = r.astype(BF16)
    lo = (r - mid.astype(F32)).astype(BF16)
    return hi, mid, lo


def _cumsum_mm(x, ones_tri, parts=3):
    ps = _split3(x)[:parts]
    acc = _dot(ps[0], ones_tri)
    for p in ps[1:]:
        acc = acc + _dot(p, ones_tri)
    return acc


def _tri(n, kind):
    r = lax.broadcasted_iota(jnp.int32, (n, n), 0)
    c = lax.broadcasted_iota(jnp.int32, (n, n), 1)
    m = {"row_gt_col": r > c, "row_le_col": r <= c, "row_lt_col": r < c}[kind]
    return jnp.where(m, 1.0, 0.0).astype(BF16)


def _normmod(x, mod_ref):
    xn, rstd = _rms_rows(x)
    h = xn * mod_ref[3:4, :] * mod_ref[1:2, :] + mod_ref[0:1, :]
    return h, xn, rstd


def _normmod_bwd(dh, xn, rstd, mod_ref, dmod_ref, wacc_ref):
    gn = mod_ref[3:4, :]
    sc = mod_ref[1:2, :]
    dmod_ref[0:1, :] += _colsum(dh)
    t = _colsum(dh * xn)
    dmod_ref[1:2, :] += t * gn
    wacc_ref[0:1, :] += t * sc
    return _rms_bwd(xn, rstd, dh * (gn * sc))


def _tile(n, want):
    t = min(n, want)
    while n % t:
        t //= 2
    return t


def _tile_div8(n, cap, mult=8):
    best = mult
    for t in range(mult, min(n, cap) + 1, mult):
        if n % t == 0:
            best = t
    assert n % best == 0
    return best


def _ffn_fwd(x, mod, wup, wdn, l, j, S, comm=None):
    T = x.shape[0]
    tf = wup.shape[-1]
    nk = 2
    tm = _tile(S, 512)
    tpb = S // tm
    nt = T // tm
    c_args, c_specs, c_outs, c_scr = _hosted(comm)
    n_ci, n_co = len(c_args), len(c_outs)

    def body(*refs):
        x_ref, mod_ref, wg_ref, wu_ref, wd_ref = refs[:5]
        c_in = refs[5:5 + n_ci]
        xo_ref, g_ref, u_ref, f_ref = refs[5 + n_ci:9 + n_ci]
        c_out = refs[9 + n_ci:9 + n_ci + n_co]
        h_sc, acc_sc = refs[9 + n_ci + n_co:11 + n_ci + n_co]
        c_sems = refs[11 + n_ci + n_co:]
        i = pl.program_id(0)
        k = pl.program_id(1)
        if comm is not None:
            @pl.when((i == 0) & (k == 0))
            def _():
                comm["start"](c_in, c_out, c_sems)

        @pl.when(k == 0)
        def _():
            h, _, _ = _normmod(x_ref[...], mod_ref)
            h_sc[...] = h.astype(BF16)
            acc_sc[...] = jnp.zeros_like(acc_sc)

        h = h_sc[...]
        g = _dot(h, wg_ref[...])
        u = _dot(h, wu_ref[...])
        g_ref[...] = g.astype(BF16)
        u_ref[...] = u.astype(BF16)
        a = (g * _sigmoid(g)) * u
        acc_sc[...] += _dot(a.astype(BF16), wd_ref[...])

        @pl.when(k == nk - 1)
        def _():
            f = acc_sc[...]
            f_ref[...] = f.astype(BF16)
            xo_ref[...] = x_ref[...] + (0.5 * mod_ref[2:3, :]) * f

        if comm is not None:
            @pl.when((i == nt - 1) & (k == nk - 1))
            def _():
                comm["finish"](c_in, c_out, c_sems)

    anyspec = pl.BlockSpec(memory_space=pl.ANY)
    out = pl.pallas_call(
        body, name=f"ffn_fwd_{l}_{j}",
        grid=(nt, nk),
        in_specs=[
            pl.BlockSpec((tm, D), lambda i, k: (i, 0)),
            pl.BlockSpec((None, 8, D), lambda i, k: (i // tpb, 0, 0)),
            pl.BlockSpec((None, D, tf), lambda i, k: (k, 0, 0)),
            pl.BlockSpec((None, D, tf), lambda i, k: (nk + k, 0, 0)),
            pl.BlockSpec((tf, D), lambda i, k: (k, 0)),
        ] + c_specs,
        out_specs=[
            pl.BlockSpec((tm, D), lambda i, k: (i, 0)),
            pl.BlockSpec((tm, tf), lambda i, k: (i, k)),
            pl.BlockSpec((tm, tf), lambda i, k: (i, k)),
            pl.BlockSpec((tm, D), lambda i, k: (i, 0)),
        ] + [anyspec] * n_co,
        out_shape=[
            jax.ShapeDtypeStruct((T, D), F32),
            jax.ShapeDtypeStruct((T, nk * tf), BF16),
            jax.ShapeDtypeStruct((T, nk * tf), BF16),
            jax.ShapeDtypeStruct((T, D), BF16),
        ] + c_outs,
        scratch_shapes=[pltpu.VMEM((tm, D), BF16), pltpu.VMEM((tm, D), F32)] + c_scr,
        compiler_params=_cp(("arbitrary", "arbitrary"), VMEM_BIG),
    )(x, mod, wup, wup, wdn, *c_args)
    return out[0], out[1], out[2], out[3], list(out[4:])


def _ffn_bwd(x, dy, mod, f, g, u, wup, wdn, l, j, S, comm=None):
    T = x.shape[0]
    tf = wup.shape[-1]
    nk = 2
    tm = _tile(S, 256)
    tpb = S // tm
    nt = T // tm
    c_args, c_specs, c_outs, c_scr = _hosted(comm)
    n_ci, n_co = len(c_args), len(c_outs)

    def body(*refs):
        x_ref, dy_ref, mod_ref, f_ref, g_ref, u_ref, wup_ref, wd_ref = refs[:8]
        c_in = refs[8:8 + n_ci]
        dx_ref, dmod_ref, wacc_ref, h_ref, df_ref, a_ref, dgu_ref = refs[8 + n_ci:15 + n_ci]
        c_out = refs[15 + n_ci:15 + n_ci + n_co]
        c_sems = refs[15 + n_ci + n_co:]
        i = pl.program_id(0)

        @pl.when(i == 0)
        def _():
            wacc_ref[...] = jnp.zeros_like(wacc_ref)
            if comm is not None:
                comm["start"](c_in, c_out, c_sems)

        @pl.when(i % tpb == 0)
        def _():
            dmod_ref[...] = jnp.zeros_like(dmod_ref)

        dy_ = dy_ref[...]
        h, xn, rstd = _normmod(x_ref[...], mod_ref)
        h_ref[...] = h.astype(BF16)
        dfb = ((0.5 * mod_ref[2:3, :]) * dy_).astype(BF16)
        df_ref[...] = dfb
        dmod_ref[2:3, :] += _colsum(0.5 * f_ref[...].astype(F32) * dy_)
        dh = None
        for k in range(nk):
            cols = slice(k * tf, (k + 1) * tf)
            da = _dot_nt(dfb, wd_ref[cols, :])
            gg = g_ref[:, cols].astype(F32)
            uu = u_ref[:, cols].astype(F32)
            sig = _sigmoid(gg)
            s = gg * sig
            a_ref[:, cols] = (s * uu).astype(BF16)
            du = (da * s).astype(BF16)
            dg = (da * uu * (sig * (1.0 + gg * (1.0 - sig)))).astype(BF16)
            dgu_ref[0, :, cols] = dg
            dgu_ref[1, :, cols] = du
            part = _dot_nt(dg, wup_ref[k]) + _dot_nt(du, wup_ref[nk + k])
            dh = part if dh is None else dh + part
        dx_ref[...] = dy_ + _normmod_bwd(dh, xn, rstd, mod_ref, dmod_ref, wacc_ref)

        if comm is not None:
            @pl.when(i == nt - 1)
            def _():
                comm["finish"](c_in, c_out, c_sems)

    once = pl.Buffered(1)
    anyspec = pl.BlockSpec(memory_space=pl.ANY)
    out = pl.pallas_call(
        body, name=f"ffn_bwd_{l}_{j}",
        grid=(nt,),
        in_specs=[
            pl.BlockSpec((tm, D), lambda i: (i, 0)),
            pl.BlockSpec((tm, D), lambda i: (i, 0)),
            pl.BlockSpec((None, 8, D), lambda i: (i // tpb, 0, 0)),
            pl.BlockSpec((tm, D), lambda i: (i, 0)),
            pl.BlockSpec((tm, nk * tf), lambda i: (i, 0)),
            pl.BlockSpec((tm, nk * tf), lambda i: (i, 0)),
            pl.BlockSpec((2 * nk, D, tf), lambda i: (0, 0, 0), pipeline_mode=once),
            pl.BlockSpec((nk * tf, D), lambda i: (0, 0), pipeline_mode=once),
        ] + c_specs,
        out_specs=[
            pl.BlockSpec((tm, D), lambda i: (i, 0)),
            pl.BlockSpec((None, 8, D), lambda i: (i // tpb, 0, 0)),
            pl.BlockSpec((8, D), lambda i: (0, 0)),
            pl.BlockSpec((tm, D), lambda i: (i, 0)),
            pl.BlockSpec((tm, D), lambda i: (i, 0)),
            pl.BlockSpec((tm, nk * tf), lambda i: (i, 0)),
            pl.BlockSpec((2, tm, nk * tf), lambda i: (0, i, 0)),
        ] + [anyspec] * n_co,
        out_shape=[
            jax.ShapeDtypeStruct((T, D), F32),
            jax.ShapeDtypeStruct((T // S, 8, D), F32),
            jax.ShapeDtypeStruct((8, D), F32),
            jax.ShapeDtypeStruct((T, D), BF16),
            jax.ShapeDtypeStruct((T, D), BF16),
            jax.ShapeDtypeStruct((T, nk * tf), BF16),
            jax.ShapeDtypeStruct((2, T, nk * tf), BF16),
        ] + c_outs,
        scratch_shapes=c_scr,
        compiler_params=_cp(("arbitrary",), VMEM_BIG),
    )(x, dy, mod, f, g, u, wup, wdn, *c_args)
    return tuple(out[:7]) + (list(out[7:]),)


def _mm_tn(a, b, name, tma=None, tnb=None, split_n=False, with_bf16=False):
    T, M = a.shape
    b3 = b if b.ndim == 3 else b[None]
    nb, _, N = b3.shape
    tma = tma or M
    tnb = tnb or N
    npb = N // tnb
    tt = _tile(T, 1024)
    nt = T // tt

    def body(a_ref, b_ref, o_ref, *ob_ref):
        @pl.when(pl.program_id(2) == 0)
        def _():
            o_ref[...] = jnp.zeros_like(o_ref)

        o_ref[...] += _dot_tn(a_ref[...], b_ref[...])

        if with_bf16:
            @pl.when(pl.program_id(2) == nt - 1)
            def _():
                ob_ref[0][...] = o_ref[...].astype(BF16)

    if split_n:
        shape = (nb * npb, M, tnb)
        out_spec = pl.BlockSpec((None, tma, tnb), lambda m, n, t: (n, m, 0))
    else:
        assert nb == 1
        shape = (M, N)
        out_spec = pl.BlockSpec((tma, tnb), lambda m, n, t: (m, n))
    dts = (F32, BF16) if with_bf16 else (F32,)
    out = pl.pallas_call(
        body, name=name,
        grid=(M // tma, nb * npb, nt),
        in_specs=[pl.BlockSpec((tt, tma), lambda m, n, t: (t, m)),
                  pl.BlockSpec((None, tt, tnb), lambda m, n, t: (n // npb, t, n % npb))],
        out_specs=[out_spec] * len(dts),
        out_shape=[jax.ShapeDtypeStruct(shape, dt) for dt in dts],
        compiler_params=_cp(("arbitrary", "arbitrary", "arbitrary"), VMEM_BIG),
    )(a, b3)
    return tuple(out) if with_bf16 else out[0]


def _mix_in_fwd(x, mod, winp, l, S):
    T = x.shape[0]
    tm = _tile(S, 512)
    tpb = S // tm

    def body(x_ref, mod_ref, w_ref, h_ref, p_ref):
        h, _, _ = _normmod(x_ref[...], mod_ref)
        hb = h.astype(BF16)
        h_ref[...] = hb
        p_ref[...] = _dot(hb, w_ref[...])

    return pl.pallas_call(
        body, name=f"mix_in_fwd_{l}",
        grid=(T // tm,),
        in_specs=[pl.BlockSpec((tm, D), lambda i: (i, 0)),
                  pl.BlockSpec((None, 8, D), lambda i: (i // tpb, 0, 0)),
                  pl.BlockSpec((D, N_INP), lambda i: (0, 0))],
        out_specs=[pl.BlockSpec((tm, D), lambda i: (i, 0)),
                   pl.BlockSpec((tm, N_INP), lambda i: (i, 0))],
        out_shape=[jax.ShapeDtypeStruct((T, D), BF16), jax.ShapeDtypeStruct((T, N_INP), F32)],
        compiler_params=_cp(("arbitrary",), VMEM_BIG),
    )(x, mod, winp)


def _mix_in_bwd(x, dres, mod, dproj, winp, l, S):
    T = x.shape[0]
    tm = _tile(S, 512)
    tpb = S // tm

    def body(x_ref, dr_ref, mod_ref, dp_ref, w_ref, dx_ref, dmod_ref, wacc_ref):
        i = pl.program_id(0)

        @pl.when(i == 0)
        def _():
            wacc_ref[...] = jnp.zeros_like(wacc_ref)

        @pl.when(i % tpb == 0)
        def _():
            dmod_ref[...] = jnp.zeros_like(dmod_ref)

        dh = _dot_nt(dp_ref[...], w_ref[...])
        _, xn, rstd = _normmod(x_ref[...], mod_ref)
        dx_ref[...] = dr_ref[...] + _normmod_bwd(dh, xn, rstd, mod_ref, dmod_ref, wacc_ref)

    return pl.pallas_call(
        body, name=f"mix_in_bwd_{l}",
        grid=(T // tm,),
        in_specs=[pl.BlockSpec((tm, D), lambda i: (i, 0)),
                  pl.BlockSpec((tm, D), lambda i: (i, 0)),
                  pl.BlockSpec((None, 8, D), lambda i: (i // tpb, 0, 0)),
                  pl.BlockSpec((tm, N_INP), lambda i: (i, 0)),
                  pl.BlockSpec((D, N_INP), lambda i: (0, 0))],
        out_specs=[pl.BlockSpec((tm, D), lambda i: (i, 0)),
                   pl.BlockSpec((None, 8, D), lambda i: (i // tpb, 0, 0)),
                   pl.BlockSpec((8, D), lambda i: (0, 0))],
        out_shape=[jax.ShapeDtypeStruct((T, D), F32),
                   jax.ShapeDtypeStruct((T // S, 8, D), F32),
                   jax.ShapeDtypeStruct((8, D), F32)],
        compiler_params=_cp(("arbitrary",), VMEM_BIG),
    )(x, dres, mod, dproj, winp)


_GROUPS = ((0, LW), (LW, LW + AW), (LW + AW, D))


def _mix_out_fwd(x, ylru, osb, ofox, mod, gmix, wout, l, S):
    T = x.shape[0]
    tm = _tile(S, 512)
    tpb = S // tm

    def body(x_ref, yl_ref, sb_ref, fx_ref, mod_ref, gm_ref, w_ref, xo_ref, y_ref, mo_ref):
        for src, (lo, hi) in zip((yl_ref, sb_ref, fx_ref), _GROUPS):
            vn, _ = _rms_rows(src[...])
            y_ref[:, lo:hi] = (vn * gm_ref[0:1, lo:hi]).astype(BF16)
        mo = _dot(y_ref[...], w_ref[...])
        mo_ref[...] = mo.astype(BF16)
        xo_ref[...] = x_ref[...] + mod_ref[2:3, :] * mo

    return pl.pallas_call(
        body, name=f"mix_out_fwd_{l}",
        grid=(T // tm,),
        in_specs=[pl.BlockSpec((tm, D), lambda i: (i, 0)),
                  pl.BlockSpec((tm, LW), lambda i: (i, 0)),
                  pl.BlockSpec((tm, AW), lambda i: (i, 0)),
                  pl.BlockSpec((tm, AW), lambda i: (i, 0)),
                  pl.BlockSpec((None, 8, D), lambda i: (i // tpb, 0, 0)),
                  pl.BlockSpec((None, 8, D), lambda i: (l, 0, 0)),
                  pl.BlockSpec((D, D), lambda i: (0, 0))],
        out_specs=[pl.BlockSpec((tm, D), lambda i: (i, 0)),
                   pl.BlockSpec((tm, D), lambda i: (i, 0)),
                   pl.BlockSpec((tm, D), lambda i: (i, 0))],
        out_shape=[jax.ShapeDtypeStruct((T, D), F32),
                   jax.ShapeDtypeStruct((T, D), BF16),
                   jax.ShapeDtypeStruct((T, D), BF16)],
        compiler_params=_cp(("arbitrary",), VMEM_BIG),
    )(x, ylru, osb, ofox, mod, gmix, wout)


def _mix_out_bwd(dx2, ylru, osb, ofox, mo, mod, gmix, wout, l, S):
    T = dx2.shape[0]
    tm = _tile(S, 512)
    tpb = S // tm

    def body(dx_ref, yl_ref, sb_ref, fx_ref, mo_ref, mod_ref, gm_ref, w_ref,
             dyl_ref, dsb_ref, dfx_ref, dmo_ref, dmod_ref, wacc_ref):
        i = pl.program_id(0)

        @pl.when(i == 0)
        def _():
            wacc_ref[...] = jnp.zeros_like(wacc_ref)

        @pl.when(i % tpb == 0)
        def _():
            dmod_ref[...] = jnp.zeros_like(dmod_ref)

        dx = dx_ref[...]
        dmod_ref[2:3, :] += _colsum(mo_ref[...].astype(F32) * dx)
        dmo = (mod_ref[2:3, :] * dx).astype(BF16)
        dmo_ref[...] = dmo
        dy = _dot_nt(dmo, w_ref[...])
        for src, dst, (lo, hi) in zip((yl_ref, sb_ref, fx_ref), (dyl_ref, dsb_ref, dfx_ref), _GROUPS):
            vn, rstd = _rms_rows(src[...])
            dyg = dy[:, lo:hi]
            wacc_ref[0:1, lo:hi] += _colsum(dyg * vn)
            dst[...] = _rms_bwd(vn, rstd, dyg * gm_ref[0:1, lo:hi])

    return pl.pallas_call(
        body, name=f"mix_out_bwd_{l}",
        grid=(T // tm,),
        in_specs=[pl.BlockSpec((tm, D), lambda i: (i, 0)),
                  pl.BlockSpec((tm, LW), lambda i: (i, 0)),
                  pl.BlockSpec((tm, AW), lambda i: (i, 0)),
                  pl.BlockSpec((tm, AW), lambda i: (i, 0)),
                  pl.BlockSpec((tm, D), lambda i: (i, 0)),
                  pl.BlockSpec((None, 8, D), lambda i: (i // tpb, 0, 0)),
                  pl.BlockSpec((None, 8, D), lambda i: (l, 0, 0)),
                  pl.BlockSpec((D, D), lambda i: (0, 0))],
        out_specs=[pl.BlockSpec((tm, LW), lambda i: (i, 0)),
                   pl.BlockSpec((tm, AW), lambda i: (i, 0)),
                   pl.BlockSpec((tm, AW), lambda i: (i, 0)),
                   pl.BlockSpec((tm, D), lambda i: (i, 0)),
                   pl.BlockSpec((None, 8, D), lambda i: (i // tpb, 0, 0)),
                   pl.BlockSpec((8, D), lambda i: (0, 0))],
        out_shape=[jax.ShapeDtypeStruct((T, LW), F32),
                   jax.ShapeDtypeStruct((T, AW), F32),
                   jax.ShapeDtypeStruct((T, AW), F32),
                   jax.ShapeDtypeStruct((T, D), BF16),
                   jax.ShapeDtypeStruct((T // S, 8, D), F32),
                   jax.ShapeDtypeStruct((8, D), F32)],
        compiler_params=_cp(("arbitrary",), VMEM_BIG),
    )(dx2, ylru, osb, ofox, mo, mod, gmix, wout)


def _loss_head(y, tgt, S):
    T = y.shape[0]
    tm = _tile(S, 512)

    def body(y_ref, t_ref, dy_ref, l_ref):
        @pl.when(pl.program_id(0) == 0)
        def _():
            l_ref[...] = jnp.zeros_like(l_ref)

        d = y_ref[...] - t_ref[...]
        dy_ref[...] = d * (1.0 / D)
        l_ref[...] += (0.5 / D) * _rowsum(_colsum(d * d))

    return pl.pallas_call(
        body, name="loss_head",
        grid=(T // tm,),
        in_specs=[pl.BlockSpec((tm, D), lambda i: (i, 0)), pl.BlockSpec((tm, D), lambda i: (i, 0))],
        out_specs=[pl.BlockSpec((tm, D), lambda i: (i, 0)), pl.BlockSpec((8, 128), lambda i: (0, 0))],
        out_shape=[jax.ShapeDtypeStruct((T, D), F32), jax.ShapeDtypeStruct((8, 128), F32)],
        compiler_params=_cp(("arbitrary",)),
    )(y, tgt)


def _lru_gates(u, vp_ref, wr_ref, wi_ref):
    ub = u.astype(BF16)
    r = _sigmoid(_dot(ub, wr_ref[...]) + vp_ref[1:2, :])
    ig = _sigmoid(_dot(ub, wi_ref[...]) + vp_ref[2:3, :])
    lam = vp_ref[3:4, :]
    sp = jnp.maximum(-lam, 0.0) + _log1p(jnp.exp(-jnp.abs(lam)))
    log_a = (-LRU_C) * r * sp
    a = jnp.exp(log_a)
    mult = jnp.sqrt(-_expm1_neg(2.0 * log_a))
    return ub, r, ig, sp, a, mult


def _conv_taps(x, xp, row, cw_ref):
    xs = [x]
    for d in (1, 2, 3):
        xs.append(jnp.where(row >= d, pltpu.roll(x, d, 0), pltpu.roll(xp, d, 0)))
    u = xs[0] * cw_ref[3:4, :]
    for d in (1, 2, 3):
        u = u + xs[d] * cw_ref[3 - d:4 - d, :]
    return xs, u


def _lru_fwd(proj, cw, vp, wr, wi, l, S):
    T = proj.shape[0]
    ts = _tile(S, 256)
    nb = S // ts

    def body(x_ref, lg_ref, cw_ref, vp_ref, wr_ref, wi_ref, y_ref, h_ref, xp_sc, hc_sc):
        @pl.when(pl.program_id(1) == 0)
        def _():
            xp_sc[...] = jnp.zeros_like(xp_sc)
            hc_sc[...] = jnp.zeros_like(hc_sc)

        row = lax.broadcasted_iota(jnp.int32, (ts, LW), 0)
        x = x_ref[...]
        _, u = _conv_taps(x, xp_sc[...], row, cw_ref)
        u = u + vp_ref[0:1, :]
        xp_sc[...] = x
        _, _, ig, _, a, mult = _lru_gates(u, vp_ref, wr_ref, wi_ref)
        bv = mult * (ig * u)
        av = a
        d = 1
        while d < ts:
            a_s = jnp.where(row >= d, pltpu.roll(av, d, 0), 1.0)
            b_s = jnp.where(row >= d, pltpu.roll(bv, d, 0), 0.0)
            bv = av * b_s + bv
            av = av * a_s
            d *= 2
        h = bv + av * hc_sc[7:8, :]
        hc_sc[...] = h[ts - 8:ts, :]
        h_ref[...] = h
        gl, _ = _gelu_and_grad(lg_ref[...])
        y_ref[...] = h * gl

    return pl.pallas_call(
        body, name=f"lru_fwd_{l}",
        grid=(T // S, nb),
        in_specs=[pl.BlockSpec((ts, LW), lambda b, j: (b * nb + j, 0)),
                  pl.BlockSpec((ts, LW), lambda b, j: (b * nb + j, 1)),
                  pl.BlockSpec((None, 8, LW), lambda b, j: (l, 0, 0)),
                  pl.BlockSpec((None, 8, LW), lambda b, j: (l, 0, 0)),
                  pl.BlockSpec((None, LW, LW), lambda b, j: (l, 0, 0)),
                  pl.BlockSpec((None, LW, LW), lambda b, j: (l, 0, 0))],
        out_specs=[pl.BlockSpec((ts, LW), lambda b, j: (b * nb + j, 0)),
                   pl.BlockSpec((ts, LW), lambda b, j: (b * nb + j, 0))],
        out_shape=[jax.ShapeDtypeStruct((T, LW), F32), jax.ShapeDtypeStruct((T, LW), F32)],
        scratch_shapes=[pltpu.VMEM((ts, LW), F32), pltpu.VMEM((8, LW), F32)],
        compiler_params=_cp(("arbitrary", "arbitrary")),
    )(proj, proj, cw, vp, wr, wi)


def _lru_bwd(dyl, proj, h, cw, vp, wr, wi, l, S):
    T = proj.shape[0]
    ts = _tile(S, 256)
    nb = S // ts

    def body(dy_ref, x_ref, xprev_ref, lg_ref, h_ref, hprev_ref, cw_ref, vp_ref, wr_ref, wi_ref,
             dx_ref, dlg_ref, dpr_ref, dpi_ref, ub_ref, wacc_ref, gc_sc, af_sc, dun_sc):
        b = pl.program_id(0)
        j = pl.program_id(1)
        first = j == nb - 1

        @pl.when((b == 0) & (j == 0))
        def _():
            wacc_ref[...] = jnp.zeros_like(wacc_ref)

        @pl.when(j == 0)
        def _():
            gc_sc[...] = jnp.zeros_like(gc_sc)
            af_sc[...] = jnp.ones_like(af_sc)
            dun_sc[...] = jnp.zeros_like(dun_sc)

        row = lax.broadcasted_iota(jnp.int32, (ts, LW), 0)
        keep = jnp.where(first, 0.0, 1.0)
        x = x_ref[...]
        xs, u = _conv_taps(x, xprev_ref[...] * keep, row, cw_ref)
        u = u + vp_ref[0:1, :]
        ub, r, ig, sp, a, mult = _lru_gates(u, vp_ref, wr_ref, wi_ref)
        ub_ref[...] = ub
        hh = h_ref[...]
        h_m1 = jnp.where(row >= 1, pltpu.roll(hh, 1, 0), pltpu.roll(hprev_ref[...] * keep, 1, 0))
        dy = dy_ref[...]
        gl, dgl = _gelu_and_grad(lg_ref[...])
        dlg_ref[...] = (dy * hh * dgl).astype(BF16)
        bv = dy * gl
        av = jnp.where(row < ts - 1, pltpu.roll(a, ts - 1, 0), af_sc[0:1, :])
        d = 1
        while d < ts:
            a_s = jnp.where(row < ts - d, pltpu.roll(av, ts - d, 0), 1.0)
            b_s = jnp.where(row < ts - d, pltpu.roll(bv, ts - d, 0), 0.0)
            bv = av * b_s + bv
            av = av * a_s
            d *= 2
        gt = bv + av * gc_sc[0:1, :]
        gc_sc[...] = gt[0:8, :]
        af_sc[...] = a[0:8, :]
        da = gt * h_m1
        d_ig = gt * mult * u
        d_mult = gt * ig * u
        du = gt * mult * ig
        dlog_a = da * a - d_mult * (a * a) / mult
        dpre_r = (dlog_a * ((-LRU_C) * sp)) * r * (1.0 - r)
        dpre_i = d_ig * ig * (1.0 - ig)
        lam = vp_ref[3:4, :]
        wacc_ref[7:8, :] += _colsum(dlog_a * r) * (LRU_C * _sigmoid(-lam))
        wacc_ref[5:6, :] += _colsum(dpre_r)
        wacc_ref[6:7, :] += _colsum(dpre_i)
        dprb = dpre_r.astype(BF16)
        dpib = dpre_i.astype(BF16)
        dpr_ref[...] = dprb
        dpi_ref[...] = dpib
        du = du + _dot_nt(dprb, wr_ref[...]) + _dot_nt(dpib, wi_ref[...])
        wacc_ref[4:5, :] += _colsum(du)
        dun = dun_sc[...]
        dx = du * cw_ref[3:4, :]
        wacc_ref[3:4, :] += _colsum(du * xs[0])
        for dd in (1, 2, 3):
            du_s = jnp.where(row < ts - dd, pltpu.roll(du, ts - dd, 0), pltpu.roll(dun, ts - dd, 0))
            dx = dx + du_s * cw_ref[3 - dd:4 - dd, :]
            wacc_ref[3 - dd:4 - dd, :] += _colsum(du * xs[dd])
        dun_sc[...] = du
        dx_ref[...] = dx.astype(BF16)

    def tb(b, j):
        return b * nb + (nb - 1 - j)

    def tbp(b, j):
        return b * nb + jnp.maximum(nb - 2 - j, 0)

    return pl.pallas_call(
        body, name=f"lru_bwd_{l}",
        grid=(T // S, nb),
        in_specs=[pl.BlockSpec((ts, LW), lambda b, j: (tb(b, j), 0)),
                  pl.BlockSpec((ts, LW), lambda b, j: (tb(b, j), 0)),
                  pl.BlockSpec((ts, LW), lambda b, j: (tbp(b, j), 0)),
                  pl.BlockSpec((ts, LW), lambda b, j: (tb(b, j), 1)),
                  pl.BlockSpec((ts, LW), lambda b, j: (tb(b, j), 0)),
                  pl.BlockSpec((ts, LW), lambda b, j: (tbp(b, j), 0)),
                  pl.BlockSpec((None, 8, LW), lambda b, j: (l, 0, 0)),
                  pl.BlockSpec((None, 8, LW), lambda b, j: (l, 0, 0)),
                  pl.BlockSpec((None, LW, LW), lambda b, j: (l, 0, 0)),
                  pl.BlockSpec((None, LW, LW), lambda b, j: (l, 0, 0))],
        out_specs=[pl.BlockSpec((ts, LW), lambda b, j: (tb(b, j), 0)),
                   pl.BlockSpec((ts, LW), lambda b, j: (tb(b, j), 0)),
                   pl.BlockSpec((ts, LW), lambda b, j: (tb(b, j), 0)),
                   pl.BlockSpec((ts, LW), lambda b, j: (tb(b, j), 0)),
                   pl.BlockSpec((ts, LW), lambda b, j: (tb(b, j), 0)),
                   pl.BlockSpec((8, LW), lambda b, j: (0, 0))],
        out_shape=[jax.ShapeDtypeStruct((T, LW), BF16),
                   jax.ShapeDtypeStruct((T, LW), BF16),
                   jax.ShapeDtypeStruct((T, LW), BF16),
                   jax.ShapeDtypeStruct((T, LW), BF16),
                   jax.ShapeDtypeStruct((T, LW), BF16),
                   jax.ShapeDtypeStruct((8, LW), F32)],
        scratch_shapes=[pltpu.VMEM((8, LW), F32), pltpu.VMEM((8, LW), F32), pltpu.VMEM((ts, LW), F32)],
        compiler_params=_cp(("arbitrary", "arbitrary")),
    )(dyl, proj, proj, proj, h, h, cw, vp, wr, wi)


def _fgate_fwd(proj, bfp, l, S):
    T = proj.shape[0]

    def body(x_ref, b_ref, o_ref):
        z = x_ref[...] + b_ref[0:1, :]
        v = jnp.minimum(z, 0.0) - _log1p(jnp.exp(-jnp.abs(z)))
        row = lax.broadcasted_iota(jnp.int32, (S, 128), 0)
        d = 1
        while d < S:
            v = v + jnp.where(row >= d, pltpu.roll(v, d, 0), 0.0)
            d *= 2
        o_ref[...] = v

    return pl.pallas_call(
        body, name=f"fgate_fwd_{l}",
        grid=(T // S,),
        in_specs=[pl.BlockSpec((S, 128), lambda b: (b, F_BLK)),
                  pl.BlockSpec((None, 8, 128), lambda b: (l, 0, 0))],
        out_specs=pl.BlockSpec((S, 128), lambda b: (b, 0)),
        out_shape=jax.ShapeDtypeStruct((T, 128), F32),
        compiler_params=_cp(("arbitrary",)),
    )(proj, bfp)


def _fgate_bwd(dcum, proj, bfp, l, S):
    T = proj.shape[0]

    def body(d_ref, x_ref, b_ref, o_ref, wacc_ref):
        @pl.when(pl.program_id(0) == 0)
        def _():
            wacc_ref[...] = jnp.zeros_like(wacc_ref)

        v = d_ref[...]
        row = lax.broadcasted_iota(jnp.int32, (S, 128), 0)
        d = 1
        while d < S:
            v = v + jnp.where(row < S - d, pltpu.roll(v, S - d, 0), 0.0)
            d *= 2
        z = x_ref[...] + b_ref[0:1, :]
        dz = v * _sigmoid(-z)
        o_ref[...] = dz.astype(BF16)
        wacc_ref[0:1, :] += _colsum(dz)

    return pl.pallas_call(
        body, name=f"fgate_bwd_{l}",
        grid=(T // S,),
        in_specs=[pl.BlockSpec((S, 128), lambda b: (b, 0)),
                  pl.BlockSpec((S, 128), lambda b: (b, F_BLK)),
                  pl.BlockSpec((None, 8, 128), lambda b: (l, 0, 0))],
        out_specs=[pl.BlockSpec((S, 128), lambda b: (b, 0)), pl.BlockSpec((8, 128), lambda b: (0, 0))],
        out_shape=[jax.ShapeDtypeStruct((T, 128), BF16), jax.ShapeDtypeStruct((8, 128), F32)],
        compiler_params=_cp(("arbitrary",)),
    )(dcum, proj, bfp)


SBQ_BLK, SBK_BLK, SBV_BLK = 8, 10, 12
FXQ_BLK, FXK_BLK, FXV_BLK = 14, 16, 18
PAIR = 2 * HD


def _lane_masks():
    lane = lax.broadcasted_iota(jnp.int32, (1, PAIR), 1)
    return lane, lane < HD


def _pair_select(m0, a0, a1):
    return jnp.where(m0, a0, a1)


def _pair_split(x, m0):
    return jnp.where(m0, x, 0.0).astype(BF16), jnp.where(m0, 0.0, x).astype(BF16)


def _pair_mean(x, m0):
    s0 = _rowsum(jnp.where(m0, x, 0.0))
    s1 = _rowsum(x) - s0
    return jnp.where(m0, s0, s1) * (1.0 / HD)


def _pair_rms(x, m0):
    rstd = lax.rsqrt(_pair_mean(x * x, m0) + EPS)
    return x * rstd, rstd


def _pair_rms_bwd(xn, rstd, dyn, m0):
    return rstd * (dyn - xn * _pair_mean(dyn * xn, m0))


def _logsig2(z):
    l1p = jnp.log(1.0 + jnp.exp(-jnp.abs(z)))
    lb = jnp.minimum(z, 0.0) - l1p
    return lb, lb - z


def _rows(ref, blk, size):
    return ref[pl.ds(pl.multiple_of(blk * size, size), size), :]


def _loop_grouped(n, body, init, groups=(4, 2, 1)):
    c, done = init, 0
    for per in groups:
        def several(i, cc, per=per, done=done):
            for j in range(per):
                cc = body(done + per * i + j, cc)
            return cc

        trips = (n - done) // per
        c = lax.fori_loop(0, trips, several, c)
        done = done + trips * per
    return c


def _transpose_blocks(src_ref, dst_sc, nblk, blk):
    for kb in range(nblk):
        dst_sc[kb] = src_ref[kb * blk:(kb + 1) * blk, :].astype(F32).T.astype(BF16)


def _sbq_fwd(proj, l, S, comm=None):
    T = proj.shape[0]
    tb = TQ_(S)
    nb = S // tb
    nbat = T // S
    c_args, c_specs, c_outs, c_scr = _hosted(comm)
    n_ci, n_co = len(c_args), len(c_outs)

    def body(*refs):
        q_ref, k_ref, v_ref = refs[:3]
        c_in = refs[3:3 + n_ci]
        o_ref, t1_ref = refs[3 + n_ci:5 + n_ci]
        c_out = refs[5 + n_ci:5 + n_ci + n_co]
        kt_sc, vb_sc = refs[5 + n_ci + n_co:7 + n_ci + n_co]
        c_sems = refs[7 + n_ci + n_co:]
        step = pl.program_id(0) * 2 + pl.program_id(1)
        if comm is not None:
            @pl.when(step == 0)
            def _():
                comm["start"](c_in, c_out, c_sems)

        _transpose_blocks(k_ref, kt_sc, nb, tb)
        vb_sc[...] = v_ref[...].astype(BF16)
        lane, m0 = _lane_masks()
        tri = _tri(tb, "row_gt_col")
        past = lax.broadcasted_iota(jnp.int32, (tb, tb), 1) < lax.broadcasted_iota(jnp.int32, (tb, tb), 0)

        def qloop(qb, carry):
            qh = _pair_split(_rows(q_ref, qb, tb) * SCALE, m0)

            def scores(kb):
                return tuple(_dot(qh[h], kt_sc[kb]) for h in range(2))

            def block(kb, kb_next, z, c, masked):
                mid = []
                for h in range(2):
                    lb, l1 = _logsig2(z[h])
                    if masked:
                        l1 = jnp.where(past, l1, 0.0)
                    mid.append((lb, l1, _cumsum_mm(l1, tri, parts=2)))
                z_next = scores(kb_next)
                pv, runs = [], []
                for h in range(2):
                    lb, l1, cs = mid[h]
                    w = jnp.exp(lb + (cs + c[h][1]))
                    if masked:
                        w = jnp.where(past, w, 0.0)
                    pv.append(_dot(w.astype(BF16), _rows(vb_sc, kb, tb)))
                    runs.append(c[h][1] + (cs[:, 0:1] + l1[:, 0:1]))
                return z_next, tuple((c[h][0] + pv[h], runs[h]) for h in range(2))

            zero = (jnp.zeros((tb, PAIR), F32), jnp.zeros((tb, 1), F32))
            z, c = block(qb, jnp.maximum(qb - 1, 0), scores(qb), (zero, zero), True)

            def off_diag(i, zc):
                kb = qb - 1 - i
                return block(kb, jnp.maximum(kb - 1, 0), zc[0], zc[1], False)

            _, c = _loop_grouped(qb, off_diag, (z, c))
            r0 = pl.multiple_of(qb * tb, tb)
            o_ref[pl.ds(r0, tb), :] = _pair_select(m0, c[0][0], c[1][0])
            t1_ref[pl.ds(r0, tb), :] = jnp.where(lane == 0, c[0][1], jnp.where(lane == 1, c[1][1], 0.0))
            return carry

        lax.fori_loop(0, nb, qloop, 0)
        if comm is not None:
            @pl.when(step == 2 * nbat - 1)
            def _():
                comm["finish"](c_in, c_out, c_sems)

    def col(blk):
        return pl.BlockSpec((S, PAIR), lambda b, p: (b, blk + p))

    anyspec = pl.BlockSpec(memory_space=pl.ANY)
    out = pl.pallas_call(
        body, name=f"sb_fwd_{l}",
        grid=(nbat, 2),
        in_specs=[col(SBQ_BLK), col(SBK_BLK), col(SBV_BLK)] + c_specs,
        out_specs=[col(0), col(0)] + [anyspec] * n_co,
        out_shape=[jax.ShapeDtypeStruct((T, AW), F32), jax.ShapeDtypeStruct((T, AW), F32)] + c_outs,
        scratch_shapes=[pltpu.VMEM((nb, PAIR, tb), BF16), pltpu.VMEM((S, PAIR), BF16)] + c_scr,
        compiler_params=_cp(("arbitrary", "arbitrary"), VMEM_BIG),
    )(proj, proj, proj, *c_args)
    return out[0], out[1], list(out[2:])


def _sbq_bwd(proj, do, t1, l, S, comm=None):
    T = proj.shape[0]
    tb = TQ_(S)
    nb = S // tb
    nbat = T // S
    c_args, c_specs, c_outs, c_scr = _hosted(comm)
    n_ci, n_co = len(c_args), len(c_outs)

    def body(*refs):
        q_ref, k_ref, v_ref, do_ref, t1_ref = refs[:5]
        c_in = refs[5:5 + n_ci]
        dq_ref, dk_ref, dv_ref = refs[5 + n_ci:8 + n_ci]
        c_out = refs[8 + n_ci:8 + n_ci + n_co]
        kb_sc, kt_sc, vt_sc, dkt_sc, dvt_sc = refs[8 + n_ci + n_co:13 + n_ci + n_co]
        c_sems = refs[13 + n_ci + n_co:]
        step = pl.program_id(0) * 2 + pl.program_id(1)
        if comm is not None:
            @pl.when(step == 0)
            def _():
                comm["start"](c_in, c_out, c_sems)

        kb_sc[...] = k_ref[...].astype(BF16)
        _transpose_blocks(k_ref, kt_sc, nb, tb)
        _transpose_blocks(v_ref, vt_sc, nb, tb)
        dkt_sc[...] = jnp.zeros_like(dkt_sc)
        dvt_sc[...] = jnp.zeros_like(dvt_sc)
        _, m0 = _lane_masks()
        mt0 = lax.broadcasted_iota(jnp.int32, (PAIR, 1), 0) < HD
        tri_in = _tri(tb, "row_le_col")
        tri_ex = _tri(tb, "row_lt_col")
        past = lax.broadcasted_iota(jnp.int32, (tb, tb), 1) < lax.broadcasted_iota(jnp.int32, (tb, tb), 0)

        def qloop(qb, carry):
            qf = _rows(q_ref, qb, tb) * SCALE
            dof = _rows(do_ref, qb, tb)
            qh = _pair_split(qf, m0)
            doh = _pair_split(dof, m0)
            qth = _pair_split(qf.T, mt0)
            doth = _pair_split(dof.T, mt0)
            t1v = _rows(t1_ref, qb, tb)
            tot = (t1v[:, 0:1], t1v[:, 1:2])

            def accumulate(kb, wz, dqs):
                out = []
                for h in range(2):
                    wb, dz = wz[h]
                    dvt_sc[kb] += _dot(doth[h], wb)
                    dkt_sc[kb] += _dot(qth[h], dz)
                    out.append(dqs[h] + _dot(dz, _rows(kb_sc, kb, tb)))
                return tuple(out)

            def block(kb, c, masked):
                hs = range(2)
                runs, dqs, (kb_prev, wz_prev) = c
                z = [_dot(qh[h], kt_sc[kb]) for h in hs]
                dw = [_dot(doh[h], vt_sc[kb]) for h in hs]
                st = []
                for h in hs:
                    lb, l1 = _logsig2(z[h])
                    if masked:
                        l1 = jnp.where(past, l1, 0.0)
                    st.append((lb, _cumsum_mm(l1, tri_in, parts=2)))
                dqs = accumulate(kb_prev, wz_prev, dqs)
                mid = []
                for h in hs:
                    lb, p1 = st[h]
                    w = jnp.exp(lb + (tot[h] - (runs[h][0] + p1)))
                    if masked:
                        w = jnp.where(past, w, 0.0)
                    gm = w * dw[h]
                    mid.append((w.astype(BF16), gm, _cumsum_mm(gm, tri_ex, parts=1)))
                new_runs, wz = [], []
                for h in hs:
                    run1, rung = runs[h]
                    wb, gm, cx = mid[h]
                    dz = gm - (gm + (rung + cx)) * jnp.exp(st[h][0])
                    if masked:
                        dz = jnp.where(past, dz, 0.0)
                    wz.append((wb, dz.astype(BF16)))
                    p1 = st[h][1]
                    new_runs.append((run1 + p1[:, tb - 1:tb], rung + (cx[:, tb - 1:tb] + gm[:, tb - 1:tb])))
                return tuple(new_runs), dqs, (kb, tuple(wz))

            z1 = jnp.zeros((tb, 1), F32)
            zq = jnp.zeros((tb, PAIR), F32)
            zb = jnp.zeros((tb, tb), BF16)
            none = (jnp.int32(0), ((zb, zb), (zb, zb)))
            c = _loop_grouped(qb, lambda i, cc: block(i, cc, False), (((z1, z1), (z1, z1)), (zq, zq), none))
            _, dqs, (kb_last, wz_last) = block(qb, c, True)
            dqs = accumulate(kb_last, wz_last, dqs)
            r0 = pl.multiple_of(qb * tb, tb)
            dq_ref[pl.ds(r0, tb), :] = (_pair_select(m0, dqs[0], dqs[1]) * SCALE).astype(BF16)
            return carry

        lax.fori_loop(0, nb, qloop, 0)
        for kb in range(nb):
            dk_ref[kb * tb:(kb + 1) * tb, :] = dkt_sc[kb].T.astype(BF16)
            dv_ref[kb * tb:(kb + 1) * tb, :] = dvt_sc[kb].T.astype(BF16)
        if comm is not None:
            @pl.when(step == 2 * nbat - 1)
            def _():
                comm["finish"](c_in, c_out, c_sems)

    def col(blk):
        return pl.BlockSpec((S, PAIR), lambda b, p: (b, blk + p))

    sh = jax.ShapeDtypeStruct((T, AW), BF16)
    anyspec = pl.BlockSpec(memory_space=pl.ANY)
    out = pl.pallas_call(
        body, name=f"sb_bwd_{l}",
        grid=(nbat, 2),
        in_specs=[col(SBQ_BLK), col(SBK_BLK), col(SBV_BLK), col(0), col(0)] + c_specs,
        out_specs=[col(0), col(0), col(0)] + [anyspec] * n_co,
        out_shape=[sh, sh, sh] + c_outs,
        scratch_shapes=[pltpu.VMEM((S, PAIR), BF16), pltpu.VMEM((nb, PAIR, tb), BF16), pltpu.VMEM((nb, PAIR, tb), BF16),
                        pltpu.VMEM((nb, PAIR, tb), F32), pltpu.VMEM((nb, PAIR, tb), F32)] + c_scr,
        compiler_params=_cp(("arbitrary", "arbitrary"), VMEM_BIG),
    )(proj, proj, proj, do, t1, *c_args)
    return out[0], out[1], out[2], list(out[3:])


def _foxq_fwd(proj, cum, ck, gqk2, l, S):
    T = proj.shape[0]
    tb = TQ_(S)
    nb = S // tb

    def body(q_ref, k_ref, v_ref, cum_ref, ck_ref, g_ref, o_ref, nl_ref, fk_sc, fkt_sc, vb_sc):
        lane, m0 = _lane_masks()
        p = pl.program_id(1)
        kn, _ = _pair_rms(k_ref[...], m0)
        fk_sc[...] = kn * g_ref[1:2, :]
        _transpose_blocks(fk_sc, fkt_sc, nb, tb)
        vb_sc[...] = v_ref[...].astype(BF16)
        causal = lax.broadcasted_iota(jnp.int32, (tb, tb), 1) <= lax.broadcasted_iota(jnp.int32, (tb, tb), 0)

        def qloop(qb, carry):
            qn, _ = _pair_rms(_rows(q_ref, qb, tb), m0)
            fqh = _pair_split(qn * (g_ref[0:1, :] * SCALE), m0)
            cumv = _rows(cum_ref, qb, tb)
            cq = [_rowsum(jnp.where(lane == 2 * p + h, cumv, 0.0)) for h in range(2)]

            def scores(kb):
                return tuple(_dot(fqh[h], fkt_sc[kb]) for h in range(2))

            def block(kb, kb_next, qk, c, masked):
                st = []
                for h in range(2):
                    s = qk[h] + (cq[h] - ck_ref[h, kb])
                    if masked:
                        s = jnp.where(causal, s, NEG)
                    m2 = jnp.maximum(c[h][0], jnp.max(s, axis=1, keepdims=True))
                    pr = jnp.exp(s - m2)
                    hi = pr.astype(BF16)
                    lo = (pr - hi.astype(F32)).astype(BF16)
                    vv = _rows(vb_sc, kb, tb)
                    st.append((m2, pr, _dot(hi, vv) + _dot(lo, vv)))
                qk_next = scores(kb_next)
                out = []
                for h in range(2):
                    m, lsum, acc = c[h]
                    m2, pr, pv = st[h]
                    al = jnp.exp(m - m2)
                    out.append((m2, al * lsum + _rowsum(pr), al * acc + pv))
                return qk_next, tuple(out)

            zero = (jnp.full((tb, 1), NEG, F32), jnp.zeros((tb, 1), F32), jnp.zeros((tb, PAIR), F32))

            def off_diag(i, sc):
                return block(i, i + 1, sc[0], sc[1], False)

            qk, c = lax.fori_loop(0, qb, off_diag, (scores(0), (zero, zero)))
            _, c = block(qb, qb, qk, c, True)
            r0 = pl.multiple_of(qb * tb, tb)
            o_ref[pl.ds(r0, tb), :] = _pair_select(m0, c[0][2] / c[0][1], c[1][2] / c[1][1])
            nl = [cq[h] - (c[h][0] + jnp.log(c[h][1])) for h in range(2)]
            nl_ref[pl.ds(r0, tb), :] = jnp.where(lane == 0, nl[0], jnp.where(lane == 1, nl[1], 0.0))
            return carry

        lax.fori_loop(0, nb, qloop, 0)

    def col(blk):
        return pl.BlockSpec((S, PAIR), lambda b, p: (b, blk + p))

    return pl.pallas_call(
        body, name=f"fox_fwd_{l}",
        grid=(T // S, 2),
        in_specs=[col(FXQ_BLK), col(FXK_BLK), col(FXV_BLK),
                  pl.BlockSpec((S, 128), lambda b, p: (b, 0)),
                  pl.BlockSpec((None, 2, nb, 1, tb), lambda b, p: (b, p, 0, 0, 0)),
                  pl.BlockSpec((None, 8, PAIR), lambda b, p: (l, 0, 0))],
        out_specs=[col(0), col(0)],
        out_shape=[jax.ShapeDtypeStruct((T, AW), F32)] * 2,
        scratch_shapes=[pltpu.VMEM((S, PAIR), F32), pltpu.VMEM((nb, PAIR, tb), BF16), pltpu.VMEM((S, PAIR), BF16)],
        compiler_params=_cp(("arbitrary", "arbitrary"), VMEM_BIG),
    )(proj, proj, proj, cum, ck, gqk2)


def _foxq_bwd(proj, do, nl, ox, ck, gqk2, l, S):
    T = proj.shape[0]
    tb = TQ_(S)
    nb = S // tb

    def body(q_ref, k_ref, v_ref, do_ref, nl_ref, ox_ref, ck_ref, g_ref,
             dq_ref, dk_ref, dv_ref, dc_ref, wacc_ref, fk_sc, fkt_sc, vt_sc, dfkt_sc, dvt_sc):
        @pl.when((pl.program_id(0) == 0) & (pl.program_id(1) == 0))
        def _():
            wacc_ref[...] = jnp.zeros_like(wacc_ref)

        _, m0 = _lane_masks()
        mt0 = lax.broadcasted_iota(jnp.int32, (PAIR, 1), 0) < HD
        g0 = g_ref[0:1, :]
        g1 = g_ref[1:2, :]
        fk_sc[...] = (_pair_rms(k_ref[...], m0)[0] * g1).astype(BF16)
        _transpose_blocks(fk_sc, fkt_sc, nb, tb)
        _transpose_blocks(v_ref, vt_sc, nb, tb)
        dfkt_sc[...] = jnp.zeros_like(dfkt_sc)
        dvt_sc[...] = jnp.zeros_like(dvt_sc)
        dc_ref[...] = jnp.zeros_like(dc_ref)
        causal = lax.broadcasted_iota(jnp.int32, (tb, tb), 1) <= lax.broadcasted_iota(jnp.int32, (tb, tb), 0)

        def qloop(qb, carry):
            qn, qr = _pair_rms(_rows(q_ref, qb, tb), m0)
            fqf = qn * (g0 * SCALE)
            dof = _rows(do_ref, qb, tb)
            fqh = _pair_split(fqf, m0)
            doh = _pair_split(dof, m0)
            fqth = _pair_split(fqf.T, mt0)
            doth = _pair_split(dof.T, mt0)
            nlv = _rows(nl_ref, qb, tb)
            cql = (nlv[:, 0:1], nlv[:, 1:2])

            def probs(kb, masked):
                qk = [_dot(fqh[h], fkt_sc[kb]) for h in range(2)]
                dp = [_dot(doh[h], vt_sc[kb]) for h in range(2)]
                pr = []
                for h in range(2):
                    e = jnp.exp(qk[h] + (cql[h] - ck_ref[h, kb]))
                    pr.append(jnp.where(causal, e, 0.0) if masked else e)
                return pr, dp

            oxv = _rows(ox_ref, qb, tb)
            dlt = [_rowsum(doh[h].astype(F32) * oxv) for h in range(2)]

            def accumulate(kb, pd, dfqs):
                out = []
                for h in range(2):
                    prb, dsb = pd[h]
                    dvt_sc[kb] += _dot(doth[h], prb)
                    dfkt_sc[kb] += _dot(fqth[h], dsb)
                    out.append(dfqs[h] + _dot(dsb, _rows(fk_sc, kb, tb)))
                return tuple(out)

            def block(kb, c, masked):
                dfqs, (kb_prev, pd_prev) = c
                pr, dp = probs(kb, masked)
                dfqs = accumulate(kb_prev, pd_prev, dfqs)
                pd = []
                for h in range(2):
                    ds = pr[h] * (dp[h] - dlt[h])
                    dc_ref[h, kb] += jnp.broadcast_to(-_colsum(ds), (8, tb))
                    pd.append((pr[h].astype(BF16), ds.astype(BF16)))
                return dfqs, (kb, tuple(pd))

            zq = jnp.zeros((tb, PAIR), F32)
            zb = jnp.zeros((tb, tb), BF16)
            none = (jnp.int32(0), ((zb, zb), (zb, zb)))
            c = _loop_grouped(qb, lambda i, cc: block(i, cc, False), ((zq, zq), none))
            dfqs, (kb_last, pd_last) = block(qb, c, True)
            c = accumulate(kb_last, pd_last, dfqs)
            dfq = _pair_select(m0, c[0], c[1]) * SCALE
            wacc_ref[0:1, :] += _colsum(dfq * qn)
            r0 = pl.multiple_of(qb * tb, tb)
            dq_ref[pl.ds(r0, tb), :] = _pair_rms_bwd(qn, qr, dfq * g0, m0).astype(BF16)
            return carry

        lax.fori_loop(0, nb, qloop, 0)
        for kb in range(nb):
            rows = slice(kb * tb, (kb + 1) * tb)
            dfk = dfkt_sc[kb].T
            knb, krb = _pair_rms(k_ref[rows, :], m0)
            wacc_ref[1:2, :] += _colsum(dfk * knb)
            dk_ref[rows, :] = _pair_rms_bwd(knb, krb, dfk * g1, m0).astype(BF16)
            dv_ref[rows, :] = dvt_sc[kb].T.astype(BF16)

    def col(blk):
        return pl.BlockSpec((S, PAIR), lambda b, p: (b, blk + p))

    sh = jax.ShapeDtypeStruct((T, AW), BF16)
    return pl.pallas_call(
        body, name=f"fox_bwd_{l}",
        grid=(T // S, 2),
        in_specs=[col(FXQ_BLK), col(FXK_BLK), col(FXV_BLK), col(0), col(0), col(0),
                  pl.BlockSpec((None, 2, nb, 1, tb), lambda b, p: (b, p, 0, 0, 0)),
                  pl.BlockSpec((None, 8, PAIR), lambda b, p: (l, 0, 0))],
        out_specs=[col(0), col(0), col(0),
                   pl.BlockSpec((None, 2, nb, 8, tb), lambda b, p: (b, p, 0, 0, 0)),
                   pl.BlockSpec((8, PAIR), lambda b, p: (0, 0))],
        out_shape=[sh, sh, sh,
                   jax.ShapeDtypeStruct((T // S, NH, nb, 8, tb), F32),
                   jax.ShapeDtypeStruct((8, PAIR), F32)],
        scratch_shapes=[pltpu.VMEM((S, PAIR), BF16), pltpu.VMEM((nb, PAIR, tb), BF16), pltpu.VMEM((nb, PAIR, tb), BF16),
                        pltpu.VMEM((nb, PAIR, tb), F32), pltpu.VMEM((nb, PAIR, tb), F32)],
        compiler_params=_cp(("arbitrary", "arbitrary"), VMEM_BIG),
    )(proj, proj, proj, do, nl, ox, ck, gqk2)


def _ada_fwd(c_all, w_ada, b_cols):
    nb, ncol = c_all.shape[0], w_ada.shape[2]
    tn = _tile(ncol, 768)

    def body(c_ref, w_ref, b_ref, o_ref):
        c = c_ref[...]
        ca = (c * _sigmoid(c)).astype(BF16)
        o_ref[...] = _dot(ca, w_ref[...].astype(BF16)) + b_ref[...]

    return pl.pallas_call(
        body, name="ada_fwd",
        grid=(2, ncol // tn),
        in_specs=[pl.BlockSpec((nb, D), lambda l, n: (0, 0)),
                  pl.BlockSpec((None, D, tn), lambda l, n: (l, 0, n)),
                  pl.BlockSpec((None, 1, tn), lambda l, n: (l, 0, n))],
        out_specs=pl.BlockSpec((None, nb, tn), lambda l, n: (l, 0, n)),
        out_shape=jax.ShapeDtypeStruct((2, nb, ncol), F32),
        compiler_params=_cp(("arbitrary", "arbitrary")),
    )(c_all, w_ada, b_cols)


def _ada_bwd(c_all, dmod_cols):
    nb, ncol = c_all.shape[0], dmod_cols.shape[2]
    tn = _tile(ncol, 768)

    def body(c_ref, d_ref, o_ref):
        c = c_ref[...]
        ca = (c * _sigmoid(c)).astype(BF16)
        o_ref[...] = _dot_tn(ca, d_ref[...].astype(BF16))

    return pl.pallas_call(
        body, name="ada_bwd",
        grid=(2, ncol // tn),
        in_specs=[pl.BlockSpec((nb, D), lambda l, n: (0, 0)),
                  pl.BlockSpec((None, nb, tn), lambda l, n: (l, 0, n))],
        out_specs=pl.BlockSpec((None, D, tn), lambda l, n: (l, 0, n)),
        out_shape=jax.ShapeDtypeStruct((2, D, ncol), F32),
        compiler_params=_cp(("arbitrary", "arbitrary")),
    )(c_all, dmod_cols)


def _sum_lead(a, name):
    n, R, C = a.shape
    tr = _tile_div8(R, 256)

    def body(a_ref, o_ref):
        acc = a_ref[0]
        for i in range(1, n):
            acc = acc + a_ref[i]
        o_ref[...] = acc

    return pl.pallas_call(
        body, name=name,
        grid=(R // tr,),
        in_specs=[pl.BlockSpec((n, tr, C), lambda i: (0, i, 0))],
        out_specs=pl.BlockSpec((tr, C), lambda i: (i, 0)),
        out_shape=jax.ShapeDtypeStruct((R, C), F32),
        compiler_params=_cp(("arbitrary",)),
    )(a)


def _adamw(w, g, m, v, name):
    R, C = w.shape
    tr = _tile_div8(R, max(8, (1 << 18) // C))
    c1 = 1.0 / (1.0 - ADAM_B1 ** ADAM_STEP)
    c2 = 1.0 / (1.0 - ADAM_B2 ** ADAM_STEP)

    def body(w_ref, g_ref, m_ref, v_ref, d_ref, mo_ref, vo_ref):
        gg = g_ref[...]
        mn = ADAM_B1 * m_ref[...] + (1.0 - ADAM_B1) * gg
        vn = ADAM_B2 * v_ref[...] + (1.0 - ADAM_B2) * (gg * gg)
        mo_ref[...] = mn
        vo_ref[...] = vn
        d_ref[...] = (-ADAM_LR) * ((mn * c1) / (jnp.sqrt(vn * c2) + ADAM_EPS) + ADAM_WD * w_ref[...])

    spec = pl.BlockSpec((tr, C), lambda i: (i, 0))
    sh = jax.ShapeDtypeStruct((R, C), F32)
    return pl.pallas_call(
        body, name=name, grid=(R // tr,),
        in_specs=[spec] * 4, out_specs=[spec] * 3, out_shape=[sh] * 3,
        compiler_params=_cp(("arbitrary",)),
    )(w, g, m, v)


def _coords():
    return lax.axis_index("x"), lax.axis_index("y"), lax.axis_index("c")


def _all_gather8(blk, name, vmem):
    m_per, n = blk.shape
    space = pltpu.VMEM if vmem else pl.ANY

    def body(x_ref, out_ref, send_sems, recv_sems, local_sem):
        x, y, c = _coords()
        me, sibling = (x, y, c), (x, y, 1 - c)
        chips = [(1 - x, y), (x, 1 - y), (1 - x, 1 - y)]

        def rows(px, py, pc):
            return out_ref.at[4 * px + 2 * py + pc]

        def copy(k, block, to, src=None):
            return pltpu.make_async_remote_copy(
                src_ref=rows(*block) if src is None else src, dst_ref=rows(*block),
                send_sem=send_sems.at[k], recv_sem=recv_sems.at[k], device_id=to, device_id_type=MESH)

        mine = pltpu.make_async_copy(x_ref, rows(*me), local_sem)
        mine.start()
        first = [copy(0, me, sibling, src=x_ref)]
        first += [copy(1 + j, me, (*chip, c), src=x_ref) for j, chip in enumerate(chips)]
        for cp in first:
            cp.start()
        passed = [copy(4 + j, (*chip, c), sibling) for j, chip in enumerate(chips)]
        for j, chip in enumerate(chips):
            copy(1 + j, (*chip, c), me).wait_recv()
            passed[j].start()
        copy(0, sibling, me).wait_recv()
        for j, chip in enumerate(chips):
            copy(4 + j, (*chip, 1 - c), me).wait_recv()
        for cp in first + passed:
            cp.wait_send()
        mine.wait()

    return pl.pallas_call(
        body, name=name,
        out_shape=jax.ShapeDtypeStruct((N_DEV, m_per, n), blk.dtype),
        in_specs=[pl.BlockSpec(memory_space=space)],
        out_specs=pl.BlockSpec(memory_space=space),
        scratch_shapes=[pltpu.SemaphoreType.DMA((7,)), pltpu.SemaphoreType.DMA((7,)), pltpu.SemaphoreType.DMA],
        compiler_params=pltpu.CompilerParams(vmem_limit_bytes=VMEM_BIG if vmem else None),
    )(blk)


def _run_comm(comm, name):
    n_in, n_out = len(comm["args"]), len(comm["out_shapes"])

    def body(*refs):
        parts = (refs[:n_in], refs[n_in:n_in + n_out], refs[n_in + n_out:])
        comm["start"](*parts)
        comm["finish"](*parts)

    anyspec = pl.BlockSpec(memory_space=pl.ANY)
    return pl.pallas_call(
        body, name=name, out_shape=comm["out_shapes"],
        in_specs=[anyspec] * n_in, out_specs=[anyspec] * n_out, scratch_shapes=comm["scratch"],
    )(*comm["args"])


def _hosted(comm):
    if comm is None:
        return [], [], [], []
    anyspec = pl.BlockSpec(memory_space=pl.ANY)
    return list(comm["args"]), [anyspec] * len(comm["args"]), list(comm["out_shapes"]), list(comm["scratch"])


def _ag_comm(wshards, pieces):
    n_piece = len(pieces)
    halves = [wshards[i].shape[len(lead)] // 2 for i, lead, _ in pieces]

    def plan(ins, outs, sems):
        send_sems, recv_sems, local_sems = sems
        x, y, c = _coords()
        me, sibling = (x, y, c), (x, y, 1 - c)
        chips = [(1 - x, y), (x, 1 - y), (1 - x, 1 - y)]

        def dsts(px, py, pc):
            s = 2 * px + py
            return [outs[p].at[s, pl.ds(pc * r2, r2)] if stacked else outs[p].at[pl.ds((2 * s + pc) * r2, r2)]
                    for p, ((_, _, stacked), r2) in enumerate(zip(pieces, halves))]

        srcs = [ins[i].at[(*lead, pl.ds(c * r2, r2))] for (i, lead, _), r2 in zip(pieces, halves)]

        def copies(k, block, to, own=False):
            d = dsts(*block)
            return [pltpu.make_async_remote_copy(
                src_ref=srcs[p] if own else d[p], dst_ref=d[p], send_sem=send_sems.at[k, p],
                recv_sem=recv_sems.at[k, p], device_id=to, device_id_type=MESH) for p in range(n_piece)]

        mine = [pltpu.make_async_copy(srcs[p], d, local_sems.at[p]) for p, d in enumerate(dsts(*me))]
        first = copies(0, me, sibling, own=True)
        for j, chip in enumerate(chips):
            first += copies(1 + j, me, (*chip, c), own=True)
        return me, sibling, chips, c, copies, mine, first

    def start(ins, outs, sems):
        *_, mine, first = plan(ins, outs, sems)
        for cp in mine + first:
            cp.start()

    def finish(ins, outs, sems):
        me, sibling, chips, c, copies, mine, first = plan(ins, outs, sems)
        passed = []
        for j, chip in enumerate(chips):
            for cp in copies(1 + j, (*chip, c), me):
                cp.wait_recv()
            fwd = copies(4 + j, (*chip, c), sibling)
            for cp in fwd:
                cp.start()
            passed += fwd
        for cp in copies(0, sibling, me):
            cp.wait_recv()
        for j, chip in enumerate(chips):
            for cp in copies(4 + j, (*chip, 1 - c), me):
                cp.wait_recv()
        for cp in first + passed:
            cp.wait_send()
        for cp in mine:
            cp.wait()

    out_shapes = []
    for (i, lead, stacked), r2 in zip(pieces, halves):
        cols = wshards[i].shape[-1]
        out_shapes.append(jax.ShapeDtypeStruct((N_SHARD, 2 * r2, cols) if stacked else (N_SHARD * 2 * r2, cols), BF16))
    return dict(
        args=list(wshards), out_shapes=out_shapes,
        scratch=[pltpu.SemaphoreType.DMA((7, n_piece)), pltpu.SemaphoreType.DMA((7, n_piece)),
                 pltpu.SemaphoreType.DMA((n_piece,))],
        start=start, finish=finish)


def _rs_to_chips_comm(hs):
    n = len(hs)

    def copies(h, r, sems):
        send_sems, recv_sems = sems
        x, y, c = _coords()
        chips = [(1 - x, y), (x, 1 - y), (1 - x, 1 - y)]
        return [pltpu.make_async_remote_copy(
            src_ref=h[p].at[2 * px + py], dst_ref=r[p].at[k], send_sem=send_sems.at[k, p], recv_sem=recv_sems.at[k, p],
            device_id=(px, py, c), device_id_type=MESH) for k, (px, py) in enumerate(chips) for p in range(n)]

    def start(h, r, sems):
        for cp in copies(h, r, sems):
            cp.start()

    def finish(h, r, sems):
        for cp in copies(h, r, sems):
            cp.wait()

    return dict(args=list(hs), out_shapes=[jax.ShapeDtypeStruct((3,) + h.shape[1:], h.dtype) for h in hs],
                scratch=[pltpu.SemaphoreType.DMA((3, n)), pltpu.SemaphoreType.DMA((3, n))],
                start=start, finish=finish)


def _rs_to_sibling(pieces, name):
    n = len(pieces)

    def body(*refs):
        g, r, (send_sems, recv_sems) = refs[:n], refs[n:2 * n], refs[2 * n:]
        x, y, c = _coords()
        cps = []
        for p in range(n):
            r2 = g[p].shape[1] // 2
            cps.append(pltpu.make_async_remote_copy(
                src_ref=g[p].at[:, pl.ds((1 - c) * r2, r2)], dst_ref=r[p], send_sem=send_sems.at[p],
                recv_sem=recv_sems.at[p], device_id=(x, y, 1 - c), device_id_type=MESH))
        for cp in cps:
            cp.start()
        for cp in cps:
            cp.wait()

    anyspec = pl.BlockSpec(memory_space=pl.ANY)
    return pl.pallas_call(
        body, name=name,
        out_shape=[jax.ShapeDtypeStruct((N_SHARD, g.shape[1] // 2, g.shape[2]), g.dtype) for g in pieces],
        in_specs=[anyspec] * n, out_specs=[anyspec] * n,
        scratch_shapes=[pltpu.SemaphoreType.DMA((n,)), pltpu.SemaphoreType.DMA((n,))],
    )(*pieces)


def _share_halves(tensors, places, r2s):
    n, no = len(places), len(tensors)

    def body(*refs):
        o, (send_sems, recv_sems) = refs[no:2 * no], refs[2 * no:]
        x, y, c = _coords()

        def half(p, hc):
            oi, lead = places[p]
            return o[oi].at[(*lead, pl.ds(hc * r2s[p], r2s[p]))]

        outs = [pltpu.make_async_remote_copy(
            src_ref=half(p, c), dst_ref=half(p, c), send_sem=send_sems.at[p], recv_sem=recv_sems.at[p],
            device_id=(x, y, 1 - c), device_id_type=MESH) for p in range(n)]
        for cp in outs:
            cp.start()
        for p in range(n):
            pltpu.make_async_remote_copy(
                src_ref=half(p, 1 - c), dst_ref=half(p, 1 - c), send_sem=send_sems.at[p], recv_sem=recv_sems.at[p],
                device_id=(x, y, 1 - c), device_id_type=MESH).wait_recv()
        for cp in outs:
            cp.wait_send()

    anyspec = pl.BlockSpec(memory_space=pl.ANY)
    return pl.pallas_call(
        body, name="share_halves",
        out_shape=[jax.ShapeDtypeStruct(t.shape, t.dtype) for t in tensors],
        in_specs=[anyspec] * no, out_specs=[anyspec] * no,
        input_output_aliases={i: i for i in range(no)},
        scratch_shapes=[pltpu.SemaphoreType.DMA((n,)), pltpu.SemaphoreType.DMA((n,))],
    )(*tensors)


def _add_rows(r2, cols, n_arrays):
    lanes = -(-cols // 128) * 128
    return _tile_div8(r2, max(16, (24 << 20) // (2 * n_arrays * lanes * 4)), mult=16)


def _add_sibling(pieces, recvs, cidx, name):
    n = len(pieces)
    _, R, C = pieces[0].shape
    r2 = R // 2
    tr = _add_rows(r2, C, 2 * n)
    nt = r2 // tr

    def body(c_ref, *refs):
        for p in range(n):
            refs[2 * n + p][...] = (refs[p][...] + refs[n + p][...].astype(F32)).astype(BF16)

    return pl.pallas_call(
        body, name=name,
        grid_spec=pltpu.PrefetchScalarGridSpec(
            num_scalar_prefetch=1, grid=(N_SHARD, nt),
            in_specs=[pl.BlockSpec((None, tr, C), lambda s, i, c_ref: (s, c_ref[0] * nt + i, 0))] * n
            + [pl.BlockSpec((None, tr, C), lambda s, i, c_ref: (s, i, 0))] * n,
            out_specs=[pl.BlockSpec((None, tr, C), lambda s, i, c_ref: (s, i, 0))] * n),
        out_shape=[jax.ShapeDtypeStruct((N_SHARD, r2, C), BF16)] * n,
        compiler_params=_cp(("arbitrary", "arbitrary"), VMEM_BIG),
    )(cidx, *pieces, *recvs)


def _add_chips_into(piece, recv_a, recv_b, sc, prev, shape, lead, name):
    _, R, C = piece.shape
    r2 = R // 2
    tr = _add_rows(r2, C, 4)
    nt = r2 // tr
    nl = len(lead)

    def body(sc_ref, p_ref, a_ref, b_ref, *rest):
        o_ref = rest[-1]
        acc = p_ref[...] + a_ref[...].astype(F32)
        for k in range(3):
            acc = acc + b_ref[k].astype(F32)
        o_ref[...] = acc

    in_specs = [pl.BlockSpec((None, tr, C), lambda i, sc_ref: (sc_ref[0], sc_ref[1] * nt + i, 0)),
                pl.BlockSpec((None, tr, C), lambda i, sc_ref: (sc_ref[0], i, 0)),
                pl.BlockSpec((3, tr, C), lambda i, sc_ref: (0, i, 0))]
    args = [sc, piece, recv_a, recv_b]
    aliases = {}
    if prev is not None:
        in_specs.append(pl.BlockSpec(memory_space=pl.ANY))
        args.append(prev)
        aliases = {4: 0}
    return pl.pallas_call(
        body, name=name,
        grid_spec=pltpu.PrefetchScalarGridSpec(
            num_scalar_prefetch=1, grid=(nt,), in_specs=in_specs,
            out_specs=pl.BlockSpec((None,) * nl + (tr, C), lambda i, sc_ref: (*lead, sc_ref[1] * nt + i, 0))),
        out_shape=jax.ShapeDtypeStruct(shape, F32),
        input_output_aliases=aliases,
        compiler_params=_cp(("arbitrary",), VMEM_BIG),
    )(*args)


def _pack_rows(parts, rows, dtype):
    flat = jnp.concatenate([p.reshape(-1).astype(dtype) for p in parts])
    return jnp.pad(flat, (0, rows * ROW - flat.shape[0])).reshape(rows, ROW)


def _unpack(flat, shapes):
    out, off = [], 0
    for sh in shapes:
        n = math.prod(sh)
        out.append(flat[off:off + n].reshape(sh))
        off += n
    return out


def _block_diag(w):
    eye = jnp.eye(LW // HD, dtype=w.dtype)
    return jnp.einsum("lhij,hg->lhigj", w, eye).reshape(w.shape[0], LW, LW)


def _diag_blocks(w):
    nbk = LW // HD
    w4 = w.reshape(nbk, HD, nbk, HD)
    return jnp.stack([w4[h, :, h, :] for h in range(nbk)])


def _rows8(rows, width):
    z = jnp.zeros((width,), F32)
    return jnp.stack(list(rows) + [z] * (8 - len(rows)))


def kernel(x, c, w_ada, b_ada, g_norm, w_ffn_up, w_ffn_down, w_in, b_fgate, conv_w, conv_b, w_rgate, b_rgate, w_igate, b_igate, lru_lambda, g_qk, g_mix_out, w_out, loss_target, m_w_ada, m_b_ada, m_g_norm, m_w_ffn_up, m_w_ffn_down, m_w_in, m_b_fgate, m_conv_w, m_conv_b, m_w_rgate, m_b_rgate, m_w_igate, m_b_igate, m_lru_lambda, m_g_qk, m_g_mix_out, m_w_out, v_w_ada, v_b_ada, v_g_norm, v_w_ffn_up, v_w_ffn_down, v_w_in, v_b_fgate, v_conv_w, v_conv_b, v_w_rgate, v_b_rgate, v_w_igate, v_b_igate, v_lru_lambda, v_g_qk, v_g_mix_out, v_w_out):
    B, S, _ = x.shape
    T = B * S
    xi, yi, ci = _coords()
    sidx = 2 * xi + yi
    didx = 4 * xi + 2 * yi + ci
    ada_cols = w_ada.shape[2]
    gn_cols = g_norm.shape[2]
    cw_cols = conv_w.shape[2]
    n_all = B * N_DEV

    blk1 = _pack_rows([c, jnp.pad(g_norm.reshape(-1), (0, 2 * ROW - g_norm.size)), conv_w], 8, F32)
    ag1 = _all_gather8(blk1, "ag_small_in", True)
    c_all = ag1[:, 0:B].reshape(n_all, D)
    chip_rows = ag1[0::2]
    g_norm_full = chip_rows[:, 2:4].reshape(N_SHARD, 2 * ROW)[:, :g_norm.size] \
        .reshape(N_SHARD, 2, 3, gn_cols).transpose(1, 2, 0, 3).reshape(2, 3, D)
    conv_w_full = chip_rows[:, 4].reshape(N_SHARD, 2, 4, cw_cols).transpose(1, 2, 0, 3).reshape(2, 4, LW)

    b_cols = lax.dynamic_slice(b_ada, (0, sidx * ada_cols), (2, ada_cols)).reshape(2, 1, ada_cols)
    mod_cols = _ada_fwd(c_all, w_ada, b_cols)
    mrows = (2 * n_all * ada_cols) // ROW
    ag2 = _all_gather8(mod_cols.reshape(mrows, ROW), "ag_mod", True)
    mod_sh = ag2[0::2].reshape(N_SHARD, 2, n_all, ada_cols)
    mod_me = lax.dynamic_slice(mod_sh, (0, 0, didx * B, 0), (N_SHARD, 2, B, ada_cols))
    mod_me = mod_me.transpose(1, 2, 0, 3).reshape(2, B, 3, 3, D)
    zrow = jnp.zeros((B, D), F32)
    mods = [[jnp.stack([mod_me[l, :, j, 0], 1.0 + mod_me[l, :, j, 1], 1.0 + mod_me[l, :, j, 2],
                        jnp.broadcast_to(g_norm_full[l, j], (B, D)), zrow, zrow, zrow, zrow], axis=1)
             for j in range(3)] for l in range(2)]

    wshards = (w_ffn_up.astype(BF16), w_ffn_down.astype(BF16), w_in.astype(BF16), w_out.astype(BF16))

    def ffn_pieces(l, j):
        return [(0, (l, j), True), (1, (l, j), False)]

    def mixer_pieces(l):
        return [(2, (l,), True), (3, (l,), False)]

    def ffn_weights(up, dn):
        return dict(up=up, dn=dn)

    def mixer_weights(g_in, g_out):
        return dict(inp=jnp.pad(g_in.transpose(1, 0, 2).reshape(D, N_IN), ((0, 0), (0, N_INP - N_IN))), out=g_out)

    wl = [dict(), dict()]
    wl[0][0] = ffn_weights(*_run_comm(_ag_comm(wshards, ffn_pieces(0, 0)), "ag_weights_0_0"))

    wr_d = _block_diag(w_rgate).astype(BF16)
    wi_d = _block_diag(w_igate).astype(BF16)
    cw8 = jnp.pad(conv_w_full, ((0, 0), (0, 4), (0, 0)))
    vp8 = jnp.stack([_rows8([conv_b[l], b_rgate[l], b_igate[l], lru_lambda[l]], LW) for l in range(2)])
    bfp = jnp.pad(b_fgate, ((0, 0), (0, 128 - NH)))[:, None, :] * jnp.ones((1, 8, 1), F32)
    gqk2 = jnp.tile(jnp.pad(g_qk, ((0, 0), (0, 6), (0, 0))), (1, 1, 2))
    gmix8 = jnp.pad(g_mix_out[:, None, :], ((0, 0), (0, 7), (0, 0)))

    x2 = x.reshape(T, D)
    tgt = loss_target.reshape(T, D)

    saved = []
    xc = x2
    for l in range(2):
        sv = {}
        sv["x0"] = xc
        w = wl[l]
        rest0 = _ag_comm(wshards, mixer_pieces(0) + ffn_pieces(0, 1)) if l == 0 else None
        xc, sv["g0"], sv["u0"], sv["f0"], got = _ffn_fwd(xc, mods[l][0], w[0]["up"], w[0]["dn"], l, 0, S, rest0)
        if l == 0:
            w["mix"] = mixer_weights(got[0], got[1])
            w[1] = ffn_weights(got[2], got[3])
        sv["x1"] = xc
        sv["h1"], proj = _mix_in_fwd(xc, mods[l][1], w["mix"]["inp"], l, S)
        sv["proj"] = proj
        sv["ylru"], sv["hl"] = _lru_fwd(proj, cw8, vp8, wr_d, wi_d, l, S)
        all1 = _ag_comm(wshards, ffn_pieces(1, 0) + mixer_pieces(1) + ffn_pieces(1, 1)) if l == 0 else None
        sv["osb"], sv["t1"], got = _sbq_fwd(proj, l, S, all1)
        if l == 0:
            wl[1][0] = ffn_weights(got[0], got[1])
            wl[1]["mix"] = mixer_weights(got[2], got[3])
            wl[1][1] = ffn_weights(got[4], got[5])
        cum = _fgate_fwd(proj, bfp, l, S)
        sv["ck"] = cum[:, :NH].reshape(B, S, NH).transpose(0, 2, 1).reshape(B, NH, S // TQ_(S), 1, TQ_(S))
        sv["ofx"], sv["nl"] = _foxq_fwd(proj, cum, sv["ck"], gqk2, l, S)
        xc, sv["y"], sv["mo"] = _mix_out_fwd(xc, sv["ylru"], sv["osb"], sv["ofx"], mods[l][1], gmix8, w["mix"]["out"], l, S)
        sv["x2"] = xc
        xc, sv["g2"], sv["u2"], sv["f2"], _ = _ffn_fwd(xc, mods[l][2], w[1]["up"], w[1]["dn"], l, 1, S)
        saved.append(sv)

    dxc, lpart = _loss_head(xc, tgt, S)
    loss = lax.psum(lpart[0, 0], ("x", "y", "c"))

    tf = wl[0][0]["up"].shape[-1]
    g_up_l = [[None, None], [None, None]]
    g_dn_l = [[None, None], [None, None]]
    g_in_l, g_out_l = [None, None], [None, None]
    dmods = [[None] * 3 for _ in range(2)]
    small = [dict() for _ in range(2)]
    cvec = jnp.reshape(ci, (1,)).astype(jnp.int32)
    scvec = jnp.stack([sidx, ci]).astype(jnp.int32)

    def ffn_groups(l, j):
        return [(0, "up", [g_up_l[l][j]], [(l, j)]), (1, "dn", [g_dn_l[l][j]], [(l, j)])]

    def mixer_groups(l):
        return [(2, "in", [g_in_l[l]], [(l,)]), (3, "out", [g_out_l[l]], [(l,)])]

    def rs_sibling_phase(groups, tag):
        recv_a = _rs_to_sibling([pb for _, _, ps, _ in groups for _, pb in ps], f"rs_to_sibling_{tag}")
        hs, off = [], 0
        for _, gname, ps, leads in groups:
            hs += _add_sibling([pf for pf, _ in ps], recv_a[off:off + len(ps)], cvec,
                               f"rs_add_sibling_{gname}_{'_'.join(map(str, leads[0]))}")
            off += len(ps)
        return groups, recv_a, hs

    def ffn_back(l, j, xin, dy, sv, sub, comm=None):
        dx, dmod, wacc, hb, dfb, ab, dgub, got = _ffn_bwd(
            xin, dy, mods[l][sub], sv[f"f{sub}"], sv[f"g{sub}"], sv[f"u{sub}"],
            wl[l][j]["up"], wl[l][j]["dn"], l, j, S, comm)
        g_up_l[l][j] = _mm_tn(hb, dgub, f"dw_up_{l}_{j}", tnb=tf, split_n=True, with_bf16=True)
        g_dn_l[l][j] = tuple(g.reshape(N_SHARD, -1, D)
                             for g in _mm_tn(ab, dfb, f"dw_dn_{l}_{j}", tma=tf, with_bf16=True))
        dmods[l][sub] = dmod
        small[l][f"gn{sub}"] = wacc[0]
        return dx, got

    batches = []
    for l in (1, 0):
        sv = saved[l]
        dxc, _ = ffn_back(l, 1, sv["x2"], dxc, sv, 2)
        dyl, dsb, dfx, dmo, dmod1, wacc_mo = _mix_out_bwd(
            dxc, sv["ylru"], sv["osb"], sv["ofx"], sv["mo"], mods[l][1], gmix8, wl[l]["mix"]["out"], l, S)
        small[l]["gmix"] = wacc_mo[0]
        g_out_l[l] = tuple(g.reshape(N_SHARD, -1, D) for g in _mm_tn(sv["y"], dmo, f"dw_out_{l}", with_bf16=True))
        dsq, dsk, dsv, got = _sbq_bwd(sv["proj"], dsb, sv["t1"], l, S,
                                       _rs_to_chips_comm(rs1[2]) if l == 0 else None)
        if l == 0:
            batches.append((rs1[0], rs1[1], got))
        dfq, dfk, dfv, dck, wacc_fx = _foxq_bwd(sv["proj"], dfx, sv["nl"], sv["ofx"], sv["ck"], gqk2, l, S)
        small[l]["gqk"] = wacc_fx[0:2, :HD] + wacc_fx[0:2, HD:]
        dcum = dck[:, :, :, 0, :].reshape(B, NH, S).transpose(0, 2, 1).reshape(T, NH)
        dff_, wacc_fg = _fgate_bwd(jnp.pad(dcum, ((0, 0), (0, 128 - NH))), sv["proj"], bfp, l, S)
        small[l]["bf"] = wacc_fg[0, :NH]
        dlx, dlg, dpr, dpi, ub, wacc_lru = _lru_bwd(dyl, sv["proj"], sv["hl"], cw8, vp8, wr_d, wi_d, l, S)
        small[l]["lru"] = wacc_lru
        small[l]["wr"] = _diag_blocks(_mm_tn(ub, dpr, f"dw_rgate_{l}"))
        small[l]["wi"] = _diag_blocks(_mm_tn(ub, dpi, f"dw_igate_{l}"))
        dproj = jnp.concatenate(
            [dlx, dlg, dsq, dsk, dsv, dfq, dfk, dfv, dff_], axis=1)
        g_in = _mm_tn(sv["h1"], dproj, f"dw_in_{l}", tnb=N_INP // 3)[:, :N_IN]
        g_in = g_in.reshape(D, N_SHARD, -1).transpose(1, 0, 2)
        g_in_l[l] = (g_in, g_in.astype(BF16))
        dxc, dmod_in, wacc_in = _mix_in_bwd(sv["x1"], dxc, mods[l][1], dproj, wl[l]["mix"]["inp"], l, S)
        dmods[l][1] = dmod_in + dmod1
        small[l]["gn1"] = wacc_in[0]
        if l == 1:
            dxc, _ = ffn_back(l, 0, sv["x0"], dxc, sv, 0)
            rs1 = rs_sibling_phase(ffn_groups(1, 0) + mixer_groups(1) + ffn_groups(1, 1), "1")
        else:
            late = rs_sibling_phase(mixer_groups(0) + ffn_groups(0, 1), "0_late")
            dxc, got = ffn_back(l, 0, sv["x0"], dxc, sv, 0, _rs_to_chips_comm(late[2]))
            batches.append((late[0], late[1], got))
    grad_x = dxc.reshape(B, S, D)

    dmod_loc = jnp.stack([jnp.stack([dmods[l][j][:, 0:3, :] for j in range(3)], axis=1) for l in range(2)])
    drows = 2 * B * 9
    blk3 = _pack_rows([dmod_loc], -(-drows // 8) * 8, F32)
    ag3 = _all_gather8(blk3, "ag_dmod", True)
    dmod_all = ag3[:, :drows].reshape(N_DEV, 2, B, 9 * D).transpose(1, 0, 2, 3).reshape(2, n_all, 9 * D)
    dmod_mine = lax.dynamic_slice(dmod_all, (0, 0, sidx * ada_cols), (2, n_all, ada_cols))
    grad_w_ada = _ada_bwd(c_all, dmod_mine)
    dmod_rows = jnp.pad(dmod_all.transpose(1, 0, 2).reshape(n_all, 2 * 9, D), ((0, 0), (0, 6), (0, 0)))
    grad_b_ada = _sum_lead(dmod_rows, "grad_b_ada")[:2 * 9].reshape(2, 9 * D)

    sm_parts = [
        jnp.stack([small[l]["bf"] for l in range(2)]),
        jnp.stack([small[l]["lru"][4] for l in range(2)]),
        jnp.stack([small[l]["wr"] for l in range(2)]),
        jnp.stack([small[l]["lru"][5] for l in range(2)]),
        jnp.stack([small[l]["wi"] for l in range(2)]),
        jnp.stack([small[l]["lru"][6] for l in range(2)]),
        jnp.stack([small[l]["lru"][7] for l in range(2)]),
        jnp.stack([small[l]["gqk"] for l in range(2)]),
        jnp.stack([small[l]["gmix"] for l in range(2)]),
        jnp.stack([jnp.stack([small[l][f"gn{j}"] for j in range(3)]) for l in range(2)]),
        jnp.stack([small[l]["lru"][0:4] for l in range(2)]),
    ]
    sm_shapes = [p.shape for p in sm_parts]
    sm_rows = -(-sum(p.size for p in sm_parts) // (8 * ROW)) * 8
    ag4 = _all_gather8(_pack_rows(sm_parts, sm_rows, F32), "ag_small_grads", True)
    sm_sum = _sum_lead(ag4, "sum_small_grads").reshape(-1)
    (g_bf, g_cb, g_wr, g_br, g_wi, g_bi, g_lam, g_gqk, g_gmix, g_gn_full, g_cw_full) = _unpack(sm_sum, sm_shapes)
    g_gn = lax.dynamic_slice(g_gn_full, (0, 0, sidx * gn_cols), (2, 3, gn_cols))
    g_cw = lax.dynamic_slice(g_cw_full, (0, 0, sidx * cw_cols), (2, 4, cw_cols))

    last = rs_sibling_phase(ffn_groups(0, 0), "0_first")
    batches.append((last[0], last[1], _run_comm(_rs_to_chips_comm(last[2]), "rs_to_chips_0_first")))
    shapes4 = [w_ffn_up.shape, w_ffn_down.shape, w_in.shape, w_out.shape]
    tensors, places, r2s = [None] * 4, [], []
    for groups, recv_a, recv_b in batches:
        k = 0
        for gi, gname, ps, leads in groups:
            for (pf, _), lead in zip(ps, leads):
                tensors[gi] = _add_chips_into(pf, recv_a[k], recv_b[k], scvec, tensors[gi], shapes4[gi], lead,
                                              f"rs_add_chips_{gname}_{'_'.join(map(str, lead))}")
                places.append((gi, lead))
                r2s.append(pf.shape[1] // 2)
                k += 1
    gw_up, gw_dn, gw_in, gw_out = _share_halves(tensors, places, r2s)

    def upd(w, g, m, v, name):
        sh = w.shape
        two = (w.size // sh[-1], sh[-1])
        dlt, mn, vn = _adamw(w.reshape(two), g.reshape(two), m.reshape(two), v.reshape(two), name)
        return dlt.reshape(sh), mn.reshape(sh), vn.reshape(sh)

    big = {
        "w_ada": (w_ada, grad_w_ada, m_w_ada, v_w_ada),
        "w_ffn_up": (w_ffn_up, gw_up, m_w_ffn_up, v_w_ffn_up),
        "w_ffn_down": (w_ffn_down, gw_dn, m_w_ffn_down, v_w_ffn_down),
        "w_in": (w_in, gw_in, m_w_in, v_w_in),
        "w_out": (w_out, gw_out, m_w_out, v_w_out),
    }
    res = {n: (t[1],) + upd(*t, f"adamw_{n}") for n, t in big.items()}

    smalls = {
        "b_ada": (b_ada, grad_b_ada, m_b_ada, v_b_ada),
        "g_norm": (g_norm, g_gn, m_g_norm, v_g_norm),
        "b_fgate": (b_fgate, g_bf, m_b_fgate, v_b_fgate),
        "conv_w": (conv_w, g_cw, m_conv_w, v_conv_w),
        "conv_b": (conv_b, g_cb, m_conv_b, v_conv_b),
        "w_rgate": (w_rgate, g_wr, m_w_rgate, v_w_rgate),
        "b_rgate": (b_rgate, g_br, m_b_rgate, v_b_rgate),
        "w_igate": (w_igate, g_wi, m_w_igate, v_w_igate),
        "b_igate": (b_igate, g_bi, m_b_igate, v_b_igate),
        "lru_lambda": (lru_lambda, g_lam, m_lru_lambda, v_lru_lambda),
        "g_qk": (g_qk, g_gqk, m_g_qk, v_g_qk),
        "g_mix_out": (g_mix_out, g_gmix, m_g_mix_out, v_g_mix_out),
    }
    names = list(smalls)
    shapes = [smalls[n][0].shape for n in names]
    prow = -(-sum(math.prod(s) for s in shapes) // (8 * ROW)) * 8
    packed = [_pack_rows([smalls[n][i].reshape(shapes[k]) for k, n in enumerate(names)], prow, F32) for i in range(4)]
    outs = _adamw(packed[0], packed[1], packed[2], packed[3], "adamw_small")
    un = [_unpack(o.reshape(-1), shapes) for o in outs]
    for k, n in enumerate(names):
        res[n] = (smalls[n][1].reshape(shapes[k]), un[0][k], un[1][k], un[2][k])

    order = ["w_ada", "b_ada", "g_norm", "w_ffn_up", "w_ffn_down", "w_in", "b_fgate", "conv_w", "conv_b",
             "w_rgate", "b_rgate", "w_igate", "b_igate", "lru_lambda", "g_qk", "g_mix_out", "w_out"]
    return (loss, grad_x, *[res[n][0] for n in order], *[res[n][1] for n in order],
            *[res[n][2] for n in order], *[res[n][3] for n in order])


def TQ_(S):
    return min(TQ, S)
```

```python
import math

import jax
import jax.numpy as jnp
from jax import lax
from jax.experimental import pallas as pl
from jax.experimental.pallas import tpu as pltpu

F32 = jnp.float32
BF16 = jnp.bfloat16
MESH = pl.DeviceIdType.MESH

D = 1024
HD = 64
LW = 512
NH = 4
AW = NH * HD
N_IN = 2564
N_INP = 2688
F_BLK = 2560 // 128
EPS = 1e-6
LRU_C = 8.0
SCALE = HD ** -0.5
NEG = -1e30
TQ = 256

ADAM_LR, ADAM_B1, ADAM_B2, ADAM_EPS, ADAM_WD, ADAM_STEP = 0.001, 0.9, 0.999, 1e-08, 0.01, 10

VMEM_BIG = 56 * 1024 * 1024
N_DEV = 8
N_SHARD = 4
ROW = 1024


def _cp(sem, vmem=None):
    return pltpu.CompilerParams(dimension_semantics=sem, vmem_limit_bytes=vmem)


def _dot(a, b):
    return jnp.dot(a, b, preferred_element_type=F32)


def _dot_nt(a, b):
    return lax.dot_general(a, b, (((1,), (1,)), ((), ())), preferred_element_type=F32)


def _dot_tn(a, b):
    return lax.dot_general(a, b, (((0,), (0,)), ((), ())), preferred_element_type=F32)


def _log1p(e):
    small = e * (1.0 - e * (0.5 - e * (1.0 / 3.0 - e * 0.25)))
    return jnp.where(e < 0.01, small, jnp.log(1.0 + e))


def _expm1_neg(x):
    small = x * (1.0 + x * 0.5 * (1.0 + x * (1.0 / 3.0) * (1.0 + x * 0.25 * (1.0 + x * 0.2))))
    return jnp.where(x > -0.05, small, jnp.exp(x) - 1.0)


def _sigmoid(x):
    return 1.0 / (1.0 + jnp.exp(-x))


_GELU_C = math.sqrt(2.0 / math.pi)


def _gelu_and_grad(x):
    x2 = x * x
    th = jnp.tanh(_GELU_C * (x + 0.044715 * x * x2))
    g = 0.5 * x * (1.0 + th)
    dg = 0.5 * (1.0 + th) + 0.5 * x * (1.0 - th * th) * _GELU_C * (1.0 + 3.0 * 0.044715 * x2)
    return g, dg


def _rms_rows(x):
    rstd = lax.rsqrt(jnp.mean(x * x, axis=-1, keepdims=True) + EPS)
    return x * rstd, rstd


def _rms_bwd(xn, rstd, dyn):
    return rstd * (dyn - xn * jnp.mean(dyn * xn, axis=-1, keepdims=True))


def _colsum(x):
    return jnp.sum(x, axis=0, keepdims=True)


def _rowsum(x):
    return jnp.sum(x, axis=1, keepdims=True)


def _split3(x):
    hi = x.astype(BF16)
    r = x - hi.astype(F32)
    mid = r.astype(BF16)
    lo = (r - mid.astype(F32)).astype(BF16)
    return hi, mid, lo


def _cumsum_mm(x, ones_tri, parts=3):
    ps = _split3(x)[:parts]
    acc = _dot(ps[0], ones_tri)
    for p in ps[1:]:
        acc = acc + _dot(p, ones_tri)
    return acc


def _tri(n, kind):
    r = lax.broadcasted_iota(jnp.int32, (n, n), 0)
    c = lax.broadcasted_iota(jnp.int32, (n, n), 1)
    m = {"row_gt_col": r > c, "row_le_col": r <= c, "row_lt_col": r < c}[kind]
    return jnp.where(m, 1.0, 0.0).astype(BF16)


def _normmod(x, mod_ref):
    xn, rstd = _rms_rows(x)
    h = xn * mod_ref[3:4, :] * mod_ref[1:2, :] + mod_ref[0:1, :]
    return h, xn, rstd


def _normmod_bwd(dh, xn, rstd, mod_ref, dmod_ref, wacc_ref):
    gn = mod_ref[3:4, :]
    sc = mod_ref[1:2, :]
    dmod_ref[0:1, :] += _colsum(dh)
    t = _colsum(dh * xn)
    dmod_ref[1:2, :] += t * gn
    wacc_ref[0:1, :] += t * sc
    return _rms_bwd(xn, rstd, dh * (gn * sc))


def _tile(n, want):
    t = min(n, want)
    while n % t:
        t //= 2
    return t


def _tile_div8(n, cap, mult=8):
    best = mult
    for t in range(mult, min(n, cap) + 1, mult):
        if n % t == 0:
            best = t
    assert n % best == 0
    return best


def _ffn_fwd(x, mod, wup, wdn, l, j, S, comm=None):
    T = x.shape[0]
    tf = wup.shape[-1]
    nk = 2
    tm = _tile(S, 512)
    tpb = S // tm
    nt = T // tm
    c_args, c_specs, c_outs, c_scr = _hosted(comm)
    n_ci, n_co = len(c_args), len(c_outs)

    def body(*refs):
        x_ref, mod_ref, wg_ref, wu_ref, wd_ref = refs[:5]
        c_in = refs[5:5 + n_ci]
        xo_ref, g_ref, u_ref, f_ref = refs[5 + n_ci:9 + n_ci]
        c_out = refs[9 + n_ci:9 + n_ci + n_co]
        h_sc, acc_sc = refs[9 + n_ci + n_co:11 + n_ci + n_co]
        c_sems = refs[11 + n_ci + n_co:]
        i = pl.program_id(0)
        k = pl.program_id(1)
        if comm is not None:
            @pl.when((i == 0) & (k == 0))
            def _():
                comm["start"](c_in, c_out, c_sems)

            @pl.when((i == nt - 2) & (k == 0))
            def _():
                _relay(comm, (c_in, c_out, c_sems))

        @pl.when(k == 0)
        def _():
            h, _, _ = _normmod(x_ref[...], mod_ref)
            h_sc[...] = h.astype(BF16)
            acc_sc[...] = jnp.zeros_like(acc_sc)

        h = h_sc[...]
        g = _dot(h, wg_ref[...])
        u = _dot(h, wu_ref[...])
        g_ref[...] = g.astype(BF16)
        u_ref[...] = u.astype(BF16)
        a = (g * _sigmoid(g)) * u
        acc_sc[...] += _dot(a.astype(BF16), wd_ref[...])

        @pl.when(k == nk - 1)
        def _():
            f = acc_sc[...]
            f_ref[...] = f.astype(BF16)
            xo_ref[...] = x_ref[...] + (0.5 * mod_ref[2:3, :]) * f

        if comm is not None:
            @pl.when((i == nt - 1) & (k == nk - 1))
            def _():
                comm["finish"](c_in, c_out, c_sems)

    anyspec = pl.BlockSpec(memory_space=pl.ANY)
    out = pl.pallas_call(
        body, name=f"ffn_fwd_{l}_{j}",
        grid=(nt, nk),
        in_specs=[
            pl.BlockSpec((tm, D), lambda i, k: (i, 0)),
            pl.BlockSpec((None, 8, D), lambda i, k: (i // tpb, 0, 0)),
            pl.BlockSpec((None, D, tf), lambda i, k: (k, 0, 0)),
            pl.BlockSpec((None, D, tf), lambda i, k: (nk + k, 0, 0)),
            pl.BlockSpec((tf, D), lambda i, k: (k, 0)),
        ] + c_specs,
        out_specs=[
            pl.BlockSpec((tm, D), lambda i, k: (i, 0)),
            pl.BlockSpec((tm, tf), lambda i, k: (i, k)),
            pl.BlockSpec((tm, tf), lambda i, k: (i, k)),
            pl.BlockSpec((tm, D), lambda i, k: (i, 0)),
        ] + [anyspec] * n_co,
        out_shape=[
            jax.ShapeDtypeStruct((T, D), F32),
            jax.ShapeDtypeStruct((T, nk * tf), BF16),
            jax.ShapeDtypeStruct((T, nk * tf), BF16),
            jax.ShapeDtypeStruct((T, D), BF16),
        ] + c_outs,
        scratch_shapes=[pltpu.VMEM((tm, D), BF16), pltpu.VMEM((tm, D), F32)] + c_scr,
        compiler_params=_cp(("arbitrary", "arbitrary"), VMEM_BIG),
    )(x, mod, wup, wup, wdn, *c_args)
    return out[0], out[1], out[2], out[3], list(out[4:])


def _ffn_bwd(x, dy, mod, f, g, u, wup, wdn, l, j, S, comm=None):
    T = x.shape[0]
    tf = wup.shape[-1]
    nk = 2
    tm = _tile(S, 256)
    tpb = S // tm
    nt = T // tm
    c_args, c_specs, c_outs, c_scr = _hosted(comm)
    n_ci, n_co = len(c_args), len(c_outs)

    def body(*refs):
        x_ref, dy_ref, mod_ref, f_ref, g_ref, u_ref, wup_ref, wd_ref = refs[:8]
        c_in = refs[8:8 + n_ci]
        dx_ref, dmod_ref, wacc_ref, h_ref, df_ref, a_ref, dgu_ref = refs[8 + n_ci:15 + n_ci]
        c_out = refs[15 + n_ci:15 + n_ci + n_co]
        c_sems = refs[15 + n_ci + n_co:]
        i = pl.program_id(0)

        @pl.when(i == 0)
        def _():
            wacc_ref[...] = jnp.zeros_like(wacc_ref)
            if comm is not None:
                comm["start"](c_in, c_out, c_sems)

        @pl.when(i % tpb == 0)
        def _():
            dmod_ref[...] = jnp.zeros_like(dmod_ref)

        dy_ = dy_ref[...]
        h, xn, rstd = _normmod(x_ref[...], mod_ref)
        h_ref[...] = h.astype(BF16)
        dfb = ((0.5 * mod_ref[2:3, :]) * dy_).astype(BF16)
        df_ref[...] = dfb
        dmod_ref[2:3, :] += _colsum(0.5 * f_ref[...].astype(F32) * dy_)
        dh = None
        for k in range(nk):
            cols = slice(k * tf, (k + 1) * tf)
            da = _dot_nt(dfb, wd_ref[cols, :])
            gg = g_ref[:, cols].astype(F32)
            uu = u_ref[:, cols].astype(F32)
            sig = _sigmoid(gg)
            s = gg * sig
            a_ref[:, cols] = (s * uu).astype(BF16)
            du = (da * s).astype(BF16)
            dg = (da * uu * (sig * (1.0 + gg * (1.0 - sig)))).astype(BF16)
            dgu_ref[0, :, cols] = dg
            dgu_ref[1, :, cols] = du
            part = _dot_nt(dg, wup_ref[k]) + _dot_nt(du, wup_ref[nk + k])
            dh = part if dh is None else dh + part
        dx_ref[...] = dy_ + _normmod_bwd(dh, xn, rstd, mod_ref, dmod_ref, wacc_ref)

        if comm is not None:
            @pl.when(i == nt - 1)
            def _():
                comm["finish"](c_in, c_out, c_sems)

    once = pl.Buffered(1)
    anyspec = pl.BlockSpec(memory_space=pl.ANY)
    out = pl.pallas_call(
        body, name=f"ffn_bwd_{l}_{j}",
        grid=(nt,),
        in_specs=[
            pl.BlockSpec((tm, D), lambda i: (i, 0)),
            pl.BlockSpec((tm, D), lambda i: (i, 0)),
            pl.BlockSpec((None, 8, D), lambda i: (i // tpb, 0, 0)),
            pl.BlockSpec((tm, D), lambda i: (i, 0)),
            pl.BlockSpec((tm, nk * tf), lambda i: (i, 0)),
            pl.BlockSpec((tm, nk * tf), lambda i: (i, 0)),
            pl.BlockSpec((2 * nk, D, tf), lambda i: (0, 0, 0), pipeline_mode=once),
            pl.BlockSpec((nk * tf, D), lambda i: (0, 0), pipeline_mode=once),
        ] + c_specs,
        out_specs=[
            pl.BlockSpec((tm, D), lambda i: (i, 0)),
            pl.BlockSpec((None, 8, D), lambda i: (i // tpb, 0, 0)),
            pl.BlockSpec((8, D), lambda i: (0, 0)),
            pl.BlockSpec((tm, D), lambda i: (i, 0)),
            pl.BlockSpec((tm, D), lambda i: (i, 0)),
            pl.BlockSpec((tm, nk * tf), lambda i: (i, 0)),
            pl.BlockSpec((2, tm, nk * tf), lambda i: (0, i, 0)),
        ] + [anyspec] * n_co,
        out_shape=[
            jax.ShapeDtypeStruct((T, D), F32),
            jax.ShapeDtypeStruct((T // S, 8, D), F32),
            jax.ShapeDtypeStruct((8, D), F32),
            jax.ShapeDtypeStruct((T, D), BF16),
            jax.ShapeDtypeStruct((T, D), BF16),
            jax.ShapeDtypeStruct((T, nk * tf), BF16),
            jax.ShapeDtypeStruct((2, T, nk * tf), BF16),
        ] + c_outs,
        scratch_shapes=c_scr,
        compiler_params=_cp(("arbitrary",), VMEM_BIG),
    )(x, dy, mod, f, g, u, wup, wdn, *c_args)
    return tuple(out[:7]) + (list(out[7:]),)


def _mm_tn(a, b, name, tma=None, tnb=None, split_n=False, with_bf16=False):
    T, M = a.shape
    b3 = b if b.ndim == 3 else b[None]
    nb, _, N = b3.shape
    tma = tma or M
    tnb = tnb or N
    npb = N // tnb
    tt = _tile(T, 1024)
    nt = T // tt

    def body(a_ref, b_ref, o_ref, *ob_ref):
        @pl.when(pl.program_id(2) == 0)
        def _():
            o_ref[...] = jnp.zeros_like(o_ref)

        o_ref[...] += _dot_tn(a_ref[...], b_ref[...])

        if with_bf16:
            @pl.when(pl.program_id(2) == nt - 1)
            def _():
                ob_ref[0][...] = o_ref[...].astype(BF16)

    if split_n:
        shape = (nb * npb, M, tnb)
        out_spec = pl.BlockSpec((None, tma, tnb), lambda m, n, t: (n, m, 0))
    else:
        assert nb == 1
        shape = (M, N)
        out_spec = pl.BlockSpec((tma, tnb), lambda m, n, t: (m, n))
    dts = (F32, BF16) if with_bf16 else (F32,)
    out = pl.pallas_call(
        body, name=name,
        grid=(M // tma, nb * npb, nt),
        in_specs=[pl.BlockSpec((tt, tma), lambda m, n, t: (t, m)),
                  pl.BlockSpec((None, tt, tnb), lambda m, n, t: (n // npb, t, n % npb))],
        out_specs=[out_spec] * len(dts),
        out_shape=[jax.ShapeDtypeStruct(shape, dt) for dt in dts],
        compiler_params=_cp(("arbitrary", "arbitrary", "arbitrary"), VMEM_BIG),
    )(a, b3)
    return tuple(out) if with_bf16 else out[0]


def _mix_in_fwd(x, mod, winp, l, S):
    T = x.shape[0]
    tm = _tile(S, 512)
    tpb = S // tm

    def body(x_ref, mod_ref, w_ref, h_ref, p_ref):
        h, _, _ = _normmod(x_ref[...], mod_ref)
        hb = h.astype(BF16)
        h_ref[...] = hb
        p_ref[...] = _dot(hb, w_ref[...])

    return pl.pallas_call(
        body, name=f"mix_in_fwd_{l}",
        grid=(T // tm,),
        in_specs=[pl.BlockSpec((tm, D), lambda i: (i, 0)),
                  pl.BlockSpec((None, 8, D), lambda i: (i // tpb, 0, 0)),
                  pl.BlockSpec((D, N_INP), lambda i: (0, 0))],
        out_specs=[pl.BlockSpec((tm, D), lambda i: (i, 0)),
                   pl.BlockSpec((tm, N_INP), lambda i: (i, 0))],
        out_shape=[jax.ShapeDtypeStruct((T, D), BF16), jax.ShapeDtypeStruct((T, N_INP), F32)],
        compiler_params=_cp(("arbitrary",), VMEM_BIG),
    )(x, mod, winp)


def _mix_in_bwd(x, dres, mod, dproj, winp, l, S):
    T = x.shape[0]
    tm = _tile(S, 512)
    tpb = S // tm

    def body(x_ref, dr_ref, mod_ref, dp_ref, w_ref, dx_ref, dmod_ref, wacc_ref):
        i = pl.program_id(0)

        @pl.when(i == 0)
        def _():
            wacc_ref[...] = jnp.zeros_like(wacc_ref)

        @pl.when(i % tpb == 0)
        def _():
            dmod_ref[...] = jnp.zeros_like(dmod_ref)

        dh = _dot_nt(dp_ref[...], w_ref[...])
        _, xn, rstd = _normmod(x_ref[...], mod_ref)
        dx_ref[...] = dr_ref[...] + _normmod_bwd(dh, xn, rstd, mod_ref, dmod_ref, wacc_ref)

    return pl.pallas_call(
        body, name=f"mix_in_bwd_{l}",
        grid=(T // tm,),
        in_specs=[pl.BlockSpec((tm, D), lambda i: (i, 0)),
                  pl.BlockSpec((tm, D), lambda i: (i, 0)),
                  pl.BlockSpec((None, 8, D), lambda i: (i // tpb, 0, 0)),
                  pl.BlockSpec((tm, N_INP), lambda i: (i, 0)),
                  pl.BlockSpec((D, N_INP), lambda i: (0, 0))],
        out_specs=[pl.BlockSpec((tm, D), lambda i: (i, 0)),
                   pl.BlockSpec((None, 8, D), lambda i: (i // tpb, 0, 0)),
                   pl.BlockSpec((8, D), lambda i: (0, 0))],
        out_shape=[jax.ShapeDtypeStruct((T, D), F32),
                   jax.ShapeDtypeStruct((T // S, 8, D), F32),
                   jax.ShapeDtypeStruct((8, D), F32)],
        compiler_params=_cp(("arbitrary",), VMEM_BIG),
    )(x, dres, mod, dproj, winp)


_GROUPS = ((0, LW), (LW, LW + AW), (LW + AW, D))


def _mix_out_fwd(x, ylru, osb, ofox, mod, gmix, wout, l, S):
    T = x.shape[0]
    tm = _tile(S, 512)
    tpb = S // tm

    def body(x_ref, yl_ref, sb_ref, fx_ref, mod_ref, gm_ref, w_ref, xo_ref, y_ref, mo_ref):
        for src, (lo, hi) in zip((yl_ref, sb_ref, fx_ref), _GROUPS):
            vn, _ = _rms_rows(src[...])
            y_ref[:, lo:hi] = (vn * gm_ref[0:1, lo:hi]).astype(BF16)
        mo = _dot(y_ref[...], w_ref[...])
        mo_ref[...] = mo.astype(BF16)
        xo_ref[...] = x_ref[...] + mod_ref[2:3, :] * mo

    return pl.pallas_call(
        body, name=f"mix_out_fwd_{l}",
        grid=(T // tm,),
        in_specs=[pl.BlockSpec((tm, D), lambda i: (i, 0)),
                  pl.BlockSpec((tm, LW), lambda i: (i, 0)),
                  pl.BlockSpec((tm, AW), lambda i: (i, 0)),
                  pl.BlockSpec((tm, AW), lambda i: (i, 0)),
                  pl.BlockSpec((None, 8, D), lambda i: (i // tpb, 0, 0)),
                  pl.BlockSpec((None, 8, D), lambda i: (l, 0, 0)),
                  pl.BlockSpec((D, D), lambda i: (0, 0))],
        out_specs=[pl.BlockSpec((tm, D), lambda i: (i, 0)),
                   pl.BlockSpec((tm, D), lambda i: (i, 0)),
                   pl.BlockSpec((tm, D), lambda i: (i, 0))],
        out_shape=[jax.ShapeDtypeStruct((T, D), F32),
                   jax.ShapeDtypeStruct((T, D), BF16),
                   jax.ShapeDtypeStruct((T, D), BF16)],
        compiler_params=_cp(("arbitrary",), VMEM_BIG),
    )(x, ylru, osb, ofox, mod, gmix, wout)


def _mix_out_bwd(dx2, ylru, osb, ofox, mo, mod, gmix, wout, l, S):
    T = dx2.shape[0]
    tm = _tile(S, 512)
    tpb = S // tm

    def body(dx_ref, yl_ref, sb_ref, fx_ref, mo_ref, mod_ref, gm_ref, w_ref,
             dyl_ref, dsb_ref, dfx_ref, dmo_ref, dmod_ref, wacc_ref):
        i = pl.program_id(0)

        @pl.when(i == 0)
        def _():
            wacc_ref[...] = jnp.zeros_like(wacc_ref)

        @pl.when(i % tpb == 0)
        def _():
            dmod_ref[...] = jnp.zeros_like(dmod_ref)

        dx = dx_ref[...]
        dmod_ref[2:3, :] += _colsum(mo_ref[...].astype(F32) * dx)
        dmo = (mod_ref[2:3, :] * dx).astype(BF16)
        dmo_ref[...] = dmo
        dy = _dot_nt(dmo, w_ref[...])
        for src, dst, (lo, hi) in zip((yl_ref, sb_ref, fx_ref), (dyl_ref, dsb_ref, dfx_ref), _GROUPS):
            vn, rstd = _rms_rows(src[...])
            dyg = dy[:, lo:hi]
            wacc_ref[0:1, lo:hi] += _colsum(dyg * vn)
            dst[...] = _rms_bwd(vn, rstd, dyg * gm_ref[0:1, lo:hi])

    return pl.pallas_call(
        body, name=f"mix_out_bwd_{l}",
        grid=(T // tm,),
        in_specs=[pl.BlockSpec((tm, D), lambda i: (i, 0)),
                  pl.BlockSpec((tm, LW), lambda i: (i, 0)),
                  pl.BlockSpec((tm, AW), lambda i: (i, 0)),
                  pl.BlockSpec((tm, AW), lambda i: (i, 0)),
                  pl.BlockSpec((tm, D), lambda i: (i, 0)),
                  pl.BlockSpec((None, 8, D), lambda i: (i // tpb, 0, 0)),
                  pl.BlockSpec((None, 8, D), lambda i: (l, 0, 0)),
                  pl.BlockSpec((D, D), lambda i: (0, 0))],
        out_specs=[pl.BlockSpec((tm, LW), lambda i: (i, 0)),
                   pl.BlockSpec((tm, AW), lambda i: (i, 0)),
                   pl.BlockSpec((tm, AW), lambda i: (i, 0)),
                   pl.BlockSpec((tm, D), lambda i: (i, 0)),
                   pl.BlockSpec((None, 8, D), lambda i: (i // tpb, 0, 0)),
                   pl.BlockSpec((8, D), lambda i: (0, 0))],
        out_shape=[jax.ShapeDtypeStruct((T, LW), F32),
                   jax.ShapeDtypeStruct((T, AW), F32),
                   jax.ShapeDtypeStruct((T, AW), F32),
                   jax.ShapeDtypeStruct((T, D), BF16),
                   jax.ShapeDtypeStruct((T // S, 8, D), F32),
                   jax.ShapeDtypeStruct((8, D), F32)],
        compiler_params=_cp(("arbitrary",), VMEM_BIG),
    )(dx2, ylru, osb, ofox, mo, mod, gmix, wout)


def _loss_head(y, tgt, S):
    T = y.shape[0]
    tm = _tile(S, 512)

    def body(y_ref, t_ref, dy_ref, l_ref):
        @pl.when(pl.program_id(0) == 0)
        def _():
            l_ref[...] = jnp.zeros_like(l_ref)

        d = y_ref[...] - t_ref[...]
        dy_ref[...] = d * (1.0 / D)
        l_ref[...] += (0.5 / D) * _rowsum(_colsum(d * d))

    return pl.pallas_call(
        body, name="loss_head",
        grid=(T // tm,),
        in_specs=[pl.BlockSpec((tm, D), lambda i: (i, 0)), pl.BlockSpec((tm, D), lambda i: (i, 0))],
        out_specs=[pl.BlockSpec((tm, D), lambda i: (i, 0)), pl.BlockSpec((8, 128), lambda i: (0, 0))],
        out_shape=[jax.ShapeDtypeStruct((T, D), F32), jax.ShapeDtypeStruct((8, 128), F32)],
        compiler_params=_cp(("arbitrary",)),
    )(y, tgt)


def _lru_gates(u, vp_ref, wr_ref, wi_ref):
    ub = u.astype(BF16)
    r = _sigmoid(_dot(ub, wr_ref[...]) + vp_ref[1:2, :])
    ig = _sigmoid(_dot(ub, wi_ref[...]) + vp_ref[2:3, :])
    lam = vp_ref[3:4, :]
    sp = jnp.maximum(-lam, 0.0) + _log1p(jnp.exp(-jnp.abs(lam)))
    log_a = (-LRU_C) * r * sp
    a = jnp.exp(log_a)
    mult = jnp.sqrt(-_expm1_neg(2.0 * log_a))
    return ub, r, ig, sp, a, mult


def _conv_taps(x, xp, row, cw_ref):
    xs = [x]
    for d in (1, 2, 3):
        xs.append(jnp.where(row >= d, pltpu.roll(x, d, 0), pltpu.roll(xp, d, 0)))
    u = xs[0] * cw_ref[3:4, :]
    for d in (1, 2, 3):
        u = u + xs[d] * cw_ref[3 - d:4 - d, :]
    return xs, u


def _lru_fwd(proj, cw, vp, wr, wi, l, S):
    T = proj.shape[0]
    ts = _tile(S, 256)
    nb = S // ts

    def body(x_ref, lg_ref, cw_ref, vp_ref, wr_ref, wi_ref, y_ref, h_ref, xp_sc, hc_sc):
        @pl.when(pl.program_id(1) == 0)
        def _():
            xp_sc[...] = jnp.zeros_like(xp_sc)
            hc_sc[...] = jnp.zeros_like(hc_sc)

        row = lax.broadcasted_iota(jnp.int32, (ts, LW), 0)
        x = x_ref[...]
        _, u = _conv_taps(x, xp_sc[...], row, cw_ref)
        u = u + vp_ref[0:1, :]
        xp_sc[...] = x
        _, _, ig, _, a, mult = _lru_gates(u, vp_ref, wr_ref, wi_ref)
        bv = mult * (ig * u)
        av = a
        d = 1
        while d < ts:
            a_s = jnp.where(row >= d, pltpu.roll(av, d, 0), 1.0)
            b_s = jnp.where(row >= d, pltpu.roll(bv, d, 0), 0.0)
            bv = av * b_s + bv
            av = av * a_s
            d *= 2
        h = bv + av * hc_sc[7:8, :]
        hc_sc[...] = h[ts - 8:ts, :]
        h_ref[...] = h
        gl, _ = _gelu_and_grad(lg_ref[...])
        y_ref[...] = h * gl

    return pl.pallas_call(
        body, name=f"lru_fwd_{l}",
        grid=(T // S, nb),
        in_specs=[pl.BlockSpec((ts, LW), lambda b, j: (b * nb + j, 0)),
                  pl.BlockSpec((ts, LW), lambda b, j: (b * nb + j, 1)),
                  pl.BlockSpec((None, 8, LW), lambda b, j: (l, 0, 0)),
                  pl.BlockSpec((None, 8, LW), lambda b, j: (l, 0, 0)),
                  pl.BlockSpec((None, LW, LW), lambda b, j: (l, 0, 0)),
                  pl.BlockSpec((None, LW, LW), lambda b, j: (l, 0, 0))],
        out_specs=[pl.BlockSpec((ts, LW), lambda b, j: (b * nb + j, 0)),
                   pl.BlockSpec((ts, LW), lambda b, j: (b * nb + j, 0))],
        out_shape=[jax.ShapeDtypeStruct((T, LW), F32), jax.ShapeDtypeStruct((T, LW), F32)],
        scratch_shapes=[pltpu.VMEM((ts, LW), F32), pltpu.VMEM((8, LW), F32)],
        compiler_params=_cp(("arbitrary", "arbitrary")),
    )(proj, proj, cw, vp, wr, wi)


def _lru_bwd(dyl, proj, h, cw, vp, wr, wi, l, S):
    T = proj.shape[0]
    ts = _tile(S, 256)
    nb = S // ts

    def body(dy_ref, x_ref, xprev_ref, lg_ref, h_ref, hprev_ref, cw_ref, vp_ref, wr_ref, wi_ref,
             dx_ref, dlg_ref, dpr_ref, dpi_ref, ub_ref, wacc_ref, gc_sc, af_sc, dun_sc):
        b = pl.program_id(0)
        j = pl.program_id(1)
        first = j == nb - 1

        @pl.when((b == 0) & (j == 0))
        def _():
            wacc_ref[...] = jnp.zeros_like(wacc_ref)

        @pl.when(j == 0)
        def _():
            gc_sc[...] = jnp.zeros_like(gc_sc)
            af_sc[...] = jnp.ones_like(af_sc)
            dun_sc[...] = jnp.zeros_like(dun_sc)

        row = lax.broadcasted_iota(jnp.int32, (ts, LW), 0)
        keep = jnp.where(first, 0.0, 1.0)
        x = x_ref[...]
        xs, u = _conv_taps(x, xprev_ref[...] * keep, row, cw_ref)
        u = u + vp_ref[0:1, :]
        ub, r, ig, sp, a, mult = _lru_gates(u, vp_ref, wr_ref, wi_ref)
        ub_ref[...] = ub
        hh = h_ref[...]
        h_m1 = jnp.where(row >= 1, pltpu.roll(hh, 1, 0), pltpu.roll(hprev_ref[...] * keep, 1, 0))
        dy = dy_ref[...]
        gl, dgl = _gelu_and_grad(lg_ref[...])
        dlg_ref[...] = (dy * hh * dgl).astype(BF16)
        bv = dy * gl
        av = jnp.where(row < ts - 1, pltpu.roll(a, ts - 1, 0), af_sc[0:1, :])
        d = 1
        while d < ts:
            a_s = jnp.where(row < ts - d, pltpu.roll(av, ts - d, 0), 1.0)
            b_s = jnp.where(row < ts - d, pltpu.roll(bv, ts - d, 0), 0.0)
            bv = av * b_s + bv
            av = av * a_s
            d *= 2
        gt = bv + av * gc_sc[0:1, :]
        gc_sc[...] = gt[0:8, :]
        af_sc[...] = a[0:8, :]
        da = gt * h_m1
        d_ig = gt * mult * u
        d_mult = gt * ig * u
        du = gt * mult * ig
        dlog_a = da * a - d_mult * (a * a) / mult
        dpre_r = (dlog_a * ((-LRU_C) * sp)) * r * (1.0 - r)
        dpre_i = d_ig * ig * (1.0 - ig)
        lam = vp_ref[3:4, :]
        wacc_ref[7:8, :] += _colsum(dlog_a * r) * (LRU_C * _sigmoid(-lam))
        wacc_ref[5:6, :] += _colsum(dpre_r)
        wacc_ref[6:7, :] += _colsum(dpre_i)
        dprb = dpre_r.astype(BF16)
        dpib = dpre_i.astype(BF16)
        dpr_ref[...] = dprb
        dpi_ref[...] = dpib
        du = du + _dot_nt(dprb, wr_ref[...]) + _dot_nt(dpib, wi_ref[...])
        wacc_ref[4:5, :] += _colsum(du)
        dun = dun_sc[...]
        dx = du * cw_ref[3:4, :]
        wacc_ref[3:4, :] += _colsum(du * xs[0])
        for dd in (1, 2, 3):
            du_s = jnp.where(row < ts - dd, pltpu.roll(du, ts - dd, 0), pltpu.roll(dun, ts - dd, 0))
            dx = dx + du_s * cw_ref[3 - dd:4 - dd, :]
            wacc_ref[3 - dd:4 - dd, :] += _colsum(du * xs[dd])
        dun_sc[...] = du
        dx_ref[...] = dx.astype(BF16)

    def tb(b, j):
        return b * nb + (nb - 1 - j)

    def tbp(b, j):
        return b * nb + jnp.maximum(nb - 2 - j, 0)

    return pl.pallas_call(
        body, name=f"lru_bwd_{l}",
        grid=(T // S, nb),
        in_specs=[pl.BlockSpec((ts, LW), lambda b, j: (tb(b, j), 0)),
                  pl.BlockSpec((ts, LW), lambda b, j: (tb(b, j), 0)),
                  pl.BlockSpec((ts, LW), lambda b, j: (tbp(b, j), 0)),
                  pl.BlockSpec((ts, LW), lambda b, j: (tb(b, j), 1)),
                  pl.BlockSpec((ts, LW), lambda b, j: (tb(b, j), 0)),
                  pl.BlockSpec((ts, LW), lambda b, j: (tbp(b, j), 0)),
                  pl.BlockSpec((None, 8, LW), lambda b, j: (l, 0, 0)),
                  pl.BlockSpec((None, 8, LW), lambda b, j: (l, 0, 0)),
                  pl.BlockSpec((None, LW, LW), lambda b, j: (l, 0, 0)),
                  pl.BlockSpec((None, LW, LW), lambda b, j: (l, 0, 0))],
        out_specs=[pl.BlockSpec((ts, LW), lambda b, j: (tb(b, j), 0)),
                   pl.BlockSpec((ts, LW), lambda b, j: (tb(b, j), 0)),
                   pl.BlockSpec((ts, LW), lambda b, j: (tb(b, j), 0)),
                   pl.BlockSpec((ts, LW), lambda b, j: (tb(b, j), 0)),
                   pl.BlockSpec((ts, LW), lambda b, j: (tb(b, j), 0)),
                   pl.BlockSpec((8, LW), lambda b, j: (0, 0))],
        out_shape=[jax.ShapeDtypeStruct((T, LW), BF16),
                   jax.ShapeDtypeStruct((T, LW), BF16),
                   jax.ShapeDtypeStruct((T, LW), BF16),
                   jax.ShapeDtypeStruct((T, LW), BF16),
                   jax.ShapeDtypeStruct((T, LW), BF16),
                   jax.ShapeDtypeStruct((8, LW), F32)],
        scratch_shapes=[pltpu.VMEM((8, LW), F32), pltpu.VMEM((8, LW), F32), pltpu.VMEM((ts, LW), F32)],
        compiler_params=_cp(("arbitrary", "arbitrary")),
    )(dyl, proj, proj, proj, h, h, cw, vp, wr, wi)


def _fgate_fwd(proj, bfp, l, S):
    T = proj.shape[0]

    def body(x_ref, b_ref, o_ref):
        z = x_ref[...] + b_ref[0:1, :]
        v = jnp.minimum(z, 0.0) - _log1p(jnp.exp(-jnp.abs(z)))
        row = lax.broadcasted_iota(jnp.int32, (S, 128), 0)
        d = 1
        while d < S:
            v = v + jnp.where(row >= d, pltpu.roll(v, d, 0), 0.0)
            d *= 2
        o_ref[...] = v

    return pl.pallas_call(
        body, name=f"fgate_fwd_{l}",
        grid=(T // S,),
        in_specs=[pl.BlockSpec((S, 128), lambda b: (b, F_BLK)),
                  pl.BlockSpec((None, 8, 128), lambda b: (l, 0, 0))],
        out_specs=pl.BlockSpec((S, 128), lambda b: (b, 0)),
        out_shape=jax.ShapeDtypeStruct((T, 128), F32),
        compiler_params=_cp(("arbitrary",)),
    )(proj, bfp)


def _fgate_bwd(dcum, proj, bfp, l, S):
    T = proj.shape[0]

    def body(d_ref, x_ref, b_ref, o_ref, wacc_ref):
        @pl.when(pl.program_id(0) == 0)
        def _():
            wacc_ref[...] = jnp.zeros_like(wacc_ref)

        v = d_ref[...]
        row = lax.broadcasted_iota(jnp.int32, (S, 128), 0)
        d = 1
        while d < S:
            v = v + jnp.where(row < S - d, pltpu.roll(v, S - d, 0), 0.0)
            d *= 2
        z = x_ref[...] + b_ref[0:1, :]
        dz = v * _sigmoid(-z)
        o_ref[...] = dz.astype(BF16)
        wacc_ref[0:1, :] += _colsum(dz)

    return pl.pallas_call(
        body, name=f"fgate_bwd_{l}",
        grid=(T // S,),
        in_specs=[pl.BlockSpec((S, 128), lambda b: (b, 0)),
                  pl.BlockSpec((S, 128), lambda b: (b, F_BLK)),
                  pl.BlockSpec((None, 8, 128), lambda b: (l, 0, 0))],
        out_specs=[pl.BlockSpec((S, 128), lambda b: (b, 0)), pl.BlockSpec((8, 128), lambda b: (0, 0))],
        out_shape=[jax.ShapeDtypeStruct((T, 128), BF16), jax.ShapeDtypeStruct((8, 128), F32)],
        compiler_params=_cp(("arbitrary",)),
    )(dcum, proj, bfp)


SBQ_BLK, SBK_BLK, SBV_BLK = 8, 10, 12
FXQ_BLK, FXK_BLK, FXV_BLK = 14, 16, 18
PAIR = 2 * HD


def _lane_masks():
    lane = lax.broadcasted_iota(jnp.int32, (1, PAIR), 1)
    return lane, lane < HD


def _pair_select(m0, a0, a1):
    return jnp.where(m0, a0, a1)


def _pair_split(x, m0):
    return jnp.where(m0, x, 0.0).astype(BF16), jnp.where(m0, 0.0, x).astype(BF16)


def _pair_mean(x, m0):
    s0 = _rowsum(jnp.where(m0, x, 0.0))
    s1 = _rowsum(x) - s0
    return jnp.where(m0, s0, s1) * (1.0 / HD)


def _pair_rms(x, m0):
    rstd = lax.rsqrt(_pair_mean(x * x, m0) + EPS)
    return x * rstd, rstd


def _pair_rms_bwd(xn, rstd, dyn, m0):
    return rstd * (dyn - xn * _pair_mean(dyn * xn, m0))


def _logsig2(z):
    l1p = jnp.log(1.0 + jnp.exp(-jnp.abs(z)))
    lb = jnp.minimum(z, 0.0) - l1p
    return lb, lb - z


def _rows(ref, blk, size):
    return ref[pl.ds(pl.multiple_of(blk * size, size), size), :]


def _loop_grouped(n, body, init, groups=(4, 2, 1)):
    c, done = init, 0
    for per in groups:
        def several(i, cc, per=per, done=done):
            for j in range(per):
                cc = body(done + per * i + j, cc)
            return cc

        trips = (n - done) // per
        c = lax.fori_loop(0, trips, several, c)
        done = done + trips * per
    return c


def _transpose_blocks(src_ref, dst_sc, nblk, blk):
    for kb in range(nblk):
        dst_sc[kb] = src_ref[kb * blk:(kb + 1) * blk, :].astype(F32).T.astype(BF16)


def _sbq_fwd(proj, l, S, comm=None):
    T = proj.shape[0]
    tb = TQ_(S)
    nb = S // tb
    nbat = T // S
    c_args, c_specs, c_outs, c_scr = _hosted(comm)
    n_ci, n_co = len(c_args), len(c_outs)

    def body(*refs):
        q_ref, k_ref, v_ref = refs[:3]
        c_in = refs[3:3 + n_ci]
        o_ref, t1_ref = refs[3 + n_ci:5 + n_ci]
        c_out = refs[5 + n_ci:5 + n_ci + n_co]
        kt_sc, vb_sc = refs[5 + n_ci + n_co:7 + n_ci + n_co]
        c_sems = refs[7 + n_ci + n_co:]
        step = pl.program_id(0) * 2 + pl.program_id(1)
        if comm is not None:
            @pl.when(step == 0)
            def _():
                comm["start"](c_in, c_out, c_sems)

            @pl.when(step == 2 * nbat - 1)
            def _():
                _relay(comm, (c_in, c_out, c_sems))

        _transpose_blocks(k_ref, kt_sc, nb, tb)
        vb_sc[...] = v_ref[...].astype(BF16)
        lane, m0 = _lane_masks()
        tri = _tri(tb, "row_gt_col")
        past = lax.broadcasted_iota(jnp.int32, (tb, tb), 1) < lax.broadcasted_iota(jnp.int32, (tb, tb), 0)

        def qloop(qb, carry):
            qh = _pair_split(_rows(q_ref, qb, tb) * SCALE, m0)

            def scores(kb):
                return tuple(_dot(qh[h], kt_sc[kb]) for h in range(2))

            def block(kb, kb_next, z, c, masked):
                mid = []
                for h in range(2):
                    lb, l1 = _logsig2(z[h])
                    if masked:
                        l1 = jnp.where(past, l1, 0.0)
                    mid.append((lb, l1, _cumsum_mm(l1, tri, parts=2)))
                z_next = scores(kb_next)
                pv, runs = [], []
                for h in range(2):
                    lb, l1, cs = mid[h]
                    w = jnp.exp(lb + (cs + c[h][1]))
                    if masked:
                        w = jnp.where(past, w, 0.0)
                    pv.append(_dot(w.astype(BF16), _rows(vb_sc, kb, tb)))
                    runs.append(c[h][1] + (cs[:, 0:1] + l1[:, 0:1]))
                return z_next, tuple((c[h][0] + pv[h], runs[h]) for h in range(2))

            zero = (jnp.zeros((tb, PAIR), F32), jnp.zeros((tb, 1), F32))
            z, c = block(qb, jnp.maximum(qb - 1, 0), scores(qb), (zero, zero), True)

            def off_diag(i, zc):
                kb = qb - 1 - i
                return block(kb, jnp.maximum(kb - 1, 0), zc[0], zc[1], False)

            _, c = _loop_grouped(qb, off_diag, (z, c))
            r0 = pl.multiple_of(qb * tb, tb)
            o_ref[pl.ds(r0, tb), :] = _pair_select(m0, c[0][0], c[1][0])
            t1_ref[pl.ds(r0, tb), :] = jnp.where(lane == 0, c[0][1], jnp.where(lane == 1, c[1][1], 0.0))
            return carry

        lax.fori_loop(0, nb, qloop, 0)
        if comm is not None:
            @pl.when(step == 2 * nbat - 1)
            def _():
                comm["finish"](c_in, c_out, c_sems)

    def col(blk):
        return pl.BlockSpec((S, PAIR), lambda b, p: (b, blk + p))

    anyspec = pl.BlockSpec(memory_space=pl.ANY)
    out = pl.pallas_call(
        body, name=f"sb_fwd_{l}",
        grid=(nbat, 2),
        in_specs=[col(SBQ_BLK), col(SBK_BLK), col(SBV_BLK)] + c_specs,
        out_specs=[col(0), col(0)] + [anyspec] * n_co,
        out_shape=[jax.ShapeDtypeStruct((T, AW), F32), jax.ShapeDtypeStruct((T, AW), F32)] + c_outs,
        scratch_shapes=[pltpu.VMEM((nb, PAIR, tb), BF16), pltpu.VMEM((S, PAIR), BF16)] + c_scr,
        compiler_params=_cp(("arbitrary", "arbitrary"), VMEM_BIG),
    )(proj, proj, proj, *c_args)
    return out[0], out[1], list(out[2:])


def _sbq_bwd(proj, do, t1, l, S, comm=None):
    T = proj.shape[0]
    tb = TQ_(S)
    nb = S // tb
    nbat = T // S
    c_args, c_specs, c_outs, c_scr = _hosted(comm)
    n_ci, n_co = len(c_args), len(c_outs)

    def body(*refs):
        q_ref, k_ref, v_ref, do_ref, t1_ref = refs[:5]
        c_in = refs[5:5 + n_ci]
        dq_ref, dk_ref, dv_ref = refs[5 + n_ci:8 + n_ci]
        c_out = refs[8 + n_ci:8 + n_ci + n_co]
        kb_sc, kt_sc, vt_sc, dkt_sc, dvt_sc = refs[8 + n_ci + n_co:13 + n_ci + n_co]
        c_sems = refs[13 + n_ci + n_co:]
        step = pl.program_id(0) * 2 + pl.program_id(1)
        if comm is not None:
            @pl.when(step == 0)
            def _():
                comm["start"](c_in, c_out, c_sems)

        kb_sc[...] = k_ref[...].astype(BF16)
        _transpose_blocks(k_ref, kt_sc, nb, tb)
        _transpose_blocks(v_ref, vt_sc, nb, tb)
        dkt_sc[...] = jnp.zeros_like(dkt_sc)
        dvt_sc[...] = jnp.zeros_like(dvt_sc)
        _, m0 = _lane_masks()
        mt0 = lax.broadcasted_iota(jnp.int32, (PAIR, 1), 0) < HD
        tri_in = _tri(tb, "row_le_col")
        tri_ex = _tri(tb, "row_lt_col")
        past = lax.broadcasted_iota(jnp.int32, (tb, tb), 1) < lax.broadcasted_iota(jnp.int32, (tb, tb), 0)

        def qloop(qb, carry):
            qf = _rows(q_ref, qb, tb) * SCALE
            dof = _rows(do_ref, qb, tb)
            qh = _pair_split(qf, m0)
            doh = _pair_split(dof, m0)
            qth = _pair_split(qf.T, mt0)
            doth = _pair_split(dof.T, mt0)
            t1v = _rows(t1_ref, qb, tb)
            tot = (t1v[:, 0:1], t1v[:, 1:2])

            def accumulate(kb, wz, dqs):
                out = []
                for h in range(2):
                    wb, dz = wz[h]
                    dvt_sc[kb] += _dot(doth[h], wb)
                    dkt_sc[kb] += _dot(qth[h], dz)
                    out.append(dqs[h] + _dot(dz, _rows(kb_sc, kb, tb)))
                return tuple(out)

            def block(kb, c, masked):
                hs = range(2)
                runs, dqs, (kb_prev, wz_prev) = c
                z = [_dot(qh[h], kt_sc[kb]) for h in hs]
                dw = [_dot(doh[h], vt_sc[kb]) for h in hs]
                st = []
                for h in hs:
                    lb, l1 = _logsig2(z[h])
                    if masked:
                        l1 = jnp.where(past, l1, 0.0)
                    st.append((lb, _cumsum_mm(l1, tri_in, parts=2)))
                dqs = accumulate(kb_prev, wz_prev, dqs)
                mid = []
                for h in hs:
                    lb, p1 = st[h]
                    w = jnp.exp(lb + (tot[h] - (runs[h][0] + p1)))
                    if masked:
                        w = jnp.where(past, w, 0.0)
                    gm = w * dw[h]
                    mid.append((w.astype(BF16), gm, _cumsum_mm(gm, tri_ex, parts=1)))
                new_runs, wz = [], []
                for h in hs:
                    run1, rung = runs[h]
                    wb, gm, cx = mid[h]
                    dz = gm - (gm + (rung + cx)) * jnp.exp(st[h][0])
                    if masked:
                        dz = jnp.where(past, dz, 0.0)
                    wz.append((wb, dz.astype(BF16)))
                    p1 = st[h][1]
                    new_runs.append((run1 + p1[:, tb - 1:tb], rung + (cx[:, tb - 1:tb] + gm[:, tb - 1:tb])))
                return tuple(new_runs), dqs, (kb, tuple(wz))

            z1 = jnp.zeros((tb, 1), F32)
            zq = jnp.zeros((tb, PAIR), F32)
            zb = jnp.zeros((tb, tb), BF16)
            none = (jnp.int32(0), ((zb, zb), (zb, zb)))
            c = _loop_grouped(qb, lambda i, cc: block(i, cc, False), (((z1, z1), (z1, z1)), (zq, zq), none))
            _, dqs, (kb_last, wz_last) = block(qb, c, True)
            dqs = accumulate(kb_last, wz_last, dqs)
            r0 = pl.multiple_of(qb * tb, tb)
            dq_ref[pl.ds(r0, tb), :] = (_pair_select(m0, dqs[0], dqs[1]) * SCALE).astype(BF16)
            return carry

        lax.fori_loop(0, nb, qloop, 0)
        for kb in range(nb):
            dk_ref[kb * tb:(kb + 1) * tb, :] = dkt_sc[kb].T.astype(BF16)
            dv_ref[kb * tb:(kb + 1) * tb, :] = dvt_sc[kb].T.astype(BF16)
        if comm is not None:
            @pl.when(step == 2 * nbat - 1)
            def _():
                comm["finish"](c_in, c_out, c_sems)

    def col(blk):
        return pl.BlockSpec((S, PAIR), lambda b, p: (b, blk + p))

    sh = jax.ShapeDtypeStruct((T, AW), BF16)
    anyspec = pl.BlockSpec(memory_space=pl.ANY)
    out = pl.pallas_call(
        body, name=f"sb_bwd_{l}",
        grid=(nbat, 2),
        in_specs=[col(SBQ_BLK), col(SBK_BLK), col(SBV_BLK), col(0), col(0)] + c_specs,
        out_specs=[col(0), col(0), col(0)] + [anyspec] * n_co,
        out_shape=[sh, sh, sh] + c_outs,
        scratch_shapes=[pltpu.VMEM((S, PAIR), BF16), pltpu.VMEM((nb, PAIR, tb), BF16), pltpu.VMEM((nb, PAIR, tb), BF16),
                        pltpu.VMEM((nb, PAIR, tb), F32), pltpu.VMEM((nb, PAIR, tb), F32)] + c_scr,
        compiler_params=_cp(("arbitrary", "arbitrary"), VMEM_BIG),
    )(proj, proj, proj, do, t1, *c_args)
    return out[0], out[1], out[2], list(out[3:])


def _foxq_fwd(proj, cum, ck, gqk2, l, S):
    T = proj.shape[0]
    tb = TQ_(S)
    nb = S // tb

    def body(q_ref, k_ref, v_ref, cum_ref, ck_ref, g_ref, o_ref, nl_ref, fk_sc, fkt_sc, vb_sc):
        lane, m0 = _lane_masks()
        p = pl.program_id(1)
        kn, _ = _pair_rms(k_ref[...], m0)
        fk_sc[...] = kn * g_ref[1:2, :]
        _transpose_blocks(fk_sc, fkt_sc, nb, tb)
        vb_sc[...] = v_ref[...].astype(BF16)
        causal = lax.broadcasted_iota(jnp.int32, (tb, tb), 1) <= lax.broadcasted_iota(jnp.int32, (tb, tb), 0)

        def qloop(qb, carry):
            qn, _ = _pair_rms(_rows(q_ref, qb, tb), m0)
            fqh = _pair_split(qn * (g_ref[0:1, :] * SCALE), m0)
            cumv = _rows(cum_ref, qb, tb)
            cq = [_rowsum(jnp.where(lane == 2 * p + h, cumv, 0.0)) for h in range(2)]

            def scores(kb):
                return tuple(_dot(fqh[h], fkt_sc[kb]) for h in range(2))

            def block(kb, kb_next, qk, c, masked):
                st = []
                for h in range(2):
                    s = qk[h] + (cq[h] - ck_ref[h, kb])
                    if masked:
                        s = jnp.where(causal, s, NEG)
                    m2 = jnp.maximum(c[h][0], jnp.max(s, axis=1, keepdims=True))
                    pr = jnp.exp(s - m2)
                    hi = pr.astype(BF16)
                    lo = (pr - hi.astype(F32)).astype(BF16)
                    vv = _rows(vb_sc, kb, tb)
                    st.append((m2, pr, _dot(hi, vv) + _dot(lo, vv)))
                qk_next = scores(kb_next)
                out = []
                for h in range(2):
                    m, lsum, acc = c[h]
                    m2, pr, pv = st[h]
                    al = jnp.exp(m - m2)
                    out.append((m2, al * lsum + _rowsum(pr), al * acc + pv))
                return qk_next, tuple(out)

            zero = (jnp.full((tb, 1), NEG, F32), jnp.zeros((tb, 1), F32), jnp.zeros((tb, PAIR), F32))

            def off_diag(i, sc):
                return block(i, i + 1, sc[0], sc[1], False)

            qk, c = lax.fori_loop(0, qb, off_diag, (scores(0), (zero, zero)))
            _, c = block(qb, qb, qk, c, True)
            r0 = pl.multiple_of(qb * tb, tb)
            o_ref[pl.ds(r0, tb), :] = _pair_select(m0, c[0][2] / c[0][1], c[1][2] / c[1][1])
            nl = [cq[h] - (c[h][0] + jnp.log(c[h][1])) for h in range(2)]
            nl_ref[pl.ds(r0, tb), :] = jnp.where(lane == 0, nl[0], jnp.where(lane == 1, nl[1], 0.0))
            return carry

        lax.fori_loop(0, nb, qloop, 0)

    def col(blk):
        return pl.BlockSpec((S, PAIR), lambda b, p: (b, blk + p))

    return pl.pallas_call(
        body, name=f"fox_fwd_{l}",
        grid=(T // S, 2),
        in_specs=[col(FXQ_BLK), col(FXK_BLK), col(FXV_BLK),
                  pl.BlockSpec((S, 128), lambda b, p: (b, 0)),
                  pl.BlockSpec((None, 2, nb, 1, tb), lambda b, p: (b, p, 0, 0, 0)),
                  pl.BlockSpec((None, 8, PAIR), lambda b, p: (l, 0, 0))],
        out_specs=[col(0), col(0)],
        out_shape=[jax.ShapeDtypeStruct((T, AW), F32)] * 2,
        scratch_shapes=[pltpu.VMEM((S, PAIR), F32), pltpu.VMEM((nb, PAIR, tb), BF16), pltpu.VMEM((S, PAIR), BF16)],
        compiler_params=_cp(("arbitrary", "arbitrary"), VMEM_BIG),
    )(proj, proj, proj, cum, ck, gqk2)


def _foxq_bwd(proj, do, nl, ox, ck, gqk2, l, S):
    T = proj.shape[0]
    tb = TQ_(S)
    nb = S // tb

    def body(q_ref, k_ref, v_ref, do_ref, nl_ref, ox_ref, ck_ref, g_ref,
             dq_ref, dk_ref, dv_ref, dc_ref, wacc_ref, fk_sc, fkt_sc, vt_sc, dfkt_sc, dvt_sc):
        @pl.when((pl.program_id(0) == 0) & (pl.program_id(1) == 0))
        def _():
            wacc_ref[...] = jnp.zeros_like(wacc_ref)

        _, m0 = _lane_masks()
        mt0 = lax.broadcasted_iota(jnp.int32, (PAIR, 1), 0) < HD
        g0 = g_ref[0:1, :]
        g1 = g_ref[1:2, :]
        fk_sc[...] = (_pair_rms(k_ref[...], m0)[0] * g1).astype(BF16)
        _transpose_blocks(fk_sc, fkt_sc, nb, tb)
        _transpose_blocks(v_ref, vt_sc, nb, tb)
        dfkt_sc[...] = jnp.zeros_like(dfkt_sc)
        dvt_sc[...] = jnp.zeros_like(dvt_sc)
        dc_ref[...] = jnp.zeros_like(dc_ref)
        causal = lax.broadcasted_iota(jnp.int32, (tb, tb), 1) <= lax.broadcasted_iota(jnp.int32, (tb, tb), 0)

        def qloop(qb, carry):
            qn, qr = _pair_rms(_rows(q_ref, qb, tb), m0)
            fqf = qn * (g0 * SCALE)
            dof = _rows(do_ref, qb, tb)
            fqh = _pair_split(fqf, m0)
            doh = _pair_split(dof, m0)
            fqth = _pair_split(fqf.T, mt0)
            doth = _pair_split(dof.T, mt0)
            nlv = _rows(nl_ref, qb, tb)
            cql = (nlv[:, 0:1], nlv[:, 1:2])

            def probs(kb, masked):
                qk = [_dot(fqh[h], fkt_sc[kb]) for h in range(2)]
                dp = [_dot(doh[h], vt_sc[kb]) for h in range(2)]
                pr = []
                for h in range(2):
                    e = jnp.exp(qk[h] + (cql[h] - ck_ref[h, kb]))
                    pr.append(jnp.where(causal, e, 0.0) if masked else e)
                return pr, dp

            oxv = _rows(ox_ref, qb, tb)
            dlt = [_rowsum(doh[h].astype(F32) * oxv) for h in range(2)]

            def accumulate(kb, pd, dfqs):
                out = []
                for h in range(2):
                    prb, dsb = pd[h]
                    dvt_sc[kb] += _dot(doth[h], prb)
                    dfkt_sc[kb] += _dot(fqth[h], dsb)
                    out.append(dfqs[h] + _dot(dsb, _rows(fk_sc, kb, tb)))
                return tuple(out)

            def block(kb, c, masked):
                dfqs, (kb_prev, pd_prev) = c
                pr, dp = probs(kb, masked)
                dfqs = accumulate(kb_prev, pd_prev, dfqs)
                pd = []
                for h in range(2):
                    ds = pr[h] * (dp[h] - dlt[h])
                    dc_ref[h, kb] += jnp.broadcast_to(-_colsum(ds), (8, tb))
                    pd.append((pr[h].astype(BF16), ds.astype(BF16)))
                return dfqs, (kb, tuple(pd))

            zq = jnp.zeros((tb, PAIR), F32)
            zb = jnp.zeros((tb, tb), BF16)
            none = (jnp.int32(0), ((zb, zb), (zb, zb)))
            c = _loop_grouped(qb, lambda i, cc: block(i, cc, False), ((zq, zq), none))
            dfqs, (kb_last, pd_last) = block(qb, c, True)
            c = accumulate(kb_last, pd_last, dfqs)
            dfq = _pair_select(m0, c[0], c[1]) * SCALE
            wacc_ref[0:1, :] += _colsum(dfq * qn)
            r0 = pl.multiple_of(qb * tb, tb)
            dq_ref[pl.ds(r0, tb), :] = _pair_rms_bwd(qn, qr, dfq * g0, m0).astype(BF16)
            return carry

        lax.fori_loop(0, nb, qloop, 0)
        for kb in range(nb):
            rows = slice(kb * tb, (kb + 1) * tb)
            dfk = dfkt_sc[kb].T
            knb, krb = _pair_rms(k_ref[rows, :], m0)
            wacc_ref[1:2, :] += _colsum(dfk * knb)
            dk_ref[rows, :] = _pair_rms_bwd(knb, krb, dfk * g1, m0).astype(BF16)
            dv_ref[rows, :] = dvt_sc[kb].T.astype(BF16)

    def col(blk):
        return pl.BlockSpec((S, PAIR), lambda b, p: (b, blk + p))

    sh = jax.ShapeDtypeStruct((T, AW), BF16)
    return pl.pallas_call(
        body, name=f"fox_bwd_{l}",
        grid=(T // S, 2),
        in_specs=[col(FXQ_BLK), col(FXK_BLK), col(FXV_BLK), col(0), col(0), col(0),
                  pl.BlockSpec((None, 2, nb, 1, tb), lambda b, p: (b, p, 0, 0, 0)),
                  pl.BlockSpec((None, 8, PAIR), lambda b, p: (l, 0, 0))],
        out_specs=[col(0), col(0), col(0),
                   pl.BlockSpec((None, 2, nb, 8, tb), lambda b, p: (b, p, 0, 0, 0)),
                   pl.BlockSpec((8, PAIR), lambda b, p: (0, 0))],
        out_shape=[sh, sh, sh,
                   jax.ShapeDtypeStruct((T // S, NH, nb, 8, tb), F32),
                   jax.ShapeDtypeStruct((8, PAIR), F32)],
        scratch_shapes=[pltpu.VMEM((S, PAIR), BF16), pltpu.VMEM((nb, PAIR, tb), BF16), pltpu.VMEM((nb, PAIR, tb), BF16),
                        pltpu.VMEM((nb, PAIR, tb), F32), pltpu.VMEM((nb, PAIR, tb), F32)],
        compiler_params=_cp(("arbitrary", "arbitrary"), VMEM_BIG),
    )(proj, proj, proj, do, nl, ox, ck, gqk2)


def _ada_fwd(c_all, w_ada, b_cols):
    nb, ncol = c_all.shape[0], w_ada.shape[2]
    tn = _tile(ncol, 768)

    def body(c_ref, w_ref, b_ref, o_ref):
        c = c_ref[...]
        ca = (c * _sigmoid(c)).astype(BF16)
        o_ref[...] = _dot(ca, w_ref[...].astype(BF16)) + b_ref[...]

    return pl.pallas_call(
        body, name="ada_fwd",
        grid=(2, ncol // tn),
        in_specs=[pl.BlockSpec((nb, D), lambda l, n: (0, 0)),
                  pl.BlockSpec((None, D, tn), lambda l, n: (l, 0, n)),
                  pl.BlockSpec((None, 1, tn), lambda l, n: (l, 0, n))],
        out_specs=pl.BlockSpec((None, nb, tn), lambda l, n: (l, 0, n)),
        out_shape=jax.ShapeDtypeStruct((2, nb, ncol), F32),
        compiler_params=_cp(("arbitrary", "arbitrary")),
    )(c_all, w_ada, b_cols)


def _ada_bwd(c_all, dmod_cols):
    nb, ncol = c_all.shape[0], dmod_cols.shape[2]
    tn = _tile(ncol, 768)

    def body(c_ref, d_ref, o_ref):
        c = c_ref[...]
        ca = (c * _sigmoid(c)).astype(BF16)
        o_ref[...] = _dot_tn(ca, d_ref[...].astype(BF16))

    return pl.pallas_call(
        body, name="ada_bwd",
        grid=(2, ncol // tn),
        in_specs=[pl.BlockSpec((nb, D), lambda l, n: (0, 0)),
                  pl.BlockSpec((None, nb, tn), lambda l, n: (l, 0, n))],
        out_specs=pl.BlockSpec((None, D, tn), lambda l, n: (l, 0, n)),
        out_shape=jax.ShapeDtypeStruct((2, D, ncol), F32),
        compiler_params=_cp(("arbitrary", "arbitrary")),
    )(c_all, dmod_cols)


def _sum_lead(a, name):
    n, R, C = a.shape
    tr = _tile_div8(R, 256)

    def body(a_ref, o_ref):
        acc = a_ref[0]
        for i in range(1, n):
            acc = acc + a_ref[i]
        o_ref[...] = acc

    return pl.pallas_call(
        body, name=name,
        grid=(R // tr,),
        in_specs=[pl.BlockSpec((n, tr, C), lambda i: (0, i, 0))],
        out_specs=pl.BlockSpec((tr, C), lambda i: (i, 0)),
        out_shape=jax.ShapeDtypeStruct((R, C), F32),
        compiler_params=_cp(("arbitrary",)),
    )(a)


def _adamw(w, g, m, v, name):
    R, C = w.shape
    tr = _tile_div8(R, max(8, (1 << 18) // C))
    c1 = 1.0 / (1.0 - ADAM_B1 ** ADAM_STEP)
    c2 = 1.0 / (1.0 - ADAM_B2 ** ADAM_STEP)

    def body(w_ref, g_ref, m_ref, v_ref, d_ref, mo_ref, vo_ref):
        gg = g_ref[...]
        mn = ADAM_B1 * m_ref[...] + (1.0 - ADAM_B1) * gg
        vn = ADAM_B2 * v_ref[...] + (1.0 - ADAM_B2) * (gg * gg)
        mo_ref[...] = mn
        vo_ref[...] = vn
        d_ref[...] = (-ADAM_LR) * ((mn * c1) / (jnp.sqrt(vn * c2) + ADAM_EPS) + ADAM_WD * w_ref[...])

    spec = pl.BlockSpec((tr, C), lambda i: (i, 0))
    sh = jax.ShapeDtypeStruct((R, C), F32)
    return pl.pallas_call(
        body, name=name, grid=(R // tr,),
        in_specs=[spec] * 4, out_specs=[spec] * 3, out_shape=[sh] * 3,
        compiler_params=_cp(("arbitrary",)),
    )(w, g, m, v)


def _coords():
    return lax.axis_index("x"), lax.axis_index("y"), lax.axis_index("c")


def _all_gather8(blk, name, vmem):
    m_per, n = blk.shape
    space = pltpu.VMEM if vmem else pl.ANY

    def body(x_ref, out_ref, send_sems, recv_sems, local_sem):
        x, y, c = _coords()
        me, sibling = (x, y, c), (x, y, 1 - c)
        chips = [(1 - x, y), (x, 1 - y), (1 - x, 1 - y)]

        def rows(px, py, pc):
            return out_ref.at[4 * px + 2 * py + pc]

        def copy(k, block, to, src=None):
            return pltpu.make_async_remote_copy(
                src_ref=rows(*block) if src is None else src, dst_ref=rows(*block),
                send_sem=send_sems.at[k], recv_sem=recv_sems.at[k], device_id=to, device_id_type=MESH)

        mine = pltpu.make_async_copy(x_ref, rows(*me), local_sem)
        mine.start()
        first = [copy(0, me, sibling, src=x_ref)]
        first += [copy(1 + j, me, (*chip, c), src=x_ref) for j, chip in enumerate(chips)]
        for cp in first:
            cp.start()
        passed = [copy(4 + j, (*chip, c), sibling) for j, chip in enumerate(chips)]
        for j, chip in enumerate(chips):
            copy(1 + j, (*chip, c), me).wait_recv()
            passed[j].start()
        copy(0, sibling, me).wait_recv()
        for j, chip in enumerate(chips):
            copy(4 + j, (*chip, 1 - c), me).wait_recv()
        for cp in first + passed:
            cp.wait_send()
        mine.wait()

    return pl.pallas_call(
        body, name=name,
        out_shape=jax.ShapeDtypeStruct((N_DEV, m_per, n), blk.dtype),
        in_specs=[pl.BlockSpec(memory_space=space)],
        out_specs=pl.BlockSpec(memory_space=space),
        scratch_shapes=[pltpu.SemaphoreType.DMA((7,)), pltpu.SemaphoreType.DMA((7,)), pltpu.SemaphoreType.DMA],
        compiler_params=pltpu.CompilerParams(vmem_limit_bytes=VMEM_BIG if vmem else None),
    )(blk)


def _relay(comm, parts):
    if "relay" in comm:
        comm["relay"](*parts)


def _run_comm(comm, name):
    n_in, n_out = len(comm["args"]), len(comm["out_shapes"])

    def body(*refs):
        parts = (refs[:n_in], refs[n_in:n_in + n_out], refs[n_in + n_out:])
        comm["start"](*parts)
        _relay(comm, parts)
        comm["finish"](*parts)

    anyspec = pl.BlockSpec(memory_space=pl.ANY)
    return pl.pallas_call(
        body, name=name, out_shape=comm["out_shapes"],
        in_specs=[anyspec] * n_in, out_specs=[anyspec] * n_out, scratch_shapes=comm["scratch"],
    )(*comm["args"])


def _hosted(comm):
    if comm is None:
        return [], [], [], []
    anyspec = pl.BlockSpec(memory_space=pl.ANY)
    return list(comm["args"]), [anyspec] * len(comm["args"]), list(comm["out_shapes"]), list(comm["scratch"])


def _ag_comm(wshards, pieces):
    n_piece = len(pieces)
    halves = [wshards[i].shape[len(lead)] // 2 for i, lead, _ in pieces]

    def plan(ins, outs, sems):
        send_sems, recv_sems, local_sems = sems
        x, y, c = _coords()
        me, sibling = (x, y, c), (x, y, 1 - c)
        chips = [(1 - x, y), (x, 1 - y), (1 - x, 1 - y)]

        def dsts(px, py, pc):
            s = 2 * px + py
            return [outs[p].at[s, pl.ds(pc * r2, r2)] if stacked else outs[p].at[pl.ds((2 * s + pc) * r2, r2)]
                    for p, ((_, _, stacked), r2) in enumerate(zip(pieces, halves))]

        srcs = [ins[i].at[(*lead, pl.ds(c * r2, r2))] for (i, lead, _), r2 in zip(pieces, halves)]

        def copies(k, block, to, own=False):
            d = dsts(*block)
            return [pltpu.make_async_remote_copy(
                src_ref=srcs[p] if own else d[p], dst_ref=d[p], send_sem=send_sems.at[k, p],
                recv_sem=recv_sems.at[k, p], device_id=to, device_id_type=MESH) for p in range(n_piece)]

        mine = [pltpu.make_async_copy(srcs[p], d, local_sems.at[p]) for p, d in enumerate(dsts(*me))]
        first = copies(0, me, sibling, own=True)
        for j, chip in enumerate(chips):
            first += copies(1 + j, me, (*chip, c), own=True)
        return me, sibling, chips, c, copies, mine, first

    def start(ins, outs, sems):
        *_, mine, first = plan(ins, outs, sems)
        for cp in mine + first:
            cp.start()

    def relay(ins, outs, sems):
        me, sibling, chips, c, copies, _, _ = plan(ins, outs, sems)
        for j, chip in enumerate(chips):
            for cp in copies(1 + j, (*chip, c), me):
                cp.wait_recv()
            for cp in copies(4 + j, (*chip, c), sibling):
                cp.start()

    def finish(ins, outs, sems):
        me, sibling, chips, c, copies, mine, first = plan(ins, outs, sems)
        passed = [cp for j, chip in enumerate(chips) for cp in copies(4 + j, (*chip, c), sibling)]
        for cp in copies(0, sibling, me):
            cp.wait_recv()
        for j, chip in enumerate(chips):
            for cp in copies(4 + j, (*chip, 1 - c), me):
                cp.wait_recv()
        for cp in first + passed:
            cp.wait_send()
        for cp in mine:
            cp.wait()

    out_shapes = []
    for (i, lead, stacked), r2 in zip(pieces, halves):
        cols = wshards[i].shape[-1]
        out_shapes.append(jax.ShapeDtypeStruct((N_SHARD, 2 * r2, cols) if stacked else (N_SHARD * 2 * r2, cols), BF16))
    return dict(
        args=list(wshards), out_shapes=out_shapes,
        scratch=[pltpu.SemaphoreType.DMA((7, n_piece)), pltpu.SemaphoreType.DMA((7, n_piece)),
                 pltpu.SemaphoreType.DMA((n_piece,))],
        start=start, relay=relay, finish=finish)


def _rs_to_chips_comm(hs):
    n = len(hs)

    def copies(h, r, sems):
        send_sems, recv_sems = sems
        x, y, c = _coords()
        chips = [(1 - x, y), (x, 1 - y), (1 - x, 1 - y)]
        return [pltpu.make_async_remote_copy(
            src_ref=h[p].at[2 * px + py], dst_ref=r[p].at[k], send_sem=send_sems.at[k, p], recv_sem=recv_sems.at[k, p],
            device_id=(px, py, c), device_id_type=MESH) for k, (px, py) in enumerate(chips) for p in range(n)]

    def start(h, r, sems):
        for cp in copies(h, r, sems):
            cp.start()

    def finish(h, r, sems):
        for cp in copies(h, r, sems):
            cp.wait()

    return dict(args=list(hs), out_shapes=[jax.ShapeDtypeStruct((3,) + h.shape[1:], h.dtype) for h in hs],
                scratch=[pltpu.SemaphoreType.DMA((3, n)), pltpu.SemaphoreType.DMA((3, n))],
                start=start, finish=finish)


def _rs_to_sibling(pieces, name):
    n = len(pieces)

    def body(*refs):
        g, r, (send_sems, recv_sems) = refs[:n], refs[n:2 * n], refs[2 * n:]
        x, y, c = _coords()
        cps = []
        for p in range(n):
            r2 = g[p].shape[1] // 2
            cps.append(pltpu.make_async_remote_copy(
                src_ref=g[p].at[:, pl.ds((1 - c) * r2, r2)], dst_ref=r[p], send_sem=send_sems.at[p],
                recv_sem=recv_sems.at[p], device_id=(x, y, 1 - c), device_id_type=MESH))
        for cp in cps:
            cp.start()
        for cp in cps:
            cp.wait()

    anyspec = pl.BlockSpec(memory_space=pl.ANY)
    return pl.pallas_call(
        body, name=name,
        out_shape=[jax.ShapeDtypeStruct((N_SHARD, g.shape[1] // 2, g.shape[2]), g.dtype) for g in pieces],
        in_specs=[anyspec] * n, out_specs=[anyspec] * n,
        scratch_shapes=[pltpu.SemaphoreType.DMA((n,)), pltpu.SemaphoreType.DMA((n,))],
    )(*pieces)


def _share_halves(tensors, places, r2s):
    n, no = len(places), len(tensors)

    def body(*refs):
        o, (send_sems, recv_sems) = refs[no:2 * no], refs[2 * no:]
        x, y, c = _coords()

        def half(p, hc):
            oi, lead = places[p]
            return o[oi].at[(*lead, pl.ds(hc * r2s[p], r2s[p]))]

        outs = [pltpu.make_async_remote_copy(
            src_ref=half(p, c), dst_ref=half(p, c), send_sem=send_sems.at[p], recv_sem=recv_sems.at[p],
            device_id=(x, y, 1 - c), device_id_type=MESH) for p in range(n)]
        for cp in outs:
            cp.start()
        for p in range(n):
            pltpu.make_async_remote_copy(
                src_ref=half(p, 1 - c), dst_ref=half(p, 1 - c), send_sem=send_sems.at[p], recv_sem=recv_sems.at[p],
                device_id=(x, y, 1 - c), device_id_type=MESH).wait_recv()
        for cp in outs:
            cp.wait_send()

    anyspec = pl.BlockSpec(memory_space=pl.ANY)
    return pl.pallas_call(
        body, name="share_halves",
        out_shape=[jax.ShapeDtypeStruct(t.shape, t.dtype) for t in tensors],
        in_specs=[anyspec] * no, out_specs=[anyspec] * no,
        input_output_aliases={i: i for i in range(no)},
        scratch_shapes=[pltpu.SemaphoreType.DMA((n,)), pltpu.SemaphoreType.DMA((n,))],
    )(*tensors)


def _add_rows(r2, cols, n_arrays):
    lanes = -(-cols // 128) * 128
    return _tile_div8(r2, max(16, (24 << 20) // (2 * n_arrays * lanes * 4)), mult=16)


def _add_sibling(pieces, recvs, cidx, name):
    n = len(pieces)
    _, R, C = pieces[0].shape
    r2 = R // 2
    tr = _add_rows(r2, C, 2 * n)
    nt = r2 // tr

    def body(c_ref, *refs):
        for p in range(n):
            refs[2 * n + p][...] = (refs[p][...] + refs[n + p][...].astype(F32)).astype(BF16)

    return pl.pallas_call(
        body, name=name,
        grid_spec=pltpu.PrefetchScalarGridSpec(
            num_scalar_prefetch=1, grid=(N_SHARD, nt),
            in_specs=[pl.BlockSpec((None, tr, C), lambda s, i, c_ref: (s, c_ref[0] * nt + i, 0))] * n
            + [pl.BlockSpec((None, tr, C), lambda s, i, c_ref: (s, i, 0))] * n,
            out_specs=[pl.BlockSpec((None, tr, C), lambda s, i, c_ref: (s, i, 0))] * n),
        out_shape=[jax.ShapeDtypeStruct((N_SHARD, r2, C), BF16)] * n,
        compiler_params=_cp(("arbitrary", "arbitrary"), VMEM_BIG),
    )(cidx, *pieces, *recvs)


def _add_chips_into(piece, recv_a, recv_b, sc, prev, shape, lead, name):
    _, R, C = piece.shape
    r2 = R // 2
    tr = _add_rows(r2, C, 4)
    nt = r2 // tr
    nl = len(lead)

    def body(sc_ref, p_ref, a_ref, b_ref, *rest):
        o_ref = rest[-1]
        acc = p_ref[...] + a_ref[...].astype(F32)
        for k in range(3):
            acc = acc + b_ref[k].astype(F32)
        o_ref[...] = acc

    in_specs = [pl.BlockSpec((None, tr, C), lambda i, sc_ref: (sc_ref[0], sc_ref[1] * nt + i, 0)),
                pl.BlockSpec((None, tr, C), lambda i, sc_ref: (sc_ref[0], i, 0)),
                pl.BlockSpec((3, tr, C), lambda i, sc_ref: (0, i, 0))]
    args = [sc, piece, recv_a, recv_b]
    aliases = {}
    if prev is not None:
        in_specs.append(pl.BlockSpec(memory_space=pl.ANY))
        args.append(prev)
        aliases = {4: 0}
    return pl.pallas_call(
        body, name=name,
        grid_spec=pltpu.PrefetchScalarGridSpec(
            num_scalar_prefetch=1, grid=(nt,), in_specs=in_specs,
            out_specs=pl.BlockSpec((None,) * nl + (tr, C), lambda i, sc_ref: (*lead, sc_ref[1] * nt + i, 0))),
        out_shape=jax.ShapeDtypeStruct(shape, F32),
        input_output_aliases=aliases,
        compiler_params=_cp(("arbitrary",), VMEM_BIG),
    )(*args)


def _pack_rows(parts, rows, dtype):
    flat = jnp.concatenate([p.reshape(-1).astype(dtype) for p in parts])
    return jnp.pad(flat, (0, rows * ROW - flat.shape[0])).reshape(rows, ROW)


def _unpack(flat, shapes):
    out, off = [], 0
    for sh in shapes:
        n = math.prod(sh)
        out.append(flat[off:off + n].reshape(sh))
        off += n
    return out


def _block_diag(w):
    eye = jnp.eye(LW // HD, dtype=w.dtype)
    return jnp.einsum("lhij,hg->lhigj", w, eye).reshape(w.shape[0], LW, LW)


def _diag_blocks(w):
    nbk = LW // HD
    w4 = w.reshape(nbk, HD, nbk, HD)
    return jnp.stack([w4[h, :, h, :] for h in range(nbk)])


def _rows8(rows, width):
    z = jnp.zeros((width,), F32)
    return jnp.stack(list(rows) + [z] * (8 - len(rows)))


def kernel(x, c, w_ada, b_ada, g_norm, w_ffn_up, w_ffn_down, w_in, b_fgate, conv_w, conv_b, w_rgate, b_rgate, w_igate, b_igate, lru_lambda, g_qk, g_mix_out, w_out, loss_target, m_w_ada, m_b_ada, m_g_norm, m_w_ffn_up, m_w_ffn_down, m_w_in, m_b_fgate, m_conv_w, m_conv_b, m_w_rgate, m_b_rgate, m_w_igate, m_b_igate, m_lru_lambda, m_g_qk, m_g_mix_out, m_w_out, v_w_ada, v_b_ada, v_g_norm, v_w_ffn_up, v_w_ffn_down, v_w_in, v_b_fgate, v_conv_w, v_conv_b, v_w_rgate, v_b_rgate, v_w_igate, v_b_igate, v_lru_lambda, v_g_qk, v_g_mix_out, v_w_out):
    B, S, _ = x.shape
    T = B * S
    xi, yi, ci = _coords()
    sidx = 2 * xi + yi
    didx = 4 * xi + 2 * yi + ci
    ada_cols = w_ada.shape[2]
    gn_cols = g_norm.shape[2]
    cw_cols = conv_w.shape[2]
    n_all = B * N_DEV

    blk1 = _pack_rows([c, jnp.pad(g_norm.reshape(-1), (0, 2 * ROW - g_norm.size)), conv_w], 8, F32)
    ag1 = _all_gather8(blk1, "ag_small_in", True)
    c_all = ag1[:, 0:B].reshape(n_all, D)
    chip_rows = ag1[0::2]
    g_norm_full = chip_rows[:, 2:4].reshape(N_SHARD, 2 * ROW)[:, :g_norm.size] \
        .reshape(N_SHARD, 2, 3, gn_cols).transpose(1, 2, 0, 3).reshape(2, 3, D)
    conv_w_full = chip_rows[:, 4].reshape(N_SHARD, 2, 4, cw_cols).transpose(1, 2, 0, 3).reshape(2, 4, LW)

    b_cols = lax.dynamic_slice(b_ada, (0, sidx * ada_cols), (2, ada_cols)).reshape(2, 1, ada_cols)
    mod_cols = _ada_fwd(c_all, w_ada, b_cols)
    mrows = (2 * n_all * ada_cols) // ROW
    ag2 = _all_gather8(mod_cols.reshape(mrows, ROW), "ag_mod", True)
    mod_sh = ag2[0::2].reshape(N_SHARD, 2, n_all, ada_cols)
    mod_me = lax.dynamic_slice(mod_sh, (0, 0, didx * B, 0), (N_SHARD, 2, B, ada_cols))
    mod_me = mod_me.transpose(1, 2, 0, 3).reshape(2, B, 3, 3, D)
    zrow = jnp.zeros((B, D), F32)
    mods = [[jnp.stack([mod_me[l, :, j, 0], 1.0 + mod_me[l, :, j, 1], 1.0 + mod_me[l, :, j, 2],
                        jnp.broadcast_to(g_norm_full[l, j], (B, D)), zrow, zrow, zrow, zrow], axis=1)
             for j in range(3)] for l in range(2)]

    wshards = (w_ffn_up.astype(BF16), w_ffn_down.astype(BF16), w_in.astype(BF16), w_out.astype(BF16))

    def ffn_pieces(l, j):
        return [(0, (l, j), True), (1, (l, j), False)]

    def mixer_pieces(l):
        return [(2, (l,), True), (3, (l,), False)]

    def ffn_weights(up, dn):
        return dict(up=up, dn=dn)

    def mixer_weights(g_in, g_out):
        return dict(inp=jnp.pad(g_in.transpose(1, 0, 2).reshape(D, N_IN), ((0, 0), (0, N_INP - N_IN))), out=g_out)

    wl = [dict(), dict()]
    wl[0][0] = ffn_weights(*_run_comm(_ag_comm(wshards, ffn_pieces(0, 0)), "ag_weights_0_0"))

    wr_d = _block_diag(w_rgate).astype(BF16)
    wi_d = _block_diag(w_igate).astype(BF16)
    cw8 = jnp.pad(conv_w_full, ((0, 0), (0, 4), (0, 0)))
    vp8 = jnp.stack([_rows8([conv_b[l], b_rgate[l], b_igate[l], lru_lambda[l]], LW) for l in range(2)])
    bfp = jnp.pad(b_fgate, ((0, 0), (0, 128 - NH)))[:, None, :] * jnp.ones((1, 8, 1), F32)
    gqk2 = jnp.tile(jnp.pad(g_qk, ((0, 0), (0, 6), (0, 0))), (1, 1, 2))
    gmix8 = jnp.pad(g_mix_out[:, None, :], ((0, 0), (0, 7), (0, 0)))

    x2 = x.reshape(T, D)
    tgt = loss_target.reshape(T, D)

    saved = []
    xc = x2
    for l in range(2):
        sv = {}
        sv["x0"] = xc
        w = wl[l]
        rest0 = _ag_comm(wshards, mixer_pieces(0) + ffn_pieces(0, 1)) if l == 0 else None
        xc, sv["g0"], sv["u0"], sv["f0"], got = _ffn_fwd(xc, mods[l][0], w[0]["up"], w[0]["dn"], l, 0, S, rest0)
        if l == 0:
            w["mix"] = mixer_weights(got[0], got[1])
            w[1] = ffn_weights(got[2], got[3])
        sv["x1"] = xc
        sv["h1"], proj = _mix_in_fwd(xc, mods[l][1], w["mix"]["inp"], l, S)
        sv["proj"] = proj
        sv["ylru"], sv["hl"] = _lru_fwd(proj, cw8, vp8, wr_d, wi_d, l, S)
        all1 = _ag_comm(wshards, ffn_pieces(1, 0) + mixer_pieces(1) + ffn_pieces(1, 1)) if l == 0 else None
        sv["osb"], sv["t1"], got = _sbq_fwd(proj, l, S, all1)
        if l == 0:
            wl[1][0] = ffn_weights(got[0], got[1])
            wl[1]["mix"] = mixer_weights(got[2], got[3])
            wl[1][1] = ffn_weights(got[4], got[5])
        cum = _fgate_fwd(proj, bfp, l, S)
        sv["ck"] = cum[:, :NH].reshape(B, S, NH).transpose(0, 2, 1).reshape(B, NH, S // TQ_(S), 1, TQ_(S))
        sv["ofx"], sv["nl"] = _foxq_fwd(proj, cum, sv["ck"], gqk2, l, S)
        xc, sv["y"], sv["mo"] = _mix_out_fwd(xc, sv["ylru"], sv["osb"], sv["ofx"], mods[l][1], gmix8, w["mix"]["out"], l, S)
        sv["x2"] = xc
        xc, sv["g2"], sv["u2"], sv["f2"], _ = _ffn_fwd(xc, mods[l][2], w[1]["up"], w[1]["dn"], l, 1, S)
        saved.append(sv)

    dxc, lpart = _loss_head(xc, tgt, S)
    loss = lax.psum(lpart[0, 0], ("x", "y", "c"))

    tf = wl[0][0]["up"].shape[-1]
    g_up_l = [[None, None], [None, None]]
    g_dn_l = [[None, None], [None, None]]
    g_in_l, g_out_l = [None, None], [None, None]
    dmods = [[None] * 3 for _ in range(2)]
    small = [dict() for _ in range(2)]
    cvec = jnp.reshape(ci, (1,)).astype(jnp.int32)
    scvec = jnp.stack([sidx, ci]).astype(jnp.int32)

    def ffn_groups(l, j):
        return [(0, "up", [g_up_l[l][j]], [(l, j)]), (1, "dn", [g_dn_l[l][j]], [(l, j)])]

    def mixer_groups(l):
        return [(2, "in", [g_in_l[l]], [(l,)]), (3, "out", [g_out_l[l]], [(l,)])]

    def rs_sibling_phase(groups, tag):
        recv_a = _rs_to_sibling([pb for _, _, ps, _ in groups for _, pb in ps], f"rs_to_sibling_{tag}")
        hs, off = [], 0
        for _, gname, ps, leads in groups:
            hs += _add_sibling([pf for pf, _ in ps], recv_a[off:off + len(ps)], cvec,
                               f"rs_add_sibling_{gname}_{'_'.join(map(str, leads[0]))}")
            off += len(ps)
        return groups, recv_a, hs

    def ffn_back(l, j, xin, dy, sv, sub, comm=None):
        dx, dmod, wacc, hb, dfb, ab, dgub, got = _ffn_bwd(
            xin, dy, mods[l][sub], sv[f"f{sub}"], sv[f"g{sub}"], sv[f"u{sub}"],
            wl[l][j]["up"], wl[l][j]["dn"], l, j, S, comm)
        g_up_l[l][j] = _mm_tn(hb, dgub, f"dw_up_{l}_{j}", tnb=tf, split_n=True, with_bf16=True)
        g_dn_l[l][j] = tuple(g.reshape(N_SHARD, -1, D)
                             for g in _mm_tn(ab, dfb, f"dw_dn_{l}_{j}", tma=tf, with_bf16=True))
        dmods[l][sub] = dmod
        small[l][f"gn{sub}"] = wacc[0]
        return dx, got

    batches = []
    for l in (1, 0):
        sv = saved[l]
        dxc, _ = ffn_back(l, 1, sv["x2"], dxc, sv, 2)
        dyl, dsb, dfx, dmo, dmod1, wacc_mo = _mix_out_bwd(
            dxc, sv["ylru"], sv["osb"], sv["ofx"], sv["mo"], mods[l][1], gmix8, wl[l]["mix"]["out"], l, S)
        small[l]["gmix"] = wacc_mo[0]
        g_out_l[l] = tuple(g.reshape(N_SHARD, -1, D) for g in _mm_tn(sv["y"], dmo, f"dw_out_{l}", with_bf16=True))
        dsq, dsk, dsv, got = _sbq_bwd(sv["proj"], dsb, sv["t1"], l, S,
                                       _rs_to_chips_comm(rs1[2]) if l == 0 else None)
        if l == 0:
            batches.append((rs1[0], rs1[1], got))
        dfq, dfk, dfv, dck, wacc_fx = _foxq_bwd(sv["proj"], dfx, sv["nl"], sv["ofx"], sv["ck"], gqk2, l, S)
        small[l]["gqk"] = wacc_fx[0:2, :HD] + wacc_fx[0:2, HD:]
        dcum = dck[:, :, :, 0, :].reshape(B, NH, S).transpose(0, 2, 1).reshape(T, NH)
        dff_, wacc_fg = _fgate_bwd(jnp.pad(dcum, ((0, 0), (0, 128 - NH))), sv["proj"], bfp, l, S)
        small[l]["bf"] = wacc_fg[0, :NH]
        dlx, dlg, dpr, dpi, ub, wacc_lru = _lru_bwd(dyl, sv["proj"], sv["hl"], cw8, vp8, wr_d, wi_d, l, S)
        small[l]["lru"] = wacc_lru
        small[l]["wr"] = _diag_blocks(_mm_tn(ub, dpr, f"dw_rgate_{l}"))
        small[l]["wi"] = _diag_blocks(_mm_tn(ub, dpi, f"dw_igate_{l}"))
        dproj = jnp.concatenate(
            [dlx, dlg, dsq, dsk, dsv, dfq, dfk, dfv, dff_], axis=1)
        g_in = _mm_tn(sv["h1"], dproj, f"dw_in_{l}", tnb=N_INP // 3)[:, :N_IN]
        g_in = g_in.reshape(D, N_SHARD, -1).transpose(1, 0, 2)
        g_in_l[l] = (g_in, g_in.astype(BF16))
        dxc, dmod_in, wacc_in = _mix_in_bwd(sv["x1"], dxc, mods[l][1], dproj, wl[l]["mix"]["inp"], l, S)
        dmods[l][1] = dmod_in + dmod1
        small[l]["gn1"] = wacc_in[0]
        if l == 1:
            dxc, _ = ffn_back(l, 0, sv["x0"], dxc, sv, 0)
            rs1 = rs_sibling_phase(ffn_groups(1, 0) + mixer_groups(1) + ffn_groups(1, 1), "1")
        else:
            late = rs_sibling_phase(mixer_groups(0) + ffn_groups(0, 1), "0_late")
            dxc, got = ffn_back(l, 0, sv["x0"], dxc, sv, 0, _rs_to_chips_comm(late[2]))
            batches.append((late[0], late[1], got))
    grad_x = dxc.reshape(B, S, D)

    dmod_loc = jnp.stack([jnp.stack([dmods[l][j][:, 0:3, :] for j in range(3)], axis=1) for l in range(2)])
    drows = 2 * B * 9
    blk3 = _pack_rows([dmod_loc], -(-drows // 8) * 8, F32)
    ag3 = _all_gather8(blk3, "ag_dmod", True)
    dmod_all = ag3[:, :drows].reshape(N_DEV, 2, B, 9 * D).transpose(1, 0, 2, 3).reshape(2, n_all, 9 * D)
    dmod_mine = lax.dynamic_slice(dmod_all, (0, 0, sidx * ada_cols), (2, n_all, ada_cols))
    grad_w_ada = _ada_bwd(c_all, dmod_mine)
    dmod_rows = jnp.pad(dmod_all.transpose(1, 0, 2).reshape(n_all, 2 * 9, D), ((0, 0), (0, 6), (0, 0)))
    grad_b_ada = _sum_lead(dmod_rows, "grad_b_ada")[:2 * 9].reshape(2, 9 * D)

    sm_parts = [
        jnp.stack([small[l]["bf"] for l in range(2)]),
        jnp.stack([small[l]["lru"][4] for l in range(2)]),
        jnp.stack([small[l]["wr"] for l in range(2)]),
        jnp.stack([small[l]["lru"][5] for l in range(2)]),
        jnp.stack([small[l]["wi"] for l in range(2)]),
        jnp.stack([small[l]["lru"][6] for l in range(2)]),
        jnp.stack([small[l]["lru"][7] for l in range(2)]),
        jnp.stack([small[l]["gqk"] for l in range(2)]),
        jnp.stack([small[l]["gmix"] for l in range(2)]),
        jnp.stack([jnp.stack([small[l][f"gn{j}"] for j in range(3)]) for l in range(2)]),
        jnp.stack([small[l]["lru"][0:4] for l in range(2)]),
    ]
    sm_shapes = [p.shape for p in sm_parts]
    sm_rows = -(-sum(p.size for p in sm_parts) // (8 * ROW)) * 8
    ag4 = _all_gather8(_pack_rows(sm_parts, sm_rows, F32), "ag_small_grads", True)
    sm_sum = _sum_lead(ag4, "sum_small_grads").reshape(-1)
    (g_bf, g_cb, g_wr, g_br, g_wi, g_bi, g_lam, g_gqk, g_gmix, g_gn_full, g_cw_full) = _unpack(sm_sum, sm_shapes)
    g_gn = lax.dynamic_slice(g_gn_full, (0, 0, sidx * gn_cols), (2, 3, gn_cols))
    g_cw = lax.dynamic_slice(g_cw_full, (0, 0, sidx * cw_cols), (2, 4, cw_cols))

    last = rs_sibling_phase(ffn_groups(0, 0), "0_first")
    batches.append((last[0], last[1], _run_comm(_rs_to_chips_comm(last[2]), "rs_to_chips_0_first")))
    shapes4 = [w_ffn_up.shape, w_ffn_down.shape, w_in.shape, w_out.shape]
    tensors, places, r2s = [None] * 4, [], []
    for groups, recv_a, recv_b in batches:
        k = 0
        for gi, gname, ps, leads in groups:
            for (pf, _), lead in zip(ps, leads):
                tensors[gi] = _add_chips_into(pf, recv_a[k], recv_b[k], scvec, tensors[gi], shapes4[gi], lead,
                                              f"rs_add_chips_{gname}_{'_'.join(map(str, lead))}")
                places.append((gi, lead))
                r2s.append(pf.shape[1] // 2)
                k += 1
    gw_up, gw_dn, gw_in, gw_out = _share_halves(tensors, places, r2s)

    def upd(w, g, m, v, name):
        sh = w.shape
        two = (w.size // sh[-1], sh[-1])
        dlt, mn, vn = _adamw(w.reshape(two), g.reshape(two), m.reshape(two), v.reshape(two), name)
        return dlt.reshape(sh), mn.reshape(sh), vn.reshape(sh)

    big = {
        "w_ada": (w_ada, grad_w_ada, m_w_ada, v_w_ada),
        "w_ffn_up": (w_ffn_up, gw_up, m_w_ffn_up, v_w_ffn_up),
        "w_ffn_down": (w_ffn_down, gw_dn, m_w_ffn_down, v_w_ffn_down),
        "w_in": (w_in, gw_in, m_w_in, v_w_in),
        "w_out": (w_out, gw_out, m_w_out, v_w_out),
    }
    res = {n: (t[1],) + upd(*t, f"adamw_{n}") for n, t in big.items()}

    smalls = {
        "b_ada": (b_ada, grad_b_ada, m_b_ada, v_b_ada),
        "g_norm": (g_norm, g_gn, m_g_norm, v_g_norm),
        "b_fgate": (b_fgate, g_bf, m_b_fgate, v_b_fgate),
        "conv_w": (conv_w, g_cw, m_conv_w, v_conv_w),
        "conv_b": (conv_b, g_cb, m_conv_b, v_conv_b),
        "w_rgate": (w_rgate, g_wr, m_w_rgate, v_w_rgate),
        "b_rgate": (b_rgate, g_br, m_b_rgate, v_b_rgate),
        "w_igate": (w_igate, g_wi, m_w_igate, v_w_igate),
        "b_igate": (b_igate, g_bi, m_b_igate, v_b_igate),
        "lru_lambda": (lru_lambda, g_lam, m_lru_lambda, v_lru_lambda),
        "g_qk": (g_qk, g_gqk, m_g_qk, v_g_qk),
        "g_mix_out": (g_mix_out, g_gmix, m_g_mix_out, v_g_mix_out),
    }
    names = list(smalls)
    shapes = [smalls[n][0].shape for n in names]
    prow = -(-sum(math.prod(s) for s in shapes) // (8 * ROW)) * 8
    packed = [_pack_rows([smalls[n][i].reshape(shapes[k]) for k, n in enumerate(names)], prow, F32) for i in range(4)]
    outs = _adamw(packed[0], packed[1], packed[2], packed[3], "adamw_small")
    un = [_unpack(o.reshape(-1), shapes) for o in outs]
    for k, n in enumerate(names):
        res[n] = (smalls[n][1].reshape(shapes[k]), un[0][k], un[1][k], un[2][k])

    order = ["w_ada", "b_ada", "g_norm", "w_ffn_up", "w_ffn_down", "w_in", "b_fgate", "conv_w", "conv_b",
             "w_rgate", "b_rgate", "w_igate", "b_igate", "lru_lambda", "g_qk", "g_mix_out", "w_out"]
    return (loss, grad_x, *[res[n][0] for n in order], *[res[n][1] for n in order],
            *[res[n][2] for n in order], *[res[n][3] for n in order])


def TQ_(S):
    return min(TQ, S)
```

```python
import math

import jax
import jax.numpy as jnp
from jax import lax
from jax.experimental import pallas as pl
from jax.experimental.pallas import tpu as pltpu

F32 = jnp.float32
BF16 = jnp.bfloat16
MESH = pl.DeviceIdType.MESH

D = 1024
HD = 64
LW = 512
NH = 4
AW = NH * HD
N_IN = 2564
N_INP = 2688
F_BLK = 2560 // 128
EPS = 1e-6
LRU_C = 8.0
SCALE = HD ** -0.5
NEG = -1e30
TQ = 256

ADAM_LR, ADAM_B1, ADAM_B2, ADAM_EPS, ADAM_WD, ADAM_STEP = 0.001, 0.9, 0.999, 1e-08, 0.01, 10

VMEM_BIG = 56 * 1024 * 1024
N_DEV = 8
N_SHARD = 4
ROW = 1024


def _cp(sem, vmem=None):
    return pltpu.CompilerParams(dimension_semantics=sem, vmem_limit_bytes=vmem)


def _dot(a, b):
    return jnp.dot(a, b, preferred_element_type=F32)


def _dot_nt(a, b):
    return lax.dot_general(a, b, (((1,), (1,)), ((), ())), preferred_element_type=F32)


def _dot_tn(a, b):
    return lax.dot_general(a, b, (((0,), (0,)), ((), ())), preferred_element_type=F32)


def _log1p(e):
    small = e * (1.0 - e * (0.5 - e * (1.0 / 3.0 - e * 0.25)))
    return jnp.where(e < 0.01, small, jnp.log(1.0 + e))


def _expm1_neg(x):
    small = x * (1.0 + x * 0.5 * (1.0 + x * (1.0 / 3.0) * (1.0 + x * 0.25 * (1.0 + x * 0.2))))
    return jnp.where(x > -0.05, small, jnp.exp(x) - 1.0)


def _sigmoid(x):
    return 1.0 / (1.0 + jnp.exp(-x))


_GELU_C = math.sqrt(2.0 / math.pi)


def _gelu_and_grad(x):
    x2 = x * x
    th = jnp.tanh(_GELU_C * (x + 0.044715 * x * x2))
    g = 0.5 * x * (1.0 + th)
    dg = 0.5 * (1.0 + th) + 0.5 * x * (1.0 - th * th) * _GELU_C * (1.0 + 3.0 * 0.044715 * x2)
    return g, dg


def _rms_rows(x):
    rstd = lax.rsqrt(jnp.mean(x * x, axis=-1, keepdims=True) + EPS)
    return x * rstd, rstd


def _rms_bwd(xn, rstd, dyn):
    return rstd * (dyn - xn * jnp.mean(dyn * xn, axis=-1, keepdims=True))


def _colsum(x):
    return jnp.sum(x, axis=0, keepdims=True)


def _rowsum(x):
    return jnp.sum(x, axis=1, keepdims=True)


def _split3(x):
    hi = x.astype(BF16)
    r = x - hi.astype(F32)
    mid = r.astype(BF16)
    lo = (r - mid.astype(F32)).astype(BF16)
    return hi, mid, lo


def _cumsum_mm(x, ones_tri, parts=3):
    ps = _split3(x)[:parts]
    acc = _dot(ps[0], ones_tri)
    for p in ps[1:]:
        acc = acc + _dot(p, ones_tri)
    return acc


def _tri(n, kind):
    r = lax.broadcasted_iota(jnp.int32, (n, n), 0)
    c = lax.broadcasted_iota(jnp.int32, (n, n), 1)
    m = {"row_gt_col": r > c, "row_le_col": r <= c, "row_lt_col": r < c}[kind]
    return jnp.where(m, 1.0, 0.0).astype(BF16)


def _normmod(x, mod_ref):
    xn, rstd = _rms_rows(x)
    h = xn * mod_ref[3:4, :] * mod_ref[1:2, :] + mod_ref[0:1, :]
    return h, xn, rstd


def _normmod_bwd(dh, xn, rstd, mod_ref, dmod_ref, wacc_ref):
    gn = mod_ref[3:4, :]
    sc = mod_ref[1:2, :]
    dmod_ref[0:1, :] += _colsum(dh)
    t = _colsum(dh * xn)
    dmod_ref[1:2, :] += t * gn
    wacc_ref[0:1, :] += t * sc
    return _rms_bwd(xn, rstd, dh * (gn * sc))


def _tile(n, want):
    t = min(n, want)
    while n % t:
        t //= 2
    return t


def _tile_div8(n, cap, mult=8):
    best = mult
    for t in range(mult, min(n, cap) + 1, mult):
        if n % t == 0:
            best = t
    assert n % best == 0
    return best


def _ffn_fwd(x, mod, wup, wdn, l, j, S, comm=None):
    T = x.shape[0]
    tf = wup.shape[-1]
    nk = 2
    tm = _tile(S, 512)
    tpb = S // tm
    nt = T // tm
    c_args, c_specs, c_outs, c_scr = _hosted(comm)
    n_ci, n_co = len(c_args), len(c_outs)

    def body(*refs):
        x_ref, mod_ref, wg_ref, wu_ref, wd_ref = refs[:5]
        c_in = refs[5:5 + n_ci]
        xo_ref, g_ref, u_ref, f_ref = refs[5 + n_ci:9 + n_ci]
        c_out = refs[9 + n_ci:9 + n_ci + n_co]
        h_sc, acc_sc = refs[9 + n_ci + n_co:11 + n_ci + n_co]
        c_sems = refs[11 + n_ci + n_co:]
        i = pl.program_id(0)
        k = pl.program_id(1)
        if comm is not None:
            @pl.when((i == 0) & (k == 0))
            def _():
                comm["start"](c_in, c_out, c_sems)

            @pl.when((i == nt - 2) & (k == 0))
            def _():
                _relay(comm, (c_in, c_out, c_sems))

        @pl.when(k == 0)
        def _():
            h, _, _ = _normmod(x_ref[...], mod_ref)
            h_sc[...] = h.astype(BF16)
            acc_sc[...] = jnp.zeros_like(acc_sc)

        h = h_sc[...]
        g = _dot(h, wg_ref[...])
        u = _dot(h, wu_ref[...])
        g_ref[...] = g.astype(BF16)
        u_ref[...] = u.astype(BF16)
        a = (g * _sigmoid(g)) * u
        acc_sc[...] += _dot(a.astype(BF16), wd_ref[...])

        @pl.when(k == nk - 1)
        def _():
            f = acc_sc[...]
            f_ref[...] = f.astype(BF16)
            xo_ref[...] = x_ref[...] + (0.5 * mod_ref[2:3, :]) * f

        if comm is not None:
            @pl.when((i == nt - 1) & (k == nk - 1))
            def _():
                comm["finish"](c_in, c_out, c_sems)

    anyspec = pl.BlockSpec(memory_space=pl.ANY)
    out = pl.pallas_call(
        body, name=f"ffn_fwd_{l}_{j}",
        grid=(nt, nk),
        in_specs=[
            pl.BlockSpec((tm, D), lambda i, k: (i, 0)),
            pl.BlockSpec((None, 8, D), lambda i, k: (i // tpb, 0, 0)),
            pl.BlockSpec((None, D, tf), lambda i, k: (k, 0, 0)),
            pl.BlockSpec((None, D, tf), lambda i, k: (nk + k, 0, 0)),
            pl.BlockSpec((tf, D), lambda i, k: (k, 0)),
        ] + c_specs,
        out_specs=[
            pl.BlockSpec((tm, D), lambda i, k: (i, 0)),
            pl.BlockSpec((tm, tf), lambda i, k: (i, k)),
            pl.BlockSpec((tm, tf), lambda i, k: (i, k)),
            pl.BlockSpec((tm, D), lambda i, k: (i, 0)),
        ] + [anyspec] * n_co,
        out_shape=[
            jax.ShapeDtypeStruct((T, D), F32),
            jax.ShapeDtypeStruct((T, nk * tf), BF16),
            jax.ShapeDtypeStruct((T, nk * tf), BF16),
            jax.ShapeDtypeStruct((T, D), BF16),
        ] + c_outs,
        scratch_shapes=[pltpu.VMEM((tm, D), BF16), pltpu.VMEM((tm, D), F32)] + c_scr,
        compiler_params=_cp(("arbitrary", "arbitrary"), VMEM_BIG),
    )(x, mod, wup, wup, wdn, *c_args)
    return out[0], out[1], out[2], out[3], list(out[4:])


def _ffn_bwd(x, dy, mod, f, g, u, wup, wdn, l, j, S, comm=None):
    T = x.shape[0]
    tf = wup.shape[-1]
    nk = 2
    tm = _tile(S, 256)
    tpb = S // tm
    nt = T // tm
    c_args, c_specs, c_outs, c_scr = _hosted(comm)
    n_ci, n_co = len(c_args), len(c_outs)

    def body(*refs):
        x_ref, dy_ref, mod_ref, f_ref, g_ref, u_ref, wup_ref, wd_ref = refs[:8]
        c_in = refs[8:8 + n_ci]
        dx_ref, dmod_ref, wacc_ref, h_ref, df_ref, a_ref, dgu_ref = refs[8 + n_ci:15 + n_ci]
        c_out = refs[15 + n_ci:15 + n_ci + n_co]
        c_sems = refs[15 + n_ci + n_co:]
        i = pl.program_id(0)

        @pl.when(i == 0)
        def _():
            wacc_ref[...] = jnp.zeros_like(wacc_ref)
            if comm is not None:
                comm["start"](c_in, c_out, c_sems)

        @pl.when(i % tpb == 0)
        def _():
            dmod_ref[...] = jnp.zeros_like(dmod_ref)

        dy_ = dy_ref[...]
        h, xn, rstd = _normmod(x_ref[...], mod_ref)
        h_ref[...] = h.astype(BF16)
        dfb = ((0.5 * mod_ref[2:3, :]) * dy_).astype(BF16)
        df_ref[...] = dfb
        dmod_ref[2:3, :] += _colsum(0.5 * f_ref[...].astype(F32) * dy_)
        dh = None
        for k in range(nk):
            cols = slice(k * tf, (k + 1) * tf)
            da = _dot_nt(dfb, wd_ref[cols, :])
            gg = g_ref[:, cols].astype(F32)
            uu = u_ref[:, cols].astype(F32)
            sig = _sigmoid(gg)
            s = gg * sig
            a_ref[:, cols] = (s * uu).astype(BF16)
            du = (da * s).astype(BF16)
            dg = (da * uu * (sig * (1.0 + gg * (1.0 - sig)))).astype(BF16)
            dgu_ref[0, :, cols] = dg
            dgu_ref[1, :, cols] = du
            part = _dot_nt(dg, wup_ref[k]) + _dot_nt(du, wup_ref[nk + k])
            dh = part if dh is None else dh + part
        dx_ref[...] = dy_ + _normmod_bwd(dh, xn, rstd, mod_ref, dmod_ref, wacc_ref)

        if comm is not None:
            @pl.when(i == nt - 1)
            def _():
                comm["finish"](c_in, c_out, c_sems)

    once = pl.Buffered(1)
    anyspec = pl.BlockSpec(memory_space=pl.ANY)
    out = pl.pallas_call(
        body, name=f"ffn_bwd_{l}_{j}",
        grid=(nt,),
        in_specs=[
            pl.BlockSpec((tm, D), lambda i: (i, 0)),
            pl.BlockSpec((tm, D), lambda i: (i, 0)),
            pl.BlockSpec((None, 8, D), lambda i: (i // tpb, 0, 0)),
            pl.BlockSpec((tm, D), lambda i: (i, 0)),
            pl.BlockSpec((tm, nk * tf), lambda i: (i, 0)),
            pl.BlockSpec((tm, nk * tf), lambda i: (i, 0)),
            pl.BlockSpec((2 * nk, D, tf), lambda i: (0, 0, 0), pipeline_mode=once),
            pl.BlockSpec((nk * tf, D), lambda i: (0, 0), pipeline_mode=once),
        ] + c_specs,
        out_specs=[
            pl.BlockSpec((tm, D), lambda i: (i, 0)),
            pl.BlockSpec((None, 8, D), lambda i: (i // tpb, 0, 0)),
            pl.BlockSpec((8, D), lambda i: (0, 0)),
            pl.BlockSpec((tm, D), lambda i: (i, 0)),
            pl.BlockSpec((tm, D), lambda i: (i, 0)),
            pl.BlockSpec((tm, nk * tf), lambda i: (i, 0)),
            pl.BlockSpec((2, tm, nk * tf), lambda i: (0, i, 0)),
        ] + [anyspec] * n_co,
        out_shape=[
            jax.ShapeDtypeStruct((T, D), F32),
            jax.ShapeDtypeStruct((T // S, 8, D), F32),
            jax.ShapeDtypeStruct((8, D), F32),
            jax.ShapeDtypeStruct((T, D), BF16),
            jax.ShapeDtypeStruct((T, D), BF16),
            jax.ShapeDtypeStruct((T, nk * tf), BF16),
            jax.ShapeDtypeStruct((2, T, nk * tf), BF16),
        ] + c_outs,
        scratch_shapes=c_scr,
        compiler_params=_cp(("arbitrary",), VMEM_BIG),
    )(x, dy, mod, f, g, u, wup, wdn, *c_args)
    return tuple(out[:7]) + (list(out[7:]),)


def _mm_tn(a, b, name, tma=None, tnb=None, split_n=False, with_bf16=False):
    T, M = a.shape
    b3 = b if b.ndim == 3 else b[None]
    nb, _, N = b3.shape
    tma = tma or M
    tnb = tnb or N
    npb = N // tnb
    tt = _tile(T, 1024)
    nt = T // tt

    def body(a_ref, b_ref, o_ref, *ob_ref):
        @pl.when(pl.program_id(2) == 0)
        def _():
            o_ref[...] = jnp.zeros_like(o_ref)

        o_ref[...] += _dot_tn(a_ref[...], b_ref[...])

        if with_bf16:
            @pl.when(pl.program_id(2) == nt - 1)
            def _():
                ob_ref[0][...] = o_ref[...].astype(BF16)

    if split_n:
        shape = (nb * npb, M, tnb)
        out_spec = pl.BlockSpec((None, tma, tnb), lambda m, n, t: (n, m, 0))
    else:
        assert nb == 1
        shape = (M, N)
        out_spec = pl.BlockSpec((tma, tnb), lambda m, n, t: (m, n))
    dts = (F32, BF16) if with_bf16 else (F32,)
    out = pl.pallas_call(
        body, name=name,
        grid=(M // tma, nb * npb, nt),
        in_specs=[pl.BlockSpec((tt, tma), lambda m, n, t: (t, m)),
                  pl.BlockSpec((None, tt, tnb), lambda m, n, t: (n // npb, t, n % npb))],
        out_specs=[out_spec] * len(dts),
        out_shape=[jax.ShapeDtypeStruct(shape, dt) for dt in dts],
        compiler_params=_cp(("arbitrary", "arbitrary", "arbitrary"), VMEM_BIG),
    )(a, b3)
    return tuple(out) if with_bf16 else out[0]


def _mix_in_fwd(x, mod, winp, l, S):
    T = x.shape[0]
    tm = _tile(S, 512)
    tpb = S // tm

    def body(x_ref, mod_ref, w_ref, h_ref, p_ref):
        h, _, _ = _normmod(x_ref[...], mod_ref)
        hb = h.astype(BF16)
        h_ref[...] = hb
        p_ref[...] = _dot(hb, w_ref[...])

    return pl.pallas_call(
        body, name=f"mix_in_fwd_{l}",
        grid=(T // tm,),
        in_specs=[pl.BlockSpec((tm, D), lambda i: (i, 0)),
                  pl.BlockSpec((None, 8, D), lambda i: (i // tpb, 0, 0)),
                  pl.BlockSpec((D, N_INP), lambda i: (0, 0))],
        out_specs=[pl.BlockSpec((tm, D), lambda i: (i, 0)),
                   pl.BlockSpec((tm, N_INP), lambda i: (i, 0))],
        out_shape=[jax.ShapeDtypeStruct((T, D), BF16), jax.ShapeDtypeStruct((T, N_INP), F32)],
        compiler_params=_cp(("arbitrary",), VMEM_BIG),
    )(x, mod, winp)


def _mix_in_bwd(x, dres, mod, dproj, winp, l, S):
    T = x.shape[0]
    tm = _tile(S, 512)
    tpb = S // tm

    def body(x_ref, dr_ref, mod_ref, dp_ref, w_ref, dx_ref, dmod_ref, wacc_ref):
        i = pl.program_id(0)

        @pl.when(i == 0)
        def _():
            wacc_ref[...] = jnp.zeros_like(wacc_ref)

        @pl.when(i % tpb == 0)
        def _():
            dmod_ref[...] = jnp.zeros_like(dmod_ref)

        dh = _dot_nt(dp_ref[...], w_ref[...])
        _, xn, rstd = _normmod(x_ref[...], mod_ref)
        dx_ref[...] = dr_ref[...] + _normmod_bwd(dh, xn, rstd, mod_ref, dmod_ref, wacc_ref)

    return pl.pallas_call(
        body, name=f"mix_in_bwd_{l}",
        grid=(T // tm,),
        in_specs=[pl.BlockSpec((tm, D), lambda i: (i, 0)),
                  pl.BlockSpec((tm, D), lambda i: (i, 0)),
                  pl.BlockSpec((None, 8, D), lambda i: (i // tpb, 0, 0)),
                  pl.BlockSpec((tm, N_INP), lambda i: (i, 0)),
                  pl.BlockSpec((D, N_INP), lambda i: (0, 0))],
        out_specs=[pl.BlockSpec((tm, D), lambda i: (i, 0)),
                   pl.BlockSpec((None, 8, D), lambda i: (i // tpb, 0, 0)),
                   pl.BlockSpec((8, D), lambda i: (0, 0))],
        out_shape=[jax.ShapeDtypeStruct((T, D), F32),
                   jax.ShapeDtypeStruct((T // S, 8, D), F32),
                   jax.ShapeDtypeStruct((8, D), F32)],
        compiler_params=_cp(("arbitrary",), VMEM_BIG),
    )(x, dres, mod, dproj, winp)


_GROUPS = ((0, LW), (LW, LW + AW), (LW + AW, D))


def _mix_out_fwd(x, ylru, osb, ofox, mod, gmix, wout, l, S):
    T = x.shape[0]
    tm = _tile(S, 512)
    tpb = S // tm

    def body(x_ref, yl_ref, sb_ref, fx_ref, mod_ref, gm_ref, w_ref, xo_ref, y_ref, mo_ref):
        for src, (lo, hi) in zip((yl_ref, sb_ref, fx_ref), _GROUPS):
            vn, _ = _rms_rows(src[...])
            y_ref[:, lo:hi] = (vn * gm_ref[0:1, lo:hi]).astype(BF16)
        mo = _dot(y_ref[...], w_ref[...])
        mo_ref[...] = mo.astype(BF16)
        xo_ref[...] = x_ref[...] + mod_ref[2:3, :] * mo

    return pl.pallas_call(
        body, name=f"mix_out_fwd_{l}",
        grid=(T // tm,),
        in_specs=[pl.BlockSpec((tm, D), lambda i: (i, 0)),
                  pl.BlockSpec((tm, LW), lambda i: (i, 0)),
                  pl.BlockSpec((tm, AW), lambda i: (i, 0)),
                  pl.BlockSpec((tm, AW), lambda i: (i, 0)),
                  pl.BlockSpec((None, 8, D), lambda i: (i // tpb, 0, 0)),
                  pl.BlockSpec((None, 8, D), lambda i: (l, 0, 0)),
                  pl.BlockSpec((D, D), lambda i: (0, 0))],
        out_specs=[pl.BlockSpec((tm, D), lambda i: (i, 0)),
                   pl.BlockSpec((tm, D), lambda i: (i, 0)),
                   pl.BlockSpec((tm, D), lambda i: (i, 0))],
        out_shape=[jax.ShapeDtypeStruct((T, D), F32),
                   jax.ShapeDtypeStruct((T, D), BF16),
                   jax.ShapeDtypeStruct((T, D), BF16)],
        compiler_params=_cp(("arbitrary",), VMEM_BIG),
    )(x, ylru, osb, ofox, mod, gmix, wout)


def _mix_out_bwd(dx2, ylru, osb, ofox, mo, mod, gmix, wout, l, S):
    T = dx2.shape[0]
    tm = _tile(S, 512)
    tpb = S // tm

    def body(dx_ref, yl_ref, sb_ref, fx_ref, mo_ref, mod_ref, gm_ref, w_ref,
             dyl_ref, dsb_ref, dfx_ref, dmo_ref, dmod_ref, wacc_ref):
        i = pl.program_id(0)

        @pl.when(i == 0)
        def _():
            wacc_ref[...] = jnp.zeros_like(wacc_ref)

        @pl.when(i % tpb == 0)
        def _():
            dmod_ref[...] = jnp.zeros_like(dmod_ref)

        dx = dx_ref[...]
        dmod_ref[2:3, :] += _colsum(mo_ref[...].astype(F32) * dx)
        dmo = (mod_ref[2:3, :] * dx).astype(BF16)
        dmo_ref[...] = dmo
        dy = _dot_nt(dmo, w_ref[...])
        for src, dst, (lo, hi) in zip((yl_ref, sb_ref, fx_ref), (dyl_ref, dsb_ref, dfx_ref), _GROUPS):
            vn, rstd = _rms_rows(src[...])
            dyg = dy[:, lo:hi]
            wacc_ref[0:1, lo:hi] += _colsum(dyg * vn)
            dst[...] = _rms_bwd(vn, rstd, dyg * gm_ref[0:1, lo:hi])

    return pl.pallas_call(
        body, name=f"mix_out_bwd_{l}",
        grid=(T // tm,),
        in_specs=[pl.BlockSpec((tm, D), lambda i: (i, 0)),
                  pl.BlockSpec((tm, LW), lambda i: (i, 0)),
                  pl.BlockSpec((tm, AW), lambda i: (i, 0)),
                  pl.BlockSpec((tm, AW), lambda i: (i, 0)),
                  pl.BlockSpec((tm, D), lambda i: (i, 0)),
                  pl.BlockSpec((None, 8, D), lambda i: (i // tpb, 0, 0)),
                  pl.BlockSpec((None, 8, D), lambda i: (l, 0, 0)),
                  pl.BlockSpec((D, D), lambda i: (0, 0))],
        out_specs=[pl.BlockSpec((tm, LW), lambda i: (i, 0)),
                   pl.BlockSpec((tm, AW), lambda i: (i, 0)),
                   pl.BlockSpec((tm, AW), lambda i: (i, 0)),
                   pl.BlockSpec((tm, D), lambda i: (i, 0)),
                   pl.BlockSpec((None, 8, D), lambda i: (i // tpb, 0, 0)),
                   pl.BlockSpec((8, D), lambda i: (0, 0))],
        out_shape=[jax.ShapeDtypeStruct((T, LW), F32),
                   jax.ShapeDtypeStruct((T, AW), F32),
                   jax.ShapeDtypeStruct((T, AW), F32),
                   jax.ShapeDtypeStruct((T, D), BF16),
                   jax.ShapeDtypeStruct((T // S, 8, D), F32),
                   jax.ShapeDtypeStruct((8, D), F32)],
        compiler_params=_cp(("arbitrary",), VMEM_BIG),
    )(dx2, ylru, osb, ofox, mo, mod, gmix, wout)


def _loss_head(y, tgt, S):
    T = y.shape[0]
    tm = _tile(S, 512)

    def body(y_ref, t_ref, dy_ref, l_ref):
        @pl.when(pl.program_id(0) == 0)
        def _():
            l_ref[...] = jnp.zeros_like(l_ref)

        d = y_ref[...] - t_ref[...]
        dy_ref[...] = d * (1.0 / D)
        l_ref[...] += (0.5 / D) * _rowsum(_colsum(d * d))

    return pl.pallas_call(
        body, name="loss_head",
        grid=(T // tm,),
        in_specs=[pl.BlockSpec((tm, D), lambda i: (i, 0)), pl.BlockSpec((tm, D), lambda i: (i, 0))],
        out_specs=[pl.BlockSpec((tm, D), lambda i: (i, 0)), pl.BlockSpec((8, 128), lambda i: (0, 0))],
        out_shape=[jax.ShapeDtypeStruct((T, D), F32), jax.ShapeDtypeStruct((8, 128), F32)],
        compiler_params=_cp(("arbitrary",)),
    )(y, tgt)


def _lru_gates(u, vp_ref, wr_ref, wi_ref):
    ub = u.astype(BF16)
    r = _sigmoid(_dot(ub, wr_ref[...]) + vp_ref[1:2, :])
    ig = _sigmoid(_dot(ub, wi_ref[...]) + vp_ref[2:3, :])
    lam = vp_ref[3:4, :]
    sp = jnp.maximum(-lam, 0.0) + _log1p(jnp.exp(-jnp.abs(lam)))
    log_a = (-LRU_C) * r * sp
    a = jnp.exp(log_a)
    mult = jnp.sqrt(-_expm1_neg(2.0 * log_a))
    return ub, r, ig, sp, a, mult


def _conv_taps(x, xp, row, cw_ref):
    xs = [x]
    for d in (1, 2, 3):
        xs.append(jnp.where(row >= d, pltpu.roll(x, d, 0), pltpu.roll(xp, d, 0)))
    u = xs[0] * cw_ref[3:4, :]
    for d in (1, 2, 3):
        u = u + xs[d] * cw_ref[3 - d:4 - d, :]
    return xs, u


def _lru_fwd(proj, cw, vp, wr, wi, l, S):
    T = proj.shape[0]
    ts = _tile(S, 256)
    nb = S // ts

    def body(x_ref, lg_ref, cw_ref, vp_ref, wr_ref, wi_ref, y_ref, h_ref, xp_sc, hc_sc):
        @pl.when(pl.program_id(1) == 0)
        def _():
            xp_sc[...] = jnp.zeros_like(xp_sc)
            hc_sc[...] = jnp.zeros_like(hc_sc)

        row = lax.broadcasted_iota(jnp.int32, (ts, LW), 0)
        x = x_ref[...]
        _, u = _conv_taps(x, xp_sc[...], row, cw_ref)
        u = u + vp_ref[0:1, :]
        xp_sc[...] = x
        _, _, ig, _, a, mult = _lru_gates(u, vp_ref, wr_ref, wi_ref)
        bv = mult * (ig * u)
        av = a
        d = 1
        while d < ts:
            a_s = jnp.where(row >= d, pltpu.roll(av, d, 0), 1.0)
            b_s = jnp.where(row >= d, pltpu.roll(bv, d, 0), 0.0)
            bv = av * b_s + bv
            av = av * a_s
            d *= 2
        h = bv + av * hc_sc[7:8, :]
        hc_sc[...] = h[ts - 8:ts, :]
        h_ref[...] = h
        gl, _ = _gelu_and_grad(lg_ref[...])
        y_ref[...] = h * gl

    return pl.pallas_call(
        body, name=f"lru_fwd_{l}",
        grid=(T // S, nb),
        in_specs=[pl.BlockSpec((ts, LW), lambda b, j: (b * nb + j, 0)),
                  pl.BlockSpec((ts, LW), lambda b, j: (b * nb + j, 1)),
                  pl.BlockSpec((None, 8, LW), lambda b, j: (l, 0, 0)),
                  pl.BlockSpec((None, 8, LW), lambda b, j: (l, 0, 0)),
                  pl.BlockSpec((None, LW, LW), lambda b, j: (l, 0, 0)),
                  pl.BlockSpec((None, LW, LW), lambda b, j: (l, 0, 0))],
        out_specs=[pl.BlockSpec((ts, LW), lambda b, j: (b * nb + j, 0)),
                   pl.BlockSpec((ts, LW), lambda b, j: (b * nb + j, 0))],
        out_shape=[jax.ShapeDtypeStruct((T, LW), F32), jax.ShapeDtypeStruct((T, LW), F32)],
        scratch_shapes=[pltpu.VMEM((ts, LW), F32), pltpu.VMEM((8, LW), F32)],
        compiler_params=_cp(("arbitrary", "arbitrary")),
    )(proj, proj, cw, vp, wr, wi)


def _lru_bwd(dyl, proj, h, cw, vp, wr, wi, l, S):
    T = proj.shape[0]
    ts = _tile(S, 256)
    nb = S // ts

    def body(dy_ref, x_ref, xprev_ref, lg_ref, h_ref, hprev_ref, cw_ref, vp_ref, wr_ref, wi_ref,
             dx_ref, dlg_ref, dpr_ref, dpi_ref, ub_ref, wacc_ref, gc_sc, af_sc, dun_sc):
        b = pl.program_id(0)
        j = pl.program_id(1)
        first = j == nb - 1

        @pl.when((b == 0) & (j == 0))
        def _():
            wacc_ref[...] = jnp.zeros_like(wacc_ref)

        @pl.when(j == 0)
        def _():
            gc_sc[...] = jnp.zeros_like(gc_sc)
            af_sc[...] = jnp.ones_like(af_sc)
            dun_sc[...] = jnp.zeros_like(dun_sc)

        row = lax.broadcasted_iota(jnp.int32, (ts, LW), 0)
        keep = jnp.where(first, 0.0, 1.0)
        x = x_ref[...]
        xs, u = _conv_taps(x, xprev_ref[...] * keep, row, cw_ref)
        u = u + vp_ref[0:1, :]
        ub, r, ig, sp, a, mult = _lru_gates(u, vp_ref, wr_ref, wi_ref)
        ub_ref[...] = ub
        hh = h_ref[...]
        h_m1 = jnp.where(row >= 1, pltpu.roll(hh, 1, 0), pltpu.roll(hprev_ref[...] * keep, 1, 0))
        dy = dy_ref[...]
        gl, dgl = _gelu_and_grad(lg_ref[...])
        dlg_ref[...] = (dy * hh * dgl).astype(BF16)
        bv = dy * gl
        av = jnp.where(row < ts - 1, pltpu.roll(a, ts - 1, 0), af_sc[0:1, :])
        d = 1
        while d < ts:
            a_s = jnp.where(row < ts - d, pltpu.roll(av, ts - d, 0), 1.0)
            b_s = jnp.where(row < ts - d, pltpu.roll(bv, ts - d, 0), 0.0)
            bv = av * b_s + bv
            av = av * a_s
            d *= 2
        gt = bv + av * gc_sc[0:1, :]
        gc_sc[...] = gt[0:8, :]
        af_sc[...] = a[0:8, :]
        da = gt * h_m1
        d_ig = gt * mult * u
        d_mult = gt * ig * u
        du = gt * mult * ig
        dlog_a = da * a - d_mult * (a * a) / mult
        dpre_r = (dlog_a * ((-LRU_C) * sp)) * r * (1.0 - r)
        dpre_i = d_ig * ig * (1.0 - ig)
        lam = vp_ref[3:4, :]
        wacc_ref[7:8, :] += _colsum(dlog_a * r) * (LRU_C * _sigmoid(-lam))
        wacc_ref[5:6, :] += _colsum(dpre_r)
        wacc_ref[6:7, :] += _colsum(dpre_i)
        dprb = dpre_r.astype(BF16)
        dpib = dpre_i.astype(BF16)
        dpr_ref[...] = dprb
        dpi_ref[...] = dpib
        du = du + _dot_nt(dprb, wr_ref[...]) + _dot_nt(dpib, wi_ref[...])
        wacc_ref[4:5, :] += _colsum(du)
        dun = dun_sc[...]
        dx = du * cw_ref[3:4, :]
        wacc_ref[3:4, :] += _colsum(du * xs[0])
        for dd in (1, 2, 3):
            du_s = jnp.where(row < ts - dd, pltpu.roll(du, ts - dd, 0), pltpu.roll(dun, ts - dd, 0))
            dx = dx + du_s * cw_ref[3 - dd:4 - dd, :]
            wacc_ref[3 - dd:4 - dd, :] += _colsum(du * xs[dd])
        dun_sc[...] = du
        dx_ref[...] = dx.astype(BF16)

    def tb(b, j):
        return b * nb + (nb - 1 - j)

    def tbp(b, j):
        return b * nb + jnp.maximum(nb - 2 - j, 0)

    return pl.pallas_call(
        body, name=f"lru_bwd_{l}",
        grid=(T // S, nb),
        in_specs=[pl.BlockSpec((ts, LW), lambda b, j: (tb(b, j), 0)),
                  pl.BlockSpec((ts, LW), lambda b, j: (tb(b, j), 0)),
                  pl.BlockSpec((ts, LW), lambda b, j: (tbp(b, j), 0)),
                  pl.BlockSpec((ts, LW), lambda b, j: (tb(b, j), 1)),
                  pl.BlockSpec((ts, LW), lambda b, j: (tb(b, j), 0)),
                  pl.BlockSpec((ts, LW), lambda b, j: (tbp(b, j), 0)),
                  pl.BlockSpec((None, 8, LW), lambda b, j: (l, 0, 0)),
                  pl.BlockSpec((None, 8, LW), lambda b, j: (l, 0, 0)),
                  pl.BlockSpec((None, LW, LW), lambda b, j: (l, 0, 0)),
                  pl.BlockSpec((None, LW, LW), lambda b, j: (l, 0, 0))],
        out_specs=[pl.BlockSpec((ts, LW), lambda b, j: (tb(b, j), 0)),
                   pl.BlockSpec((ts, LW), lambda b, j: (tb(b, j), 0)),
                   pl.BlockSpec((ts, LW), lambda b, j: (tb(b, j), 0)),
                   pl.BlockSpec((ts, LW), lambda b, j: (tb(b, j), 0)),
                   pl.BlockSpec((ts, LW), lambda b, j: (tb(b, j), 0)),
                   pl.BlockSpec((8, LW), lambda b, j: (0, 0))],
        out_shape=[jax.ShapeDtypeStruct((T, LW), BF16),
                   jax.ShapeDtypeStruct((T, LW), BF16),
                   jax.ShapeDtypeStruct((T, LW), BF16),
                   jax.ShapeDtypeStruct((T, LW), BF16),
                   jax.ShapeDtypeStruct((T, LW), BF16),
                   jax.ShapeDtypeStruct((8, LW), F32)],
        scratch_shapes=[pltpu.VMEM((8, LW), F32), pltpu.VMEM((8, LW), F32), pltpu.VMEM((ts, LW), F32)],
        compiler_params=_cp(("arbitrary", "arbitrary")),
    )(dyl, proj, proj, proj, h, h, cw, vp, wr, wi)


def _fgate_fwd(proj, bfp, l, S):
    T = proj.shape[0]

    def body(x_ref, b_ref, o_ref):
        z = x_ref[...] + b_ref[0:1, :]
        v = jnp.minimum(z, 0.0) - _log1p(jnp.exp(-jnp.abs(z)))
        row = lax.broadcasted_iota(jnp.int32, (S, 128), 0)
        d = 1
        while d < S:
            v = v + jnp.where(row >= d, pltpu.roll(v, d, 0), 0.0)
            d *= 2
        o_ref[...] = v

    return pl.pallas_call(
        body, name=f"fgate_fwd_{l}",
        grid=(T // S,),
        in_specs=[pl.BlockSpec((S, 128), lambda b: (b, F_BLK)),
                  pl.BlockSpec((None, 8, 128), lambda b: (l, 0, 0))],
        out_specs=pl.BlockSpec((S, 128), lambda b: (b, 0)),
        out_shape=jax.ShapeDtypeStruct((T, 128), F32),
        compiler_params=_cp(("arbitrary",)),
    )(proj, bfp)


def _fgate_bwd(dcum, proj, bfp, l, S):
    T = proj.shape[0]

    def body(d_ref, x_ref, b_ref, o_ref, wacc_ref):
        @pl.when(pl.program_id(0) == 0)
        def _():
            wacc_ref[...] = jnp.zeros_like(wacc_ref)

        v = d_ref[...]
        row = lax.broadcasted_iota(jnp.int32, (S, 128), 0)
        d = 1
        while d < S:
            v = v + jnp.where(row < S - d, pltpu.roll(v, S - d, 0), 0.0)
            d *= 2
        z = x_ref[...] + b_ref[0:1, :]
        dz = v * _sigmoid(-z)
        o_ref[...] = dz.astype(BF16)
        wacc_ref[0:1, :] += _colsum(dz)

    return pl.pallas_call(
        body, name=f"fgate_bwd_{l}",
        grid=(T // S,),
        in_specs=[pl.BlockSpec((S, 128), lambda b: (b, 0)),
                  pl.BlockSpec((S, 128), lambda b: (b, F_BLK)),
                  pl.BlockSpec((None, 8, 128), lambda b: (l, 0, 0))],
        out_specs=[pl.BlockSpec((S, 128), lambda b: (b, 0)), pl.BlockSpec((8, 128), lambda b: (0, 0))],
        out_shape=[jax.ShapeDtypeStruct((T, 128), BF16), jax.ShapeDtypeStruct((8, 128), F32)],
        compiler_params=_cp(("arbitrary",)),
    )(dcum, proj, bfp)


SBQ_BLK, SBK_BLK, SBV_BLK = 8, 10, 12
FXQ_BLK, FXK_BLK, FXV_BLK = 14, 16, 18
PAIR = 2 * HD


def _lane_masks():
    lane = lax.broadcasted_iota(jnp.int32, (1, PAIR), 1)
    return lane, lane < HD


def _pair_select(m0, a0, a1):
    return jnp.where(m0, a0, a1)


def _pair_split(x, m0):
    return jnp.where(m0, x, 0.0).astype(BF16), jnp.where(m0, 0.0, x).astype(BF16)


def _pair_mean(x, m0):
    s0 = _rowsum(jnp.where(m0, x, 0.0))
    s1 = _rowsum(x) - s0
    return jnp.where(m0, s0, s1) * (1.0 / HD)


def _pair_rms(x, m0):
    rstd = lax.rsqrt(_pair_mean(x * x, m0) + EPS)
    return x * rstd, rstd


def _pair_rms_bwd(xn, rstd, dyn, m0):
    return rstd * (dyn - xn * _pair_mean(dyn * xn, m0))


def _logsig2(z):
    l1p = jnp.log(1.0 + jnp.exp(-jnp.abs(z)))
    lb = jnp.minimum(z, 0.0) - l1p
    return lb, lb - z


def _rows(ref, blk, size):
    return ref[pl.ds(pl.multiple_of(blk * size, size), size), :]


def _loop_grouped(n, body, init, groups=(4, 2, 1)):
    c, done = init, 0
    for per in groups:
        def several(i, cc, per=per, done=done):
            for j in range(per):
                cc = body(done + per * i + j, cc)
            return cc

        trips = (n - done) // per
        c = lax.fori_loop(0, trips, several, c)
        done = done + trips * per
    return c


def _transpose_blocks(src_ref, dst_sc, nblk, blk):
    for kb in range(nblk):
        dst_sc[kb] = src_ref[kb * blk:(kb + 1) * blk, :].astype(F32).T.astype(BF16)


def _sbq_fwd(proj, l, S, comm=None):
    T = proj.shape[0]
    tb = TQ_(S)
    nb = S // tb
    nbat = T // S
    c_args, c_specs, c_outs, c_scr = _hosted(comm)
    n_ci, n_co = len(c_args), len(c_outs)

    def body(*refs):
        q_ref, k_ref, v_ref = refs[:3]
        c_in = refs[3:3 + n_ci]
        o_ref, t1_ref = refs[3 + n_ci:5 + n_ci]
        c_out = refs[5 + n_ci:5 + n_ci + n_co]
        kt_sc, vb_sc = refs[5 + n_ci + n_co:7 + n_ci + n_co]
        c_sems = refs[7 + n_ci + n_co:]
        step = pl.program_id(0) * 2 + pl.program_id(1)
        if comm is not None:
            @pl.when(step == 0)
            def _():
                comm["start"](c_in, c_out, c_sems)

            @pl.when(step == 2 * nbat - 1)
            def _():
                _relay(comm, (c_in, c_out, c_sems))

        _transpose_blocks(k_ref, kt_sc, nb, tb)
        vb_sc[...] = v_ref[...].astype(BF16)
        lane, m0 = _lane_masks()
        tri = _tri(tb, "row_gt_col")
        past = lax.broadcasted_iota(jnp.int32, (tb, tb), 1) < lax.broadcasted_iota(jnp.int32, (tb, tb), 0)

        def qloop(qb, carry):
            qh = _pair_split(_rows(q_ref, qb, tb) * SCALE, m0)

            def scores(kb):
                return tuple(_dot(qh[h], kt_sc[kb]) for h in range(2))

            def block(kb, kb_next, z, c, masked):
                mid = []
                for h in range(2):
                    lb, l1 = _logsig2(z[h])
                    if masked:
                        l1 = jnp.where(past, l1, 0.0)
                    mid.append((lb, l1, _cumsum_mm(l1, tri, parts=2)))
                z_next = scores(kb_next)
                pv, runs = [], []
                for h in range(2):
                    lb, l1, cs = mid[h]
                    w = jnp.exp(lb + (cs + c[h][1]))
                    if masked:
                        w = jnp.where(past, w, 0.0)
                    pv.append(_dot(w.astype(BF16), _rows(vb_sc, kb, tb)))
                    runs.append(c[h][1] + (cs[:, 0:1] + l1[:, 0:1]))
                return z_next, tuple((c[h][0] + pv[h], runs[h]) for h in range(2))

            zero = (jnp.zeros((tb, PAIR), F32), jnp.zeros((tb, 1), F32))
            z, c = block(qb, jnp.maximum(qb - 1, 0), scores(qb), (zero, zero), True)

            def off_diag(i, zc):
                kb = qb - 1 - i
                return block(kb, jnp.maximum(kb - 1, 0), zc[0], zc[1], False)

            _, c = _loop_grouped(qb, off_diag, (z, c))
            r0 = pl.multiple_of(qb * tb, tb)
            o_ref[pl.ds(r0, tb), :] = _pair_select(m0, c[0][0], c[1][0])
            t1_ref[pl.ds(r0, tb), :] = jnp.where(lane == 0, c[0][1], jnp.where(lane == 1, c[1][1], 0.0))
            return carry

        lax.fori_loop(0, nb, qloop, 0)
        if comm is not None:
            @pl.when(step == 2 * nbat - 1)
            def _():
                comm["finish"](c_in, c_out, c_sems)

    def col(blk):
        return pl.BlockSpec((S, PAIR), lambda b, p: (b, blk + p))

    anyspec = pl.BlockSpec(memory_space=pl.ANY)
    out = pl.pallas_call(
        body, name=f"sb_fwd_{l}",
        grid=(nbat, 2),
        in_specs=[col(SBQ_BLK), col(SBK_BLK), col(SBV_BLK)] + c_specs,
        out_specs=[col(0), col(0)] + [anyspec] * n_co,
        out_shape=[jax.ShapeDtypeStruct((T, AW), F32), jax.ShapeDtypeStruct((T, AW), F32)] + c_outs,
        scratch_shapes=[pltpu.VMEM((nb, PAIR, tb), BF16), pltpu.VMEM((S, PAIR), BF16)] + c_scr,
        compiler_params=_cp(("arbitrary", "arbitrary"), VMEM_BIG),
    )(proj, proj, proj, *c_args)
    return out[0], out[1], list(out[2:])


def _sbq_bwd(proj, do, t1, l, S, comm=None):
    T = proj.shape[0]
    tb = TQ_(S)
    nb = S // tb
    nbat = T // S
    c_args, c_specs, c_outs, c_scr = _hosted(comm)
    n_ci, n_co = len(c_args), len(c_outs)

    def body(*refs):
        q_ref, k_ref, v_ref, do_ref, t1_ref = refs[:5]
        c_in = refs[5:5 + n_ci]
        dq_ref, dk_ref, dv_ref = refs[5 + n_ci:8 + n_ci]
        c_out = refs[8 + n_ci:8 + n_ci + n_co]
        kb_sc, kt_sc, vt_sc, dkt_sc, dvt_sc = refs[8 + n_ci + n_co:13 + n_ci + n_co]
        c_sems = refs[13 + n_ci + n_co:]
        step = pl.program_id(0) * 2 + pl.program_id(1)
        if comm is not None:
            @pl.when(step == 0)
            def _():
                comm["start"](c_in, c_out, c_sems)

        kb_sc[...] = k_ref[...].astype(BF16)
        _transpose_blocks(k_ref, kt_sc, nb, tb)
        _transpose_blocks(v_ref, vt_sc, nb, tb)
        dkt_sc[...] = jnp.zeros_like(dkt_sc)
        dvt_sc[...] = jnp.zeros_like(dvt_sc)
        _, m0 = _lane_masks()
        mt0 = lax.broadcasted_iota(jnp.int32, (PAIR, 1), 0) < HD
        tri_in = _tri(tb, "row_le_col")
        tri_ex = _tri(tb, "row_lt_col")
        past = lax.broadcasted_iota(jnp.int32, (tb, tb), 1) < lax.broadcasted_iota(jnp.int32, (tb, tb), 0)

        def qloop(qb, carry):
            qf = _rows(q_ref, qb, tb) * SCALE
            dof = _rows(do_ref, qb, tb)
            qh = _pair_split(qf, m0)
            doh = _pair_split(dof, m0)
            qth = _pair_split(qf.T, mt0)
            doth = _pair_split(dof.T, mt0)
            t1v = _rows(t1_ref, qb, tb)
            tot = (t1v[:, 0:1], t1v[:, 1:2])

            def accumulate(kb, wz, dqs):
                out = []
                for h in range(2):
                    wb, dz = wz[h]
                    dvt_sc[kb] += _dot(doth[h], wb)
                    dkt_sc[kb] += _dot(qth[h], dz)
                    out.append(dqs[h] + _dot(dz, _rows(kb_sc, kb, tb)))
                return tuple(out)

            def block(kb, c, masked):
                hs = range(2)
                runs, dqs, (kb_prev, wz_prev) = c
                z = [_dot(qh[h], kt_sc[kb]) for h in hs]
                dw = [_dot(doh[h], vt_sc[kb]) for h in hs]
                st = []
                for h in hs:
                    lb, l1 = _logsig2(z[h])
                    if masked:
                        l1 = jnp.where(past, l1, 0.0)
                    st.append((lb, _cumsum_mm(l1, tri_in, parts=2)))
                dqs = accumulate(kb_prev, wz_prev, dqs)
                mid = []
                for h in hs:
                    lb, p1 = st[h]
                    w = jnp.exp(lb + (tot[h] - (runs[h][0] + p1)))
                    if masked:
                        w = jnp.where(past, w, 0.0)
                    gm = w * dw[h]
                    mid.append((w.astype(BF16), gm, _cumsum_mm(gm, tri_ex, parts=1)))
                new_runs, wz = [], []
                for h in hs:
                    run1, rung = runs[h]
                    wb, gm, cx = mid[h]
                    dz = gm - (gm + (rung + cx)) * jnp.exp(st[h][0])
                    if masked:
                        dz = jnp.where(past, dz, 0.0)
                    wz.append((wb, dz.astype(BF16)))
                    p1 = st[h][1]
                    new_runs.append((run1 + p1[:, tb - 1:tb], rung + (cx[:, tb - 1:tb] + gm[:, tb - 1:tb])))
                return tuple(new_runs), dqs, (kb, tuple(wz))

            z1 = jnp.zeros((tb, 1), F32)
            zq = jnp.zeros((tb, PAIR), F32)
            zb = jnp.zeros((tb, tb), BF16)
            none = (jnp.int32(0), ((zb, zb), (zb, zb)))
            c = _loop_grouped(qb, lambda i, cc: block(i, cc, False), (((z1, z1), (z1, z1)), (zq, zq), none))
            _, dqs, (kb_last, wz_last) = block(qb, c, True)
            dqs = accumulate(kb_last, wz_last, dqs)
            r0 = pl.multiple_of(qb * tb, tb)
            dq_ref[pl.ds(r0, tb), :] = (_pair_select(m0, dqs[0], dqs[1]) * SCALE).astype(BF16)
            return carry

        lax.fori_loop(0, nb, qloop, 0)
        for kb in range(nb):
            dk_ref[kb * tb:(kb + 1) * tb, :] = dkt_sc[kb].T.astype(BF16)
            dv_ref[kb * tb:(kb + 1) * tb, :] = dvt_sc[kb].T.astype(BF16)
        if comm is not None:
            @pl.when(step == 2 * nbat - 1)
            def _():
                comm["finish"](c_in, c_out, c_sems)

    def col(blk):
        return pl.BlockSpec((S, PAIR), lambda b, p: (b, blk + p))

    sh = jax.ShapeDtypeStruct((T, AW), BF16)
    anyspec = pl.BlockSpec(memory_space=pl.ANY)
    out = pl.pallas_call(
        body, name=f"sb_bwd_{l}",
        grid=(nbat, 2),
        in_specs=[col(SBQ_BLK), col(SBK_BLK), col(SBV_BLK), col(0), col(0)] + c_specs,
        out_specs=[col(0), col(0), col(0)] + [anyspec] * n_co,
        out_shape=[sh, sh, sh] + c_outs,
        scratch_shapes=[pltpu.VMEM((S, PAIR), BF16), pltpu.VMEM((nb, PAIR, tb), BF16), pltpu.VMEM((nb, PAIR, tb), BF16),
                        pltpu.VMEM((nb, PAIR, tb), F32), pltpu.VMEM((nb, PAIR, tb), F32)] + c_scr,
        compiler_params=_cp(("arbitrary", "arbitrary"), VMEM_BIG),
    )(proj, proj, proj, do, t1, *c_args)
    return out[0], out[1], out[2], list(out[3:])


def _foxq_fwd(proj, cum, ck, gqk2, l, S):
    T = proj.shape[0]
    tb = TQ_(S)
    nb = S // tb

    def body(q_ref, k_ref, v_ref, cum_ref, ck_ref, g_ref, o_ref, nl_ref, fk_sc, fkt_sc, vb_sc):
        lane, m0 = _lane_masks()
        p = pl.program_id(1)
        kn, _ = _pair_rms(k_ref[...], m0)
        fk_sc[...] = kn * g_ref[1:2, :]
        _transpose_blocks(fk_sc, fkt_sc, nb, tb)
        vb_sc[...] = v_ref[...].astype(BF16)
        causal = lax.broadcasted_iota(jnp.int32, (tb, tb), 1) <= lax.broadcasted_iota(jnp.int32, (tb, tb), 0)

        def qloop(qb, carry):
            qn, _ = _pair_rms(_rows(q_ref, qb, tb), m0)
            fqh = _pair_split(qn * (g_ref[0:1, :] * SCALE), m0)
            cumv = _rows(cum_ref, qb, tb)
            cq = [_rowsum(jnp.where(lane == 2 * p + h, cumv, 0.0)) for h in range(2)]

            def scores(kb):
                return tuple(_dot(fqh[h], fkt_sc[kb]) for h in range(2))

            def block(kb, kb_next, qk, c, masked):
                st = []
                for h in range(2):
                    s = qk[h] + (cq[h] - ck_ref[h, kb])
                    if masked:
                        s = jnp.where(causal, s, NEG)
                    m2 = jnp.maximum(c[h][0], jnp.max(s, axis=1, keepdims=True))
                    pr = jnp.exp(s - m2)
                    hi = pr.astype(BF16)
                    lo = (pr - hi.astype(F32)).astype(BF16)
                    vv = _rows(vb_sc, kb, tb)
                    st.append((m2, pr, _dot(hi, vv) + _dot(lo, vv)))
                qk_next = scores(kb_next)
                out = []
                for h in range(2):
                    m, lsum, acc = c[h]
                    m2, pr, pv = st[h]
                    al = jnp.exp(m - m2)
                    out.append((m2, al * lsum + _rowsum(pr), al * acc + pv))
                return qk_next, tuple(out)

            zero = (jnp.full((tb, 1), NEG, F32), jnp.zeros((tb, 1), F32), jnp.zeros((tb, PAIR), F32))

            def off_diag(i, sc):
                return block(i, i + 1, sc[0], sc[1], False)

            qk, c = lax.fori_loop(0, qb, off_diag, (scores(0), (zero, zero)))
            _, c = block(qb, qb, qk, c, True)
            r0 = pl.multiple_of(qb * tb, tb)
            o_ref[pl.ds(r0, tb), :] = _pair_select(m0, c[0][2] / c[0][1], c[1][2] / c[1][1])
            nl = [cq[h] - (c[h][0] + jnp.log(c[h][1])) for h in range(2)]
            nl_ref[pl.ds(r0, tb), :] = jnp.where(lane == 0, nl[0], jnp.where(lane == 1, nl[1], 0.0))
            return carry

        lax.fori_loop(0, nb, qloop, 0)

    def col(blk):
        return pl.BlockSpec((S, PAIR), lambda b, p: (b, blk + p))

    return pl.pallas_call(
        body, name=f"fox_fwd_{l}",
        grid=(T // S, 2),
        in_specs=[col(FXQ_BLK), col(FXK_BLK), col(FXV_BLK),
                  pl.BlockSpec((S, 128), lambda b, p: (b, 0)),
                  pl.BlockSpec((None, 2, nb, 1, tb), lambda b, p: (b, p, 0, 0, 0)),
                  pl.BlockSpec((None, 8, PAIR), lambda b, p: (l, 0, 0))],
        out_specs=[col(0), col(0)],
        out_shape=[jax.ShapeDtypeStruct((T, AW), F32)] * 2,
        scratch_shapes=[pltpu.VMEM((S, PAIR), F32), pltpu.VMEM((nb, PAIR, tb), BF16), pltpu.VMEM((S, PAIR), BF16)],
        compiler_params=_cp(("arbitrary", "arbitrary"), VMEM_BIG),
    )(proj, proj, proj, cum, ck, gqk2)


def _foxq_bwd(proj, do, nl, ox, ck, gqk2, l, S):
    T = proj.shape[0]
    tb = TQ_(S)
    nb = S // tb

    def body(q_ref, k_ref, v_ref, do_ref, nl_ref, ox_ref, ck_ref, g_ref,
             dq_ref, dk_ref, dv_ref, dc_ref, wacc_ref, fk_sc, fkt_sc, vt_sc, dfkt_sc, dvt_sc):
        @pl.when((pl.program_id(0) == 0) & (pl.program_id(1) == 0))
        def _():
            wacc_ref[...] = jnp.zeros_like(wacc_ref)

        _, m0 = _lane_masks()
        mt0 = lax.broadcasted_iota(jnp.int32, (PAIR, 1), 0) < HD
        g0 = g_ref[0:1, :]
        g1 = g_ref[1:2, :]
        fk_sc[...] = (_pair_rms(k_ref[...], m0)[0] * g1).astype(BF16)
        _transpose_blocks(fk_sc, fkt_sc, nb, tb)
        _transpose_blocks(v_ref, vt_sc, nb, tb)
        dfkt_sc[...] = jnp.zeros_like(dfkt_sc)
        dvt_sc[...] = jnp.zeros_like(dvt_sc)
        dc_ref[...] = jnp.zeros_like(dc_ref)
        causal = lax.broadcasted_iota(jnp.int32, (tb, tb), 1) <= lax.broadcasted_iota(jnp.int32, (tb, tb), 0)

        def qloop(qb, carry):
            qn, qr = _pair_rms(_rows(q_ref, qb, tb), m0)
            fqf = qn * (g0 * SCALE)
            dof = _rows(do_ref, qb, tb)
            fqh = _pair_split(fqf, m0)
            doh = _pair_split(dof, m0)
            fqth = _pair_split(fqf.T, mt0)
            doth = _pair_split(dof.T, mt0)
            nlv = _rows(nl_ref, qb, tb)
            cql = (nlv[:, 0:1], nlv[:, 1:2])

            def probs(kb, masked):
                qk = [_dot(fqh[h], fkt_sc[kb]) for h in range(2)]
                dp = [_dot(doh[h], vt_sc[kb]) for h in range(2)]
                pr = []
                for h in range(2):
                    e = jnp.exp(qk[h] + (cql[h] - ck_ref[h, kb]))
                    pr.append(jnp.where(causal, e, 0.0) if masked else e)
                return pr, dp

            oxv = _rows(ox_ref, qb, tb)
            dlt = [_rowsum(doh[h].astype(F32) * oxv) for h in range(2)]

            def accumulate(kb, pd, dfqs):
                out = []
                for h in range(2):
                    prb, dsb = pd[h]
                    dvt_sc[kb] += _dot(doth[h], prb)
                    dfkt_sc[kb] += _dot(fqth[h], dsb)
                    out.append(dfqs[h] + _dot(dsb, _rows(fk_sc, kb, tb)))
                return tuple(out)

            def block(kb, c, masked):
                dfqs, (kb_prev, pd_prev) = c
                pr, dp = probs(kb, masked)
                dfqs = accumulate(kb_prev, pd_prev, dfqs)
                pd = []
                for h in range(2):
                    ds = pr[h] * (dp[h] - dlt[h])
                    dc_ref[h, kb] += jnp.broadcast_to(-_colsum(ds), (8, tb))
                    pd.append((pr[h].astype(BF16), ds.astype(BF16)))
                return dfqs, (kb, tuple(pd))

            zq = jnp.zeros((tb, PAIR), F32)
            zb = jnp.zeros((tb, tb), BF16)
            none = (jnp.int32(0), ((zb, zb), (zb, zb)))
            c = _loop_grouped(qb, lambda i, cc: block(i, cc, False), ((zq, zq), none))
            dfqs, (kb_last, pd_last) = block(qb, c, True)
            c = accumulate(kb_last, pd_last, dfqs)
            dfq = _pair_select(m0, c[0], c[1]) * SCALE
            wacc_ref[0:1, :] += _colsum(dfq * qn)
            r0 = pl.multiple_of(qb * tb, tb)
            dq_ref[pl.ds(r0, tb), :] = _pair_rms_bwd(qn, qr, dfq * g0, m0).astype(BF16)
            return carry

        lax.fori_loop(0, nb, qloop, 0)
        for kb in range(nb):
            rows = slice(kb * tb, (kb + 1) * tb)
            dfk = dfkt_sc[kb].T
            knb, krb = _pair_rms(k_ref[rows, :], m0)
            wacc_ref[1:2, :] += _colsum(dfk * knb)
            dk_ref[rows, :] = _pair_rms_bwd(knb, krb, dfk * g1, m0).astype(BF16)
            dv_ref[rows, :] = dvt_sc[kb].T.astype(BF16)

    def col(blk):
        return pl.BlockSpec((S, PAIR), lambda b, p: (b, blk + p))

    sh = jax.ShapeDtypeStruct((T, AW), BF16)
    return pl.pallas_call(
        body, name=f"fox_bwd_{l}",
        grid=(T // S, 2),
        in_specs=[col(FXQ_BLK), col(FXK_BLK), col(FXV_BLK), col(0), col(0), col(0),
                  pl.BlockSpec((None, 2, nb, 1, tb), lambda b, p: (b, p, 0, 0, 0)),
                  pl.BlockSpec((None, 8, PAIR), lambda b, p: (l, 0, 0))],
        out_specs=[col(0), col(0), col(0),
                   pl.BlockSpec((None, 2, nb, 8, tb), lambda b, p: (b, p, 0, 0, 0)),
                   pl.BlockSpec((8, PAIR), lambda b, p: (0, 0))],
        out_shape=[sh, sh, sh,
                   jax.ShapeDtypeStruct((T // S, NH, nb, 8, tb), F32),
                   jax.ShapeDtypeStruct((8, PAIR), F32)],
        scratch_shapes=[pltpu.VMEM((S, PAIR), BF16), pltpu.VMEM((nb, PAIR, tb), BF16), pltpu.VMEM((nb, PAIR, tb), BF16),
                        pltpu.VMEM((nb, PAIR, tb), F32), pltpu.VMEM((nb, PAIR, tb), F32)],
        compiler_params=_cp(("arbitrary", "arbitrary"), VMEM_BIG),
    )(proj, proj, proj, do, nl, ox, ck, gqk2)


def _ada_fwd(c_all, w_ada, b_cols):
    nb, ncol = c_all.shape[0], w_ada.shape[2]
    tn = _tile(ncol, 768)

    def body(c_ref, w_ref, b_ref, o_ref):
        c = c_ref[...]
        ca = (c * _sigmoid(c)).astype(BF16)
        o_ref[...] = _dot(ca, w_ref[...].astype(BF16)) + b_ref[...]

    return pl.pallas_call(
        body, name="ada_fwd",
        grid=(2, ncol // tn),
        in_specs=[pl.BlockSpec((nb, D), lambda l, n: (0, 0)),
                  pl.BlockSpec((None, D, tn), lambda l, n: (l, 0, n)),
                  pl.BlockSpec((None, 1, tn), lambda l, n: (l, 0, n))],
        out_specs=pl.BlockSpec((None, nb, tn), lambda l, n: (l, 0, n)),
        out_shape=jax.ShapeDtypeStruct((2, nb, ncol), F32),
        compiler_params=_cp(("arbitrary", "arbitrary")),
    )(c_all, w_ada, b_cols)


def _ada_bwd(c_all, dmod_cols):
    nb, ncol = c_all.shape[0], dmod_cols.shape[2]
    tn = _tile(ncol, 768)

    def body(c_ref, d_ref, o_ref):
        c = c_ref[...]
        ca = (c * _sigmoid(c)).astype(BF16)
        o_ref[...] = _dot_tn(ca, d_ref[...].astype(BF16))

    return pl.pallas_call(
        body, name="ada_bwd",
        grid=(2, ncol // tn),
        in_specs=[pl.BlockSpec((nb, D), lambda l, n: (0, 0)),
                  pl.BlockSpec((None, nb, tn), lambda l, n: (l, 0, n))],
        out_specs=pl.BlockSpec((None, D, tn), lambda l, n: (l, 0, n)),
        out_shape=jax.ShapeDtypeStruct((2, D, ncol), F32),
        compiler_params=_cp(("arbitrary", "arbitrary")),
    )(c_all, dmod_cols)


def _sum_lead(a, name):
    n, R, C = a.shape
    tr = _tile_div8(R, 256)

    def body(a_ref, o_ref):
        acc = a_ref[0]
        for i in range(1, n):
            acc = acc + a_ref[i]
        o_ref[...] = acc

    return pl.pallas_call(
        body, name=name,
        grid=(R // tr,),
        in_specs=[pl.BlockSpec((n, tr, C), lambda i: (0, i, 0))],
        out_specs=pl.BlockSpec((tr, C), lambda i: (i, 0)),
        out_shape=jax.ShapeDtypeStruct((R, C), F32),
        compiler_params=_cp(("arbitrary",)),
    )(a)


def _adamw(w, g, m, v, name):
    R, C = w.shape
    tr = _tile_div8(R, max(8, (1 << 18) // C))
    c1 = 1.0 / (1.0 - ADAM_B1 ** ADAM_STEP)
    c2 = 1.0 / (1.0 - ADAM_B2 ** ADAM_STEP)

    def body(w_ref, g_ref, m_ref, v_ref, d_ref, mo_ref, vo_ref):
        gg = g_ref[...]
        mn = ADAM_B1 * m_ref[...] + (1.0 - ADAM_B1) * gg
        vn = ADAM_B2 * v_ref[...] + (1.0 - ADAM_B2) * (gg * gg)
        mo_ref[...] = mn
        vo_ref[...] = vn
        d_ref[...] = (-ADAM_LR) * ((mn * c1) / (jnp.sqrt(vn * c2) + ADAM_EPS) + ADAM_WD * w_ref[...])

    spec = pl.BlockSpec((tr, C), lambda i: (i, 0))
    sh = jax.ShapeDtypeStruct((R, C), F32)
    return pl.pallas_call(
        body, name=name, grid=(R // tr,),
        in_specs=[spec] * 4, out_specs=[spec] * 3, out_shape=[sh] * 3,
        compiler_params=_cp(("arbitrary",)),
    )(w, g, m, v)


def _coords():
    return lax.axis_index("x"), lax.axis_index("y"), lax.axis_index("c")


def _all_gather8(blk, name, vmem):
    m_per, n = blk.shape
    space = pltpu.VMEM if vmem else pl.ANY

    def body(x_ref, out_ref, send_sems, recv_sems, local_sem):
        x, y, c = _coords()
        me, sibling = (x, y, c), (x, y, 1 - c)
        chips = [(1 - x, y), (x, 1 - y), (1 - x, 1 - y)]

        def rows(px, py, pc):
            return out_ref.at[4 * px + 2 * py + pc]

        def copy(k, block, to, src=None):
            return pltpu.make_async_remote_copy(
                src_ref=rows(*block) if src is None else src, dst_ref=rows(*block),
                send_sem=send_sems.at[k], recv_sem=recv_sems.at[k], device_id=to, device_id_type=MESH)

        mine = pltpu.make_async_copy(x_ref, rows(*me), local_sem)
        mine.start()
        first = [copy(0, me, sibling, src=x_ref)]
        first += [copy(1 + j, me, (*chip, c), src=x_ref) for j, chip in enumerate(chips)]
        for cp in first:
            cp.start()
        passed = [copy(4 + j, (*chip, c), sibling) for j, chip in enumerate(chips)]
        for j, chip in enumerate(chips):
            copy(1 + j, (*chip, c), me).wait_recv()
            passed[j].start()
        copy(0, sibling, me).wait_recv()
        for j, chip in enumerate(chips):
            copy(4 + j, (*chip, 1 - c), me).wait_recv()
        for cp in first + passed:
            cp.wait_send()
        mine.wait()

    return pl.pallas_call(
        body, name=name,
        out_shape=jax.ShapeDtypeStruct((N_DEV, m_per, n), blk.dtype),
        in_specs=[pl.BlockSpec(memory_space=space)],
        out_specs=pl.BlockSpec(memory_space=space),
        scratch_shapes=[pltpu.SemaphoreType.DMA((7,)), pltpu.SemaphoreType.DMA((7,)), pltpu.SemaphoreType.DMA],
        compiler_params=pltpu.CompilerParams(vmem_limit_bytes=VMEM_BIG if vmem else None),
    )(blk)


def _relay(comm, parts):
    if "relay" in comm:
        comm["relay"](*parts)


def _run_comm(comm, name):
    n_in, n_out = len(comm["args"]), len(comm["out_shapes"])

    def body(*refs):
        parts = (refs[:n_in], refs[n_in:n_in + n_out], refs[n_in + n_out:])
        comm["start"](*parts)
        _relay(comm, parts)
        comm["finish"](*parts)

    anyspec = pl.BlockSpec(memory_space=pl.ANY)
    return pl.pallas_call(
        body, name=name, out_shape=comm["out_shapes"],
        in_specs=[anyspec] * n_in, out_specs=[anyspec] * n_out, scratch_shapes=comm["scratch"],
    )(*comm["args"])


def _hosted(comm):
    if comm is None:
        return [], [], [], []
    anyspec = pl.BlockSpec(memory_space=pl.ANY)
    return list(comm["args"]), [anyspec] * len(comm["args"]), list(comm["out_shapes"]), list(comm["scratch"])


def _ag_comm(wshards, pieces):
    n_piece = len(pieces)
    halves = [wshards[i].shape[len(lead)] // 2 for i, lead, _ in pieces]

    def plan(ins, outs, sems):
        send_sems, recv_sems, local_sems = sems
        x, y, c = _coords()
        me, sibling = (x, y, c), (x, y, 1 - c)
        chips = [(1 - x, y), (x, 1 - y), (1 - x, 1 - y)]

        def dsts(px, py, pc):
            s = 2 * px + py
            return [outs[p].at[s, pl.ds(pc * r2, r2)] if stacked else outs[p].at[pl.ds((2 * s + pc) * r2, r2)]
                    for p, ((_, _, stacked), r2) in enumerate(zip(pieces, halves))]

        srcs = [ins[i].at[(*lead, pl.ds(c * r2, r2))] for (i, lead, _), r2 in zip(pieces, halves)]

        def copies(k, block, to, own=False):
            d = dsts(*block)
            return [pltpu.make_async_remote_copy(
                src_ref=srcs[p] if own else d[p], dst_ref=d[p], send_sem=send_sems.at[k, p],
                recv_sem=recv_sems.at[k, p], device_id=to, device_id_type=MESH) for p in range(n_piece)]

        mine = [pltpu.make_async_copy(srcs[p], d, local_sems.at[p]) for p, d in enumerate(dsts(*me))]
        first = copies(0, me, sibling, own=True)
        for j, chip in enumerate(chips):
            first += copies(1 + j, me, (*chip, c), own=True)
        return me, sibling, chips, c, copies, mine, first

    def start(ins, outs, sems):
        *_, mine, first = plan(ins, outs, sems)
        for cp in mine + first:
            cp.start()

    def relay(ins, outs, sems):
        me, sibling, chips, c, copies, _, _ = plan(ins, outs, sems)
        for j, chip in enumerate(chips):
            for cp in copies(1 + j, (*chip, c), me):
                cp.wait_recv()
            for cp in copies(4 + j, (*chip, c), sibling):
                cp.start()

    def finish(ins, outs, sems):
        me, sibling, chips, c, copies, mine, first = plan(ins, outs, sems)
        passed = [cp for j, chip in enumerate(chips) for cp in copies(4 + j, (*chip, c), sibling)]
        for cp in copies(0, sibling, me):
            cp.wait_recv()
        for j, chip in enumerate(chips):
            for cp in copies(4 + j, (*chip, 1 - c), me):
                cp.wait_recv()
        for cp in first + passed:
            cp.wait_send()
        for cp in mine:
            cp.wait()

    out_shapes = []
    for (i, lead, stacked), r2 in zip(pieces, halves):
        cols = wshards[i].shape[-1]
        out_shapes.append(jax.ShapeDtypeStruct((N_SHARD, 2 * r2, cols) if stacked else (N_SHARD * 2 * r2, cols), BF16))
    return dict(
        args=list(wshards), out_shapes=out_shapes,
        scratch=[pltpu.SemaphoreType.DMA((7, n_piece)), pltpu.SemaphoreType.DMA((7, n_piece)),
                 pltpu.SemaphoreType.DMA((n_piece,))],
        start=start, relay=relay, finish=finish)


def _rs_to_chips_comm(hs):
    n = len(hs)

    def copies(h, r, sems):
        send_sems, recv_sems = sems
        x, y, c = _coords()
        chips = [(1 - x, y), (x, 1 - y), (1 - x, 1 - y)]
        return [pltpu.make_async_remote_copy(
            src_ref=h[p].at[2 * px + py], dst_ref=r[p].at[k], send_sem=send_sems.at[k, p], recv_sem=recv_sems.at[k, p],
            device_id=(px, py, c), device_id_type=MESH) for k, (px, py) in enumerate(chips) for p in range(n)]

    def start(h, r, sems):
        for cp in copies(h, r, sems):
            cp.start()

    def finish(h, r, sems):
        for cp in copies(h, r, sems):
            cp.wait()

    return dict(args=list(hs), out_shapes=[jax.ShapeDtypeStruct((3,) + h.shape[1:], h.dtype) for h in hs],
                scratch=[pltpu.SemaphoreType.DMA((3, n)), pltpu.SemaphoreType.DMA((3, n))],
                start=start, finish=finish)


def _rs_to_sibling_comm(pieces):
    n = len(pieces)

    def copies(g, r, sems):
        send_sems, recv_sems = sems
        x, y, c = _coords()
        cps = []
        for p in range(n):
            r2 = g[p].shape[1] // 2
            cps.append(pltpu.make_async_remote_copy(
                src_ref=g[p].at[:, pl.ds((1 - c) * r2, r2)], dst_ref=r[p], send_sem=send_sems.at[p],
                recv_sem=recv_sems.at[p], device_id=(x, y, 1 - c), device_id_type=MESH))
        return cps

    def start(g, r, sems):
        for cp in copies(g, r, sems):
            cp.start()

    def finish(g, r, sems):
        for cp in copies(g, r, sems):
            cp.wait()

    return dict(args=list(pieces),
                out_shapes=[jax.ShapeDtypeStruct((N_SHARD, g.shape[1] // 2, g.shape[2]), g.dtype) for g in pieces],
                scratch=[pltpu.SemaphoreType.DMA((n,)), pltpu.SemaphoreType.DMA((n,))],
                start=start, finish=finish)


def _share_halves(tensors, places, r2s):
    n, no = len(places), len(tensors)

    def body(*refs):
        o, (send_sems, recv_sems) = refs[no:2 * no], refs[2 * no:]
        x, y, c = _coords()

        def half(p, hc):
            oi, lead = places[p]
            return o[oi].at[(*lead, pl.ds(hc * r2s[p], r2s[p]))]

        outs = [pltpu.make_async_remote_copy(
            src_ref=half(p, c), dst_ref=half(p, c), send_sem=send_sems.at[p], recv_sem=recv_sems.at[p],
            device_id=(x, y, 1 - c), device_id_type=MESH) for p in range(n)]
        for cp in outs:
            cp.start()
        for p in range(n):
            pltpu.make_async_remote_copy(
                src_ref=half(p, 1 - c), dst_ref=half(p, 1 - c), send_sem=send_sems.at[p], recv_sem=recv_sems.at[p],
                device_id=(x, y, 1 - c), device_id_type=MESH).wait_recv()
        for cp in outs:
            cp.wait_send()

    anyspec = pl.BlockSpec(memory_space=pl.ANY)
    return pl.pallas_call(
        body, name="share_halves",
        out_shape=[jax.ShapeDtypeStruct(t.shape, t.dtype) for t in tensors],
        in_specs=[anyspec] * no, out_specs=[anyspec] * no,
        input_output_aliases={i: i for i in range(no)},
        scratch_shapes=[pltpu.SemaphoreType.DMA((n,)), pltpu.SemaphoreType.DMA((n,))],
    )(*tensors)


def _add_rows(r2, cols, n_arrays):
    lanes = -(-cols // 128) * 128
    return _tile_div8(r2, max(16, (24 << 20) // (2 * n_arrays * lanes * 4)), mult=16)


def _add_sibling(pieces, recvs, cidx, name):
    n = len(pieces)
    _, R, C = pieces[0].shape
    r2 = R // 2
    tr = _add_rows(r2, C, 2 * n)
    nt = r2 // tr

    def body(c_ref, *refs):
        for p in range(n):
            refs[2 * n + p][...] = (refs[p][...] + refs[n + p][...].astype(F32)).astype(BF16)

    return pl.pallas_call(
        body, name=name,
        grid_spec=pltpu.PrefetchScalarGridSpec(
            num_scalar_prefetch=1, grid=(N_SHARD, nt),
            in_specs=[pl.BlockSpec((None, tr, C), lambda s, i, c_ref: (s, c_ref[0] * nt + i, 0))] * n
            + [pl.BlockSpec((None, tr, C), lambda s, i, c_ref: (s, i, 0))] * n,
            out_specs=[pl.BlockSpec((None, tr, C), lambda s, i, c_ref: (s, i, 0))] * n),
        out_shape=[jax.ShapeDtypeStruct((N_SHARD, r2, C), BF16)] * n,
        compiler_params=_cp(("arbitrary", "arbitrary"), VMEM_BIG),
    )(cidx, *pieces, *recvs)


def _add_chips_into(piece, recv_a, recv_b, sc, prev, shape, lead, name):
    _, R, C = piece.shape
    r2 = R // 2
    tr = _add_rows(r2, C, 4)
    nt = r2 // tr
    nl = len(lead)

    def body(sc_ref, p_ref, a_ref, b_ref, *rest):
        o_ref = rest[-1]
        acc = p_ref[...] + a_ref[...].astype(F32)
        for k in range(3):
            acc = acc + b_ref[k].astype(F32)
        o_ref[...] = acc

    in_specs = [pl.BlockSpec((None, tr, C), lambda i, sc_ref: (sc_ref[0], sc_ref[1] * nt + i, 0)),
                pl.BlockSpec((None, tr, C), lambda i, sc_ref: (sc_ref[0], i, 0)),
                pl.BlockSpec((3, tr, C), lambda i, sc_ref: (0, i, 0))]
    args = [sc, piece, recv_a, recv_b]
    aliases = {}
    if prev is not None:
        in_specs.append(pl.BlockSpec(memory_space=pl.ANY))
        args.append(prev)
        aliases = {4: 0}
    return pl.pallas_call(
        body, name=name,
        grid_spec=pltpu.PrefetchScalarGridSpec(
            num_scalar_prefetch=1, grid=(nt,), in_specs=in_specs,
            out_specs=pl.BlockSpec((None,) * nl + (tr, C), lambda i, sc_ref: (*lead, sc_ref[1] * nt + i, 0))),
        out_shape=jax.ShapeDtypeStruct(shape, F32),
        input_output_aliases=aliases,
        compiler_params=_cp(("arbitrary",), VMEM_BIG),
    )(*args)


def _pack_rows(parts, rows, dtype):
    flat = jnp.concatenate([p.reshape(-1).astype(dtype) for p in parts])
    return jnp.pad(flat, (0, rows * ROW - flat.shape[0])).reshape(rows, ROW)


def _unpack(flat, shapes):
    out, off = [], 0
    for sh in shapes:
        n = math.prod(sh)
        out.append(flat[off:off + n].reshape(sh))
        off += n
    return out


def _block_diag(w):
    eye = jnp.eye(LW // HD, dtype=w.dtype)
    return jnp.einsum("lhij,hg->lhigj", w, eye).reshape(w.shape[0], LW, LW)


def _diag_blocks(w):
    nbk = LW // HD
    w4 = w.reshape(nbk, HD, nbk, HD)
    return jnp.stack([w4[h, :, h, :] for h in range(nbk)])


def _rows8(rows, width):
    z = jnp.zeros((width,), F32)
    return jnp.stack(list(rows) + [z] * (8 - len(rows)))


def kernel(x, c, w_ada, b_ada, g_norm, w_ffn_up, w_ffn_down, w_in, b_fgate, conv_w, conv_b, w_rgate, b_rgate, w_igate, b_igate, lru_lambda, g_qk, g_mix_out, w_out, loss_target, m_w_ada, m_b_ada, m_g_norm, m_w_ffn_up, m_w_ffn_down, m_w_in, m_b_fgate, m_conv_w, m_conv_b, m_w_rgate, m_b_rgate, m_w_igate, m_b_igate, m_lru_lambda, m_g_qk, m_g_mix_out, m_w_out, v_w_ada, v_b_ada, v_g_norm, v_w_ffn_up, v_w_ffn_down, v_w_in, v_b_fgate, v_conv_w, v_conv_b, v_w_rgate, v_b_rgate, v_w_igate, v_b_igate, v_lru_lambda, v_g_qk, v_g_mix_out, v_w_out):
    B, S, _ = x.shape
    T = B * S
    xi, yi, ci = _coords()
    sidx = 2 * xi + yi
    didx = 4 * xi + 2 * yi + ci
    ada_cols = w_ada.shape[2]
    gn_cols = g_norm.shape[2]
    cw_cols = conv_w.shape[2]
    n_all = B * N_DEV

    blk1 = _pack_rows([c, jnp.pad(g_norm.reshape(-1), (0, 2 * ROW - g_norm.size)), conv_w], 8, F32)
    ag1 = _all_gather8(blk1, "ag_small_in", True)
    c_all = ag1[:, 0:B].reshape(n_all, D)
    chip_rows = ag1[0::2]
    g_norm_full = chip_rows[:, 2:4].reshape(N_SHARD, 2 * ROW)[:, :g_norm.size] \
        .reshape(N_SHARD, 2, 3, gn_cols).transpose(1, 2, 0, 3).reshape(2, 3, D)
    conv_w_full = chip_rows[:, 4].reshape(N_SHARD, 2, 4, cw_cols).transpose(1, 2, 0, 3).reshape(2, 4, LW)

    b_cols = lax.dynamic_slice(b_ada, (0, sidx * ada_cols), (2, ada_cols)).reshape(2, 1, ada_cols)
    mod_cols = _ada_fwd(c_all, w_ada, b_cols)
    mrows = (2 * n_all * ada_cols) // ROW
    ag2 = _all_gather8(mod_cols.reshape(mrows, ROW), "ag_mod", True)
    mod_sh = ag2[0::2].reshape(N_SHARD, 2, n_all, ada_cols)
    mod_me = lax.dynamic_slice(mod_sh, (0, 0, didx * B, 0), (N_SHARD, 2, B, ada_cols))
    mod_me = mod_me.transpose(1, 2, 0, 3).reshape(2, B, 3, 3, D)
    zrow = jnp.zeros((B, D), F32)
    mods = [[jnp.stack([mod_me[l, :, j, 0], 1.0 + mod_me[l, :, j, 1], 1.0 + mod_me[l, :, j, 2],
                        jnp.broadcast_to(g_norm_full[l, j], (B, D)), zrow, zrow, zrow, zrow], axis=1)
             for j in range(3)] for l in range(2)]

    wshards = (w_ffn_up.astype(BF16), w_ffn_down.astype(BF16), w_in.astype(BF16), w_out.astype(BF16))

    def ffn_pieces(l, j):
        return [(0, (l, j), True), (1, (l, j), False)]

    def mixer_pieces(l):
        return [(2, (l,), True), (3, (l,), False)]

    def ffn_weights(up, dn):
        return dict(up=up, dn=dn)

    def mixer_weights(g_in, g_out):
        return dict(inp=jnp.pad(g_in.transpose(1, 0, 2).reshape(D, N_IN), ((0, 0), (0, N_INP - N_IN))), out=g_out)

    wl = [dict(), dict()]
    wl[0][0] = ffn_weights(*_run_comm(_ag_comm(wshards, ffn_pieces(0, 0)), "ag_weights_0_0"))

    wr_d = _block_diag(w_rgate).astype(BF16)
    wi_d = _block_diag(w_igate).astype(BF16)
    cw8 = jnp.pad(conv_w_full, ((0, 0), (0, 4), (0, 0)))
    vp8 = jnp.stack([_rows8([conv_b[l], b_rgate[l], b_igate[l], lru_lambda[l]], LW) for l in range(2)])
    bfp = jnp.pad(b_fgate, ((0, 0), (0, 128 - NH)))[:, None, :] * jnp.ones((1, 8, 1), F32)
    gqk2 = jnp.tile(jnp.pad(g_qk, ((0, 0), (0, 6), (0, 0))), (1, 1, 2))
    gmix8 = jnp.pad(g_mix_out[:, None, :], ((0, 0), (0, 7), (0, 0)))

    x2 = x.reshape(T, D)
    tgt = loss_target.reshape(T, D)

    saved = []
    xc = x2
    for l in range(2):
        sv = {}
        sv["x0"] = xc
        w = wl[l]
        rest0 = _ag_comm(wshards, mixer_pieces(0) + ffn_pieces(0, 1)) if l == 0 else None
        xc, sv["g0"], sv["u0"], sv["f0"], got = _ffn_fwd(xc, mods[l][0], w[0]["up"], w[0]["dn"], l, 0, S, rest0)
        if l == 0:
            w["mix"] = mixer_weights(got[0], got[1])
            w[1] = ffn_weights(got[2], got[3])
        sv["x1"] = xc
        sv["h1"], proj = _mix_in_fwd(xc, mods[l][1], w["mix"]["inp"], l, S)
        sv["proj"] = proj
        sv["ylru"], sv["hl"] = _lru_fwd(proj, cw8, vp8, wr_d, wi_d, l, S)
        all1 = _ag_comm(wshards, ffn_pieces(1, 0) + mixer_pieces(1) + ffn_pieces(1, 1)) if l == 0 else None
        sv["osb"], sv["t1"], got = _sbq_fwd(proj, l, S, all1)
        if l == 0:
            wl[1][0] = ffn_weights(got[0], got[1])
            wl[1]["mix"] = mixer_weights(got[2], got[3])
            wl[1][1] = ffn_weights(got[4], got[5])
        cum = _fgate_fwd(proj, bfp, l, S)
        sv["ck"] = cum[:, :NH].reshape(B, S, NH).transpose(0, 2, 1).reshape(B, NH, S // TQ_(S), 1, TQ_(S))
        sv["ofx"], sv["nl"] = _foxq_fwd(proj, cum, sv["ck"], gqk2, l, S)
        xc, sv["y"], sv["mo"] = _mix_out_fwd(xc, sv["ylru"], sv["osb"], sv["ofx"], mods[l][1], gmix8, w["mix"]["out"], l, S)
        sv["x2"] = xc
        xc, sv["g2"], sv["u2"], sv["f2"], _ = _ffn_fwd(xc, mods[l][2], w[1]["up"], w[1]["dn"], l, 1, S)
        saved.append(sv)

    dxc, lpart = _loss_head(xc, tgt, S)
    loss = lax.psum(lpart[0, 0], ("x", "y", "c"))

    tf = wl[0][0]["up"].shape[-1]
    g_up_l = [[None, None], [None, None]]
    g_dn_l = [[None, None], [None, None]]
    g_in_l, g_out_l = [None, None], [None, None]
    dmods = [[None] * 3 for _ in range(2)]
    small = [dict() for _ in range(2)]
    cvec = jnp.reshape(ci, (1,)).astype(jnp.int32)
    scvec = jnp.stack([sidx, ci]).astype(jnp.int32)

    def ffn_groups(l, j):
        return [(0, "up", [g_up_l[l][j]], [(l, j)]), (1, "dn", [g_dn_l[l][j]], [(l, j)])]

    def mixer_groups(l):
        return [(2, "in", [g_in_l[l]], [(l,)]), (3, "out", [g_out_l[l]], [(l,)])]

    def sibling_comm(groups):
        return _rs_to_sibling_comm([pb for _, _, ps, _ in groups for _, pb in ps])

    def rs_sibling_phase(groups, tag, recv_a=None):
        if recv_a is None:
            recv_a = _run_comm(sibling_comm(groups), f"rs_to_sibling_{tag}")
        hs, off = [], 0
        for _, gname, ps, leads in groups:
            hs += _add_sibling([pf for pf, _ in ps], recv_a[off:off + len(ps)], cvec,
                               f"rs_add_sibling_{gname}_{'_'.join(map(str, leads[0]))}")
            off += len(ps)
        return groups, recv_a, hs

    def ffn_back(l, j, xin, dy, sv, sub, comm=None):
        dx, dmod, wacc, hb, dfb, ab, dgub, got = _ffn_bwd(
            xin, dy, mods[l][sub], sv[f"f{sub}"], sv[f"g{sub}"], sv[f"u{sub}"],
            wl[l][j]["up"], wl[l][j]["dn"], l, j, S, comm)
        g_up_l[l][j] = _mm_tn(hb, dgub, f"dw_up_{l}_{j}", tnb=tf, split_n=True, with_bf16=True)
        g_dn_l[l][j] = tuple(g.reshape(N_SHARD, -1, D)
                             for g in _mm_tn(ab, dfb, f"dw_dn_{l}_{j}", tma=tf, with_bf16=True))
        dmods[l][sub] = dmod
        small[l][f"gn{sub}"] = wacc[0]
        return dx, got

    batches = []
    for l in (1, 0):
        sv = saved[l]
        if l == 1:
            dxc, _ = ffn_back(l, 1, sv["x2"], dxc, sv, 2)
        else:
            dxc, got = ffn_back(l, 1, sv["x2"], dxc, sv, 2, sibling_comm(groups1))
            rs1 = rs_sibling_phase(groups1, "1", got)
        dyl, dsb, dfx, dmo, dmod1, wacc_mo = _mix_out_bwd(
            dxc, sv["ylru"], sv["osb"], sv["ofx"], sv["mo"], mods[l][1], gmix8, wl[l]["mix"]["out"], l, S)
        small[l]["gmix"] = wacc_mo[0]
        g_out_l[l] = tuple(g.reshape(N_SHARD, -1, D) for g in _mm_tn(sv["y"], dmo, f"dw_out_{l}", with_bf16=True))
        dsq, dsk, dsv, got = _sbq_bwd(sv["proj"], dsb, sv["t1"], l, S,
                                       _rs_to_chips_comm(rs1[2]) if l == 0 else None)
        if l == 0:
            batches.append((rs1[0], rs1[1], got))
        dfq, dfk, dfv, dck, wacc_fx = _foxq_bwd(sv["proj"], dfx, sv["nl"], sv["ofx"], sv["ck"], gqk2, l, S)
        small[l]["gqk"] = wacc_fx[0:2, :HD] + wacc_fx[0:2, HD:]
        dcum = dck[:, :, :, 0, :].reshape(B, NH, S).transpose(0, 2, 1).reshape(T, NH)
        dff_, wacc_fg = _fgate_bwd(jnp.pad(dcum, ((0, 0), (0, 128 - NH))), sv["proj"], bfp, l, S)
        small[l]["bf"] = wacc_fg[0, :NH]
        dlx, dlg, dpr, dpi, ub, wacc_lru = _lru_bwd(dyl, sv["proj"], sv["hl"], cw8, vp8, wr_d, wi_d, l, S)
        small[l]["lru"] = wacc_lru
        small[l]["wr"] = _diag_blocks(_mm_tn(ub, dpr, f"dw_rgate_{l}"))
        small[l]["wi"] = _diag_blocks(_mm_tn(ub, dpi, f"dw_igate_{l}"))
        dproj = jnp.concatenate(
            [dlx, dlg, dsq, dsk, dsv, dfq, dfk, dfv, dff_], axis=1)
        g_in = _mm_tn(sv["h1"], dproj, f"dw_in_{l}", tnb=N_INP // 3)[:, :N_IN]
        g_in = g_in.reshape(D, N_SHARD, -1).transpose(1, 0, 2)
        g_in_l[l] = (g_in, g_in.astype(BF16))
        dxc, dmod_in, wacc_in = _mix_in_bwd(sv["x1"], dxc, mods[l][1], dproj, wl[l]["mix"]["inp"], l, S)
        dmods[l][1] = dmod_in + dmod1
        small[l]["gn1"] = wacc_in[0]
        if l == 1:
            dxc, _ = ffn_back(l, 0, sv["x0"], dxc, sv, 0)
            groups1 = ffn_groups(1, 0) + mixer_groups(1) + ffn_groups(1, 1)
        else:
            late = rs_sibling_phase(mixer_groups(0) + ffn_groups(0, 1), "0_late")
            dxc, got = ffn_back(l, 0, sv["x0"], dxc, sv, 0, _rs_to_chips_comm(late[2]))
            batches.append((late[0], late[1], got))
    grad_x = dxc.reshape(B, S, D)

    dmod_loc = jnp.stack([jnp.stack([dmods[l][j][:, 0:3, :] for j in range(3)], axis=1) for l in range(2)])
    drows = 2 * B * 9
    blk3 = _pack_rows([dmod_loc], -(-drows // 8) * 8, F32)
    ag3 = _all_gather8(blk3, "ag_dmod", True)
    dmod_all = ag3[:, :drows].reshape(N_DEV, 2, B, 9 * D).transpose(1, 0, 2, 3).reshape(2, n_all, 9 * D)
    dmod_mine = lax.dynamic_slice(dmod_all, (0, 0, sidx * ada_cols), (2, n_all, ada_cols))
    grad_w_ada = _ada_bwd(c_all, dmod_mine)
    dmod_rows = jnp.pad(dmod_all.transpose(1, 0, 2).reshape(n_all, 2 * 9, D), ((0, 0), (0, 6), (0, 0)))
    grad_b_ada = _sum_lead(dmod_rows, "grad_b_ada")[:2 * 9].reshape(2, 9 * D)

    sm_parts = [
        jnp.stack([small[l]["bf"] for l in range(2)]),
        jnp.stack([small[l]["lru"][4] for l in range(2)]),
        jnp.stack([small[l]["wr"] for l in range(2)]),
        jnp.stack([small[l]["lru"][5] for l in range(2)]),
        jnp.stack([small[l]["wi"] for l in range(2)]),
        jnp.stack([small[l]["lru"][6] for l in range(2)]),
        jnp.stack([small[l]["lru"][7] for l in range(2)]),
        jnp.stack([small[l]["gqk"] for l in range(2)]),
        jnp.stack([small[l]["gmix"] for l in range(2)]),
        jnp.stack([jnp.stack([small[l][f"gn{j}"] for j in range(3)]) for l in range(2)]),
        jnp.stack([small[l]["lru"][0:4] for l in range(2)]),
    ]
    sm_shapes = [p.shape for p in sm_parts]
    sm_rows = -(-sum(p.size for p in sm_parts) // (8 * ROW)) * 8
    ag4 = _all_gather8(_pack_rows(sm_parts, sm_rows, F32), "ag_small_grads", True)
    sm_sum = _sum_lead(ag4, "sum_small_grads").reshape(-1)
    (g_bf, g_cb, g_wr, g_br, g_wi, g_bi, g_lam, g_gqk, g_gmix, g_gn_full, g_cw_full) = _unpack(sm_sum, sm_shapes)
    g_gn = lax.dynamic_slice(g_gn_full, (0, 0, sidx * gn_cols), (2, 3, gn_cols))
    g_cw = lax.dynamic_slice(g_cw_full, (0, 0, sidx * cw_cols), (2, 4, cw_cols))

    last = rs_sibling_phase(ffn_groups(0, 0), "0_first")
    batches.append((last[0], last[1], _run_comm(_rs_to_chips_comm(last[2]), "rs_to_chips_0_first")))
    shapes4 = [w_ffn_up.shape, w_ffn_down.shape, w_in.shape, w_out.shape]
    tensors, places, r2s = [None] * 4, [], []
    for groups, recv_a, recv_b in batches:
        k = 0
        for gi, gname, ps, leads in groups:
            for (pf, _), lead in zip(ps, leads):
                tensors[gi] = _add_chips_into(pf, recv_a[k], recv_b[k], scvec, tensors[gi], shapes4[gi], lead,
                                              f"rs_add_chips_{gname}_{'_'.join(map(str, lead))}")
                places.append((gi, lead))
                r2s.append(pf.shape[1] // 2)
                k += 1
    gw_up, gw_dn, gw_in, gw_out = _share_halves(tensors, places, r2s)

    def upd(w, g, m, v, name):
        sh = w.shape
        two = (w.size // sh[-1], sh[-1])
        dlt, mn, vn = _adamw(w.reshape(two), g.reshape(two), m.reshape(two), v.reshape(two), name)
        return dlt.reshape(sh), mn.reshape(sh), vn.reshape(sh)

    big = {
        "w_ada": (w_ada, grad_w_ada, m_w_ada, v_w_ada),
        "w_ffn_up": (w_ffn_up, gw_up, m_w_ffn_up, v_w_ffn_up),
        "w_ffn_down": (w_ffn_down, gw_dn, m_w_ffn_down, v_w_ffn_down),
        "w_in": (w_in, gw_in, m_w_in, v_w_in),
        "w_out": (w_out, gw_out, m_w_out, v_w_out),
    }
    res = {n: (t[1],) + upd(*t, f"adamw_{n}") for n, t in big.items()}

    smalls = {
        "b_ada": (b_ada, grad_b_ada, m_b_ada, v_b_ada),
        "g_norm": (g_norm, g_gn, m_g_norm, v_g_norm),
        "b_fgate": (b_fgate, g_bf, m_b_fgate, v_b_fgate),
        "conv_w": (conv_w, g_cw, m_conv_w, v_conv_w),
        "conv_b": (conv_b, g_cb, m_conv_b, v_conv_b),
        "w_rgate": (w_rgate, g_wr, m_w_rgate, v_w_rgate),
        "b_rgate": (b_rgate, g_br, m_b_rgate, v_b_rgate),
        "w_igate": (w_igate, g_wi, m_w_igate, v_w_igate),
        "b_igate": (b_igate, g_bi, m_b_igate, v_b_igate),
        "lru_lambda": (lru_lambda, g_lam, m_lru_lambda, v_lru_lambda),
        "g_qk": (g_qk, g_gqk, m_g_qk, v_g_qk),
        "g_mix_out": (g_mix_out, g_gmix, m_g_mix_out, v_g_mix_out),
    }
    names = list(smalls)
    shapes = [smalls[n][0].shape for n in names]
    prow = -(-sum(math.prod(s) for s in shapes) // (8 * ROW)) * 8
    packed = [_pack_rows([smalls[n][i].reshape(shapes[k]) for k, n in enumerate(names)], prow, F32) for i in range(4)]
    outs = _adamw(packed[0], packed[1], packed[2], packed[3], "adamw_small")
    un = [_unpack(o.reshape(-1), shapes) for o in outs]
    for k, n in enumerate(names):
        res[n] = (smalls[n][1].reshape(shapes[k]), un[0][k], un[1][k], un[2][k])

    order = ["w_ada", "b_ada", "g_norm", "w_ffn_up", "w_ffn_down", "w_in", "b_fgate", "conv_w", "conv_b",
             "w_rgate", "b_rgate", "w_igate", "b_igate", "lru_lambda", "g_qk", "g_mix_out", "w_out"]
    return (loss, grad_x, *[res[n][0] for n in order], *[res[n][1] for n in order],
            *[res[n][2] for n in order], *[res[n][3] for n in order])


def TQ_(S):
    return min(TQ, S)
```
